```python
import math
import jax, jax.numpy as jnp
from jax import lax
import numpy as np


D_MODEL = 1024
BATCH = 8
SEQ = 4096
DEPTH = 4

D_ATTN = D_MODEL // 2
HEAD_DIM = 64
N_HEADS = D_ATTN // HEAD_DIM
ROPE_DIM = HEAD_DIM // 4
ROPE_THETA = 500000.0
DILATED_PATTERNS = ((128, 1), (512, 4), (2048, 16))
D_SSM = D_MODEL - D_ATTN
SSM_GROUP = 16
N_SSM_GROUPS = D_SSM // SSM_GROUP
SSM_STATE = 64
DT_MIN = 0.001
DT_MAX = 0.1
D_MIX = D_ATTN + D_SSM
D_IN_PROJ = 3 * D_ATTN + D_SSM
D_FF = 128 * (-(-(8 * D_MODEL // 3) // 128))
PLE_DIM = 256
NORM_EPS = 1e-6

kernel_name = 'hybrid_s5_dilated_macaron_block'


def rms_norm(x, g):
    xf = x.astype(jnp.float32)
    y = xf * lax.rsqrt(jnp.mean(xf * xf, axis=-1, keepdims=True) + NORM_EPS)
    return (y * g.astype(jnp.float32)).astype(x.dtype)


def swiglu(x, w_gate, w_up, w_down):
    return (jax.nn.silu(x @ w_gate) * (x @ w_up)) @ w_down


def partial_rotary(t, positions):
    half = ROPE_DIM // 2
    inv_freq = ROPE_THETA ** (-jnp.arange(half, dtype=jnp.float32) * (2.0 / ROPE_DIM))
    ang = positions.astype(jnp.float32)[:, :, None, None] * inv_freq
    cos, sin = jnp.cos(ang), jnp.sin(ang)
    tf = t.astype(jnp.float32)
    t1, t2, rest = tf[..., :half], tf[..., half:ROPE_DIM], tf[..., ROPE_DIM:]
    out = jnp.concatenate([t1 * cos - t2 * sin, t2 * cos + t1 * sin, rest], axis=-1)
    return out.astype(t.dtype)


def dilated_band_attention(q, k, v, window, dilation):
    b_, s_, h_, dh = q.shape
    band = window // dilation
    n_str = s_ // dilation
    nb = -(-n_str // band)
    lp = nb * band
    nrows = b_ * dilation

    def to_blocks(t):
        t = t.reshape(b_, n_str, dilation, h_, dh).transpose(0, 2, 1, 3, 4).reshape(nrows, n_str, h_, dh)
        t = jnp.pad(t, ((0, 0), (0, lp - n_str), (0, 0), (0, 0)))
        return t.reshape(nrows, nb, band, h_, dh)

    def with_prev(t):
        prev = jnp.pad(t[:, :-1], ((0, 0), (1, 0), (0, 0), (0, 0), (0, 0)))
        return jnp.concatenate([prev, t], axis=2)

    qb = to_blocks(q)
    kc = with_prev(to_blocks(k))
    vc = with_prev(to_blocks(v))
    scores = jnp.einsum('nbqhd,nbkhd->nbhqk', qb, kc).astype(jnp.float32) * (HEAD_DIM ** -0.5)
    qi = jnp.arange(band)[:, None]
    kj = jnp.arange(2 * band)[None, :]
    dist = qi + band - kj
    band_ok = (dist >= 0) & (dist <= band)
    blk = jnp.arange(nb)[:, None, None]
    mask = band_ok[None] & ((blk > 0) | (kj >= band)[None])
    scores = jnp.where(mask[None, :, None], scores, -jnp.inf)
    m = jnp.max(scores, axis=-1, keepdims=True)
    e = jnp.exp(scores - m)
    den = jnp.sum(e, axis=-1, keepdims=True)
    probs = (e / den).astype(v.dtype)
    out = jnp.einsum('nbhqk,nbkhd->nbqhd', probs, vc).astype(jnp.float32)
    lse = (m + jnp.log(den))[..., 0]
    out = out.reshape(nrows, lp, h_, dh)[:, :n_str]
    out = out.reshape(b_, dilation, n_str, h_, dh).transpose(0, 2, 1, 3, 4).reshape(b_, s_, h_, dh)
    lse = lse.transpose(0, 1, 3, 2).reshape(nrows, lp, h_)[:, :n_str]
    lse = lse.reshape(b_, dilation, n_str, h_).transpose(0, 2, 1, 3).reshape(b_, s_, h_)
    return out, lse


def dilated_mixture_attention(q, k, v):
    outs, lses = [], []
    for window, dilation in DILATED_PATTERNS:
        o, l = dilated_band_attention(q, k, v, window, dilation)
        outs.append(o)
        lses.append(l)
    wts = jax.nn.softmax(jnp.stack(lses, axis=0), axis=0)
    out = jnp.sum(wts[..., None] * jnp.stack(outs, axis=0), axis=0)
    b_, s_ = q.shape[0], q.shape[1]
    return out.reshape(b_, s_, D_ATTN).astype(q.dtype)


def s5_mixer(u, lam_re, lam_im, log_dt, b_re, b_im, c_re, c_im, d_skip, w_glu, b_glu):
    b_, s_ = u.shape[0], u.shape[1]
    uf = u.astype(jnp.float32).reshape(b_, s_, N_SSM_GROUPS, SSM_GROUP)
    lam = lax.complex(lam_re.astype(jnp.float32), lam_im.astype(jnp.float32))
    dt = jnp.exp(log_dt.astype(jnp.float32))[:, None]
    lam_bar = jnp.exp(lam * dt)
    b_mat = lax.complex(b_re.astype(jnp.float32), b_im.astype(jnp.float32))
    b_bar = ((lam_bar - 1.0) / lam)[..., None] * b_mat
    bu = jnp.einsum('bsgh,gph->bsgp', uf.astype(jnp.complex64), b_bar)
    a = jnp.broadcast_to(lam_bar, bu.shape)

    def combine(left, right):
        a_l, x_l = left
        a_r, x_r = right
        return a_r * a_l, a_r * x_l + x_r

    _, states = lax.associative_scan(combine, (a, bu), axis=1)
    c_mat = lax.complex(c_re.astype(jnp.float32), c_im.astype(jnp.float32))
    y = jnp.real(jnp.einsum('bsgp,ghp->bsgh', states, c_mat)) + d_skip.astype(jnp.float32) * uf
    y = jax.nn.gelu(y).reshape(b_, s_, D_SSM)
    y = y * jax.nn.sigmoid(y @ w_glu.astype(jnp.float32) + b_glu.astype(jnp.float32))
    return y.astype(u.dtype)


def _fwd_setup_inputs(seed: int = 0) -> dict:
    key = jax.random.key(seed)
    ks = jax.random.split(key, 40)
    f32 = jnp.float32

    def nrm(k, shape, fan_in):
        return jax.random.normal(k, shape, f32) * (fan_in ** -0.5)

    def gain(k, shape):
        return 1.0 + 0.05 * jax.random.normal(k, shape, f32)

    lam_im_base = jnp.pi * jnp.arange(SSM_STATE, dtype=f32)
    return {
        'x': jax.random.normal(ks[0], (BATCH, SEQ, D_MODEL), f32),
        'p': jax.random.normal(ks[1], (DEPTH, BATCH, SEQ, PLE_DIM), f32),
        'positions': (jax.random.randint(ks[2], (BATCH, 1), 0, 1024, dtype=jnp.int32)
                      + jnp.arange(SEQ, dtype=jnp.int32)[None, :]),
        'ffn1_pre_g': gain(ks[3], (DEPTH, D_MODEL)),
        'ffn1_w_gate': nrm(ks[4], (DEPTH, D_MODEL, D_FF), D_MODEL),
        'ffn1_w_up': nrm(ks[5], (DEPTH, D_MODEL, D_FF), D_MODEL),
        'ffn1_w_down': nrm(ks[6], (DEPTH, D_FF, D_MODEL), D_FF),
        'ffn1_post_g': gain(ks[7], (DEPTH, D_MODEL)),
        'mix_pre_g': gain(ks[8], (DEPTH, D_MODEL)),
        'w_in': nrm(ks[9], (DEPTH, D_MODEL, D_IN_PROJ), D_MODEL),
        'attn_norm_g': gain(ks[10], (DEPTH, D_ATTN)),
        'ssm_lam_re': -0.5 + 0.01 * jax.random.normal(ks[11], (DEPTH, N_SSM_GROUPS, SSM_STATE), f32),
        'ssm_lam_im': lam_im_base + 0.01 * jax.random.normal(ks[12], (DEPTH, N_SSM_GROUPS, SSM_STATE), f32),
        'ssm_log_dt': jax.random.uniform(ks[13], (DEPTH, N_SSM_GROUPS), f32,
                                         minval=math.log(DT_MIN), maxval=math.log(DT_MAX)),
        'ssm_b_re': nrm(ks[14], (DEPTH, N_SSM_GROUPS, SSM_STATE, SSM_GROUP), 2 * SSM_GROUP),
        'ssm_b_im': nrm(ks[15], (DEPTH, N_SSM_GROUPS, SSM_STATE, SSM_GROUP), 2 * SSM_GROUP),
        'ssm_c_re': nrm(ks[16], (DEPTH, N_SSM_GROUPS, SSM_GROUP, SSM_STATE), 2 * SSM_STATE),
        'ssm_c_im': nrm(ks[17], (DEPTH, N_SSM_GROUPS, SSM_GROUP, SSM_STATE), 2 * SSM_STATE),
        'ssm_d': jax.random.normal(ks[18], (DEPTH, N_SSM_GROUPS, SSM_GROUP), f32),
        'ssm_w_glu': nrm(ks[19], (DEPTH, D_SSM, D_SSM), D_SSM),
        'ssm_b_glu': 0.01 * jax.random.normal(ks[20], (DEPTH, D_SSM), f32),
        'ssm_norm_g': gain(ks[21], (DEPTH, D_SSM)),
        'w_out': nrm(ks[22], (DEPTH, D_MIX, D_MODEL), D_MIX),
        'mix_post_g': gain(ks[23], (DEPTH, D_MODEL)),
        'ffn2_pre_g': gain(ks[24], (DEPTH, D_MODEL)),
        'ffn2_w_gate': nrm(ks[25], (DEPTH, D_MODEL, D_FF), D_MODEL),
        'ffn2_w_up': nrm(ks[26], (DEPTH, D_MODEL, D_FF), D_MODEL),
        'ffn2_w_down': nrm(ks[27], (DEPTH, D_FF, D_MODEL), D_FF),
        'ffn2_post_g': gain(ks[28], (DEPTH, D_MODEL)),
        'ple_w_up': nrm(ks[29], (DEPTH, PLE_DIM, D_MODEL), PLE_DIM),
        'ple_w_gate': nrm(ks[30], (DEPTH, D_MODEL, D_MODEL), D_MODEL),
        'ple_post_g': gain(ks[31], (DEPTH, D_MODEL)),
    }


def _fwd_reference(x, p, positions,
              ffn1_pre_g, ffn1_w_gate, ffn1_w_up, ffn1_w_down, ffn1_post_g,
              mix_pre_g, w_in, attn_norm_g,
              ssm_lam_re, ssm_lam_im, ssm_log_dt, ssm_b_re, ssm_b_im, ssm_c_re, ssm_c_im,
              ssm_d, ssm_w_glu, ssm_b_glu, ssm_norm_g, w_out, mix_post_g,
              ffn2_pre_g, ffn2_w_gate, ffn2_w_up, ffn2_w_down, ffn2_post_g,
              ple_w_up, ple_w_gate, ple_post_g):
    b_, s_ = x.shape[0], x.shape[1]
    h = x
    for i in range(DEPTH):
        f = swiglu(rms_norm(h, ffn1_pre_g[i]), ffn1_w_gate[i], ffn1_w_up[i], ffn1_w_down[i])
        h = h + 0.5 * rms_norm(f, ffn1_post_g[i])

        a_in = rms_norm(h, mix_pre_g[i])
        proj = a_in @ w_in[i]
        q, k, v, u = jnp.split(proj, [D_ATTN, 2 * D_ATTN, 3 * D_ATTN], axis=-1)
        q = partial_rotary(q.reshape(b_, s_, N_HEADS, HEAD_DIM), positions)
        k = partial_rotary(k.reshape(b_, s_, N_HEADS, HEAD_DIM), positions)
        v = v.reshape(b_, s_, N_HEADS, HEAD_DIM)
        attn = dilated_mixture_attention(q, k, v)
        ssm = s5_mixer(u, ssm_lam_re[i], ssm_lam_im[i], ssm_log_dt[i], ssm_b_re[i], ssm_b_im[i],
                       ssm_c_re[i], ssm_c_im[i], ssm_d[i], ssm_w_glu[i], ssm_b_glu[i])
        mixed = jnp.concatenate([rms_norm(attn, attn_norm_g[i]), rms_norm(ssm, ssm_norm_g[i])], axis=-1)
        h = h + rms_norm(mixed @ w_out[i], mix_post_g[i])

        f = swiglu(rms_norm(h, ffn2_pre_g[i]), ffn2_w_gate[i], ffn2_w_up[i], ffn2_w_down[i])
        h = h + 0.5 * rms_norm(f, ffn2_post_g[i])

        ple = (p[i] @ ple_w_up[i]) * jax.nn.sigmoid(h @ ple_w_gate[i])
        h = h + rms_norm(ple, ple_post_g[i])
    return h


import jax as _jax
import jax.numpy as _jnp

TWIN_FORMAT = 'train_step'
FWD_PARAMS = ['x', 'p', 'positions', 'ffn1_pre_g', 'ffn1_w_gate', 'ffn1_w_up', 'ffn1_w_down', 'ffn1_post_g', 'mix_pre_g', 'w_in', 'attn_norm_g', 'ssm_lam_re', 'ssm_lam_im', 'ssm_log_dt', 'ssm_b_re', 'ssm_b_im', 'ssm_c_re', 'ssm_c_im', 'ssm_d', 'ssm_w_glu', 'ssm_b_glu', 'ssm_norm_g', 'w_out', 'mix_post_g', 'ffn2_pre_g', 'ffn2_w_gate', 'ffn2_w_up', 'ffn2_w_down', 'ffn2_post_g', 'ple_w_up', 'ple_w_gate', 'ple_post_g']
TWIN_WEIGHTS = ['ffn1_pre_g', 'ffn1_w_gate', 'ffn1_w_up', 'ffn1_w_down', 'ffn1_post_g', 'mix_pre_g', 'w_in', 'attn_norm_g', 'ssm_lam_re', 'ssm_lam_im', 'ssm_log_dt', 'ssm_b_re', 'ssm_b_im', 'ssm_c_re', 'ssm_c_im', 'ssm_d', 'ssm_w_glu', 'ssm_b_glu', 'ssm_norm_g', 'w_out', 'mix_post_g', 'ffn2_pre_g', 'ffn2_w_gate', 'ffn2_w_up', 'ffn2_w_down', 'ffn2_post_g', 'ple_w_up', 'ple_w_gate', 'ple_post_g']
TWIN_DIFF_INPUT = 'x'
TWIN_INPUTS = ['x', 'p', 'positions', 'ffn1_pre_g', 'ffn1_w_gate', 'ffn1_w_up', 'ffn1_w_down', 'ffn1_post_g', 'mix_pre_g', 'w_in', 'attn_norm_g', 'ssm_lam_re', 'ssm_lam_im', 'ssm_log_dt', 'ssm_b_re', 'ssm_b_im', 'ssm_c_re', 'ssm_c_im', 'ssm_d', 'ssm_w_glu', 'ssm_b_glu', 'ssm_norm_g', 'w_out', 'mix_post_g', 'ffn2_pre_g', 'ffn2_w_gate', 'ffn2_w_up', 'ffn2_w_down', 'ffn2_post_g', 'ple_w_up', 'ple_w_gate', 'ple_post_g', 'loss_target', 'm_ffn1_pre_g', 'm_ffn1_w_gate', 'm_ffn1_w_up', 'm_ffn1_w_down', 'm_ffn1_post_g', 'm_mix_pre_g', 'm_w_in', 'm_attn_norm_g', 'm_ssm_lam_re', 'm_ssm_lam_im', 'm_ssm_log_dt', 'm_ssm_b_re', 'm_ssm_b_im', 'm_ssm_c_re', 'm_ssm_c_im', 'm_ssm_d', 'm_ssm_w_glu', 'm_ssm_b_glu', 'm_ssm_norm_g', 'm_w_out', 'm_mix_post_g', 'm_ffn2_pre_g', 'm_ffn2_w_gate', 'm_ffn2_w_up', 'm_ffn2_w_down', 'm_ffn2_post_g', 'm_ple_w_up', 'm_ple_w_gate', 'm_ple_post_g', 'v_ffn1_pre_g', 'v_ffn1_w_gate', 'v_ffn1_w_up', 'v_ffn1_w_down', 'v_ffn1_post_g', 'v_mix_pre_g', 'v_w_in', 'v_attn_norm_g', 'v_ssm_lam_re', 'v_ssm_lam_im', 'v_ssm_log_dt', 'v_ssm_b_re', 'v_ssm_b_im', 'v_ssm_c_re', 'v_ssm_c_im', 'v_ssm_d', 'v_ssm_w_glu', 'v_ssm_b_glu', 'v_ssm_norm_g', 'v_w_out', 'v_mix_post_g', 'v_ffn2_pre_g', 'v_ffn2_w_gate', 'v_ffn2_w_up', 'v_ffn2_w_down', 'v_ffn2_post_g', 'v_ple_w_up', 'v_ple_w_gate', 'v_ple_post_g']
TWIN_OUTPUTS = ['loss', 'grad_x', 'grad_ffn1_pre_g', 'grad_ffn1_w_gate', 'grad_ffn1_w_up', 'grad_ffn1_w_down', 'grad_ffn1_post_g', 'grad_mix_pre_g', 'grad_w_in', 'grad_attn_norm_g', 'grad_ssm_lam_re', 'grad_ssm_lam_im', 'grad_ssm_log_dt', 'grad_ssm_b_re', 'grad_ssm_b_im', 'grad_ssm_c_re', 'grad_ssm_c_im', 'grad_ssm_d', 'grad_ssm_w_glu', 'grad_ssm_b_glu', 'grad_ssm_norm_g', 'grad_w_out', 'grad_mix_post_g', 'grad_ffn2_pre_g', 'grad_ffn2_w_gate', 'grad_ffn2_w_up', 'grad_ffn2_w_down', 'grad_ffn2_post_g', 'grad_ple_w_up', 'grad_ple_w_gate', 'grad_ple_post_g', 'delta_ffn1_pre_g', 'delta_ffn1_w_gate', 'delta_ffn1_w_up', 'delta_ffn1_w_down', 'delta_ffn1_post_g', 'delta_mix_pre_g', 'delta_w_in', 'delta_attn_norm_g', 'delta_ssm_lam_re', 'delta_ssm_lam_im', 'delta_ssm_log_dt', 'delta_ssm_b_re', 'delta_ssm_b_im', 'delta_ssm_c_re', 'delta_ssm_c_im', 'delta_ssm_d', 'delta_ssm_w_glu', 'delta_ssm_b_glu', 'delta_ssm_norm_g', 'delta_w_out', 'delta_mix_post_g', 'delta_ffn2_pre_g', 'delta_ffn2_w_gate', 'delta_ffn2_w_up', 'delta_ffn2_w_down', 'delta_ffn2_post_g', 'delta_ple_w_up', 'delta_ple_w_gate', 'delta_ple_post_g', 'new_m_ffn1_pre_g', 'new_m_ffn1_w_gate', 'new_m_ffn1_w_up', 'new_m_ffn1_w_down', 'new_m_ffn1_post_g', 'new_m_mix_pre_g', 'new_m_w_in', 'new_m_attn_norm_g', 'new_m_ssm_lam_re', 'new_m_ssm_lam_im', 'new_m_ssm_log_dt', 'new_m_ssm_b_re', 'new_m_ssm_b_im', 'new_m_ssm_c_re', 'new_m_ssm_c_im', 'new_m_ssm_d', 'new_m_ssm_w_glu', 'new_m_ssm_b_glu', 'new_m_ssm_norm_g', 'new_m_w_out', 'new_m_mix_post_g', 'new_m_ffn2_pre_g', 'new_m_ffn2_w_gate', 'new_m_ffn2_w_up', 'new_m_ffn2_w_down', 'new_m_ffn2_post_g', 'new_m_ple_w_up', 'new_m_ple_w_gate', 'new_m_ple_post_g', 'new_v_ffn1_pre_g', 'new_v_ffn1_w_gate', 'new_v_ffn1_w_up', 'new_v_ffn1_w_down', 'new_v_ffn1_post_g', 'new_v_mix_pre_g', 'new_v_w_in', 'new_v_attn_norm_g', 'new_v_ssm_lam_re', 'new_v_ssm_lam_im', 'new_v_ssm_log_dt', 'new_v_ssm_b_re', 'new_v_ssm_b_im', 'new_v_ssm_c_re', 'new_v_ssm_c_im', 'new_v_ssm_d', 'new_v_ssm_w_glu', 'new_v_ssm_b_glu', 'new_v_ssm_norm_g', 'new_v_w_out', 'new_v_mix_post_g', 'new_v_ffn2_pre_g', 'new_v_ffn2_w_gate', 'new_v_ffn2_w_up', 'new_v_ffn2_w_down', 'new_v_ffn2_post_g', 'new_v_ple_w_up', 'new_v_ple_w_gate', 'new_v_ple_post_g']
TWIN_LEAF_KINDS = {'loss': 'loss', 'grad_x': 'grad_x', 'grad_ffn1_pre_g': 'grad_w', 'grad_ffn1_w_gate': 'grad_w', 'grad_ffn1_w_up': 'grad_w', 'grad_ffn1_w_down': 'grad_w', 'grad_ffn1_post_g': 'grad_w', 'grad_mix_pre_g': 'grad_w', 'grad_w_in': 'grad_w', 'grad_attn_norm_g': 'grad_w', 'grad_ssm_lam_re': 'grad_w', 'grad_ssm_lam_im': 'grad_w', 'grad_ssm_log_dt': 'grad_w', 'grad_ssm_b_re': 'grad_w', 'grad_ssm_b_im': 'grad_w', 'grad_ssm_c_re': 'grad_w', 'grad_ssm_c_im': 'grad_w', 'grad_ssm_d': 'grad_w', 'grad_ssm_w_glu': 'grad_w', 'grad_ssm_b_glu': 'grad_w', 'grad_ssm_norm_g': 'grad_w', 'grad_w_out': 'grad_w', 'grad_mix_post_g': 'grad_w', 'grad_ffn2_pre_g': 'grad_w', 'grad_ffn2_w_gate': 'grad_w', 'grad_ffn2_w_up': 'grad_w', 'grad_ffn2_w_down': 'grad_w', 'grad_ffn2_post_g': 'grad_w', 'grad_ple_w_up': 'grad_w', 'grad_ple_w_gate': 'grad_w', 'grad_ple_post_g': 'grad_w', 'delta_ffn1_pre_g': 'delta_w', 'delta_ffn1_w_gate': 'delta_w', 'delta_ffn1_w_up': 'delta_w', 'delta_ffn1_w_down': 'delta_w', 'delta_ffn1_post_g': 'delta_w', 'delta_mix_pre_g': 'delta_w', 'delta_w_in': 'delta_w', 'delta_attn_norm_g': 'delta_w', 'delta_ssm_lam_re': 'delta_w', 'delta_ssm_lam_im': 'delta_w', 'delta_ssm_log_dt': 'delta_w', 'delta_ssm_b_re': 'delta_w', 'delta_ssm_b_im': 'delta_w', 'delta_ssm_c_re': 'delta_w', 'delta_ssm_c_im': 'delta_w', 'delta_ssm_d': 'delta_w', 'delta_ssm_w_glu': 'delta_w', 'delta_ssm_b_glu': 'delta_w', 'delta_ssm_norm_g': 'delta_w', 'delta_w_out': 'delta_w', 'delta_mix_post_g': 'delta_w', 'delta_ffn2_pre_g': 'delta_w', 'delta_ffn2_w_gate': 'delta_w', 'delta_ffn2_w_up': 'delta_w', 'delta_ffn2_w_down': 'delta_w', 'delta_ffn2_post_g': 'delta_w', 'delta_ple_w_up': 'delta_w', 'delta_ple_w_gate': 'delta_w', 'delta_ple_post_g': 'delta_w', 'new_m_ffn1_pre_g': 'new_m', 'new_m_ffn1_w_gate': 'new_m', 'new_m_ffn1_w_up': 'new_m', 'new_m_ffn1_w_down': 'new_m', 'new_m_ffn1_post_g': 'new_m', 'new_m_mix_pre_g': 'new_m', 'new_m_w_in': 'new_m', 'new_m_attn_norm_g': 'new_m', 'new_m_ssm_lam_re': 'new_m', 'new_m_ssm_lam_im': 'new_m', 'new_m_ssm_log_dt': 'new_m', 'new_m_ssm_b_re': 'new_m', 'new_m_ssm_b_im': 'new_m', 'new_m_ssm_c_re': 'new_m', 'new_m_ssm_c_im': 'new_m', 'new_m_ssm_d': 'new_m', 'new_m_ssm_w_glu': 'new_m', 'new_m_ssm_b_glu': 'new_m', 'new_m_ssm_norm_g': 'new_m', 'new_m_w_out': 'new_m', 'new_m_mix_post_g': 'new_m', 'new_m_ffn2_pre_g': 'new_m', 'new_m_ffn2_w_gate': 'new_m', 'new_m_ffn2_w_up': 'new_m', 'new_m_ffn2_w_down': 'new_m', 'new_m_ffn2_post_g': 'new_m', 'new_m_ple_w_up': 'new_m', 'new_m_ple_w_gate': 'new_m', 'new_m_ple_post_g': 'new_m', 'new_v_ffn1_pre_g': 'new_v', 'new_v_ffn1_w_gate': 'new_v', 'new_v_ffn1_w_up': 'new_v', 'new_v_ffn1_w_down': 'new_v', 'new_v_ffn1_post_g': 'new_v', 'new_v_mix_pre_g': 'new_v', 'new_v_w_in': 'new_v', 'new_v_attn_norm_g': 'new_v', 'new_v_ssm_lam_re': 'new_v', 'new_v_ssm_lam_im': 'new_v', 'new_v_ssm_log_dt': 'new_v', 'new_v_ssm_b_re': 'new_v', 'new_v_ssm_b_im': 'new_v', 'new_v_ssm_c_re': 'new_v', 'new_v_ssm_c_im': 'new_v', 'new_v_ssm_d': 'new_v', 'new_v_ssm_w_glu': 'new_v', 'new_v_ssm_b_glu': 'new_v', 'new_v_ssm_norm_g': 'new_v', 'new_v_w_out': 'new_v', 'new_v_mix_post_g': 'new_v', 'new_v_ffn2_pre_g': 'new_v', 'new_v_ffn2_w_gate': 'new_v', 'new_v_ffn2_w_up': 'new_v', 'new_v_ffn2_w_down': 'new_v', 'new_v_ffn2_post_g': 'new_v', 'new_v_ple_w_up': 'new_v', 'new_v_ple_w_gate': 'new_v', 'new_v_ple_post_g': 'new_v'}


def _forward(args):
    return _fwd_reference(*[args[k] for k in FWD_PARAMS])


def _output_shape():
    out = _jax.eval_shape(lambda: _forward(_fwd_setup_inputs(0)))
    return out.shape, out.dtype

N_MICROBATCH = 1
ADAM_LR = 0.001
ADAM_B1 = 0.9
ADAM_B2 = 0.999
ADAM_EPS = 1e-08
ADAM_WD = 0.01
ADAM_STEP = 10
PER_EXAMPLE_BATCH_AXIS = {'x': 0, 'p': 1, 'positions': 0, 'loss_target': 0}
SHARED_INPUTS = []
_WEIGHT_DTYPES = {'ffn1_pre_g': _jnp.float32, 'ffn1_w_gate': _jnp.float32, 'ffn1_w_up': _jnp.float32, 'ffn1_w_down': _jnp.float32, 'ffn1_post_g': _jnp.float32, 'mix_pre_g': _jnp.float32, 'w_in': _jnp.float32, 'attn_norm_g': _jnp.float32, 'ssm_lam_re': _jnp.float32, 'ssm_lam_im': _jnp.float32, 'ssm_log_dt': _jnp.float32, 'ssm_b_re': _jnp.float32, 'ssm_b_im': _jnp.float32, 'ssm_c_re': _jnp.float32, 'ssm_c_im': _jnp.float32, 'ssm_d': _jnp.float32, 'ssm_w_glu': _jnp.float32, 'ssm_b_glu': _jnp.float32, 'ssm_norm_g': _jnp.float32, 'w_out': _jnp.float32, 'mix_post_g': _jnp.float32, 'ffn2_pre_g': _jnp.float32, 'ffn2_w_gate': _jnp.float32, 'ffn2_w_up': _jnp.float32, 'ffn2_w_down': _jnp.float32, 'ffn2_post_g': _jnp.float32, 'ple_w_up': _jnp.float32, 'ple_w_gate': _jnp.float32, 'ple_post_g': _jnp.float32}
MOMENT_SCALE = {'ffn1_pre_g': 7.015144e+00, 'ffn1_w_gate': 1.899932e+00, 'ffn1_w_up': 2.378075e+00, 'ffn1_w_down': 3.822783e+00, 'ffn1_post_g': 8.840517e+00, 'mix_pre_g': 1.873744e+01, 'w_in': 1.338010e+01, 'attn_norm_g': 2.434129e+01, 'ssm_lam_re': 3.416182e-01, 'ssm_lam_im': 5.193857e-01, 'ssm_log_dt': 4.087935e+01, 'ssm_b_re': 4.056774e-01, 'ssm_b_im': 3.935669e-01, 'ssm_c_re': 6.783462e-01, 'ssm_c_im': 6.465389e-01, 'ssm_d': 2.528969e+01, 'ssm_w_glu': 3.769786e+00, 'ssm_b_glu': 1.004667e+01, 'ssm_norm_g': 2.232150e+01, 'w_out': 2.389404e+01, 'mix_post_g': 4.218835e+01, 'ffn2_pre_g': 4.693208e+00, 'ffn2_w_gate': 1.344994e+00, 'ffn2_w_up': 2.379743e+00, 'ffn2_w_down': 4.012240e+00, 'ffn2_post_g': 8.906546e+00, 'ple_w_up': 1.024832e+00, 'ple_w_gate': 5.709066e-01, 'ple_post_g': 3.317319e+01}


def _to_microbatches(a, axis):
    t = _jnp.moveaxis(a, axis, 0)
    t = t.reshape((N_MICROBATCH, t.shape[0] // N_MICROBATCH) + t.shape[1:])
    return _jnp.moveaxis(t, 1, axis + 1)


def setup_inputs(seed: int = 0) -> dict:
    inp = _fwd_setup_inputs(seed)
    key = _jax.random.fold_in(_jax.random.key(seed), 7919)
    shape, _ = _output_shape()
    out = dict(inp)
    out["loss_target"] = _jax.random.normal(_jax.random.fold_in(key, 0), shape, _jnp.float32)
    for i, name in enumerate(TWIN_WEIGHTS):
        w = inp[name].astype(_jnp.float32)
        if MOMENT_SCALE is None:
            s = _jnp.sqrt(_jnp.mean(_jnp.square(w)) + 1e-30)
        else:
            s = MOMENT_SCALE[name]
        km, kv = _jax.random.split(_jax.random.fold_in(key, i + 1))
        out[name] = w
        out["m_" + name] = s * _jax.random.normal(km, w.shape, _jnp.float32)
        out["v_" + name] = (s * s) * _jax.random.uniform(kv, w.shape, _jnp.float32, 0.5, 1.5)
    if N_MICROBATCH > 1:
        for name, axis in PER_EXAMPLE_BATCH_AXIS.items():
            out[name] = _to_microbatches(out[name], axis)
    return {'x': out['x'], 'p': out['p'], 'positions': out['positions'], 'ffn1_pre_g': out['ffn1_pre_g'], 'ffn1_w_gate': out['ffn1_w_gate'], 'ffn1_w_up': out['ffn1_w_up'], 'ffn1_w_down': out['ffn1_w_down'], 'ffn1_post_g': out['ffn1_post_g'], 'mix_pre_g': out['mix_pre_g'], 'w_in': out['w_in'], 'attn_norm_g': out['attn_norm_g'], 'ssm_lam_re': out['ssm_lam_re'], 'ssm_lam_im': out['ssm_lam_im'], 'ssm_log_dt': out['ssm_log_dt'], 'ssm_b_re': out['ssm_b_re'], 'ssm_b_im': out['ssm_b_im'], 'ssm_c_re': out['ssm_c_re'], 'ssm_c_im': out['ssm_c_im'], 'ssm_d': out['ssm_d'], 'ssm_w_glu': out['ssm_w_glu'], 'ssm_b_glu': out['ssm_b_glu'], 'ssm_norm_g': out['ssm_norm_g'], 'w_out': out['w_out'], 'mix_post_g': out['mix_post_g'], 'ffn2_pre_g': out['ffn2_pre_g'], 'ffn2_w_gate': out['ffn2_w_gate'], 'ffn2_w_up': out['ffn2_w_up'], 'ffn2_w_down': out['ffn2_w_down'], 'ffn2_post_g': out['ffn2_post_g'], 'ple_w_up': out['ple_w_up'], 'ple_w_gate': out['ple_w_gate'], 'ple_post_g': out['ple_post_g'], 'loss_target': out['loss_target'], 'm_ffn1_pre_g': out['m_ffn1_pre_g'], 'm_ffn1_w_gate': out['m_ffn1_w_gate'], 'm_ffn1_w_up': out['m_ffn1_w_up'], 'm_ffn1_w_down': out['m_ffn1_w_down'], 'm_ffn1_post_g': out['m_ffn1_post_g'], 'm_mix_pre_g': out['m_mix_pre_g'], 'm_w_in': out['m_w_in'], 'm_attn_norm_g': out['m_attn_norm_g'], 'm_ssm_lam_re': out['m_ssm_lam_re'], 'm_ssm_lam_im': out['m_ssm_lam_im'], 'm_ssm_log_dt': out['m_ssm_log_dt'], 'm_ssm_b_re': out['m_ssm_b_re'], 'm_ssm_b_im': out['m_ssm_b_im'], 'm_ssm_c_re': out['m_ssm_c_re'], 'm_ssm_c_im': out['m_ssm_c_im'], 'm_ssm_d': out['m_ssm_d'], 'm_ssm_w_glu': out['m_ssm_w_glu'], 'm_ssm_b_glu': out['m_ssm_b_glu'], 'm_ssm_norm_g': out['m_ssm_norm_g'], 'm_w_out': out['m_w_out'], 'm_mix_post_g': out['m_mix_post_g'], 'm_ffn2_pre_g': out['m_ffn2_pre_g'], 'm_ffn2_w_gate': out['m_ffn2_w_gate'], 'm_ffn2_w_up': out['m_ffn2_w_up'], 'm_ffn2_w_down': out['m_ffn2_w_down'], 'm_ffn2_post_g': out['m_ffn2_post_g'], 'm_ple_w_up': out['m_ple_w_up'], 'm_ple_w_gate': out['m_ple_w_gate'], 'm_ple_post_g': out['m_ple_post_g'], 'v_ffn1_pre_g': out['v_ffn1_pre_g'], 'v_ffn1_w_gate': out['v_ffn1_w_gate'], 'v_ffn1_w_up': out['v_ffn1_w_up'], 'v_ffn1_w_down': out['v_ffn1_w_down'], 'v_ffn1_post_g': out['v_ffn1_post_g'], 'v_mix_pre_g': out['v_mix_pre_g'], 'v_w_in': out['v_w_in'], 'v_attn_norm_g': out['v_attn_norm_g'], 'v_ssm_lam_re': out['v_ssm_lam_re'], 'v_ssm_lam_im': out['v_ssm_lam_im'], 'v_ssm_log_dt': out['v_ssm_log_dt'], 'v_ssm_b_re': out['v_ssm_b_re'], 'v_ssm_b_im': out['v_ssm_b_im'], 'v_ssm_c_re': out['v_ssm_c_re'], 'v_ssm_c_im': out['v_ssm_c_im'], 'v_ssm_d': out['v_ssm_d'], 'v_ssm_w_glu': out['v_ssm_w_glu'], 'v_ssm_b_glu': out['v_ssm_b_glu'], 'v_ssm_norm_g': out['v_ssm_norm_g'], 'v_w_out': out['v_w_out'], 'v_mix_post_g': out['v_mix_post_g'], 'v_ffn2_pre_g': out['v_ffn2_pre_g'], 'v_ffn2_w_gate': out['v_ffn2_w_gate'], 'v_ffn2_w_up': out['v_ffn2_w_up'], 'v_ffn2_w_down': out['v_ffn2_w_down'], 'v_ffn2_post_g': out['v_ffn2_post_g'], 'v_ple_w_up': out['v_ple_w_up'], 'v_ple_w_gate': out['v_ple_w_gate'], 'v_ple_post_g': out['v_ple_post_g']}


def _loss(weights, diff, rest, loss_target):
    with _jax.named_scope("forward"):
        args = {**rest, TWIN_DIFF_INPUT: diff, **{k: w.astype(_WEIGHT_DTYPES[k]) for k, w in weights.items()}}
        y = _forward(args)
    with _jax.named_scope("loss_head"):
        err = _jnp.square(y.astype(_jnp.float32) - loss_target)
        return 0.5 * _jnp.sum(_jnp.mean(err, axis=-1)) if err.ndim else 0.5 * err


def _adamw(w, g, m, v):
    m = ADAM_B1 * m + (1.0 - ADAM_B1) * g
    v = ADAM_B2 * v + (1.0 - ADAM_B2) * _jnp.square(g)
    m_hat = m / (1.0 - ADAM_B1 ** ADAM_STEP)
    v_hat = v / (1.0 - ADAM_B2 ** ADAM_STEP)
    delta = -ADAM_LR * (m_hat / (_jnp.sqrt(v_hat) + ADAM_EPS) + ADAM_WD * w)
    return delta, m, v


def reference(x, p, positions, ffn1_pre_g, ffn1_w_gate, ffn1_w_up, ffn1_w_down, ffn1_post_g, mix_pre_g, w_in, attn_norm_g, ssm_lam_re, ssm_lam_im, ssm_log_dt, ssm_b_re, ssm_b_im, ssm_c_re, ssm_c_im, ssm_d, ssm_w_glu, ssm_b_glu, ssm_norm_g, w_out, mix_post_g, ffn2_pre_g, ffn2_w_gate, ffn2_w_up, ffn2_w_down, ffn2_post_g, ple_w_up, ple_w_gate, ple_post_g, loss_target, m_ffn1_pre_g, m_ffn1_w_gate, m_ffn1_w_up, m_ffn1_w_down, m_ffn1_post_g, m_mix_pre_g, m_w_in, m_attn_norm_g, m_ssm_lam_re, m_ssm_lam_im, m_ssm_log_dt, m_ssm_b_re, m_ssm_b_im, m_ssm_c_re, m_ssm_c_im, m_ssm_d, m_ssm_w_glu, m_ssm_b_glu, m_ssm_norm_g, m_w_out, m_mix_post_g, m_ffn2_pre_g, m_ffn2_w_gate, m_ffn2_w_up, m_ffn2_w_down, m_ffn2_post_g, m_ple_w_up, m_ple_w_gate, m_ple_post_g, v_ffn1_pre_g, v_ffn1_w_gate, v_ffn1_w_up, v_ffn1_w_down, v_ffn1_post_g, v_mix_pre_g, v_w_in, v_attn_norm_g, v_ssm_lam_re, v_ssm_lam_im, v_ssm_log_dt, v_ssm_b_re, v_ssm_b_im, v_ssm_c_re, v_ssm_c_im, v_ssm_d, v_ssm_w_glu, v_ssm_b_glu, v_ssm_norm_g, v_w_out, v_mix_post_g, v_ffn2_pre_g, v_ffn2_w_gate, v_ffn2_w_up, v_ffn2_w_down, v_ffn2_post_g, v_ple_w_up, v_ple_w_gate, v_ple_post_g):
    given = dict(x=x, p=p, positions=positions, ffn1_pre_g=ffn1_pre_g, ffn1_w_gate=ffn1_w_gate, ffn1_w_up=ffn1_w_up, ffn1_w_down=ffn1_w_down, ffn1_post_g=ffn1_post_g, mix_pre_g=mix_pre_g, w_in=w_in, attn_norm_g=attn_norm_g, ssm_lam_re=ssm_lam_re, ssm_lam_im=ssm_lam_im, ssm_log_dt=ssm_log_dt, ssm_b_re=ssm_b_re, ssm_b_im=ssm_b_im, ssm_c_re=ssm_c_re, ssm_c_im=ssm_c_im, ssm_d=ssm_d, ssm_w_glu=ssm_w_glu, ssm_b_glu=ssm_b_glu, ssm_norm_g=ssm_norm_g, w_out=w_out, mix_post_g=mix_post_g, ffn2_pre_g=ffn2_pre_g, ffn2_w_gate=ffn2_w_gate, ffn2_w_up=ffn2_w_up, ffn2_w_down=ffn2_w_down, ffn2_post_g=ffn2_post_g, ple_w_up=ple_w_up, ple_w_gate=ple_w_gate, ple_post_g=ple_post_g, loss_target=loss_target, m_ffn1_pre_g=m_ffn1_pre_g, m_ffn1_w_gate=m_ffn1_w_gate, m_ffn1_w_up=m_ffn1_w_up, m_ffn1_w_down=m_ffn1_w_down, m_ffn1_post_g=m_ffn1_post_g, m_mix_pre_g=m_mix_pre_g, m_w_in=m_w_in, m_attn_norm_g=m_attn_norm_g, m_ssm_lam_re=m_ssm_lam_re, m_ssm_lam_im=m_ssm_lam_im, m_ssm_log_dt=m_ssm_log_dt, m_ssm_b_re=m_ssm_b_re, m_ssm_b_im=m_ssm_b_im, m_ssm_c_re=m_ssm_c_re, m_ssm_c_im=m_ssm_c_im, m_ssm_d=m_ssm_d, m_ssm_w_glu=m_ssm_w_glu, m_ssm_b_glu=m_ssm_b_glu, m_ssm_norm_g=m_ssm_norm_g, m_w_out=m_w_out, m_mix_post_g=m_mix_post_g, m_ffn2_pre_g=m_ffn2_pre_g, m_ffn2_w_gate=m_ffn2_w_gate, m_ffn2_w_up=m_ffn2_w_up, m_ffn2_w_down=m_ffn2_w_down, m_ffn2_post_g=m_ffn2_post_g, m_ple_w_up=m_ple_w_up, m_ple_w_gate=m_ple_w_gate, m_ple_post_g=m_ple_post_g, v_ffn1_pre_g=v_ffn1_pre_g, v_ffn1_w_gate=v_ffn1_w_gate, v_ffn1_w_up=v_ffn1_w_up, v_ffn1_w_down=v_ffn1_w_down, v_ffn1_post_g=v_ffn1_post_g, v_mix_pre_g=v_mix_pre_g, v_w_in=v_w_in, v_attn_norm_g=v_attn_norm_g, v_ssm_lam_re=v_ssm_lam_re, v_ssm_lam_im=v_ssm_lam_im, v_ssm_log_dt=v_ssm_log_dt, v_ssm_b_re=v_ssm_b_re, v_ssm_b_im=v_ssm_b_im, v_ssm_c_re=v_ssm_c_re, v_ssm_c_im=v_ssm_c_im, v_ssm_d=v_ssm_d, v_ssm_w_glu=v_ssm_w_glu, v_ssm_b_glu=v_ssm_b_glu, v_ssm_norm_g=v_ssm_norm_g, v_w_out=v_w_out, v_mix_post_g=v_mix_post_g, v_ffn2_pre_g=v_ffn2_pre_g, v_ffn2_w_gate=v_ffn2_w_gate, v_ffn2_w_up=v_ffn2_w_up, v_ffn2_w_down=v_ffn2_w_down, v_ffn2_post_g=v_ffn2_post_g, v_ple_w_up=v_ple_w_up, v_ple_w_gate=v_ple_w_gate, v_ple_post_g=v_ple_post_g)
    weights = {n: given[n] for n in TWIN_WEIGHTS}
    shared = {n: given[n] for n in SHARED_INPUTS}
    per_example = {n: given[n] for n in ['x', 'p', 'positions']}
    grad_fn = _jax.value_and_grad(_loss, argnums=(0, 1))

    def one_microbatch(ex, loss_target):
        ex = dict(ex)
        diff = ex.pop(TWIN_DIFF_INPUT)
        return grad_fn(weights, diff, {**shared, **ex}, loss_target)

    if N_MICROBATCH == 1:
        loss, (grad_w, grad_x) = one_microbatch(per_example, given["loss_target"])
    else:
        def body(carry, xs):
            loss_sum, grad_sum = carry
            l_k, (gw_k, gx_k) = one_microbatch(xs[0], xs[1])
            with _jax.named_scope("update"):
                return (loss_sum + l_k, _jax.tree.map(_jnp.add, grad_sum, gw_k)), gx_k

        init = (_jnp.zeros((), _jnp.float32), _jax.tree.map(_jnp.zeros_like, weights))
        (loss, grad_w), grad_x = _jax.lax.scan(body, init, (per_example, given["loss_target"]))
    with _jax.named_scope("update"):
        delta_w, new_m, new_v = {}, {}, {}
        for n in TWIN_WEIGHTS:
            delta_w[n], new_m[n], new_v[n] = _adamw(weights[n], grad_w[n], given["m_" + n], given["v_" + n])
    return (loss, grad_x, *[grad_w[n] for n in TWIN_WEIGHTS], *[delta_w[n] for n in TWIN_WEIGHTS],
            *[new_m[n] for n in TWIN_WEIGHTS], *[new_v[n] for n in TWIN_WEIGHTS])
```

```python
import functools
import math

import numpy as np
import jax
import jax.numpy as jnp
from jax import lax
from jax.experimental import pallas as pl
from jax.experimental.pallas import tpu as pltpu

F32 = jnp.float32
BF16 = jnp.bfloat16
S = jax.ShapeDtypeStruct
MESH = pl.DeviceIdType.MESH

D = 1024
DA = 512
DSS = 512
HD = 64
NH = 8
BAND = 128
NSH = 4
DFS = 704
PLE = 256
EPS = 1e-6
ROPE_THETA = 500000.0
PATTERN_DILATIONS = (1, 4, 16)
NCH = 8
NLB = 16
ADAM_LR, ADAM_B1, ADAM_B2, ADAM_EPS, ADAM_WD, ADAM_STEP = 0.001, 0.9, 0.999, 1e-08, 0.01, 10

VMEM_LIMIT = 56 * 1024 * 1024
TM = 512
TMB = 256

BIG = (
    ("ffn1_w_gate", D, DFS), ("ffn1_w_up", D, DFS), ("ffn1_w_down", DFS, D),
    ("w_in", D, 512), ("ssm_w_glu", 128, 512), ("w_out", 256, D),
    ("ffn2_w_gate", D, DFS), ("ffn2_w_up", D, DFS), ("ffn2_w_down", DFS, D),
    ("ple_w_up", PLE, 256), ("ple_w_gate", 256, D),
)
SMALL = ("ffn1_pre_g", "ffn1_post_g", "mix_pre_g", "attn_norm_g", "ssm_lam_re", "ssm_lam_im", "ssm_log_dt",
         "ssm_b_re", "ssm_b_im", "ssm_c_re", "ssm_c_im", "ssm_d", "ssm_b_glu", "ssm_norm_g", "mix_post_g",
         "ffn2_pre_g", "ffn2_post_g", "ple_post_g")
WEIGHTS = ("ffn1_pre_g", "ffn1_w_gate", "ffn1_w_up", "ffn1_w_down", "ffn1_post_g", "mix_pre_g", "w_in", "attn_norm_g",
           "ssm_lam_re", "ssm_lam_im", "ssm_log_dt", "ssm_b_re", "ssm_b_im", "ssm_c_re", "ssm_c_im", "ssm_d",
           "ssm_w_glu", "ssm_b_glu", "ssm_norm_g", "w_out", "mix_post_g", "ffn2_pre_g", "ffn2_w_gate", "ffn2_w_up",
           "ffn2_w_down", "ffn2_post_g", "ple_w_up", "ple_w_gate", "ple_post_g")


def _pc(body, **kw):
    return pl.pallas_call(body, **kw)


def _cp(n_grid):
    return pltpu.CompilerParams(dimension_semantics=("arbitrary",) * n_grid, vmem_limit_bytes=VMEM_LIMIT)


def _dot(a, b):
    return jnp.dot(a, b, preferred_element_type=F32)


def _dot_nt(a, b):
    return lax.dot_general(a, b, (((1,), (1,)), ((), ())), preferred_element_type=F32)


def _dot_tn(a, b):
    return lax.dot_general(a, b, (((0,), (0,)), ((), ())), preferred_element_type=F32)


def _split(a):
    hi = a.astype(BF16)
    return hi, (a - hi.astype(F32)).astype(BF16)


def _dot3(fn, a, b):
    ah, al = _split(a)
    bh, bl = _split(b)
    return fn(ah, bh) + fn(ah, bl) + fn(al, bh)


def _rms_fwd(x, g):
    r = lax.rsqrt(jnp.mean(x * x, axis=-1, keepdims=True) + EPS)
    return x * r * g


def _rms_bwd(dy, x, g):
    r = lax.rsqrt(jnp.mean(x * x, axis=-1, keepdims=True) + EPS)
    xr = x * r
    gd = dy * g
    dx = r * (gd - xr * jnp.mean(gd * xr, axis=-1, keepdims=True))
    dg = jnp.sum(dy * xr, axis=0, keepdims=True)
    return dx, dg


def _gelu(y):
    k = math.sqrt(2.0 / math.pi)
    return 0.5 * y * (1.0 + jnp.tanh(k * (y + 0.044715 * y * y * y)))


def _gelu_grad(y):
    k = math.sqrt(2.0 / math.pi)
    t = jnp.tanh(k * (y + 0.044715 * y * y * y))
    return 0.5 * (1.0 + t) + 0.5 * y * (1.0 - t * t) * k * (1.0 + 3 * 0.044715 * y * y)


def _gain_spec(n, layer):
    return pl.BlockSpec((None, 1, n), lambda *_: (layer, 0, 0))


def _row_acc_spec(n):
    return pl.BlockSpec((1, n), lambda *_: (0, 0))


def _rot_tables(pos_col):
    T = pos_col.shape[0]
    half = HD // 8
    inv = (ROPE_THETA ** (-np.arange(half, dtype=np.float32) * (2.0 / (2 * half)))).astype(np.float32)
    lane_freq = np.tile(np.concatenate([inv, inv, np.zeros(HD - 2 * half, np.float32)]), NH)[None, :]

    def body(p_ref, f_ref, c_ref, s1_ref, s2_ref):
        ang = p_ref[...] * f_ref[...]
        d = lax.broadcasted_iota(jnp.int32, ang.shape, 1) % HD
        cs = jnp.cos(ang)
        sn = jnp.sin(ang)
        c_ref[...] = jnp.where(d < 2 * half, cs, 1.0)
        s1_ref[...] = jnp.where(d < half, -sn, 0.0)
        s2_ref[...] = jnp.where((d >= half) & (d < 2 * half), sn, 0.0)

    tm = TM
    return _pc(body, name="rot_tables", grid=(T // tm,),
               in_specs=[pl.BlockSpec((tm, 1), lambda i: (i, 0)), pl.BlockSpec((1, DA), lambda i: (0, 0))],
               out_specs=[pl.BlockSpec((tm, DA), lambda i: (i, 0))] * 3,
               out_shape=[S((T, DA), F32)] * 3, compiler_params=_cp(1))(pos_col, jnp.asarray(lane_freq))


def _rot_fwd(t, c, s1, s2):
    return t * c + pltpu.roll(t, DA - 8, 1) * s1 + pltpu.roll(t, 8, 1) * s2


def _rot_bwd(g, c, s1, s2):
    return g * c + pltpu.roll(g * s1, 8, 1) + pltpu.roll(g * s2, DA - 8, 1)


def _ffn_fwd(h, pre_g, post_g, wg, wu, wd, layer, tag):
    T = h.shape[0]
    tm = TM
    nt = T // tm

    def body(h_ref, pg_ref, qg_ref, wg_ref, wu_ref, wd_ref, ho_ref, a_ref, b_ref, f_ref, xn_ref, xs, facc):
        j = pl.program_id(1)

        @pl.when(j == 0)
        def _():
            xb = _rms_fwd(h_ref[...], pg_ref[...]).astype(BF16)
            xs[...] = xb
            xn_ref[...] = xb
            facc[...] = jnp.zeros_like(facc)

        xb = xs[...]
        ab = _dot(xb, wg_ref[...]).astype(BF16)
        bb = _dot(xb, wu_ref[...]).astype(BF16)
        a_ref[...] = ab
        b_ref[...] = bb
        a = ab.astype(F32)
        hh = (a * jax.nn.sigmoid(a) * bb.astype(F32)).astype(BF16)
        facc[...] += _dot(hh, wd_ref[...])

        @pl.when(j == NSH - 1)
        def _():
            f = facc[...]
            f_ref[...] = f
            ho_ref[...] = h_ref[...] + 0.5 * _rms_fwd(f, qg_ref[...])

    row = pl.BlockSpec((tm, D), lambda i, j: (i, 0))
    act = pl.BlockSpec((None, tm, DFS), lambda i, j: (j, i, 0))
    wcol = pl.BlockSpec((None, None, D, DFS), lambda i, j: (j, layer, 0, 0))
    wrow = pl.BlockSpec((None, None, DFS, D), lambda i, j: (j, layer, 0, 0))
    return _pc(body, name=f"ffn_fwd_{tag}_l{layer}", grid=(nt, NSH),
               in_specs=[row, _gain_spec(D, layer), _gain_spec(D, layer), wcol, wcol, wrow],
               out_specs=[row, act, act, row, row],
               out_shape=[S((T, D), F32), S((NSH, T, DFS), BF16), S((NSH, T, DFS), BF16), S((T, D), F32), S((T, D), BF16)],
               scratch_shapes=[pltpu.VMEM((tm, D), BF16), pltpu.VMEM((tm, D), F32)],
               compiler_params=_cp(2))(h, pre_g, post_g, wg, wu, wd)


def _ffn_bwd(dout, h, f, a, b, pre_g, post_g, wg, wu, wd, layer, tag):
    T = h.shape[0]
    tm = TMB
    nt = T // tm

    def body(do_ref, h_ref, f_ref, a_ref, b_ref, pg_ref, qg_ref, wg_ref, wu_ref, wd_ref,
             dh_ref, df_ref, da_ref, db_ref, hh_ref, dpg_ref, dqg_ref, dfs, dxn):
        i = pl.program_id(0)
        j = pl.program_id(1)

        @pl.when((i == 0) & (j == 0))
        def _():
            dpg_ref[...] = jnp.zeros_like(dpg_ref)
            dqg_ref[...] = jnp.zeros_like(dqg_ref)

        @pl.when(j == 0)
        def _():
            df, dq = _rms_bwd(0.5 * do_ref[...], f_ref[...], qg_ref[...])
            dqg_ref[...] += dq
            dfb = df.astype(BF16)
            dfs[...] = dfb
            df_ref[...] = dfb
            dxn[...] = jnp.zeros_like(dxn)

        dhh = _dot_nt(dfs[...], wd_ref[...])
        av = a_ref[...].astype(F32)
        bv = b_ref[...].astype(F32)
        sg = jax.nn.sigmoid(av)
        sa = av * sg
        hh_ref[...] = (sa * bv).astype(BF16)
        dab = (dhh * bv * (sg * (1.0 + av * (1.0 - sg)))).astype(BF16)
        dbb = (dhh * sa).astype(BF16)
        da_ref[...] = dab
        db_ref[...] = dbb
        dxn[...] += _dot_nt(dab, wg_ref[...]) + _dot_nt(dbb, wu_ref[...])

        @pl.when(j == NSH - 1)
        def _():
            dx, dp = _rms_bwd(dxn[...], h_ref[...], pg_ref[...])
            dpg_ref[...] += dp
            dh_ref[...] = do_ref[...] + dx

    row = pl.BlockSpec((tm, D), lambda i, j: (i, 0))
    act = pl.BlockSpec((None, tm, DFS), lambda i, j: (j, i, 0))
    wcol = pl.BlockSpec((None, None, D, DFS), lambda i, j: (j, layer, 0, 0))
    wrow = pl.BlockSpec((None, None, DFS, D), lambda i, j: (j, layer, 0, 0))
    return _pc(body, name=f"ffn_bwd_{tag}_l{layer}", grid=(nt, NSH),
               in_specs=[row, row, row, act, act, _gain_spec(D, layer), _gain_spec(D, layer), wcol, wcol, wrow],
               out_specs=[row, row, act, act, act, _row_acc_spec(D), _row_acc_spec(D)],
               out_shape=[S((T, D), F32), S((T, D), BF16), S((NSH, T, DFS), BF16), S((NSH, T, DFS), BF16),
                          S((NSH, T, DFS), BF16), S((1, D), F32), S((1, D), F32)],
               scratch_shapes=[pltpu.VMEM((tm, D), BF16), pltpu.VMEM((tm, D), F32)],
               compiler_params=_cp(2))(dout, h, f, a, b, pre_g, post_g, wg, wu, wd)


def _dw(A, B, buf, layer, kb, nb, a_idx, b_idx, name):
    T = A.shape[1]
    tt = TM
    nt = T // tt

    def body(a_ref, b_ref, buf_ref, o_ref, acc):
        t = pl.program_id(1)

        @pl.when(t == 0)
        def _():
            acc[...] = jnp.zeros_like(acc)

        acc[...] += _dot_tn(a_ref[...].astype(BF16), b_ref[...].astype(BF16))

        @pl.when(t == nt - 1)
        def _():
            o_ref[...] = acc[...].astype(o_ref.dtype)

    return _pc(body, name=name, grid=(NSH, nt),
               in_specs=[pl.BlockSpec((None, tt, kb), lambda j, t: (a_idx(j)[0], t, a_idx(j)[1])),
                         pl.BlockSpec((None, tt, nb), lambda j, t: (b_idx(j)[0], t, b_idx(j)[1])),
                         pl.BlockSpec(memory_space=pl.ANY)],
               out_specs=pl.BlockSpec((None, None, kb, nb), lambda j, t: (j, layer, 0, 0)),
               out_shape=S(buf.shape, buf.dtype), input_output_aliases={2: 0},
               scratch_shapes=[pltpu.VMEM((kb, nb), F32)], compiler_params=_cp(2))(A, B, buf)


def _mix_proj(h, pre_g, win, rot, layer):
    T = h.shape[0]
    tm = TM

    def body(h_ref, g_ref, w_ref, c_ref, s1_ref, s2_ref, p_ref, xn_ref, xs):
        j = pl.program_id(1)

        @pl.when(j == 0)
        def _():
            xb = _rms_fwd(h_ref[...], g_ref[...]).astype(BF16)
            xs[...] = xb
            xn_ref[...] = xb

        o = _dot(xs[...], w_ref[...])

        @pl.when(j < 2)
        def _():
            p_ref[...] = _rot_fwd(o, c_ref[...], s1_ref[...], s2_ref[...])

        @pl.when(j >= 2)
        def _():
            p_ref[...] = o

    row = pl.BlockSpec((tm, D), lambda i, j: (i, 0))
    half = pl.BlockSpec((tm, DA), lambda i, j: (i, 0))
    return _pc(body, name=f"mix_proj_l{layer}", grid=(T // tm, NSH),
               in_specs=[row, _gain_spec(D, layer), pl.BlockSpec((None, None, D, DA), lambda i, j: (j, layer, 0, 0)),
                         half, half, half],
               out_specs=[pl.BlockSpec((None, tm, DA), lambda i, j: (j, i, 0)), row],
               out_shape=[S((NSH, T, DA), F32), S((T, D), BF16)],
               scratch_shapes=[pltpu.VMEM((tm, D), BF16)], compiler_params=_cp(2))(h, pre_g, win, *rot)


def _mix_proj_bwd(dq, dk, dv, du, dh_up, h, pre_g, win, rot, layer):
    T = h.shape[0]
    tm = TM

    def body(dq_ref, dk_ref, dv_ref, du_ref, up_ref, h_ref, g_ref, w_ref, c_ref, s1_ref, s2_ref,
             dh_ref, dp_ref, dg_ref, dps, dxn):
        i = pl.program_id(0)
        j = pl.program_id(1)

        @pl.when((i == 0) & (j == 0))
        def _():
            dg_ref[...] = jnp.zeros_like(dg_ref)

        @pl.when(j == 0)
        def _():
            dxn[...] = jnp.zeros_like(dxn)
            dps[...] = _rot_bwd(dq_ref[...], c_ref[...], s1_ref[...], s2_ref[...]).astype(BF16)

        @pl.when(j == 1)
        def _():
            dps[...] = _rot_bwd(dk_ref[...], c_ref[...], s1_ref[...], s2_ref[...]).astype(BF16)

        @pl.when(j == 2)
        def _():
            dps[...] = dv_ref[...].astype(BF16)

        @pl.when(j == 3)
        def _():
            dps[...] = du_ref[...].astype(BF16)

        dpb = dps[...]
        dp_ref[...] = dpb
        dxn[...] += _dot_nt(dpb, w_ref[...])

        @pl.when(j == NSH - 1)
        def _():
            dx, dg = _rms_bwd(dxn[...], h_ref[...], g_ref[...])
            dg_ref[...] += dg
            dh_ref[...] = up_ref[...] + dx

    row = pl.BlockSpec((tm, D), lambda i, j: (i, 0))
    half = pl.BlockSpec((tm, DA), lambda i, j: (i, 0))
    return _pc(body, name=f"mix_proj_bwd_l{layer}", grid=(T // tm, NSH),
               in_specs=[half, half, half, half, row, row, _gain_spec(D, layer),
                         pl.BlockSpec((None, None, D, DA), lambda i, j: (j, layer, 0, 0)), half, half, half],
               out_specs=[row, pl.BlockSpec((None, tm, DA), lambda i, j: (j, i, 0)), _row_acc_spec(D)],
               out_shape=[S((T, D), F32), S((NSH, T, DA), BF16), S((1, D), F32)],
               scratch_shapes=[pltpu.VMEM((tm, DA), BF16), pltpu.VMEM((tm, D), F32)],
               compiler_params=_cp(2))(dq, dk, dv, du, dh_up, h, pre_g, win, *rot)


def _band_masks(b):
    qi = lax.broadcasted_iota(jnp.int32, (BAND, BAND), 0)
    kj = lax.broadcasted_iota(jnp.int32, (BAND, BAND), 1)
    return kj <= qi, (kj >= qi) & (b > 0)


def _attn_fwd(P, d, layer):
    T = P.shape[1]
    nb = T // d // BAND
    Pv = P.reshape(NSH, T // d, d * DA)
    scale = HD ** -0.5

    def body(q_ref, kp_ref, kc_ref, vp_ref, vc_ref, o_ref, l_ref):
        b = pl.program_id(1)
        mask_c, mask_p = _band_masks(b)
        for hd in range(NH):
            sl = slice(hd * HD, (hd + 1) * HD)
            q = q_ref[:, sl].astype(BF16)
            sc = jnp.where(mask_c, _dot_nt(q, kc_ref[:, sl].astype(BF16)) * scale, -1e30)
            sp = jnp.where(mask_p, _dot_nt(q, kp_ref[:, sl].astype(BF16)) * scale, -1e30)
            m = jnp.maximum(jnp.max(sc, axis=-1, keepdims=True), jnp.max(sp, axis=-1, keepdims=True))
            ec = jnp.exp(sc - m)
            ep = jnp.exp(sp - m)
            den = jnp.sum(ec, axis=-1, keepdims=True) + jnp.sum(ep, axis=-1, keepdims=True)
            o = _dot(ec.astype(BF16), vc_ref[:, sl].astype(BF16)) + _dot(ep.astype(BF16), vp_ref[:, sl].astype(BF16))
            o_ref[:, sl] = o / den
            l_ref[:, sl] = jnp.broadcast_to(m + jnp.log(den), (BAND, HD))

    def cur(s):
        return pl.BlockSpec((None, BAND, DA), lambda r, b: (s, b, r))

    def prev(s):
        return pl.BlockSpec((None, BAND, DA), lambda r, b: (s, jnp.maximum(b - 1, 0), r))

    out = pl.BlockSpec((BAND, DA), lambda r, b: (b, r))
    o, l = _pc(body, name=f"attn_fwd_d{d}_l{layer}", grid=(d, nb),
               in_specs=[cur(0), prev(1), cur(1), prev(2), cur(2)], out_specs=[out, out],
               out_shape=[S((T // d, d * DA), F32)] * 2, compiler_params=_cp(2))(Pv, Pv, Pv, Pv, Pv)
    return o.reshape(T, DA), l.reshape(T, DA)


def _attn_bwd(P, dO, lse, delta, acc, d, layer):
    T = P.shape[1]
    nb = T // d // BAND
    Tv, Cv = T // d, d * DA
    Pv = P.reshape(NSH, Tv, Cv)
    scale = HD ** -0.5
    first = acc is None

    def body(*refs):
        q_ref, kp_ref, kc_ref, vp_ref, vc_ref, do_ref, l_ref, dl_ref = refs[:8]
        if first:
            dq_ref, dk_ref, dv_ref, ck, cv = refs[8:]
        else:
            aq_ref, ak_ref, av_ref, dq_ref, dk_ref, dv_ref, ck, cv = refs[8:]
        b = pl.program_id(1)

        @pl.when(b == 0)
        def _():
            ck[...] = jnp.zeros_like(ck)
            cv[...] = jnp.zeros_like(cv)

        @pl.when(b < nb)
        def _():
            mask_c, mask_p = _band_masks(b)
            for hd in range(NH):
                sl = slice(hd * HD, (hd + 1) * HD)
                one = slice(hd * HD, hd * HD + 1)
                q = q_ref[:, sl].astype(BF16)
                kc = kc_ref[:, sl].astype(BF16)
                kp = kp_ref[:, sl].astype(BF16)
                do = do_ref[:, sl].astype(BF16)
                lrow = l_ref[:, one]
                drow = dl_ref[:, one]
                pc = jnp.where(mask_c, jnp.exp(_dot_nt(q, kc) * scale - lrow), 0.0)
                pp = jnp.where(mask_p, jnp.exp(_dot_nt(q, kp) * scale - lrow), 0.0)
                dsc = (pc * (_dot_nt(do, vc_ref[:, sl].astype(BF16)) - drow) * scale).astype(BF16)
                dsp = (pp * (_dot_nt(do, vp_ref[:, sl].astype(BF16)) - drow) * scale).astype(BF16)
                dq = _dot(dsc, kc) + _dot(dsp, kp)
                dkp = ck[:, sl] + _dot_tn(dsp, q)
                dvp = cv[:, sl] + _dot_tn(pp.astype(BF16), do)
                if first:
                    dq_ref[:, sl] = dq
                    dk_ref[:, sl] = dkp
                    dv_ref[:, sl] = dvp
                else:
                    dq_ref[:, sl] = aq_ref[:, sl] + dq
                    dk_ref[:, sl] = ak_ref[:, sl] + dkp
                    dv_ref[:, sl] = av_ref[:, sl] + dvp
                ck[:, sl] = _dot_tn(dsc, q)
                cv[:, sl] = _dot_tn(pc.astype(BF16), do)

        @pl.when(b == nb)
        def _():
            if first:
                dk_ref[...] = ck[...]
                dv_ref[...] = cv[...]
            else:
                dk_ref[...] = ak_ref[...] + ck[...]
                dv_ref[...] = av_ref[...] + cv[...]

    def qb(b):
        return jnp.minimum(b, nb - 1)

    def cur(s):
        return pl.BlockSpec((None, BAND, DA), lambda r, b: (s, qb(b), r))

    def prev(s):
        return pl.BlockSpec((None, BAND, DA), lambda r, b: (s, jnp.maximum(qb(b) - 1, 0), r))

    qrow = pl.BlockSpec((BAND, DA), lambda r, b: (qb(b), r))
    krow = pl.BlockSpec((BAND, DA), lambda r, b: (jnp.maximum(b - 1, 0), r))
    view = lambda t: t.reshape(Tv, Cv)
    ins = [Pv, Pv, Pv, Pv, Pv, view(dO), view(lse), view(delta)]
    specs = [cur(0), prev(1), cur(1), prev(2), cur(2), qrow, qrow, qrow]
    if not first:
        ins += [view(t) for t in acc]
        specs += [qrow, krow, krow]
    dq, dk, dv = _pc(body, name=f"attn_bwd_d{d}_l{layer}", grid=(d, nb + 1), in_specs=specs,
                     out_specs=[qrow, krow, krow], out_shape=[S((Tv, Cv), F32)] * 3,
                     scratch_shapes=[pltpu.VMEM((BAND, DA), F32)] * 2, compiler_params=_cp(2))(*ins)
    return dq.reshape(T, DA), dk.reshape(T, DA), dv.reshape(T, DA)


def _ssm_prep(lam_re, lam_im, log_dt, b_re, b_im, c_re, c_im):
    dt = jnp.exp(log_dt)[:, None]
    er = jnp.exp(lam_re * dt)
    a_re = er * jnp.cos(lam_im * dt)
    a_im = er * jnp.sin(lam_im * dt)
    nr, ni = a_re - 1.0, a_im
    den = lam_re * lam_re + lam_im * lam_im
    cr = (nr * lam_re + ni * lam_im) / den
    ci = (ni * lam_re - nr * lam_im) / den
    bbr = cr[..., None] * b_re - ci[..., None] * b_im
    bbi = cr[..., None] * b_im + ci[..., None] * b_re
    eye = jnp.eye(8, dtype=F32)

    def bblock(bb):
        t = bb.reshape(4, 8, 64, 16).transpose(0, 1, 3, 2)
        return (t[:, :, :, None, :] * eye[None, :, None, :, None]).reshape(4, 128, 512)

    def cblock(cc):
        t = cc.reshape(4, 8, 16, 64).transpose(0, 1, 3, 2)
        return (t[:, :, :, None, :] * eye[None, :, None, :, None]).reshape(4, 512, 128)

    return (a_re.reshape(NLB, 1, 128), a_im.reshape(NLB, 1, 128), bblock(bbr), bblock(bbi), cblock(c_re), cblock(c_im))


def _ssm_in(P, bre, bim, layer):
    T = P.shape[1]
    tm = TM

    def body(u_ref, br_ref, bi_ref, or_ref, oi_ref):
        for s in range(4):
            uc = u_ref[:, s * 128:(s + 1) * 128]
            r = _dot3(_dot, uc, br_ref[s])
            m = _dot3(_dot, uc, bi_ref[s])
            for q in range(4):
                or_ref[4 * s + q] = r[:, q * 128:(q + 1) * 128]
                oi_ref[4 * s + q] = m[:, q * 128:(q + 1) * 128]

    whole = pl.BlockSpec((4, 128, 512), lambda i: (0, 0, 0))
    st = pl.BlockSpec((NLB, tm, 128), lambda i: (0, i, 0))
    return _pc(body, name=f"ssm_in_l{layer}", grid=(T // tm,),
               in_specs=[pl.BlockSpec((None, tm, DSS), lambda i: (3, i, 0)), whole, whole], out_specs=[st, st],
               out_shape=[S((NLB, T, 128), F32)] * 2, compiler_params=_cp(1))(P, bre, bim)


def _scan(br, bi, a_re, a_im, reverse, layer):
    T = br.shape[1]
    ch = T // NCH
    nbk = 2
    sgn = -1.0 if reverse else 1.0

    def body(br_ref, bi_ref, ar_ref, ai_ref, xr_ref, xi_ref):
        brs, bis = [br_ref.at[k] for k in range(nbk)], [bi_ref.at[k] for k in range(nbk)]
        xrs, xis = [xr_ref.at[k] for k in range(nbk)], [xi_ref.at[k] for k in range(nbk)]
        ars = [jnp.broadcast_to(ar_ref[k], (NCH, 128)) for k in range(nbk)]
        ais = [sgn * jnp.broadcast_to(ai_ref[k], (NCH, 128)) for k in range(nbk)]
        zero = jnp.zeros((NCH, 128), F32)
        one = jnp.ones((NCH, 128), F32)

        def at(i):
            return pl.ds((ch - 1 - i) if reverse else i, NCH, stride=ch)

        def local(i, carry):
            out = []
            for k in range(nbk):
                xr, xi, pr, pi_ = carry[k]
                nr = ars[k] * xr - ais[k] * xi + brs[k][at(i), :]
                ni = ars[k] * xi + ais[k] * xr + bis[k][at(i), :]
                xrs[k][at(i), :] = nr
                xis[k][at(i), :] = ni
                out.append((nr, ni, ars[k] * pr - ais[k] * pi_, ars[k] * pi_ + ais[k] * pr))
            return tuple(out)

        ends = lax.fori_loop(0, ch, local, tuple((zero, zero, one, zero) for _ in range(nbk)))
        row = lax.broadcasted_iota(jnp.int32, (NCH, 128), 0)
        edge = (NCH - 1) if reverse else 0
        shift = (NCH - 1) if reverse else 1
        carries = []
        for k in range(nbk):
            er, ei, pr, pi_ = ends[k]
            sr, si = zero, zero
            for _ in range(NCH - 1):
                tr = pr * sr - pi_ * si + er
                ti = pr * si + pi_ * sr + ei
                sr = jnp.where(row == edge, 0.0, pltpu.roll(tr, shift, 0))
                si = jnp.where(row == edge, 0.0, pltpu.roll(ti, shift, 0))
            carries.append((sr, si))

        def fix(i, pw):
            out = []
            for k in range(nbk):
                pr, pi_ = pw[k]
                sr, si = carries[k]
                xrs[k][at(i), :] = xrs[k][at(i), :] + (pr * sr - pi_ * si)
                xis[k][at(i), :] = xis[k][at(i), :] + (pr * si + pi_ * sr)
                out.append((ars[k] * pr - ais[k] * pi_, ars[k] * pi_ + ais[k] * pr))
            return tuple(out)

        lax.fori_loop(0, ch, fix, tuple((ars[k], ais[k]) for k in range(nbk)))

    st = pl.BlockSpec((nbk, T, 128), lambda i: (i, 0, 0))
    av = pl.BlockSpec((nbk, 1, 128), lambda i: (i, 0, 0))
    return _pc(body, name=f"scan_{'bwd' if reverse else 'fwd'}_l{layer}", grid=(NLB // nbk,),
               in_specs=[st, st, av, av], out_specs=[st, st], out_shape=[S((NLB, T, 128), F32)] * 2,
               compiler_params=_cp(1))(br, bi, a_re, a_im)


def _ssm_out(xr, xi, P, cre, cim, dvec, wglu, bglu, layer):
    T = P.shape[1]
    tm = TM

    def body(xr_ref, xi_ref, u_ref, cr_ref, ci_ref, d_ref, w_ref, bg_ref, s_ref, y_ref, z_ref):
        ys = []
        for s in range(4):
            xrc = jnp.concatenate([xr_ref[4 * s + q] for q in range(4)], axis=1)
            xic = jnp.concatenate([xi_ref[4 * s + q] for q in range(4)], axis=1)
            ys.append(_dot3(_dot, xrc, cr_ref[s]) - _dot3(_dot, xic, ci_ref[s]))
        y = jnp.concatenate(ys, axis=1) + d_ref[...] * u_ref[...]
        yg = _gelu(y)
        ygb = yg.astype(BF16)
        z = bg_ref[...] + sum(_dot(ygb[:, j * 128:(j + 1) * 128], w_ref[j]) for j in range(NSH))
        y_ref[...] = y
        z_ref[...] = z
        s_ref[...] = yg * jax.nn.sigmoid(z)

    st = pl.BlockSpec((NLB, tm, 128), lambda i: (0, i, 0))
    cw = pl.BlockSpec((4, 512, 128), lambda i: (0, 0, 0))
    half = pl.BlockSpec((tm, DSS), lambda i: (i, 0))
    return _pc(body, name=f"ssm_out_l{layer}", grid=(T // tm,),
               in_specs=[st, st, pl.BlockSpec((None, tm, DSS), lambda i: (3, i, 0)), cw, cw, _gain_spec(DSS, layer),
                         pl.BlockSpec((NSH, None, 128, DSS), lambda i: (0, layer, 0, 0)), _gain_spec(DSS, layer)],
               out_specs=[half, half, half], out_shape=[S((T, DSS), F32)] * 3,
               compiler_params=_cp(1))(xr, xi, P, cre, cim, dvec, wglu, bglu)


def _ssm_out_bwd(dssm, y, z, xr, xi, P, cre, cim, dvec, wglu, layer):
    T = P.shape[1]
    tm = TMB

    def body(ds_ref, y_ref, z_ref, xr_ref, xi_ref, u_ref, cr_ref, ci_ref, d_ref, w_ref,
             gr_ref, gi_ref, du_ref, dz_ref, yg_ref, dbg_ref, dd_ref, dcr_ref, dci_ref):
        i = pl.program_id(0)

        @pl.when(i == 0)
        def _():
            dbg_ref[...] = jnp.zeros_like(dbg_ref)
            dd_ref[...] = jnp.zeros_like(dd_ref)
            dcr_ref[...] = jnp.zeros_like(dcr_ref)
            dci_ref[...] = jnp.zeros_like(dci_ref)

        yv = y_ref[...]
        yg = _gelu(yv)
        sg = jax.nn.sigmoid(z_ref[...])
        ds = ds_ref[...]
        dz = ds * yg * sg * (1.0 - sg)
        dzb = dz.astype(BF16)
        dz_ref[...] = dzb
        yg_ref[...] = yg.astype(BF16)
        dbg_ref[...] += jnp.sum(dz, axis=0, keepdims=True)
        dyg = ds * sg + jnp.concatenate([_dot_nt(dzb, w_ref[j]) for j in range(NSH)], axis=1)
        dy = dyg * _gelu_grad(yv)
        u = u_ref[...]
        dd_ref[...] += jnp.sum(dy * u, axis=0, keepdims=True)
        du_ref[...] = dy * d_ref[...]
        for s in range(4):
            dyc = dy[:, s * 128:(s + 1) * 128]
            g_r = _dot3(_dot_nt, dyc, cr_ref[s])
            g_i = -_dot3(_dot_nt, dyc, ci_ref[s])
            for q in range(4):
                gr_ref[4 * s + q] = g_r[:, q * 128:(q + 1) * 128]
                gi_ref[4 * s + q] = g_i[:, q * 128:(q + 1) * 128]
            xrc = jnp.concatenate([xr_ref[4 * s + q] for q in range(4)], axis=1)
            xic = jnp.concatenate([xi_ref[4 * s + q] for q in range(4)], axis=1)
            dcr_ref[s] += _dot3(_dot_tn, xrc, dyc)
            dci_ref[s] -= _dot3(_dot_tn, xic, dyc)

    st = pl.BlockSpec((NLB, tm, 128), lambda i: (0, i, 0))
    cw = pl.BlockSpec((4, 512, 128), lambda i: (0, 0, 0))
    half = pl.BlockSpec((tm, DSS), lambda i: (i, 0))
    return _pc(body, name=f"ssm_out_bwd_l{layer}", grid=(T // tm,),
               in_specs=[half, half, half, st, st, pl.BlockSpec((None, tm, DSS), lambda i: (3, i, 0)), cw, cw,
                         _gain_spec(DSS, layer), pl.BlockSpec((NSH, None, 128, DSS), lambda i: (0, layer, 0, 0))],
               out_specs=[st, st, half, half, half, _row_acc_spec(DSS), _row_acc_spec(DSS), cw, cw],
               out_shape=[S((NLB, T, 128), F32)] * 2 + [S((T, DSS), F32), S((T, DSS), BF16), S((T, DSS), BF16),
                                                        S((1, DSS), F32), S((1, DSS), F32),
                                                        S((4, 512, 128), F32), S((4, 512, 128), F32)],
               compiler_params=_cp(1))(dssm, y, z, xr, xi, P, cre, cim, dvec, wglu)


def _ssm_da(gr, gi, xr, xi, layer):
    T = gr.shape[1]
    tb = 1024 if T % 1024 == 0 else T

    def body(gr_ref, gi_ref, xr_ref, xi_ref, dr_ref, di_ref, lr, li):
        t = pl.program_id(1)

        @pl.when(t == 0)
        def _():
            dr_ref[...] = jnp.zeros_like(dr_ref)
            di_ref[...] = jnp.zeros_like(di_ref)
            lr[...] = jnp.zeros_like(lr)
            li[...] = jnp.zeros_like(li)

        g_r, g_i, x_r, x_i = gr_ref[...], gi_ref[...], xr_ref[...], xi_ref[...]
        pr = pltpu.roll(x_r, 1, 0)
        pi_ = pltpu.roll(x_i, 1, 0)
        g0r, g0i = g_r[0:1, :], g_i[0:1, :]
        fr = lr[7:8, :] - x_r[tb - 1:tb, :]
        fi = li[7:8, :] - x_i[tb - 1:tb, :]
        dr_ref[...] += jnp.sum(g_r * pr + g_i * pi_, axis=0, keepdims=True) + g0r * fr + g0i * fi
        di_ref[...] += jnp.sum(g_i * pr - g_r * pi_, axis=0, keepdims=True) + g0i * fr - g0r * fi
        lr[...] = x_r[tb - 8:tb, :]
        li[...] = x_i[tb - 8:tb, :]

    st = pl.BlockSpec((None, tb, 128), lambda k, t: (k, t, 0))
    out = pl.BlockSpec((None, 1, 128), lambda k, t: (k, 0, 0))
    return _pc(body, name=f"ssm_da_l{layer}", grid=(NLB, T // tb), in_specs=[st] * 4, out_specs=[out, out],
               out_shape=[S((NLB, 1, 128), F32)] * 2, scratch_shapes=[pltpu.VMEM((8, 128), F32)] * 2,
               compiler_params=_cp(2))(gr, gi, xr, xi)


def _ssm_in_bwd(gr, gi, P, bre, bim, du_direct, layer):
    T = P.shape[1]
    tm = TM

    def body(gr_ref, gi_ref, u_ref, br_ref, bi_ref, dd_ref, du_ref, dbr_ref, dbi_ref):
        i = pl.program_id(0)

        @pl.when(i == 0)
        def _():
            dbr_ref[...] = jnp.zeros_like(dbr_ref)
            dbi_ref[...] = jnp.zeros_like(dbi_ref)

        dus = []
        for s in range(4):
            grc = jnp.concatenate([gr_ref[4 * s + q] for q in range(4)], axis=1)
            gic = jnp.concatenate([gi_ref[4 * s + q] for q in range(4)], axis=1)
            uc = u_ref[:, s * 128:(s + 1) * 128]
            dus.append(_dot3(_dot_nt, grc, br_ref[s]) + _dot3(_dot_nt, gic, bi_ref[s]))
            dbr_ref[s] += _dot3(_dot_tn, uc, grc)
            dbi_ref[s] += _dot3(_dot_tn, uc, gic)
        du_ref[...] = jnp.concatenate(dus, axis=1) + dd_ref[...]

    whole = pl.BlockSpec((4, 128, 512), lambda i: (0, 0, 0))
    st = pl.BlockSpec((NLB, tm, 128), lambda i: (0, i, 0))
    half = pl.BlockSpec((tm, DSS), lambda i: (i, 0))
    return _pc(body, name=f"ssm_in_bwd_l{layer}", grid=(T // tm,),
               in_specs=[st, st, pl.BlockSpec((None, tm, DSS), lambda i: (3, i, 0)), whole, whole, half],
               out_specs=[half, whole, whole],
               out_shape=[S((T, DSS), F32), S((4, 128, 512), F32), S((4, 128, 512), F32)],
               compiler_params=_cp(1))(gr, gi, P, bre, bim, du_direct)


def _mix_out(outs, lses, ssm, h, attn_g, ssm_g, post_g, wout, layer):
    T = h.shape[0]
    tm = TM

    def body(o1, o2, o3, l1, l2, l3, s_ref, h_ref, ag_ref, sg_ref, pg_ref, w_ref, ho_ref, at_ref, ls_ref, mx_ref, mo_ref):
        la, lb, lc = l1[...], l2[...], l3[...]
        m = jnp.maximum(jnp.maximum(la, lb), lc)
        wa, wb, wc = jnp.exp(la - m), jnp.exp(lb - m), jnp.exp(lc - m)
        zs = wa + wb + wc
        attn = (wa * o1[...] + wb * o2[...] + wc * o3[...]) / zs
        at_ref[...] = attn
        ls_ref[...] = m + jnp.log(zs)
        mixed = jnp.concatenate([_rms_fwd(attn, ag_ref[...]), _rms_fwd(s_ref[...], sg_ref[...])], axis=1).astype(BF16)
        mx_ref[...] = mixed
        mo = sum(_dot(mixed[:, j * 256:(j + 1) * 256], w_ref[j]) for j in range(NSH))
        mo_ref[...] = mo
        ho_ref[...] = h_ref[...] + _rms_fwd(mo, pg_ref[...])

    row = pl.BlockSpec((tm, D), lambda i: (i, 0))
    half = pl.BlockSpec((tm, DA), lambda i: (i, 0))
    return _pc(body, name=f"mix_out_l{layer}", grid=(T // tm,),
               in_specs=[half] * 7 + [row, _gain_spec(DA, layer), _gain_spec(DSS, layer), _gain_spec(D, layer),
                                      pl.BlockSpec((NSH, None, 256, D), lambda i: (0, layer, 0, 0))],
               out_specs=[row, half, half, row, row],
               out_shape=[S((T, D), F32), S((T, DA), F32), S((T, DA), F32), S((T, D), BF16), S((T, D), F32)],
               compiler_params=_cp(1))(*outs, *lses, ssm, h, attn_g, ssm_g, post_g, wout)


def _mix_out_bwd(dout, mo, attn, ssm, attn_g, ssm_g, post_g, wout, layer):
    T = dout.shape[0]
    tm = TMB
    head_sum = jnp.asarray(np.kron(np.eye(NH, dtype=np.float32), np.ones((HD, HD), np.float32)), BF16)

    def body(do_ref, mo_ref, at_ref, s_ref, ag_ref, sg_ref, pg_ref, w_ref, e_ref,
             da_ref, ds_ref, dl_ref, dmo_ref, dpg_ref, dag_ref, dsg_ref):
        i = pl.program_id(0)

        @pl.when(i == 0)
        def _():
            dpg_ref[...] = jnp.zeros_like(dpg_ref)
            dag_ref[...] = jnp.zeros_like(dag_ref)
            dsg_ref[...] = jnp.zeros_like(dsg_ref)

        dmo, dpg = _rms_bwd(do_ref[...], mo_ref[...], pg_ref[...])
        dpg_ref[...] += dpg
        dmob = dmo.astype(BF16)
        dmo_ref[...] = dmob
        dmix = jnp.concatenate([_dot_nt(dmob, w_ref[j]) for j in range(NSH)], axis=1)
        attn = at_ref[...]
        dat, dag = _rms_bwd(dmix[:, :DA], attn, ag_ref[...])
        dss, dsg = _rms_bwd(dmix[:, DA:], s_ref[...], sg_ref[...])
        dag_ref[...] += dag
        dsg_ref[...] += dsg
        da_ref[...] = dat
        ds_ref[...] = dss
        prod = dat * attn
        p1 = prod.astype(BF16)
        r1 = prod - p1.astype(F32)
        p2 = r1.astype(BF16)
        p3 = (r1 - p2.astype(F32)).astype(BF16)
        e = e_ref[...]
        dl_ref[...] = _dot(p1, e) + _dot(p2, e) + _dot(p3, e)

    row = pl.BlockSpec((tm, D), lambda i: (i, 0))
    half = pl.BlockSpec((tm, DA), lambda i: (i, 0))
    return _pc(body, name=f"mix_out_bwd_l{layer}", grid=(T // tm,),
               in_specs=[row, row, half, half, _gain_spec(DA, layer), _gain_spec(DSS, layer), _gain_spec(D, layer),
                         pl.BlockSpec((NSH, None, 256, D), lambda i: (0, layer, 0, 0)),
                         pl.BlockSpec((DA, DA), lambda i: (0, 0))],
               out_specs=[half, half, half, row, _row_acc_spec(D), _row_acc_spec(DA), _row_acc_spec(DSS)],
               out_shape=[S((T, DA), F32)] * 3 + [S((T, D), BF16), S((1, D), F32), S((1, DA), F32), S((1, DSS), F32)],
               compiler_params=_cp(1))(dout, mo, attn, ssm, attn_g, ssm_g, post_g, wout, head_sum)


def _ple_fwd(h, p3, wup, wgate, post_g, layer):
    T = h.shape[0]
    tm = TM

    def body(h_ref, p_ref, wu_ref, wg_ref, g_ref, ho_ref, e_ref, gt_ref):
        hv = h_ref[...]
        hb = hv.astype(BF16)
        pb = p_ref[...].astype(BF16)
        gte = sum(_dot(hb[:, j * 256:(j + 1) * 256], wg_ref[j]) for j in range(NSH))
        e = jnp.concatenate([_dot(pb, wu_ref[j]) for j in range(NSH)], axis=1)
        e_ref[...] = e
        gt_ref[...] = gte
        ho_ref[...] = hv + _rms_fwd(e * jax.nn.sigmoid(gte), g_ref[...])

    row = pl.BlockSpec((tm, D), lambda i: (i, 0))
    return _pc(body, name=f"ple_fwd_l{layer}", grid=(T // tm,),
               in_specs=[row, pl.BlockSpec((None, tm, PLE), lambda i: (layer, i, 0)),
                         pl.BlockSpec((NSH, None, PLE, 256), lambda i: (0, layer, 0, 0)),
                         pl.BlockSpec((NSH, None, 256, D), lambda i: (0, layer, 0, 0)), _gain_spec(D, layer)],
               out_specs=[row, row, row], out_shape=[S((T, D), F32)] * 3,
               compiler_params=_cp(1))(h, p3, wup, wgate, post_g)


def _ple_bwd(dout, e, gte, wgate, post_g, layer):
    T = dout.shape[0]
    tm = TMB

    def body(do_ref, e_ref, gt_ref, wg_ref, g_ref, dh_ref, de_ref, dgt_ref, dg_ref):
        i = pl.program_id(0)

        @pl.when(i == 0)
        def _():
            dg_ref[...] = jnp.zeros_like(dg_ref)

        ev = e_ref[...]
        sg = jax.nn.sigmoid(gt_ref[...])
        do = do_ref[...]
        dple, dg = _rms_bwd(do, ev * sg, g_ref[...])
        dg_ref[...] += dg
        de = (dple * sg).astype(BF16)
        for j in range(NSH):
            de_ref[j] = de[:, j * 256:(j + 1) * 256]
        dgb = (dple * ev * sg * (1.0 - sg)).astype(BF16)
        dgt_ref[...] = dgb
        dh_ref[...] = do + jnp.concatenate([_dot_nt(dgb, wg_ref[j]) for j in range(NSH)], axis=1)

    row = pl.BlockSpec((tm, D), lambda i: (i, 0))
    return _pc(body, name=f"ple_bwd_l{layer}", grid=(T // tm,),
               in_specs=[row, row, row, pl.BlockSpec((NSH, None, 256, D), lambda i: (0, layer, 0, 0)), _gain_spec(D, layer)],
               out_specs=[row, pl.BlockSpec((NSH, tm, 256), lambda i: (0, i, 0)), row, _row_acc_spec(D)],
               out_shape=[S((T, D), F32), S((NSH, T, 256), BF16), S((T, D), BF16), S((1, D), F32)],
               compiler_params=_cp(1))(dout, e, gte, wgate, post_g)


def _loss_head(h, target):
    T = h.shape[0]
    tm = TM

    def body(h_ref, t_ref, dy_ref, l_ref):
        i = pl.program_id(0)

        @pl.when(i == 0)
        def _():
            l_ref[...] = jnp.zeros_like(l_ref)

        err = h_ref[...] - t_ref[...]
        dy_ref[...] = err * (1.0 / D)
        l_ref[...] += jnp.broadcast_to((0.5 / D) * jnp.sum(err * err), (1, 128))

    row = pl.BlockSpec((tm, D), lambda i: (i, 0))
    return _pc(body, name="loss_head", grid=(T // tm,), in_specs=[row, row],
               out_specs=[row, pl.BlockSpec((1, 128), lambda i: (0, 0))],
               out_shape=[S((T, D), F32), S((1, 128), F32)], compiler_params=_cp(1))(h, target)


def _local_step(x, p3, pos_col, target, W, Sm):
    L = p3.shape[0]
    g3 = {n: Sm[n].reshape(L, 1, -1) for n in ("ffn1_pre_g", "ffn1_post_g", "mix_pre_g", "attn_norm_g", "ssm_norm_g",
                                                "mix_post_g", "ffn2_pre_g", "ffn2_post_g", "ple_post_g", "ssm_b_glu", "ssm_d")}
    rot = _rot_tables(pos_col)
    prep_names = ("ssm_lam_re", "ssm_lam_im", "ssm_log_dt", "ssm_b_re", "ssm_b_im", "ssm_c_re", "ssm_c_im")

    saved = []
    h = x
    for l in range(L):
        sv = {"h0": h}
        h, sv["a1"], sv["b1"], sv["f1"], sv["xn1"] = _ffn_fwd(
            h, g3["ffn1_pre_g"], g3["ffn1_post_g"], W["ffn1_w_gate"], W["ffn1_w_up"], W["ffn1_w_down"], l, "1")
        sv["h1"] = h
        P, sv["ain"] = _mix_proj(h, g3["mix_pre_g"], W["w_in"], rot, l)
        sv["P"] = P
        ol = [_attn_fwd(P, d, l) for d in PATTERN_DILATIONS]
        prep, sv["prep_vjp"] = jax.vjp(_ssm_prep, *[Sm[n][l] for n in prep_names])
        a_re, a_im, bre, bim, cre, cim = prep
        sv["prep"] = prep
        bur, bui = _ssm_in(P, bre, bim, l)
        xr, xi = _scan(bur, bui, a_re, a_im, False, l)
        sv["xr"], sv["xi"] = xr, xi
        ssm, sv["y"], sv["z"] = _ssm_out(xr, xi, P, cre, cim, g3["ssm_d"], W["ssm_w_glu"], g3["ssm_b_glu"], l)
        sv["ssm"] = ssm
        h, sv["attn"], sv["lse"], sv["mixed"], sv["mo"] = _mix_out(
            [o for o, _ in ol], [s for _, s in ol], ssm, h, g3["attn_norm_g"], g3["ssm_norm_g"], g3["mix_post_g"],
            W["w_out"], l)
        sv["h2"] = h
        h, sv["a2"], sv["b2"], sv["f2"], sv["xn2"] = _ffn_fwd(
            h, g3["ffn2_pre_g"], g3["ffn2_post_g"], W["ffn2_w_gate"], W["ffn2_w_up"], W["ffn2_w_down"], l, "2")
        sv["h3"] = h
        h, sv["e"], sv["gte"] = _ple_fwd(h, p3, W["ple_w_up"], W["ple_w_gate"], g3["ple_post_g"], l)
        saved.append(sv)

    dh, loss = _loss_head(h, target)

    G = {n: lax.empty((NSH, L, r, c), BF16) for n, r, c in BIG}
    sg = {n: [None] * L for n in SMALL}
    whole = lambda j: (0, 0)
    shard = lambda j: (j, 0)
    kcol = lambda j: (0, j)
    for l in reversed(range(L)):
        sv = saved[l]
        dh, de, dgte, sg["ple_post_g"][l] = _ple_bwd(dh, sv["e"], sv["gte"], W["ple_w_gate"], g3["ple_post_g"], l)
        G["ple_w_up"] = _dw(p3[l][None], de, G["ple_w_up"], l, PLE, 256, whole, shard, f"dw_ple_up_l{l}")
        G["ple_w_gate"] = _dw(sv["h3"][None], dgte[None], G["ple_w_gate"], l, 256, D, kcol, whole, f"dw_ple_gate_l{l}")
        dh, df, da, db, hh, sg["ffn2_pre_g"][l], sg["ffn2_post_g"][l] = _ffn_bwd(
            dh, sv["h2"], sv["f2"], sv["a2"], sv["b2"], g3["ffn2_pre_g"], g3["ffn2_post_g"],
            W["ffn2_w_gate"], W["ffn2_w_up"], W["ffn2_w_down"], l, "2")
        G["ffn2_w_gate"] = _dw(sv["xn2"][None], da, G["ffn2_w_gate"], l, D, DFS, whole, shard, f"dw_ffn2_gate_l{l}")
        G["ffn2_w_up"] = _dw(sv["xn2"][None], db, G["ffn2_w_up"], l, D, DFS, whole, shard, f"dw_ffn2_up_l{l}")
        G["ffn2_w_down"] = _dw(hh, df[None], G["ffn2_w_down"], l, DFS, D, shard, whole, f"dw_ffn2_down_l{l}")
        a_re, a_im, bre, bim, cre, cim = sv["prep"]
        dattn, dssm, delta, dmo, sg["mix_post_g"][l], sg["attn_norm_g"][l], sg["ssm_norm_g"][l] = _mix_out_bwd(
            dh, sv["mo"], sv["attn"], sv["ssm"], g3["attn_norm_g"], g3["ssm_norm_g"], g3["mix_post_g"], W["w_out"], l)
        G["w_out"] = _dw(sv["mixed"][None], dmo[None], G["w_out"], l, 256, D, kcol, whole, f"dw_out_l{l}")
        gnr, gni, du_direct, dz, yg, sg["ssm_b_glu"][l], dd, dcre, dcim = _ssm_out_bwd(
            dssm, sv["y"], sv["z"], sv["xr"], sv["xi"], sv["P"], cre, cim, g3["ssm_d"], W["ssm_w_glu"], l)
        sg["ssm_d"][l] = dd.reshape(Sm["ssm_d"].shape[1:])
        G["ssm_w_glu"] = _dw(yg[None], dz[None], G["ssm_w_glu"], l, 128, DSS, kcol, whole, f"dw_glu_l{l}")
        gr, gi = _scan(gnr, gni, a_re, a_im, True, l)
        dar, dai = _ssm_da(gr, gi, sv["xr"], sv["xi"], l)
        du, dbre, dbim = _ssm_in_bwd(gr, gi, sv["P"], bre, bim, du_direct, l)
        for n, g in zip(prep_names, sv["prep_vjp"]((dar, dai, dbre, dbim, dcre, dcim))):
            sg[n][l] = g
        acc = None
        for d in PATTERN_DILATIONS:
            acc = _attn_bwd(sv["P"], dattn, sv["lse"], delta, acc, d, l)
        dh, dP, sg["mix_pre_g"][l] = _mix_proj_bwd(acc[0], acc[1], acc[2], du, dh, sv["h1"], g3["mix_pre_g"],
                                                   W["w_in"], rot, l)
        G["w_in"] = _dw(sv["ain"][None], dP, G["w_in"], l, D, DA, whole, shard, f"dw_in_l{l}")
        dh, df, da, db, hh, sg["ffn1_pre_g"][l], sg["ffn1_post_g"][l] = _ffn_bwd(
            dh, sv["h0"], sv["f1"], sv["a1"], sv["b1"], g3["ffn1_pre_g"], g3["ffn1_post_g"],
            W["ffn1_w_gate"], W["ffn1_w_up"], W["ffn1_w_down"], l, "1")
        G["ffn1_w_gate"] = _dw(sv["xn1"][None], da, G["ffn1_w_gate"], l, D, DFS, whole, shard, f"dw_ffn1_gate_l{l}")
        G["ffn1_w_up"] = _dw(sv["xn1"][None], db, G["ffn1_w_up"], l, D, DFS, whole, shard, f"dw_ffn1_up_l{l}")
        G["ffn1_w_down"] = _dw(hh, df[None], G["ffn1_w_down"], l, DFS, D, shard, whole, f"dw_ffn1_down_l{l}")

    small = {n: jnp.stack([g.reshape(Sm[n].shape[1:]) for g in sg[n]]) for n in SMALL}
    return loss, dh, G, small


HBM_SPEC = pl.BlockSpec(memory_space=pltpu.HBM)


def _place():
    x, y, c = lax.axis_index("x"), lax.axis_index("y"), lax.axis_index("c")
    chips = [(1 - x, y), (x, 1 - y), (1 - x, 1 - y)]
    return x, y, c, chips


def _comm_params():
    return pltpu.CompilerParams(vmem_limit_bytes=VMEM_LIMIT)


def _gather_weights(ws):
    n = len(ws)

    def body(*refs):
        ins, outs = refs[:n], refs[n:2 * n]
        s_ici, r_ici, s_d2d, r_d2d, s_loc = refs[2 * n:]
        x, y, c, chips = _place()
        me = 2 * x + y

        def half(ref, t, hc):
            r2 = ws[t].shape[1] // 2
            return ref.at[:, pl.ds(hc * r2, r2), :]

        def ici(t, k, src_chip, to):
            j = 2 * src_chip[0] + src_chip[1]
            src = half(ins[t], t, c) if to is not None else half(outs[t].at[j], t, c)
            return pltpu.make_async_remote_copy(src_ref=src, dst_ref=half(outs[t].at[j], t, c),
                                                send_sem=s_ici.at[3 * t + k], recv_sem=r_ici.at[3 * t + k],
                                                device_id=to if to is not None else (x, y, c), device_id_type=MESH)

        def d2d(t, k, hc):
            j = 2 * chips[k][0] + chips[k][1]
            r = half(outs[t].at[j], t, hc)
            return pltpu.make_async_remote_copy(src_ref=r, dst_ref=r, send_sem=s_d2d.at[3 * t + k],
                                                recv_sem=r_d2d.at[3 * t + k], device_id=(x, y, 1 - c),
                                                device_id_type=MESH)

        own = [pltpu.make_async_copy(ins[t], outs[t].at[me], s_loc.at[t]) for t in range(n)]
        for cp in own:
            cp.start()
        sends = [ici(t, k, (x, y), (*chips[k], c)) for t in range(n) for k in range(3)]
        for cp in sends:
            cp.start()
        passed = []
        for t in range(n):
            for k in range(3):
                ici(t, k, chips[k], None).wait_recv()
                passed.append(d2d(t, k, c))
                passed[-1].start()
        for t in range(n):
            for k in range(3):
                d2d(t, k, 1 - c).wait_recv()
        for cp in sends + passed:
            cp.wait_send()
        for cp in own:
            cp.wait()

    return _pc(body, name="gather_weights", in_specs=[HBM_SPEC] * n, out_specs=[HBM_SPEC] * n,
               out_shape=[S((NSH,) + w.shape, w.dtype) for w in ws],
               scratch_shapes=[pltpu.SemaphoreType.DMA((3 * n,))] * 4 + [pltpu.SemaphoreType.DMA((n,))],
               compiler_params=_comm_params())(*ws)


def _swap_halves(gs):
    n = len(gs)

    def body(*refs):
        ins, outs = refs[:n], refs[n:2 * n]
        s_sem, r_sem = refs[2 * n:]
        x, y, c, _ = _place()
        cps = []
        for t in range(n):
            r2 = gs[t].shape[2] // 2
            cps.append(pltpu.make_async_remote_copy(
                src_ref=ins[t].at[:, :, pl.ds((1 - c) * r2, r2), :], dst_ref=outs[t], send_sem=s_sem.at[t],
                recv_sem=r_sem.at[t], device_id=(x, y, 1 - c), device_id_type=MESH))
            cps[-1].start()
        for cp in cps:
            cp.wait_recv()
        for cp in cps:
            cp.wait_send()

    return _pc(body, name="grad_swap_halves", in_specs=[HBM_SPEC] * n, out_specs=[HBM_SPEC] * n,
               out_shape=[S(g.shape[:2] + (g.shape[2] // 2, g.shape[3]), g.dtype) for g in gs],
               scratch_shapes=[pltpu.SemaphoreType.DMA((n,))] * 2, compiler_params=_comm_params())(*gs)


def _add_half(g, landed, c_arr, name):
    _, L, r2, cols = landed.shape

    def body(c_ref, g_ref, l_ref, o_ref):
        o_ref[...] = (g_ref[...].astype(F32) + l_ref[...].astype(F32)).astype(BF16)

    gs = pltpu.PrefetchScalarGridSpec(
        num_scalar_prefetch=1, grid=(NSH, L),
        in_specs=[pl.BlockSpec((None, None, r2, cols), lambda j, l, c: (j, l, c[0], 0)),
                  pl.BlockSpec((None, None, r2, cols), lambda j, l, c: (j, l, 0, 0))],
        out_specs=pl.BlockSpec((None, None, r2, cols), lambda j, l, c: (j, l, 0, 0)))
    return _pc(body, name=name, grid_spec=gs, out_shape=S(landed.shape, BF16), compiler_params=_cp(2))(c_arr, g, landed)


def _send_shards(ps):
    n = len(ps)

    def body(*refs):
        ins, outs = refs[:n], refs[n:2 * n]
        s_sem, r_sem = refs[2 * n:]
        x, y, c, chips = _place()
        cps = []
        for t in range(n):
            for k in range(3):
                cps.append(pltpu.make_async_remote_copy(
                    src_ref=ins[t].at[2 * chips[k][0] + chips[k][1]], dst_ref=outs[t].at[k],
                    send_sem=s_sem.at[3 * t + k], recv_sem=r_sem.at[3 * t + k], device_id=(*chips[k], c),
                    device_id_type=MESH))
                cps[-1].start()
        for cp in cps:
            cp.wait_recv()
        for cp in cps:
            cp.wait_send()

    return _pc(body, name="grad_send_shards", in_specs=[HBM_SPEC] * n, out_specs=[HBM_SPEC] * n,
               out_shape=[S((3,) + p.shape[1:], p.dtype) for p in ps],
               scratch_shapes=[pltpu.SemaphoreType.DMA((3 * n,))] * 2, compiler_params=_comm_params())(*ps)


def _sum_shards(part, landed, me_arr, name):
    _, L, r2, cols = landed.shape

    def body(me_ref, p_ref, l_ref, o_ref):
        o_ref[...] = ((p_ref[...].astype(F32) + l_ref[0].astype(F32)) + l_ref[1].astype(F32)) + l_ref[2].astype(F32)

    gs = pltpu.PrefetchScalarGridSpec(
        num_scalar_prefetch=1, grid=(L,),
        in_specs=[pl.BlockSpec((None, None, r2, cols), lambda l, me: (me[0], l, 0, 0)),
                  pl.BlockSpec((3, None, r2, cols), lambda l, me: (0, l, 0, 0))],
        out_specs=pl.BlockSpec((None, r2, cols), lambda l, me: (l, 0, 0)))
    return _pc(body, name=name, grid_spec=gs, out_shape=S((L, r2, cols), F32), compiler_params=_cp(1))(me_arr, part, landed)


def _share_halves(hs):
    n = len(hs)

    def body(*refs):
        ins, outs = refs[:n], refs[n:2 * n]
        s_sem, r_sem, s_loc = refs[2 * n:]
        x, y, c, _ = _place()
        cps, own = [], []
        for t in range(n):
            r2 = hs[t].shape[1]
            dst = outs[t].at[:, pl.ds(c * r2, r2), :]
            own.append(pltpu.make_async_copy(ins[t], dst, s_loc.at[t]))
            own[-1].start()
            cps.append(pltpu.make_async_remote_copy(src_ref=ins[t], dst_ref=dst, send_sem=s_sem.at[t],
                                                    recv_sem=r_sem.at[t], device_id=(x, y, 1 - c), device_id_type=MESH))
            cps[-1].start()
        for cp in cps:
            cp.wait_recv()
        for cp in cps:
            cp.wait_send()
        for cp in own:
            cp.wait()

    return _pc(body, name="grad_share_halves", in_specs=[HBM_SPEC] * n, out_specs=[HBM_SPEC] * n,
               out_shape=[S((h.shape[0], 2 * h.shape[1], h.shape[2]), h.dtype) for h in hs],
               scratch_shapes=[pltpu.SemaphoreType.DMA((n,))] * 3, compiler_params=_comm_params())(*hs)


def _gather_small(v):
    nr = v.shape[0]

    def body(v_ref, out_ref, send_sems, recv_sems, local_sem):
        x, y, c, chips = _place()
        me, sibling = (x, y, c), (x, y, 1 - c)

        def rows(px, py, pc):
            return out_ref.at[pl.ds((4 * px + 2 * py + pc) * nr, nr), :]

        def copy(k, block, to, src=None):
            return pltpu.make_async_remote_copy(src_ref=rows(*block) if src is None else src, dst_ref=rows(*block),
                                                send_sem=send_sems.at[k], recv_sem=recv_sems.at[k], device_id=to,
                                                device_id_type=MESH)

        mine = pltpu.make_async_copy(v_ref, rows(*me), local_sem)
        mine.start()
        first = [copy(0, me, sibling, src=v_ref)]
        first += [copy(1 + j, me, (*chip, c), src=v_ref) for j, chip in enumerate(chips)]
        for cp in first:
            cp.start()
        passed = [copy(4 + j, (*chip, c), sibling) for j, chip in enumerate(chips)]
        for j, chip in enumerate(chips):
            copy(1 + j, (*chip, c), me).wait_recv()
            passed[j].start()
        copy(0, sibling, me).wait_recv()
        for j, chip in enumerate(chips):
            copy(4 + j, (*chip, 1 - c), me).wait_recv()
        for cp in first + passed:
            cp.wait_send()
        mine.wait()

    vm = pl.BlockSpec(memory_space=pltpu.VMEM)
    return _pc(body, name="gather_small_grads", in_specs=[vm], out_specs=vm, out_shape=S((8 * nr, 128), F32),
               scratch_shapes=[pltpu.SemaphoreType.DMA((7,)), pltpu.SemaphoreType.DMA((7,)), pltpu.SemaphoreType.DMA],
               compiler_params=_comm_params())(v)


def _adamw_math(w, g, m, v):
    m2 = ADAM_B1 * m + (1.0 - ADAM_B1) * g
    v2 = ADAM_B2 * v + (1.0 - ADAM_B2) * (g * g)
    m_hat = m2 / (1.0 - ADAM_B1 ** ADAM_STEP)
    v_hat = v2 / (1.0 - ADAM_B2 ** ADAM_STEP)
    return -ADAM_LR * (m_hat / (jnp.sqrt(v_hat) + ADAM_EPS) + ADAM_WD * w), m2, v2


def _adamw(w, g, m, v, name):
    L, R, C = w.shape
    rb = R // 2 if R >= 512 else R

    def body(w_ref, g_ref, m_ref, v_ref, d_ref, m2_ref, v2_ref):
        d_ref[...], m2_ref[...], v2_ref[...] = _adamw_math(w_ref[...], g_ref[...], m_ref[...], v_ref[...])

    blk = pl.BlockSpec((None, rb, C), lambda l, r: (l, r, 0))
    return _pc(body, name=name, grid=(L, R // rb), in_specs=[blk] * 4, out_specs=[blk] * 3,
               out_shape=[S(w.shape, F32)] * 3, compiler_params=_cp(2))(w, g, m, v)


def _adamw_small(gathered, w, m, v):
    nr = w.shape[0]
    rb = nr // 5

    def body(a_ref, w_ref, m_ref, v_ref, g_ref, d_ref, m2_ref, v2_ref):
        g = a_ref[0]
        for k in range(1, 8):
            g = g + a_ref[k]
        g_ref[...] = g
        d_ref[...], m2_ref[...], v2_ref[...] = _adamw_math(w_ref[...], g, m_ref[...], v_ref[...])

    blk = pl.BlockSpec((rb, 128), lambda i: (i, 0))
    return _pc(body, name="adamw_small", grid=(nr // rb,), in_specs=[pl.BlockSpec((8, rb, 128), lambda i: (0, i, 0))] + [blk] * 3,
               out_specs=[blk] * 4, out_shape=[S((nr, 128), F32)] * 4, compiler_params=_cp(1))(gathered, w, m, v)


SMALL_ROWS = 4520


def _pack(arrs):
    flat = jnp.concatenate([a.reshape(-1) for a in arrs])
    return jnp.pad(flat, (0, SMALL_ROWS * 128 - flat.shape[0])).reshape(SMALL_ROWS, 128)


def _unpack(packed, like):
    flat = packed.reshape(-1)
    out, off = [], 0
    for a in like:
        out.append(flat[off:off + a.size].reshape(a.shape))
        off += a.size
    return out


def kernel(x, p, positions, ffn1_pre_g, ffn1_w_gate, ffn1_w_up, ffn1_w_down, ffn1_post_g, mix_pre_g, w_in, attn_norm_g, ssm_lam_re, ssm_lam_im, ssm_log_dt, ssm_b_re, ssm_b_im, ssm_c_re, ssm_c_im, ssm_d, ssm_w_glu, ssm_b_glu, ssm_norm_g, w_out, mix_post_g, ffn2_pre_g, ffn2_w_gate, ffn2_w_up, ffn2_w_down, ffn2_post_g, ple_w_up, ple_w_gate, ple_post_g, loss_target, m_ffn1_pre_g, m_ffn1_w_gate, m_ffn1_w_up, m_ffn1_w_down, m_ffn1_post_g, m_mix_pre_g, m_w_in, m_attn_norm_g, m_ssm_lam_re, m_ssm_lam_im, m_ssm_log_dt, m_ssm_b_re, m_ssm_b_im, m_ssm_c_re, m_ssm_c_im, m_ssm_d, m_ssm_w_glu, m_ssm_b_glu, m_ssm_norm_g, m_w_out, m_mix_post_g, m_ffn2_pre_g, m_ffn2_w_gate, m_ffn2_w_up, m_ffn2_w_down, m_ffn2_post_g, m_ple_w_up, m_ple_w_gate, m_ple_post_g, v_ffn1_pre_g, v_ffn1_w_gate, v_ffn1_w_up, v_ffn1_w_down, v_ffn1_post_g, v_mix_pre_g, v_w_in, v_attn_norm_g, v_ssm_lam_re, v_ssm_lam_im, v_ssm_log_dt, v_ssm_b_re, v_ssm_b_im, v_ssm_c_re, v_ssm_c_im, v_ssm_d, v_ssm_w_glu, v_ssm_b_glu, v_ssm_norm_g, v_w_out, v_mix_post_g, v_ffn2_pre_g, v_ffn2_w_gate, v_ffn2_w_up, v_ffn2_w_down, v_ffn2_post_g, v_ple_w_up, v_ple_w_gate, v_ple_post_g):
    a = dict(locals())
    T = x.shape[1]
    big_names = [n for n, _, _ in BIG]

    gathered = _gather_weights([a[n].astype(BF16) for n in big_names])
    W = dict(zip(big_names, gathered))
    Sm = {n: a[n] for n in SMALL}

    loss, gx, G, small = _local_step(x[0], p[:, 0], positions.reshape(T, 1).astype(F32), loss_target[0], W, Sm)

    c_arr = lax.axis_index("c").astype(jnp.int32).reshape(1)
    me_arr = (2 * lax.axis_index("x") + lax.axis_index("y")).astype(jnp.int32).reshape(1)
    gs = [G[n] for n in big_names]
    landed = _swap_halves(gs)
    parts = [_add_half(g, la, c_arr, f"grad_add_half_{n}") for g, la, n in zip(gs, landed, big_names)]
    landed = _send_shards(parts)
    halves = [_sum_shards(pt, la, me_arr, f"grad_sum_shards_{n}") for pt, la, n in zip(parts, landed, big_names)]
    grads = dict(zip(big_names, _share_halves(halves)))

    small_g = _gather_small(_pack([small[n] for n in SMALL])).reshape(8, SMALL_ROWS, 128)
    sg, sd, sm, sv = _adamw_small(small_g, _pack([a[n] for n in SMALL]), _pack([a["m_" + n] for n in SMALL]),
                                  _pack([a["v_" + n] for n in SMALL]))
    like = [a[n] for n in SMALL]
    res = {}
    for n, g_, d_, m_, v_ in zip(SMALL, _unpack(sg, like), _unpack(sd, like), _unpack(sm, like), _unpack(sv, like)):
        res[n] = (g_, d_, m_, v_)
    for n in big_names:
        d_, m_, v_ = _adamw(a[n], grads[n], a["m_" + n], a["v_" + n], f"adamw_{n}")
        res[n] = (grads[n], d_, m_, v_)

    total = lax.psum(loss[0, 0], ("x", "y", "c"))
    return (total, gx[None], *[res[n][0] for n in WEIGHTS], *[res[n][1] for n in WEIGHTS],
            *[res[n][2] for n in WEIGHTS], *[res[n][3] for n in WEIGHTS])
```

```python
import functools
import math

import numpy as np
import jax
import jax.numpy as jnp
from jax import lax
from jax.experimental import pallas as pl
from jax.experimental.pallas import tpu as pltpu

F32 = jnp.float32
BF16 = jnp.bfloat16
S = jax.ShapeDtypeStruct
MESH = pl.DeviceIdType.MESH

D = 1024
DA = 512
DSS = 512
HD = 64
NH = 8
BAND = 128
NSH = 4
DFS = 704
PLE = 256
EPS = 1e-6
ROPE_THETA = 500000.0
PATTERN_DILATIONS = (1, 4, 16)
NCH = 32
NLB = 16
ADAM_LR, ADAM_B1, ADAM_B2, ADAM_EPS, ADAM_WD, ADAM_STEP = 0.001, 0.9, 0.999, 1e-08, 0.01, 10

VMEM_LIMIT = 56 * 1024 * 1024
TM = 512
TMB = 256

BIG = (
    ("ffn1_w_gate", DFS, D), ("ffn1_w_up", DFS, D), ("ffn1_w_down", DFS, D),
    ("w_in", D, 512), ("ssm_w_glu", 128, 512), ("w_out", 256, D),
    ("ffn2_w_gate", DFS, D), ("ffn2_w_up", DFS, D), ("ffn2_w_down", DFS, D),
    ("ple_w_up", PLE, 256), ("ple_w_gate", 256, D),
)
TRANSPOSED = ("ffn1_w_gate", "ffn1_w_up", "ffn2_w_gate", "ffn2_w_up")
SMALL = ("ffn1_pre_g", "ffn1_post_g", "mix_pre_g", "attn_norm_g", "ssm_lam_re", "ssm_lam_im", "ssm_log_dt",
         "ssm_b_re", "ssm_b_im", "ssm_c_re", "ssm_c_im", "ssm_d", "ssm_b_glu", "ssm_norm_g", "mix_post_g",
         "ffn2_pre_g", "ffn2_post_g", "ple_post_g")
WEIGHTS = ("ffn1_pre_g", "ffn1_w_gate", "ffn1_w_up", "ffn1_w_down", "ffn1_post_g", "mix_pre_g", "w_in", "attn_norm_g",
           "ssm_lam_re", "ssm_lam_im", "ssm_log_dt", "ssm_b_re", "ssm_b_im", "ssm_c_re", "ssm_c_im", "ssm_d",
           "ssm_w_glu", "ssm_b_glu", "ssm_norm_g", "w_out", "mix_post_g", "ffn2_pre_g", "ffn2_w_gate", "ffn2_w_up",
           "ffn2_w_down", "ffn2_post_g", "ple_w_up", "ple_w_gate", "ple_post_g")


def _pc(body, **kw):
    return pl.pallas_call(body, **kw)


def _cp(n_grid):
    return pltpu.CompilerParams(dimension_semantics=("arbitrary",) * n_grid, vmem_limit_bytes=VMEM_LIMIT)


def _dot(a, b):
    return jnp.dot(a, b, preferred_element_type=F32)


def _dot_nt(a, b):
    return lax.dot_general(a, b, (((1,), (1,)), ((), ())), preferred_element_type=F32)


def _dot_tn(a, b):
    return lax.dot_general(a, b, (((0,), (0,)), ((), ())), preferred_element_type=F32)


def _split(a):
    hi = a.astype(BF16)
    return hi, (a - hi.astype(F32)).astype(BF16)


def _dot3(fn, a, b):
    ah, al = _split(a)
    bh, bl = _split(b)
    return fn(ah, bh) + fn(ah, bl) + fn(al, bh)


def _rms_fwd(x, g):
    r = lax.rsqrt(jnp.mean(x * x, axis=-1, keepdims=True) + EPS)
    return x * r * g


def _rms_bwd(dy, x, g):
    r = lax.rsqrt(jnp.mean(x * x, axis=-1, keepdims=True) + EPS)
    xr = x * r
    gd = dy * g
    dx = r * (gd - xr * jnp.mean(gd * xr, axis=-1, keepdims=True))
    dg = jnp.sum(dy * xr, axis=0, keepdims=True)
    return dx, dg


def _gelu(y):
    k = math.sqrt(2.0 / math.pi)
    return 0.5 * y * (1.0 + jnp.tanh(k * (y + 0.044715 * y * y * y)))


def _gelu_grad(y):
    k = math.sqrt(2.0 / math.pi)
    t = jnp.tanh(k * (y + 0.044715 * y * y * y))
    return 0.5 * (1.0 + t) + 0.5 * y * (1.0 - t * t) * k * (1.0 + 3 * 0.044715 * y * y)


def _gain_spec(n, layer):
    return pl.BlockSpec((None, 1, n), lambda *_: (layer, 0, 0))


def _row_acc_spec(n):
    return pl.BlockSpec((1, n), lambda *_: (0, 0))


def _rot_tables(pos_col):
    T = pos_col.shape[0]
    half = HD // 8
    inv = (ROPE_THETA ** (-np.arange(half, dtype=np.float32) * (2.0 / (2 * half)))).astype(np.float32)
    lane_freq = np.tile(np.concatenate([inv, inv, np.zeros(HD - 2 * half, np.float32)]), NH)[None, :]

    def body(p_ref, f_ref, c_ref, s1_ref, s2_ref):
        ang = p_ref[...] * f_ref[...]
        d = lax.broadcasted_iota(jnp.int32, ang.shape, 1) % HD
        cs = jnp.cos(ang)
        sn = jnp.sin(ang)
        c_ref[...] = jnp.where(d < 2 * half, cs, 1.0)
        s1_ref[...] = jnp.where(d < half, -sn, 0.0)
        s2_ref[...] = jnp.where((d >= half) & (d < 2 * half), sn, 0.0)

    tm = TM
    return _pc(body, name="rot_tables", grid=(T // tm,),
               in_specs=[pl.BlockSpec((tm, 1), lambda i: (i, 0)), pl.BlockSpec((1, DA), lambda i: (0, 0))],
               out_specs=[pl.BlockSpec((tm, DA), lambda i: (i, 0))] * 3,
               out_shape=[S((T, DA), F32)] * 3, compiler_params=_cp(1))(pos_col, jnp.asarray(lane_freq))


def _rot_fwd(t, c, s1, s2):
    return t * c + pltpu.roll(t, DA - 8, 1) * s1 + pltpu.roll(t, 8, 1) * s2


def _rot_bwd(g, c, s1, s2):
    return g * c + pltpu.roll(g * s1, 8, 1) + pltpu.roll(g * s2, DA - 8, 1)


def _ffn_fwd(h, pre_g, post_g, wg, wu, wd, layer, tag):
    T = h.shape[0]
    tm = TM
    nt = T // tm

    def body(h_ref, pg_ref, qg_ref, wg_ref, wu_ref, wd_ref, ho_ref, a_ref, b_ref, f_ref, xn_ref, xs, facc):
        j = pl.program_id(1)

        @pl.when(j == 0)
        def _():
            xb = _rms_fwd(h_ref[...], pg_ref[...]).astype(BF16)
            xs[...] = xb
            xn_ref[...] = xb
            facc[...] = jnp.zeros_like(facc)

        xb = xs[...]
        ab = _dot_nt(xb, wg_ref[...]).astype(BF16)
        bb = _dot_nt(xb, wu_ref[...]).astype(BF16)
        a_ref[...] = ab
        b_ref[...] = bb
        a = ab.astype(F32)
        hh = (a * jax.nn.sigmoid(a) * bb.astype(F32)).astype(BF16)
        facc[...] += _dot(hh, wd_ref[...])

        @pl.when(j == NSH - 1)
        def _():
            f = facc[...]
            f_ref[...] = f
            ho_ref[...] = h_ref[...] + 0.5 * _rms_fwd(f, qg_ref[...])

    row = pl.BlockSpec((tm, D), lambda i, j: (i, 0))
    act = pl.BlockSpec((None, tm, DFS), lambda i, j: (j, i, 0))
    wrow = pl.BlockSpec((None, None, DFS, D), lambda i, j: (j, layer, 0, 0))
    return _pc(body, name=f"ffn_fwd_{tag}_l{layer}", grid=(nt, NSH),
               in_specs=[row, _gain_spec(D, layer), _gain_spec(D, layer), wrow, wrow, wrow],
               out_specs=[row, act, act, row, row],
               out_shape=[S((T, D), F32), S((NSH, T, DFS), BF16), S((NSH, T, DFS), BF16), S((T, D), F32), S((T, D), BF16)],
               scratch_shapes=[pltpu.VMEM((tm, D), BF16), pltpu.VMEM((tm, D), F32)],
               compiler_params=_cp(2))(h, pre_g, post_g, wg, wu, wd)


def _ffn_bwd(dout, h, f, a, b, pre_g, post_g, wg, wu, wd, layer, tag):
    T = h.shape[0]
    tm = TM
    nt = T // tm

    def body(do_ref, h_ref, f_ref, a_ref, b_ref, pg_ref, qg_ref, wg_ref, wu_ref, wd_ref,
             dh_ref, df_ref, da_ref, db_ref, hh_ref, dpg_ref, dqg_ref, dfs, dxn):
        i = pl.program_id(0)
        j = pl.program_id(1)

        @pl.when((i == 0) & (j == 0))
        def _():
            dpg_ref[...] = jnp.zeros_like(dpg_ref)
            dqg_ref[...] = jnp.zeros_like(dqg_ref)

        @pl.when(j == 0)
        def _():
            df, dq = _rms_bwd(0.5 * do_ref[...], f_ref[...], qg_ref[...])
            dqg_ref[...] += dq
            dfb = df.astype(BF16)
            dfs[...] = dfb
            df_ref[...] = dfb
            dxn[...] = jnp.zeros_like(dxn)

        dhh = _dot_nt(dfs[...], wd_ref[...])
        av = a_ref[...].astype(F32)
        bv = b_ref[...].astype(F32)
        sg = jax.nn.sigmoid(av)
        sa = av * sg
        hh_ref[...] = (sa * bv).astype(BF16)
        dab = (dhh * bv * (sg * (1.0 + av * (1.0 - sg)))).astype(BF16)
        dbb = (dhh * sa).astype(BF16)
        da_ref[...] = dab
        db_ref[...] = dbb
        dxn[...] += _dot(dab, wg_ref[...]) + _dot(dbb, wu_ref[...])

        @pl.when(j == NSH - 1)
        def _():
            dx, dp = _rms_bwd(dxn[...], h_ref[...], pg_ref[...])
            dpg_ref[...] += dp
            dh_ref[...] = do_ref[...] + dx

    row = pl.BlockSpec((tm, D), lambda i, j: (i, 0))
    act = pl.BlockSpec((None, tm, DFS), lambda i, j: (j, i, 0))
    wrow = pl.BlockSpec((None, None, DFS, D), lambda i, j: (j, layer, 0, 0))
    return _pc(body, name=f"ffn_bwd_{tag}_l{layer}", grid=(nt, NSH),
               in_specs=[row, row, row, act, act, _gain_spec(D, layer), _gain_spec(D, layer), wrow, wrow, wrow],
               out_specs=[row, row, act, act, act, _row_acc_spec(D), _row_acc_spec(D)],
               out_shape=[S((T, D), F32), S((T, D), BF16), S((NSH, T, DFS), BF16), S((NSH, T, DFS), BF16),
                          S((NSH, T, DFS), BF16), S((1, D), F32), S((1, D), F32)],
               scratch_shapes=[pltpu.VMEM((tm, D), BF16), pltpu.VMEM((tm, D), F32)],
               compiler_params=_cp(2))(dout, h, f, a, b, pre_g, post_g, wg, wu, wd)


def _dw(A, B, buf, layer, kb, nb, a_idx, b_idx, name):
    T = A.shape[1]
    tt = 2 * TM if T % (2 * TM) == 0 else TM
    nt = T // tt

    def body(a_ref, b_ref, buf_ref, o_ref, acc):
        t = pl.program_id(1)

        @pl.when(t == 0)
        def _():
            acc[...] = jnp.zeros_like(acc)

        acc[...] += _dot_tn(a_ref[...].astype(BF16), b_ref[...].astype(BF16))

        @pl.when(t == nt - 1)
        def _():
            o_ref[...] = acc[...].astype(o_ref.dtype)

    return _pc(body, name=name, grid=(NSH, nt),
               in_specs=[pl.BlockSpec((None, tt, kb), lambda j, t: (a_idx(j)[0], t, a_idx(j)[1])),
                         pl.BlockSpec((None, tt, nb), lambda j, t: (b_idx(j)[0], t, b_idx(j)[1])),
                         pl.BlockSpec(memory_space=pl.ANY)],
               out_specs=pl.BlockSpec((None, None, kb, nb), lambda j, t: (j, layer, 0, 0)),
               out_shape=S(buf.shape, buf.dtype), input_output_aliases={2: 0},
               scratch_shapes=[pltpu.VMEM((kb, nb), F32)], compiler_params=_cp(2))(A, B, buf)


def _mix_proj(h, pre_g, win, rot, layer):
    T = h.shape[0]
    tm = TM

    def body(h_ref, g_ref, w_ref, c_ref, s1_ref, s2_ref, p_ref, xn_ref, xs):
        j = pl.program_id(1)

        @pl.when(j == 0)
        def _():
            xb = _rms_fwd(h_ref[...], g_ref[...]).astype(BF16)
            xs[...] = xb
            xn_ref[...] = xb

        o = _dot(xs[...], w_ref[...])

        @pl.when(j < 2)
        def _():
            p_ref[...] = _rot_fwd(o, c_ref[...], s1_ref[...], s2_ref[...])

        @pl.when(j >= 2)
        def _():
            p_ref[...] = o

    row = pl.BlockSpec((tm, D), lambda i, j: (i, 0))
    half = pl.BlockSpec((tm, DA), lambda i, j: (i, 0))
    return _pc(body, name=f"mix_proj_l{layer}", grid=(T // tm, NSH),
               in_specs=[row, _gain_spec(D, layer), pl.BlockSpec((None, None, D, DA), lambda i, j: (j, layer, 0, 0)),
                         half, half, half],
               out_specs=[pl.BlockSpec((None, tm, DA), lambda i, j: (j, i, 0)), row],
               out_shape=[S((NSH, T, DA), F32), S((T, D), BF16)],
               scratch_shapes=[pltpu.VMEM((tm, D), BF16)], compiler_params=_cp(2))(h, pre_g, win, *rot)


def _mix_proj_bwd(dq, dk, dv, du, dh_up, h, pre_g, win, rot, layer):
    T = h.shape[0]
    tm = TM

    def body(dq_ref, dk_ref, dv_ref, du_ref, up_ref, h_ref, g_ref, w_ref, c_ref, s1_ref, s2_ref,
             dh_ref, dp_ref, dg_ref, dps, dxn):
        i = pl.program_id(0)
        j = pl.program_id(1)

        @pl.when((i == 0) & (j == 0))
        def _():
            dg_ref[...] = jnp.zeros_like(dg_ref)

        @pl.when(j == 0)
        def _():
            dxn[...] = jnp.zeros_like(dxn)
            dps[...] = _rot_bwd(dq_ref[...], c_ref[...], s1_ref[...], s2_ref[...]).astype(BF16)

        @pl.when(j == 1)
        def _():
            dps[...] = _rot_bwd(dk_ref[...], c_ref[...], s1_ref[...], s2_ref[...]).astype(BF16)

        @pl.when(j == 2)
        def _():
            dps[...] = dv_ref[...].astype(BF16)

        @pl.when(j == 3)
        def _():
            dps[...] = du_ref[...].astype(BF16)

        dpb = dps[...]
        dp_ref[...] = dpb
        dxn[...] += _dot_nt(dpb, w_ref[...])

        @pl.when(j == NSH - 1)
        def _():
            dx, dg = _rms_bwd(dxn[...], h_ref[...], g_ref[...])
            dg_ref[...] += dg
            dh_ref[...] = up_ref[...] + dx

    row = pl.BlockSpec((tm, D), lambda i, j: (i, 0))
    half = pl.BlockSpec((tm, DA), lambda i, j: (i, 0))
    return _pc(body, name=f"mix_proj_bwd_l{layer}", grid=(T // tm, NSH),
               in_specs=[half, half, half, half, row, row, _gain_spec(D, layer),
                         pl.BlockSpec((None, None, D, DA), lambda i, j: (j, layer, 0, 0)), half, half, half],
               out_specs=[row, pl.BlockSpec((None, tm, DA), lambda i, j: (j, i, 0)), _row_acc_spec(D)],
               out_shape=[S((T, D), F32), S((NSH, T, DA), BF16), S((1, D), F32)],
               scratch_shapes=[pltpu.VMEM((tm, DA), BF16), pltpu.VMEM((tm, D), F32)],
               compiler_params=_cp(2))(dq, dk, dv, du, dh_up, h, pre_g, win, *rot)


def _band_masks(b):
    qi = lax.broadcasted_iota(jnp.int32, (BAND, BAND), 0)
    kj = lax.broadcasted_iota(jnp.int32, (BAND, BAND), 1)
    return kj <= qi, (kj >= qi) & (b > 0)


def _attn_fwd(P, d, layer):
    T = P.shape[1]
    nb = T // d // BAND
    Pv = P.reshape(NSH, T // d, d * DA)
    scale = HD ** -0.5

    def body(q_ref, kp_ref, kc_ref, vp_ref, vc_ref, o_ref, l_ref):
        b = pl.program_id(1)
        mask_c, mask_p = _band_masks(b)
        for hd in range(NH):
            sl = slice(hd * HD, (hd + 1) * HD)
            q = q_ref[:, sl].astype(BF16)
            sc = jnp.where(mask_c, _dot_nt(q, kc_ref[:, sl].astype(BF16)) * scale, -1e30)
            sp = jnp.where(mask_p, _dot_nt(q, kp_ref[:, sl].astype(BF16)) * scale, -1e30)
            m = jnp.maximum(jnp.max(sc, axis=-1, keepdims=True), jnp.max(sp, axis=-1, keepdims=True))
            ec = jnp.exp(sc - m)
            ep = jnp.exp(sp - m)
            den = jnp.sum(ec, axis=-1, keepdims=True) + jnp.sum(ep, axis=-1, keepdims=True)
            o = _dot(ec.astype(BF16), vc_ref[:, sl].astype(BF16)) + _dot(ep.astype(BF16), vp_ref[:, sl].astype(BF16))
            o_ref[:, sl] = o / den
            l_ref[:, sl] = jnp.broadcast_to(m + jnp.log(den), (BAND, HD))

    def cur(s):
        return pl.BlockSpec((None, BAND, DA), lambda r, b: (s, b, r))

    def prev(s):
        return pl.BlockSpec((None, BAND, DA), lambda r, b: (s, jnp.maximum(b - 1, 0), r))

    out = pl.BlockSpec((BAND, DA), lambda r, b: (b, r))
    o, l = _pc(body, name=f"attn_fwd_d{d}_l{layer}", grid=(d, nb),
               in_specs=[cur(0), prev(1), cur(1), prev(2), cur(2)], out_specs=[out, out],
               out_shape=[S((T // d, d * DA), F32)] * 2, compiler_params=_cp(2))(Pv, Pv, Pv, Pv, Pv)
    return o.reshape(T, DA), l.reshape(T, DA)


def _attn_bwd(P, dO, lse, delta, acc, d, layer):
    T = P.shape[1]
    nb = T // d // BAND
    Tv, Cv = T // d, d * DA
    Pv = P.reshape(NSH, Tv, Cv)
    scale = HD ** -0.5
    first = acc is None

    def body(*refs):
        q_ref, kp_ref, kc_ref, vp_ref, vc_ref, do_ref, l_ref, dl_ref = refs[:8]
        if first:
            dq_ref, dk_ref, dv_ref, ck, cv = refs[8:]
        else:
            aq_ref, ak_ref, av_ref, dq_ref, dk_ref, dv_ref, ck, cv = refs[8:]
        b = pl.program_id(1)

        @pl.when(b == 0)
        def _():
            ck[...] = jnp.zeros_like(ck)
            cv[...] = jnp.zeros_like(cv)

        @pl.when(b < nb)
        def _():
            mask_c, mask_p = _band_masks(b)
            for hd in range(NH):
                sl = slice(hd * HD, (hd + 1) * HD)
                one = slice(hd * HD, hd * HD + 1)
                q = q_ref[:, sl].astype(BF16)
                kc = kc_ref[:, sl].astype(BF16)
                kp = kp_ref[:, sl].astype(BF16)
                do = do_ref[:, sl].astype(BF16)
                lrow = l_ref[:, one]
                drow = dl_ref[:, one]
                pc = jnp.where(mask_c, jnp.exp(_dot_nt(q, kc) * scale - lrow), 0.0)
                pp = jnp.where(mask_p, jnp.exp(_dot_nt(q, kp) * scale - lrow), 0.0)
                dsc = (pc * (_dot_nt(do, vc_ref[:, sl].astype(BF16)) - drow) * scale).astype(BF16)
                dsp = (pp * (_dot_nt(do, vp_ref[:, sl].astype(BF16)) - drow) * scale).astype(BF16)
                dq = _dot(dsc, kc) + _dot(dsp, kp)
                dkp = ck[:, sl] + _dot_tn(dsp, q)
                dvp = cv[:, sl] + _dot_tn(pp.astype(BF16), do)
                if first:
                    dq_ref[:, sl] = dq
                    dk_ref[:, sl] = dkp
                    dv_ref[:, sl] = dvp
                else:
                    dq_ref[:, sl] = aq_ref[:, sl] + dq
                    dk_ref[:, sl] = ak_ref[:, sl] + dkp
                    dv_ref[:, sl] = av_ref[:, sl] + dvp
                ck[:, sl] = _dot_tn(dsc, q)
                cv[:, sl] = _dot_tn(pc.astype(BF16), do)

        @pl.when(b == nb)
        def _():
            if first:
                dk_ref[...] = ck[...]
                dv_ref[...] = cv[...]
            else:
                dk_ref[...] = ak_ref[...] + ck[...]
                dv_ref[...] = av_ref[...] + cv[...]

    def qb(b):
        return jnp.minimum(b, nb - 1)

    def cur(s):
        return pl.BlockSpec((None, BAND, DA), lambda r, b: (s, qb(b), r))

    def prev(s):
        return pl.BlockSpec((None, BAND, DA), lambda r, b: (s, jnp.maximum(qb(b) - 1, 0), r))

    qrow = pl.BlockSpec((BAND, DA), lambda r, b: (qb(b), r))
    krow = pl.BlockSpec((BAND, DA), lambda r, b: (jnp.maximum(b - 1, 0), r))
    view = lambda t: t.reshape(Tv, Cv)
    ins = [Pv, Pv, Pv, Pv, Pv, view(dO), view(lse), view(delta)]
    specs = [cur(0), prev(1), cur(1), prev(2), cur(2), qrow, qrow, qrow]
    if not first:
        ins += [view(t) for t in acc]
        specs += [qrow, krow, krow]
    dq, dk, dv = _pc(body, name=f"attn_bwd_d{d}_l{layer}", grid=(d, nb + 1), in_specs=specs,
                     out_specs=[qrow, krow, krow], out_shape=[S((Tv, Cv), F32)] * 3,
                     scratch_shapes=[pltpu.VMEM((BAND, DA), F32)] * 2, compiler_params=_cp(2))(*ins)
    return dq.reshape(T, DA), dk.reshape(T, DA), dv.reshape(T, DA)


def _ssm_prep(lam_re, lam_im, log_dt, b_re, b_im, c_re, c_im):
    dt = jnp.exp(log_dt)[:, None]
    er = jnp.exp(lam_re * dt)
    a_re = er * jnp.cos(lam_im * dt)
    a_im = er * jnp.sin(lam_im * dt)
    nr, ni = a_re - 1.0, a_im
    den = lam_re * lam_re + lam_im * lam_im
    cr = (nr * lam_re + ni * lam_im) / den
    ci = (ni * lam_re - nr * lam_im) / den
    bbr = cr[..., None] * b_re - ci[..., None] * b_im
    bbi = cr[..., None] * b_im + ci[..., None] * b_re
    eye = jnp.eye(8, dtype=F32)

    def bblock(bb):
        t = bb.reshape(4, 8, 64, 16).transpose(0, 1, 3, 2)
        return (t[:, :, :, None, :] * eye[None, :, None, :, None]).reshape(4, 128, 512)

    def cblock(cc):
        t = cc.reshape(4, 8, 16, 64).transpose(0, 1, 3, 2)
        return (t[:, :, :, None, :] * eye[None, :, None, :, None]).reshape(4, 512, 128)

    return (a_re.reshape(NLB, 1, 128), a_im.reshape(NLB, 1, 128), bblock(bbr), bblock(bbi), cblock(c_re), cblock(c_im))


def _ssm_in(P, bre, bim, layer):
    T = P.shape[1]
    tm = TM

    def body(u_ref, br_ref, bi_ref, or_ref, oi_ref):
        for s in range(4):
            uc = u_ref[:, s * 128:(s + 1) * 128]
            r = _dot3(_dot, uc, br_ref[s])
            m = _dot3(_dot, uc, bi_ref[s])
            for q in range(4):
                or_ref[4 * s + q] = r[:, q * 128:(q + 1) * 128]
                oi_ref[4 * s + q] = m[:, q * 128:(q + 1) * 128]

    whole = pl.BlockSpec((4, 128, 512), lambda i: (0, 0, 0))
    st = pl.BlockSpec((NLB, tm, 128), lambda i: (0, i, 0))
    return _pc(body, name=f"ssm_in_l{layer}", grid=(T // tm,),
               in_specs=[pl.BlockSpec((None, tm, DSS), lambda i: (3, i, 0)), whole, whole], out_specs=[st, st],
               out_shape=[S((NLB, T, 128), F32)] * 2, compiler_params=_cp(1))(P, bre, bim)


def _scan(br, bi, a_re, a_im, reverse, layer):
    T = br.shape[1]
    ch = T // NCH
    nvg = NCH // 8
    nbk = 2
    sgn = -1.0 if reverse else 1.0
    kv = [(k, v) for k in range(nbk) for v in range(nvg)]

    def body(br_ref, bi_ref, ar_ref, ai_ref, xr_ref, xi_ref):
        brs, bis = [br_ref.at[k] for k in range(nbk)], [bi_ref.at[k] for k in range(nbk)]
        xrs, xis = [xr_ref.at[k] for k in range(nbk)], [xi_ref.at[k] for k in range(nbk)]
        ars = [jnp.broadcast_to(ar_ref[k], (8, 128)) for k in range(nbk)]
        ais = [sgn * jnp.broadcast_to(ai_ref[k], (8, 128)) for k in range(nbk)]
        zero = jnp.zeros((8, 128), F32)
        one = jnp.ones((8, 128), F32)

        def at(v, i):
            return pl.ds(v * 8 * ch + ((ch - 1 - i) if reverse else i), 8, stride=ch)

        def cmul(k, pr, pi_):
            return ars[k] * pr - ais[k] * pi_, ars[k] * pi_ + ais[k] * pr

        def local(i, carry):
            states, pws = carry
            out = []
            for (k, v), (xr, xi) in zip(kv, states):
                nr = ars[k] * xr - ais[k] * xi + brs[k][at(v, i), :]
                ni = ars[k] * xi + ais[k] * xr + bis[k][at(v, i), :]
                xrs[k][at(v, i), :] = nr
                xis[k][at(v, i), :] = ni
                out.append((nr, ni))
            return tuple(out), tuple(cmul(k, *pws[k]) for k in range(nbk))

        ends, apow = lax.fori_loop(0, ch, local, (tuple((zero, zero) for _ in kv), tuple((one, zero) for _ in range(nbk))))
        row = lax.broadcasted_iota(jnp.int32, (8, 128), 0)
        edge = 7 if reverse else 0
        shift = 7 if reverse else 1

        def resolve(_, carries):
            moved = []
            for (k, v), (sr, si), (er, ei) in zip(kv, carries, ends):
                pr, pi_ = apow[k]
                moved.append((pltpu.roll(pr * sr - pi_ * si + er, shift, 0), pltpu.roll(pr * si + pi_ * sr + ei, shift, 0)))
            out = []
            for n, (k, v) in enumerate(kv):
                nb_v = v + 1 if reverse else v - 1
                src = moved[n + (1 if reverse else -1)] if 0 <= nb_v < nvg else (zero, zero)
                out.append((jnp.where(row == edge, src[0], moved[n][0]), jnp.where(row == edge, src[1], moved[n][1])))
            return tuple(out)

        carries = lax.fori_loop(0, NCH - 1, resolve, tuple((zero, zero) for _ in kv))

        def fix(i, pws):
            for (k, v), (sr, si) in zip(kv, carries):
                pr, pi_ = pws[k]
                xrs[k][at(v, i), :] = xrs[k][at(v, i), :] + (pr * sr - pi_ * si)
                xis[k][at(v, i), :] = xis[k][at(v, i), :] + (pr * si + pi_ * sr)
            return tuple(cmul(k, *pws[k]) for k in range(nbk))

        lax.fori_loop(0, ch, fix, tuple((ars[k], ais[k]) for k in range(nbk)))

    st = pl.BlockSpec((nbk, T, 128), lambda i: (i, 0, 0))
    av = pl.BlockSpec((nbk, 1, 128), lambda i: (i, 0, 0))
    return _pc(body, name=f"scan_{'bwd' if reverse else 'fwd'}_l{layer}", grid=(NLB // nbk,),
               in_specs=[st, st, av, av], out_specs=[st, st], out_shape=[S((NLB, T, 128), F32)] * 2,
               compiler_params=_cp(1))(br, bi, a_re, a_im)


def _ssm_out(xr, xi, P, cre, cim, dvec, wglu, bglu, layer):
    T = P.shape[1]
    tm = TM

    def body(xr_ref, xi_ref, u_ref, cr_ref, ci_ref, d_ref, w_ref, bg_ref, s_ref, y_ref, z_ref):
        ys = []
        for s in range(4):
            xrc = jnp.concatenate([xr_ref[4 * s + q] for q in range(4)], axis=1)
            xic = jnp.concatenate([xi_ref[4 * s + q] for q in range(4)], axis=1)
            ys.append(_dot3(_dot, xrc, cr_ref[s]) - _dot3(_dot, xic, ci_ref[s]))
        y = jnp.concatenate(ys, axis=1) + d_ref[...] * u_ref[...]
        yg = _gelu(y)
        ygb = yg.astype(BF16)
        z = bg_ref[...] + sum(_dot(ygb[:, j * 128:(j + 1) * 128], w_ref[j]) for j in range(NSH))
        y_ref[...] = y
        z_ref[...] = z
        s_ref[...] = yg * jax.nn.sigmoid(z)

    st = pl.BlockSpec((NLB, tm, 128), lambda i: (0, i, 0))
    cw = pl.BlockSpec((4, 512, 128), lambda i: (0, 0, 0))
    half = pl.BlockSpec((tm, DSS), lambda i: (i, 0))
    return _pc(body, name=f"ssm_out_l{layer}", grid=(T // tm,),
               in_specs=[st, st, pl.BlockSpec((None, tm, DSS), lambda i: (3, i, 0)), cw, cw, _gain_spec(DSS, layer),
                         pl.BlockSpec((NSH, None, 128, DSS), lambda i: (0, layer, 0, 0)), _gain_spec(DSS, layer)],
               out_specs=[half, half, half], out_shape=[S((T, DSS), F32)] * 3,
               compiler_params=_cp(1))(xr, xi, P, cre, cim, dvec, wglu, bglu)


def _ssm_out_bwd(dssm, y, z, xr, xi, P, cre, cim, dvec, wglu, layer):
    T = P.shape[1]
    tm = TMB

    def body(ds_ref, y_ref, z_ref, xr_ref, xi_ref, u_ref, cr_ref, ci_ref, d_ref, w_ref,
             gr_ref, gi_ref, du_ref, dz_ref, yg_ref, dbg_ref, dd_ref, dcr_ref, dci_ref):
        i = pl.program_id(0)

        @pl.when(i == 0)
        def _():
            dbg_ref[...] = jnp.zeros_like(dbg_ref)
            dd_ref[...] = jnp.zeros_like(dd_ref)
            dcr_ref[...] = jnp.zeros_like(dcr_ref)
            dci_ref[...] = jnp.zeros_like(dci_ref)

        yv = y_ref[...]
        yg = _gelu(yv)
        sg = jax.nn.sigmoid(z_ref[...])
        ds = ds_ref[...]
        dz = ds * yg * sg * (1.0 - sg)
        dzb = dz.astype(BF16)
        dz_ref[...] = dzb
        yg_ref[...] = yg.astype(BF16)
        dbg_ref[...] += jnp.sum(dz, axis=0, keepdims=True)
        dyg = ds * sg + jnp.concatenate([_dot_nt(dzb, w_ref[j]) for j in range(NSH)], axis=1)
        dy = dyg * _gelu_grad(yv)
        u = u_ref[...]
        dd_ref[...] += jnp.sum(dy * u, axis=0, keepdims=True)
        du_ref[...] = dy * d_ref[...]
        for s in range(4):
            dyc = dy[:, s * 128:(s + 1) * 128]
            g_r = _dot3(_dot_nt, dyc, cr_ref[s])
            g_i = -_dot3(_dot_nt, dyc, ci_ref[s])
            for q in range(4):
                gr_ref[4 * s + q] = g_r[:, q * 128:(q + 1) * 128]
                gi_ref[4 * s + q] = g_i[:, q * 128:(q + 1) * 128]
            xrc = jnp.concatenate([xr_ref[4 * s + q] for q in range(4)], axis=1)
            xic = jnp.concatenate([xi_ref[4 * s + q] for q in range(4)], axis=1)
            dcr_ref[s] += _dot3(_dot_tn, xrc, dyc)
            dci_ref[s] -= _dot3(_dot_tn, xic, dyc)

    st = pl.BlockSpec((NLB, tm, 128), lambda i: (0, i, 0))
    cw = pl.BlockSpec((4, 512, 128), lambda i: (0, 0, 0))
    half = pl.BlockSpec((tm, DSS), lambda i: (i, 0))
    return _pc(body, name=f"ssm_out_bwd_l{layer}", grid=(T // tm,),
               in_specs=[half, half, half, st, st, pl.BlockSpec((None, tm, DSS), lambda i: (3, i, 0)), cw, cw,
                         _gain_spec(DSS, layer), pl.BlockSpec((NSH, None, 128, DSS), lambda i: (0, layer, 0, 0))],
               out_specs=[st, st, half, half, half, _row_acc_spec(DSS), _row_acc_spec(DSS), cw, cw],
               out_shape=[S((NLB, T, 128), F32)] * 2 + [S((T, DSS), F32), S((T, DSS), BF16), S((T, DSS), BF16),
                                                        S((1, DSS), F32), S((1, DSS), F32),
                                                        S((4, 512, 128), F32), S((4, 512, 128), F32)],
               compiler_params=_cp(1))(dssm, y, z, xr, xi, P, cre, cim, dvec, wglu)


def _ssm_da(gr, gi, xr, xi, layer):
    T = gr.shape[1]
    tb = 1024 if T % 1024 == 0 else T

    def body(gr_ref, gi_ref, xr_ref, xi_ref, dr_ref, di_ref, lr, li):
        t = pl.program_id(1)

        @pl.when(t == 0)
        def _():
            dr_ref[...] = jnp.zeros_like(dr_ref)
            di_ref[...] = jnp.zeros_like(di_ref)
            lr[...] = jnp.zeros_like(lr)
            li[...] = jnp.zeros_like(li)

        g_r, g_i, x_r, x_i = gr_ref[...], gi_ref[...], xr_ref[...], xi_ref[...]
        pr = pltpu.roll(x_r, 1, 0)
        pi_ = pltpu.roll(x_i, 1, 0)
        g0r, g0i = g_r[0:1, :], g_i[0:1, :]
        fr = lr[7:8, :] - x_r[tb - 1:tb, :]
        fi = li[7:8, :] - x_i[tb - 1:tb, :]
        dr_ref[...] += jnp.sum(g_r * pr + g_i * pi_, axis=0, keepdims=True) + g0r * fr + g0i * fi
        di_ref[...] += jnp.sum(g_i * pr - g_r * pi_, axis=0, keepdims=True) + g0i * fr - g0r * fi
        lr[...] = x_r[tb - 8:tb, :]
        li[...] = x_i[tb - 8:tb, :]

    st = pl.BlockSpec((None, tb, 128), lambda k, t: (k, t, 0))
    out = pl.BlockSpec((None, 1, 128), lambda k, t: (k, 0, 0))
    return _pc(body, name=f"ssm_da_l{layer}", grid=(NLB, T // tb), in_specs=[st] * 4, out_specs=[out, out],
               out_shape=[S((NLB, 1, 128), F32)] * 2, scratch_shapes=[pltpu.VMEM((8, 128), F32)] * 2,
               compiler_params=_cp(2))(gr, gi, xr, xi)


def _ssm_in_bwd(gr, gi, P, bre, bim, du_direct, layer):
    T = P.shape[1]
    tm = TM

    def body(gr_ref, gi_ref, u_ref, br_ref, bi_ref, dd_ref, du_ref, dbr_ref, dbi_ref):
        i = pl.program_id(0)

        @pl.when(i == 0)
        def _():
            dbr_ref[...] = jnp.zeros_like(dbr_ref)
            dbi_ref[...] = jnp.zeros_like(dbi_ref)

        dus = []
        for s in range(4):
            grc = jnp.concatenate([gr_ref[4 * s + q] for q in range(4)], axis=1)
            gic = jnp.concatenate([gi_ref[4 * s + q] for q in range(4)], axis=1)
            uc = u_ref[:, s * 128:(s + 1) * 128]
            dus.append(_dot3(_dot_nt, grc, br_ref[s]) + _dot3(_dot_nt, gic, bi_ref[s]))
            dbr_ref[s] += _dot3(_dot_tn, uc, grc)
            dbi_ref[s] += _dot3(_dot_tn, uc, gic)
        du_ref[...] = jnp.concatenate(dus, axis=1) + dd_ref[...]

    whole = pl.BlockSpec((4, 128, 512), lambda i: (0, 0, 0))
    st = pl.BlockSpec((NLB, tm, 128), lambda i: (0, i, 0))
    half = pl.BlockSpec((tm, DSS), lambda i: (i, 0))
    return _pc(body, name=f"ssm_in_bwd_l{layer}", grid=(T // tm,),
               in_specs=[st, st, pl.BlockSpec((None, tm, DSS), lambda i: (3, i, 0)), whole, whole, half],
               out_specs=[half, whole, whole],
               out_shape=[S((T, DSS), F32), S((4, 128, 512), F32), S((4, 128, 512), F32)],
               compiler_params=_cp(1))(gr, gi, P, bre, bim, du_direct)


def _mix_out(outs, lses, ssm, h, attn_g, ssm_g, post_g, wout, layer):
    T = h.shape[0]
    tm = TM

    def body(o1, o2, o3, l1, l2, l3, s_ref, h_ref, ag_ref, sg_ref, pg_ref, w_ref, ho_ref, at_ref, ls_ref, mx_ref, mo_ref):
        la, lb, lc = l1[...], l2[...], l3[...]
        m = jnp.maximum(jnp.maximum(la, lb), lc)
        wa, wb, wc = jnp.exp(la - m), jnp.exp(lb - m), jnp.exp(lc - m)
        zs = wa + wb + wc
        attn = (wa * o1[...] + wb * o2[...] + wc * o3[...]) / zs
        at_ref[...] = attn
        ls_ref[...] = m + jnp.log(zs)
        mixed = jnp.concatenate([_rms_fwd(attn, ag_ref[...]), _rms_fwd(s_ref[...], sg_ref[...])], axis=1).astype(BF16)
        mx_ref[...] = mixed
        mo = sum(_dot(mixed[:, j * 256:(j + 1) * 256], w_ref[j]) for j in range(NSH))
        mo_ref[...] = mo
        ho_ref[...] = h_ref[...] + _rms_fwd(mo, pg_ref[...])

    row = pl.BlockSpec((tm, D), lambda i: (i, 0))
    half = pl.BlockSpec((tm, DA), lambda i: (i, 0))
    return _pc(body, name=f"mix_out_l{layer}", grid=(T // tm,),
               in_specs=[half] * 7 + [row, _gain_spec(DA, layer), _gain_spec(DSS, layer), _gain_spec(D, layer),
                                      pl.BlockSpec((NSH, None, 256, D), lambda i: (0, layer, 0, 0))],
               out_specs=[row, half, half, row, row],
               out_shape=[S((T, D), F32), S((T, DA), F32), S((T, DA), F32), S((T, D), BF16), S((T, D), F32)],
               compiler_params=_cp(1))(*outs, *lses, ssm, h, attn_g, ssm_g, post_g, wout)


def _mix_out_bwd(dout, mo, attn, ssm, attn_g, ssm_g, post_g, wout, layer):
    T = dout.shape[0]
    tm = TMB
    head_sum = jnp.asarray(np.kron(np.eye(NH, dtype=np.float32), np.ones((HD, HD), np.float32)), BF16)

    def body(do_ref, mo_ref, at_ref, s_ref, ag_ref, sg_ref, pg_ref, w_ref, e_ref,
             da_ref, ds_ref, dl_ref, dmo_ref, dpg_ref, dag_ref, dsg_ref):
        i = pl.program_id(0)

        @pl.when(i == 0)
        def _():
            dpg_ref[...] = jnp.zeros_like(dpg_ref)
            dag_ref[...] = jnp.zeros_like(dag_ref)
            dsg_ref[...] = jnp.zeros_like(dsg_ref)

        dmo, dpg = _rms_bwd(do_ref[...], mo_ref[...], pg_ref[...])
        dpg_ref[...] += dpg
        dmob = dmo.astype(BF16)
        dmo_ref[...] = dmob
        dmix = jnp.concatenate([_dot_nt(dmob, w_ref[j]) for j in range(NSH)], axis=1)
        attn = at_ref[...]
        dat, dag = _rms_bwd(dmix[:, :DA], attn, ag_ref[...])
        dss, dsg = _rms_bwd(dmix[:, DA:], s_ref[...], sg_ref[...])
        dag_ref[...] += dag
        dsg_ref[...] += dsg
        da_ref[...] = dat
        ds_ref[...] = dss
        prod = dat * attn
        p1 = prod.astype(BF16)
        r1 = prod - p1.astype(F32)
        p2 = r1.astype(BF16)
        p3 = (r1 - p2.astype(F32)).astype(BF16)
        e = e_ref[...]
        dl_ref[...] = _dot(p1, e) + _dot(p2, e) + _dot(p3, e)

    row = pl.BlockSpec((tm, D), lambda i: (i, 0))
    half = pl.BlockSpec((tm, DA), lambda i: (i, 0))
    return _pc(body, name=f"mix_out_bwd_l{layer}", grid=(T // tm,),
               in_specs=[row, row, half, half, _gain_spec(DA, layer), _gain_spec(DSS, layer), _gain_spec(D, layer),
                         pl.BlockSpec((NSH, None, 256, D), lambda i: (0, layer, 0, 0)),
                         pl.BlockSpec((DA, DA), lambda i: (0, 0))],
               out_specs=[half, half, half, row, _row_acc_spec(D), _row_acc_spec(DA), _row_acc_spec(DSS)],
               out_shape=[S((T, DA), F32)] * 3 + [S((T, D), BF16), S((1, D), F32), S((1, DA), F32), S((1, DSS), F32)],
               compiler_params=_cp(1))(dout, mo, attn, ssm, attn_g, ssm_g, post_g, wout, head_sum)


def _ple_fwd(h, p3, wup, wgate, post_g, layer):
    T = h.shape[0]
    tm = TM

    def body(h_ref, p_ref, wu_ref, wg_ref, g_ref, ho_ref, e_ref, gt_ref):
        hv = h_ref[...]
        hb = hv.astype(BF16)
        pb = p_ref[...].astype(BF16)
        gte = sum(_dot(hb[:, j * 256:(j + 1) * 256], wg_ref[j]) for j in range(NSH))
        e = jnp.concatenate([_dot(pb, wu_ref[j]) for j in range(NSH)], axis=1)
        e_ref[...] = e
        gt_ref[...] = gte
        ho_ref[...] = hv + _rms_fwd(e * jax.nn.sigmoid(gte), g_ref[...])

    row = pl.BlockSpec((tm, D), lambda i: (i, 0))
    return _pc(body, name=f"ple_fwd_l{layer}", grid=(T // tm,),
               in_specs=[row, pl.BlockSpec((None, tm, PLE), lambda i: (layer, i, 0)),
                         pl.BlockSpec((NSH, None, PLE, 256), lambda i: (0, layer, 0, 0)),
                         pl.BlockSpec((NSH, None, 256, D), lambda i: (0, layer, 0, 0)), _gain_spec(D, layer)],
               out_specs=[row, row, row], out_shape=[S((T, D), F32)] * 3,
               compiler_params=_cp(1))(h, p3, wup, wgate, post_g)


def _ple_bwd(dout, e, gte, wgate, post_g, layer):
    T = dout.shape[0]
    tm = TMB

    def body(do_ref, e_ref, gt_ref, wg_ref, g_ref, dh_ref, de_ref, dgt_ref, dg_ref):
        i = pl.program_id(0)

        @pl.when(i == 0)
        def _():
            dg_ref[...] = jnp.zeros_like(dg_ref)

        ev = e_ref[...]
        sg = jax.nn.sigmoid(gt_ref[...])
        do = do_ref[...]
        dple, dg = _rms_bwd(do, ev * sg, g_ref[...])
        dg_ref[...] += dg
        de = (dple * sg).astype(BF16)
        for j in range(NSH):
            de_ref[j] = de[:, j * 256:(j + 1) * 256]
        dgb = (dple * ev * sg * (1.0 - sg)).astype(BF16)
        dgt_ref[...] = dgb
        dh_ref[...] = do + jnp.concatenate([_dot_nt(dgb, wg_ref[j]) for j in range(NSH)], axis=1)

    row = pl.BlockSpec((tm, D), lambda i: (i, 0))
    return _pc(body, name=f"ple_bwd_l{layer}", grid=(T // tm,),
               in_specs=[row, row, row, pl.BlockSpec((NSH, None, 256, D), lambda i: (0, layer, 0, 0)), _gain_spec(D, layer)],
               out_specs=[row, pl.BlockSpec((NSH, tm, 256), lambda i: (0, i, 0)), row, _row_acc_spec(D)],
               out_shape=[S((T, D), F32), S((NSH, T, 256), BF16), S((T, D), BF16), S((1, D), F32)],
               compiler_params=_cp(1))(dout, e, gte, wgate, post_g)


def _loss_head(h, target):
    T = h.shape[0]
    tm = TM

    def body(h_ref, t_ref, dy_ref, l_ref):
        i = pl.program_id(0)

        @pl.when(i == 0)
        def _():
            l_ref[...] = jnp.zeros_like(l_ref)

        err = h_ref[...] - t_ref[...]
        dy_ref[...] = err * (1.0 / D)
        l_ref[...] += jnp.broadcast_to((0.5 / D) * jnp.sum(err * err), (1, 128))

    row = pl.BlockSpec((tm, D), lambda i: (i, 0))
    return _pc(body, name="loss_head", grid=(T // tm,), in_specs=[row, row],
               out_specs=[row, pl.BlockSpec((1, 128), lambda i: (0, 0))],
               out_shape=[S((T, D), F32), S((1, 128), F32)], compiler_params=_cp(1))(h, target)


def _local_step(x, p3, pos_col, target, W, Sm):
    L = p3.shape[0]
    g3 = {n: Sm[n].reshape(L, 1, -1) for n in ("ffn1_pre_g", "ffn1_post_g", "mix_pre_g", "attn_norm_g", "ssm_norm_g",
                                                "mix_post_g", "ffn2_pre_g", "ffn2_post_g", "ple_post_g", "ssm_b_glu", "ssm_d")}
    rot = _rot_tables(pos_col)
    prep_names = ("ssm_lam_re", "ssm_lam_im", "ssm_log_dt", "ssm_b_re", "ssm_b_im", "ssm_c_re", "ssm_c_im")

    saved = []
    h = x
    for l in range(L):
        sv = {"h0": h}
        h, sv["a1"], sv["b1"], sv["f1"], sv["xn1"] = _ffn_fwd(
            h, g3["ffn1_pre_g"], g3["ffn1_post_g"], W["ffn1_w_gate"], W["ffn1_w_up"], W["ffn1_w_down"], l, "1")
        sv["h1"] = h
        P, sv["ain"] = _mix_proj(h, g3["mix_pre_g"], W["w_in"], rot, l)
        sv["P"] = P
        ol = [_attn_fwd(P, d, l) for d in PATTERN_DILATIONS]
        prep, sv["prep_vjp"] = jax.vjp(_ssm_prep, *[Sm[n][l] for n in prep_names])
        a_re, a_im, bre, bim, cre, cim = prep
        sv["prep"] = prep
        bur, bui = _ssm_in(P, bre, bim, l)
        xr, xi = _scan(bur, bui, a_re, a_im, False, l)
        sv["xr"], sv["xi"] = xr, xi
        ssm, sv["y"], sv["z"] = _ssm_out(xr, xi, P, cre, cim, g3["ssm_d"], W["ssm_w_glu"], g3["ssm_b_glu"], l)
        sv["ssm"] = ssm
        h, sv["attn"], sv["lse"], sv["mixed"], sv["mo"] = _mix_out(
            [o for o, _ in ol], [s for _, s in ol], ssm, h, g3["attn_norm_g"], g3["ssm_norm_g"], g3["mix_post_g"],
            W["w_out"], l)
        sv["h2"] = h
        h, sv["a2"], sv["b2"], sv["f2"], sv["xn2"] = _ffn_fwd(
            h, g3["ffn2_pre_g"], g3["ffn2_post_g"], W["ffn2_w_gate"], W["ffn2_w_up"], W["ffn2_w_down"], l, "2")
        sv["h3"] = h
        h, sv["e"], sv["gte"] = _ple_fwd(h, p3, W["ple_w_up"], W["ple_w_gate"], g3["ple_post_g"], l)
        saved.append(sv)

    dh, loss = _loss_head(h, target)

    G = {n: lax.empty((NSH, L, r, c), BF16) for n, r, c in BIG}
    sg = {n: [None] * L for n in SMALL}
    whole = lambda j: (0, 0)
    shard = lambda j: (j, 0)
    kcol = lambda j: (0, j)
    for l in reversed(range(L)):
        sv = saved[l]
        dh, de, dgte, sg["ple_post_g"][l] = _ple_bwd(dh, sv["e"], sv["gte"], W["ple_w_gate"], g3["ple_post_g"], l)
        G["ple_w_up"] = _dw(p3[l][None], de, G["ple_w_up"], l, PLE, 256, whole, shard, f"dw_ple_up_l{l}")
        G["ple_w_gate"] = _dw(sv["h3"][None], dgte[None], G["ple_w_gate"], l, 256, D, kcol, whole, f"dw_ple_gate_l{l}")
        dh, df, da, db, hh, sg["ffn2_pre_g"][l], sg["ffn2_post_g"][l] = _ffn_bwd(
            dh, sv["h2"], sv["f2"], sv["a2"], sv["b2"], g3["ffn2_pre_g"], g3["ffn2_post_g"],
            W["ffn2_w_gate"], W["ffn2_w_up"], W["ffn2_w_down"], l, "2")
        G["ffn2_w_gate"] = _dw(da, sv["xn2"][None], G["ffn2_w_gate"], l, DFS, D, shard, whole, f"dw_ffn2_gate_l{l}")
        G["ffn2_w_up"] = _dw(db, sv["xn2"][None], G["ffn2_w_up"], l, DFS, D, shard, whole, f"dw_ffn2_up_l{l}")
        G["ffn2_w_down"] = _dw(hh, df[None], G["ffn2_w_down"], l, DFS, D, shard, whole, f"dw_ffn2_down_l{l}")
        a_re, a_im, bre, bim, cre, cim = sv["prep"]
        dattn, dssm, delta, dmo, sg["mix_post_g"][l], sg["attn_norm_g"][l], sg["ssm_norm_g"][l] = _mix_out_bwd(
            dh, sv["mo"], sv["attn"], sv["ssm"], g3["attn_norm_g"], g3["ssm_norm_g"], g3["mix_post_g"], W["w_out"], l)
        G["w_out"] = _dw(sv["mixed"][None], dmo[None], G["w_out"], l, 256, D, kcol, whole, f"dw_out_l{l}")
        gnr, gni, du_direct, dz, yg, sg["ssm_b_glu"][l], dd, dcre, dcim = _ssm_out_bwd(
            dssm, sv["y"], sv["z"], sv["xr"], sv["xi"], sv["P"], cre, cim, g3["ssm_d"], W["ssm_w_glu"], l)
        sg["ssm_d"][l] = dd.reshape(Sm["ssm_d"].shape[1:])
        G["ssm_w_glu"] = _dw(yg[None], dz[None], G["ssm_w_glu"], l, 128, DSS, kcol, whole, f"dw_glu_l{l}")
        gr, gi = _scan(gnr, gni, a_re, a_im, True, l)
        dar, dai = _ssm_da(gr, gi, sv["xr"], sv["xi"], l)
        du, dbre, dbim = _ssm_in_bwd(gr, gi, sv["P"], bre, bim, du_direct, l)
        for n, g in zip(prep_names, sv["prep_vjp"]((dar, dai, dbre, dbim, dcre, dcim))):
            sg[n][l] = g
        acc = None
        for d in PATTERN_DILATIONS:
            acc = _attn_bwd(sv["P"], dattn, sv["lse"], delta, acc, d, l)
        dh, dP, sg["mix_pre_g"][l] = _mix_proj_bwd(acc[0], acc[1], acc[2], du, dh, sv["h1"], g3["mix_pre_g"],
                                                   W["w_in"], rot, l)
        G["w_in"] = _dw(sv["ain"][None], dP, G["w_in"], l, D, DA, whole, shard, f"dw_in_l{l}")
        dh, df, da, db, hh, sg["ffn1_pre_g"][l], sg["ffn1_post_g"][l] = _ffn_bwd(
            dh, sv["h0"], sv["f1"], sv["a1"], sv["b1"], g3["ffn1_pre_g"], g3["ffn1_post_g"],
            W["ffn1_w_gate"], W["ffn1_w_up"], W["ffn1_w_down"], l, "1")
        G["ffn1_w_gate"] = _dw(da, sv["xn1"][None], G["ffn1_w_gate"], l, DFS, D, shard, whole, f"dw_ffn1_gate_l{l}")
        G["ffn1_w_up"] = _dw(db, sv["xn1"][None], G["ffn1_w_up"], l, DFS, D, shard, whole, f"dw_ffn1_up_l{l}")
        G["ffn1_w_down"] = _dw(hh, df[None], G["ffn1_w_down"], l, DFS, D, shard, whole, f"dw_ffn1_down_l{l}")

    small = {n: jnp.stack([g.reshape(Sm[n].shape[1:]) for g in sg[n]]) for n in SMALL}
    return loss, dh, G, small


HBM_SPEC = pl.BlockSpec(memory_space=pltpu.HBM)


def _place():
    x, y, c = lax.axis_index("x"), lax.axis_index("y"), lax.axis_index("c")
    chips = [(1 - x, y), (x, 1 - y), (1 - x, 1 - y)]
    return x, y, c, chips


def _comm_params():
    return pltpu.CompilerParams(vmem_limit_bytes=VMEM_LIMIT)


def _gather_weights(ws):
    n = len(ws)

    def body(*refs):
        ins, outs = refs[:n], refs[n:2 * n]
        s_ici, r_ici, s_d2d, r_d2d, s_loc = refs[2 * n:]
        x, y, c, chips = _place()
        me = 2 * x + y

        def half(ref, t, hc):
            r2 = ws[t].shape[1] // 2
            return ref.at[:, pl.ds(hc * r2, r2), :]

        def ici(t, k, src_chip, to):
            j = 2 * src_chip[0] + src_chip[1]
            src = half(ins[t], t, c) if to is not None else half(outs[t].at[j], t, c)
            return pltpu.make_async_remote_copy(src_ref=src, dst_ref=half(outs[t].at[j], t, c),
                                                send_sem=s_ici.at[3 * t + k], recv_sem=r_ici.at[3 * t + k],
                                                device_id=to if to is not None else (x, y, c), device_id_type=MESH)

        def d2d(t, k, hc):
            j = 2 * chips[k][0] + chips[k][1]
            r = half(outs[t].at[j], t, hc)
            return pltpu.make_async_remote_copy(src_ref=r, dst_ref=r, send_sem=s_d2d.at[3 * t + k],
                                                recv_sem=r_d2d.at[3 * t + k], device_id=(x, y, 1 - c),
                                                device_id_type=MESH)

        own = [pltpu.make_async_copy(ins[t], outs[t].at[me], s_loc.at[t]) for t in range(n)]
        for cp in own:
            cp.start()
        sends = [ici(t, k, (x, y), (*chips[k], c)) for t in range(n) for k in range(3)]
        for cp in sends:
            cp.start()
        passed = []
        for t in range(n):
            for k in range(3):
                ici(t, k, chips[k], None).wait_recv()
                passed.append(d2d(t, k, c))
                passed[-1].start()
        for t in range(n):
            for k in range(3):
                d2d(t, k, 1 - c).wait_recv()
        for cp in sends + passed:
            cp.wait_send()
        for cp in own:
            cp.wait()

    return _pc(body, name="gather_weights", in_specs=[HBM_SPEC] * n, out_specs=[HBM_SPEC] * n,
               out_shape=[S((NSH,) + w.shape, w.dtype) for w in ws],
               scratch_shapes=[pltpu.SemaphoreType.DMA((3 * n,))] * 4 + [pltpu.SemaphoreType.DMA((n,))],
               compiler_params=_comm_params())(*ws)


def _swap_halves(gs):
    n = len(gs)

    def body(*refs):
        ins, outs = refs[:n], refs[n:2 * n]
        s_sem, r_sem = refs[2 * n:]
        x, y, c, _ = _place()
        cps = []
        for t in range(n):
            r2 = gs[t].shape[2] // 2
            cps.append(pltpu.make_async_remote_copy(
                src_ref=ins[t].at[:, :, pl.ds((1 - c) * r2, r2), :], dst_ref=outs[t], send_sem=s_sem.at[t],
                recv_sem=r_sem.at[t], device_id=(x, y, 1 - c), device_id_type=MESH))
            cps[-1].start()
        for cp in cps:
            cp.wait_recv()
        for cp in cps:
            cp.wait_send()

    return _pc(body, name="grad_swap_halves", in_specs=[HBM_SPEC] * n, out_specs=[HBM_SPEC] * n,
               out_shape=[S(g.shape[:2] + (g.shape[2] // 2, g.shape[3]), g.dtype) for g in gs],
               scratch_shapes=[pltpu.SemaphoreType.DMA((n,))] * 2, compiler_params=_comm_params())(*gs)


def _add_half(g, landed, c_arr, name):
    _, L, r2, cols = landed.shape

    def body(c_ref, g_ref, l_ref, o_ref):
        o_ref[...] = (g_ref[...].astype(F32) + l_ref[...].astype(F32)).astype(BF16)

    gs = pltpu.PrefetchScalarGridSpec(
        num_scalar_prefetch=1, grid=(NSH, L),
        in_specs=[pl.BlockSpec((None, None, r2, cols), lambda j, l, c: (j, l, c[0], 0)),
                  pl.BlockSpec((None, None, r2, cols), lambda j, l, c: (j, l, 0, 0))],
        out_specs=pl.BlockSpec((None, None, r2, cols), lambda j, l, c: (j, l, 0, 0)))
    return _pc(body, name=name, grid_spec=gs, out_shape=S(landed.shape, BF16), compiler_params=_cp(2))(c_arr, g, landed)


def _send_shards(ps):
    n = len(ps)

    def body(*refs):
        ins, outs = refs[:n], refs[n:2 * n]
        s_sem, r_sem = refs[2 * n:]
        x, y, c, chips = _place()
        cps = []
        for t in range(n):
            for k in range(3):
                cps.append(pltpu.make_async_remote_copy(
                    src_ref=ins[t].at[2 * chips[k][0] + chips[k][1]], dst_ref=outs[t].at[k],
                    send_sem=s_sem.at[3 * t + k], recv_sem=r_sem.at[3 * t + k], device_id=(*chips[k], c),
                    device_id_type=MESH))
                cps[-1].start()
        for cp in cps:
            cp.wait_recv()
        for cp in cps:
            cp.wait_send()

    return _pc(body, name="grad_send_shards", in_specs=[HBM_SPEC] * n, out_specs=[HBM_SPEC] * n,
               out_shape=[S((3,) + p.shape[1:], p.dtype) for p in ps],
               scratch_shapes=[pltpu.SemaphoreType.DMA((3 * n,))] * 2, compiler_params=_comm_params())(*ps)


def _sum_shards(part, landed, mc_arr, name):
    _, L, r2, cols = landed.shape

    def body(mc_ref, p_ref, l_ref, o_ref):
        o_ref[...] = ((p_ref[...].astype(F32) + l_ref[0].astype(F32)) + l_ref[1].astype(F32)) + l_ref[2].astype(F32)

    gs = pltpu.PrefetchScalarGridSpec(
        num_scalar_prefetch=1, grid=(L,),
        in_specs=[pl.BlockSpec((None, None, r2, cols), lambda l, mc: (mc[0], l, 0, 0)),
                  pl.BlockSpec((3, None, r2, cols), lambda l, mc: (0, l, 0, 0))],
        out_specs=pl.BlockSpec((None, r2, cols), lambda l, mc: (l, mc[1], 0)))
    return _pc(body, name=name, grid_spec=gs, out_shape=S((L, 2 * r2, cols), F32),
               compiler_params=_cp(1))(mc_arr, part, landed)


def _share_halves(bufs):
    n = len(bufs)

    def body(*refs):
        ins, outs = refs[:n], refs[n:2 * n]
        s_sem, r_sem = refs[2 * n:]
        x, y, c, _ = _place()
        cps = []
        for t in range(n):
            r2 = bufs[t].shape[1] // 2
            cps.append(pltpu.make_async_remote_copy(
                src_ref=ins[t].at[:, pl.ds(c * r2, r2), :], dst_ref=outs[t].at[:, pl.ds(c * r2, r2), :],
                send_sem=s_sem.at[t], recv_sem=r_sem.at[t], device_id=(x, y, 1 - c), device_id_type=MESH))
            cps[-1].start()
        for cp in cps:
            cp.wait_recv()
        for cp in cps:
            cp.wait_send()

    return _pc(body, name="grad_share_halves", in_specs=[HBM_SPEC] * n, out_specs=[HBM_SPEC] * n,
               out_shape=[S(b.shape, b.dtype) for b in bufs], input_output_aliases={t: t for t in range(n)},
               scratch_shapes=[pltpu.SemaphoreType.DMA((n,))] * 2, compiler_params=_comm_params())(*bufs)


def _gather_small(v):
    nr = v.shape[0]

    def body(v_ref, out_ref, send_sems, recv_sems, local_sem):
        x, y, c, chips = _place()
        me, sibling = (x, y, c), (x, y, 1 - c)

        def rows(px, py, pc):
            return out_ref.at[pl.ds((4 * px + 2 * py + pc) * nr, nr), :]

        def copy(k, block, to, src=None):
            return pltpu.make_async_remote_copy(src_ref=rows(*block) if src is None else src, dst_ref=rows(*block),
                                                send_sem=send_sems.at[k], recv_sem=recv_sems.at[k], device_id=to,
                                                device_id_type=MESH)

        mine = pltpu.make_async_copy(v_ref, rows(*me), local_sem)
        mine.start()
        first = [copy(0, me, sibling, src=v_ref)]
        first += [copy(1 + j, me, (*chip, c), src=v_ref) for j, chip in enumerate(chips)]
        for cp in first:
            cp.start()
        passed = [copy(4 + j, (*chip, c), sibling) for j, chip in enumerate(chips)]
        for j, chip in enumerate(chips):
            copy(1 + j, (*chip, c), me).wait_recv()
            passed[j].start()
        copy(0, sibling, me).wait_recv()
        for j, chip in enumerate(chips):
            copy(4 + j, (*chip, 1 - c), me).wait_recv()
        for cp in first + passed:
            cp.wait_send()
        mine.wait()

    vm = pl.BlockSpec(memory_space=pltpu.VMEM)
    return _pc(body, name="gather_small_grads", in_specs=[vm], out_specs=vm, out_shape=S((8 * nr, 128), F32),
               scratch_shapes=[pltpu.SemaphoreType.DMA((7,)), pltpu.SemaphoreType.DMA((7,)), pltpu.SemaphoreType.DMA],
               compiler_params=_comm_params())(v)


def _adamw_math(w, g, m, v):
    m2 = ADAM_B1 * m + (1.0 - ADAM_B1) * g
    v2 = ADAM_B2 * v + (1.0 - ADAM_B2) * (g * g)
    m_hat = m2 / (1.0 - ADAM_B1 ** ADAM_STEP)
    v_hat = v2 / (1.0 - ADAM_B2 ** ADAM_STEP)
    return -ADAM_LR * (m_hat / (jnp.sqrt(v_hat) + ADAM_EPS) + ADAM_WD * w), m2, v2


def _adamw(w, g, m, v, name):
    L, R, C = w.shape
    rb = R // 2 if R >= 512 else R

    def body(w_ref, g_ref, m_ref, v_ref, d_ref, m2_ref, v2_ref):
        d_ref[...], m2_ref[...], v2_ref[...] = _adamw_math(w_ref[...], g_ref[...], m_ref[...], v_ref[...])

    blk = pl.BlockSpec((None, rb, C), lambda l, r: (l, r, 0))
    return _pc(body, name=name, grid=(L, R // rb), in_specs=[blk] * 4, out_specs=[blk] * 3,
               out_shape=[S(w.shape, F32)] * 3, compiler_params=_cp(2))(w, g, m, v)


def _adamw_small(gathered, w, m, v):
    nr = w.shape[0]
    rb = nr // 5

    def body(a_ref, w_ref, m_ref, v_ref, g_ref, d_ref, m2_ref, v2_ref):
        g = a_ref[0]
        for k in range(1, 8):
            g = g + a_ref[k]
        g_ref[...] = g
        d_ref[...], m2_ref[...], v2_ref[...] = _adamw_math(w_ref[...], g, m_ref[...], v_ref[...])

    blk = pl.BlockSpec((rb, 128), lambda i: (i, 0))
    return _pc(body, name="adamw_small", grid=(nr // rb,), in_specs=[pl.BlockSpec((8, rb, 128), lambda i: (0, i, 0))] + [blk] * 3,
               out_specs=[blk] * 4, out_shape=[S((nr, 128), F32)] * 4, compiler_params=_cp(1))(gathered, w, m, v)


SMALL_ROWS = 4520


def _pack(arrs):
    flat = jnp.concatenate([a.reshape(-1) for a in arrs])
    return jnp.pad(flat, (0, SMALL_ROWS * 128 - flat.shape[0])).reshape(SMALL_ROWS, 128)


def _unpack(packed, like):
    flat = packed.reshape(-1)
    out, off = [], 0
    for a in like:
        out.append(flat[off:off + a.size].reshape(a.shape))
        off += a.size
    return out


def kernel(x, p, positions, ffn1_pre_g, ffn1_w_gate, ffn1_w_up, ffn1_w_down, ffn1_post_g, mix_pre_g, w_in, attn_norm_g, ssm_lam_re, ssm_lam_im, ssm_log_dt, ssm_b_re, ssm_b_im, ssm_c_re, ssm_c_im, ssm_d, ssm_w_glu, ssm_b_glu, ssm_norm_g, w_out, mix_post_g, ffn2_pre_g, ffn2_w_gate, ffn2_w_up, ffn2_w_down, ffn2_post_g, ple_w_up, ple_w_gate, ple_post_g, loss_target, m_ffn1_pre_g, m_ffn1_w_gate, m_ffn1_w_up, m_ffn1_w_down, m_ffn1_post_g, m_mix_pre_g, m_w_in, m_attn_norm_g, m_ssm_lam_re, m_ssm_lam_im, m_ssm_log_dt, m_ssm_b_re, m_ssm_b_im, m_ssm_c_re, m_ssm_c_im, m_ssm_d, m_ssm_w_glu, m_ssm_b_glu, m_ssm_norm_g, m_w_out, m_mix_post_g, m_ffn2_pre_g, m_ffn2_w_gate, m_ffn2_w_up, m_ffn2_w_down, m_ffn2_post_g, m_ple_w_up, m_ple_w_gate, m_ple_post_g, v_ffn1_pre_g, v_ffn1_w_gate, v_ffn1_w_up, v_ffn1_w_down, v_ffn1_post_g, v_mix_pre_g, v_w_in, v_attn_norm_g, v_ssm_lam_re, v_ssm_lam_im, v_ssm_log_dt, v_ssm_b_re, v_ssm_b_im, v_ssm_c_re, v_ssm_c_im, v_ssm_d, v_ssm_w_glu, v_ssm_b_glu, v_ssm_norm_g, v_w_out, v_mix_post_g, v_ffn2_pre_g, v_ffn2_w_gate, v_ffn2_w_up, v_ffn2_w_down, v_ffn2_post_g, v_ple_w_up, v_ple_w_gate, v_ple_post_g):
    a = dict(locals())
    T = x.shape[1]
    big_names = [n for n, _, _ in BIG]
    for n in TRANSPOSED:
        for pre in ("", "m_", "v_"):
            a[pre + n] = jnp.swapaxes(a[pre + n], 1, 2)

    gathered = _gather_weights([a[n].astype(BF16) for n in big_names])
    W = dict(zip(big_names, gathered))
    Sm = {n: a[n] for n in SMALL}

    loss, gx, G, small = _local_step(x[0], p[:, 0], positions.reshape(T, 1).astype(F32), loss_target[0], W, Sm)

    c_arr = lax.axis_index("c").astype(jnp.int32).reshape(1)
    mc_arr = jnp.stack([2 * lax.axis_index("x") + lax.axis_index("y"), lax.axis_index("c")]).astype(jnp.int32)
    gs = [G[n] for n in big_names]
    landed = _swap_halves(gs)
    parts = [_add_half(g, la, c_arr, f"grad_add_half_{n}") for g, la, n in zip(gs, landed, big_names)]
    landed = _send_shards(parts)
    halves = [_sum_shards(pt, la, mc_arr, f"grad_sum_shards_{n}") for pt, la, n in zip(parts, landed, big_names)]
    grads = dict(zip(big_names, _share_halves(halves)))

    small_g = _gather_small(_pack([small[n] for n in SMALL])).reshape(8, SMALL_ROWS, 128)
    sg, sd, sm, sv = _adamw_small(small_g, _pack([a[n] for n in SMALL]), _pack([a["m_" + n] for n in SMALL]),
                                  _pack([a["v_" + n] for n in SMALL]))
    like = [a[n] for n in SMALL]
    res = {}
    for n, g_, d_, m_, v_ in zip(SMALL, _unpack(sg, like), _unpack(sd, like), _unpack(sm, like), _unpack(sv, like)):
        res[n] = (g_, d_, m_, v_)
    for n in big_names:
        d_, m_, v_ = _adamw(a[n], grads[n], a["m_" + n], a["v_" + n], f"adamw_{n}")
        res[n] = (grads[n], d_, m_, v_)
        if n in TRANSPOSED:
            res[n] = tuple(jnp.swapaxes(t, 1, 2) for t in res[n])

    total = lax.psum(loss[0, 0], ("x", "y", "c"))
    return (total, gx[None], *[res[n][0] for n in WEIGHTS], *[res[n][1] for n in WEIGHTS],
            *[res[n][2] for n in WEIGHTS], *[res[n][3] for n in WEIGHTS])
```

```python
import functools
import math

import numpy as np
import jax
import jax.numpy as jnp
from jax import lax
from jax.experimental import pallas as pl
from jax.experimental.pallas import tpu as pltpu

F32 = jnp.float32
BF16 = jnp.bfloat16
S = jax.ShapeDtypeStruct
MESH = pl.DeviceIdType.MESH

D = 1024
DA = 512
DSS = 512
HD = 64
NH = 8
BAND = 128
NSH = 4
DFS = 704
PLE = 256
EPS = 1e-6
ROPE_THETA = 500000.0
PATTERN_DILATIONS = (1, 4, 16)
NLB = 16
ADAM_LR, ADAM_B1, ADAM_B2, ADAM_EPS, ADAM_WD, ADAM_STEP = 0.001, 0.9, 0.999, 1e-08, 0.01, 10

VMEM_LIMIT = 56 * 1024 * 1024
TM = 512
TMB = 256

BIG = (
    ("ffn1_w_gate", DFS, D), ("ffn1_w_up", DFS, D), ("ffn1_w_down", DFS, D),
    ("w_in", D, 512), ("ssm_w_glu", 128, 512), ("w_out", 256, D),
    ("ffn2_w_gate", DFS, D), ("ffn2_w_up", DFS, D), ("ffn2_w_down", DFS, D),
    ("ple_w_up", PLE, 256), ("ple_w_gate", 256, D),
)
TRANSPOSED = ("ffn1_w_gate", "ffn1_w_up", "ffn2_w_gate", "ffn2_w_up")
SMALL = ("ffn1_pre_g", "ffn1_post_g", "mix_pre_g", "attn_norm_g", "ssm_lam_re", "ssm_lam_im", "ssm_log_dt",
         "ssm_b_re", "ssm_b_im", "ssm_c_re", "ssm_c_im", "ssm_d", "ssm_b_glu", "ssm_norm_g", "mix_post_g",
         "ffn2_pre_g", "ffn2_post_g", "ple_post_g")
WEIGHTS = ("ffn1_pre_g", "ffn1_w_gate", "ffn1_w_up", "ffn1_w_down", "ffn1_post_g", "mix_pre_g", "w_in", "attn_norm_g",
           "ssm_lam_re", "ssm_lam_im", "ssm_log_dt", "ssm_b_re", "ssm_b_im", "ssm_c_re", "ssm_c_im", "ssm_d",
           "ssm_w_glu", "ssm_b_glu", "ssm_norm_g", "w_out", "mix_post_g", "ffn2_pre_g", "ffn2_w_gate", "ffn2_w_up",
           "ffn2_w_down", "ffn2_post_g", "ple_w_up", "ple_w_gate", "ple_post_g")


def _pc(body, **kw):
    return pl.pallas_call(body, **kw)


def _cp(n_grid):
    return pltpu.CompilerParams(dimension_semantics=("arbitrary",) * n_grid, vmem_limit_bytes=VMEM_LIMIT)


def _dot(a, b):
    return jnp.dot(a, b, preferred_element_type=F32)


def _dot_nt(a, b):
    return lax.dot_general(a, b, (((1,), (1,)), ((), ())), preferred_element_type=F32)


def _dot_tn(a, b):
    return lax.dot_general(a, b, (((0,), (0,)), ((), ())), preferred_element_type=F32)


def _split(a):
    hi = a.astype(BF16)
    return hi, (a - hi.astype(F32)).astype(BF16)


def _dot3(fn, a, b):
    ah, al = _split(a)
    bh, bl = _split(b)
    return fn(ah, bh) + fn(ah, bl) + fn(al, bh)


def _rms_fwd(x, g):
    r = lax.rsqrt(jnp.mean(x * x, axis=-1, keepdims=True) + EPS)
    return x * r * g


def _rms_bwd(dy, x, g):
    r = lax.rsqrt(jnp.mean(x * x, axis=-1, keepdims=True) + EPS)
    xr = x * r
    gd = dy * g
    dx = r * (gd - xr * jnp.mean(gd * xr, axis=-1, keepdims=True))
    dg = jnp.sum(dy * xr, axis=0, keepdims=True)
    return dx, dg


def _gelu(y):
    k = math.sqrt(2.0 / math.pi)
    return 0.5 * y * (1.0 + jnp.tanh(k * (y + 0.044715 * y * y * y)))


def _gelu_grad(y):
    k = math.sqrt(2.0 / math.pi)
    t = jnp.tanh(k * (y + 0.044715 * y * y * y))
    return 0.5 * (1.0 + t) + 0.5 * y * (1.0 - t * t) * k * (1.0 + 3 * 0.044715 * y * y)


def _gain_spec(n, layer):
    return pl.BlockSpec((None, 1, n), lambda *_: (layer, 0, 0))


def _row_acc_spec(n):
    return pl.BlockSpec((1, n), lambda *_: (0, 0))


def _rot_tables(pos_col):
    T = pos_col.shape[0]
    half = HD // 8
    inv = (ROPE_THETA ** (-np.arange(half, dtype=np.float32) * (2.0 / (2 * half)))).astype(np.float32)
    lane_freq = np.tile(np.concatenate([inv, inv, np.zeros(HD - 2 * half, np.float32)]), NH)[None, :]

    def body(p_ref, f_ref, c_ref, s1_ref, s2_ref):
        ang = p_ref[...] * f_ref[...]
        d = lax.broadcasted_iota(jnp.int32, ang.shape, 1) % HD
        cs = jnp.cos(ang)
        sn = jnp.sin(ang)
        c_ref[...] = jnp.where(d < 2 * half, cs, 1.0)
        s1_ref[...] = jnp.where(d < half, -sn, 0.0)
        s2_ref[...] = jnp.where((d >= half) & (d < 2 * half), sn, 0.0)

    tm = TM
    return _pc(body, name="rot_tables", grid=(T // tm,),
               in_specs=[pl.BlockSpec((tm, 1), lambda i: (i, 0)), pl.BlockSpec((1, DA), lambda i: (0, 0))],
               out_specs=[pl.BlockSpec((tm, DA), lambda i: (i, 0))] * 3,
               out_shape=[S((T, DA), F32)] * 3, compiler_params=_cp(1))(pos_col, jnp.asarray(lane_freq))


def _rot_fwd(t, c, s1, s2):
    return t * c + pltpu.roll(t, DA - 8, 1) * s1 + pltpu.roll(t, 8, 1) * s2


def _rot_bwd(g, c, s1, s2):
    return g * c + pltpu.roll(g * s1, 8, 1) + pltpu.roll(g * s2, DA - 8, 1)


def _ffn_fwd(h, pre_g, post_g, wg, wu, wd, layer, tag):
    T = h.shape[0]
    tm = TM
    nt = T // tm

    def body(h_ref, pg_ref, qg_ref, wg_ref, wu_ref, wd_ref, ho_ref, a_ref, b_ref, f_ref, xn_ref, xs, facc):
        j = pl.program_id(1)

        @pl.when(j == 0)
        def _():
            xb = _rms_fwd(h_ref[...], pg_ref[...]).astype(BF16)
            xs[...] = xb
            xn_ref[...] = xb
            facc[...] = jnp.zeros_like(facc)

        xb = xs[...]
        ab = _dot_nt(xb, wg_ref[...]).astype(BF16)
        bb = _dot_nt(xb, wu_ref[...]).astype(BF16)
        a_ref[...] = ab
        b_ref[...] = bb
        a = ab.astype(F32)
        hh = (a * jax.nn.sigmoid(a) * bb.astype(F32)).astype(BF16)
        facc[...] += _dot(hh, wd_ref[...])

        @pl.when(j == NSH - 1)
        def _():
            f = facc[...]
            f_ref[...] = f
            ho_ref[...] = h_ref[...] + 0.5 * _rms_fwd(f, qg_ref[...])

    row = pl.BlockSpec((tm, D), lambda i, j: (i, 0))
    act = pl.BlockSpec((None, tm, DFS), lambda i, j: (j, i, 0))
    wrow = pl.BlockSpec((None, None, DFS, D), lambda i, j: (j, layer, 0, 0))
    return _pc(body, name=f"ffn_fwd_{tag}_l{layer}", grid=(nt, NSH),
               in_specs=[row, _gain_spec(D, layer), _gain_spec(D, layer), wrow, wrow, wrow],
               out_specs=[row, act, act, row, row],
               out_shape=[S((T, D), F32), S((NSH, T, DFS), BF16), S((NSH, T, DFS), BF16), S((T, D), F32), S((T, D), BF16)],
               scratch_shapes=[pltpu.VMEM((tm, D), BF16), pltpu.VMEM((tm, D), F32)],
               compiler_params=_cp(2))(h, pre_g, post_g, wg, wu, wd)


def _ffn_bwd(dout, h, f, a, b, pre_g, post_g, wg, wu, wd, layer, tag):
    T = h.shape[0]
    tm = TM
    nt = T // tm

    def body(do_ref, h_ref, f_ref, a_ref, b_ref, pg_ref, qg_ref, wg_ref, wu_ref, wd_ref,
             dh_ref, df_ref, da_ref, db_ref, hh_ref, dpg_ref, dqg_ref, dfs, dxn):
        i = pl.program_id(0)
        j = pl.program_id(1)

        @pl.when((i == 0) & (j == 0))
        def _():
            dpg_ref[...] = jnp.zeros_like(dpg_ref)
            dqg_ref[...] = jnp.zeros_like(dqg_ref)

        @pl.when(j == 0)
        def _():
            df, dq = _rms_bwd(0.5 * do_ref[...], f_ref[...], qg_ref[...])
            dqg_ref[...] += dq
            dfb = df.astype(BF16)
            dfs[...] = dfb
            df_ref[...] = dfb
            dxn[...] = jnp.zeros_like(dxn)

        dhh = _dot_nt(dfs[...], wd_ref[...])
        av = a_ref[...].astype(F32)
        bv = b_ref[...].astype(F32)
        sg = jax.nn.sigmoid(av)
        sa = av * sg
        hh_ref[...] = (sa * bv).astype(BF16)
        dab = (dhh * bv * (sg * (1.0 + av * (1.0 - sg)))).astype(BF16)
        dbb = (dhh * sa).astype(BF16)
        da_ref[...] = dab
        db_ref[...] = dbb
        dxn[...] += _dot(dab, wg_ref[...]) + _dot(dbb, wu_ref[...])

        @pl.when(j == NSH - 1)
        def _():
            dx, dp = _rms_bwd(dxn[...], h_ref[...], pg_ref[...])
            dpg_ref[...] += dp
            dh_ref[...] = do_ref[...] + dx

    row = pl.BlockSpec((tm, D), lambda i, j: (i, 0))
    act = pl.BlockSpec((None, tm, DFS), lambda i, j: (j, i, 0))
    wrow = pl.BlockSpec((None, None, DFS, D), lambda i, j: (j, layer, 0, 0))
    return _pc(body, name=f"ffn_bwd_{tag}_l{layer}", grid=(nt, NSH),
               in_specs=[row, row, row, act, act, _gain_spec(D, layer), _gain_spec(D, layer), wrow, wrow, wrow],
               out_specs=[row, row, act, act, act, _row_acc_spec(D), _row_acc_spec(D)],
               out_shape=[S((T, D), F32), S((T, D), BF16), S((NSH, T, DFS), BF16), S((NSH, T, DFS), BF16),
                          S((NSH, T, DFS), BF16), S((1, D), F32), S((1, D), F32)],
               scratch_shapes=[pltpu.VMEM((tm, D), BF16), pltpu.VMEM((tm, D), F32)],
               compiler_params=_cp(2))(dout, h, f, a, b, pre_g, post_g, wg, wu, wd)


def _dw(A, B, buf, layer, kb, nb, a_idx, b_idx, name):
    T = A.shape[1]
    tt = 2 * TM if T % (2 * TM) == 0 else TM
    nt = T // tt

    def body(a_ref, b_ref, buf_ref, o_ref, acc):
        t = pl.program_id(1)

        @pl.when(t == 0)
        def _():
            acc[...] = jnp.zeros_like(acc)

        acc[...] += _dot_tn(a_ref[...].astype(BF16), b_ref[...].astype(BF16))

        @pl.when(t == nt - 1)
        def _():
            o_ref[...] = acc[...].astype(o_ref.dtype)

    return _pc(body, name=name, grid=(NSH, nt),
               in_specs=[pl.BlockSpec((None, tt, kb), lambda j, t: (a_idx(j)[0], t, a_idx(j)[1])),
                         pl.BlockSpec((None, tt, nb), lambda j, t: (b_idx(j)[0], t, b_idx(j)[1])),
                         pl.BlockSpec(memory_space=pl.ANY)],
               out_specs=pl.BlockSpec((None, None, kb, nb), lambda j, t: (j, layer, 0, 0)),
               out_shape=S(buf.shape, buf.dtype), input_output_aliases={2: 0},
               scratch_shapes=[pltpu.VMEM((kb, nb), F32)], compiler_params=_cp(2))(A, B, buf)


def _mix_proj(h, pre_g, win, rot, layer):
    T = h.shape[0]
    tm = TM

    def body(h_ref, g_ref, w_ref, c_ref, s1_ref, s2_ref, p_ref, xn_ref, xs):
        j = pl.program_id(1)

        @pl.when(j == 0)
        def _():
            xb = _rms_fwd(h_ref[...], g_ref[...]).astype(BF16)
            xs[...] = xb
            xn_ref[...] = xb

        o = _dot(xs[...], w_ref[...])

        @pl.when(j < 2)
        def _():
            p_ref[...] = _rot_fwd(o, c_ref[...], s1_ref[...], s2_ref[...])

        @pl.when(j >= 2)
        def _():
            p_ref[...] = o

    row = pl.BlockSpec((tm, D), lambda i, j: (i, 0))
    half = pl.BlockSpec((tm, DA), lambda i, j: (i, 0))
    return _pc(body, name=f"mix_proj_l{layer}", grid=(T // tm, NSH),
               in_specs=[row, _gain_spec(D, layer), pl.BlockSpec((None, None, D, DA), lambda i, j: (j, layer, 0, 0)),
                         half, half, half],
               out_specs=[pl.BlockSpec((None, tm, DA), lambda i, j: (j, i, 0)), row],
               out_shape=[S((NSH, T, DA), F32), S((T, D), BF16)],
               scratch_shapes=[pltpu.VMEM((tm, D), BF16)], compiler_params=_cp(2))(h, pre_g, win, *rot)


def _mix_proj_bwd(dq, dk, dv, du, dh_up, h, pre_g, win, rot, layer):
    T = h.shape[0]
    tm = TM

    def body(dq_ref, dk_ref, dv_ref, du_ref, up_ref, h_ref, g_ref, w_ref, c_ref, s1_ref, s2_ref,
             dh_ref, dp_ref, dg_ref, dps, dxn):
        i = pl.program_id(0)
        j = pl.program_id(1)

        @pl.when((i == 0) & (j == 0))
        def _():
            dg_ref[...] = jnp.zeros_like(dg_ref)

        @pl.when(j == 0)
        def _():
            dxn[...] = jnp.zeros_like(dxn)
            dps[...] = _rot_bwd(dq_ref[...], c_ref[...], s1_ref[...], s2_ref[...]).astype(BF16)

        @pl.when(j == 1)
        def _():
            dps[...] = _rot_bwd(dk_ref[...], c_ref[...], s1_ref[...], s2_ref[...]).astype(BF16)

        @pl.when(j == 2)
        def _():
            dps[...] = dv_ref[...].astype(BF16)

        @pl.when(j == 3)
        def _():
            dps[...] = du_ref[...].astype(BF16)

        dpb = dps[...]
        dp_ref[...] = dpb
        dxn[...] += _dot_nt(dpb, w_ref[...])

        @pl.when(j == NSH - 1)
        def _():
            dx, dg = _rms_bwd(dxn[...], h_ref[...], g_ref[...])
            dg_ref[...] += dg
            dh_ref[...] = up_ref[...] + dx

    row = pl.BlockSpec((tm, D), lambda i, j: (i, 0))
    half = pl.BlockSpec((tm, DA), lambda i, j: (i, 0))
    return _pc(body, name=f"mix_proj_bwd_l{layer}", grid=(T // tm, NSH),
               in_specs=[half, half, half, half, row, row, _gain_spec(D, layer),
                         pl.BlockSpec((None, None, D, DA), lambda i, j: (j, layer, 0, 0)), half, half, half],
               out_specs=[row, pl.BlockSpec((None, tm, DA), lambda i, j: (j, i, 0)), _row_acc_spec(D)],
               out_shape=[S((T, D), F32), S((NSH, T, DA), BF16), S((1, D), F32)],
               scratch_shapes=[pltpu.VMEM((tm, DA), BF16), pltpu.VMEM((tm, D), F32)],
               compiler_params=_cp(2))(dq, dk, dv, du, dh_up, h, pre_g, win, *rot)


def _stream_pos(d, axis):
    i = lax.broadcasted_iota(jnp.int32, (BAND, BAND), axis)
    if d == 16:
        return i
    if d == 4:
        return 4 * (i % 32) + i // 32
    return 16 * (i % 8) + i // 8


def _band_masks(b, d):
    qi, kj = _stream_pos(d, 0), _stream_pos(d, 1)
    return kj <= qi, (kj >= qi) & (b > 0)


def _pattern(d, T):
    n16 = T // 16
    if d == 16:
        return (16, n16, DA), (None, BAND, DA), lambda r, k: (r, k, 0)
    if d == 4:
        return (4, 4, n16, DA), (4, None, 32, DA), lambda r, k: (0, r, k, 0)
    return (16, n16, DA), (16, 8, DA), lambda r, k: (0, k, 0)


def _pattern_spec(d, T, kmap, lead=None):
    _, blk, idx = _pattern(d, T)
    if lead is None:
        return pl.BlockSpec(blk, lambda r, b: idx(r, kmap(b)))
    return pl.BlockSpec((None,) + blk, lambda r, b: (lead,) + idx(r, kmap(b)))


def _attn_fwd(P, d, layer):
    T = P.shape[1]
    nb = T // d // BAND
    vshape = _pattern(d, T)[0]
    Pv = P.reshape((NSH,) + vshape)
    scale = HD ** -0.5

    def body(q_ref, kp_ref, kc_ref, vp_ref, vc_ref, o_ref, l_ref, qs, kps, kcs, vps, vcs, osc, lsc):
        b = pl.program_id(1)
        for src, dst in ((q_ref, qs), (kp_ref, kps), (kc_ref, kcs), (vp_ref, vps), (vc_ref, vcs)):
            dst[...] = src[...].reshape(BAND, DA).astype(BF16)
        mask_c, mask_p = _band_masks(b, d)
        for hd in range(NH):
            sl = slice(hd * HD, (hd + 1) * HD)
            q = qs[:, sl]
            sc = jnp.where(mask_c, _dot_nt(q, kcs[:, sl]) * scale, -1e30)
            sp = jnp.where(mask_p, _dot_nt(q, kps[:, sl]) * scale, -1e30)
            m = jnp.maximum(jnp.max(sc, axis=-1, keepdims=True), jnp.max(sp, axis=-1, keepdims=True))
            ec = jnp.exp(sc - m)
            ep = jnp.exp(sp - m)
            den = jnp.sum(ec, axis=-1, keepdims=True) + jnp.sum(ep, axis=-1, keepdims=True)
            o = _dot(ec.astype(BF16), vcs[:, sl]) + _dot(ep.astype(BF16), vps[:, sl])
            osc[:, sl] = o / den
            lsc[:, sl] = jnp.broadcast_to(m + jnp.log(den), (BAND, HD))
        o_ref[...] = osc[...].reshape(o_ref.shape)
        l_ref[...] = lsc[...].reshape(l_ref.shape)

    cur = lambda b: b
    prev = lambda b: jnp.maximum(b - 1, 0)
    out = _pattern_spec(d, T, cur)
    o, l = _pc(body, name=f"attn_fwd_d{d}_l{layer}", grid=(d, nb),
               in_specs=[_pattern_spec(d, T, cur, 0), _pattern_spec(d, T, prev, 1), _pattern_spec(d, T, cur, 1),
                         _pattern_spec(d, T, prev, 2), _pattern_spec(d, T, cur, 2)],
               out_specs=[out, out], out_shape=[S(vshape, F32)] * 2,
               scratch_shapes=[pltpu.VMEM((BAND, DA), BF16)] * 5 + [pltpu.VMEM((BAND, DA), F32)] * 2,
               compiler_params=_cp(2))(Pv, Pv, Pv, Pv, Pv)
    return o.reshape(T, DA), l.reshape(T, DA)


def _attn_bwd(P, dO, lse, delta, acc, d, layer):
    T = P.shape[1]
    nb = T // d // BAND
    vshape = _pattern(d, T)[0]
    Pv = P.reshape((NSH,) + vshape)
    scale = HD ** -0.5
    first = acc is None

    def body(*refs):
        q_ref, kp_ref, kc_ref, vp_ref, vc_ref, do_ref, l_ref, dl_ref = refs[:8]
        if first:
            dq_ref, dk_ref, dv_ref = refs[8:11]
        else:
            aq_ref, ak_ref, av_ref, dq_ref, dk_ref, dv_ref = refs[8:14]
        qs, kps, kcs, vps, vcs, dos, ls, dls, oq, ok, ov, ck, cv = refs[-13:]
        b = pl.program_id(1)
        flat = lambda ref: ref[...].reshape(BAND, DA)

        @pl.when(b == 0)
        def _():
            ck[...] = jnp.zeros_like(ck)
            cv[...] = jnp.zeros_like(cv)

        @pl.when(b < nb)
        def _():
            for src, dst in ((q_ref, qs), (kp_ref, kps), (kc_ref, kcs), (vp_ref, vps), (vc_ref, vcs), (do_ref, dos)):
                dst[...] = flat(src).astype(BF16)
            ls[...] = flat(l_ref)
            dls[...] = flat(dl_ref)
            mask_c, mask_p = _band_masks(b, d)
            for hd in range(NH):
                sl = slice(hd * HD, (hd + 1) * HD)
                one = slice(hd * HD, hd * HD + 1)
                q, kc, kp, do = qs[:, sl], kcs[:, sl], kps[:, sl], dos[:, sl]
                lrow = ls[:, one]
                drow = dls[:, one]
                pc = jnp.where(mask_c, jnp.exp(_dot_nt(q, kc) * scale - lrow), 0.0)
                pp = jnp.where(mask_p, jnp.exp(_dot_nt(q, kp) * scale - lrow), 0.0)
                dsc = (pc * (_dot_nt(do, vcs[:, sl]) - drow) * scale).astype(BF16)
                dsp = (pp * (_dot_nt(do, vps[:, sl]) - drow) * scale).astype(BF16)
                oq[:, sl] = _dot(dsc, kc) + _dot(dsp, kp)
                ok[:, sl] = ck[:, sl] + _dot_tn(dsp, q)
                ov[:, sl] = cv[:, sl] + _dot_tn(pp.astype(BF16), do)
                ck[:, sl] = _dot_tn(dsc, q)
                cv[:, sl] = _dot_tn(pc.astype(BF16), do)
            if first:
                dq_ref[...] = oq[...].reshape(dq_ref.shape)
                dk_ref[...] = ok[...].reshape(dk_ref.shape)
                dv_ref[...] = ov[...].reshape(dv_ref.shape)
            else:
                dq_ref[...] = aq_ref[...] + oq[...].reshape(dq_ref.shape)
                dk_ref[...] = ak_ref[...] + ok[...].reshape(dk_ref.shape)
                dv_ref[...] = av_ref[...] + ov[...].reshape(dv_ref.shape)

        @pl.when(b == nb)
        def _():
            if first:
                dk_ref[...] = ck[...].reshape(dk_ref.shape)
                dv_ref[...] = cv[...].reshape(dv_ref.shape)
            else:
                dk_ref[...] = ak_ref[...] + ck[...].reshape(dk_ref.shape)
                dv_ref[...] = av_ref[...] + cv[...].reshape(dv_ref.shape)

    qb = lambda b: jnp.minimum(b, nb - 1)
    qprev = lambda b: jnp.maximum(qb(b) - 1, 0)
    kb = lambda b: jnp.maximum(b - 1, 0)
    qrow = _pattern_spec(d, T, qb)
    krow = _pattern_spec(d, T, kb)
    view = lambda t: t.reshape(vshape)
    ins = [Pv, Pv, Pv, Pv, Pv, view(dO), view(lse), view(delta)]
    specs = [_pattern_spec(d, T, qb, 0), _pattern_spec(d, T, qprev, 1), _pattern_spec(d, T, qb, 1),
             _pattern_spec(d, T, qprev, 2), _pattern_spec(d, T, qb, 2), qrow, qrow, qrow]
    if not first:
        ins += [view(t) for t in acc]
        specs += [qrow, krow, krow]
    dq, dk, dv = _pc(body, name=f"attn_bwd_d{d}_l{layer}", grid=(d, nb + 1), in_specs=specs,
                     out_specs=[qrow, krow, krow], out_shape=[S(vshape, F32)] * 3,
                     scratch_shapes=[pltpu.VMEM((BAND, DA), BF16)] * 6 + [pltpu.VMEM((BAND, DA), F32)] * 7,
                     compiler_params=_cp(2))(*ins)
    return dq.reshape(T, DA), dk.reshape(T, DA), dv.reshape(T, DA)


def _ssm_prep(lam_re, lam_im, log_dt, b_re, b_im, c_re, c_im):
    dt = jnp.exp(log_dt)[:, None]
    er = jnp.exp(lam_re * dt)
    a_re = er * jnp.cos(lam_im * dt)
    a_im = er * jnp.sin(lam_im * dt)
    nr, ni = a_re - 1.0, a_im
    den = lam_re * lam_re + lam_im * lam_im
    cr = (nr * lam_re + ni * lam_im) / den
    ci = (ni * lam_re - nr * lam_im) / den
    bbr = cr[..., None] * b_re - ci[..., None] * b_im
    bbi = cr[..., None] * b_im + ci[..., None] * b_re
    eye = jnp.eye(8, dtype=F32)

    def bblock(bb):
        t = bb.reshape(4, 8, 64, 16).transpose(0, 1, 3, 2)
        return (t[:, :, :, None, :] * eye[None, :, None, :, None]).reshape(4, 128, 512)

    def cblock(cc):
        t = cc.reshape(4, 8, 16, 64).transpose(0, 1, 3, 2)
        return (t[:, :, :, None, :] * eye[None, :, None, :, None]).reshape(4, 512, 128)

    return (a_re.reshape(NLB, 1, 128), a_im.reshape(NLB, 1, 128), bblock(bbr), bblock(bbi), cblock(c_re), cblock(c_im))


def _perm_matrix(tm):
    n = tm // 16
    pm = np.zeros((tm, tm), np.float32)
    for r in range(16):
        pm[16 * np.arange(n) + r, r * n + np.arange(n)] = 1.0
    return jnp.asarray(pm, BF16)


def _pieces(x):
    p1 = x.astype(BF16)
    r1 = x - p1.astype(F32)
    p2 = r1.astype(BF16)
    return p1, p2, (r1 - p2.astype(F32)).astype(BF16)


def _to_time(x, pm):
    return sum(_dot(pm, p) for p in _pieces(x))


def _to_streams(x, pm):
    return sum(_dot_tn(pm, p) for p in _pieces(x))


def _stream_block(tm, cols, lead=None):
    if lead is None:
        return pl.BlockSpec((16, tm // 16, cols), lambda i: (0, i, 0))
    return pl.BlockSpec((None, 16, tm // 16, cols), lambda i: (lead, 0, i, 0))


def _ssm_in(P, bre, bim, layer):
    T = P.shape[1]
    tm = TM

    def body(u_ref, pm_ref, br_ref, bi_ref, un_ref, or_ref, oi_ref):
        u = _to_time(u_ref[...].reshape(tm, DSS), pm_ref[...])
        un_ref[...] = u
        for s in range(4):
            uc = u[:, s * 128:(s + 1) * 128]
            r = _dot3(_dot, uc, br_ref[s])
            m = _dot3(_dot, uc, bi_ref[s])
            for q in range(4):
                or_ref[4 * s + q] = r[:, q * 128:(q + 1) * 128]
                oi_ref[4 * s + q] = m[:, q * 128:(q + 1) * 128]

    whole = pl.BlockSpec((4, 128, 512), lambda i: (0, 0, 0))
    st = pl.BlockSpec((NLB, tm, 128), lambda i: (0, i, 0))
    return _pc(body, name=f"ssm_in_l{layer}", grid=(T // tm,),
               in_specs=[_stream_block(tm, DSS, 3), pl.BlockSpec((tm, tm), lambda i: (0, 0)), whole, whole],
               out_specs=[pl.BlockSpec((tm, DSS), lambda i: (i, 0)), st, st],
               out_shape=[S((T, DSS), F32)] + [S((NLB, T, 128), F32)] * 2,
               compiler_params=_cp(1))(P.reshape(NSH, 16, T // 16, DSS), _perm_matrix(tm), bre, bim)


def _scan(br, bi, a_re, a_im, reverse, layer):
    T = br.shape[1]
    nbk = 2
    tt = min(T, 1024)
    nT = T // tt
    ntile = tt // 8
    sgn = -1.0 if reverse else 1.0
    last = 0 if reverse else 7

    def body(br_ref, bi_ref, ar_ref, ai_ref, xr_ref, xi_ref, cr, ci):
        @pl.when(pl.program_id(1) == 0)
        def _():
            cr[...] = jnp.zeros_like(cr)
            ci[...] = jnp.zeros_like(ci)

        row = lax.broadcasted_iota(jnp.int32, (8, 128), 0)
        consts = []
        for k in range(nbk):
            a1r = jnp.broadcast_to(ar_ref[k], (8, 128))
            a1i = sgn * jnp.broadcast_to(ai_ref[k], (8, 128))
            pows = [(a1r, a1i)]
            for _ in range(7):
                pr, pi_ = pows[-1]
                pows.append((a1r * pr - a1i * pi_, a1r * pi_ + a1i * pr))
            rounds = []
            for s in (1, 2, 4):
                inside = (row <= 7 - s) if reverse else (row >= s)
                rounds.append((jnp.where(inside, pows[s - 1][0], 0.0), jnp.where(inside, pows[s - 1][1], 0.0)))
            cmr, cmi = jnp.zeros((8, 128), F32), jnp.zeros((8, 128), F32)
            for r in range(8):
                e = (7 - r) if reverse else r
                cmr = jnp.where(row == r, pows[e][0], cmr)
                cmi = jnp.where(row == r, pows[e][1], cmi)
            consts.append((rounds, cmr, cmi))

        def tile(i, carry):
            j = (ntile - 1 - i) if reverse else i
            rows = pl.ds(pl.multiple_of(j * 8, 8), 8)
            out = []
            for k in range(nbk):
                rounds, cmr, cmi = consts[k]
                xr = br_ref[k, rows, :]
                xi = bi_ref[k, rows, :]
                for (mr, mi), s in zip(rounds, (1, 2, 4)):
                    sh = (8 - s) if reverse else s
                    rr = pltpu.roll(xr, sh, 0)
                    ri = pltpu.roll(xi, sh, 0)
                    xr, xi = xr + (mr * rr - mi * ri), xi + (mr * ri + mi * rr)
                c_r, c_i = carry[k]
                xr, xi = xr + (cmr * c_r - cmi * c_i), xi + (cmr * c_i + cmi * c_r)
                xr_ref[k, rows, :] = xr
                xi_ref[k, rows, :] = xi
                out.append((jnp.broadcast_to(xr[last:last + 1, :], (8, 128)),
                            jnp.broadcast_to(xi[last:last + 1, :], (8, 128))))
            return tuple(out)

        carry = lax.fori_loop(0, ntile, tile, tuple((cr[k], ci[k]) for k in range(nbk)), unroll=2)
        for k in range(nbk):
            cr[k] = carry[k][0]
            ci[k] = carry[k][1]

    tmap = (lambda t: nT - 1 - t) if reverse else (lambda t: t)
    st = pl.BlockSpec((nbk, tt, 128), lambda i, t: (i, tmap(t), 0))
    av = pl.BlockSpec((nbk, 1, 128), lambda i, t: (i, 0, 0))
    return _pc(body, name=f"scan_{'bwd' if reverse else 'fwd'}_l{layer}", grid=(NLB // nbk, nT),
               in_specs=[st, st, av, av], out_specs=[st, st], out_shape=[S((NLB, T, 128), F32)] * 2,
               scratch_shapes=[pltpu.VMEM((nbk, 8, 128), F32)] * 2, compiler_params=_cp(2))(br, bi, a_re, a_im)


def _ssm_out(xr, xi, u, cre, cim, dvec, wglu, bglu, layer):
    T = u.shape[0]
    tm = TM

    def body(xr_ref, xi_ref, u_ref, pm_ref, cr_ref, ci_ref, d_ref, w_ref, bg_ref, s_ref, y_ref, z_ref):
        ys = []
        for s in range(4):
            xrc = jnp.concatenate([xr_ref[4 * s + q] for q in range(4)], axis=1)
            xic = jnp.concatenate([xi_ref[4 * s + q] for q in range(4)], axis=1)
            ys.append(_dot3(_dot, xrc, cr_ref[s]) - _dot3(_dot, xic, ci_ref[s]))
        y = jnp.concatenate(ys, axis=1) + d_ref[...] * u_ref[...]
        yg = _gelu(y)
        ygb = yg.astype(BF16)
        z = bg_ref[...] + sum(_dot(ygb[:, j * 128:(j + 1) * 128], w_ref[j]) for j in range(NSH))
        y_ref[...] = y
        z_ref[...] = z
        s_ref[...] = _to_streams(yg * jax.nn.sigmoid(z), pm_ref[...]).reshape(s_ref.shape)

    st = pl.BlockSpec((NLB, tm, 128), lambda i: (0, i, 0))
    cw = pl.BlockSpec((4, 512, 128), lambda i: (0, 0, 0))
    half = pl.BlockSpec((tm, DSS), lambda i: (i, 0))
    s, y, z = _pc(body, name=f"ssm_out_l{layer}", grid=(T // tm,),
                  in_specs=[st, st, half, pl.BlockSpec((tm, tm), lambda i: (0, 0)), cw, cw, _gain_spec(DSS, layer),
                            pl.BlockSpec((NSH, None, 128, DSS), lambda i: (0, layer, 0, 0)), _gain_spec(DSS, layer)],
                  out_specs=[_stream_block(tm, DSS), half, half],
                  out_shape=[S((16, T // 16, DSS), F32), S((T, DSS), F32), S((T, DSS), F32)],
                  compiler_params=_cp(1))(xr, xi, u, _perm_matrix(tm), cre, cim, dvec, wglu, bglu)
    return s.reshape(T, DSS), y, z


def _ssm_out_bwd(dssm, y, z, xr, xi, u, cre, cim, dvec, wglu, layer):
    T = u.shape[0]
    tm = TMB

    def body(ds_ref, pm_ref, y_ref, z_ref, xr_ref, xi_ref, u_ref, cr_ref, ci_ref, d_ref, w_ref,
             gr_ref, gi_ref, du_ref, dz_ref, yg_ref, dbg_ref, dd_ref, dcr_ref, dci_ref):
        i = pl.program_id(0)

        @pl.when(i == 0)
        def _():
            dbg_ref[...] = jnp.zeros_like(dbg_ref)
            dd_ref[...] = jnp.zeros_like(dd_ref)
            dcr_ref[...] = jnp.zeros_like(dcr_ref)
            dci_ref[...] = jnp.zeros_like(dci_ref)

        yv = y_ref[...]
        yg = _gelu(yv)
        sg = jax.nn.sigmoid(z_ref[...])
        ds = _to_time(ds_ref[...].reshape(tm, DSS), pm_ref[...])
        dz = ds * yg * sg * (1.0 - sg)
        dzb = dz.astype(BF16)
        dz_ref[...] = dzb
        yg_ref[...] = yg.astype(BF16)
        dbg_ref[...] += jnp.sum(dz, axis=0, keepdims=True)
        dyg = ds * sg + jnp.concatenate([_dot_nt(dzb, w_ref[j]) for j in range(NSH)], axis=1)
        dy = dyg * _gelu_grad(yv)
        u = u_ref[...]
        dd_ref[...] += jnp.sum(dy * u, axis=0, keepdims=True)
        du_ref[...] = dy * d_ref[...]
        for s in range(4):
            dyc = dy[:, s * 128:(s + 1) * 128]
            g_r = _dot3(_dot_nt, dyc, cr_ref[s])
            g_i = -_dot3(_dot_nt, dyc, ci_ref[s])
            for q in range(4):
                gr_ref[4 * s + q] = g_r[:, q * 128:(q + 1) * 128]
                gi_ref[4 * s + q] = g_i[:, q * 128:(q + 1) * 128]
            xrc = jnp.concatenate([xr_ref[4 * s + q] for q in range(4)], axis=1)
            xic = jnp.concatenate([xi_ref[4 * s + q] for q in range(4)], axis=1)
            dcr_ref[s] += _dot3(_dot_tn, xrc, dyc)
            dci_ref[s] -= _dot3(_dot_tn, xic, dyc)

    st = pl.BlockSpec((NLB, tm, 128), lambda i: (0, i, 0))
    cw = pl.BlockSpec((4, 512, 128), lambda i: (0, 0, 0))
    half = pl.BlockSpec((tm, DSS), lambda i: (i, 0))
    return _pc(body, name=f"ssm_out_bwd_l{layer}", grid=(T // tm,),
               in_specs=[_stream_block(tm, DSS), pl.BlockSpec((tm, tm), lambda i: (0, 0)), half, half, st, st, half,
                         cw, cw, _gain_spec(DSS, layer), pl.BlockSpec((NSH, None, 128, DSS), lambda i: (0, layer, 0, 0))],
               out_specs=[st, st, half, half, half, _row_acc_spec(DSS), _row_acc_spec(DSS), cw, cw],
               out_shape=[S((NLB, T, 128), F32)] * 2 + [S((T, DSS), F32), S((T, DSS), BF16), S((T, DSS), BF16),
                                                        S((1, DSS), F32), S((1, DSS), F32),
                                                        S((4, 512, 128), F32), S((4, 512, 128), F32)],
               compiler_params=_cp(1))(dssm.reshape(16, T // 16, DSS), _perm_matrix(tm), y, z, xr, xi, u, cre, cim,
                                       dvec, wglu)


def _ssm_da(gr, gi, xr, xi, layer):
    T = gr.shape[1]
    tb = 1024 if T % 1024 == 0 else T

    def body(gr_ref, gi_ref, xr_ref, xi_ref, dr_ref, di_ref, lr, li):
        t = pl.program_id(1)

        @pl.when(t == 0)
        def _():
            dr_ref[...] = jnp.zeros_like(dr_ref)
            di_ref[...] = jnp.zeros_like(di_ref)
            lr[...] = jnp.zeros_like(lr)
            li[...] = jnp.zeros_like(li)

        g_r, g_i, x_r, x_i = gr_ref[...], gi_ref[...], xr_ref[...], xi_ref[...]
        pr = pltpu.roll(x_r, 1, 0)
        pi_ = pltpu.roll(x_i, 1, 0)
        g0r, g0i = g_r[0:1, :], g_i[0:1, :]
        fr = lr[7:8, :] - x_r[tb - 1:tb, :]
        fi = li[7:8, :] - x_i[tb - 1:tb, :]
        dr_ref[...] += jnp.sum(g_r * pr + g_i * pi_, axis=0, keepdims=True) + g0r * fr + g0i * fi
        di_ref[...] += jnp.sum(g_i * pr - g_r * pi_, axis=0, keepdims=True) + g0i * fr - g0r * fi
        lr[...] = x_r[tb - 8:tb, :]
        li[...] = x_i[tb - 8:tb, :]

    st = pl.BlockSpec((None, tb, 128), lambda k, t: (k, t, 0))
    out = pl.BlockSpec((None, 1, 128), lambda k, t: (k, 0, 0))
    return _pc(body, name=f"ssm_da_l{layer}", grid=(NLB, T // tb), in_specs=[st] * 4, out_specs=[out, out],
               out_shape=[S((NLB, 1, 128), F32)] * 2, scratch_shapes=[pltpu.VMEM((8, 128), F32)] * 2,
               compiler_params=_cp(2))(gr, gi, xr, xi)


def _ssm_in_bwd(gr, gi, u, bre, bim, du_direct, layer):
    T = u.shape[0]
    tm = TM

    def body(gr_ref, gi_ref, u_ref, pm_ref, br_ref, bi_ref, dd_ref, du_ref, dbr_ref, dbi_ref):
        i = pl.program_id(0)

        @pl.when(i == 0)
        def _():
            dbr_ref[...] = jnp.zeros_like(dbr_ref)
            dbi_ref[...] = jnp.zeros_like(dbi_ref)

        dus = []
        for s in range(4):
            grc = jnp.concatenate([gr_ref[4 * s + q] for q in range(4)], axis=1)
            gic = jnp.concatenate([gi_ref[4 * s + q] for q in range(4)], axis=1)
            uc = u_ref[:, s * 128:(s + 1) * 128]
            dus.append(_dot3(_dot_nt, grc, br_ref[s]) + _dot3(_dot_nt, gic, bi_ref[s]))
            dbr_ref[s] += _dot3(_dot_tn, uc, grc)
            dbi_ref[s] += _dot3(_dot_tn, uc, gic)
        du = jnp.concatenate(dus, axis=1) + dd_ref[...]
        du_ref[...] = _to_streams(du, pm_ref[...]).reshape(du_ref.shape)

    whole = pl.BlockSpec((4, 128, 512), lambda i: (0, 0, 0))
    st = pl.BlockSpec((NLB, tm, 128), lambda i: (0, i, 0))
    half = pl.BlockSpec((tm, DSS), lambda i: (i, 0))
    du, dbr, dbi = _pc(body, name=f"ssm_in_bwd_l{layer}", grid=(T // tm,),
                       in_specs=[st, st, half, pl.BlockSpec((tm, tm), lambda i: (0, 0)), whole, whole, half],
                       out_specs=[_stream_block(tm, DSS), whole, whole],
                       out_shape=[S((16, T // 16, DSS), F32), S((4, 128, 512), F32), S((4, 128, 512), F32)],
                       compiler_params=_cp(1))(gr, gi, u, _perm_matrix(tm), bre, bim, du_direct)
    return du.reshape(T, DSS), dbr, dbi


def _mix_out(outs, lses, ssm, h, attn_g, ssm_g, post_g, wout, layer):
    T = h.shape[0]
    tm = TM

    def body(o1, o2, o3, l1, l2, l3, s_ref, h_ref, ag_ref, sg_ref, pg_ref, w_ref, ho_ref, at_ref, ls_ref, mx_ref, mo_ref):
        la, lb, lc = l1[...], l2[...], l3[...]
        m = jnp.maximum(jnp.maximum(la, lb), lc)
        wa, wb, wc = jnp.exp(la - m), jnp.exp(lb - m), jnp.exp(lc - m)
        zs = wa + wb + wc
        attn = (wa * o1[...] + wb * o2[...] + wc * o3[...]) / zs
        at_ref[...] = attn
        ls_ref[...] = m + jnp.log(zs)
        mixed = jnp.concatenate([_rms_fwd(attn, ag_ref[...]), _rms_fwd(s_ref[...], sg_ref[...])], axis=1).astype(BF16)
        mx_ref[...] = mixed
        mo = sum(_dot(mixed[:, j * 256:(j + 1) * 256], w_ref[j]) for j in range(NSH))
        mo_ref[...] = mo
        ho_ref[...] = h_ref[...] + _rms_fwd(mo, pg_ref[...])

    row = pl.BlockSpec((tm, D), lambda i: (i, 0))
    half = pl.BlockSpec((tm, DA), lambda i: (i, 0))
    return _pc(body, name=f"mix_out_l{layer}", grid=(T // tm,),
               in_specs=[half] * 7 + [row, _gain_spec(DA, layer), _gain_spec(DSS, layer), _gain_spec(D, layer),
                                      pl.BlockSpec((NSH, None, 256, D), lambda i: (0, layer, 0, 0))],
               out_specs=[row, half, half, row, row],
               out_shape=[S((T, D), F32), S((T, DA), F32), S((T, DA), F32), S((T, D), BF16), S((T, D), F32)],
               compiler_params=_cp(1))(*outs, *lses, ssm, h, attn_g, ssm_g, post_g, wout)


def _mix_out_bwd(dout, mo, attn, ssm, attn_g, ssm_g, post_g, wout, layer):
    T = dout.shape[0]
    tm = TMB
    head_sum = jnp.asarray(np.kron(np.eye(NH, dtype=np.float32), np.ones((HD, HD), np.float32)), BF16)

    def body(do_ref, mo_ref, at_ref, s_ref, ag_ref, sg_ref, pg_ref, w_ref, e_ref,
             da_ref, ds_ref, dl_ref, dmo_ref, dpg_ref, dag_ref, dsg_ref):
        i = pl.program_id(0)

        @pl.when(i == 0)
        def _():
            dpg_ref[...] = jnp.zeros_like(dpg_ref)
            dag_ref[...] = jnp.zeros_like(dag_ref)
            dsg_ref[...] = jnp.zeros_like(dsg_ref)

        dmo, dpg = _rms_bwd(do_ref[...], mo_ref[...], pg_ref[...])
        dpg_ref[...] += dpg
        dmob = dmo.astype(BF16)
        dmo_ref[...] = dmob
        dmix = jnp.concatenate([_dot_nt(dmob, w_ref[j]) for j in range(NSH)], axis=1)
        attn = at_ref[...]
        dat, dag = _rms_bwd(dmix[:, :DA], attn, ag_ref[...])
        dss, dsg = _rms_bwd(dmix[:, DA:], s_ref[...], sg_ref[...])
        dag_ref[...] += dag
        dsg_ref[...] += dsg
        da_ref[...] = dat
        ds_ref[...] = dss
        prod = dat * attn
        p1 = prod.astype(BF16)
        r1 = prod - p1.astype(F32)
        p2 = r1.astype(BF16)
        p3 = (r1 - p2.astype(F32)).astype(BF16)
        e = e_ref[...]
        dl_ref[...] = _dot(p1, e) + _dot(p2, e) + _dot(p3, e)

    row = pl.BlockSpec((tm, D), lambda i: (i, 0))
    half = pl.BlockSpec((tm, DA), lambda i: (i, 0))
    return _pc(body, name=f"mix_out_bwd_l{layer}", grid=(T // tm,),
               in_specs=[row, row, half, half, _gain_spec(DA, layer), _gain_spec(DSS, layer), _gain_spec(D, layer),
                         pl.BlockSpec((NSH, None, 256, D), lambda i: (0, layer, 0, 0)),
                         pl.BlockSpec((DA, DA), lambda i: (0, 0))],
               out_specs=[half, half, half, row, _row_acc_spec(D), _row_acc_spec(DA), _row_acc_spec(DSS)],
               out_shape=[S((T, DA), F32)] * 3 + [S((T, D), BF16), S((1, D), F32), S((1, DA), F32), S((1, DSS), F32)],
               compiler_params=_cp(1))(dout, mo, attn, ssm, attn_g, ssm_g, post_g, wout, head_sum)


def _ple_fwd(h, p3, wup, wgate, post_g, layer):
    T = h.shape[0]
    tm = TM

    def body(h_ref, p_ref, wu_ref, wg_ref, g_ref, ho_ref, e_ref, gt_ref):
        hv = h_ref[...]
        hb = hv.astype(BF16)
        pb = p_ref[...].astype(BF16)
        gte = sum(_dot(hb[:, j * 256:(j + 1) * 256], wg_ref[j]) for j in range(NSH))
        e = jnp.concatenate([_dot(pb, wu_ref[j]) for j in range(NSH)], axis=1)
        e_ref[...] = e
        gt_ref[...] = gte
        ho_ref[...] = hv + _rms_fwd(e * jax.nn.sigmoid(gte), g_ref[...])

    row = pl.BlockSpec((tm, D), lambda i: (i, 0))
    return _pc(body, name=f"ple_fwd_l{layer}", grid=(T // tm,),
               in_specs=[row, pl.BlockSpec((None, tm, PLE), lambda i: (layer, i, 0)),
                         pl.BlockSpec((NSH, None, PLE, 256), lambda i: (0, layer, 0, 0)),
                         pl.BlockSpec((NSH, None, 256, D), lambda i: (0, layer, 0, 0)), _gain_spec(D, layer)],
               out_specs=[row, row, row], out_shape=[S((T, D), F32)] * 3,
               compiler_params=_cp(1))(h, p3, wup, wgate, post_g)


def _ple_bwd(dout, e, gte, wgate, post_g, layer):
    T = dout.shape[0]
    tm = TMB

    def body(do_ref, e_ref, gt_ref, wg_ref, g_ref, dh_ref, de_ref, dgt_ref, dg_ref):
        i = pl.program_id(0)

        @pl.when(i == 0)
        def _():
            dg_ref[...] = jnp.zeros_like(dg_ref)

        ev = e_ref[...]
        sg = jax.nn.sigmoid(gt_ref[...])
        do = do_ref[...]
        dple, dg = _rms_bwd(do, ev * sg, g_ref[...])
        dg_ref[...] += dg
        de = (dple * sg).astype(BF16)
        for j in range(NSH):
            de_ref[j] = de[:, j * 256:(j + 1) * 256]
        dgb = (dple * ev * sg * (1.0 - sg)).astype(BF16)
        dgt_ref[...] = dgb
        dh_ref[...] = do + jnp.concatenate([_dot_nt(dgb, wg_ref[j]) for j in range(NSH)], axis=1)

    row = pl.BlockSpec((tm, D), lambda i: (i, 0))
    return _pc(body, name=f"ple_bwd_l{layer}", grid=(T // tm,),
               in_specs=[row, row, row, pl.BlockSpec((NSH, None, 256, D), lambda i: (0, layer, 0, 0)), _gain_spec(D, layer)],
               out_specs=[row, pl.BlockSpec((NSH, tm, 256), lambda i: (0, i, 0)), row, _row_acc_spec(D)],
               out_shape=[S((T, D), F32), S((NSH, T, 256), BF16), S((T, D), BF16), S((1, D), F32)],
               compiler_params=_cp(1))(dout, e, gte, wgate, post_g)


def _loss_head(h, target):
    T = h.shape[0]
    tm = TM

    def body(h_ref, t_ref, dy_ref, l_ref):
        i = pl.program_id(0)

        @pl.when(i == 0)
        def _():
            l_ref[...] = jnp.zeros_like(l_ref)

        err = h_ref[...] - t_ref[...]
        dy_ref[...] = err * (1.0 / D)
        l_ref[...] += jnp.broadcast_to((0.5 / D) * jnp.sum(err * err), (1, 128))

    row = pl.BlockSpec((tm, D), lambda i: (i, 0))
    return _pc(body, name="loss_head", grid=(T // tm,), in_specs=[row, row],
               out_specs=[row, pl.BlockSpec((1, 128), lambda i: (0, 0))],
               out_shape=[S((T, D), F32), S((1, 128), F32)], compiler_params=_cp(1))(h, target)


def _local_step(x, p3, pos_col, target, W, Sm):
    L = p3.shape[0]
    g3 = {n: Sm[n].reshape(L, 1, -1) for n in ("ffn1_pre_g", "ffn1_post_g", "mix_pre_g", "attn_norm_g", "ssm_norm_g",
                                                "mix_post_g", "ffn2_pre_g", "ffn2_post_g", "ple_post_g", "ssm_b_glu", "ssm_d")}
    rot = _rot_tables(pos_col)
    prep_names = ("ssm_lam_re", "ssm_lam_im", "ssm_log_dt", "ssm_b_re", "ssm_b_im", "ssm_c_re", "ssm_c_im")

    saved = []
    h = x
    for l in range(L):
        sv = {"h0": h}
        h, sv["a1"], sv["b1"], sv["f1"], sv["xn1"] = _ffn_fwd(
            h, g3["ffn1_pre_g"], g3["ffn1_post_g"], W["ffn1_w_gate"], W["ffn1_w_up"], W["ffn1_w_down"], l, "1")
        sv["h1"] = h
        P, sv["ain"] = _mix_proj(h, g3["mix_pre_g"], W["w_in"], rot, l)
        sv["P"] = P
        ol = [_attn_fwd(P, d, l) for d in PATTERN_DILATIONS]
        prep, sv["prep_vjp"] = jax.vjp(_ssm_prep, *[Sm[n][l] for n in prep_names])
        a_re, a_im, bre, bim, cre, cim = prep
        sv["prep"] = prep
        sv["u"], bur, bui = _ssm_in(P, bre, bim, l)
        xr, xi = _scan(bur, bui, a_re, a_im, False, l)
        sv["xr"], sv["xi"] = xr, xi
        ssm, sv["y"], sv["z"] = _ssm_out(xr, xi, sv["u"], cre, cim, g3["ssm_d"], W["ssm_w_glu"], g3["ssm_b_glu"], l)
        sv["ssm"] = ssm
        h, sv["attn"], sv["lse"], sv["mixed"], sv["mo"] = _mix_out(
            [o for o, _ in ol], [s for _, s in ol], ssm, h, g3["attn_norm_g"], g3["ssm_norm_g"], g3["mix_post_g"],
            W["w_out"], l)
        sv["h2"] = h
        h, sv["a2"], sv["b2"], sv["f2"], sv["xn2"] = _ffn_fwd(
            h, g3["ffn2_pre_g"], g3["ffn2_post_g"], W["ffn2_w_gate"], W["ffn2_w_up"], W["ffn2_w_down"], l, "2")
        sv["h3"] = h
        h, sv["e"], sv["gte"] = _ple_fwd(h, p3, W["ple_w_up"], W["ple_w_gate"], g3["ple_post_g"], l)
        saved.append(sv)

    dh, loss = _loss_head(h, target)

    G = {n: lax.empty((NSH, L, r, c), BF16) for n, r, c in BIG}
    sg = {n: [None] * L for n in SMALL}
    whole = lambda j: (0, 0)
    shard = lambda j: (j, 0)
    kcol = lambda j: (0, j)
    for l in reversed(range(L)):
        sv = saved[l]
        dh, de, dgte, sg["ple_post_g"][l] = _ple_bwd(dh, sv["e"], sv["gte"], W["ple_w_gate"], g3["ple_post_g"], l)
        G["ple_w_up"] = _dw(p3[l][None], de, G["ple_w_up"], l, PLE, 256, whole, shard, f"dw_ple_up_l{l}")
        G["ple_w_gate"] = _dw(sv["h3"][None], dgte[None], G["ple_w_gate"], l, 256, D, kcol, whole, f"dw_ple_gate_l{l}")
        dh, df, da, db, hh, sg["ffn2_pre_g"][l], sg["ffn2_post_g"][l] = _ffn_bwd(
            dh, sv["h2"], sv["f2"], sv["a2"], sv["b2"], g3["ffn2_pre_g"], g3["ffn2_post_g"],
            W["ffn2_w_gate"], W["ffn2_w_up"], W["ffn2_w_down"], l, "2")
        G["ffn2_w_gate"] = _dw(da, sv["xn2"][None], G["ffn2_w_gate"], l, DFS, D, shard, whole, f"dw_ffn2_gate_l{l}")
        G["ffn2_w_up"] = _dw(db, sv["xn2"][None], G["ffn2_w_up"], l, DFS, D, shard, whole, f"dw_ffn2_up_l{l}")
        G["ffn2_w_down"] = _dw(hh, df[None], G["ffn2_w_down"], l, DFS, D, shard, whole, f"dw_ffn2_down_l{l}")
        a_re, a_im, bre, bim, cre, cim = sv["prep"]
        dattn, dssm, delta, dmo, sg["mix_post_g"][l], sg["attn_norm_g"][l], sg["ssm_norm_g"][l] = _mix_out_bwd(
            dh, sv["mo"], sv["attn"], sv["ssm"], g3["attn_norm_g"], g3["ssm_norm_g"], g3["mix_post_g"], W["w_out"], l)
        G["w_out"] = _dw(sv["mixed"][None], dmo[None], G["w_out"], l, 256, D, kcol, whole, f"dw_out_l{l}")
        gnr, gni, du_direct, dz, yg, sg["ssm_b_glu"][l], dd, dcre, dcim = _ssm_out_bwd(
            dssm, sv["y"], sv["z"], sv["xr"], sv["xi"], sv["u"], cre, cim, g3["ssm_d"], W["ssm_w_glu"], l)
        sg["ssm_d"][l] = dd.reshape(Sm["ssm_d"].shape[1:])
        G["ssm_w_glu"] = _dw(yg[None], dz[None], G["ssm_w_glu"], l, 128, DSS, kcol, whole, f"dw_glu_l{l}")
        gr, gi = _scan(gnr, gni, a_re, a_im, True, l)
        dar, dai = _ssm_da(gr, gi, sv["xr"], sv["xi"], l)
        du, dbre, dbim = _ssm_in_bwd(gr, gi, sv["u"], bre, bim, du_direct, l)
        for n, g in zip(prep_names, sv["prep_vjp"]((dar, dai, dbre, dbim, dcre, dcim))):
            sg[n][l] = g
        acc = None
        for d in PATTERN_DILATIONS:
            acc = _attn_bwd(sv["P"], dattn, sv["lse"], delta, acc, d, l)
        dh, dP, sg["mix_pre_g"][l] = _mix_proj_bwd(acc[0], acc[1], acc[2], du, dh, sv["h1"], g3["mix_pre_g"],
                                                   W["w_in"], rot, l)
        G["w_in"] = _dw(sv["ain"][None], dP, G["w_in"], l, D, DA, whole, shard, f"dw_in_l{l}")
        dh, df, da, db, hh, sg["ffn1_pre_g"][l], sg["ffn1_post_g"][l] = _ffn_bwd(
            dh, sv["h0"], sv["f1"], sv["a1"], sv["b1"], g3["ffn1_pre_g"], g3["ffn1_post_g"],
            W["ffn1_w_gate"], W["ffn1_w_up"], W["ffn1_w_down"], l, "1")
        G["ffn1_w_gate"] = _dw(da, sv["xn1"][None], G["ffn1_w_gate"], l, DFS, D, shard, whole, f"dw_ffn1_gate_l{l}")
        G["ffn1_w_up"] = _dw(db, sv["xn1"][None], G["ffn1_w_up"], l, DFS, D, shard, whole, f"dw_ffn1_up_l{l}")
        G["ffn1_w_down"] = _dw(hh, df[None], G["ffn1_w_down"], l, DFS, D, shard, whole, f"dw_ffn1_down_l{l}")

    small = {n: jnp.stack([g.reshape(Sm[n].shape[1:]) for g in sg[n]]) for n in SMALL}
    return loss, dh, G, small


HBM_SPEC = pl.BlockSpec(memory_space=pltpu.HBM)


def _place():
    x, y, c = lax.axis_index("x"), lax.axis_index("y"), lax.axis_index("c")
    chips = [(1 - x, y), (x, 1 - y), (1 - x, 1 - y)]
    return x, y, c, chips


def _comm_params():
    return pltpu.CompilerParams(vmem_limit_bytes=VMEM_LIMIT)


def _gather_weights(ws):
    n = len(ws)

    def body(*refs):
        ins, outs = refs[:n], refs[n:2 * n]
        s_ici, r_ici, s_d2d, r_d2d, s_loc = refs[2 * n:]
        x, y, c, chips = _place()
        me = 2 * x + y

        def half(ref, t, hc):
            r2 = ws[t].shape[1] // 2
            return ref.at[:, pl.ds(hc * r2, r2), :]

        def ici(t, k, src_chip, to):
            j = 2 * src_chip[0] + src_chip[1]
            src = half(ins[t], t, c) if to is not None else half(outs[t].at[j], t, c)
            return pltpu.make_async_remote_copy(src_ref=src, dst_ref=half(outs[t].at[j], t, c),
                                                send_sem=s_ici.at[3 * t + k], recv_sem=r_ici.at[3 * t + k],
                                                device_id=to if to is not None else (x, y, c), device_id_type=MESH)

        def d2d(t, k, hc):
            j = 2 * chips[k][0] + chips[k][1]
            r = half(outs[t].at[j], t, hc)
            return pltpu.make_async_remote_copy(src_ref=r, dst_ref=r, send_sem=s_d2d.at[3 * t + k],
                                                recv_sem=r_d2d.at[3 * t + k], device_id=(x, y, 1 - c),
                                                device_id_type=MESH)

        own = [pltpu.make_async_copy(ins[t], outs[t].at[me], s_loc.at[t]) for t in range(n)]
        for cp in own:
            cp.start()
        sends = [ici(t, k, (x, y), (*chips[k], c)) for t in range(n) for k in range(3)]
        for cp in sends:
            cp.start()
        passed = []
        for t in range(n):
            for k in range(3):
                ici(t, k, chips[k], None).wait_recv()
                passed.append(d2d(t, k, c))
                passed[-1].start()
        for t in range(n):
            for k in range(3):
                d2d(t, k, 1 - c).wait_recv()
        for cp in sends + passed:
            cp.wait_send()
        for cp in own:
            cp.wait()

    return _pc(body, name="gather_weights", in_specs=[HBM_SPEC] * n, out_specs=[HBM_SPEC] * n,
               out_shape=[S((NSH,) + w.shape, w.dtype) for w in ws],
               scratch_shapes=[pltpu.SemaphoreType.DMA((3 * n,))] * 4 + [pltpu.SemaphoreType.DMA((n,))],
               compiler_params=_comm_params())(*ws)


def _swap_halves(gs):
    n = len(gs)

    def body(*refs):
        ins, outs = refs[:n], refs[n:2 * n]
        s_sem, r_sem = refs[2 * n:]
        x, y, c, _ = _place()
        cps = []
        for t in range(n):
            r2 = gs[t].shape[2] // 2
            cps.append(pltpu.make_async_remote_copy(
                src_ref=ins[t].at[:, :, pl.ds((1 - c) * r2, r2), :], dst_ref=outs[t], send_sem=s_sem.at[t],
                recv_sem=r_sem.at[t], device_id=(x, y, 1 - c), device_id_type=MESH))
            cps[-1].start()
        for cp in cps:
            cp.wait_recv()
        for cp in cps:
            cp.wait_send()

    return _pc(body, name="grad_swap_halves", in_specs=[HBM_SPEC] * n, out_specs=[HBM_SPEC] * n,
               out_shape=[S(g.shape[:2] + (g.shape[2] // 2, g.shape[3]), g.dtype) for g in gs],
               scratch_shapes=[pltpu.SemaphoreType.DMA((n,))] * 2, compiler_params=_comm_params())(*gs)


def _add_half(g, landed, c_arr, name):
    _, L, r2, cols = landed.shape

    def body(c_ref, g_ref, l_ref, o_ref):
        o_ref[...] = (g_ref[...].astype(F32) + l_ref[...].astype(F32)).astype(BF16)

    gs = pltpu.PrefetchScalarGridSpec(
        num_scalar_prefetch=1, grid=(NSH, L),
        in_specs=[pl.BlockSpec((None, None, r2, cols), lambda j, l, c: (j, l, c[0], 0)),
                  pl.BlockSpec((None, None, r2, cols), lambda j, l, c: (j, l, 0, 0))],
        out_specs=pl.BlockSpec((None, None, r2, cols), lambda j, l, c: (j, l, 0, 0)))
    return _pc(body, name=name, grid_spec=gs, out_shape=S(landed.shape, BF16), compiler_params=_cp(2))(c_arr, g, landed)


def _send_shards(ps):
    n = len(ps)

    def body(*refs):
        ins, outs = refs[:n], refs[n:2 * n]
        s_sem, r_sem = refs[2 * n:]
        x, y, c, chips = _place()
        cps = []
        for t in range(n):
            for k in range(3):
                cps.append(pltpu.make_async_remote_copy(
                    src_ref=ins[t].at[2 * chips[k][0] + chips[k][1]], dst_ref=outs[t].at[k],
                    send_sem=s_sem.at[3 * t + k], recv_sem=r_sem.at[3 * t + k], device_id=(*chips[k], c),
                    device_id_type=MESH))
                cps[-1].start()
        for cp in cps:
            cp.wait_recv()
        for cp in cps:
            cp.wait_send()

    return _pc(body, name="grad_send_shards", in_specs=[HBM_SPEC] * n, out_specs=[HBM_SPEC] * n,
               out_shape=[S((3,) + p.shape[1:], p.dtype) for p in ps],
               scratch_shapes=[pltpu.SemaphoreType.DMA((3 * n,))] * 2, compiler_params=_comm_params())(*ps)


def _sum_shards(part, landed, mc_arr, name):
    _, L, r2, cols = landed.shape

    def body(mc_ref, p_ref, l_ref, o_ref):
        o_ref[...] = ((p_ref[...].astype(F32) + l_ref[0].astype(F32)) + l_ref[1].astype(F32)) + l_ref[2].astype(F32)

    gs = pltpu.PrefetchScalarGridSpec(
        num_scalar_prefetch=1, grid=(L,),
        in_specs=[pl.BlockSpec((None, None, r2, cols), lambda l, mc: (mc[0], l, 0, 0)),
                  pl.BlockSpec((3, None, r2, cols), lambda l, mc: (0, l, 0, 0))],
        out_specs=pl.BlockSpec((None, r2, cols), lambda l, mc: (l, mc[1], 0)))
    return _pc(body, name=name, grid_spec=gs, out_shape=S((L, 2 * r2, cols), F32),
               compiler_params=_cp(1))(mc_arr, part, landed)


def _share_halves(bufs):
    n = len(bufs)

    def body(*refs):
        ins, outs = refs[:n], refs[n:2 * n]
        s_sem, r_sem = refs[2 * n:]
        x, y, c, _ = _place()
        cps = []
        for t in range(n):
            r2 = bufs[t].shape[1] // 2
            cps.append(pltpu.make_async_remote_copy(
                src_ref=ins[t].at[:, pl.ds(c * r2, r2), :], dst_ref=outs[t].at[:, pl.ds(c * r2, r2), :],
                send_sem=s_sem.at[t], recv_sem=r_sem.at[t], device_id=(x, y, 1 - c), device_id_type=MESH))
            cps[-1].start()
        for cp in cps:
            cp.wait_recv()
        for cp in cps:
            cp.wait_send()

    return _pc(body, name="grad_share_halves", in_specs=[HBM_SPEC] * n, out_specs=[HBM_SPEC] * n,
               out_shape=[S(b.shape, b.dtype) for b in bufs], input_output_aliases={t: t for t in range(n)},
               scratch_shapes=[pltpu.SemaphoreType.DMA((n,))] * 2, compiler_params=_comm_params())(*bufs)


def _gather_small(v):
    nr = v.shape[0]

    def body(v_ref, out_ref, send_sems, recv_sems, local_sem):
        x, y, c, chips = _place()
        me, sibling = (x, y, c), (x, y, 1 - c)

        def rows(px, py, pc):
            return out_ref.at[pl.ds((4 * px + 2 * py + pc) * nr, nr), :]

        def copy(k, block, to, src=None):
            return pltpu.make_async_remote_copy(src_ref=rows(*block) if src is None else src, dst_ref=rows(*block),
                                                send_sem=send_sems.at[k], recv_sem=recv_sems.at[k], device_id=to,
                                                device_id_type=MESH)

        mine = pltpu.make_async_copy(v_ref, rows(*me), local_sem)
        mine.start()
        first = [copy(0, me, sibling, src=v_ref)]
        first += [copy(1 + j, me, (*chip, c), src=v_ref) for j, chip in enumerate(chips)]
        for cp in first:
            cp.start()
        passed = [copy(4 + j, (*chip, c), sibling) for j, chip in enumerate(chips)]
        for j, chip in enumerate(chips):
            copy(1 + j, (*chip, c), me).wait_recv()
            passed[j].start()
        copy(0, sibling, me).wait_recv()
        for j, chip in enumerate(chips):
            copy(4 + j, (*chip, 1 - c), me).wait_recv()
        for cp in first + passed:
            cp.wait_send()
        mine.wait()

    vm = pl.BlockSpec(memory_space=pltpu.VMEM)
    return _pc(body, name="gather_small_grads", in_specs=[vm], out_specs=vm, out_shape=S((8 * nr, 128), F32),
               scratch_shapes=[pltpu.SemaphoreType.DMA((7,)), pltpu.SemaphoreType.DMA((7,)), pltpu.SemaphoreType.DMA],
               compiler_params=_comm_params())(v)


def _adamw_math(w, g, m, v):
    m2 = ADAM_B1 * m + (1.0 - ADAM_B1) * g
    v2 = ADAM_B2 * v + (1.0 - ADAM_B2) * (g * g)
    m_hat = m2 / (1.0 - ADAM_B1 ** ADAM_STEP)
    v_hat = v2 / (1.0 - ADAM_B2 ** ADAM_STEP)
    return -ADAM_LR * (m_hat / (jnp.sqrt(v_hat) + ADAM_EPS) + ADAM_WD * w), m2, v2


def _adamw(w, g, m, v, name):
    L, R, C = w.shape
    rb = R // 2 if R >= 512 else R

    def body(w_ref, g_ref, m_ref, v_ref, d_ref, m2_ref, v2_ref):
        d_ref[...], m2_ref[...], v2_ref[...] = _adamw_math(w_ref[...], g_ref[...], m_ref[...], v_ref[...])

    blk = pl.BlockSpec((None, rb, C), lambda l, r: (l, r, 0))
    return _pc(body, name=name, grid=(L, R // rb), in_specs=[blk] * 4, out_specs=[blk] * 3,
               out_shape=[S(w.shape, F32)] * 3, compiler_params=_cp(2))(w, g, m, v)


def _adamw_small(gathered, w, m, v):
    nr = w.shape[0]
    rb = nr // 5

    def body(a_ref, w_ref, m_ref, v_ref, g_ref, d_ref, m2_ref, v2_ref):
        g = a_ref[0]
        for k in range(1, 8):
            g = g + a_ref[k]
        g_ref[...] = g
        d_ref[...], m2_ref[...], v2_ref[...] = _adamw_math(w_ref[...], g, m_ref[...], v_ref[...])

    blk = pl.BlockSpec((rb, 128), lambda i: (i, 0))
    return _pc(body, name="adamw_small", grid=(nr // rb,), in_specs=[pl.BlockSpec((8, rb, 128), lambda i: (0, i, 0))] + [blk] * 3,
               out_specs=[blk] * 4, out_shape=[S((nr, 128), F32)] * 4, compiler_params=_cp(1))(gathered, w, m, v)


SMALL_ROWS = 4520


def _pack(arrs):
    flat = jnp.concatenate([a.reshape(-1) for a in arrs])
    return jnp.pad(flat, (0, SMALL_ROWS * 128 - flat.shape[0])).reshape(SMALL_ROWS, 128)


def _unpack(packed, like):
    flat = packed.reshape(-1)
    out, off = [], 0
    for a in like:
        out.append(flat[off:off + a.size].reshape(a.shape))
        off += a.size
    return out


def kernel(x, p, positions, ffn1_pre_g, ffn1_w_gate, ffn1_w_up, ffn1_w_down, ffn1_post_g, mix_pre_g, w_in, attn_norm_g, ssm_lam_re, ssm_lam_im, ssm_log_dt, ssm_b_re, ssm_b_im, ssm_c_re, ssm_c_im, ssm_d, ssm_w_glu, ssm_b_glu, ssm_norm_g, w_out, mix_post_g, ffn2_pre_g, ffn2_w_gate, ffn2_w_up, ffn2_w_down, ffn2_post_g, ple_w_up, ple_w_gate, ple_post_g, loss_target, m_ffn1_pre_g, m_ffn1_w_gate, m_ffn1_w_up, m_ffn1_w_down, m_ffn1_post_g, m_mix_pre_g, m_w_in, m_attn_norm_g, m_ssm_lam_re, m_ssm_lam_im, m_ssm_log_dt, m_ssm_b_re, m_ssm_b_im, m_ssm_c_re, m_ssm_c_im, m_ssm_d, m_ssm_w_glu, m_ssm_b_glu, m_ssm_norm_g, m_w_out, m_mix_post_g, m_ffn2_pre_g, m_ffn2_w_gate, m_ffn2_w_up, m_ffn2_w_down, m_ffn2_post_g, m_ple_w_up, m_ple_w_gate, m_ple_post_g, v_ffn1_pre_g, v_ffn1_w_gate, v_ffn1_w_up, v_ffn1_w_down, v_ffn1_post_g, v_mix_pre_g, v_w_in, v_attn_norm_g, v_ssm_lam_re, v_ssm_lam_im, v_ssm_log_dt, v_ssm_b_re, v_ssm_b_im, v_ssm_c_re, v_ssm_c_im, v_ssm_d, v_ssm_w_glu, v_ssm_b_glu, v_ssm_norm_g, v_w_out, v_mix_post_g, v_ffn2_pre_g, v_ffn2_w_gate, v_ffn2_w_up, v_ffn2_w_down, v_ffn2_post_g, v_ple_w_up, v_ple_w_gate, v_ple_post_g):
    a = dict(locals())
    T = x.shape[1]
    big_names = [n for n, _, _ in BIG]
    for n in TRANSPOSED:
        for pre in ("", "m_", "v_"):
            a[pre + n] = jnp.swapaxes(a[pre + n], 1, 2)

    gathered = _gather_weights([a[n].astype(BF16) for n in big_names])
    W = dict(zip(big_names, gathered))
    Sm = {n: a[n] for n in SMALL}

    def to_streams(t):
        lead = t.shape[:-2]
        return jnp.swapaxes(t.reshape(lead + (T // 16, 16, t.shape[-1])), -3, -2).reshape(t.shape)

    def to_time(t):
        lead = t.shape[:-2]
        return jnp.swapaxes(t.reshape(lead + (16, T // 16, t.shape[-1])), -3, -2).reshape(t.shape)

    loss, gx, G, small = _local_step(to_streams(x[0]), to_streams(p[:, 0]),
                                     to_streams(positions.reshape(T, 1).astype(F32)), to_streams(loss_target[0]), W, Sm)
    gx = to_time(gx)

    c_arr = lax.axis_index("c").astype(jnp.int32).reshape(1)
    mc_arr = jnp.stack([2 * lax.axis_index("x") + lax.axis_index("y"), lax.axis_index("c")]).astype(jnp.int32)
    gs = [G[n] for n in big_names]
    landed = _swap_halves(gs)
    parts = [_add_half(g, la, c_arr, f"grad_add_half_{n}") for g, la, n in zip(gs, landed, big_names)]
    landed = _send_shards(parts)
    halves = [_sum_shards(pt, la, mc_arr, f"grad_sum_shards_{n}") for pt, la, n in zip(parts, landed, big_names)]
    grads = dict(zip(big_names, _share_halves(halves)))

    small_g = _gather_small(_pack([small[n] for n in SMALL])).reshape(8, SMALL_ROWS, 128)
    sg, sd, sm, sv = _adamw_small(small_g, _pack([a[n] for n in SMALL]), _pack([a["m_" + n] for n in SMALL]),
                                  _pack([a["v_" + n] for n in SMALL]))
    like = [a[n] for n in SMALL]
    res = {}
    for n, g_, d_, m_, v_ in zip(SMALL, _unpack(sg, like), _unpack(sd, like), _unpack(sm, like), _unpack(sv, like)):
        res[n] = (g_, d_, m_, v_)
    for n in big_names:
        d_, m_, v_ = _adamw(a[n], grads[n], a["m_" + n], a["v_" + n], f"adamw_{n}")
        res[n] = (grads[n], d_, m_, v_)
        if n in TRANSPOSED:
            res[n] = tuple(jnp.swapaxes(t, 1, 2) for t in res[n])

    total = lax.psum(loss[0, 0], ("x", "y", "c"))
    return (total, gx[None], *[res[n][0] for n in WEIGHTS], *[res[n][1] for n in WEIGHTS],
            *[res[n][2] for n in WEIGHTS], *[res[n][3] for n in WEIGHTS])
```

```python
import functools
import math

import numpy as np
import jax
import jax.numpy as jnp
from jax import lax
from jax.experimental import pallas as pl
from jax.experimental.pallas import tpu as pltpu

F32 = jnp.float32
BF16 = jnp.bfloat16
S = jax.ShapeDtypeStruct
MESH = pl.DeviceIdType.MESH

D = 1024
DA = 512
DSS = 512
HD = 64
NH = 8
BAND = 128
NSH = 4
DFS = 704
PLE = 256
EPS = 1e-6
ROPE_THETA = 500000.0
PATTERN_DILATIONS = (1, 4, 16)
NLB = 16
ADAM_LR, ADAM_B1, ADAM_B2, ADAM_EPS, ADAM_WD, ADAM_STEP = 0.001, 0.9, 0.999, 1e-08, 0.01, 10

VMEM_LIMIT = 56 * 1024 * 1024
TM = 512
TMB = 256

BIG = (
    ("ffn1_w_gate", DFS, D), ("ffn1_w_up", DFS, D), ("ffn1_w_down", DFS, D),
    ("w_in", D, 512), ("ssm_w_glu", 128, 512), ("w_out", 256, D),
    ("ffn2_w_gate", DFS, D), ("ffn2_w_up", DFS, D), ("ffn2_w_down", DFS, D),
    ("ple_w_up", PLE, 256), ("ple_w_gate", 256, D),
)
TRANSPOSED = ("ffn1_w_gate", "ffn1_w_up", "ffn2_w_gate", "ffn2_w_up")
SMALL = ("ffn1_pre_g", "ffn1_post_g", "mix_pre_g", "attn_norm_g", "ssm_lam_re", "ssm_lam_im", "ssm_log_dt",
         "ssm_b_re", "ssm_b_im", "ssm_c_re", "ssm_c_im", "ssm_d", "ssm_b_glu", "ssm_norm_g", "mix_post_g",
         "ffn2_pre_g", "ffn2_post_g", "ple_post_g")
WEIGHTS = ("ffn1_pre_g", "ffn1_w_gate", "ffn1_w_up", "ffn1_w_down", "ffn1_post_g", "mix_pre_g", "w_in", "attn_norm_g",
           "ssm_lam_re", "ssm_lam_im", "ssm_log_dt", "ssm_b_re", "ssm_b_im", "ssm_c_re", "ssm_c_im", "ssm_d",
           "ssm_w_glu", "ssm_b_glu", "ssm_norm_g", "w_out", "mix_post_g", "ffn2_pre_g", "ffn2_w_gate", "ffn2_w_up",
           "ffn2_w_down", "ffn2_post_g", "ple_w_up", "ple_w_gate", "ple_post_g")


def _pc(body, **kw):
    return pl.pallas_call(body, **kw)


def _cp(n_grid):
    return pltpu.CompilerParams(dimension_semantics=("arbitrary",) * n_grid, vmem_limit_bytes=VMEM_LIMIT)


def _dot(a, b):
    return jnp.dot(a, b, preferred_element_type=F32)


def _dot_nt(a, b):
    return lax.dot_general(a, b, (((1,), (1,)), ((), ())), preferred_element_type=F32)


def _dot_tn(a, b):
    return lax.dot_general(a, b, (((0,), (0,)), ((), ())), preferred_element_type=F32)


def _split(a):
    hi = a.astype(BF16)
    return hi, (a - hi.astype(F32)).astype(BF16)


def _dot3(fn, a, b):
    ah, al = _split(a)
    bh, bl = _split(b)
    return fn(ah, bh) + fn(ah, bl) + fn(al, bh)


def _rms_fwd(x, g):
    r = lax.rsqrt(jnp.mean(x * x, axis=-1, keepdims=True) + EPS)
    return x * r * g


def _rms_bwd(dy, x, g):
    r = lax.rsqrt(jnp.mean(x * x, axis=-1, keepdims=True) + EPS)
    xr = x * r
    gd = dy * g
    dx = r * (gd - xr * jnp.mean(gd * xr, axis=-1, keepdims=True))
    dg = jnp.sum(dy * xr, axis=0, keepdims=True)
    return dx, dg


def _gelu(y):
    k = math.sqrt(2.0 / math.pi)
    return 0.5 * y * (1.0 + jnp.tanh(k * (y + 0.044715 * y * y * y)))


def _gelu_grad(y):
    k = math.sqrt(2.0 / math.pi)
    t = jnp.tanh(k * (y + 0.044715 * y * y * y))
    return 0.5 * (1.0 + t) + 0.5 * y * (1.0 - t * t) * k * (1.0 + 3 * 0.044715 * y * y)


def _gain_spec(n, layer):
    return pl.BlockSpec((None, 1, n), lambda *_: (layer, 0, 0))


def _row_acc_spec(n):
    return pl.BlockSpec((1, n), lambda *_: (0, 0))


def _rot_tables(pos_col):
    T = pos_col.shape[0]
    half = HD // 8
    inv = (ROPE_THETA ** (-np.arange(half, dtype=np.float32) * (2.0 / (2 * half)))).astype(np.float32)
    lane_freq = np.tile(np.concatenate([inv, inv, np.zeros(HD - 2 * half, np.float32)]), NH)[None, :]

    def body(p_ref, f_ref, c_ref, s1_ref, s2_ref):
        ang = p_ref[...] * f_ref[...]
        d = lax.broadcasted_iota(jnp.int32, ang.shape, 1) % HD
        cs = jnp.cos(ang)
        sn = jnp.sin(ang)
        c_ref[...] = jnp.where(d < 2 * half, cs, 1.0)
        s1_ref[...] = jnp.where(d < half, -sn, 0.0)
        s2_ref[...] = jnp.where((d >= half) & (d < 2 * half), sn, 0.0)

    tm = TM
    return _pc(body, name="rot_tables", grid=(T // tm,),
               in_specs=[pl.BlockSpec((tm, 1), lambda i: (i, 0)), pl.BlockSpec((1, DA), lambda i: (0, 0))],
               out_specs=[pl.BlockSpec((tm, DA), lambda i: (i, 0))] * 3,
               out_shape=[S((T, DA), F32)] * 3, compiler_params=_cp(1))(pos_col, jnp.asarray(lane_freq))


def _rot_fwd(t, c, s1, s2):
    return t * c + pltpu.roll(t, DA - 8, 1) * s1 + pltpu.roll(t, 8, 1) * s2


def _rot_bwd(g, c, s1, s2):
    return g * c + pltpu.roll(g * s1, 8, 1) + pltpu.roll(g * s2, DA - 8, 1)


def _ffn_fwd(h, pre_g, post_g, wg, wu, wd, layer, tag):
    T = h.shape[0]
    tm = TM
    nt = T // tm

    def body(h_ref, pg_ref, qg_ref, wg_ref, wu_ref, wd_ref, ho_ref, a_ref, b_ref, f_ref, xn_ref, xs, facc):
        j = pl.program_id(1)

        @pl.when(j == 0)
        def _():
            xb = _rms_fwd(h_ref[...], pg_ref[...]).astype(BF16)
            xs[...] = xb
            xn_ref[...] = xb
            facc[...] = jnp.zeros_like(facc)

        xb = xs[...]
        ab = _dot_nt(xb, wg_ref[...]).astype(BF16)
        bb = _dot_nt(xb, wu_ref[...]).astype(BF16)
        a_ref[...] = ab
        b_ref[...] = bb
        a = ab.astype(F32)
        hh = (a * jax.nn.sigmoid(a) * bb.astype(F32)).astype(BF16)
        facc[...] += _dot(hh, wd_ref[...])

        @pl.when(j == NSH - 1)
        def _():
            f = facc[...]
            f_ref[...] = f
            ho_ref[...] = h_ref[...] + 0.5 * _rms_fwd(f, qg_ref[...])

    row = pl.BlockSpec((tm, D), lambda i, j: (i, 0))
    act = pl.BlockSpec((None, tm, DFS), lambda i, j: (j, i, 0))
    wrow = pl.BlockSpec((None, None, DFS, D), lambda i, j: (j, 0, 0, 0))
    return _pc(body, name=f"ffn_fwd_{tag}_l{layer}", grid=(nt, NSH),
               in_specs=[row, _gain_spec(D, layer), _gain_spec(D, layer), wrow, wrow, wrow],
               out_specs=[row, act, act, row, row],
               out_shape=[S((T, D), F32), S((NSH, T, DFS), BF16), S((NSH, T, DFS), BF16), S((T, D), F32), S((T, D), BF16)],
               scratch_shapes=[pltpu.VMEM((tm, D), BF16), pltpu.VMEM((tm, D), F32)],
               compiler_params=_cp(2))(h, pre_g, post_g, wg, wu, wd)


def _ffn_bwd(dout, h, f, a, b, pre_g, post_g, wg, wu, wd, layer, tag):
    T = h.shape[0]
    tm = TM
    nt = T // tm

    def body(do_ref, h_ref, f_ref, a_ref, b_ref, pg_ref, qg_ref, wg_ref, wu_ref, wd_ref,
             dh_ref, df_ref, da_ref, db_ref, hh_ref, dpg_ref, dqg_ref, dfs, dxn):
        i = pl.program_id(0)
        j = pl.program_id(1)

        @pl.when((i == 0) & (j == 0))
        def _():
            dpg_ref[...] = jnp.zeros_like(dpg_ref)
            dqg_ref[...] = jnp.zeros_like(dqg_ref)

        @pl.when(j == 0)
        def _():
            df, dq = _rms_bwd(0.5 * do_ref[...], f_ref[...], qg_ref[...])
            dqg_ref[...] += dq
            dfb = df.astype(BF16)
            dfs[...] = dfb
            df_ref[...] = dfb
            dxn[...] = jnp.zeros_like(dxn)

        dhh = _dot_nt(dfs[...], wd_ref[...])
        av = a_ref[...].astype(F32)
        bv = b_ref[...].astype(F32)
        sg = jax.nn.sigmoid(av)
        sa = av * sg
        hh_ref[...] = (sa * bv).astype(BF16)
        dab = (dhh * bv * (sg * (1.0 + av * (1.0 - sg)))).astype(BF16)
        dbb = (dhh * sa).astype(BF16)
        da_ref[...] = dab
        db_ref[...] = dbb
        dxn[...] += _dot(dab, wg_ref[...]) + _dot(dbb, wu_ref[...])

        @pl.when(j == NSH - 1)
        def _():
            dx, dp = _rms_bwd(dxn[...], h_ref[...], pg_ref[...])
            dpg_ref[...] += dp
            dh_ref[...] = do_ref[...] + dx

    row = pl.BlockSpec((tm, D), lambda i, j: (i, 0))
    act = pl.BlockSpec((None, tm, DFS), lambda i, j: (j, i, 0))
    wrow = pl.BlockSpec((None, None, DFS, D), lambda i, j: (j, 0, 0, 0))
    return _pc(body, name=f"ffn_bwd_{tag}_l{layer}", grid=(nt, NSH),
               in_specs=[row, row, row, act, act, _gain_spec(D, layer), _gain_spec(D, layer), wrow, wrow, wrow],
               out_specs=[row, row, act, act, act, _row_acc_spec(D), _row_acc_spec(D)],
               out_shape=[S((T, D), F32), S((T, D), BF16), S((NSH, T, DFS), BF16), S((NSH, T, DFS), BF16),
                          S((NSH, T, DFS), BF16), S((1, D), F32), S((1, D), F32)],
               scratch_shapes=[pltpu.VMEM((tm, D), BF16), pltpu.VMEM((tm, D), F32)],
               compiler_params=_cp(2))(dout, h, f, a, b, pre_g, post_g, wg, wu, wd)


def _dw(A, B, buf, layer, kb, nb, a_idx, b_idx, name):
    T = A.shape[1]
    tt = 2 * TM if T % (2 * TM) == 0 else TM
    nt = T // tt

    def body(a_ref, b_ref, buf_ref, o_ref, acc):
        t = pl.program_id(1)

        @pl.when(t == 0)
        def _():
            acc[...] = jnp.zeros_like(acc)

        acc[...] += _dot_tn(a_ref[...].astype(BF16), b_ref[...].astype(BF16))

        @pl.when(t == nt - 1)
        def _():
            o_ref[...] = acc[...].astype(o_ref.dtype)

    return _pc(body, name=name, grid=(NSH, nt),
               in_specs=[pl.BlockSpec((None, tt, kb), lambda j, t: (a_idx(j)[0], t, a_idx(j)[1])),
                         pl.BlockSpec((None, tt, nb), lambda j, t: (b_idx(j)[0], t, b_idx(j)[1])),
                         pl.BlockSpec(memory_space=pl.ANY)],
               out_specs=pl.BlockSpec((None, None, kb, nb), lambda j, t: (j, layer, 0, 0)),
               out_shape=S(buf.shape, buf.dtype), input_output_aliases={2: 0},
               scratch_shapes=[pltpu.VMEM((kb, nb), F32)], compiler_params=_cp(2))(A, B, buf)


def _mix_proj(h, pre_g, win, rot, layer):
    T = h.shape[0]
    tm = TM

    def body(h_ref, g_ref, w_ref, c_ref, s1_ref, s2_ref, p_ref, xn_ref, xs):
        j = pl.program_id(1)

        @pl.when(j == 0)
        def _():
            xb = _rms_fwd(h_ref[...], g_ref[...]).astype(BF16)
            xs[...] = xb
            xn_ref[...] = xb

        o = _dot(xs[...], w_ref[...])

        @pl.when(j < 2)
        def _():
            p_ref[...] = _rot_fwd(o, c_ref[...], s1_ref[...], s2_ref[...])

        @pl.when(j >= 2)
        def _():
            p_ref[...] = o

    row = pl.BlockSpec((tm, D), lambda i, j: (i, 0))
    half = pl.BlockSpec((tm, DA), lambda i, j: (i, 0))
    return _pc(body, name=f"mix_proj_l{layer}", grid=(T // tm, NSH),
               in_specs=[row, _gain_spec(D, layer), pl.BlockSpec((None, None, D, DA), lambda i, j: (j, 0, 0, 0)),
                         half, half, half],
               out_specs=[pl.BlockSpec((None, tm, DA), lambda i, j: (j, i, 0)), row],
               out_shape=[S((NSH, T, DA), F32), S((T, D), BF16)],
               scratch_shapes=[pltpu.VMEM((tm, D), BF16)], compiler_params=_cp(2))(h, pre_g, win, *rot)


def _mix_proj_bwd(dq, dk, dv, du, dh_up, h, pre_g, win, rot, layer):
    T = h.shape[0]
    tm = TM

    def body(dq_ref, dk_ref, dv_ref, du_ref, up_ref, h_ref, g_ref, w_ref, c_ref, s1_ref, s2_ref,
             dh_ref, dp_ref, dg_ref, dps, dxn):
        i = pl.program_id(0)
        j = pl.program_id(1)

        @pl.when((i == 0) & (j == 0))
        def _():
            dg_ref[...] = jnp.zeros_like(dg_ref)

        @pl.when(j == 0)
        def _():
            dxn[...] = jnp.zeros_like(dxn)
            dps[...] = _rot_bwd(dq_ref[...], c_ref[...], s1_ref[...], s2_ref[...]).astype(BF16)

        @pl.when(j == 1)
        def _():
            dps[...] = _rot_bwd(dk_ref[...], c_ref[...], s1_ref[...], s2_ref[...]).astype(BF16)

        @pl.when(j == 2)
        def _():
            dps[...] = dv_ref[...].astype(BF16)

        @pl.when(j == 3)
        def _():
            dps[...] = du_ref[...].astype(BF16)

        dpb = dps[...]
        dp_ref[...] = dpb
        dxn[...] += _dot_nt(dpb, w_ref[...])

        @pl.when(j == NSH - 1)
        def _():
            dx, dg = _rms_bwd(dxn[...], h_ref[...], g_ref[...])
            dg_ref[...] += dg
            dh_ref[...] = up_ref[...] + dx

    row = pl.BlockSpec((tm, D), lambda i, j: (i, 0))
    half = pl.BlockSpec((tm, DA), lambda i, j: (i, 0))
    return _pc(body, name=f"mix_proj_bwd_l{layer}", grid=(T // tm, NSH),
               in_specs=[half, half, half, half, row, row, _gain_spec(D, layer),
                         pl.BlockSpec((None, None, D, DA), lambda i, j: (j, 0, 0, 0)), half, half, half],
               out_specs=[row, pl.BlockSpec((None, tm, DA), lambda i, j: (j, i, 0)), _row_acc_spec(D)],
               out_shape=[S((T, D), F32), S((NSH, T, DA), BF16), S((1, D), F32)],
               scratch_shapes=[pltpu.VMEM((tm, DA), BF16), pltpu.VMEM((tm, D), F32)],
               compiler_params=_cp(2))(dq, dk, dv, du, dh_up, h, pre_g, win, *rot)


def _stream_pos(d, axis):
    i = lax.broadcasted_iota(jnp.int32, (BAND, BAND), axis)
    if d == 16:
        return i
    if d == 4:
        return 4 * (i % 32) + i // 32
    return 16 * (i % 8) + i // 8


def _band_masks(b, d):
    qi, kj = _stream_pos(d, 0), _stream_pos(d, 1)
    return kj <= qi, (kj >= qi) & (b > 0)


def _pattern(d, T):
    n16 = T // 16
    if d == 16:
        return (16, n16, DA), (None, BAND, DA), lambda r, k: (r, k, 0)
    if d == 4:
        return (4, 4, n16, DA), (4, None, 32, DA), lambda r, k: (0, r, k, 0)
    return (16, n16, DA), (16, 8, DA), lambda r, k: (0, k, 0)


def _pattern_spec(d, T, kmap, lead=None):
    _, blk, idx = _pattern(d, T)
    if lead is None:
        return pl.BlockSpec(blk, lambda r, b: idx(r, kmap(b)))
    return pl.BlockSpec((None,) + blk, lambda r, b: (lead,) + idx(r, kmap(b)))


def _attn_fwd(P, d, layer):
    T = P.shape[1]
    nb = T // d // BAND
    vshape = _pattern(d, T)[0]
    Pv = P.reshape((NSH,) + vshape)
    scale = HD ** -0.5

    def body(q_ref, kp_ref, kc_ref, vp_ref, vc_ref, o_ref, l_ref, qs, kps, kcs, vps, vcs, osc, lsc):
        b = pl.program_id(1)
        for src, dst in ((q_ref, qs), (kp_ref, kps), (kc_ref, kcs), (vp_ref, vps), (vc_ref, vcs)):
            dst[...] = src[...].reshape(BAND, DA).astype(BF16)
        mask_c, mask_p = _band_masks(b, d)
        for hd in range(NH):
            sl = slice(hd * HD, (hd + 1) * HD)
            q = qs[:, sl]
            sc = jnp.where(mask_c, _dot_nt(q, kcs[:, sl]) * scale, -1e30)
            sp = jnp.where(mask_p, _dot_nt(q, kps[:, sl]) * scale, -1e30)
            m = jnp.maximum(jnp.max(sc, axis=-1, keepdims=True), jnp.max(sp, axis=-1, keepdims=True))
            ec = jnp.exp(sc - m)
            ep = jnp.exp(sp - m)
            den = jnp.sum(ec, axis=-1, keepdims=True) + jnp.sum(ep, axis=-1, keepdims=True)
            o = _dot(ec.astype(BF16), vcs[:, sl]) + _dot(ep.astype(BF16), vps[:, sl])
            osc[:, sl] = o / den
            lsc[:, sl] = jnp.broadcast_to(m + jnp.log(den), (BAND, HD))
        o_ref[...] = osc[...].reshape(o_ref.shape)
        l_ref[...] = lsc[...].reshape(l_ref.shape)

    cur = lambda b: b
    prev = lambda b: jnp.maximum(b - 1, 0)
    out = _pattern_spec(d, T, cur)
    o, l = _pc(body, name=f"attn_fwd_d{d}_l{layer}", grid=(d, nb),
               in_specs=[_pattern_spec(d, T, cur, 0), _pattern_spec(d, T, prev, 1), _pattern_spec(d, T, cur, 1),
                         _pattern_spec(d, T, prev, 2), _pattern_spec(d, T, cur, 2)],
               out_specs=[out, out], out_shape=[S(vshape, F32)] * 2,
               scratch_shapes=[pltpu.VMEM((BAND, DA), BF16)] * 5 + [pltpu.VMEM((BAND, DA), F32)] * 2,
               compiler_params=_cp(2))(Pv, Pv, Pv, Pv, Pv)
    return o.reshape(T, DA), l.reshape(T, DA)


def _attn_bwd(P, dO, lse, delta, acc, d, layer):
    T = P.shape[1]
    nb = T // d // BAND
    vshape = _pattern(d, T)[0]
    Pv = P.reshape((NSH,) + vshape)
    scale = HD ** -0.5
    first = acc is None

    def body(*refs):
        q_ref, kp_ref, kc_ref, vp_ref, vc_ref, do_ref, l_ref, dl_ref = refs[:8]
        if first:
            dq_ref, dk_ref, dv_ref = refs[8:11]
        else:
            aq_ref, ak_ref, av_ref, dq_ref, dk_ref, dv_ref = refs[8:14]
        qs, kps, kcs, vps, vcs, dos, ls, dls, oq, ok, ov, ck, cv = refs[-13:]
        b = pl.program_id(1)
        flat = lambda ref: ref[...].reshape(BAND, DA)

        @pl.when(b == 0)
        def _():
            ck[...] = jnp.zeros_like(ck)
            cv[...] = jnp.zeros_like(cv)

        @pl.when(b < nb)
        def _():
            for src, dst in ((q_ref, qs), (kp_ref, kps), (kc_ref, kcs), (vp_ref, vps), (vc_ref, vcs), (do_ref, dos)):
                dst[...] = flat(src).astype(BF16)
            ls[...] = flat(l_ref)
            dls[...] = flat(dl_ref)
            mask_c, mask_p = _band_masks(b, d)
            for hd in range(NH):
                sl = slice(hd * HD, (hd + 1) * HD)
                one = slice(hd * HD, hd * HD + 1)
                q, kc, kp, do = qs[:, sl], kcs[:, sl], kps[:, sl], dos[:, sl]
                lrow = ls[:, one]
                drow = dls[:, one]
                pc = jnp.where(mask_c, jnp.exp(_dot_nt(q, kc) * scale - lrow), 0.0)
                pp = jnp.where(mask_p, jnp.exp(_dot_nt(q, kp) * scale - lrow), 0.0)
                dsc = (pc * (_dot_nt(do, vcs[:, sl]) - drow) * scale).astype(BF16)
                dsp = (pp * (_dot_nt(do, vps[:, sl]) - drow) * scale).astype(BF16)
                oq[:, sl] = _dot(dsc, kc) + _dot(dsp, kp)
                ok[:, sl] = ck[:, sl] + _dot_tn(dsp, q)
                ov[:, sl] = cv[:, sl] + _dot_tn(pp.astype(BF16), do)
                ck[:, sl] = _dot_tn(dsc, q)
                cv[:, sl] = _dot_tn(pc.astype(BF16), do)
            if first:
                dq_ref[...] = oq[...].reshape(dq_ref.shape)
                dk_ref[...] = ok[...].reshape(dk_ref.shape)
                dv_ref[...] = ov[...].reshape(dv_ref.shape)
            else:
                dq_ref[...] = aq_ref[...] + oq[...].reshape(dq_ref.shape)
                dk_ref[...] = ak_ref[...] + ok[...].reshape(dk_ref.shape)
                dv_ref[...] = av_ref[...] + ov[...].reshape(dv_ref.shape)

        @pl.when(b == nb)
        def _():
            if first:
                dk_ref[...] = ck[...].reshape(dk_ref.shape)
                dv_ref[...] = cv[...].reshape(dv_ref.shape)
            else:
                dk_ref[...] = ak_ref[...] + ck[...].reshape(dk_ref.shape)
                dv_ref[...] = av_ref[...] + cv[...].reshape(dv_ref.shape)

    qb = lambda b: jnp.minimum(b, nb - 1)
    qprev = lambda b: jnp.maximum(qb(b) - 1, 0)
    kb = lambda b: jnp.maximum(b - 1, 0)
    qrow = _pattern_spec(d, T, qb)
    krow = _pattern_spec(d, T, kb)
    view = lambda t: t.reshape(vshape)
    ins = [Pv, Pv, Pv, Pv, Pv, view(dO), view(lse), view(delta)]
    specs = [_pattern_spec(d, T, qb, 0), _pattern_spec(d, T, qprev, 1), _pattern_spec(d, T, qb, 1),
             _pattern_spec(d, T, qprev, 2), _pattern_spec(d, T, qb, 2), qrow, qrow, qrow]
    if not first:
        ins += [view(t) for t in acc]
        specs += [qrow, krow, krow]
    dq, dk, dv = _pc(body, name=f"attn_bwd_d{d}_l{layer}", grid=(d, nb + 1), in_specs=specs,
                     out_specs=[qrow, krow, krow], out_shape=[S(vshape, F32)] * 3,
                     scratch_shapes=[pltpu.VMEM((BAND, DA), BF16)] * 6 + [pltpu.VMEM((BAND, DA), F32)] * 7,
                     compiler_params=_cp(2))(*ins)
    return dq.reshape(T, DA), dk.reshape(T, DA), dv.reshape(T, DA)


def _ssm_prep(lam_re, lam_im, log_dt, b_re, b_im, c_re, c_im):
    dt = jnp.exp(log_dt)[:, None]
    er = jnp.exp(lam_re * dt)
    a_re = er * jnp.cos(lam_im * dt)
    a_im = er * jnp.sin(lam_im * dt)
    nr, ni = a_re - 1.0, a_im
    den = lam_re * lam_re + lam_im * lam_im
    cr = (nr * lam_re + ni * lam_im) / den
    ci = (ni * lam_re - nr * lam_im) / den
    bbr = cr[..., None] * b_re - ci[..., None] * b_im
    bbi = cr[..., None] * b_im + ci[..., None] * b_re
    eye = jnp.eye(8, dtype=F32)

    def bblock(bb):
        t = bb.reshape(4, 8, 64, 16).transpose(0, 1, 3, 2)
        return (t[:, :, :, None, :] * eye[None, :, None, :, None]).reshape(4, 128, 512)

    def cblock(cc):
        t = cc.reshape(4, 8, 16, 64).transpose(0, 1, 3, 2)
        return (t[:, :, :, None, :] * eye[None, :, None, :, None]).reshape(4, 512, 128)

    return (a_re.reshape(NLB, 1, 128), a_im.reshape(NLB, 1, 128), bblock(bbr), bblock(bbi), cblock(c_re), cblock(c_im))


def _perm_matrix(tm):
    n = tm // 16
    pm = np.zeros((tm, tm), np.float32)
    for r in range(16):
        pm[16 * np.arange(n) + r, r * n + np.arange(n)] = 1.0
    return jnp.asarray(pm, BF16)


def _pieces(x):
    p1 = x.astype(BF16)
    r1 = x - p1.astype(F32)
    p2 = r1.astype(BF16)
    return p1, p2, (r1 - p2.astype(F32)).astype(BF16)


def _to_time(x, pm):
    return sum(_dot(pm, p) for p in _pieces(x))


def _to_streams(x, pm):
    return sum(_dot_tn(pm, p) for p in _pieces(x))


def _stream_block(tm, cols, lead=None):
    if lead is None:
        return pl.BlockSpec((16, tm // 16, cols), lambda i: (0, i, 0))
    return pl.BlockSpec((None, 16, tm // 16, cols), lambda i: (lead, 0, i, 0))


def _reorder(t3, to_streams, name):
    B, T, C = t3.shape
    tm = TM

    def body(x_ref, pm_ref, o_ref):
        if to_streams:
            o_ref[...] = _to_streams(x_ref[...], pm_ref[...]).reshape(o_ref.shape)
        else:
            o_ref[...] = _to_time(x_ref[...].reshape(tm, C), pm_ref[...])

    time_blk = pl.BlockSpec((None, tm, C), lambda b, i: (b, i, 0))
    stream_blk = pl.BlockSpec((None, 16, tm // 16, C), lambda b, i: (b, 0, i, 0))
    src = t3 if to_streams else t3.reshape(B, 16, T // 16, C)
    out = _pc(body, name=name, grid=(B, T // tm),
              in_specs=[time_blk if to_streams else stream_blk, pl.BlockSpec((tm, tm), lambda b, i: (0, 0))],
              out_specs=stream_blk if to_streams else time_blk,
              out_shape=S((B, 16, T // 16, C) if to_streams else (B, T, C), F32),
              compiler_params=_cp(2))(src, _perm_matrix(tm))
    return out.reshape(B, T, C)


def _ssm_in(P, bre, bim, layer):
    T = P.shape[1]
    tm = TM

    def body(u_ref, pm_ref, br_ref, bi_ref, un_ref, or_ref, oi_ref):
        u = _to_time(u_ref[...].reshape(tm, DSS), pm_ref[...])
        un_ref[...] = u
        for s in range(4):
            uc = u[:, s * 128:(s + 1) * 128]
            r = _dot3(_dot, uc, br_ref[s])
            m = _dot3(_dot, uc, bi_ref[s])
            for q in range(4):
                or_ref[4 * s + q] = r[:, q * 128:(q + 1) * 128]
                oi_ref[4 * s + q] = m[:, q * 128:(q + 1) * 128]

    whole = pl.BlockSpec((4, 128, 512), lambda i: (0, 0, 0))
    st = pl.BlockSpec((NLB, tm, 128), lambda i: (0, i, 0))
    return _pc(body, name=f"ssm_in_l{layer}", grid=(T // tm,),
               in_specs=[_stream_block(tm, DSS, 3), pl.BlockSpec((tm, tm), lambda i: (0, 0)), whole, whole],
               out_specs=[pl.BlockSpec((tm, DSS), lambda i: (i, 0)), st, st],
               out_shape=[S((T, DSS), F32)] + [S((NLB, T, 128), F32)] * 2,
               compiler_params=_cp(1))(P.reshape(NSH, 16, T // 16, DSS), _perm_matrix(tm), bre, bim)


def _scan(br, bi, a_re, a_im, reverse, layer):
    T = br.shape[1]
    nbk = 2
    tt = min(T, 1024)
    nT = T // tt
    ntile = tt // 8
    sgn = -1.0 if reverse else 1.0
    last = 0 if reverse else 7

    def body(br_ref, bi_ref, ar_ref, ai_ref, xr_ref, xi_ref, cr, ci):
        @pl.when(pl.program_id(1) == 0)
        def _():
            cr[...] = jnp.zeros_like(cr)
            ci[...] = jnp.zeros_like(ci)

        row = lax.broadcasted_iota(jnp.int32, (8, 128), 0)
        consts = []
        for k in range(nbk):
            a1r = jnp.broadcast_to(ar_ref[k], (8, 128))
            a1i = sgn * jnp.broadcast_to(ai_ref[k], (8, 128))
            pows = [(a1r, a1i)]
            for _ in range(7):
                pr, pi_ = pows[-1]
                pows.append((a1r * pr - a1i * pi_, a1r * pi_ + a1i * pr))
            rounds = []
            for s in (1, 2, 4):
                inside = (row <= 7 - s) if reverse else (row >= s)
                rounds.append((jnp.where(inside, pows[s - 1][0], 0.0), jnp.where(inside, pows[s - 1][1], 0.0)))
            cmr, cmi = jnp.zeros((8, 128), F32), jnp.zeros((8, 128), F32)
            for r in range(8):
                e = (7 - r) if reverse else r
                cmr = jnp.where(row == r, pows[e][0], cmr)
                cmi = jnp.where(row == r, pows[e][1], cmi)
            consts.append((rounds, cmr, cmi))

        def tile(i, carry):
            j = (ntile - 1 - i) if reverse else i
            rows = pl.ds(pl.multiple_of(j * 8, 8), 8)
            out = []
            for k in range(nbk):
                rounds, cmr, cmi = consts[k]
                xr = br_ref[k, rows, :]
                xi = bi_ref[k, rows, :]
                for (mr, mi), s in zip(rounds, (1, 2, 4)):
                    sh = (8 - s) if reverse else s
                    rr = pltpu.roll(xr, sh, 0)
                    ri = pltpu.roll(xi, sh, 0)
                    xr, xi = xr + (mr * rr - mi * ri), xi + (mr * ri + mi * rr)
                c_r, c_i = carry[k]
                xr, xi = xr + (cmr * c_r - cmi * c_i), xi + (cmr * c_i + cmi * c_r)
                xr_ref[k, rows, :] = xr
                xi_ref[k, rows, :] = xi
                out.append((jnp.broadcast_to(xr[last:last + 1, :], (8, 128)),
                            jnp.broadcast_to(xi[last:last + 1, :], (8, 128))))
            return tuple(out)

        carry = lax.fori_loop(0, ntile, tile, tuple((cr[k], ci[k]) for k in range(nbk)), unroll=2)
        for k in range(nbk):
            cr[k] = carry[k][0]
            ci[k] = carry[k][1]

    tmap = (lambda t: nT - 1 - t) if reverse else (lambda t: t)
    st = pl.BlockSpec((nbk, tt, 128), lambda i, t: (i, tmap(t), 0))
    av = pl.BlockSpec((nbk, 1, 128), lambda i, t: (i, 0, 0))
    return _pc(body, name=f"scan_{'bwd' if reverse else 'fwd'}_l{layer}", grid=(NLB // nbk, nT),
               in_specs=[st, st, av, av], out_specs=[st, st], out_shape=[S((NLB, T, 128), F32)] * 2,
               scratch_shapes=[pltpu.VMEM((nbk, 8, 128), F32)] * 2, compiler_params=_cp(2))(br, bi, a_re, a_im)


def _ssm_out(xr, xi, u, cre, cim, dvec, wglu, bglu, layer):
    T = u.shape[0]
    tm = TM

    def body(xr_ref, xi_ref, u_ref, pm_ref, cr_ref, ci_ref, d_ref, w_ref, bg_ref, s_ref, y_ref, z_ref):
        ys = []
        for s in range(4):
            xrc = jnp.concatenate([xr_ref[4 * s + q] for q in range(4)], axis=1)
            xic = jnp.concatenate([xi_ref[4 * s + q] for q in range(4)], axis=1)
            ys.append(_dot3(_dot, xrc, cr_ref[s]) - _dot3(_dot, xic, ci_ref[s]))
        y = jnp.concatenate(ys, axis=1) + d_ref[...] * u_ref[...]
        yg = _gelu(y)
        ygb = yg.astype(BF16)
        z = bg_ref[...] + sum(_dot(ygb[:, j * 128:(j + 1) * 128], w_ref[j]) for j in range(NSH))
        y_ref[...] = y
        z_ref[...] = z
        s_ref[...] = _to_streams(yg * jax.nn.sigmoid(z), pm_ref[...]).reshape(s_ref.shape)

    st = pl.BlockSpec((NLB, tm, 128), lambda i: (0, i, 0))
    cw = pl.BlockSpec((4, 512, 128), lambda i: (0, 0, 0))
    half = pl.BlockSpec((tm, DSS), lambda i: (i, 0))
    s, y, z = _pc(body, name=f"ssm_out_l{layer}", grid=(T // tm,),
                  in_specs=[st, st, half, pl.BlockSpec((tm, tm), lambda i: (0, 0)), cw, cw, _gain_spec(DSS, layer),
                            pl.BlockSpec((NSH, None, 128, DSS), lambda i: (0, 0, 0, 0)), _gain_spec(DSS, layer)],
                  out_specs=[_stream_block(tm, DSS), half, half],
                  out_shape=[S((16, T // 16, DSS), F32), S((T, DSS), F32), S((T, DSS), F32)],
                  compiler_params=_cp(1))(xr, xi, u, _perm_matrix(tm), cre, cim, dvec, wglu, bglu)
    return s.reshape(T, DSS), y, z


def _ssm_out_bwd(dssm, y, z, xr, xi, u, cre, cim, dvec, wglu, layer):
    T = u.shape[0]
    tm = TMB

    def body(ds_ref, pm_ref, y_ref, z_ref, xr_ref, xi_ref, u_ref, cr_ref, ci_ref, d_ref, w_ref,
             gr_ref, gi_ref, du_ref, dz_ref, yg_ref, dbg_ref, dd_ref, dcr_ref, dci_ref):
        i = pl.program_id(0)

        @pl.when(i == 0)
        def _():
            dbg_ref[...] = jnp.zeros_like(dbg_ref)
            dd_ref[...] = jnp.zeros_like(dd_ref)
            dcr_ref[...] = jnp.zeros_like(dcr_ref)
            dci_ref[...] = jnp.zeros_like(dci_ref)

        yv = y_ref[...]
        yg = _gelu(yv)
        sg = jax.nn.sigmoid(z_ref[...])
        ds = _to_time(ds_ref[...].reshape(tm, DSS), pm_ref[...])
        dz = ds * yg * sg * (1.0 - sg)
        dzb = dz.astype(BF16)
        dz_ref[...] = dzb
        yg_ref[...] = yg.astype(BF16)
        dbg_ref[...] += jnp.sum(dz, axis=0, keepdims=True)
        dyg = ds * sg + jnp.concatenate([_dot_nt(dzb, w_ref[j]) for j in range(NSH)], axis=1)
        dy = dyg * _gelu_grad(yv)
        u = u_ref[...]
        dd_ref[...] += jnp.sum(dy * u, axis=0, keepdims=True)
        du_ref[...] = dy * d_ref[...]
        for s in range(4):
            dyc = dy[:, s * 128:(s + 1) * 128]
            g_r = _dot3(_dot_nt, dyc, cr_ref[s])
            g_i = -_dot3(_dot_nt, dyc, ci_ref[s])
            for q in range(4):
                gr_ref[4 * s + q] = g_r[:, q * 128:(q + 1) * 128]
                gi_ref[4 * s + q] = g_i[:, q * 128:(q + 1) * 128]
            xrc = jnp.concatenate([xr_ref[4 * s + q] for q in range(4)], axis=1)
            xic = jnp.concatenate([xi_ref[4 * s + q] for q in range(4)], axis=1)
            dcr_ref[s] += _dot3(_dot_tn, xrc, dyc)
            dci_ref[s] -= _dot3(_dot_tn, xic, dyc)

    st = pl.BlockSpec((NLB, tm, 128), lambda i: (0, i, 0))
    cw = pl.BlockSpec((4, 512, 128), lambda i: (0, 0, 0))
    half = pl.BlockSpec((tm, DSS), lambda i: (i, 0))
    return _pc(body, name=f"ssm_out_bwd_l{layer}", grid=(T // tm,),
               in_specs=[_stream_block(tm, DSS), pl.BlockSpec((tm, tm), lambda i: (0, 0)), half, half, st, st, half,
                         cw, cw, _gain_spec(DSS, layer), pl.BlockSpec((NSH, None, 128, DSS), lambda i: (0, 0, 0, 0))],
               out_specs=[st, st, half, half, half, _row_acc_spec(DSS), _row_acc_spec(DSS), cw, cw],
               out_shape=[S((NLB, T, 128), F32)] * 2 + [S((T, DSS), F32), S((T, DSS), BF16), S((T, DSS), BF16),
                                                        S((1, DSS), F32), S((1, DSS), F32),
                                                        S((4, 512, 128), F32), S((4, 512, 128), F32)],
               compiler_params=_cp(1))(dssm.reshape(16, T // 16, DSS), _perm_matrix(tm), y, z, xr, xi, u, cre, cim,
                                       dvec, wglu)


def _ssm_da(gr, gi, xr, xi, layer):
    T = gr.shape[1]
    tb = 1024 if T % 1024 == 0 else T

    def body(gr_ref, gi_ref, xr_ref, xi_ref, dr_ref, di_ref, lr, li):
        t = pl.program_id(1)

        @pl.when(t == 0)
        def _():
            dr_ref[...] = jnp.zeros_like(dr_ref)
            di_ref[...] = jnp.zeros_like(di_ref)
            lr[...] = jnp.zeros_like(lr)
            li[...] = jnp.zeros_like(li)

        g_r, g_i, x_r, x_i = gr_ref[...], gi_ref[...], xr_ref[...], xi_ref[...]
        pr = pltpu.roll(x_r, 1, 0)
        pi_ = pltpu.roll(x_i, 1, 0)
        g0r, g0i = g_r[0:1, :], g_i[0:1, :]
        fr = lr[7:8, :] - x_r[tb - 1:tb, :]
        fi = li[7:8, :] - x_i[tb - 1:tb, :]
        dr_ref[...] += jnp.sum(g_r * pr + g_i * pi_, axis=0, keepdims=True) + g0r * fr + g0i * fi
        di_ref[...] += jnp.sum(g_i * pr - g_r * pi_, axis=0, keepdims=True) + g0i * fr - g0r * fi
        lr[...] = x_r[tb - 8:tb, :]
        li[...] = x_i[tb - 8:tb, :]

    st = pl.BlockSpec((None, tb, 128), lambda k, t: (k, t, 0))
    out = pl.BlockSpec((None, 1, 128), lambda k, t: (k, 0, 0))
    return _pc(body, name=f"ssm_da_l{layer}", grid=(NLB, T // tb), in_specs=[st] * 4, out_specs=[out, out],
               out_shape=[S((NLB, 1, 128), F32)] * 2, scratch_shapes=[pltpu.VMEM((8, 128), F32)] * 2,
               compiler_params=_cp(2))(gr, gi, xr, xi)


def _ssm_in_bwd(gr, gi, u, bre, bim, du_direct, layer):
    T = u.shape[0]
    tm = TM

    def body(gr_ref, gi_ref, u_ref, pm_ref, br_ref, bi_ref, dd_ref, du_ref, dbr_ref, dbi_ref):
        i = pl.program_id(0)

        @pl.when(i == 0)
        def _():
            dbr_ref[...] = jnp.zeros_like(dbr_ref)
            dbi_ref[...] = jnp.zeros_like(dbi_ref)

        dus = []
        for s in range(4):
            grc = jnp.concatenate([gr_ref[4 * s + q] for q in range(4)], axis=1)
            gic = jnp.concatenate([gi_ref[4 * s + q] for q in range(4)], axis=1)
            uc = u_ref[:, s * 128:(s + 1) * 128]
            dus.append(_dot3(_dot_nt, grc, br_ref[s]) + _dot3(_dot_nt, gic, bi_ref[s]))
            dbr_ref[s] += _dot3(_dot_tn, uc, grc)
            dbi_ref[s] += _dot3(_dot_tn, uc, gic)
        du = jnp.concatenate(dus, axis=1) + dd_ref[...]
        du_ref[...] = _to_streams(du, pm_ref[...]).reshape(du_ref.shape)

    whole = pl.BlockSpec((4, 128, 512), lambda i: (0, 0, 0))
    st = pl.BlockSpec((NLB, tm, 128), lambda i: (0, i, 0))
    half = pl.BlockSpec((tm, DSS), lambda i: (i, 0))
    du, dbr, dbi = _pc(body, name=f"ssm_in_bwd_l{layer}", grid=(T // tm,),
                       in_specs=[st, st, half, pl.BlockSpec((tm, tm), lambda i: (0, 0)), whole, whole, half],
                       out_specs=[_stream_block(tm, DSS), whole, whole],
                       out_shape=[S((16, T // 16, DSS), F32), S((4, 128, 512), F32), S((4, 128, 512), F32)],
                       compiler_params=_cp(1))(gr, gi, u, _perm_matrix(tm), bre, bim, du_direct)
    return du.reshape(T, DSS), dbr, dbi


def _mix_out(outs, lses, ssm, h, attn_g, ssm_g, post_g, wout, layer):
    T = h.shape[0]
    tm = TM

    def body(o1, o2, o3, l1, l2, l3, s_ref, h_ref, ag_ref, sg_ref, pg_ref, w_ref, ho_ref, at_ref, ls_ref, mx_ref, mo_ref):
        la, lb, lc = l1[...], l2[...], l3[...]
        m = jnp.maximum(jnp.maximum(la, lb), lc)
        wa, wb, wc = jnp.exp(la - m), jnp.exp(lb - m), jnp.exp(lc - m)
        zs = wa + wb + wc
        attn = (wa * o1[...] + wb * o2[...] + wc * o3[...]) / zs
        at_ref[...] = attn
        ls_ref[...] = m + jnp.log(zs)
        mixed = jnp.concatenate([_rms_fwd(attn, ag_ref[...]), _rms_fwd(s_ref[...], sg_ref[...])], axis=1).astype(BF16)
        mx_ref[...] = mixed
        mo = sum(_dot(mixed[:, j * 256:(j + 1) * 256], w_ref[j]) for j in range(NSH))
        mo_ref[...] = mo
        ho_ref[...] = h_ref[...] + _rms_fwd(mo, pg_ref[...])

    row = pl.BlockSpec((tm, D), lambda i: (i, 0))
    half = pl.BlockSpec((tm, DA), lambda i: (i, 0))
    return _pc(body, name=f"mix_out_l{layer}", grid=(T // tm,),
               in_specs=[half] * 7 + [row, _gain_spec(DA, layer), _gain_spec(DSS, layer), _gain_spec(D, layer),
                                      pl.BlockSpec((NSH, None, 256, D), lambda i: (0, 0, 0, 0))],
               out_specs=[row, half, half, row, row],
               out_shape=[S((T, D), F32), S((T, DA), F32), S((T, DA), F32), S((T, D), BF16), S((T, D), F32)],
               compiler_params=_cp(1))(*outs, *lses, ssm, h, attn_g, ssm_g, post_g, wout)


def _mix_out_bwd(dout, mo, attn, ssm, attn_g, ssm_g, post_g, wout, layer):
    T = dout.shape[0]
    tm = TMB
    head_sum = jnp.asarray(np.kron(np.eye(NH, dtype=np.float32), np.ones((HD, HD), np.float32)), BF16)

    def body(do_ref, mo_ref, at_ref, s_ref, ag_ref, sg_ref, pg_ref, w_ref, e_ref,
             da_ref, ds_ref, dl_ref, dmo_ref, dpg_ref, dag_ref, dsg_ref):
        i = pl.program_id(0)

        @pl.when(i == 0)
        def _():
            dpg_ref[...] = jnp.zeros_like(dpg_ref)
            dag_ref[...] = jnp.zeros_like(dag_ref)
            dsg_ref[...] = jnp.zeros_like(dsg_ref)

        dmo, dpg = _rms_bwd(do_ref[...], mo_ref[...], pg_ref[...])
        dpg_ref[...] += dpg
        dmob = dmo.astype(BF16)
        dmo_ref[...] = dmob
        dmix = jnp.concatenate([_dot_nt(dmob, w_ref[j]) for j in range(NSH)], axis=1)
        attn = at_ref[...]
        dat, dag = _rms_bwd(dmix[:, :DA], attn, ag_ref[...])
        dss, dsg = _rms_bwd(dmix[:, DA:], s_ref[...], sg_ref[...])
        dag_ref[...] += dag
        dsg_ref[...] += dsg
        da_ref[...] = dat
        ds_ref[...] = dss
        prod = dat * attn
        p1 = prod.astype(BF16)
        r1 = prod - p1.astype(F32)
        p2 = r1.astype(BF16)
        p3 = (r1 - p2.astype(F32)).astype(BF16)
        e = e_ref[...]
        dl_ref[...] = _dot(p1, e) + _dot(p2, e) + _dot(p3, e)

    row = pl.BlockSpec((tm, D), lambda i: (i, 0))
    half = pl.BlockSpec((tm, DA), lambda i: (i, 0))
    return _pc(body, name=f"mix_out_bwd_l{layer}", grid=(T // tm,),
               in_specs=[row, row, half, half, _gain_spec(DA, layer), _gain_spec(DSS, layer), _gain_spec(D, layer),
                         pl.BlockSpec((NSH, None, 256, D), lambda i: (0, 0, 0, 0)),
                         pl.BlockSpec((DA, DA), lambda i: (0, 0))],
               out_specs=[half, half, half, row, _row_acc_spec(D), _row_acc_spec(DA), _row_acc_spec(DSS)],
               out_shape=[S((T, DA), F32)] * 3 + [S((T, D), BF16), S((1, D), F32), S((1, DA), F32), S((1, DSS), F32)],
               compiler_params=_cp(1))(dout, mo, attn, ssm, attn_g, ssm_g, post_g, wout, head_sum)


def _ple_fwd(h, p3, wup, wgate, post_g, layer):
    T = h.shape[0]
    tm = TM

    def body(h_ref, p_ref, wu_ref, wg_ref, g_ref, ho_ref, e_ref, gt_ref):
        hv = h_ref[...]
        hb = hv.astype(BF16)
        pb = p_ref[...].astype(BF16)
        gte = sum(_dot(hb[:, j * 256:(j + 1) * 256], wg_ref[j]) for j in range(NSH))
        e = jnp.concatenate([_dot(pb, wu_ref[j]) for j in range(NSH)], axis=1)
        e_ref[...] = e
        gt_ref[...] = gte
        ho_ref[...] = hv + _rms_fwd(e * jax.nn.sigmoid(gte), g_ref[...])

    row = pl.BlockSpec((tm, D), lambda i: (i, 0))
    return _pc(body, name=f"ple_fwd_l{layer}", grid=(T // tm,),
               in_specs=[row, pl.BlockSpec((None, tm, PLE), lambda i: (layer, i, 0)),
                         pl.BlockSpec((NSH, None, PLE, 256), lambda i: (0, 0, 0, 0)),
                         pl.BlockSpec((NSH, None, 256, D), lambda i: (0, 0, 0, 0)), _gain_spec(D, layer)],
               out_specs=[row, row, row], out_shape=[S((T, D), F32)] * 3,
               compiler_params=_cp(1))(h, p3, wup, wgate, post_g)


def _ple_bwd(dout, e, gte, wgate, post_g, layer):
    T = dout.shape[0]
    tm = TMB

    def body(do_ref, e_ref, gt_ref, wg_ref, g_ref, dh_ref, de_ref, dgt_ref, dg_ref):
        i = pl.program_id(0)

        @pl.when(i == 0)
        def _():
            dg_ref[...] = jnp.zeros_like(dg_ref)

        ev = e_ref[...]
        sg = jax.nn.sigmoid(gt_ref[...])
        do = do_ref[...]
        dple, dg = _rms_bwd(do, ev * sg, g_ref[...])
        dg_ref[...] += dg
        de = (dple * sg).astype(BF16)
        for j in range(NSH):
            de_ref[j] = de[:, j * 256:(j + 1) * 256]
        dgb = (dple * ev * sg * (1.0 - sg)).astype(BF16)
        dgt_ref[...] = dgb
        dh_ref[...] = do + jnp.concatenate([_dot_nt(dgb, wg_ref[j]) for j in range(NSH)], axis=1)

    row = pl.BlockSpec((tm, D), lambda i: (i, 0))
    return _pc(body, name=f"ple_bwd_l{layer}", grid=(T // tm,),
               in_specs=[row, row, row, pl.BlockSpec((NSH, None, 256, D), lambda i: (0, 0, 0, 0)), _gain_spec(D, layer)],
               out_specs=[row, pl.BlockSpec((NSH, tm, 256), lambda i: (0, i, 0)), row, _row_acc_spec(D)],
               out_shape=[S((T, D), F32), S((NSH, T, 256), BF16), S((T, D), BF16), S((1, D), F32)],
               compiler_params=_cp(1))(dout, e, gte, wgate, post_g)


def _loss_head(h, target):
    T = h.shape[0]
    tm = TM

    def body(h_ref, t_ref, dy_ref, l_ref):
        i = pl.program_id(0)

        @pl.when(i == 0)
        def _():
            l_ref[...] = jnp.zeros_like(l_ref)

        err = h_ref[...] - t_ref[...]
        dy_ref[...] = err * (1.0 / D)
        l_ref[...] += jnp.broadcast_to((0.5 / D) * jnp.sum(err * err), (1, 128))

    row = pl.BlockSpec((tm, D), lambda i: (i, 0))
    return _pc(body, name="loss_head", grid=(T // tm,), in_specs=[row, row],
               out_specs=[row, pl.BlockSpec((1, 128), lambda i: (0, 0))],
               out_shape=[S((T, D), F32), S((1, 128), F32)], compiler_params=_cp(1))(h, target)


def _local_step(x, p3, pos_col, target, weights_of, Sm):
    L = p3.shape[0]
    g3 = {n: Sm[n].reshape(L, 1, -1) for n in ("ffn1_pre_g", "ffn1_post_g", "mix_pre_g", "attn_norm_g", "ssm_norm_g",
                                                "mix_post_g", "ffn2_pre_g", "ffn2_post_g", "ple_post_g", "ssm_b_glu", "ssm_d")}
    rot = _rot_tables(pos_col)
    prep_names = ("ssm_lam_re", "ssm_lam_im", "ssm_log_dt", "ssm_b_re", "ssm_b_im", "ssm_c_re", "ssm_c_im")

    saved = []
    h = x
    for l in range(L):
        W = weights_of(l, h)
        sv = {"h0": h, "W": W}
        h, sv["a1"], sv["b1"], sv["f1"], sv["xn1"] = _ffn_fwd(
            h, g3["ffn1_pre_g"], g3["ffn1_post_g"], W["ffn1_w_gate"], W["ffn1_w_up"], W["ffn1_w_down"], l, "1")
        sv["h1"] = h
        P, sv["ain"] = _mix_proj(h, g3["mix_pre_g"], W["w_in"], rot, l)
        sv["P"] = P
        ol = [_attn_fwd(P, d, l) for d in PATTERN_DILATIONS]
        prep, sv["prep_vjp"] = jax.vjp(_ssm_prep, *[Sm[n][l] for n in prep_names])
        a_re, a_im, bre, bim, cre, cim = prep
        sv["prep"] = prep
        sv["u"], bur, bui = _ssm_in(P, bre, bim, l)
        xr, xi = _scan(bur, bui, a_re, a_im, False, l)
        sv["xr"], sv["xi"] = xr, xi
        ssm, sv["y"], sv["z"] = _ssm_out(xr, xi, sv["u"], cre, cim, g3["ssm_d"], W["ssm_w_glu"], g3["ssm_b_glu"], l)
        sv["ssm"] = ssm
        h, sv["attn"], sv["lse"], sv["mixed"], sv["mo"] = _mix_out(
            [o for o, _ in ol], [s for _, s in ol], ssm, h, g3["attn_norm_g"], g3["ssm_norm_g"], g3["mix_post_g"],
            W["w_out"], l)
        sv["h2"] = h
        h, sv["a2"], sv["b2"], sv["f2"], sv["xn2"] = _ffn_fwd(
            h, g3["ffn2_pre_g"], g3["ffn2_post_g"], W["ffn2_w_gate"], W["ffn2_w_up"], W["ffn2_w_down"], l, "2")
        sv["h3"] = h
        h, sv["e"], sv["gte"] = _ple_fwd(h, p3, W["ple_w_up"], W["ple_w_gate"], g3["ple_post_g"], l)
        saved.append(sv)

    dh, loss = _loss_head(h, target)

    G = {n: lax.empty((NSH, L, r, c), BF16) for n, r, c in BIG}
    sg = {n: [None] * L for n in SMALL}
    whole = lambda j: (0, 0)
    shard = lambda j: (j, 0)
    kcol = lambda j: (0, j)
    for l in reversed(range(L)):
        sv = saved[l]
        W = sv["W"]
        dh, de, dgte, sg["ple_post_g"][l] = _ple_bwd(dh, sv["e"], sv["gte"], W["ple_w_gate"], g3["ple_post_g"], l)
        G["ple_w_up"] = _dw(p3[l][None], de, G["ple_w_up"], l, PLE, 256, whole, shard, f"dw_ple_up_l{l}")
        G["ple_w_gate"] = _dw(sv["h3"][None], dgte[None], G["ple_w_gate"], l, 256, D, kcol, whole, f"dw_ple_gate_l{l}")
        dh, df, da, db, hh, sg["ffn2_pre_g"][l], sg["ffn2_post_g"][l] = _ffn_bwd(
            dh, sv["h2"], sv["f2"], sv["a2"], sv["b2"], g3["ffn2_pre_g"], g3["ffn2_post_g"],
            W["ffn2_w_gate"], W["ffn2_w_up"], W["ffn2_w_down"], l, "2")
        G["ffn2_w_gate"] = _dw(da, sv["xn2"][None], G["ffn2_w_gate"], l, DFS, D, shard, whole, f"dw_ffn2_gate_l{l}")
        G["ffn2_w_up"] = _dw(db, sv["xn2"][None], G["ffn2_w_up"], l, DFS, D, shard, whole, f"dw_ffn2_up_l{l}")
        G["ffn2_w_down"] = _dw(hh, df[None], G["ffn2_w_down"], l, DFS, D, shard, whole, f"dw_ffn2_down_l{l}")
        a_re, a_im, bre, bim, cre, cim = sv["prep"]
        dattn, dssm, delta, dmo, sg["mix_post_g"][l], sg["attn_norm_g"][l], sg["ssm_norm_g"][l] = _mix_out_bwd(
            dh, sv["mo"], sv["attn"], sv["ssm"], g3["attn_norm_g"], g3["ssm_norm_g"], g3["mix_post_g"], W["w_out"], l)
        G["w_out"] = _dw(sv["mixed"][None], dmo[None], G["w_out"], l, 256, D, kcol, whole, f"dw_out_l{l}")
        gnr, gni, du_direct, dz, yg, sg["ssm_b_glu"][l], dd, dcre, dcim = _ssm_out_bwd(
            dssm, sv["y"], sv["z"], sv["xr"], sv["xi"], sv["u"], cre, cim, g3["ssm_d"], W["ssm_w_glu"], l)
        sg["ssm_d"][l] = dd.reshape(Sm["ssm_d"].shape[1:])
        G["ssm_w_glu"] = _dw(yg[None], dz[None], G["ssm_w_glu"], l, 128, DSS, kcol, whole, f"dw_glu_l{l}")
        gr, gi = _scan(gnr, gni, a_re, a_im, True, l)
        dar, dai = _ssm_da(gr, gi, sv["xr"], sv["xi"], l)
        du, dbre, dbim = _ssm_in_bwd(gr, gi, sv["u"], bre, bim, du_direct, l)
        for n, g in zip(prep_names, sv["prep_vjp"]((dar, dai, dbre, dbim, dcre, dcim))):
            sg[n][l] = g
        acc = None
        for d in PATTERN_DILATIONS:
            acc = _attn_bwd(sv["P"], dattn, sv["lse"], delta, acc, d, l)
        dh, dP, sg["mix_pre_g"][l] = _mix_proj_bwd(acc[0], acc[1], acc[2], du, dh, sv["h1"], g3["mix_pre_g"],
                                                   W["w_in"], rot, l)
        G["w_in"] = _dw(sv["ain"][None], dP, G["w_in"], l, D, DA, whole, shard, f"dw_in_l{l}")
        dh, df, da, db, hh, sg["ffn1_pre_g"][l], sg["ffn1_post_g"][l] = _ffn_bwd(
            dh, sv["h0"], sv["f1"], sv["a1"], sv["b1"], g3["ffn1_pre_g"], g3["ffn1_post_g"],
            W["ffn1_w_gate"], W["ffn1_w_up"], W["ffn1_w_down"], l, "1")
        G["ffn1_w_gate"] = _dw(da, sv["xn1"][None], G["ffn1_w_gate"], l, DFS, D, shard, whole, f"dw_ffn1_gate_l{l}")
        G["ffn1_w_up"] = _dw(db, sv["xn1"][None], G["ffn1_w_up"], l, DFS, D, shard, whole, f"dw_ffn1_up_l{l}")
        G["ffn1_w_down"] = _dw(hh, df[None], G["ffn1_w_down"], l, DFS, D, shard, whole, f"dw_ffn1_down_l{l}")

    small = {n: jnp.stack([g.reshape(Sm[n].shape[1:]) for g in sg[n]]) for n in SMALL}
    return loss, dh, G, small


HBM_SPEC = pl.BlockSpec(memory_space=pltpu.HBM)


def _place():
    x, y, c = lax.axis_index("x"), lax.axis_index("y"), lax.axis_index("c")
    chips = [(1 - x, y), (x, 1 - y), (1 - x, 1 - y)]
    return x, y, c, chips


def _comm_params():
    return pltpu.CompilerParams(vmem_limit_bytes=VMEM_LIMIT)


def _gather_weights(ws):
    n = len(ws)

    def body(*refs):
        ins, outs = refs[:n], refs[n:2 * n]
        s_ici, r_ici, s_d2d, r_d2d, s_loc = refs[2 * n:]
        x, y, c, chips = _place()
        me = 2 * x + y

        def half(ref, t, hc):
            r2 = ws[t].shape[1] // 2
            return ref.at[:, pl.ds(hc * r2, r2), :]

        def ici(t, k, src_chip, to):
            j = 2 * src_chip[0] + src_chip[1]
            src = half(ins[t], t, c) if to is not None else half(outs[t].at[j], t, c)
            return pltpu.make_async_remote_copy(src_ref=src, dst_ref=half(outs[t].at[j], t, c),
                                                send_sem=s_ici.at[3 * t + k], recv_sem=r_ici.at[3 * t + k],
                                                device_id=to if to is not None else (x, y, c), device_id_type=MESH)

        def d2d(t, k, hc):
            j = 2 * chips[k][0] + chips[k][1]
            r = half(outs[t].at[j], t, hc)
            return pltpu.make_async_remote_copy(src_ref=r, dst_ref=r, send_sem=s_d2d.at[3 * t + k],
                                                recv_sem=r_d2d.at[3 * t + k], device_id=(x, y, 1 - c),
                                                device_id_type=MESH)

        own = [pltpu.make_async_copy(ins[t], outs[t].at[me], s_loc.at[t]) for t in range(n)]
        for cp in own:
            cp.start()
        sends = [ici(t, k, (x, y), (*chips[k], c)) for t in range(n) for k in range(3)]
        for cp in sends:
            cp.start()
        passed = []
        for t in range(n):
            for k in range(3):
                ici(t, k, chips[k], None).wait_recv()
                passed.append(d2d(t, k, c))
                passed[-1].start()
        for t in range(n):
            for k in range(3):
                d2d(t, k, 1 - c).wait_recv()
        for cp in sends + passed:
            cp.wait_send()
        for cp in own:
            cp.wait()

    return _pc(body, name="gather_weights", in_specs=[HBM_SPEC] * n, out_specs=[HBM_SPEC] * n,
               out_shape=[S((NSH,) + w.shape, w.dtype) for w in ws],
               scratch_shapes=[pltpu.SemaphoreType.DMA((3 * n,))] * 4 + [pltpu.SemaphoreType.DMA((n,))],
               compiler_params=_comm_params())(*ws)


SEM_SPEC = pl.BlockSpec(memory_space=pltpu.SEMAPHORE)
ANY_SPEC = pl.BlockSpec(memory_space=pl.ANY)
SPLIT_EFFECT = pltpu.SideEffectType.DATAFLOW_SIDE_EFFECTING


def _in_hbm(t):
    return pltpu.with_memory_space_constraint(t, pltpu.HBM)


def _place_own(ws, layer):
    n = len(ws)

    def body(*refs):
        ins, outs, sems = refs[:n], refs[n:2 * n], refs[2 * n]
        x, y, _, _ = _place()
        cps = [pltpu.make_async_copy(ins[t], outs[t].at[2 * x + y], sems.at[t]) for t in range(n)]
        for cp in cps:
            cp.start()
        for cp in cps:
            cp.wait()

    return _pc(body, name=f"gather_place_own_l{layer}", in_specs=[HBM_SPEC] * n, out_specs=[HBM_SPEC] * n,
               out_shape=[S((NSH,) + w.shape, w.dtype) for w in ws],
               scratch_shapes=[pltpu.SemaphoreType.DMA((n,))], compiler_params=_comm_params())(*ws)


def _gather_start(ws, lands, after, layer):
    n = len(ws)

    def body(*refs):
        ins, lz = refs[:n], refs[n:2 * n]
        s_sem, r_sem = refs[2 * n + 1], refs[2 * n + 2]
        token = refs[-1]
        x, y, c, chips = _place()
        for t in range(n):
            for k in range(3):
                pltpu.make_async_remote_copy(src_ref=ins[t], dst_ref=lz[t].at[2 * x + y], send_sem=s_sem.at[3 * t + k],
                                             recv_sem=r_sem.at[3 * t + k], device_id=(*chips[k], c),
                                             device_id_type=MESH).start()
        token[...] = jnp.zeros_like(token)

    hbm = [pltpu.HBM(w.shape, w.dtype) for w in ws] + [pltpu.HBM(z.shape, z.dtype) for z in lands]
    out = _pc(body, name=f"gather_start_l{layer}",
              out_shape=(pltpu.SemaphoreType.DMA((3 * n,)), pltpu.SemaphoreType.DMA((3 * n,)), *hbm, S((8, 128), F32)),
              in_specs=[HBM_SPEC] * (2 * n) + [ANY_SPEC],
              out_specs=(SEM_SPEC, SEM_SPEC, *([HBM_SPEC] * (2 * n)), pl.BlockSpec(memory_space=pltpu.VMEM)),
              input_output_aliases={i: 2 + i for i in range(2 * n)},
              compiler_params=pltpu.CompilerParams(has_side_effects=SPLIT_EFFECT))(
                  *[_in_hbm(w) for w in ws], *[_in_hbm(z) for z in lands], after)
    return out[0], out[1], out[2:2 + n], out[2 + n:2 + 2 * n], out[-1]


def _gather_wait(s_sem, r_sem, ws, lands, after, layer):
    n = len(ws)

    def body(*refs):
        ins, lz = refs[:n], refs[n:2 * n]
        s_ref, r_ref = refs[2 * n], refs[2 * n + 1]
        x, y, c, chips = _place()
        for t in range(n):
            for k in range(3):
                cp = pltpu.make_async_remote_copy(src_ref=ins[t], dst_ref=lz[t].at[2 * x + y], send_sem=s_ref.at[3 * t + k],
                                                  recv_sem=r_ref.at[3 * t + k], device_id=(*chips[k], c),
                                                  device_id_type=MESH)
                cp.wait_send()
                cp.wait_recv()

    hbm = [pltpu.HBM(w.shape, w.dtype) for w in ws] + [pltpu.HBM(z.shape, z.dtype) for z in lands]
    out = _pc(body, name=f"gather_wait_l{layer}", out_shape=tuple(hbm),
              in_specs=[HBM_SPEC] * (2 * n) + [SEM_SPEC, SEM_SPEC, ANY_SPEC], out_specs=tuple([HBM_SPEC] * (2 * n)),
              input_output_aliases={i: i for i in range(2 * n)},
              compiler_params=pltpu.CompilerParams(has_side_effects=SPLIT_EFFECT))(*ws, *lands, s_sem, r_sem, after)
    return out[n:]


def _swap_halves(gs):
    n = len(gs)

    def body(*refs):
        ins, outs = refs[:n], refs[n:2 * n]
        s_sem, r_sem = refs[2 * n:]
        x, y, c, _ = _place()
        cps = []
        for t in range(n):
            r2 = gs[t].shape[2] // 2
            cps.append(pltpu.make_async_remote_copy(
                src_ref=ins[t].at[:, :, pl.ds((1 - c) * r2, r2), :], dst_ref=outs[t], send_sem=s_sem.at[t],
                recv_sem=r_sem.at[t], device_id=(x, y, 1 - c), device_id_type=MESH))
            cps[-1].start()
        for cp in cps:
            cp.wait_recv()
        for cp in cps:
            cp.wait_send()

    return _pc(body, name="grad_swap_halves", in_specs=[HBM_SPEC] * n, out_specs=[HBM_SPEC] * n,
               out_shape=[S(g.shape[:2] + (g.shape[2] // 2, g.shape[3]), g.dtype) for g in gs],
               scratch_shapes=[pltpu.SemaphoreType.DMA((n,))] * 2, compiler_params=_comm_params())(*gs)


def _add_half(g, landed, c_arr, name):
    _, L, r2, cols = landed.shape

    def body(c_ref, g_ref, l_ref, o_ref):
        o_ref[...] = (g_ref[...].astype(F32) + l_ref[...].astype(F32)).astype(BF16)

    gs = pltpu.PrefetchScalarGridSpec(
        num_scalar_prefetch=1, grid=(NSH, L),
        in_specs=[pl.BlockSpec((None, None, r2, cols), lambda j, l, c: (j, l, c[0], 0)),
                  pl.BlockSpec((None, None, r2, cols), lambda j, l, c: (j, l, 0, 0))],
        out_specs=pl.BlockSpec((None, None, r2, cols), lambda j, l, c: (j, l, 0, 0)))
    return _pc(body, name=name, grid_spec=gs, out_shape=S(landed.shape, BF16), compiler_params=_cp(2))(c_arr, g, landed)


def _send_shards(ps):
    n = len(ps)

    def body(*refs):
        ins, outs = refs[:n], refs[n:2 * n]
        s_sem, r_sem = refs[2 * n:]
        x, y, c, chips = _place()
        cps = []
        for t in range(n):
            for k in range(3):
                cps.append(pltpu.make_async_remote_copy(
                    src_ref=ins[t].at[2 * chips[k][0] + chips[k][1]], dst_ref=outs[t].at[k],
                    send_sem=s_sem.at[3 * t + k], recv_sem=r_sem.at[3 * t + k], device_id=(*chips[k], c),
                    device_id_type=MESH))
                cps[-1].start()
        for cp in cps:
            cp.wait_recv()
        for cp in cps:
            cp.wait_send()

    return _pc(body, name="grad_send_shards", in_specs=[HBM_SPEC] * n, out_specs=[HBM_SPEC] * n,
               out_shape=[S((3,) + p.shape[1:], p.dtype) for p in ps],
               scratch_shapes=[pltpu.SemaphoreType.DMA((3 * n,))] * 2, compiler_params=_comm_params())(*ps)


def _sum_shards(part, landed, me_arr, c_arr, name):
    _, L, r2, cols = landed.shape

    def body(me_ref, c_ref, p_ref, l_ref, o_ref):
        o_ref[...] = ((p_ref[...].astype(F32) + l_ref[0].astype(F32)) + l_ref[1].astype(F32)) + l_ref[2].astype(F32)

    gs = pltpu.PrefetchScalarGridSpec(
        num_scalar_prefetch=2, grid=(L,),
        in_specs=[pl.BlockSpec((None, None, r2, cols), lambda l, me, c: (me[0], l, 0, 0)),
                  pl.BlockSpec((3, None, r2, cols), lambda l, me, c: (0, l, 0, 0))],
        out_specs=pl.BlockSpec((None, r2, cols), lambda l, me, c: (l, c[0], 0)))
    return _pc(body, name=name, grid_spec=gs, out_shape=S((L, 2 * r2, cols), F32),
               compiler_params=_cp(1))(me_arr, c_arr, part, landed)


def _share_halves(bufs):
    n = len(bufs)

    def body(*refs):
        ins, outs = refs[:n], refs[n:2 * n]
        s_sem, r_sem = refs[2 * n:]
        x, y, c, _ = _place()
        cps = []
        for t in range(n):
            r2 = bufs[t].shape[1] // 2
            cps.append(pltpu.make_async_remote_copy(
                src_ref=ins[t].at[:, pl.ds(c * r2, r2), :], dst_ref=outs[t].at[:, pl.ds(c * r2, r2), :],
                send_sem=s_sem.at[t], recv_sem=r_sem.at[t], device_id=(x, y, 1 - c), device_id_type=MESH))
            cps[-1].start()
        for cp in cps:
            cp.wait_recv()
        for cp in cps:
            cp.wait_send()

    return _pc(body, name="grad_share_halves", in_specs=[HBM_SPEC] * n, out_specs=[HBM_SPEC] * n,
               out_shape=[S(b.shape, b.dtype) for b in bufs], input_output_aliases={t: t for t in range(n)},
               scratch_shapes=[pltpu.SemaphoreType.DMA((n,))] * 2, compiler_params=_comm_params())(*bufs)


def _gather_small(v):
    nr = v.shape[0]

    def body(v_ref, out_ref, send_sems, recv_sems, local_sem):
        x, y, c, chips = _place()
        me, sibling = (x, y, c), (x, y, 1 - c)

        def rows(px, py, pc):
            return out_ref.at[pl.ds((4 * px + 2 * py + pc) * nr, nr), :]

        def copy(k, block, to, src=None):
            return pltpu.make_async_remote_copy(src_ref=rows(*block) if src is None else src, dst_ref=rows(*block),
                                                send_sem=send_sems.at[k], recv_sem=recv_sems.at[k], device_id=to,
                                                device_id_type=MESH)

        mine = pltpu.make_async_copy(v_ref, rows(*me), local_sem)
        mine.start()
        first = [copy(0, me, sibling, src=v_ref)]
        first += [copy(1 + j, me, (*chip, c), src=v_ref) for j, chip in enumerate(chips)]
        for cp in first:
            cp.start()
        passed = [copy(4 + j, (*chip, c), sibling) for j, chip in enumerate(chips)]
        for j, chip in enumerate(chips):
            copy(1 + j, (*chip, c), me).wait_recv()
            passed[j].start()
        copy(0, sibling, me).wait_recv()
        for j, chip in enumerate(chips):
            copy(4 + j, (*chip, 1 - c), me).wait_recv()
        for cp in first + passed:
            cp.wait_send()
        mine.wait()

    vm = pl.BlockSpec(memory_space=pltpu.VMEM)
    return _pc(body, name="gather_small_grads", in_specs=[vm], out_specs=vm, out_shape=S((8 * nr, 128), F32),
               scratch_shapes=[pltpu.SemaphoreType.DMA((7,)), pltpu.SemaphoreType.DMA((7,)), pltpu.SemaphoreType.DMA],
               compiler_params=_comm_params())(v)


def _adamw_math(w, g, m, v):
    m2 = ADAM_B1 * m + (1.0 - ADAM_B1) * g
    v2 = ADAM_B2 * v + (1.0 - ADAM_B2) * (g * g)
    m_hat = m2 / (1.0 - ADAM_B1 ** ADAM_STEP)
    v_hat = v2 / (1.0 - ADAM_B2 ** ADAM_STEP)
    return -ADAM_LR * (m_hat / (jnp.sqrt(v_hat) + ADAM_EPS) + ADAM_WD * w), m2, v2


def _adamw(w, g, m, v, name):
    L, R, C = w.shape
    rb = R // 2 if R >= 512 else R

    def body(w_ref, g_ref, m_ref, v_ref, d_ref, m2_ref, v2_ref):
        d_ref[...], m2_ref[...], v2_ref[...] = _adamw_math(w_ref[...], g_ref[...], m_ref[...], v_ref[...])

    blk = pl.BlockSpec((None, rb, C), lambda l, r: (l, r, 0))
    return _pc(body, name=name, grid=(L, R // rb), in_specs=[blk] * 4, out_specs=[blk] * 3,
               out_shape=[S(w.shape, F32)] * 3, compiler_params=_cp(2))(w, g, m, v)


def _adamw_small(gathered, w, m, v):
    nr = w.shape[0]
    rb = nr // 5

    def body(a_ref, w_ref, m_ref, v_ref, g_ref, d_ref, m2_ref, v2_ref):
        g = a_ref[0]
        for k in range(1, 8):
            g = g + a_ref[k]
        g_ref[...] = g
        d_ref[...], m2_ref[...], v2_ref[...] = _adamw_math(w_ref[...], g, m_ref[...], v_ref[...])

    blk = pl.BlockSpec((rb, 128), lambda i: (i, 0))
    return _pc(body, name="adamw_small", grid=(nr // rb,), in_specs=[pl.BlockSpec((8, rb, 128), lambda i: (0, i, 0))] + [blk] * 3,
               out_specs=[blk] * 4, out_shape=[S((nr, 128), F32)] * 4, compiler_params=_cp(1))(gathered, w, m, v)


SMALL_ROWS = 4520


def _pack(arrs):
    flat = jnp.concatenate([a.reshape(-1) for a in arrs])
    return jnp.pad(flat, (0, SMALL_ROWS * 128 - flat.shape[0])).reshape(SMALL_ROWS, 128)


def _unpack(packed, like):
    flat = packed.reshape(-1)
    out, off = [], 0
    for a in like:
        out.append(flat[off:off + a.size].reshape(a.shape))
        off += a.size
    return out


def kernel(x, p, positions, ffn1_pre_g, ffn1_w_gate, ffn1_w_up, ffn1_w_down, ffn1_post_g, mix_pre_g, w_in, attn_norm_g, ssm_lam_re, ssm_lam_im, ssm_log_dt, ssm_b_re, ssm_b_im, ssm_c_re, ssm_c_im, ssm_d, ssm_w_glu, ssm_b_glu, ssm_norm_g, w_out, mix_post_g, ffn2_pre_g, ffn2_w_gate, ffn2_w_up, ffn2_w_down, ffn2_post_g, ple_w_up, ple_w_gate, ple_post_g, loss_target, m_ffn1_pre_g, m_ffn1_w_gate, m_ffn1_w_up, m_ffn1_w_down, m_ffn1_post_g, m_mix_pre_g, m_w_in, m_attn_norm_g, m_ssm_lam_re, m_ssm_lam_im, m_ssm_log_dt, m_ssm_b_re, m_ssm_b_im, m_ssm_c_re, m_ssm_c_im, m_ssm_d, m_ssm_w_glu, m_ssm_b_glu, m_ssm_norm_g, m_w_out, m_mix_post_g, m_ffn2_pre_g, m_ffn2_w_gate, m_ffn2_w_up, m_ffn2_w_down, m_ffn2_post_g, m_ple_w_up, m_ple_w_gate, m_ple_post_g, v_ffn1_pre_g, v_ffn1_w_gate, v_ffn1_w_up, v_ffn1_w_down, v_ffn1_post_g, v_mix_pre_g, v_w_in, v_attn_norm_g, v_ssm_lam_re, v_ssm_lam_im, v_ssm_log_dt, v_ssm_b_re, v_ssm_b_im, v_ssm_c_re, v_ssm_c_im, v_ssm_d, v_ssm_w_glu, v_ssm_b_glu, v_ssm_norm_g, v_w_out, v_mix_post_g, v_ffn2_pre_g, v_ffn2_w_gate, v_ffn2_w_up, v_ffn2_w_down, v_ffn2_post_g, v_ple_w_up, v_ple_w_gate, v_ple_post_g):
    a = dict(locals())
    T = x.shape[1]
    big_names = [n for n, _, _ in BIG]
    for n in TRANSPOSED:
        for pre in ("", "m_", "v_"):
            a[pre + n] = jnp.swapaxes(a[pre + n], 1, 2)

    own = [a[n].astype(BF16) for n in big_names]
    n_layers = own[0].shape[0]
    per_layer = [[w[l:l + 1] for w in own] for l in range(n_layers)]
    first = dict(zip(big_names, _gather_weights(per_layer[0])))
    pending, anchor, queued_behind = {}, jnp.zeros((), F32), first[big_names[0]]
    for l in range(1, n_layers):
        s_sem, r_sem, ws_thru, lands_thru, token = _gather_start(per_layer[l], _place_own(per_layer[l], l),
                                                                 queued_behind, l)
        pending[l] = (s_sem, r_sem, ws_thru, lands_thru)
        anchor = anchor + token[0, 0]
        queued_behind = token

    def weights_of(l, after):
        if l == 0:
            return first
        return dict(zip(big_names, _gather_wait(*pending[l], after, l)))

    Sm = {n: a[n] for n in SMALL}
    Sm["ffn1_pre_g"] = Sm["ffn1_pre_g"] + anchor

    pos = jnp.broadcast_to(positions.reshape(1, T, 1).astype(F32), (1, T, 128))
    loss, gx, G, small = _local_step(_reorder(x, True, "to_streams_x")[0], _reorder(p[:, 0], True, "to_streams_p"),
                                     _reorder(pos, True, "to_streams_pos")[0, :, :1],
                                     _reorder(loss_target, True, "to_streams_target")[0], weights_of, Sm)
    gx = _reorder(gx[None], False, "to_time_grad_x")

    c_arr = lax.axis_index("c").astype(jnp.int32).reshape(1)
    me_arr = (2 * lax.axis_index("x") + lax.axis_index("y")).astype(jnp.int32).reshape(1)
    gs = [G[n] for n in big_names]
    landed = _swap_halves(gs)
    parts = [_add_half(g, la, c_arr, f"grad_add_half_{n}") for g, la, n in zip(gs, landed, big_names)]
    landed = _send_shards(parts)
    halves = [_sum_shards(pt, la, me_arr, c_arr, f"grad_sum_shards_{n}") for pt, la, n in zip(parts, landed, big_names)]
    grads = dict(zip(big_names, _share_halves(halves)))

    small_g = _gather_small(_pack([small[n] for n in SMALL])).reshape(8, SMALL_ROWS, 128)
    sg, sd, sm, sv = _adamw_small(small_g, _pack([a[n] for n in SMALL]), _pack([a["m_" + n] for n in SMALL]),
                                  _pack([a["v_" + n] for n in SMALL]))
    like = [a[n] for n in SMALL]
    res = {}
    for n, g_, d_, m_, v_ in zip(SMALL, _unpack(sg, like), _unpack(sd, like), _unpack(sm, like), _unpack(sv, like)):
        res[n] = (g_, d_, m_, v_)
    for n in big_names:
        d_, m_, v_ = _adamw(a[n], grads[n], a["m_" + n], a["v_" + n], f"adamw_{n}")
        res[n] = (grads[n], d_, m_, v_)
        if n in TRANSPOSED:
            res[n] = tuple(jnp.swapaxes(t, 1, 2) for t in res[n])

    total = lax.psum(loss[0, 0], ("x", "y", "c"))
    return (total, gx, *[res[n][0] for n in WEIGHTS], *[res[n][1] for n in WEIGHTS],
            *[res[n][2] for n in WEIGHTS], *[res[n][3] for n in WEIGHTS])
```

```python
import functools
import math

import numpy as np
import jax
import jax.numpy as jnp
from jax import lax
from jax.experimental import pallas as pl
from jax.experimental.pallas import tpu as pltpu

F32 = jnp.float32
BF16 = jnp.bfloat16
S = jax.ShapeDtypeStruct
MESH = pl.DeviceIdType.MESH

D = 1024
DA = 512
DSS = 512
HD = 64
NH = 8
BAND = 128
NSH = 4
DFS = 704
PLE = 256
EPS = 1e-6
ROPE_THETA = 500000.0
PATTERN_DILATIONS = (1, 4, 16)
NLB = 16
ADAM_LR, ADAM_B1, ADAM_B2, ADAM_EPS, ADAM_WD, ADAM_STEP = 0.001, 0.9, 0.999, 1e-08, 0.01, 10

VMEM_LIMIT = 56 * 1024 * 1024
TM = 512
TMB = 256

BIG = (
    ("ffn1_w_gate", DFS, D), ("ffn1_w_up", DFS, D), ("ffn1_w_down", DFS, D),
    ("w_in", D, 512), ("ssm_w_glu", 128, 512), ("w_out", 256, D),
    ("ffn2_w_gate", DFS, D), ("ffn2_w_up", DFS, D), ("ffn2_w_down", DFS, D),
    ("ple_w_up", PLE, 256), ("ple_w_gate", 256, D),
)
TRANSPOSED = ("ffn1_w_gate", "ffn1_w_up", "ffn2_w_gate", "ffn2_w_up")
SMALL = ("ffn1_pre_g", "ffn1_post_g", "mix_pre_g", "attn_norm_g", "ssm_lam_re", "ssm_lam_im", "ssm_log_dt",
         "ssm_b_re", "ssm_b_im", "ssm_c_re", "ssm_c_im", "ssm_d", "ssm_b_glu", "ssm_norm_g", "mix_post_g",
         "ffn2_pre_g", "ffn2_post_g", "ple_post_g")
WEIGHTS = ("ffn1_pre_g", "ffn1_w_gate", "ffn1_w_up", "ffn1_w_down", "ffn1_post_g", "mix_pre_g", "w_in", "attn_norm_g",
           "ssm_lam_re", "ssm_lam_im", "ssm_log_dt", "ssm_b_re", "ssm_b_im", "ssm_c_re", "ssm_c_im", "ssm_d",
           "ssm_w_glu", "ssm_b_glu", "ssm_norm_g", "w_out", "mix_post_g", "ffn2_pre_g", "ffn2_w_gate", "ffn2_w_up",
           "ffn2_w_down", "ffn2_post_g", "ple_w_up", "ple_w_gate", "ple_post_g")


def _pc(body, **kw):
    return pl.pallas_call(body, **kw)


def _cp(n_grid):
    return pltpu.CompilerParams(dimension_semantics=("arbitrary",) * n_grid, vmem_limit_bytes=VMEM_LIMIT)


def _dot(a, b):
    return jnp.dot(a, b, preferred_element_type=F32)


def _dot_nt(a, b):
    return lax.dot_general(a, b, (((1,), (1,)), ((), ())), preferred_element_type=F32)


def _dot_tn(a, b):
    return lax.dot_general(a, b, (((0,), (0,)), ((), ())), preferred_element_type=F32)


def _split(a):
    hi = a.astype(BF16)
    return hi, (a - hi.astype(F32)).astype(BF16)


def _dot3(fn, a, b):
    ah, al = _split(a)
    bh, bl = _split(b)
    return fn(ah, bh) + fn(ah, bl) + fn(al, bh)


def _rms_fwd(x, g):
    r = lax.rsqrt(jnp.mean(x * x, axis=-1, keepdims=True) + EPS)
    return x * r * g


def _rms_bwd(dy, x, g):
    r = lax.rsqrt(jnp.mean(x * x, axis=-1, keepdims=True) + EPS)
    xr = x * r
    gd = dy * g
    dx = r * (gd - xr * jnp.mean(gd * xr, axis=-1, keepdims=True))
    dg = jnp.sum(dy * xr, axis=0, keepdims=True)
    return dx, dg


def _gelu(y):
    k = math.sqrt(2.0 / math.pi)
    return 0.5 * y * (1.0 + jnp.tanh(k * (y + 0.044715 * y * y * y)))


def _gelu_grad(y):
    k = math.sqrt(2.0 / math.pi)
    t = jnp.tanh(k * (y + 0.044715 * y * y * y))
    return 0.5 * (1.0 + t) + 0.5 * y * (1.0 - t * t) * k * (1.0 + 3 * 0.044715 * y * y)


def _gain_spec(n, layer):
    return pl.BlockSpec((None, 1, n), lambda *_: (layer, 0, 0))


def _row_acc_spec(n):
    return pl.BlockSpec((1, n), lambda *_: (0, 0))


def _rot_tables(pos_col):
    T = pos_col.shape[0]
    half = HD // 8
    inv = (ROPE_THETA ** (-np.arange(half, dtype=np.float32) * (2.0 / (2 * half)))).astype(np.float32)
    lane_freq = np.tile(np.concatenate([inv, inv, np.zeros(HD - 2 * half, np.float32)]), NH)[None, :]

    def body(p_ref, f_ref, c_ref, s1_ref, s2_ref):
        ang = p_ref[...] * f_ref[...]
        d = lax.broadcasted_iota(jnp.int32, ang.shape, 1) % HD
        cs = jnp.cos(ang)
        sn = jnp.sin(ang)
        c_ref[...] = jnp.where(d < 2 * half, cs, 1.0)
        s1_ref[...] = jnp.where(d < half, -sn, 0.0)
        s2_ref[...] = jnp.where((d >= half) & (d < 2 * half), sn, 0.0)

    tm = TM
    return _pc(body, name="rot_tables", grid=(T // tm,),
               in_specs=[pl.BlockSpec((tm, 1), lambda i: (i, 0)), pl.BlockSpec((1, DA), lambda i: (0, 0))],
               out_specs=[pl.BlockSpec((tm, DA), lambda i: (i, 0))] * 3,
               out_shape=[S((T, DA), F32)] * 3, compiler_params=_cp(1))(pos_col, jnp.asarray(lane_freq))


def _rot_fwd(t, c, s1, s2):
    return t * c + pltpu.roll(t, DA - 8, 1) * s1 + pltpu.roll(t, 8, 1) * s2


def _rot_bwd(g, c, s1, s2):
    return g * c + pltpu.roll(g * s1, 8, 1) + pltpu.roll(g * s2, DA - 8, 1)


def _ffn_fwd(h, pre_g, post_g, wg, wu, wd, layer, tag):
    T = h.shape[0]
    tm = TM
    nt = T // tm

    def body(h_ref, pg_ref, qg_ref, wg_ref, wu_ref, wd_ref, ho_ref, a_ref, b_ref, f_ref, xn_ref, xs, facc):
        j = pl.program_id(1)

        @pl.when(j == 0)
        def _():
            xb = _rms_fwd(h_ref[...], pg_ref[...]).astype(BF16)
            xs[...] = xb
            xn_ref[...] = xb
            facc[...] = jnp.zeros_like(facc)

        xb = xs[...]
        ab = _dot_nt(xb, wg_ref[...]).astype(BF16)
        bb = _dot_nt(xb, wu_ref[...]).astype(BF16)
        a_ref[...] = ab
        b_ref[...] = bb
        a = ab.astype(F32)
        hh = (a * jax.nn.sigmoid(a) * bb.astype(F32)).astype(BF16)
        facc[...] += _dot(hh, wd_ref[...])

        @pl.when(j == NSH - 1)
        def _():
            f = facc[...]
            f_ref[...] = f
            ho_ref[...] = h_ref[...] + 0.5 * _rms_fwd(f, qg_ref[...])

    row = pl.BlockSpec((tm, D), lambda i, j: (i, 0))
    act = pl.BlockSpec((None, tm, DFS), lambda i, j: (j, i, 0))
    wrow = pl.BlockSpec((None, None, DFS, D), lambda i, j: (j, 0, 0, 0))
    return _pc(body, name=f"ffn_fwd_{tag}_l{layer}", grid=(nt, NSH),
               in_specs=[row, _gain_spec(D, layer), _gain_spec(D, layer), wrow, wrow, wrow],
               out_specs=[row, act, act, row, row],
               out_shape=[S((T, D), F32), S((NSH, T, DFS), BF16), S((NSH, T, DFS), BF16), S((T, D), F32), S((T, D), BF16)],
               scratch_shapes=[pltpu.VMEM((tm, D), BF16), pltpu.VMEM((tm, D), F32)],
               compiler_params=_cp(2))(h, pre_g, post_g, wg, wu, wd)


def _ffn_bwd(dout, h, f, a, b, pre_g, post_g, wg, wu, wd, layer, tag):
    T = h.shape[0]
    tm = TM
    nt = T // tm

    def body(do_ref, h_ref, f_ref, a_ref, b_ref, pg_ref, qg_ref, wg_ref, wu_ref, wd_ref,
             dh_ref, df_ref, da_ref, db_ref, hh_ref, dpg_ref, dqg_ref, dfs, dxn):
        i = pl.program_id(0)
        j = pl.program_id(1)

        @pl.when((i == 0) & (j == 0))
        def _():
            dpg_ref[...] = jnp.zeros_like(dpg_ref)
            dqg_ref[...] = jnp.zeros_like(dqg_ref)

        @pl.when(j == 0)
        def _():
            df, dq = _rms_bwd(0.5 * do_ref[...], f_ref[...], qg_ref[...])
            dqg_ref[...] += dq
            dfb = df.astype(BF16)
            dfs[...] = dfb
            df_ref[...] = dfb
            dxn[...] = jnp.zeros_like(dxn)

        dhh = _dot_nt(dfs[...], wd_ref[...])
        av = a_ref[...].astype(F32)
        bv = b_ref[...].astype(F32)
        sg = jax.nn.sigmoid(av)
        sa = av * sg
        hh_ref[...] = (sa * bv).astype(BF16)
        dab = (dhh * bv * (sg * (1.0 + av * (1.0 - sg)))).astype(BF16)
        dbb = (dhh * sa).astype(BF16)
        da_ref[...] = dab
        db_ref[...] = dbb
        dxn[...] += _dot(dab, wg_ref[...]) + _dot(dbb, wu_ref[...])

        @pl.when(j == NSH - 1)
        def _():
            dx, dp = _rms_bwd(dxn[...], h_ref[...], pg_ref[...])
            dpg_ref[...] += dp
            dh_ref[...] = do_ref[...] + dx

    row = pl.BlockSpec((tm, D), lambda i, j: (i, 0))
    act = pl.BlockSpec((None, tm, DFS), lambda i, j: (j, i, 0))
    wrow = pl.BlockSpec((None, None, DFS, D), lambda i, j: (j, 0, 0, 0))
    return _pc(body, name=f"ffn_bwd_{tag}_l{layer}", grid=(nt, NSH),
               in_specs=[row, row, row, act, act, _gain_spec(D, layer), _gain_spec(D, layer), wrow, wrow, wrow],
               out_specs=[row, row, act, act, act, _row_acc_spec(D), _row_acc_spec(D)],
               out_shape=[S((T, D), F32), S((T, D), BF16), S((NSH, T, DFS), BF16), S((NSH, T, DFS), BF16),
                          S((NSH, T, DFS), BF16), S((1, D), F32), S((1, D), F32)],
               scratch_shapes=[pltpu.VMEM((tm, D), BF16), pltpu.VMEM((tm, D), F32)],
               compiler_params=_cp(2))(dout, h, f, a, b, pre_g, post_g, wg, wu, wd)


def _dw(A, B, buf, layer, kb, nb, a_idx, b_idx, name):
    T = A.shape[1]
    tt = 2 * TM if T % (2 * TM) == 0 else TM
    nt = T // tt

    def body(a_ref, b_ref, buf_ref, o_ref, acc):
        t = pl.program_id(1)

        @pl.when(t == 0)
        def _():
            acc[...] = jnp.zeros_like(acc)

        acc[...] += _dot_tn(a_ref[...].astype(BF16), b_ref[...].astype(BF16))

        @pl.when(t == nt - 1)
        def _():
            o_ref[...] = acc[...].astype(o_ref.dtype)

    return _pc(body, name=name, grid=(NSH, nt),
               in_specs=[pl.BlockSpec((None, tt, kb), lambda j, t: (a_idx(j)[0], t, a_idx(j)[1])),
                         pl.BlockSpec((None, tt, nb), lambda j, t: (b_idx(j)[0], t, b_idx(j)[1])),
                         pl.BlockSpec(memory_space=pl.ANY)],
               out_specs=pl.BlockSpec((None, None, kb, nb), lambda j, t: (j, layer, 0, 0)),
               out_shape=S(buf.shape, buf.dtype), input_output_aliases={2: 0},
               scratch_shapes=[pltpu.VMEM((kb, nb), F32)], compiler_params=_cp(2))(A, B, buf)


def _mix_proj(h, pre_g, win, rot, layer):
    T = h.shape[0]
    tm = TM

    def body(h_ref, g_ref, w_ref, c_ref, s1_ref, s2_ref, p_ref, xn_ref, xs):
        j = pl.program_id(1)

        @pl.when(j == 0)
        def _():
            xb = _rms_fwd(h_ref[...], g_ref[...]).astype(BF16)
            xs[...] = xb
            xn_ref[...] = xb

        o = _dot(xs[...], w_ref[...])

        @pl.when(j < 2)
        def _():
            p_ref[...] = _rot_fwd(o, c_ref[...], s1_ref[...], s2_ref[...])

        @pl.when(j >= 2)
        def _():
            p_ref[...] = o

    row = pl.BlockSpec((tm, D), lambda i, j: (i, 0))
    half = pl.BlockSpec((tm, DA), lambda i, j: (i, 0))
    return _pc(body, name=f"mix_proj_l{layer}", grid=(T // tm, NSH),
               in_specs=[row, _gain_spec(D, layer), pl.BlockSpec((None, None, D, DA), lambda i, j: (j, 0, 0, 0)),
                         half, half, half],
               out_specs=[pl.BlockSpec((None, tm, DA), lambda i, j: (j, i, 0)), row],
               out_shape=[S((NSH, T, DA), F32), S((T, D), BF16)],
               scratch_shapes=[pltpu.VMEM((tm, D), BF16)], compiler_params=_cp(2))(h, pre_g, win, *rot)


def _mix_proj_bwd(dq, dk, dv, du, dh_up, h, pre_g, win, rot, layer):
    T = h.shape[0]
    tm = TM

    def body(dq_ref, dk_ref, dv_ref, du_ref, up_ref, h_ref, g_ref, w_ref, c_ref, s1_ref, s2_ref,
             dh_ref, dp_ref, dg_ref, dps, dxn):
        i = pl.program_id(0)
        j = pl.program_id(1)

        @pl.when((i == 0) & (j == 0))
        def _():
            dg_ref[...] = jnp.zeros_like(dg_ref)

        @pl.when(j == 0)
        def _():
            dxn[...] = jnp.zeros_like(dxn)
            dps[...] = _rot_bwd(dq_ref[...], c_ref[...], s1_ref[...], s2_ref[...]).astype(BF16)

        @pl.when(j == 1)
        def _():
            dps[...] = _rot_bwd(dk_ref[...], c_ref[...], s1_ref[...], s2_ref[...]).astype(BF16)

        @pl.when(j == 2)
        def _():
            dps[...] = dv_ref[...].astype(BF16)

        @pl.when(j == 3)
        def _():
            dps[...] = du_ref[...].astype(BF16)

        dpb = dps[...]
        dp_ref[...] = dpb
        dxn[...] += _dot_nt(dpb, w_ref[...])

        @pl.when(j == NSH - 1)
        def _():
            dx, dg = _rms_bwd(dxn[...], h_ref[...], g_ref[...])
            dg_ref[...] += dg
            dh_ref[...] = up_ref[...] + dx

    row = pl.BlockSpec((tm, D), lambda i, j: (i, 0))
    half = pl.BlockSpec((tm, DA), lambda i, j: (i, 0))
    return _pc(body, name=f"mix_proj_bwd_l{layer}", grid=(T // tm, NSH),
               in_specs=[half, half, half, half, row, row, _gain_spec(D, layer),
                         pl.BlockSpec((None, None, D, DA), lambda i, j: (j, 0, 0, 0)), half, half, half],
               out_specs=[row, pl.BlockSpec((None, tm, DA), lambda i, j: (j, i, 0)), _row_acc_spec(D)],
               out_shape=[S((T, D), F32), S((NSH, T, DA), BF16), S((1, D), F32)],
               scratch_shapes=[pltpu.VMEM((tm, DA), BF16), pltpu.VMEM((tm, D), F32)],
               compiler_params=_cp(2))(dq, dk, dv, du, dh_up, h, pre_g, win, *rot)


def _stream_pos(d, axis):
    i = lax.broadcasted_iota(jnp.int32, (BAND, BAND), axis)
    if d == 16:
        return i
    if d == 4:
        return 4 * (i % 32) + i // 32
    return 16 * (i % 8) + i // 8


def _band_masks(b, d):
    qi, kj = _stream_pos(d, 0), _stream_pos(d, 1)
    return kj <= qi, (kj >= qi) & (b > 0)


def _pattern(d, T):
    n16 = T // 16
    if d == 16:
        return (16, n16, DA), (None, BAND, DA), lambda r, k: (r, k, 0)
    if d == 4:
        return (4, 4, n16, DA), (4, None, 32, DA), lambda r, k: (0, r, k, 0)
    return (16, n16, DA), (16, 8, DA), lambda r, k: (0, k, 0)


def _pattern_spec(d, T, kmap, lead=None):
    _, blk, idx = _pattern(d, T)
    if lead is None:
        return pl.BlockSpec(blk, lambda r, b: idx(r, kmap(b)))
    return pl.BlockSpec((None,) + blk, lambda r, b: (lead,) + idx(r, kmap(b)))


def _attn_fwd(P, d, layer):
    T = P.shape[1]
    nb = T // d // BAND
    vshape = _pattern(d, T)[0]
    Pv = P.reshape((NSH,) + vshape)
    scale = HD ** -0.5

    def body(q_ref, kp_ref, kc_ref, vp_ref, vc_ref, o_ref, l_ref, qs, ks, vs, osc, lsc):
        b = pl.program_id(1)
        flat = lambda ref: ref[...].reshape(BAND, DA).astype(BF16)
        qs[...] = flat(q_ref)
        ks[0:BAND, :] = flat(kp_ref)
        ks[BAND:, :] = flat(kc_ref)
        vs[0:BAND, :] = flat(vp_ref)
        vs[BAND:, :] = flat(vc_ref)
        mask_c, mask_p = _band_masks(b, d)
        mask = jnp.concatenate([mask_p, mask_c], axis=1)
        for hd in range(NH):
            sl = slice(hd * HD, (hd + 1) * HD)
            s = jnp.where(mask, _dot_nt(qs[:, sl], ks[:, sl]) * scale, -1e30)
            m = jnp.max(s, axis=-1, keepdims=True)
            e = jnp.exp(s - m)
            den = jnp.sum(e, axis=-1, keepdims=True)
            osc[:, sl] = _dot(e.astype(BF16), vs[:, sl]) / den
            lsc[:, sl] = jnp.broadcast_to(m + jnp.log(den), (BAND, HD))
        o_ref[...] = osc[...].reshape(o_ref.shape)
        l_ref[...] = lsc[...].reshape(l_ref.shape)

    cur = lambda b: b
    prev = lambda b: jnp.maximum(b - 1, 0)
    out = _pattern_spec(d, T, cur)
    o, l = _pc(body, name=f"attn_fwd_d{d}_l{layer}", grid=(d, nb),
               in_specs=[_pattern_spec(d, T, cur, 0), _pattern_spec(d, T, prev, 1), _pattern_spec(d, T, cur, 1),
                         _pattern_spec(d, T, prev, 2), _pattern_spec(d, T, cur, 2)],
               out_specs=[out, out], out_shape=[S(vshape, F32)] * 2,
               scratch_shapes=[pltpu.VMEM((BAND, DA), BF16)] + [pltpu.VMEM((2 * BAND, DA), BF16)] * 2
               + [pltpu.VMEM((BAND, DA), F32)] * 2,
               compiler_params=_cp(2))(Pv, Pv, Pv, Pv, Pv)
    return o.reshape(T, DA), l.reshape(T, DA)


def _attn_bwd(P, dO, lse, delta, acc, d, layer):
    T = P.shape[1]
    nb = T // d // BAND
    vshape = _pattern(d, T)[0]
    Pv = P.reshape((NSH,) + vshape)
    scale = HD ** -0.5
    first = acc is None

    def body(*refs):
        q_ref, kp_ref, kc_ref, vp_ref, vc_ref, do_ref, l_ref, dl_ref = refs[:8]
        if first:
            dq_ref, dk_ref, dv_ref = refs[8:11]
        else:
            aq_ref, ak_ref, av_ref, dq_ref, dk_ref, dv_ref = refs[8:14]
        qs, dos, ks, vs, ls, dls, oq, ok, ov, ck, cv = refs[-11:]
        b = pl.program_id(1)
        flat = lambda ref: ref[...].reshape(BAND, DA)

        @pl.when(b == 0)
        def _():
            ck[...] = jnp.zeros_like(ck)
            cv[...] = jnp.zeros_like(cv)

        @pl.when(b < nb)
        def _():
            qs[...] = flat(q_ref).astype(BF16)
            dos[...] = flat(do_ref).astype(BF16)
            ks[0:BAND, :] = flat(kp_ref).astype(BF16)
            ks[BAND:, :] = flat(kc_ref).astype(BF16)
            vs[0:BAND, :] = flat(vp_ref).astype(BF16)
            vs[BAND:, :] = flat(vc_ref).astype(BF16)
            ls[...] = flat(l_ref)
            dls[...] = flat(dl_ref)
            mask_c, mask_p = _band_masks(b, d)
            mask = jnp.concatenate([mask_p, mask_c], axis=1)
            for hd in range(NH):
                sl = slice(hd * HD, (hd + 1) * HD)
                one = slice(hd * HD, hd * HD + 1)
                q, do, kk = qs[:, sl], dos[:, sl], ks[:, sl]
                p = jnp.where(mask, jnp.exp(_dot_nt(q, kk) * scale - ls[:, one]), 0.0)
                ds = (p * (_dot_nt(do, vs[:, sl]) - dls[:, one]) * scale).astype(BF16)
                oq[:, sl] = _dot(ds, kk)
                dk2 = _dot_tn(ds, q)
                dv2 = _dot_tn(p.astype(BF16), do)
                ok[:, sl] = ck[:, sl] + dk2[0:BAND]
                ov[:, sl] = cv[:, sl] + dv2[0:BAND]
                ck[:, sl] = dk2[BAND:]
                cv[:, sl] = dv2[BAND:]
            if first:
                dq_ref[...] = oq[...].reshape(dq_ref.shape)
                dk_ref[...] = ok[...].reshape(dk_ref.shape)
                dv_ref[...] = ov[...].reshape(dv_ref.shape)
            else:
                dq_ref[...] = aq_ref[...] + oq[...].reshape(dq_ref.shape)
                dk_ref[...] = ak_ref[...] + ok[...].reshape(dk_ref.shape)
                dv_ref[...] = av_ref[...] + ov[...].reshape(dv_ref.shape)

        @pl.when(b == nb)
        def _():
            if first:
                dk_ref[...] = ck[...].reshape(dk_ref.shape)
                dv_ref[...] = cv[...].reshape(dv_ref.shape)
            else:
                dk_ref[...] = ak_ref[...] + ck[...].reshape(dk_ref.shape)
                dv_ref[...] = av_ref[...] + cv[...].reshape(dv_ref.shape)

    qb = lambda b: jnp.minimum(b, nb - 1)
    qprev = lambda b: jnp.maximum(qb(b) - 1, 0)
    kb = lambda b: jnp.maximum(b - 1, 0)
    qrow = _pattern_spec(d, T, qb)
    krow = _pattern_spec(d, T, kb)
    view = lambda t: t.reshape(vshape)
    ins = [Pv, Pv, Pv, Pv, Pv, view(dO), view(lse), view(delta)]
    specs = [_pattern_spec(d, T, qb, 0), _pattern_spec(d, T, qprev, 1), _pattern_spec(d, T, qb, 1),
             _pattern_spec(d, T, qprev, 2), _pattern_spec(d, T, qb, 2), qrow, qrow, qrow]
    if not first:
        ins += [view(t) for t in acc]
        specs += [qrow, krow, krow]
    dq, dk, dv = _pc(body, name=f"attn_bwd_d{d}_l{layer}", grid=(d, nb + 1), in_specs=specs,
                     out_specs=[qrow, krow, krow], out_shape=[S(vshape, F32)] * 3,
                     scratch_shapes=[pltpu.VMEM((BAND, DA), BF16)] * 2 + [pltpu.VMEM((2 * BAND, DA), BF16)] * 2
                     + [pltpu.VMEM((BAND, DA), F32)] * 7,
                     compiler_params=_cp(2))(*ins)
    return dq.reshape(T, DA), dk.reshape(T, DA), dv.reshape(T, DA)


def _ssm_prep(lam_re, lam_im, log_dt, b_re, b_im, c_re, c_im):
    dt = jnp.exp(log_dt)[:, None]
    er = jnp.exp(lam_re * dt)
    a_re = er * jnp.cos(lam_im * dt)
    a_im = er * jnp.sin(lam_im * dt)
    nr, ni = a_re - 1.0, a_im
    den = lam_re * lam_re + lam_im * lam_im
    cr = (nr * lam_re + ni * lam_im) / den
    ci = (ni * lam_re - nr * lam_im) / den
    bbr = cr[..., None] * b_re - ci[..., None] * b_im
    bbi = cr[..., None] * b_im + ci[..., None] * b_re
    eye = jnp.eye(8, dtype=F32)

    def bblock(bb):
        t = bb.reshape(4, 8, 64, 16).transpose(0, 1, 3, 2)
        return (t[:, :, :, None, :] * eye[None, :, None, :, None]).reshape(4, 128, 512)

    def cblock(cc):
        t = cc.reshape(4, 8, 16, 64).transpose(0, 1, 3, 2)
        return (t[:, :, :, None, :] * eye[None, :, None, :, None]).reshape(4, 512, 128)

    return (a_re.reshape(NLB, 1, 128), a_im.reshape(NLB, 1, 128), bblock(bbr), bblock(bbi), cblock(c_re), cblock(c_im))


def _perm_matrix(tm):
    n = tm // 16
    pm = np.zeros((tm, tm), np.float32)
    for r in range(16):
        pm[16 * np.arange(n) + r, r * n + np.arange(n)] = 1.0
    return jnp.asarray(pm, BF16)


def _pieces(x):
    p1 = x.astype(BF16)
    r1 = x - p1.astype(F32)
    p2 = r1.astype(BF16)
    return p1, p2, (r1 - p2.astype(F32)).astype(BF16)


def _to_time(x, pm):
    return sum(_dot(pm, p) for p in _pieces(x))


def _to_streams(x, pm):
    return sum(_dot_tn(pm, p) for p in _pieces(x))


def _stream_block(tm, cols, lead=None):
    if lead is None:
        return pl.BlockSpec((16, tm // 16, cols), lambda i: (0, i, 0))
    return pl.BlockSpec((None, 16, tm // 16, cols), lambda i: (lead, 0, i, 0))


def _reorder(t3, to_streams, name):
    B, T, C = t3.shape
    tm = TM

    def body(x_ref, pm_ref, o_ref):
        if to_streams:
            o_ref[...] = _to_streams(x_ref[...], pm_ref[...]).reshape(o_ref.shape)
        else:
            o_ref[...] = _to_time(x_ref[...].reshape(tm, C), pm_ref[...])

    time_blk = pl.BlockSpec((None, tm, C), lambda b, i: (b, i, 0))
    stream_blk = pl.BlockSpec((None, 16, tm // 16, C), lambda b, i: (b, 0, i, 0))
    src = t3 if to_streams else t3.reshape(B, 16, T // 16, C)
    out = _pc(body, name=name, grid=(B, T // tm),
              in_specs=[time_blk if to_streams else stream_blk, pl.BlockSpec((tm, tm), lambda b, i: (0, 0))],
              out_specs=stream_blk if to_streams else time_blk,
              out_shape=S((B, 16, T // 16, C) if to_streams else (B, T, C), F32),
              compiler_params=_cp(2))(src, _perm_matrix(tm))
    return out.reshape(B, T, C)


def _ssm_in(P, bre, bim, layer):
    T = P.shape[1]
    tm = TM

    def body(u_ref, pm_ref, br_ref, bi_ref, un_ref, or_ref, oi_ref):
        u = _to_time(u_ref[...].reshape(tm, DSS), pm_ref[...])
        un_ref[...] = u
        for s in range(4):
            uc = u[:, s * 128:(s + 1) * 128]
            r = _dot3(_dot, uc, br_ref[s])
            m = _dot3(_dot, uc, bi_ref[s])
            for q in range(4):
                or_ref[4 * s + q] = r[:, q * 128:(q + 1) * 128]
                oi_ref[4 * s + q] = m[:, q * 128:(q + 1) * 128]

    whole = pl.BlockSpec((4, 128, 512), lambda i: (0, 0, 0))
    st = pl.BlockSpec((NLB, tm, 128), lambda i: (0, i, 0))
    return _pc(body, name=f"ssm_in_l{layer}", grid=(T // tm,),
               in_specs=[_stream_block(tm, DSS, 3), pl.BlockSpec((tm, tm), lambda i: (0, 0)), whole, whole],
               out_specs=[pl.BlockSpec((tm, DSS), lambda i: (i, 0)), st, st],
               out_shape=[S((T, DSS), F32)] + [S((NLB, T, 128), F32)] * 2,
               compiler_params=_cp(1))(P.reshape(NSH, 16, T // 16, DSS), _perm_matrix(tm), bre, bim)


def _scan(br, bi, a_re, a_im, reverse, layer):
    T = br.shape[1]
    nbk = 2
    tt = min(T, 1024)
    nT = T // tt
    ntile = tt // 8
    sgn = -1.0 if reverse else 1.0
    last = 0 if reverse else 7

    def body(br_ref, bi_ref, ar_ref, ai_ref, xr_ref, xi_ref, cr, ci):
        @pl.when(pl.program_id(1) == 0)
        def _():
            cr[...] = jnp.zeros_like(cr)
            ci[...] = jnp.zeros_like(ci)

        row = lax.broadcasted_iota(jnp.int32, (8, 128), 0)
        consts = []
        for k in range(nbk):
            a1r = jnp.broadcast_to(ar_ref[k], (8, 128))
            a1i = sgn * jnp.broadcast_to(ai_ref[k], (8, 128))
            pows = [(a1r, a1i)]
            for _ in range(7):
                pr, pi_ = pows[-1]
                pows.append((a1r * pr - a1i * pi_, a1r * pi_ + a1i * pr))
            rounds = []
            for s in (1, 2, 4):
                inside = (row <= 7 - s) if reverse else (row >= s)
                rounds.append((jnp.where(inside, pows[s - 1][0], 0.0), jnp.where(inside, pows[s - 1][1], 0.0)))
            cmr, cmi = jnp.zeros((8, 128), F32), jnp.zeros((8, 128), F32)
            for r in range(8):
                e = (7 - r) if reverse else r
                cmr = jnp.where(row == r, pows[e][0], cmr)
                cmi = jnp.where(row == r, pows[e][1], cmi)
            consts.append((rounds, cmr, cmi))

        def tile(i, carry):
            j = (ntile - 1 - i) if reverse else i
            rows = pl.ds(pl.multiple_of(j * 8, 8), 8)
            out = []
            for k in range(nbk):
                rounds, cmr, cmi = consts[k]
                xr = br_ref[k, rows, :]
                xi = bi_ref[k, rows, :]
                for (mr, mi), s in zip(rounds, (1, 2, 4)):
                    sh = (8 - s) if reverse else s
                    rr = pltpu.roll(xr, sh, 0)
                    ri = pltpu.roll(xi, sh, 0)
                    xr, xi = xr + (mr * rr - mi * ri), xi + (mr * ri + mi * rr)
                c_r, c_i = carry[k]
                xr, xi = xr + (cmr * c_r - cmi * c_i), xi + (cmr * c_i + cmi * c_r)
                xr_ref[k, rows, :] = xr
                xi_ref[k, rows, :] = xi
                out.append((jnp.broadcast_to(xr[last:last + 1, :], (8, 128)),
                            jnp.broadcast_to(xi[last:last + 1, :], (8, 128))))
            return tuple(out)

        carry = lax.fori_loop(0, ntile, tile, tuple((cr[k], ci[k]) for k in range(nbk)), unroll=2)
        for k in range(nbk):
            cr[k] = carry[k][0]
            ci[k] = carry[k][1]

    tmap = (lambda t: nT - 1 - t) if reverse else (lambda t: t)
    st = pl.BlockSpec((nbk, tt, 128), lambda i, t: (i, tmap(t), 0))
    av = pl.BlockSpec((nbk, 1, 128), lambda i, t: (i, 0, 0))
    return _pc(body, name=f"scan_{'bwd' if reverse else 'fwd'}_l{layer}", grid=(NLB // nbk, nT),
               in_specs=[st, st, av, av], out_specs=[st, st], out_shape=[S((NLB, T, 128), F32)] * 2,
               scratch_shapes=[pltpu.VMEM((nbk, 8, 128), F32)] * 2, compiler_params=_cp(2))(br, bi, a_re, a_im)


def _ssm_out(xr, xi, u, cre, cim, dvec, wglu, bglu, layer):
    T = u.shape[0]
    tm = TM

    def body(xr_ref, xi_ref, u_ref, pm_ref, cr_ref, ci_ref, d_ref, w_ref, bg_ref, s_ref, y_ref, z_ref):
        ys = []
        for s in range(4):
            xrc = jnp.concatenate([xr_ref[4 * s + q] for q in range(4)], axis=1)
            xic = jnp.concatenate([xi_ref[4 * s + q] for q in range(4)], axis=1)
            ys.append(_dot3(_dot, xrc, cr_ref[s]) - _dot3(_dot, xic, ci_ref[s]))
        y = jnp.concatenate(ys, axis=1) + d_ref[...] * u_ref[...]
        yg = _gelu(y)
        ygb = yg.astype(BF16)
        z = bg_ref[...] + sum(_dot(ygb[:, j * 128:(j + 1) * 128], w_ref[j]) for j in range(NSH))
        y_ref[...] = y
        z_ref[...] = z
        s_ref[...] = _to_streams(yg * jax.nn.sigmoid(z), pm_ref[...]).reshape(s_ref.shape)

    st = pl.BlockSpec((NLB, tm, 128), lambda i: (0, i, 0))
    cw = pl.BlockSpec((4, 512, 128), lambda i: (0, 0, 0))
    half = pl.BlockSpec((tm, DSS), lambda i: (i, 0))
    s, y, z = _pc(body, name=f"ssm_out_l{layer}", grid=(T // tm,),
                  in_specs=[st, st, half, pl.BlockSpec((tm, tm), lambda i: (0, 0)), cw, cw, _gain_spec(DSS, layer),
                            pl.BlockSpec((NSH, None, 128, DSS), lambda i: (0, 0, 0, 0)), _gain_spec(DSS, layer)],
                  out_specs=[_stream_block(tm, DSS), half, half],
                  out_shape=[S((16, T // 16, DSS), F32), S((T, DSS), F32), S((T, DSS), F32)],
                  compiler_params=_cp(1))(xr, xi, u, _perm_matrix(tm), cre, cim, dvec, wglu, bglu)
    return s.reshape(T, DSS), y, z


def _ssm_out_bwd(dssm, y, z, xr, xi, u, cre, cim, dvec, wglu, layer):
    T = u.shape[0]
    tm = TMB

    def body(ds_ref, pm_ref, y_ref, z_ref, xr_ref, xi_ref, u_ref, cr_ref, ci_ref, d_ref, w_ref,
             gr_ref, gi_ref, du_ref, dz_ref, yg_ref, dbg_ref, dd_ref, dcr_ref, dci_ref):
        i = pl.program_id(0)

        @pl.when(i == 0)
        def _():
            dbg_ref[...] = jnp.zeros_like(dbg_ref)
            dd_ref[...] = jnp.zeros_like(dd_ref)
            dcr_ref[...] = jnp.zeros_like(dcr_ref)
            dci_ref[...] = jnp.zeros_like(dci_ref)

        yv = y_ref[...]
        yg = _gelu(yv)
        sg = jax.nn.sigmoid(z_ref[...])
        ds = _to_time(ds_ref[...].reshape(tm, DSS), pm_ref[...])
        dz = ds * yg * sg * (1.0 - sg)
        dzb = dz.astype(BF16)
        dz_ref[...] = dzb
        yg_ref[...] = yg.astype(BF16)
        dbg_ref[...] += jnp.sum(dz, axis=0, keepdims=True)
        dyg = ds * sg + jnp.concatenate([_dot_nt(dzb, w_ref[j]) for j in range(NSH)], axis=1)
        dy = dyg * _gelu_grad(yv)
        u = u_ref[...]
        dd_ref[...] += jnp.sum(dy * u, axis=0, keepdims=True)
        du_ref[...] = dy * d_ref[...]
        for s in range(4):
            dyc = dy[:, s * 128:(s + 1) * 128]
            g_r = _dot3(_dot_nt, dyc, cr_ref[s])
            g_i = -_dot3(_dot_nt, dyc, ci_ref[s])
            for q in range(4):
                gr_ref[4 * s + q] = g_r[:, q * 128:(q + 1) * 128]
                gi_ref[4 * s + q] = g_i[:, q * 128:(q + 1) * 128]
            xrc = jnp.concatenate([xr_ref[4 * s + q] for q in range(4)], axis=1)
            xic = jnp.concatenate([xi_ref[4 * s + q] for q in range(4)], axis=1)
            dcr_ref[s] += _dot3(_dot_tn, xrc, dyc)
            dci_ref[s] -= _dot3(_dot_tn, xic, dyc)

    st = pl.BlockSpec((NLB, tm, 128), lambda i: (0, i, 0))
    cw = pl.BlockSpec((4, 512, 128), lambda i: (0, 0, 0))
    half = pl.BlockSpec((tm, DSS), lambda i: (i, 0))
    return _pc(body, name=f"ssm_out_bwd_l{layer}", grid=(T // tm,),
               in_specs=[_stream_block(tm, DSS), pl.BlockSpec((tm, tm), lambda i: (0, 0)), half, half, st, st, half,
                         cw, cw, _gain_spec(DSS, layer), pl.BlockSpec((NSH, None, 128, DSS), lambda i: (0, 0, 0, 0))],
               out_specs=[st, st, half, half, half, _row_acc_spec(DSS), _row_acc_spec(DSS), cw, cw],
               out_shape=[S((NLB, T, 128), F32)] * 2 + [S((T, DSS), F32), S((T, DSS), BF16), S((T, DSS), BF16),
                                                        S((1, DSS), F32), S((1, DSS), F32),
                                                        S((4, 512, 128), F32), S((4, 512, 128), F32)],
               compiler_params=_cp(1))(dssm.reshape(16, T // 16, DSS), _perm_matrix(tm), y, z, xr, xi, u, cre, cim,
                                       dvec, wglu)


def _ssm_da(gr, gi, xr, xi, layer):
    T = gr.shape[1]
    tb = 1024 if T % 1024 == 0 else T

    def body(gr_ref, gi_ref, xr_ref, xi_ref, dr_ref, di_ref, lr, li):
        t = pl.program_id(1)

        @pl.when(t == 0)
        def _():
            dr_ref[...] = jnp.zeros_like(dr_ref)
            di_ref[...] = jnp.zeros_like(di_ref)
            lr[...] = jnp.zeros_like(lr)
            li[...] = jnp.zeros_like(li)

        g_r, g_i, x_r, x_i = gr_ref[...], gi_ref[...], xr_ref[...], xi_ref[...]
        pr = pltpu.roll(x_r, 1, 0)
        pi_ = pltpu.roll(x_i, 1, 0)
        g0r, g0i = g_r[0:1, :], g_i[0:1, :]
        fr = lr[7:8, :] - x_r[tb - 1:tb, :]
        fi = li[7:8, :] - x_i[tb - 1:tb, :]
        dr_ref[...] += jnp.sum(g_r * pr + g_i * pi_, axis=0, keepdims=True) + g0r * fr + g0i * fi
        di_ref[...] += jnp.sum(g_i * pr - g_r * pi_, axis=0, keepdims=True) + g0i * fr - g0r * fi
        lr[...] = x_r[tb - 8:tb, :]
        li[...] = x_i[tb - 8:tb, :]

    st = pl.BlockSpec((None, tb, 128), lambda k, t: (k, t, 0))
    out = pl.BlockSpec((None, 1, 128), lambda k, t: (k, 0, 0))
    return _pc(body, name=f"ssm_da_l{layer}", grid=(NLB, T // tb), in_specs=[st] * 4, out_specs=[out, out],
               out_shape=[S((NLB, 1, 128), F32)] * 2, scratch_shapes=[pltpu.VMEM((8, 128), F32)] * 2,
               compiler_params=_cp(2))(gr, gi, xr, xi)


def _ssm_in_bwd(gr, gi, u, bre, bim, du_direct, layer):
    T = u.shape[0]
    tm = TM

    def body(gr_ref, gi_ref, u_ref, pm_ref, br_ref, bi_ref, dd_ref, du_ref, dbr_ref, dbi_ref):
        i = pl.program_id(0)

        @pl.when(i == 0)
        def _():
            dbr_ref[...] = jnp.zeros_like(dbr_ref)
            dbi_ref[...] = jnp.zeros_like(dbi_ref)

        dus = []
        for s in range(4):
            grc = jnp.concatenate([gr_ref[4 * s + q] for q in range(4)], axis=1)
            gic = jnp.concatenate([gi_ref[4 * s + q] for q in range(4)], axis=1)
            uc = u_ref[:, s * 128:(s + 1) * 128]
            dus.append(_dot3(_dot_nt, grc, br_ref[s]) + _dot3(_dot_nt, gic, bi_ref[s]))
            dbr_ref[s] += _dot3(_dot_tn, uc, grc)
            dbi_ref[s] += _dot3(_dot_tn, uc, gic)
        du = jnp.concatenate(dus, axis=1) + dd_ref[...]
        du_ref[...] = _to_streams(du, pm_ref[...]).reshape(du_ref.shape)

    whole = pl.BlockSpec((4, 128, 512), lambda i: (0, 0, 0))
    st = pl.BlockSpec((NLB, tm, 128), lambda i: (0, i, 0))
    half = pl.BlockSpec((tm, DSS), lambda i: (i, 0))
    du, dbr, dbi = _pc(body, name=f"ssm_in_bwd_l{layer}", grid=(T // tm,),
                       in_specs=[st, st, half, pl.BlockSpec((tm, tm), lambda i: (0, 0)), whole, whole, half],
                       out_specs=[_stream_block(tm, DSS), whole, whole],
                       out_shape=[S((16, T // 16, DSS), F32), S((4, 128, 512), F32), S((4, 128, 512), F32)],
                       compiler_params=_cp(1))(gr, gi, u, _perm_matrix(tm), bre, bim, du_direct)
    return du.reshape(T, DSS), dbr, dbi


def _mix_out(outs, lses, ssm, h, attn_g, ssm_g, post_g, wout, layer):
    T = h.shape[0]
    tm = TM

    def body(o1, o2, o3, l1, l2, l3, s_ref, h_ref, ag_ref, sg_ref, pg_ref, w_ref, ho_ref, at_ref, ls_ref, mx_ref, mo_ref):
        la, lb, lc = l1[...], l2[...], l3[...]
        m = jnp.maximum(jnp.maximum(la, lb), lc)
        wa, wb, wc = jnp.exp(la - m), jnp.exp(lb - m), jnp.exp(lc - m)
        zs = wa + wb + wc
        attn = (wa * o1[...] + wb * o2[...] + wc * o3[...]) / zs
        at_ref[...] = attn
        ls_ref[...] = m + jnp.log(zs)
        mixed = jnp.concatenate([_rms_fwd(attn, ag_ref[...]), _rms_fwd(s_ref[...], sg_ref[...])], axis=1).astype(BF16)
        mx_ref[...] = mixed
        mo = sum(_dot(mixed[:, j * 256:(j + 1) * 256], w_ref[j]) for j in range(NSH))
        mo_ref[...] = mo
        ho_ref[...] = h_ref[...] + _rms_fwd(mo, pg_ref[...])

    row = pl.BlockSpec((tm, D), lambda i: (i, 0))
    half = pl.BlockSpec((tm, DA), lambda i: (i, 0))
    return _pc(body, name=f"mix_out_l{layer}", grid=(T // tm,),
               in_specs=[half] * 7 + [row, _gain_spec(DA, layer), _gain_spec(DSS, layer), _gain_spec(D, layer),
                                      pl.BlockSpec((NSH, None, 256, D), lambda i: (0, 0, 0, 0))],
               out_specs=[row, half, half, row, row],
               out_shape=[S((T, D), F32), S((T, DA), F32), S((T, DA), F32), S((T, D), BF16), S((T, D), F32)],
               compiler_params=_cp(1))(*outs, *lses, ssm, h, attn_g, ssm_g, post_g, wout)


def _mix_out_bwd(dout, mo, attn, ssm, attn_g, ssm_g, post_g, wout, layer):
    T = dout.shape[0]
    tm = TMB
    head_sum = jnp.asarray(np.kron(np.eye(NH, dtype=np.float32), np.ones((HD, HD), np.float32)), BF16)

    def body(do_ref, mo_ref, at_ref, s_ref, ag_ref, sg_ref, pg_ref, w_ref, e_ref,
             da_ref, ds_ref, dl_ref, dmo_ref, dpg_ref, dag_ref, dsg_ref):
        i = pl.program_id(0)

        @pl.when(i == 0)
        def _():
            dpg_ref[...] = jnp.zeros_like(dpg_ref)
            dag_ref[...] = jnp.zeros_like(dag_ref)
            dsg_ref[...] = jnp.zeros_like(dsg_ref)

        dmo, dpg = _rms_bwd(do_ref[...], mo_ref[...], pg_ref[...])
        dpg_ref[...] += dpg
        dmob = dmo.astype(BF16)
        dmo_ref[...] = dmob
        dmix = jnp.concatenate([_dot_nt(dmob, w_ref[j]) for j in range(NSH)], axis=1)
        attn = at_ref[...]
        dat, dag = _rms_bwd(dmix[:, :DA], attn, ag_ref[...])
        dss, dsg = _rms_bwd(dmix[:, DA:], s_ref[...], sg_ref[...])
        dag_ref[...] += dag
        dsg_ref[...] += dsg
        da_ref[...] = dat
        ds_ref[...] = dss
        prod = dat * attn
        p1 = prod.astype(BF16)
        r1 = prod - p1.astype(F32)
        p2 = r1.astype(BF16)
        p3 = (r1 - p2.astype(F32)).astype(BF16)
        e = e_ref[...]
        dl_ref[...] = _dot(p1, e) + _dot(p2, e) + _dot(p3, e)

    row = pl.BlockSpec((tm, D), lambda i: (i, 0))
    half = pl.BlockSpec((tm, DA), lambda i: (i, 0))
    return _pc(body, name=f"mix_out_bwd_l{layer}", grid=(T // tm,),
               in_specs=[row, row, half, half, _gain_spec(DA, layer), _gain_spec(DSS, layer), _gain_spec(D, layer),
                         pl.BlockSpec((NSH, None, 256, D), lambda i: (0, 0, 0, 0)),
                         pl.BlockSpec((DA, DA), lambda i: (0, 0))],
               out_specs=[half, half, half, row, _row_acc_spec(D), _row_acc_spec(DA), _row_acc_spec(DSS)],
               out_shape=[S((T, DA), F32)] * 3 + [S((T, D), BF16), S((1, D), F32), S((1, DA), F32), S((1, DSS), F32)],
               compiler_params=_cp(1))(dout, mo, attn, ssm, attn_g, ssm_g, post_g, wout, head_sum)


def _ple_fwd(h, p3, wup, wgate, post_g, layer):
    T = h.shape[0]
    tm = TM

    def body(h_ref, p_ref, wu_ref, wg_ref, g_ref, ho_ref, e_ref, gt_ref):
        hv = h_ref[...]
        hb = hv.astype(BF16)
        pb = p_ref[...].astype(BF16)
        gte = sum(_dot(hb[:, j * 256:(j + 1) * 256], wg_ref[j]) for j in range(NSH))
        e = jnp.concatenate([_dot(pb, wu_ref[j]) for j in range(NSH)], axis=1)
        e_ref[...] = e
        gt_ref[...] = gte
        ho_ref[...] = hv + _rms_fwd(e * jax.nn.sigmoid(gte), g_ref[...])

    row = pl.BlockSpec((tm, D), lambda i: (i, 0))
    return _pc(body, name=f"ple_fwd_l{layer}", grid=(T // tm,),
               in_specs=[row, pl.BlockSpec((None, tm, PLE), lambda i: (layer, i, 0)),
                         pl.BlockSpec((NSH, None, PLE, 256), lambda i: (0, 0, 0, 0)),
                         pl.BlockSpec((NSH, None, 256, D), lambda i: (0, 0, 0, 0)), _gain_spec(D, layer)],
               out_specs=[row, row, row], out_shape=[S((T, D), F32)] * 3,
               compiler_params=_cp(1))(h, p3, wup, wgate, post_g)


def _ple_bwd(dout, e, gte, wgate, post_g, layer):
    T = dout.shape[0]
    tm = TMB

    def body(do_ref, e_ref, gt_ref, wg_ref, g_ref, dh_ref, de_ref, dgt_ref, dg_ref):
        i = pl.program_id(0)

        @pl.when(i == 0)
        def _():
            dg_ref[...] = jnp.zeros_like(dg_ref)

        ev = e_ref[...]
        sg = jax.nn.sigmoid(gt_ref[...])
        do = do_ref[...]
        dple, dg = _rms_bwd(do, ev * sg, g_ref[...])
        dg_ref[...] += dg
        de = (dple * sg).astype(BF16)
        for j in range(NSH):
            de_ref[j] = de[:, j * 256:(j + 1) * 256]
        dgb = (dple * ev * sg * (1.0 - sg)).astype(BF16)
        dgt_ref[...] = dgb
        dh_ref[...] = do + jnp.concatenate([_dot_nt(dgb, wg_ref[j]) for j in range(NSH)], axis=1)

    row = pl.BlockSpec((tm, D), lambda i: (i, 0))
    return _pc(body, name=f"ple_bwd_l{layer}", grid=(T // tm,),
               in_specs=[row, row, row, pl.BlockSpec((NSH, None, 256, D), lambda i: (0, 0, 0, 0)), _gain_spec(D, layer)],
               out_specs=[row, pl.BlockSpec((NSH, tm, 256), lambda i: (0, i, 0)), row, _row_acc_spec(D)],
               out_shape=[S((T, D), F32), S((NSH, T, 256), BF16), S((T, D), BF16), S((1, D), F32)],
               compiler_params=_cp(1))(dout, e, gte, wgate, post_g)


def _loss_head(h, target):
    T = h.shape[0]
    tm = TM

    def body(h_ref, t_ref, dy_ref, l_ref):
        i = pl.program_id(0)

        @pl.when(i == 0)
        def _():
            l_ref[...] = jnp.zeros_like(l_ref)

        err = h_ref[...] - t_ref[...]
        dy_ref[...] = err * (1.0 / D)
        l_ref[...] += jnp.broadcast_to((0.5 / D) * jnp.sum(err * err), (1, 128))

    row = pl.BlockSpec((tm, D), lambda i: (i, 0))
    return _pc(body, name="loss_head", grid=(T // tm,), in_specs=[row, row],
               out_specs=[row, pl.BlockSpec((1, 128), lambda i: (0, 0))],
               out_shape=[S((T, D), F32), S((1, 128), F32)], compiler_params=_cp(1))(h, target)


def _local_step(x, p3, pos_col, target, weights_of, Sm):
    L = p3.shape[0]
    g3 = {n: Sm[n].reshape(L, 1, -1) for n in ("ffn1_pre_g", "ffn1_post_g", "mix_pre_g", "attn_norm_g", "ssm_norm_g",
                                                "mix_post_g", "ffn2_pre_g", "ffn2_post_g", "ple_post_g", "ssm_b_glu", "ssm_d")}
    rot = _rot_tables(pos_col)
    prep_names = ("ssm_lam_re", "ssm_lam_im", "ssm_log_dt", "ssm_b_re", "ssm_b_im", "ssm_c_re", "ssm_c_im")

    saved = []
    h = x
    for l in range(L):
        W = weights_of(l, h)
        sv = {"h0": h, "W": W}
        h, sv["a1"], sv["b1"], sv["f1"], sv["xn1"] = _ffn_fwd(
            h, g3["ffn1_pre_g"], g3["ffn1_post_g"], W["ffn1_w_gate"], W["ffn1_w_up"], W["ffn1_w_down"], l, "1")
        sv["h1"] = h
        P, sv["ain"] = _mix_proj(h, g3["mix_pre_g"], W["w_in"], rot, l)
        sv["P"] = P
        ol = [_attn_fwd(P, d, l) for d in PATTERN_DILATIONS]
        prep, sv["prep_vjp"] = jax.vjp(_ssm_prep, *[Sm[n][l] for n in prep_names])
        a_re, a_im, bre, bim, cre, cim = prep
        sv["prep"] = prep
        sv["u"], bur, bui = _ssm_in(P, bre, bim, l)
        xr, xi = _scan(bur, bui, a_re, a_im, False, l)
        sv["xr"], sv["xi"] = xr, xi
        ssm, sv["y"], sv["z"] = _ssm_out(xr, xi, sv["u"], cre, cim, g3["ssm_d"], W["ssm_w_glu"], g3["ssm_b_glu"], l)
        sv["ssm"] = ssm
        h, sv["attn"], sv["lse"], sv["mixed"], sv["mo"] = _mix_out(
            [o for o, _ in ol], [s for _, s in ol], ssm, h, g3["attn_norm_g"], g3["ssm_norm_g"], g3["mix_post_g"],
            W["w_out"], l)
        sv["h2"] = h
        h, sv["a2"], sv["b2"], sv["f2"], sv["xn2"] = _ffn_fwd(
            h, g3["ffn2_pre_g"], g3["ffn2_post_g"], W["ffn2_w_gate"], W["ffn2_w_up"], W["ffn2_w_down"], l, "2")
        sv["h3"] = h
        h, sv["e"], sv["gte"] = _ple_fwd(h, p3, W["ple_w_up"], W["ple_w_gate"], g3["ple_post_g"], l)
        saved.append(sv)

    dh, loss = _loss_head(h, target)

    G = {n: lax.empty((NSH, L, r, c), BF16) for n, r, c in BIG}
    sg = {n: [None] * L for n in SMALL}
    whole = lambda j: (0, 0)
    shard = lambda j: (j, 0)
    kcol = lambda j: (0, j)
    for l in reversed(range(L)):
        sv = saved[l]
        W = sv["W"]
        dh, de, dgte, sg["ple_post_g"][l] = _ple_bwd(dh, sv["e"], sv["gte"], W["ple_w_gate"], g3["ple_post_g"], l)
        G["ple_w_up"] = _dw(p3[l][None], de, G["ple_w_up"], l, PLE, 256, whole, shard, f"dw_ple_up_l{l}")
        G["ple_w_gate"] = _dw(sv["h3"][None], dgte[None], G["ple_w_gate"], l, 256, D, kcol, whole, f"dw_ple_gate_l{l}")
        dh, df, da, db, hh, sg["ffn2_pre_g"][l], sg["ffn2_post_g"][l] = _ffn_bwd(
            dh, sv["h2"], sv["f2"], sv["a2"], sv["b2"], g3["ffn2_pre_g"], g3["ffn2_post_g"],
            W["ffn2_w_gate"], W["ffn2_w_up"], W["ffn2_w_down"], l, "2")
        G["ffn2_w_gate"] = _dw(da, sv["xn2"][None], G["ffn2_w_gate"], l, DFS, D, shard, whole, f"dw_ffn2_gate_l{l}")
        G["ffn2_w_up"] = _dw(db, sv["xn2"][None], G["ffn2_w_up"], l, DFS, D, shard, whole, f"dw_ffn2_up_l{l}")
        G["ffn2_w_down"] = _dw(hh, df[None], G["ffn2_w_down"], l, DFS, D, shard, whole, f"dw_ffn2_down_l{l}")
        a_re, a_im, bre, bim, cre, cim = sv["prep"]
        dattn, dssm, delta, dmo, sg["mix_post_g"][l], sg["attn_norm_g"][l], sg["ssm_norm_g"][l] = _mix_out_bwd(
            dh, sv["mo"], sv["attn"], sv["ssm"], g3["attn_norm_g"], g3["ssm_norm_g"], g3["mix_post_g"], W["w_out"], l)
        G["w_out"] = _dw(sv["mixed"][None], dmo[None], G["w_out"], l, 256, D, kcol, whole, f"dw_out_l{l}")
        gnr, gni, du_direct, dz, yg, sg["ssm_b_glu"][l], dd, dcre, dcim = _ssm_out_bwd(
            dssm, sv["y"], sv["z"], sv["xr"], sv["xi"], sv["u"], cre, cim, g3["ssm_d"], W["ssm_w_glu"], l)
        sg["ssm_d"][l] = dd.reshape(Sm["ssm_d"].shape[1:])
        G["ssm_w_glu"] = _dw(yg[None], dz[None], G["ssm_w_glu"], l, 128, DSS, kcol, whole, f"dw_glu_l{l}")
        gr, gi = _scan(gnr, gni, a_re, a_im, True, l)
        dar, dai = _ssm_da(gr, gi, sv["xr"], sv["xi"], l)
        du, dbre, dbim = _ssm_in_bwd(gr, gi, sv["u"], bre, bim, du_direct, l)
        for n, g in zip(prep_names, sv["prep_vjp"]((dar, dai, dbre, dbim, dcre, dcim))):
            sg[n][l] = g
        acc = None
        for d in PATTERN_DILATIONS:
            acc = _attn_bwd(sv["P"], dattn, sv["lse"], delta, acc, d, l)
        dh, dP, sg["mix_pre_g"][l] = _mix_proj_bwd(acc[0], acc[1], acc[2], du, dh, sv["h1"], g3["mix_pre_g"],
                                                   W["w_in"], rot, l)
        G["w_in"] = _dw(sv["ain"][None], dP, G["w_in"], l, D, DA, whole, shard, f"dw_in_l{l}")
        dh, df, da, db, hh, sg["ffn1_pre_g"][l], sg["ffn1_post_g"][l] = _ffn_bwd(
            dh, sv["h0"], sv["f1"], sv["a1"], sv["b1"], g3["ffn1_pre_g"], g3["ffn1_post_g"],
            W["ffn1_w_gate"], W["ffn1_w_up"], W["ffn1_w_down"], l, "1")
        G["ffn1_w_gate"] = _dw(da, sv["xn1"][None], G["ffn1_w_gate"], l, DFS, D, shard, whole, f"dw_ffn1_gate_l{l}")
        G["ffn1_w_up"] = _dw(db, sv["xn1"][None], G["ffn1_w_up"], l, DFS, D, shard, whole, f"dw_ffn1_up_l{l}")
        G["ffn1_w_down"] = _dw(hh, df[None], G["ffn1_w_down"], l, DFS, D, shard, whole, f"dw_ffn1_down_l{l}")

    small = {n: jnp.stack([g.reshape(Sm[n].shape[1:]) for g in sg[n]]) for n in SMALL}
    return loss, dh, G, small


HBM_SPEC = pl.BlockSpec(memory_space=pltpu.HBM)


def _place():
    x, y, c = lax.axis_index("x"), lax.axis_index("y"), lax.axis_index("c")
    chips = [(1 - x, y), (x, 1 - y), (1 - x, 1 - y)]
    return x, y, c, chips


def _comm_params():
    return pltpu.CompilerParams(vmem_limit_bytes=VMEM_LIMIT)


def _gather_weights(ws, lands):
    n = len(ws)

    def body(*refs):
        ins, outs = refs[:n], refs[2 * n:3 * n]
        s_ici, r_ici, s_d2d, r_d2d = refs[3 * n:]
        x, y, c, chips = _place()

        def half(ref, t, hc):
            r2 = ws[t].shape[1] // 2
            return ref.at[:, pl.ds(hc * r2, r2), :]

        def ici(t, k, src_chip, to):
            j = 2 * src_chip[0] + src_chip[1]
            src = half(ins[t], t, c) if to is not None else half(outs[t].at[j], t, c)
            return pltpu.make_async_remote_copy(src_ref=src, dst_ref=half(outs[t].at[j], t, c),
                                                send_sem=s_ici.at[3 * t + k], recv_sem=r_ici.at[3 * t + k],
                                                device_id=to if to is not None else (x, y, c), device_id_type=MESH)

        def d2d(t, k, hc):
            j = 2 * chips[k][0] + chips[k][1]
            r = half(outs[t].at[j], t, hc)
            return pltpu.make_async_remote_copy(src_ref=r, dst_ref=r, send_sem=s_d2d.at[3 * t + k],
                                                recv_sem=r_d2d.at[3 * t + k], device_id=(x, y, 1 - c),
                                                device_id_type=MESH)

        sends = [ici(t, k, (x, y), (*chips[k], c)) for t in range(n) for k in range(3)]
        for cp in sends:
            cp.start()
        passed = []
        for t in range(n):
            for k in range(3):
                ici(t, k, chips[k], None).wait_recv()
                passed.append(d2d(t, k, c))
                passed[-1].start()
        for t in range(n):
            for k in range(3):
                d2d(t, k, 1 - c).wait_recv()
        for cp in sends + passed:
            cp.wait_send()

    return _pc(body, name="gather_weights", in_specs=[HBM_SPEC] * (2 * n), out_specs=[HBM_SPEC] * n,
               out_shape=[S(z.shape, z.dtype) for z in lands], input_output_aliases={n + t: t for t in range(n)},
               scratch_shapes=[pltpu.SemaphoreType.DMA((3 * n,))] * 4, compiler_params=_comm_params())(*ws, *lands)


SEM_SPEC = pl.BlockSpec(memory_space=pltpu.SEMAPHORE)
ANY_SPEC = pl.BlockSpec(memory_space=pl.ANY)
SPLIT_EFFECT = pltpu.SideEffectType.DATAFLOW_SIDE_EFFECTING


def _in_hbm(t):
    return pltpu.with_memory_space_constraint(t, pltpu.HBM)


def _place_own(ws, me_arr, layer):
    n = len(ws)

    def body(me_ref, *refs):
        for t in range(n):
            refs[n + t][...] = refs[t][...]

    gs = pltpu.PrefetchScalarGridSpec(
        num_scalar_prefetch=1, grid=(2,),
        in_specs=[pl.BlockSpec((1, w.shape[1] // 2, w.shape[2]), lambda i, me: (0, i, 0)) for w in ws],
        out_specs=[pl.BlockSpec((None, 1, w.shape[1] // 2, w.shape[2]), lambda i, me: (me[0], 0, i, 0)) for w in ws])
    return _pc(body, name=f"gather_place_own_l{layer}", grid_spec=gs,
               out_shape=[S((NSH,) + w.shape, w.dtype) for w in ws], compiler_params=_cp(1))(me_arr, *ws)


def _gather_start(ws, lands, after, layer):
    n = len(ws)

    def body(*refs):
        ins, lz = refs[:n], refs[n:2 * n]
        s_sem, r_sem = refs[2 * n + 1], refs[2 * n + 2]
        token = refs[-1]
        x, y, c, chips = _place()
        for t in range(n):
            for k in range(3):
                pltpu.make_async_remote_copy(src_ref=ins[t], dst_ref=lz[t].at[2 * x + y], send_sem=s_sem.at[3 * t + k],
                                             recv_sem=r_sem.at[3 * t + k], device_id=(*chips[k], c),
                                             device_id_type=MESH).start()
        token[...] = jnp.zeros_like(token)

    hbm = [pltpu.HBM(w.shape, w.dtype) for w in ws] + [pltpu.HBM(z.shape, z.dtype) for z in lands]
    out = _pc(body, name=f"gather_start_l{layer}",
              out_shape=(pltpu.SemaphoreType.DMA((3 * n,)), pltpu.SemaphoreType.DMA((3 * n,)), *hbm, S((8, 128), F32)),
              in_specs=[HBM_SPEC] * (2 * n) + [ANY_SPEC],
              out_specs=(SEM_SPEC, SEM_SPEC, *([HBM_SPEC] * (2 * n)), pl.BlockSpec(memory_space=pltpu.VMEM)),
              input_output_aliases={i: 2 + i for i in range(2 * n)},
              compiler_params=pltpu.CompilerParams(has_side_effects=SPLIT_EFFECT))(
                  *[_in_hbm(w) for w in ws], *[_in_hbm(z) for z in lands], after)
    return out[0], out[1], out[2:2 + n], out[2 + n:2 + 2 * n], out[-1]


def _gather_wait(s_sem, r_sem, ws, lands, after, layer):
    n = len(ws)

    def body(*refs):
        ins, lz = refs[:n], refs[n:2 * n]
        s_ref, r_ref = refs[2 * n], refs[2 * n + 1]
        x, y, c, chips = _place()
        for t in range(n):
            for k in range(3):
                cp = pltpu.make_async_remote_copy(src_ref=ins[t], dst_ref=lz[t].at[2 * x + y], send_sem=s_ref.at[3 * t + k],
                                                  recv_sem=r_ref.at[3 * t + k], device_id=(*chips[k], c),
                                                  device_id_type=MESH)
                cp.wait_send()
                cp.wait_recv()

    hbm = [pltpu.HBM(w.shape, w.dtype) for w in ws] + [pltpu.HBM(z.shape, z.dtype) for z in lands]
    out = _pc(body, name=f"gather_wait_l{layer}", out_shape=tuple(hbm),
              in_specs=[HBM_SPEC] * (2 * n) + [SEM_SPEC, SEM_SPEC, ANY_SPEC], out_specs=tuple([HBM_SPEC] * (2 * n)),
              input_output_aliases={i: i for i in range(2 * n)},
              compiler_params=pltpu.CompilerParams(has_side_effects=SPLIT_EFFECT))(*ws, *lands, s_sem, r_sem, after)
    return out[n:]


def _swap_halves(gs):
    n = len(gs)

    def body(*refs):
        ins, outs = refs[:n], refs[n:2 * n]
        s_sem, r_sem = refs[2 * n:]
        x, y, c, _ = _place()
        cps = []
        for t in range(n):
            r2 = gs[t].shape[2] // 2
            cps.append(pltpu.make_async_remote_copy(
                src_ref=ins[t].at[:, :, pl.ds((1 - c) * r2, r2), :], dst_ref=outs[t], send_sem=s_sem.at[t],
                recv_sem=r_sem.at[t], device_id=(x, y, 1 - c), device_id_type=MESH))
            cps[-1].start()
        for cp in cps:
            cp.wait_recv()
        for cp in cps:
            cp.wait_send()

    return _pc(body, name="grad_swap_halves", in_specs=[HBM_SPEC] * n, out_specs=[HBM_SPEC] * n,
               out_shape=[S(g.shape[:2] + (g.shape[2] // 2, g.shape[3]), g.dtype) for g in gs],
               scratch_shapes=[pltpu.SemaphoreType.DMA((n,))] * 2, compiler_params=_comm_params())(*gs)


def _add_half(g, landed, c_arr, name):
    _, L, r2, cols = landed.shape

    def body(c_ref, g_ref, l_ref, o_ref):
        o_ref[...] = (g_ref[...].astype(F32) + l_ref[...].astype(F32)).astype(BF16)

    gs = pltpu.PrefetchScalarGridSpec(
        num_scalar_prefetch=1, grid=(NSH, L),
        in_specs=[pl.BlockSpec((None, None, r2, cols), lambda j, l, c: (j, l, c[0], 0)),
                  pl.BlockSpec((None, None, r2, cols), lambda j, l, c: (j, l, 0, 0))],
        out_specs=pl.BlockSpec((None, None, r2, cols), lambda j, l, c: (j, l, 0, 0)))
    return _pc(body, name=name, grid_spec=gs, out_shape=S(landed.shape, BF16), compiler_params=_cp(2))(c_arr, g, landed)


def _send_shards(ps):
    n = len(ps)

    def body(*refs):
        ins, outs = refs[:n], refs[n:2 * n]
        s_sem, r_sem = refs[2 * n:]
        x, y, c, chips = _place()
        cps = []
        for t in range(n):
            for k in range(3):
                cps.append(pltpu.make_async_remote_copy(
                    src_ref=ins[t].at[2 * chips[k][0] + chips[k][1]], dst_ref=outs[t].at[k],
                    send_sem=s_sem.at[3 * t + k], recv_sem=r_sem.at[3 * t + k], device_id=(*chips[k], c),
                    device_id_type=MESH))
                cps[-1].start()
        for cp in cps:
            cp.wait_recv()
        for cp in cps:
            cp.wait_send()

    return _pc(body, name="grad_send_shards", in_specs=[HBM_SPEC] * n, out_specs=[HBM_SPEC] * n,
               out_shape=[S((3,) + p.shape[1:], p.dtype) for p in ps],
               scratch_shapes=[pltpu.SemaphoreType.DMA((3 * n,))] * 2, compiler_params=_comm_params())(*ps)


def _sum_shards(part, landed, me_arr, c_arr, name):
    _, L, r2, cols = landed.shape

    def body(me_ref, c_ref, p_ref, l_ref, o_ref):
        o_ref[...] = ((p_ref[...].astype(F32) + l_ref[0].astype(F32)) + l_ref[1].astype(F32)) + l_ref[2].astype(F32)

    gs = pltpu.PrefetchScalarGridSpec(
        num_scalar_prefetch=2, grid=(L,),
        in_specs=[pl.BlockSpec((None, None, r2, cols), lambda l, me, c: (me[0], l, 0, 0)),
                  pl.BlockSpec((3, None, r2, cols), lambda l, me, c: (0, l, 0, 0))],
        out_specs=pl.BlockSpec((None, r2, cols), lambda l, me, c: (l, c[0], 0)))
    return _pc(body, name=name, grid_spec=gs, out_shape=S((L, 2 * r2, cols), F32),
               compiler_params=_cp(1))(me_arr, c_arr, part, landed)


def _share_halves(bufs):
    n = len(bufs)

    def body(*refs):
        ins, outs = refs[:n], refs[n:2 * n]
        s_sem, r_sem = refs[2 * n:]
        x, y, c, _ = _place()
        cps = []
        for t in range(n):
            r2 = bufs[t].shape[1] // 2
            cps.append(pltpu.make_async_remote_copy(
                src_ref=ins[t].at[:, pl.ds(c * r2, r2), :], dst_ref=outs[t].at[:, pl.ds(c * r2, r2), :],
                send_sem=s_sem.at[t], recv_sem=r_sem.at[t], device_id=(x, y, 1 - c), device_id_type=MESH))
            cps[-1].start()
        for cp in cps:
            cp.wait_recv()
        for cp in cps:
            cp.wait_send()

    return _pc(body, name="grad_share_halves", in_specs=[HBM_SPEC] * n, out_specs=[HBM_SPEC] * n,
               out_shape=[S(b.shape, b.dtype) for b in bufs], input_output_aliases={t: t for t in range(n)},
               scratch_shapes=[pltpu.SemaphoreType.DMA((n,))] * 2, compiler_params=_comm_params())(*bufs)


def _gather_small(v):
    nr = v.shape[0]

    def body(v_ref, out_ref, send_sems, recv_sems, local_sem):
        x, y, c, chips = _place()
        me, sibling = (x, y, c), (x, y, 1 - c)

        def rows(px, py, pc):
            return out_ref.at[pl.ds((4 * px + 2 * py + pc) * nr, nr), :]

        def copy(k, block, to, src=None):
            return pltpu.make_async_remote_copy(src_ref=rows(*block) if src is None else src, dst_ref=rows(*block),
                                                send_sem=send_sems.at[k], recv_sem=recv_sems.at[k], device_id=to,
                                                device_id_type=MESH)

        mine = pltpu.make_async_copy(v_ref, rows(*me), local_sem)
        mine.start()
        first = [copy(0, me, sibling, src=v_ref)]
        first += [copy(1 + j, me, (*chip, c), src=v_ref) for j, chip in enumerate(chips)]
        for cp in first:
            cp.start()
        passed = [copy(4 + j, (*chip, c), sibling) for j, chip in enumerate(chips)]
        for j, chip in enumerate(chips):
            copy(1 + j, (*chip, c), me).wait_recv()
            passed[j].start()
        copy(0, sibling, me).wait_recv()
        for j, chip in enumerate(chips):
            copy(4 + j, (*chip, 1 - c), me).wait_recv()
        for cp in first + passed:
            cp.wait_send()
        mine.wait()

    vm = pl.BlockSpec(memory_space=pltpu.VMEM)
    return _pc(body, name="gather_small_grads", in_specs=[vm], out_specs=vm, out_shape=S((8 * nr, 128), F32),
               scratch_shapes=[pltpu.SemaphoreType.DMA((7,)), pltpu.SemaphoreType.DMA((7,)), pltpu.SemaphoreType.DMA],
               compiler_params=_comm_params())(v)


def _adamw_math(w, g, m, v):
    m2 = ADAM_B1 * m + (1.0 - ADAM_B1) * g
    v2 = ADAM_B2 * v + (1.0 - ADAM_B2) * (g * g)
    m_hat = m2 / (1.0 - ADAM_B1 ** ADAM_STEP)
    v_hat = v2 / (1.0 - ADAM_B2 ** ADAM_STEP)
    return -ADAM_LR * (m_hat / (jnp.sqrt(v_hat) + ADAM_EPS) + ADAM_WD * w), m2, v2


def _adamw(w, g, m, v, name):
    L, R, C = w.shape
    rb = R // 2 if R >= 512 else R

    def body(w_ref, g_ref, m_ref, v_ref, d_ref, m2_ref, v2_ref):
        d_ref[...], m2_ref[...], v2_ref[...] = _adamw_math(w_ref[...], g_ref[...], m_ref[...], v_ref[...])

    blk = pl.BlockSpec((None, rb, C), lambda l, r: (l, r, 0))
    return _pc(body, name=name, grid=(L, R // rb), in_specs=[blk] * 4, out_specs=[blk] * 3,
               out_shape=[S(w.shape, F32)] * 3, compiler_params=_cp(2))(w, g, m, v)


def _adamw_small(gathered, w, m, v):
    nr = w.shape[0]
    rb = nr // 5

    def body(a_ref, w_ref, m_ref, v_ref, g_ref, d_ref, m2_ref, v2_ref):
        g = a_ref[0]
        for k in range(1, 8):
            g = g + a_ref[k]
        g_ref[...] = g
        d_ref[...], m2_ref[...], v2_ref[...] = _adamw_math(w_ref[...], g, m_ref[...], v_ref[...])

    blk = pl.BlockSpec((rb, 128), lambda i: (i, 0))
    return _pc(body, name="adamw_small", grid=(nr // rb,), in_specs=[pl.BlockSpec((8, rb, 128), lambda i: (0, i, 0))] + [blk] * 3,
               out_specs=[blk] * 4, out_shape=[S((nr, 128), F32)] * 4, compiler_params=_cp(1))(gathered, w, m, v)


SMALL_ROWS = 4520


def _pack(arrs):
    flat = jnp.concatenate([a.reshape(-1) for a in arrs])
    return jnp.pad(flat, (0, SMALL_ROWS * 128 - flat.shape[0])).reshape(SMALL_ROWS, 128)


def _unpack(packed, like):
    flat = packed.reshape(-1)
    out, off = [], 0
    for a in like:
        out.append(flat[off:off + a.size].reshape(a.shape))
        off += a.size
    return out


def kernel(x, p, positions, ffn1_pre_g, ffn1_w_gate, ffn1_w_up, ffn1_w_down, ffn1_post_g, mix_pre_g, w_in, attn_norm_g, ssm_lam_re, ssm_lam_im, ssm_log_dt, ssm_b_re, ssm_b_im, ssm_c_re, ssm_c_im, ssm_d, ssm_w_glu, ssm_b_glu, ssm_norm_g, w_out, mix_post_g, ffn2_pre_g, ffn2_w_gate, ffn2_w_up, ffn2_w_down, ffn2_post_g, ple_w_up, ple_w_gate, ple_post_g, loss_target, m_ffn1_pre_g, m_ffn1_w_gate, m_ffn1_w_up, m_ffn1_w_down, m_ffn1_post_g, m_mix_pre_g, m_w_in, m_attn_norm_g, m_ssm_lam_re, m_ssm_lam_im, m_ssm_log_dt, m_ssm_b_re, m_ssm_b_im, m_ssm_c_re, m_ssm_c_im, m_ssm_d, m_ssm_w_glu, m_ssm_b_glu, m_ssm_norm_g, m_w_out, m_mix_post_g, m_ffn2_pre_g, m_ffn2_w_gate, m_ffn2_w_up, m_ffn2_w_down, m_ffn2_post_g, m_ple_w_up, m_ple_w_gate, m_ple_post_g, v_ffn1_pre_g, v_ffn1_w_gate, v_ffn1_w_up, v_ffn1_w_down, v_ffn1_post_g, v_mix_pre_g, v_w_in, v_attn_norm_g, v_ssm_lam_re, v_ssm_lam_im, v_ssm_log_dt, v_ssm_b_re, v_ssm_b_im, v_ssm_c_re, v_ssm_c_im, v_ssm_d, v_ssm_w_glu, v_ssm_b_glu, v_ssm_norm_g, v_w_out, v_mix_post_g, v_ffn2_pre_g, v_ffn2_w_gate, v_ffn2_w_up, v_ffn2_w_down, v_ffn2_post_g, v_ple_w_up, v_ple_w_gate, v_ple_post_g):
    a = dict(locals())
    T = x.shape[1]
    big_names = [n for n, _, _ in BIG]
    for n in TRANSPOSED:
        for pre in ("", "m_", "v_"):
            a[pre + n] = jnp.swapaxes(a[pre + n], 1, 2)

    own = [a[n].astype(BF16) for n in big_names]
    n_layers = own[0].shape[0]
    per_layer = [[w[l:l + 1] for w in own] for l in range(n_layers)]
    c_arr = lax.axis_index("c").astype(jnp.int32).reshape(1)
    me_arr = (2 * lax.axis_index("x") + lax.axis_index("y")).astype(jnp.int32).reshape(1)
    first = dict(zip(big_names, _gather_weights(per_layer[0], _place_own(per_layer[0], me_arr, 0))))
    pending, anchor, queued_behind = {}, jnp.zeros((), F32), first[big_names[0]]
    for l in range(1, n_layers):
        s_sem, r_sem, ws_thru, lands_thru, token = _gather_start(per_layer[l], _place_own(per_layer[l], me_arr, l),
                                                                 queued_behind, l)
        pending[l] = (s_sem, r_sem, ws_thru, lands_thru)
        anchor = anchor + token[0, 0]
        queued_behind = token

    def weights_of(l, after):
        if l == 0:
            return first
        return dict(zip(big_names, _gather_wait(*pending[l], after, l)))

    Sm = {n: a[n] for n in SMALL}
    Sm["ffn1_pre_g"] = Sm["ffn1_pre_g"] + anchor

    pos = jnp.broadcast_to(positions.reshape(1, T, 1).astype(F32), (1, T, 128))
    loss, gx, G, small = _local_step(_reorder(x, True, "to_streams_x")[0], _reorder(p[:, 0], True, "to_streams_p"),
                                     _reorder(pos, True, "to_streams_pos")[0, :, :1],
                                     _reorder(loss_target, True, "to_streams_target")[0], weights_of, Sm)
    gx = _reorder(gx[None], False, "to_time_grad_x")

    gs = [G[n] for n in big_names]
    landed = _swap_halves(gs)
    parts = [_add_half(g, la, c_arr, f"grad_add_half_{n}") for g, la, n in zip(gs, landed, big_names)]
    landed = _send_shards(parts)
    halves = [_sum_shards(pt, la, me_arr, c_arr, f"grad_sum_shards_{n}") for pt, la, n in zip(parts, landed, big_names)]
    grads = dict(zip(big_names, _share_halves(halves)))

    small_g = _gather_small(_pack([small[n] for n in SMALL])).reshape(8, SMALL_ROWS, 128)
    sg, sd, sm, sv = _adamw_small(small_g, _pack([a[n] for n in SMALL]), _pack([a["m_" + n] for n in SMALL]),
                                  _pack([a["v_" + n] for n in SMALL]))
    like = [a[n] for n in SMALL]
    res = {}
    for n, g_, d_, m_, v_ in zip(SMALL, _unpack(sg, like), _unpack(sd, like), _unpack(sm, like), _unpack(sv, like)):
        res[n] = (g_, d_, m_, v_)
    for n in big_names:
        d_, m_, v_ = _adamw(a[n], grads[n], a["m_" + n], a["v_" + n], f"adamw_{n}")
        res[n] = (grads[n], d_, m_, v_)
        if n in TRANSPOSED:
            res[n] = tuple(jnp.swapaxes(t, 1, 2) for t in res[n])

    total = lax.psum(loss[0, 0], ("x", "y", "c"))
    return (total, gx, *[res[n][0] for n in WEIGHTS], *[res[n][1] for n in WEIGHTS],
            *[res[n][2] for n in WEIGHTS], *[res[n][3] for n in WEIGHTS])
```

```python
import functools
import math

import numpy as np
import jax
import jax.numpy as jnp
from jax import lax
from jax.experimental import pallas as pl
from jax.experimental.pallas import tpu as pltpu

F32 = jnp.float32
BF16 = jnp.bfloat16
S = jax.ShapeDtypeStruct
MESH = pl.DeviceIdType.MESH

D = 1024
DA = 512
DSS = 512
HD = 64
NH = 8
BAND = 128
NSH = 4
DFS = 704
PLE = 256
EPS = 1e-6
ROPE_THETA = 500000.0
PATTERN_DILATIONS = (1, 4, 16)
NLB = 16
ADAM_LR, ADAM_B1, ADAM_B2, ADAM_EPS, ADAM_WD, ADAM_STEP = 0.001, 0.9, 0.999, 1e-08, 0.01, 10

VMEM_LIMIT = 56 * 1024 * 1024
TM = 512
TMB = 256

BIG = (
    ("ffn1_w_gate", DFS, D), ("ffn1_w_up", DFS, D), ("ffn1_w_down", DFS, D),
    ("w_in", D, 512), ("ssm_w_glu", 128, 512), ("w_out", 256, D),
    ("ffn2_w_gate", DFS, D), ("ffn2_w_up", DFS, D), ("ffn2_w_down", DFS, D),
    ("ple_w_up", PLE, 256), ("ple_w_gate", 256, D),
)
TRANSPOSED = ("ffn1_w_gate", "ffn1_w_up", "ffn2_w_gate", "ffn2_w_up")
SMALL = ("ffn1_pre_g", "ffn1_post_g", "mix_pre_g", "attn_norm_g", "ssm_lam_re", "ssm_lam_im", "ssm_log_dt",
         "ssm_b_re", "ssm_b_im", "ssm_c_re", "ssm_c_im", "ssm_d", "ssm_b_glu", "ssm_norm_g", "mix_post_g",
         "ffn2_pre_g", "ffn2_post_g", "ple_post_g")
WEIGHTS = ("ffn1_pre_g", "ffn1_w_gate", "ffn1_w_up", "ffn1_w_down", "ffn1_post_g", "mix_pre_g", "w_in", "attn_norm_g",
           "ssm_lam_re", "ssm_lam_im", "ssm_log_dt", "ssm_b_re", "ssm_b_im", "ssm_c_re", "ssm_c_im", "ssm_d",
           "ssm_w_glu", "ssm_b_glu", "ssm_norm_g", "w_out", "mix_post_g", "ffn2_pre_g", "ffn2_w_gate", "ffn2_w_up",
           "ffn2_w_down", "ffn2_post_g", "ple_w_up", "ple_w_gate", "ple_post_g")


def _pc(body, **kw):
    return pl.pallas_call(body, **kw)


def _cp(n_grid):
    return pltpu.CompilerParams(dimension_semantics=("arbitrary",) * n_grid, vmem_limit_bytes=VMEM_LIMIT)


def _dot(a, b):
    return jnp.dot(a, b, preferred_element_type=F32)


def _dot_nt(a, b):
    return lax.dot_general(a, b, (((1,), (1,)), ((), ())), preferred_element_type=F32)


def _dot_tn(a, b):
    return lax.dot_general(a, b, (((0,), (0,)), ((), ())), preferred_element_type=F32)


def _split(a):
    hi = a.astype(BF16)
    return hi, (a - hi.astype(F32)).astype(BF16)


def _dot3(fn, a, b):
    ah, al = _split(a)
    bh, bl = _split(b)
    return fn(ah, bh) + fn(ah, bl) + fn(al, bh)


def _rms_fwd(x, g):
    r = lax.rsqrt(jnp.mean(x * x, axis=-1, keepdims=True) + EPS)
    return x * r * g


def _rms_bwd(dy, x, g):
    r = lax.rsqrt(jnp.mean(x * x, axis=-1, keepdims=True) + EPS)
    xr = x * r
    gd = dy * g
    dx = r * (gd - xr * jnp.mean(gd * xr, axis=-1, keepdims=True))
    dg = jnp.sum(dy * xr, axis=0, keepdims=True)
    return dx, dg


def _gelu(y):
    k = math.sqrt(2.0 / math.pi)
    return 0.5 * y * (1.0 + jnp.tanh(k * (y + 0.044715 * y * y * y)))


def _gelu_grad(y):
    k = math.sqrt(2.0 / math.pi)
    t = jnp.tanh(k * (y + 0.044715 * y * y * y))
    return 0.5 * (1.0 + t) + 0.5 * y * (1.0 - t * t) * k * (1.0 + 3 * 0.044715 * y * y)


def _gain_spec(n, layer):
    return pl.BlockSpec((None, 1, n), lambda *_: (layer, 0, 0))


def _row_acc_spec(n):
    return pl.BlockSpec((1, n), lambda *_: (0, 0))


def _rot_tables(pos_col):
    T = pos_col.shape[0]
    half = HD // 8
    inv = (ROPE_THETA ** (-np.arange(half, dtype=np.float32) * (2.0 / (2 * half)))).astype(np.float32)
    lane_freq = np.tile(np.concatenate([inv, inv, np.zeros(HD - 2 * half, np.float32)]), NH)[None, :]

    def body(p_ref, f_ref, c_ref, s1_ref, s2_ref):
        ang = p_ref[...] * f_ref[...]
        d = lax.broadcasted_iota(jnp.int32, ang.shape, 1) % HD
        cs = jnp.cos(ang)
        sn = jnp.sin(ang)
        c_ref[...] = jnp.where(d < 2 * half, cs, 1.0)
        s1_ref[...] = jnp.where(d < half, -sn, 0.0)
        s2_ref[...] = jnp.where((d >= half) & (d < 2 * half), sn, 0.0)

    tm = TM
    return _pc(body, name="rot_tables", grid=(T // tm,),
               in_specs=[pl.BlockSpec((tm, 1), lambda i: (i, 0)), pl.BlockSpec((1, DA), lambda i: (0, 0))],
               out_specs=[pl.BlockSpec((tm, DA), lambda i: (i, 0))] * 3,
               out_shape=[S((T, DA), F32)] * 3, compiler_params=_cp(1))(pos_col, jnp.asarray(lane_freq))


def _rot_fwd(t, c, s1, s2):
    return t * c + pltpu.roll(t, DA - 8, 1) * s1 + pltpu.roll(t, 8, 1) * s2


def _rot_bwd(g, c, s1, s2):
    return g * c + pltpu.roll(g * s1, 8, 1) + pltpu.roll(g * s2, DA - 8, 1)


def _ffn_fwd(h, pre_g, post_g, wg, wu, wd, layer, tag):
    T = h.shape[0]
    tm = TM
    nt = T // tm

    def body(h_ref, pg_ref, qg_ref, wg_ref, wu_ref, wd_ref, ho_ref, a_ref, b_ref, f_ref, xn_ref, xs, facc):
        j = pl.program_id(1)

        @pl.when(j == 0)
        def _():
            xb = _rms_fwd(h_ref[...], pg_ref[...]).astype(BF16)
            xs[...] = xb
            xn_ref[...] = xb
            facc[...] = jnp.zeros_like(facc)

        xb = xs[...]
        ab = _dot_nt(xb, wg_ref[...]).astype(BF16)
        bb = _dot_nt(xb, wu_ref[...]).astype(BF16)
        a_ref[...] = ab
        b_ref[...] = bb
        a = ab.astype(F32)
        hh = (a * jax.nn.sigmoid(a) * bb.astype(F32)).astype(BF16)
        facc[...] += _dot(hh, wd_ref[...])

        @pl.when(j == NSH - 1)
        def _():
            f = facc[...]
            f_ref[...] = f
            ho_ref[...] = h_ref[...] + 0.5 * _rms_fwd(f, qg_ref[...])

    row = pl.BlockSpec((tm, D), lambda i, j: (i, 0))
    act = pl.BlockSpec((None, tm, DFS), lambda i, j: (j, i, 0))
    wrow = pl.BlockSpec((None, None, DFS, D), lambda i, j: (j, 0, 0, 0))
    return _pc(body, name=f"ffn_fwd_{tag}_l{layer}", grid=(nt, NSH),
               in_specs=[row, _gain_spec(D, layer), _gain_spec(D, layer), wrow, wrow, wrow],
               out_specs=[row, act, act, row, row],
               out_shape=[S((T, D), F32), S((NSH, T, DFS), BF16), S((NSH, T, DFS), BF16), S((T, D), F32), S((T, D), BF16)],
               scratch_shapes=[pltpu.VMEM((tm, D), BF16), pltpu.VMEM((tm, D), F32)],
               compiler_params=_cp(2))(h, pre_g, post_g, wg, wu, wd)


def _ffn_bwd(dout, h, f, a, b, pre_g, post_g, wg, wu, wd, layer, tag):
    T = h.shape[0]
    tm = TM
    nt = T // tm

    def body(do_ref, h_ref, f_ref, a_ref, b_ref, pg_ref, qg_ref, wg_ref, wu_ref, wd_ref,
             dh_ref, df_ref, da_ref, db_ref, hh_ref, dpg_ref, dqg_ref, dfs, dxn):
        i = pl.program_id(0)
        j = pl.program_id(1)

        @pl.when((i == 0) & (j == 0))
        def _():
            dpg_ref[...] = jnp.zeros_like(dpg_ref)
            dqg_ref[...] = jnp.zeros_like(dqg_ref)

        @pl.when(j == 0)
        def _():
            df, dq = _rms_bwd(0.5 * do_ref[...], f_ref[...], qg_ref[...])
            dqg_ref[...] += dq
            dfb = df.astype(BF16)
            dfs[...] = dfb
            df_ref[...] = dfb
            dxn[...] = jnp.zeros_like(dxn)

        dhh = _dot_nt(dfs[...], wd_ref[...])
        av = a_ref[...].astype(F32)
        bv = b_ref[...].astype(F32)
        sg = jax.nn.sigmoid(av)
        sa = av * sg
        hh_ref[...] = (sa * bv).astype(BF16)
        dab = (dhh * bv * (sg * (1.0 + av * (1.0 - sg)))).astype(BF16)
        dbb = (dhh * sa).astype(BF16)
        da_ref[...] = dab
        db_ref[...] = dbb
        dxn[...] += _dot(dab, wg_ref[...]) + _dot(dbb, wu_ref[...])

        @pl.when(j == NSH - 1)
        def _():
            dx, dp = _rms_bwd(dxn[...], h_ref[...], pg_ref[...])
            dpg_ref[...] += dp
            dh_ref[...] = do_ref[...] + dx

    row = pl.BlockSpec((tm, D), lambda i, j: (i, 0))
    act = pl.BlockSpec((None, tm, DFS), lambda i, j: (j, i, 0))
    wrow = pl.BlockSpec((None, None, DFS, D), lambda i, j: (j, 0, 0, 0))
    return _pc(body, name=f"ffn_bwd_{tag}_l{layer}", grid=(nt, NSH),
               in_specs=[row, row, row, act, act, _gain_spec(D, layer), _gain_spec(D, layer), wrow, wrow, wrow],
               out_specs=[row, row, act, act, act, _row_acc_spec(D), _row_acc_spec(D)],
               out_shape=[S((T, D), F32), S((T, D), BF16), S((NSH, T, DFS), BF16), S((NSH, T, DFS), BF16),
                          S((NSH, T, DFS), BF16), S((1, D), F32), S((1, D), F32)],
               scratch_shapes=[pltpu.VMEM((tm, D), BF16), pltpu.VMEM((tm, D), F32)],
               compiler_params=_cp(2))(dout, h, f, a, b, pre_g, post_g, wg, wu, wd)


def _dw(A, B, buf, layer, kb, nb, a_idx, b_idx, name):
    T = A.shape[1]
    tt = 2 * TM if T % (2 * TM) == 0 else TM
    nt = T // tt

    def body(a_ref, b_ref, buf_ref, o_ref, acc):
        t = pl.program_id(1)

        @pl.when(t == 0)
        def _():
            acc[...] = jnp.zeros_like(acc)

        acc[...] += _dot_tn(a_ref[...].astype(BF16), b_ref[...].astype(BF16))

        @pl.when(t == nt - 1)
        def _():
            o_ref[...] = acc[...].astype(o_ref.dtype)

    return _pc(body, name=name, grid=(NSH, nt),
               in_specs=[pl.BlockSpec((None, tt, kb), lambda j, t: (a_idx(j)[0], t, a_idx(j)[1])),
                         pl.BlockSpec((None, tt, nb), lambda j, t: (b_idx(j)[0], t, b_idx(j)[1])),
                         pl.BlockSpec(memory_space=pl.ANY)],
               out_specs=pl.BlockSpec((None, None, kb, nb), lambda j, t: (j, layer, 0, 0)),
               out_shape=S(buf.shape, buf.dtype), input_output_aliases={2: 0},
               scratch_shapes=[pltpu.VMEM((kb, nb), F32)], compiler_params=_cp(2))(A, B, buf)


def _mix_proj(h, pre_g, win, rot, layer):
    T = h.shape[0]
    tm = TM

    def body(h_ref, g_ref, w_ref, c_ref, s1_ref, s2_ref, p_ref, xn_ref, xs):
        j = pl.program_id(1)

        @pl.when(j == 0)
        def _():
            xb = _rms_fwd(h_ref[...], g_ref[...]).astype(BF16)
            xs[...] = xb
            xn_ref[...] = xb

        o = _dot(xs[...], w_ref[...])

        @pl.when(j < 2)
        def _():
            p_ref[...] = _rot_fwd(o, c_ref[...], s1_ref[...], s2_ref[...])

        @pl.when(j >= 2)
        def _():
            p_ref[...] = o

    row = pl.BlockSpec((tm, D), lambda i, j: (i, 0))
    half = pl.BlockSpec((tm, DA), lambda i, j: (i, 0))
    return _pc(body, name=f"mix_proj_l{layer}", grid=(T // tm, NSH),
               in_specs=[row, _gain_spec(D, layer), pl.BlockSpec((None, None, D, DA), lambda i, j: (j, 0, 0, 0)),
                         half, half, half],
               out_specs=[pl.BlockSpec((None, tm, DA), lambda i, j: (j, i, 0)), row],
               out_shape=[S((NSH, T, DA), F32), S((T, D), BF16)],
               scratch_shapes=[pltpu.VMEM((tm, D), BF16)], compiler_params=_cp(2))(h, pre_g, win, *rot)


def _mix_proj_bwd(dq, dk, dv, du, dh_up, h, pre_g, win, rot, layer):
    T = h.shape[0]
    tm = TM

    def body(dq_ref, dk_ref, dv_ref, du_ref, up_ref, h_ref, g_ref, w_ref, c_ref, s1_ref, s2_ref,
             dh_ref, dp_ref, dg_ref, dps, dxn):
        i = pl.program_id(0)
        j = pl.program_id(1)

        @pl.when((i == 0) & (j == 0))
        def _():
            dg_ref[...] = jnp.zeros_like(dg_ref)

        @pl.when(j == 0)
        def _():
            dxn[...] = jnp.zeros_like(dxn)
            dps[...] = _rot_bwd(dq_ref[...], c_ref[...], s1_ref[...], s2_ref[...]).astype(BF16)

        @pl.when(j == 1)
        def _():
            dps[...] = _rot_bwd(dk_ref[...], c_ref[...], s1_ref[...], s2_ref[...]).astype(BF16)

        @pl.when(j == 2)
        def _():
            dps[...] = dv_ref[...].astype(BF16)

        @pl.when(j == 3)
        def _():
            dps[...] = du_ref[...].astype(BF16)

        dpb = dps[...]
        dp_ref[...] = dpb
        dxn[...] += _dot_nt(dpb, w_ref[...])

        @pl.when(j == NSH - 1)
        def _():
            dx, dg = _rms_bwd(dxn[...], h_ref[...], g_ref[...])
            dg_ref[...] += dg
            dh_ref[...] = up_ref[...] + dx

    row = pl.BlockSpec((tm, D), lambda i, j: (i, 0))
    half = pl.BlockSpec((tm, DA), lambda i, j: (i, 0))
    return _pc(body, name=f"mix_proj_bwd_l{layer}", grid=(T // tm, NSH),
               in_specs=[half, half, half, half, row, row, _gain_spec(D, layer),
                         pl.BlockSpec((None, None, D, DA), lambda i, j: (j, 0, 0, 0)), half, half, half],
               out_specs=[row, pl.BlockSpec((None, tm, DA), lambda i, j: (j, i, 0)), _row_acc_spec(D)],
               out_shape=[S((T, D), F32), S((NSH, T, DA), BF16), S((1, D), F32)],
               scratch_shapes=[pltpu.VMEM((tm, DA), BF16), pltpu.VMEM((tm, D), F32)],
               compiler_params=_cp(2))(dq, dk, dv, du, dh_up, h, pre_g, win, *rot)


def _stream_pos(d, axis):
    i = lax.broadcasted_iota(jnp.int32, (BAND, BAND), axis)
    if d == 16:
        return i
    if d == 4:
        return 4 * (i % 32) + i // 32
    return 16 * (i % 8) + i // 8


def _band_masks(b, d):
    qi, kj = _stream_pos(d, 0), _stream_pos(d, 1)
    return kj <= qi, (kj >= qi) & (b > 0)


def _pattern(d, T):
    n16 = T // 16
    if d == 16:
        return (16, n16, DA), (None, BAND, DA), lambda r, k: (r, k, 0)
    if d == 4:
        return (4, 4, n16, DA), (4, None, 32, DA), lambda r, k: (0, r, k, 0)
    return (16, n16, DA), (16, 8, DA), lambda r, k: (0, k, 0)


def _pattern_spec(d, T, kmap, lead=None):
    _, blk, idx = _pattern(d, T)
    if lead is None:
        return pl.BlockSpec(blk, lambda r, b: idx(r, kmap(b)))
    return pl.BlockSpec((None,) + blk, lambda r, b: (lead,) + idx(r, kmap(b)))


def _attn_fwd(P, d, layer):
    T = P.shape[1]
    nb = T // d // BAND
    vshape = _pattern(d, T)[0]
    Pv = P.reshape((NSH,) + vshape)
    scale = HD ** -0.5

    def body(q_ref, kp_ref, kc_ref, vp_ref, vc_ref, o_ref, l_ref, qs, ks, vs, osc, lsc):
        b = pl.program_id(1)
        flat = lambda ref: ref[...].reshape(BAND, DA).astype(BF16)
        qs[...] = flat(q_ref)
        ks[0:BAND, :] = flat(kp_ref)
        ks[BAND:, :] = flat(kc_ref)
        vs[0:BAND, :] = flat(vp_ref)
        vs[BAND:, :] = flat(vc_ref)
        mask_c, mask_p = _band_masks(b, d)
        mask = jnp.concatenate([mask_p, mask_c], axis=1)
        for hd in range(NH):
            sl = slice(hd * HD, (hd + 1) * HD)
            s = jnp.where(mask, _dot_nt(qs[:, sl], ks[:, sl]) * scale, -1e30)
            m = jnp.max(s, axis=-1, keepdims=True)
            e = jnp.exp(s - m)
            den = jnp.sum(e, axis=-1, keepdims=True)
            osc[:, sl] = _dot(e.astype(BF16), vs[:, sl]) / den
            lsc[:, sl] = jnp.broadcast_to(m + jnp.log(den), (BAND, HD))
        o_ref[...] = osc[...].reshape(o_ref.shape)
        l_ref[...] = lsc[...].reshape(l_ref.shape)

    cur = lambda b: b
    prev = lambda b: jnp.maximum(b - 1, 0)
    out = _pattern_spec(d, T, cur)
    o, l = _pc(body, name=f"attn_fwd_d{d}_l{layer}", grid=(d, nb),
               in_specs=[_pattern_spec(d, T, cur, 0), _pattern_spec(d, T, prev, 1), _pattern_spec(d, T, cur, 1),
                         _pattern_spec(d, T, prev, 2), _pattern_spec(d, T, cur, 2)],
               out_specs=[out, out], out_shape=[S(vshape, F32)] * 2,
               scratch_shapes=[pltpu.VMEM((BAND, DA), BF16)] + [pltpu.VMEM((2 * BAND, DA), BF16)] * 2
               + [pltpu.VMEM((BAND, DA), F32)] * 2,
               compiler_params=_cp(2))(Pv, Pv, Pv, Pv, Pv)
    return o.reshape(T, DA), l.reshape(T, DA)


def _attn_bwd(P, dO, lse, delta, acc, d, layer):
    T = P.shape[1]
    nb = T // d // BAND
    vshape = _pattern(d, T)[0]
    Pv = P.reshape((NSH,) + vshape)
    scale = HD ** -0.5
    first = acc is None

    def body(*refs):
        q_ref, kp_ref, kc_ref, vp_ref, vc_ref, do_ref, l_ref, dl_ref = refs[:8]
        if first:
            dq_ref, dk_ref, dv_ref = refs[8:11]
        else:
            aq_ref, ak_ref, av_ref, dq_ref, dk_ref, dv_ref = refs[8:14]
        qs, dos, ks, vs, ls, dls, oq, ok, ov, ck, cv = refs[-11:]
        b = pl.program_id(1)
        flat = lambda ref: ref[...].reshape(BAND, DA)

        @pl.when(b == 0)
        def _():
            ck[...] = jnp.zeros_like(ck)
            cv[...] = jnp.zeros_like(cv)

        @pl.when(b < nb)
        def _():
            qs[...] = flat(q_ref).astype(BF16)
            dos[...] = flat(do_ref).astype(BF16)
            ks[0:BAND, :] = flat(kp_ref).astype(BF16)
            ks[BAND:, :] = flat(kc_ref).astype(BF16)
            vs[0:BAND, :] = flat(vp_ref).astype(BF16)
            vs[BAND:, :] = flat(vc_ref).astype(BF16)
            ls[...] = flat(l_ref)
            dls[...] = flat(dl_ref)
            mask_c, mask_p = _band_masks(b, d)
            mask = jnp.concatenate([mask_p, mask_c], axis=1)
            for hd in range(NH):
                sl = slice(hd * HD, (hd + 1) * HD)
                one = slice(hd * HD, hd * HD + 1)
                q, do, kk = qs[:, sl], dos[:, sl], ks[:, sl]
                p = jnp.where(mask, jnp.exp(_dot_nt(q, kk) * scale - ls[:, one]), 0.0)
                ds = (p * (_dot_nt(do, vs[:, sl]) - dls[:, one]) * scale).astype(BF16)
                oq[:, sl] = _dot(ds, kk)
                dk2 = _dot_tn(ds, q)
                dv2 = _dot_tn(p.astype(BF16), do)
                ok[:, sl] = ck[:, sl] + dk2[0:BAND]
                ov[:, sl] = cv[:, sl] + dv2[0:BAND]
                ck[:, sl] = dk2[BAND:]
                cv[:, sl] = dv2[BAND:]
            if first:
                dq_ref[...] = oq[...].reshape(dq_ref.shape)
                dk_ref[...] = ok[...].reshape(dk_ref.shape)
                dv_ref[...] = ov[...].reshape(dv_ref.shape)
            else:
                dq_ref[...] = aq_ref[...] + oq[...].reshape(dq_ref.shape)
                dk_ref[...] = ak_ref[...] + ok[...].reshape(dk_ref.shape)
                dv_ref[...] = av_ref[...] + ov[...].reshape(dv_ref.shape)

        @pl.when(b == nb)
        def _():
            if first:
                dk_ref[...] = ck[...].reshape(dk_ref.shape)
                dv_ref[...] = cv[...].reshape(dv_ref.shape)
            else:
                dk_ref[...] = ak_ref[...] + ck[...].reshape(dk_ref.shape)
                dv_ref[...] = av_ref[...] + cv[...].reshape(dv_ref.shape)

    qb = lambda b: jnp.minimum(b, nb - 1)
    qprev = lambda b: jnp.maximum(qb(b) - 1, 0)
    kb = lambda b: jnp.maximum(b - 1, 0)
    qrow = _pattern_spec(d, T, qb)
    krow = _pattern_spec(d, T, kb)
    view = lambda t: t.reshape(vshape)
    ins = [Pv, Pv, Pv, Pv, Pv, view(dO), view(lse), view(delta)]
    specs = [_pattern_spec(d, T, qb, 0), _pattern_spec(d, T, qprev, 1), _pattern_spec(d, T, qb, 1),
             _pattern_spec(d, T, qprev, 2), _pattern_spec(d, T, qb, 2), qrow, qrow, qrow]
    if not first:
        ins += [view(t) for t in acc]
        specs += [qrow, krow, krow]
    dq, dk, dv = _pc(body, name=f"attn_bwd_d{d}_l{layer}", grid=(d, nb + 1), in_specs=specs,
                     out_specs=[qrow, krow, krow], out_shape=[S(vshape, F32)] * 3,
                     scratch_shapes=[pltpu.VMEM((BAND, DA), BF16)] * 2 + [pltpu.VMEM((2 * BAND, DA), BF16)] * 2
                     + [pltpu.VMEM((BAND, DA), F32)] * 7,
                     compiler_params=_cp(2))(*ins)
    return dq.reshape(T, DA), dk.reshape(T, DA), dv.reshape(T, DA)


def _ssm_prep(lam_re, lam_im, log_dt, b_re, b_im, c_re, c_im):
    dt = jnp.exp(log_dt)[:, None]
    er = jnp.exp(lam_re * dt)
    a_re = er * jnp.cos(lam_im * dt)
    a_im = er * jnp.sin(lam_im * dt)
    nr, ni = a_re - 1.0, a_im
    den = lam_re * lam_re + lam_im * lam_im
    cr = (nr * lam_re + ni * lam_im) / den
    ci = (ni * lam_re - nr * lam_im) / den
    bbr = cr[..., None] * b_re - ci[..., None] * b_im
    bbi = cr[..., None] * b_im + ci[..., None] * b_re
    eye = jnp.eye(8, dtype=F32)

    def bblock(bb):
        t = bb.reshape(4, 8, 64, 16).transpose(0, 1, 3, 2)
        return (t[:, :, :, None, :] * eye[None, :, None, :, None]).reshape(4, 128, 512)

    def cblock(cc):
        t = cc.reshape(4, 8, 16, 64).transpose(0, 1, 3, 2)
        return (t[:, :, :, None, :] * eye[None, :, None, :, None]).reshape(4, 512, 128)

    return (a_re.reshape(NLB, 1, 128), a_im.reshape(NLB, 1, 128), bblock(bbr), bblock(bbi), cblock(c_re), cblock(c_im))


def _perm_matrix(tm):
    n = tm // 16
    pm = np.zeros((tm, tm), np.float32)
    for r in range(16):
        pm[16 * np.arange(n) + r, r * n + np.arange(n)] = 1.0
    return jnp.asarray(pm, BF16)


def _pieces(x):
    p1 = x.astype(BF16)
    r1 = x - p1.astype(F32)
    p2 = r1.astype(BF16)
    return p1, p2, (r1 - p2.astype(F32)).astype(BF16)


def _to_time(x, pm):
    return sum(_dot(pm, p) for p in _pieces(x))


def _to_streams(x, pm):
    return sum(_dot_tn(pm, p) for p in _pieces(x))


def _stream_block(tm, cols, lead=None):
    if lead is None:
        return pl.BlockSpec((16, tm // 16, cols), lambda i: (0, i, 0))
    return pl.BlockSpec((None, 16, tm // 16, cols), lambda i: (lead, 0, i, 0))


def _reorder(t3, to_streams, name):
    B, T, C = t3.shape
    tm = TM

    def body(x_ref, pm_ref, o_ref):
        if to_streams:
            o_ref[...] = _to_streams(x_ref[...], pm_ref[...]).reshape(o_ref.shape)
        else:
            o_ref[...] = _to_time(x_ref[...].reshape(tm, C), pm_ref[...])

    time_blk = pl.BlockSpec((None, tm, C), lambda b, i: (b, i, 0))
    stream_blk = pl.BlockSpec((None, 16, tm // 16, C), lambda b, i: (b, 0, i, 0))
    src = t3 if to_streams else t3.reshape(B, 16, T // 16, C)
    out = _pc(body, name=name, grid=(B, T // tm),
              in_specs=[time_blk if to_streams else stream_blk, pl.BlockSpec((tm, tm), lambda b, i: (0, 0))],
              out_specs=stream_blk if to_streams else time_blk,
              out_shape=S((B, 16, T // 16, C) if to_streams else (B, T, C), F32),
              compiler_params=_cp(2))(src, _perm_matrix(tm))
    return out.reshape(B, T, C)


def _ssm_in(P, bre, bim, layer):
    T = P.shape[1]
    tm = TM

    def body(u_ref, pm_ref, br_ref, bi_ref, un_ref, or_ref, oi_ref):
        u = _to_time(u_ref[...].reshape(tm, DSS), pm_ref[...])
        un_ref[...] = u
        for s in range(4):
            uc = u[:, s * 128:(s + 1) * 128]
            r = _dot3(_dot, uc, br_ref[s])
            m = _dot3(_dot, uc, bi_ref[s])
            for q in range(4):
                or_ref[4 * s + q] = r[:, q * 128:(q + 1) * 128]
                oi_ref[4 * s + q] = m[:, q * 128:(q + 1) * 128]

    whole = pl.BlockSpec((4, 128, 512), lambda i: (0, 0, 0))
    st = pl.BlockSpec((NLB, tm, 128), lambda i: (0, i, 0))
    return _pc(body, name=f"ssm_in_l{layer}", grid=(T // tm,),
               in_specs=[_stream_block(tm, DSS, 3), pl.BlockSpec((tm, tm), lambda i: (0, 0)), whole, whole],
               out_specs=[pl.BlockSpec((tm, DSS), lambda i: (i, 0)), st, st],
               out_shape=[S((T, DSS), F32)] + [S((NLB, T, 128), F32)] * 2,
               compiler_params=_cp(1))(P.reshape(NSH, 16, T // 16, DSS), _perm_matrix(tm), bre, bim)


def _scan(br, bi, a_re, a_im, reverse, layer):
    T = br.shape[1]
    nbk = 2
    tt = min(T, 1024)
    nT = T // tt
    ntile = tt // 8
    sgn = -1.0 if reverse else 1.0
    last = 0 if reverse else 7

    def body(br_ref, bi_ref, ar_ref, ai_ref, xr_ref, xi_ref, cr, ci):
        @pl.when(pl.program_id(1) == 0)
        def _():
            cr[...] = jnp.zeros_like(cr)
            ci[...] = jnp.zeros_like(ci)

        row = lax.broadcasted_iota(jnp.int32, (8, 128), 0)
        consts = []
        for k in range(nbk):
            a1r = jnp.broadcast_to(ar_ref[k], (8, 128))
            a1i = sgn * jnp.broadcast_to(ai_ref[k], (8, 128))
            pows = [(a1r, a1i)]
            for _ in range(7):
                pr, pi_ = pows[-1]
                pows.append((a1r * pr - a1i * pi_, a1r * pi_ + a1i * pr))
            rounds = []
            for s in (1, 2, 4):
                inside = (row <= 7 - s) if reverse else (row >= s)
                rounds.append((jnp.where(inside, pows[s - 1][0], 0.0), jnp.where(inside, pows[s - 1][1], 0.0)))
            cmr, cmi = jnp.zeros((8, 128), F32), jnp.zeros((8, 128), F32)
            for r in range(8):
                e = (7 - r) if reverse else r
                cmr = jnp.where(row == r, pows[e][0], cmr)
                cmi = jnp.where(row == r, pows[e][1], cmi)
            consts.append((rounds, cmr, cmi))

        def tile(i, carry):
            j = (ntile - 1 - i) if reverse else i
            rows = pl.ds(pl.multiple_of(j * 8, 8), 8)
            out = []
            for k in range(nbk):
                rounds, cmr, cmi = consts[k]
                xr = br_ref[k, rows, :]
                xi = bi_ref[k, rows, :]
                for (mr, mi), s in zip(rounds, (1, 2, 4)):
                    sh = (8 - s) if reverse else s
                    rr = pltpu.roll(xr, sh, 0)
                    ri = pltpu.roll(xi, sh, 0)
                    xr, xi = xr + (mr * rr - mi * ri), xi + (mr * ri + mi * rr)
                c_r, c_i = carry[k]
                xr, xi = xr + (cmr * c_r - cmi * c_i), xi + (cmr * c_i + cmi * c_r)
                xr_ref[k, rows, :] = xr
                xi_ref[k, rows, :] = xi
                out.append((jnp.broadcast_to(xr[last:last + 1, :], (8, 128)),
                            jnp.broadcast_to(xi[last:last + 1, :], (8, 128))))
            return tuple(out)

        carry = lax.fori_loop(0, ntile, tile, tuple((cr[k], ci[k]) for k in range(nbk)), unroll=2)
        for k in range(nbk):
            cr[k] = carry[k][0]
            ci[k] = carry[k][1]

    tmap = (lambda t: nT - 1 - t) if reverse else (lambda t: t)
    st = pl.BlockSpec((nbk, tt, 128), lambda i, t: (i, tmap(t), 0))
    av = pl.BlockSpec((nbk, 1, 128), lambda i, t: (i, 0, 0))
    return _pc(body, name=f"scan_{'bwd' if reverse else 'fwd'}_l{layer}", grid=(NLB // nbk, nT),
               in_specs=[st, st, av, av], out_specs=[st, st], out_shape=[S((NLB, T, 128), F32)] * 2,
               scratch_shapes=[pltpu.VMEM((nbk, 8, 128), F32)] * 2, compiler_params=_cp(2))(br, bi, a_re, a_im)


def _ssm_out(xr, xi, u, cre, cim, dvec, wglu, bglu, layer):
    T = u.shape[0]
    tm = TM

    def body(xr_ref, xi_ref, u_ref, pm_ref, cr_ref, ci_ref, d_ref, w_ref, bg_ref, s_ref, y_ref, z_ref):
        ys = []
        for s in range(4):
            xrc = jnp.concatenate([xr_ref[4 * s + q] for q in range(4)], axis=1)
            xic = jnp.concatenate([xi_ref[4 * s + q] for q in range(4)], axis=1)
            ys.append(_dot3(_dot, xrc, cr_ref[s]) - _dot3(_dot, xic, ci_ref[s]))
        y = jnp.concatenate(ys, axis=1) + d_ref[...] * u_ref[...]
        yg = _gelu(y)
        ygb = yg.astype(BF16)
        z = bg_ref[...] + sum(_dot(ygb[:, j * 128:(j + 1) * 128], w_ref[j]) for j in range(NSH))
        y_ref[...] = y
        z_ref[...] = z
        s_ref[...] = _to_streams(yg * jax.nn.sigmoid(z), pm_ref[...]).reshape(s_ref.shape)

    st = pl.BlockSpec((NLB, tm, 128), lambda i: (0, i, 0))
    cw = pl.BlockSpec((4, 512, 128), lambda i: (0, 0, 0))
    half = pl.BlockSpec((tm, DSS), lambda i: (i, 0))
    s, y, z = _pc(body, name=f"ssm_out_l{layer}", grid=(T // tm,),
                  in_specs=[st, st, half, pl.BlockSpec((tm, tm), lambda i: (0, 0)), cw, cw, _gain_spec(DSS, layer),
                            pl.BlockSpec((NSH, None, 128, DSS), lambda i: (0, 0, 0, 0)), _gain_spec(DSS, layer)],
                  out_specs=[_stream_block(tm, DSS), half, half],
                  out_shape=[S((16, T // 16, DSS), F32), S((T, DSS), F32), S((T, DSS), F32)],
                  compiler_params=_cp(1))(xr, xi, u, _perm_matrix(tm), cre, cim, dvec, wglu, bglu)
    return s.reshape(T, DSS), y, z


def _ssm_out_bwd(dssm, y, z, xr, xi, u, cre, cim, dvec, wglu, layer):
    T = u.shape[0]
    tm = TMB

    def body(ds_ref, pm_ref, y_ref, z_ref, xr_ref, xi_ref, u_ref, cr_ref, ci_ref, d_ref, w_ref,
             gr_ref, gi_ref, du_ref, dz_ref, yg_ref, dbg_ref, dd_ref, dcr_ref, dci_ref):
        i = pl.program_id(0)

        @pl.when(i == 0)
        def _():
            dbg_ref[...] = jnp.zeros_like(dbg_ref)
            dd_ref[...] = jnp.zeros_like(dd_ref)
            dcr_ref[...] = jnp.zeros_like(dcr_ref)
            dci_ref[...] = jnp.zeros_like(dci_ref)

        yv = y_ref[...]
        yg = _gelu(yv)
        sg = jax.nn.sigmoid(z_ref[...])
        ds = _to_time(ds_ref[...].reshape(tm, DSS), pm_ref[...])
        dz = ds * yg * sg * (1.0 - sg)
        dzb = dz.astype(BF16)
        dz_ref[...] = dzb
        yg_ref[...] = yg.astype(BF16)
        dbg_ref[...] += jnp.sum(dz, axis=0, keepdims=True)
        dyg = ds * sg + jnp.concatenate([_dot_nt(dzb, w_ref[j]) for j in range(NSH)], axis=1)
        dy = dyg * _gelu_grad(yv)
        u = u_ref[...]
        dd_ref[...] += jnp.sum(dy * u, axis=0, keepdims=True)
        du_ref[...] = dy * d_ref[...]
        for s in range(4):
            dyc = dy[:, s * 128:(s + 1) * 128]
            g_r = _dot3(_dot_nt, dyc, cr_ref[s])
            g_i = -_dot3(_dot_nt, dyc, ci_ref[s])
            for q in range(4):
                gr_ref[4 * s + q] = g_r[:, q * 128:(q + 1) * 128]
                gi_ref[4 * s + q] = g_i[:, q * 128:(q + 1) * 128]
            xrc = jnp.concatenate([xr_ref[4 * s + q] for q in range(4)], axis=1)
            xic = jnp.concatenate([xi_ref[4 * s + q] for q in range(4)], axis=1)
            dcr_ref[s] += _dot3(_dot_tn, xrc, dyc)
            dci_ref[s] -= _dot3(_dot_tn, xic, dyc)

    st = pl.BlockSpec((NLB, tm, 128), lambda i: (0, i, 0))
    cw = pl.BlockSpec((4, 512, 128), lambda i: (0, 0, 0))
    half = pl.BlockSpec((tm, DSS), lambda i: (i, 0))
    return _pc(body, name=f"ssm_out_bwd_l{layer}", grid=(T // tm,),
               in_specs=[_stream_block(tm, DSS), pl.BlockSpec((tm, tm), lambda i: (0, 0)), half, half, st, st, half,
                         cw, cw, _gain_spec(DSS, layer), pl.BlockSpec((NSH, None, 128, DSS), lambda i: (0, 0, 0, 0))],
               out_specs=[st, st, half, half, half, _row_acc_spec(DSS), _row_acc_spec(DSS), cw, cw],
               out_shape=[S((NLB, T, 128), F32)] * 2 + [S((T, DSS), F32), S((T, DSS), BF16), S((T, DSS), BF16),
                                                        S((1, DSS), F32), S((1, DSS), F32),
                                                        S((4, 512, 128), F32), S((4, 512, 128), F32)],
               compiler_params=_cp(1))(dssm.reshape(16, T // 16, DSS), _perm_matrix(tm), y, z, xr, xi, u, cre, cim,
                                       dvec, wglu)


def _ssm_da(gr, gi, xr, xi, layer):
    T = gr.shape[1]
    tb = 1024 if T % 1024 == 0 else T

    def body(gr_ref, gi_ref, xr_ref, xi_ref, dr_ref, di_ref, lr, li):
        t = pl.program_id(1)

        @pl.when(t == 0)
        def _():
            dr_ref[...] = jnp.zeros_like(dr_ref)
            di_ref[...] = jnp.zeros_like(di_ref)
            lr[...] = jnp.zeros_like(lr)
            li[...] = jnp.zeros_like(li)

        g_r, g_i, x_r, x_i = gr_ref[...], gi_ref[...], xr_ref[...], xi_ref[...]
        pr = pltpu.roll(x_r, 1, 0)
        pi_ = pltpu.roll(x_i, 1, 0)
        g0r, g0i = g_r[0:1, :], g_i[0:1, :]
        fr = lr[7:8, :] - x_r[tb - 1:tb, :]
        fi = li[7:8, :] - x_i[tb - 1:tb, :]
        dr_ref[...] += jnp.sum(g_r * pr + g_i * pi_, axis=0, keepdims=True) + g0r * fr + g0i * fi
        di_ref[...] += jnp.sum(g_i * pr - g_r * pi_, axis=0, keepdims=True) + g0i * fr - g0r * fi
        lr[...] = x_r[tb - 8:tb, :]
        li[...] = x_i[tb - 8:tb, :]

    st = pl.BlockSpec((None, tb, 128), lambda k, t: (k, t, 0))
    out = pl.BlockSpec((None, 1, 128), lambda k, t: (k, 0, 0))
    return _pc(body, name=f"ssm_da_l{layer}", grid=(NLB, T // tb), in_specs=[st] * 4, out_specs=[out, out],
               out_shape=[S((NLB, 1, 128), F32)] * 2, scratch_shapes=[pltpu.VMEM((8, 128), F32)] * 2,
               compiler_params=_cp(2))(gr, gi, xr, xi)


def _ssm_in_bwd(gr, gi, u, bre, bim, du_direct, layer):
    T = u.shape[0]
    tm = TM

    def body(gr_ref, gi_ref, u_ref, pm_ref, br_ref, bi_ref, dd_ref, du_ref, dbr_ref, dbi_ref):
        i = pl.program_id(0)

        @pl.when(i == 0)
        def _():
            dbr_ref[...] = jnp.zeros_like(dbr_ref)
            dbi_ref[...] = jnp.zeros_like(dbi_ref)

        dus = []
        for s in range(4):
            grc = jnp.concatenate([gr_ref[4 * s + q] for q in range(4)], axis=1)
            gic = jnp.concatenate([gi_ref[4 * s + q] for q in range(4)], axis=1)
            uc = u_ref[:, s * 128:(s + 1) * 128]
            dus.append(_dot3(_dot_nt, grc, br_ref[s]) + _dot3(_dot_nt, gic, bi_ref[s]))
            dbr_ref[s] += _dot3(_dot_tn, uc, grc)
            dbi_ref[s] += _dot3(_dot_tn, uc, gic)
        du = jnp.concatenate(dus, axis=1) + dd_ref[...]
        du_ref[...] = _to_streams(du, pm_ref[...]).reshape(du_ref.shape)

    whole = pl.BlockSpec((4, 128, 512), lambda i: (0, 0, 0))
    st = pl.BlockSpec((NLB, tm, 128), lambda i: (0, i, 0))
    half = pl.BlockSpec((tm, DSS), lambda i: (i, 0))
    du, dbr, dbi = _pc(body, name=f"ssm_in_bwd_l{layer}", grid=(T // tm,),
                       in_specs=[st, st, half, pl.BlockSpec((tm, tm), lambda i: (0, 0)), whole, whole, half],
                       out_specs=[_stream_block(tm, DSS), whole, whole],
                       out_shape=[S((16, T // 16, DSS), F32), S((4, 128, 512), F32), S((4, 128, 512), F32)],
                       compiler_params=_cp(1))(gr, gi, u, _perm_matrix(tm), bre, bim, du_direct)
    return du.reshape(T, DSS), dbr, dbi


def _mix_out(outs, lses, ssm, h, attn_g, ssm_g, post_g, wout, layer):
    T = h.shape[0]
    tm = TM

    def body(o1, o2, o3, l1, l2, l3, s_ref, h_ref, ag_ref, sg_ref, pg_ref, w_ref, ho_ref, at_ref, ls_ref, mx_ref, mo_ref):
        la, lb, lc = l1[...], l2[...], l3[...]
        m = jnp.maximum(jnp.maximum(la, lb), lc)
        wa, wb, wc = jnp.exp(la - m), jnp.exp(lb - m), jnp.exp(lc - m)
        zs = wa + wb + wc
        attn = (wa * o1[...] + wb * o2[...] + wc * o3[...]) / zs
        at_ref[...] = attn
        ls_ref[...] = m + jnp.log(zs)
        mixed = jnp.concatenate([_rms_fwd(attn, ag_ref[...]), _rms_fwd(s_ref[...], sg_ref[...])], axis=1).astype(BF16)
        mx_ref[...] = mixed
        mo = sum(_dot(mixed[:, j * 256:(j + 1) * 256], w_ref[j]) for j in range(NSH))
        mo_ref[...] = mo
        ho_ref[...] = h_ref[...] + _rms_fwd(mo, pg_ref[...])

    row = pl.BlockSpec((tm, D), lambda i: (i, 0))
    half = pl.BlockSpec((tm, DA), lambda i: (i, 0))
    return _pc(body, name=f"mix_out_l{layer}", grid=(T // tm,),
               in_specs=[half] * 7 + [row, _gain_spec(DA, layer), _gain_spec(DSS, layer), _gain_spec(D, layer),
                                      pl.BlockSpec((NSH, None, 256, D), lambda i: (0, 0, 0, 0))],
               out_specs=[row, half, half, row, row],
               out_shape=[S((T, D), F32), S((T, DA), F32), S((T, DA), F32), S((T, D), BF16), S((T, D), F32)],
               compiler_params=_cp(1))(*outs, *lses, ssm, h, attn_g, ssm_g, post_g, wout)


def _mix_out_bwd(dout, mo, attn, ssm, attn_g, ssm_g, post_g, wout, layer):
    T = dout.shape[0]
    tm = TMB
    head_sum = jnp.asarray(np.kron(np.eye(NH, dtype=np.float32), np.ones((HD, HD), np.float32)), BF16)

    def body(do_ref, mo_ref, at_ref, s_ref, ag_ref, sg_ref, pg_ref, w_ref, e_ref,
             da_ref, ds_ref, dl_ref, dmo_ref, dpg_ref, dag_ref, dsg_ref):
        i = pl.program_id(0)

        @pl.when(i == 0)
        def _():
            dpg_ref[...] = jnp.zeros_like(dpg_ref)
            dag_ref[...] = jnp.zeros_like(dag_ref)
            dsg_ref[...] = jnp.zeros_like(dsg_ref)

        dmo, dpg = _rms_bwd(do_ref[...], mo_ref[...], pg_ref[...])
        dpg_ref[...] += dpg
        dmob = dmo.astype(BF16)
        dmo_ref[...] = dmob
        dmix = jnp.concatenate([_dot_nt(dmob, w_ref[j]) for j in range(NSH)], axis=1)
        attn = at_ref[...]
        dat, dag = _rms_bwd(dmix[:, :DA], attn, ag_ref[...])
        dss, dsg = _rms_bwd(dmix[:, DA:], s_ref[...], sg_ref[...])
        dag_ref[...] += dag
        dsg_ref[...] += dsg
        da_ref[...] = dat
        ds_ref[...] = dss
        prod = dat * attn
        p1 = prod.astype(BF16)
        r1 = prod - p1.astype(F32)
        p2 = r1.astype(BF16)
        p3 = (r1 - p2.astype(F32)).astype(BF16)
        e = e_ref[...]
        dl_ref[...] = _dot(p1, e) + _dot(p2, e) + _dot(p3, e)

    row = pl.BlockSpec((tm, D), lambda i: (i, 0))
    half = pl.BlockSpec((tm, DA), lambda i: (i, 0))
    return _pc(body, name=f"mix_out_bwd_l{layer}", grid=(T // tm,),
               in_specs=[row, row, half, half, _gain_spec(DA, layer), _gain_spec(DSS, layer), _gain_spec(D, layer),
                         pl.BlockSpec((NSH, None, 256, D), lambda i: (0, 0, 0, 0)),
                         pl.BlockSpec((DA, DA), lambda i: (0, 0))],
               out_specs=[half, half, half, row, _row_acc_spec(D), _row_acc_spec(DA), _row_acc_spec(DSS)],
               out_shape=[S((T, DA), F32)] * 3 + [S((T, D), BF16), S((1, D), F32), S((1, DA), F32), S((1, DSS), F32)],
               compiler_params=_cp(1))(dout, mo, attn, ssm, attn_g, ssm_g, post_g, wout, head_sum)


def _ple_fwd(h, p3, wup, wgate, post_g, layer):
    T = h.shape[0]
    tm = TM

    def body(h_ref, p_ref, wu_ref, wg_ref, g_ref, ho_ref, e_ref, gt_ref):
        hv = h_ref[...]
        hb = hv.astype(BF16)
        pb = p_ref[...].astype(BF16)
        gte = sum(_dot(hb[:, j * 256:(j + 1) * 256], wg_ref[j]) for j in range(NSH))
        e = jnp.concatenate([_dot(pb, wu_ref[j]) for j in range(NSH)], axis=1)
        e_ref[...] = e
        gt_ref[...] = gte
        ho_ref[...] = hv + _rms_fwd(e * jax.nn.sigmoid(gte), g_ref[...])

    row = pl.BlockSpec((tm, D), lambda i: (i, 0))
    return _pc(body, name=f"ple_fwd_l{layer}", grid=(T // tm,),
               in_specs=[row, pl.BlockSpec((None, tm, PLE), lambda i: (layer, i, 0)),
                         pl.BlockSpec((NSH, None, PLE, 256), lambda i: (0, 0, 0, 0)),
                         pl.BlockSpec((NSH, None, 256, D), lambda i: (0, 0, 0, 0)), _gain_spec(D, layer)],
               out_specs=[row, row, row], out_shape=[S((T, D), F32)] * 3,
               compiler_params=_cp(1))(h, p3, wup, wgate, post_g)


def _ple_bwd(dout, e, gte, wgate, post_g, layer):
    T = dout.shape[0]
    tm = TMB

    def body(do_ref, e_ref, gt_ref, wg_ref, g_ref, dh_ref, de_ref, dgt_ref, dg_ref):
        i = pl.program_id(0)

        @pl.when(i == 0)
        def _():
            dg_ref[...] = jnp.zeros_like(dg_ref)

        ev = e_ref[...]
        sg = jax.nn.sigmoid(gt_ref[...])
        do = do_ref[...]
        dple, dg = _rms_bwd(do, ev * sg, g_ref[...])
        dg_ref[...] += dg
        de = (dple * sg).astype(BF16)
        for j in range(NSH):
            de_ref[j] = de[:, j * 256:(j + 1) * 256]
        dgb = (dple * ev * sg * (1.0 - sg)).astype(BF16)
        dgt_ref[...] = dgb
        dh_ref[...] = do + jnp.concatenate([_dot_nt(dgb, wg_ref[j]) for j in range(NSH)], axis=1)

    row = pl.BlockSpec((tm, D), lambda i: (i, 0))
    return _pc(body, name=f"ple_bwd_l{layer}", grid=(T // tm,),
               in_specs=[row, row, row, pl.BlockSpec((NSH, None, 256, D), lambda i: (0, 0, 0, 0)), _gain_spec(D, layer)],
               out_specs=[row, pl.BlockSpec((NSH, tm, 256), lambda i: (0, i, 0)), row, _row_acc_spec(D)],
               out_shape=[S((T, D), F32), S((NSH, T, 256), BF16), S((T, D), BF16), S((1, D), F32)],
               compiler_params=_cp(1))(dout, e, gte, wgate, post_g)


def _loss_head(h, target):
    T = h.shape[0]
    tm = TM

    def body(h_ref, t_ref, dy_ref, l_ref):
        i = pl.program_id(0)

        @pl.when(i == 0)
        def _():
            l_ref[...] = jnp.zeros_like(l_ref)

        err = h_ref[...] - t_ref[...]
        dy_ref[...] = err * (1.0 / D)
        l_ref[...] += jnp.broadcast_to((0.5 / D) * jnp.sum(err * err), (1, 128))

    row = pl.BlockSpec((tm, D), lambda i: (i, 0))
    return _pc(body, name="loss_head", grid=(T // tm,), in_specs=[row, row],
               out_specs=[row, pl.BlockSpec((1, 128), lambda i: (0, 0))],
               out_shape=[S((T, D), F32), S((1, 128), F32)], compiler_params=_cp(1))(h, target)


def _local_step(x, p3, pos_col, target, weights_of, upper_grads_done, Sm):
    L = p3.shape[0]
    g3 = {n: Sm[n].reshape(L, 1, -1) for n in ("ffn1_pre_g", "ffn1_post_g", "mix_pre_g", "attn_norm_g", "ssm_norm_g",
                                                "mix_post_g", "ffn2_pre_g", "ffn2_post_g", "ple_post_g", "ssm_b_glu", "ssm_d")}
    rot = _rot_tables(pos_col)
    prep_names = ("ssm_lam_re", "ssm_lam_im", "ssm_log_dt", "ssm_b_re", "ssm_b_im", "ssm_c_re", "ssm_c_im")

    saved = []
    h = x
    for l in range(L):
        W = weights_of(l, h)
        sv = {"h0": h, "W": W}
        h, sv["a1"], sv["b1"], sv["f1"], sv["xn1"] = _ffn_fwd(
            h, g3["ffn1_pre_g"], g3["ffn1_post_g"], W["ffn1_w_gate"], W["ffn1_w_up"], W["ffn1_w_down"], l, "1")
        sv["h1"] = h
        P, sv["ain"] = _mix_proj(h, g3["mix_pre_g"], W["w_in"], rot, l)
        sv["P"] = P
        ol = [_attn_fwd(P, d, l) for d in PATTERN_DILATIONS]
        prep, sv["prep_vjp"] = jax.vjp(_ssm_prep, *[Sm[n][l] for n in prep_names])
        a_re, a_im, bre, bim, cre, cim = prep
        sv["prep"] = prep
        sv["u"], bur, bui = _ssm_in(P, bre, bim, l)
        xr, xi = _scan(bur, bui, a_re, a_im, False, l)
        sv["xr"], sv["xi"] = xr, xi
        ssm, sv["y"], sv["z"] = _ssm_out(xr, xi, sv["u"], cre, cim, g3["ssm_d"], W["ssm_w_glu"], g3["ssm_b_glu"], l)
        sv["ssm"] = ssm
        h, sv["attn"], sv["lse"], sv["mixed"], sv["mo"] = _mix_out(
            [o for o, _ in ol], [s for _, s in ol], ssm, h, g3["attn_norm_g"], g3["ssm_norm_g"], g3["mix_post_g"],
            W["w_out"], l)
        sv["h2"] = h
        h, sv["a2"], sv["b2"], sv["f2"], sv["xn2"] = _ffn_fwd(
            h, g3["ffn2_pre_g"], g3["ffn2_post_g"], W["ffn2_w_gate"], W["ffn2_w_up"], W["ffn2_w_down"], l, "2")
        sv["h3"] = h
        h, sv["e"], sv["gte"] = _ple_fwd(h, p3, W["ple_w_up"], W["ple_w_gate"], g3["ple_post_g"], l)
        saved.append(sv)

    dh, loss = _loss_head(h, target)

    G_upper = {n: lax.empty((NSH, L - 1, r, c), BF16) for n, r, c in BIG} if L > 1 else {}
    G_first = {n: lax.empty((NSH, 1, r, c), BF16) for n, r, c in BIG}
    sg = {n: [None] * L for n in SMALL}
    whole = lambda j: (0, 0)
    shard = lambda j: (j, 0)
    kcol = lambda j: (0, j)
    ple_g = g3["ple_post_g"]
    for l in reversed(range(L)):
        sv = saved[l]
        W = sv["W"]
        G, gl = (G_first, 0) if l == 0 else (G_upper, l - 1)
        if l == 0 and L > 1:
            ple_g = ple_g + upper_grads_done(G_upper)
        dh, de, dgte, sg["ple_post_g"][l] = _ple_bwd(dh, sv["e"], sv["gte"], W["ple_w_gate"], ple_g, l)
        G["ple_w_up"] = _dw(p3[l][None], de, G["ple_w_up"], gl, PLE, 256, whole, shard, f"dw_ple_up_l{l}")
        G["ple_w_gate"] = _dw(sv["h3"][None], dgte[None], G["ple_w_gate"], gl, 256, D, kcol, whole, f"dw_ple_gate_l{l}")
        dh, df, da, db, hh, sg["ffn2_pre_g"][l], sg["ffn2_post_g"][l] = _ffn_bwd(
            dh, sv["h2"], sv["f2"], sv["a2"], sv["b2"], g3["ffn2_pre_g"], g3["ffn2_post_g"],
            W["ffn2_w_gate"], W["ffn2_w_up"], W["ffn2_w_down"], l, "2")
        G["ffn2_w_gate"] = _dw(da, sv["xn2"][None], G["ffn2_w_gate"], gl, DFS, D, shard, whole, f"dw_ffn2_gate_l{l}")
        G["ffn2_w_up"] = _dw(db, sv["xn2"][None], G["ffn2_w_up"], gl, DFS, D, shard, whole, f"dw_ffn2_up_l{l}")
        G["ffn2_w_down"] = _dw(hh, df[None], G["ffn2_w_down"], gl, DFS, D, shard, whole, f"dw_ffn2_down_l{l}")
        a_re, a_im, bre, bim, cre, cim = sv["prep"]
        dattn, dssm, delta, dmo, sg["mix_post_g"][l], sg["attn_norm_g"][l], sg["ssm_norm_g"][l] = _mix_out_bwd(
            dh, sv["mo"], sv["attn"], sv["ssm"], g3["attn_norm_g"], g3["ssm_norm_g"], g3["mix_post_g"], W["w_out"], l)
        G["w_out"] = _dw(sv["mixed"][None], dmo[None], G["w_out"], gl, 256, D, kcol, whole, f"dw_out_l{l}")
        gnr, gni, du_direct, dz, yg, sg["ssm_b_glu"][l], dd, dcre, dcim = _ssm_out_bwd(
            dssm, sv["y"], sv["z"], sv["xr"], sv["xi"], sv["u"], cre, cim, g3["ssm_d"], W["ssm_w_glu"], l)
        sg["ssm_d"][l] = dd.reshape(Sm["ssm_d"].shape[1:])
        G["ssm_w_glu"] = _dw(yg[None], dz[None], G["ssm_w_glu"], gl, 128, DSS, kcol, whole, f"dw_glu_l{l}")
        gr, gi = _scan(gnr, gni, a_re, a_im, True, l)
        dar, dai = _ssm_da(gr, gi, sv["xr"], sv["xi"], l)
        du, dbre, dbim = _ssm_in_bwd(gr, gi, sv["u"], bre, bim, du_direct, l)
        for n, g in zip(prep_names, sv["prep_vjp"]((dar, dai, dbre, dbim, dcre, dcim))):
            sg[n][l] = g
        acc = None
        for d in PATTERN_DILATIONS:
            acc = _attn_bwd(sv["P"], dattn, sv["lse"], delta, acc, d, l)
        dh, dP, sg["mix_pre_g"][l] = _mix_proj_bwd(acc[0], acc[1], acc[2], du, dh, sv["h1"], g3["mix_pre_g"],
                                                   W["w_in"], rot, l)
        G["w_in"] = _dw(sv["ain"][None], dP, G["w_in"], gl, D, DA,whole, shard, f"dw_in_l{l}")
        dh, df, da, db, hh, sg["ffn1_pre_g"][l], sg["ffn1_post_g"][l] = _ffn_bwd(
            dh, sv["h0"], sv["f1"], sv["a1"], sv["b1"], g3["ffn1_pre_g"], g3["ffn1_post_g"],
            W["ffn1_w_gate"], W["ffn1_w_up"], W["ffn1_w_down"], l, "1")
        G["ffn1_w_gate"] = _dw(da, sv["xn1"][None], G["ffn1_w_gate"], gl, DFS, D, shard, whole, f"dw_ffn1_gate_l{l}")
        G["ffn1_w_up"] = _dw(db, sv["xn1"][None], G["ffn1_w_up"], gl, DFS, D, shard, whole, f"dw_ffn1_up_l{l}")
        G["ffn1_w_down"] = _dw(hh, df[None], G["ffn1_w_down"], gl, DFS, D, shard, whole, f"dw_ffn1_down_l{l}")

    small = {n: jnp.stack([g.reshape(Sm[n].shape[1:]) for g in sg[n]]) for n in SMALL}
    return loss, dh, G_upper, G_first, small


HBM_SPEC = pl.BlockSpec(memory_space=pltpu.HBM)


def _place():
    x, y, c = lax.axis_index("x"), lax.axis_index("y"), lax.axis_index("c")
    chips = [(1 - x, y), (x, 1 - y), (1 - x, 1 - y)]
    return x, y, c, chips


def _comm_params():
    return pltpu.CompilerParams(vmem_limit_bytes=VMEM_LIMIT)


def _gather_weights(ws, lands):
    n = len(ws)

    def body(*refs):
        ins, outs = refs[:n], refs[2 * n:3 * n]
        s_ici, r_ici, s_d2d, r_d2d = refs[3 * n:]
        x, y, c, chips = _place()

        def half(ref, t, hc):
            r2 = ws[t].shape[1] // 2
            return ref.at[:, pl.ds(hc * r2, r2), :]

        def ici(t, k, src_chip, to):
            j = 2 * src_chip[0] + src_chip[1]
            src = half(ins[t], t, c) if to is not None else half(outs[t].at[j], t, c)
            return pltpu.make_async_remote_copy(src_ref=src, dst_ref=half(outs[t].at[j], t, c),
                                                send_sem=s_ici.at[3 * t + k], recv_sem=r_ici.at[3 * t + k],
                                                device_id=to if to is not None else (x, y, c), device_id_type=MESH)

        def d2d(t, k, hc):
            j = 2 * chips[k][0] + chips[k][1]
            r = half(outs[t].at[j], t, hc)
            return pltpu.make_async_remote_copy(src_ref=r, dst_ref=r, send_sem=s_d2d.at[3 * t + k],
                                                recv_sem=r_d2d.at[3 * t + k], device_id=(x, y, 1 - c),
                                                device_id_type=MESH)

        sends = [ici(t, k, (x, y), (*chips[k], c)) for t in range(n) for k in range(3)]
        for cp in sends:
            cp.start()
        passed = []
        for t in range(n):
            for k in range(3):
                ici(t, k, chips[k], None).wait_recv()
                passed.append(d2d(t, k, c))
                passed[-1].start()
        for t in range(n):
            for k in range(3):
                d2d(t, k, 1 - c).wait_recv()
        for cp in sends + passed:
            cp.wait_send()

    return _pc(body, name="gather_weights", in_specs=[HBM_SPEC] * (2 * n), out_specs=[HBM_SPEC] * n,
               out_shape=[S(z.shape, z.dtype) for z in lands], input_output_aliases={n + t: t for t in range(n)},
               scratch_shapes=[pltpu.SemaphoreType.DMA((3 * n,))] * 4, compiler_params=_comm_params())(*ws, *lands)


SEM_SPEC = pl.BlockSpec(memory_space=pltpu.SEMAPHORE)
ANY_SPEC = pl.BlockSpec(memory_space=pl.ANY)
SPLIT_EFFECT = pltpu.SideEffectType.DATAFLOW_SIDE_EFFECTING


def _in_hbm(t):
    return pltpu.with_memory_space_constraint(t, pltpu.HBM)


def _place_own(ws, me_arr, layer):
    n = len(ws)

    def body(me_ref, *refs):
        for t in range(n):
            refs[n + t][...] = refs[t][...]

    gs = pltpu.PrefetchScalarGridSpec(
        num_scalar_prefetch=1, grid=(2,),
        in_specs=[pl.BlockSpec((w.shape[0], w.shape[1] // 2, w.shape[2]), lambda i, me: (0, i, 0)) for w in ws],
        out_specs=[pl.BlockSpec((None, w.shape[0], w.shape[1] // 2, w.shape[2]), lambda i, me: (me[0], 0, i, 0))
                   for w in ws])
    return _pc(body, name=f"gather_place_own_l{layer}", grid_spec=gs,
               out_shape=[S((NSH,) + w.shape, w.dtype) for w in ws], compiler_params=_cp(1))(me_arr, *ws)


def _gather_start(ws, lands, after, layer):
    n = len(ws)

    def body(*refs):
        ins, lz = refs[:n], refs[n:2 * n]
        s_sem, r_sem = refs[2 * n + 1], refs[2 * n + 2]
        token = refs[-1]
        x, y, c, chips = _place()
        for t in range(n):
            for k in range(3):
                pltpu.make_async_remote_copy(src_ref=ins[t], dst_ref=lz[t].at[2 * x + y], send_sem=s_sem.at[3 * t + k],
                                             recv_sem=r_sem.at[3 * t + k], device_id=(*chips[k], c),
                                             device_id_type=MESH).start()
        token[...] = jnp.zeros_like(token)

    hbm = [pltpu.HBM(w.shape, w.dtype) for w in ws] + [pltpu.HBM(z.shape, z.dtype) for z in lands]
    out = _pc(body, name=f"gather_start_l{layer}",
              out_shape=(pltpu.SemaphoreType.DMA((3 * n,)), pltpu.SemaphoreType.DMA((3 * n,)), *hbm, S((8, 128), F32)),
              in_specs=[HBM_SPEC] * (2 * n) + [ANY_SPEC],
              out_specs=(SEM_SPEC, SEM_SPEC, *([HBM_SPEC] * (2 * n)), pl.BlockSpec(memory_space=pltpu.VMEM)),
              input_output_aliases={i: 2 + i for i in range(2 * n)},
              compiler_params=pltpu.CompilerParams(has_side_effects=SPLIT_EFFECT))(
                  *[_in_hbm(w) for w in ws], *[_in_hbm(z) for z in lands], after)
    return out[0], out[1], out[2:2 + n], out[2 + n:2 + 2 * n], out[-1]


def _gather_wait(s_sem, r_sem, ws, lands, after, layer):
    n = len(ws)

    def body(*refs):
        ins, lz = refs[:n], refs[n:2 * n]
        s_ref, r_ref = refs[2 * n], refs[2 * n + 1]
        x, y, c, chips = _place()
        for t in range(n):
            for k in range(3):
                cp = pltpu.make_async_remote_copy(src_ref=ins[t], dst_ref=lz[t].at[2 * x + y], send_sem=s_ref.at[3 * t + k],
                                                  recv_sem=r_ref.at[3 * t + k], device_id=(*chips[k], c),
                                                  device_id_type=MESH)
                cp.wait_send()
                cp.wait_recv()

    hbm = [pltpu.HBM(w.shape, w.dtype) for w in ws] + [pltpu.HBM(z.shape, z.dtype) for z in lands]
    out = _pc(body, name=f"gather_wait_l{layer}", out_shape=tuple(hbm),
              in_specs=[HBM_SPEC] * (2 * n) + [SEM_SPEC, SEM_SPEC, ANY_SPEC], out_specs=tuple([HBM_SPEC] * (2 * n)),
              input_output_aliases={i: i for i in range(2 * n)},
              compiler_params=pltpu.CompilerParams(has_side_effects=SPLIT_EFFECT))(*ws, *lands, s_sem, r_sem, after)
    return out[n:]


def _swap_halves(gs, tag):
    n = len(gs)

    def body(*refs):
        ins, outs = refs[:n], refs[n:2 * n]
        s_sem, r_sem = refs[2 * n:]
        x, y, c, _ = _place()
        cps = []
        for t in range(n):
            r2 = gs[t].shape[2] // 2
            cps.append(pltpu.make_async_remote_copy(
                src_ref=ins[t].at[:, :, pl.ds((1 - c) * r2, r2), :], dst_ref=outs[t], send_sem=s_sem.at[t],
                recv_sem=r_sem.at[t], device_id=(x, y, 1 - c), device_id_type=MESH))
            cps[-1].start()
        for cp in cps:
            cp.wait_recv()
        for cp in cps:
            cp.wait_send()

    return _pc(body, name=f"grad_swap_halves_{tag}", in_specs=[HBM_SPEC] * n, out_specs=[HBM_SPEC] * n,
               out_shape=[S(g.shape[:2] + (g.shape[2] // 2, g.shape[3]), g.dtype) for g in gs],
               scratch_shapes=[pltpu.SemaphoreType.DMA((n,))] * 2, compiler_params=_comm_params())(*gs)


def _add_half(g, landed, c_arr, name):
    _, L, r2, cols = landed.shape

    def body(c_ref, g_ref, l_ref, o_ref):
        o_ref[...] = (g_ref[...].astype(F32) + l_ref[...].astype(F32)).astype(BF16)

    gs = pltpu.PrefetchScalarGridSpec(
        num_scalar_prefetch=1, grid=(NSH, L),
        in_specs=[pl.BlockSpec((None, None, r2, cols), lambda j, l, c: (j, l, c[0], 0)),
                  pl.BlockSpec((None, None, r2, cols), lambda j, l, c: (j, l, 0, 0))],
        out_specs=pl.BlockSpec((None, None, r2, cols), lambda j, l, c: (j, l, 0, 0)))
    return _pc(body, name=name, grid_spec=gs, out_shape=S(landed.shape, BF16), compiler_params=_cp(2))(c_arr, g, landed)


def _send_shards(ps):
    n = len(ps)

    def body(*refs):
        ins, outs = refs[:n], refs[n:2 * n]
        s_sem, r_sem = refs[2 * n:]
        x, y, c, chips = _place()
        cps = []
        for t in range(n):
            for k in range(3):
                cps.append(pltpu.make_async_remote_copy(
                    src_ref=ins[t].at[2 * chips[k][0] + chips[k][1]], dst_ref=outs[t].at[k],
                    send_sem=s_sem.at[3 * t + k], recv_sem=r_sem.at[3 * t + k], device_id=(*chips[k], c),
                    device_id_type=MESH))
                cps[-1].start()
        for cp in cps:
            cp.wait_recv()
        for cp in cps:
            cp.wait_send()

    return _pc(body, name="grad_send_shards", in_specs=[HBM_SPEC] * n, out_specs=[HBM_SPEC] * n,
               out_shape=[S((3,) + p.shape[1:], p.dtype) for p in ps],
               scratch_shapes=[pltpu.SemaphoreType.DMA((3 * n,))] * 2, compiler_params=_comm_params())(*ps)


def _send_start(ps, lands):
    n = len(ps)

    def body(*refs):
        ins, lz = refs[:n], refs[n:2 * n]
        s_sem, r_sem = refs[2 * n], refs[2 * n + 1]
        token = refs[-1]
        x, y, c, chips = _place()
        for t in range(n):
            for k in range(3):
                pltpu.make_async_remote_copy(src_ref=ins[t].at[2 * chips[k][0] + chips[k][1]], dst_ref=lz[t].at[k],
                                             send_sem=s_sem.at[3 * t + k], recv_sem=r_sem.at[3 * t + k],
                                             device_id=(*chips[k], c), device_id_type=MESH).start()
        token[...] = jnp.zeros_like(token)

    hbm = [pltpu.HBM(p.shape, p.dtype) for p in ps] + [pltpu.HBM(z.shape, z.dtype) for z in lands]
    out = _pc(body, name="grad_send_start",
              out_shape=(pltpu.SemaphoreType.DMA((3 * n,)), pltpu.SemaphoreType.DMA((3 * n,)), *hbm, S((8, 128), F32)),
              in_specs=[HBM_SPEC] * (2 * n),
              out_specs=(SEM_SPEC, SEM_SPEC, *([HBM_SPEC] * (2 * n)), pl.BlockSpec(memory_space=pltpu.VMEM)),
              input_output_aliases={i: 2 + i for i in range(2 * n)},
              compiler_params=pltpu.CompilerParams(has_side_effects=SPLIT_EFFECT))(
                  *[_in_hbm(p) for p in ps], *[_in_hbm(z) for z in lands])
    return out[0], out[1], out[2:2 + n], out[2 + n:2 + 2 * n], out[-1]


def _send_wait(s_sem, r_sem, ps, lands, after):
    n = len(ps)

    def body(*refs):
        ins, lz = refs[:n], refs[n:2 * n]
        s_ref, r_ref = refs[2 * n], refs[2 * n + 1]
        x, y, c, chips = _place()
        for t in range(n):
            for k in range(3):
                cp = pltpu.make_async_remote_copy(src_ref=ins[t].at[2 * chips[k][0] + chips[k][1]], dst_ref=lz[t].at[k],
                                                  send_sem=s_ref.at[3 * t + k], recv_sem=r_ref.at[3 * t + k],
                                                  device_id=(*chips[k], c), device_id_type=MESH)
                cp.wait_send()
                cp.wait_recv()

    hbm = [pltpu.HBM(p.shape, p.dtype) for p in ps] + [pltpu.HBM(z.shape, z.dtype) for z in lands]
    out = _pc(body, name="grad_send_wait", out_shape=tuple(hbm),
              in_specs=[HBM_SPEC] * (2 * n) + [SEM_SPEC, SEM_SPEC, ANY_SPEC], out_specs=tuple([HBM_SPEC] * (2 * n)),
              input_output_aliases={i: i for i in range(2 * n)},
              compiler_params=pltpu.CompilerParams(has_side_effects=SPLIT_EFFECT))(*ps, *lands, s_sem, r_sem, after)
    return out[:n], out[n:]


def _sum_shards(part, landed, me_arr, c_arr, buf, first_layer, name):
    _, nl, r2, cols = landed.shape

    def body(me_ref, c_ref, p_ref, l_ref, b_ref, o_ref):
        o_ref[...] = ((p_ref[...].astype(F32) + l_ref[0].astype(F32)) + l_ref[1].astype(F32)) + l_ref[2].astype(F32)

    gs = pltpu.PrefetchScalarGridSpec(
        num_scalar_prefetch=2, grid=(nl,),
        in_specs=[pl.BlockSpec((None, None, r2, cols), lambda l, me, c: (me[0], l, 0, 0)),
                  pl.BlockSpec((3, None, r2, cols), lambda l, me, c: (0, l, 0, 0)), ANY_SPEC],
        out_specs=pl.BlockSpec((None, r2, cols), lambda l, me, c: (first_layer + l, c[0], 0)))
    return _pc(body, name=name, grid_spec=gs, out_shape=S(buf.shape, F32), input_output_aliases={4: 0},
               compiler_params=_cp(1))(me_arr, c_arr, part, landed, buf)


def _share_halves(bufs):
    n = len(bufs)

    def body(*refs):
        ins, outs = refs[:n], refs[n:2 * n]
        s_sem, r_sem = refs[2 * n:]
        x, y, c, _ = _place()
        cps = []
        for t in range(n):
            r2 = bufs[t].shape[1] // 2
            cps.append(pltpu.make_async_remote_copy(
                src_ref=ins[t].at[:, pl.ds(c * r2, r2), :], dst_ref=outs[t].at[:, pl.ds(c * r2, r2), :],
                send_sem=s_sem.at[t], recv_sem=r_sem.at[t], device_id=(x, y, 1 - c), device_id_type=MESH))
            cps[-1].start()
        for cp in cps:
            cp.wait_recv()
        for cp in cps:
            cp.wait_send()

    return _pc(body, name="grad_share_halves", in_specs=[HBM_SPEC] * n, out_specs=[HBM_SPEC] * n,
               out_shape=[S(b.shape, b.dtype) for b in bufs], input_output_aliases={t: t for t in range(n)},
               scratch_shapes=[pltpu.SemaphoreType.DMA((n,))] * 2, compiler_params=_comm_params())(*bufs)


def _gather_small(v):
    nr = v.shape[0]

    def body(v_ref, out_ref, send_sems, recv_sems, local_sem):
        x, y, c, chips = _place()
        me, sibling = (x, y, c), (x, y, 1 - c)

        def rows(px, py, pc):
            return out_ref.at[pl.ds((4 * px + 2 * py + pc) * nr, nr), :]

        def copy(k, block, to, src=None):
            return pltpu.make_async_remote_copy(src_ref=rows(*block) if src is None else src, dst_ref=rows(*block),
                                                send_sem=send_sems.at[k], recv_sem=recv_sems.at[k], device_id=to,
                                                device_id_type=MESH)

        mine = pltpu.make_async_copy(v_ref, rows(*me), local_sem)
        mine.start()
        first = [copy(0, me, sibling, src=v_ref)]
        first += [copy(1 + j, me, (*chip, c), src=v_ref) for j, chip in enumerate(chips)]
        for cp in first:
            cp.start()
        passed = [copy(4 + j, (*chip, c), sibling) for j, chip in enumerate(chips)]
        for j, chip in enumerate(chips):
            copy(1 + j, (*chip, c), me).wait_recv()
            passed[j].start()
        copy(0, sibling, me).wait_recv()
        for j, chip in enumerate(chips):
            copy(4 + j, (*chip, 1 - c), me).wait_recv()
        for cp in first + passed:
            cp.wait_send()
        mine.wait()

    vm = pl.BlockSpec(memory_space=pltpu.VMEM)
    return _pc(body, name="gather_small_grads", in_specs=[vm], out_specs=vm, out_shape=S((8 * nr, 128), F32),
               scratch_shapes=[pltpu.SemaphoreType.DMA((7,)), pltpu.SemaphoreType.DMA((7,)), pltpu.SemaphoreType.DMA],
               compiler_params=_comm_params())(v)


def _adamw_math(w, g, m, v):
    m2 = ADAM_B1 * m + (1.0 - ADAM_B1) * g
    v2 = ADAM_B2 * v + (1.0 - ADAM_B2) * (g * g)
    m_hat = m2 / (1.0 - ADAM_B1 ** ADAM_STEP)
    v_hat = v2 / (1.0 - ADAM_B2 ** ADAM_STEP)
    return -ADAM_LR * (m_hat / (jnp.sqrt(v_hat) + ADAM_EPS) + ADAM_WD * w), m2, v2


def _adamw(w, g, m, v, name):
    L, R, C = w.shape
    rb = R // 2 if R >= 512 else R

    def body(w_ref, g_ref, m_ref, v_ref, d_ref, m2_ref, v2_ref):
        d_ref[...], m2_ref[...], v2_ref[...] = _adamw_math(w_ref[...], g_ref[...], m_ref[...], v_ref[...])

    blk = pl.BlockSpec((None, rb, C), lambda l, r: (l, r, 0))
    return _pc(body, name=name, grid=(L, R // rb), in_specs=[blk] * 4, out_specs=[blk] * 3,
               out_shape=[S(w.shape, F32)] * 3, compiler_params=_cp(2))(w, g, m, v)


def _adamw_small(gathered, w, m, v):
    nr = w.shape[0]
    rb = nr // 5

    def body(a_ref, w_ref, m_ref, v_ref, g_ref, d_ref, m2_ref, v2_ref):
        g = a_ref[0]
        for k in range(1, 8):
            g = g + a_ref[k]
        g_ref[...] = g
        d_ref[...], m2_ref[...], v2_ref[...] = _adamw_math(w_ref[...], g, m_ref[...], v_ref[...])

    blk = pl.BlockSpec((rb, 128), lambda i: (i, 0))
    return _pc(body, name="adamw_small", grid=(nr // rb,), in_specs=[pl.BlockSpec((8, rb, 128), lambda i: (0, i, 0))] + [blk] * 3,
               out_specs=[blk] * 4, out_shape=[S((nr, 128), F32)] * 4, compiler_params=_cp(1))(gathered, w, m, v)


SMALL_ROWS = 4520


def _pack(arrs):
    flat = jnp.concatenate([a.reshape(-1) for a in arrs])
    return jnp.pad(flat, (0, SMALL_ROWS * 128 - flat.shape[0])).reshape(SMALL_ROWS, 128)


def _unpack(packed, like):
    flat = packed.reshape(-1)
    out, off = [], 0
    for a in like:
        out.append(flat[off:off + a.size].reshape(a.shape))
        off += a.size
    return out


def kernel(x, p, positions, ffn1_pre_g, ffn1_w_gate, ffn1_w_up, ffn1_w_down, ffn1_post_g, mix_pre_g, w_in, attn_norm_g, ssm_lam_re, ssm_lam_im, ssm_log_dt, ssm_b_re, ssm_b_im, ssm_c_re, ssm_c_im, ssm_d, ssm_w_glu, ssm_b_glu, ssm_norm_g, w_out, mix_post_g, ffn2_pre_g, ffn2_w_gate, ffn2_w_up, ffn2_w_down, ffn2_post_g, ple_w_up, ple_w_gate, ple_post_g, loss_target, m_ffn1_pre_g, m_ffn1_w_gate, m_ffn1_w_up, m_ffn1_w_down, m_ffn1_post_g, m_mix_pre_g, m_w_in, m_attn_norm_g, m_ssm_lam_re, m_ssm_lam_im, m_ssm_log_dt, m_ssm_b_re, m_ssm_b_im, m_ssm_c_re, m_ssm_c_im, m_ssm_d, m_ssm_w_glu, m_ssm_b_glu, m_ssm_norm_g, m_w_out, m_mix_post_g, m_ffn2_pre_g, m_ffn2_w_gate, m_ffn2_w_up, m_ffn2_w_down, m_ffn2_post_g, m_ple_w_up, m_ple_w_gate, m_ple_post_g, v_ffn1_pre_g, v_ffn1_w_gate, v_ffn1_w_up, v_ffn1_w_down, v_ffn1_post_g, v_mix_pre_g, v_w_in, v_attn_norm_g, v_ssm_lam_re, v_ssm_lam_im, v_ssm_log_dt, v_ssm_b_re, v_ssm_b_im, v_ssm_c_re, v_ssm_c_im, v_ssm_d, v_ssm_w_glu, v_ssm_b_glu, v_ssm_norm_g, v_w_out, v_mix_post_g, v_ffn2_pre_g, v_ffn2_w_gate, v_ffn2_w_up, v_ffn2_w_down, v_ffn2_post_g, v_ple_w_up, v_ple_w_gate, v_ple_post_g):
    a = dict(locals())
    T = x.shape[1]
    big_names = [n for n, _, _ in BIG]
    for n in TRANSPOSED:
        for pre in ("", "m_", "v_"):
            a[pre + n] = jnp.swapaxes(a[pre + n], 1, 2)

    own = [a[n].astype(BF16) for n in big_names]
    n_layers = own[0].shape[0]
    per_layer = [[w[l:l + 1] for w in own] for l in range(n_layers)]
    c_arr = lax.axis_index("c").astype(jnp.int32).reshape(1)
    me_arr = (2 * lax.axis_index("x") + lax.axis_index("y")).astype(jnp.int32).reshape(1)
    first = dict(zip(big_names, _gather_weights(per_layer[0], _place_own(per_layer[0], me_arr, 0))))
    pending, anchor, queued_behind = {}, jnp.zeros((), F32), first[big_names[0]]
    for l in range(1, n_layers):
        s_sem, r_sem, ws_thru, lands_thru, token = _gather_start(per_layer[l], _place_own(per_layer[l], me_arr, l),
                                                                 queued_behind, l)
        pending[l] = (s_sem, r_sem, ws_thru, lands_thru)
        anchor = anchor + token[0, 0]
        queued_behind = token

    def weights_of(l, after):
        if l == 0:
            return first
        return dict(zip(big_names, _gather_wait(*pending[l], after, l)))

    Sm = {n: a[n] for n in SMALL}
    Sm["ffn1_pre_g"] = Sm["ffn1_pre_g"] + anchor

    pos = jnp.broadcast_to(positions.reshape(1, T, 1).astype(F32), (1, T, 128))
    def chip_partials(G, tag):
        gs = [G[n] for n in big_names]
        landed = _swap_halves(gs, tag)
        return [_add_half(g, la, c_arr, f"grad_add_half_{tag}_{n}") for g, la, n in zip(gs, landed, big_names)]

    upper = {}

    def upper_grads_done(G_upper):
        parts = chip_partials(G_upper, "upper")
        lands = [lax.empty((3,) + pt.shape[1:], BF16) for pt in parts]
        s_sem, r_sem, parts_thru, lands_thru, token = _send_start(parts, lands)
        upper["pending"] = (s_sem, r_sem, parts_thru, lands_thru)
        return token[0, 0]

    loss, gx, G_upper, G_first, small = _local_step(
        _reorder(x, True, "to_streams_x")[0], _reorder(p[:, 0], True, "to_streams_p"),
        _reorder(pos, True, "to_streams_pos")[0, :, :1], _reorder(loss_target, True, "to_streams_target")[0],
        weights_of, upper_grads_done, Sm)
    gx = _reorder(gx[None], False, "to_time_grad_x")

    bufs = [lax.empty((n_layers, r, c), F32) for _, r, c in BIG]
    if n_layers > 1:
        parts, landed = _send_wait(*upper["pending"], gx)
        bufs = [_sum_shards(pt, la, me_arr, c_arr, b, 1, f"grad_sum_shards_upper_{n}")
                for pt, la, b, n in zip(parts, landed, bufs, big_names)]
    parts = chip_partials(G_first, "first")
    landed = _send_shards(parts)
    bufs = [_sum_shards(pt, la, me_arr, c_arr, b, 0, f"grad_sum_shards_first_{n}")
            for pt, la, b, n in zip(parts, landed, bufs, big_names)]
    grads = dict(zip(big_names, _share_halves(bufs)))

    small_g = _gather_small(_pack([small[n] for n in SMALL])).reshape(8, SMALL_ROWS, 128)
    sg, sd, sm, sv = _adamw_small(small_g, _pack([a[n] for n in SMALL]), _pack([a["m_" + n] for n in SMALL]),
                                  _pack([a["v_" + n] for n in SMALL]))
    like = [a[n] for n in SMALL]
    res = {}
    for n, g_, d_, m_, v_ in zip(SMALL, _unpack(sg, like), _unpack(sd, like), _unpack(sm, like), _unpack(sv, like)):
        res[n] = (g_, d_, m_, v_)
    for n in big_names:
        d_, m_, v_ = _adamw(a[n], grads[n], a["m_" + n], a["v_" + n], f"adamw_{n}")
        res[n] = (grads[n], d_, m_, v_)
        if n in TRANSPOSED:
            res[n] = tuple(jnp.swapaxes(t, 1, 2) for t in res[n])

    total = lax.psum(loss[0, 0], ("x", "y", "c"))
    return (total, gx, *[res[n][0] for n in WEIGHTS], *[res[n][1] for n in WEIGHTS],
            *[res[n][2] for n in WEIGHTS], *[res[n][3] for n in WEIGHTS])
```

```python
import functools
import math

import numpy as np
import jax
import jax.numpy as jnp
from jax import lax
from jax.experimental import pallas as pl
from jax.experimental.pallas import tpu as pltpu

F32 = jnp.float32
BF16 = jnp.bfloat16
S = jax.ShapeDtypeStruct
MESH = pl.DeviceIdType.MESH

D = 1024
DA = 512
DSS = 512
HD = 64
NH = 8
BAND = 128
NSH = 4
DFS = 704
PLE = 256
EPS = 1e-6
ROPE_THETA = 500000.0
PATTERN_DILATIONS = (1, 4, 16)
NLB = 16
ADAM_LR, ADAM_B1, ADAM_B2, ADAM_EPS, ADAM_WD, ADAM_STEP = 0.001, 0.9, 0.999, 1e-08, 0.01, 10

VMEM_LIMIT = 56 * 1024 * 1024
TM = 512
TMB = 256

BIG = (
    ("ffn1_w_gate", DFS, D), ("ffn1_w_up", DFS, D), ("ffn1_w_down", DFS, D),
    ("w_in", D, 512), ("ssm_w_glu", 128, 512), ("w_out", 256, D),
    ("ffn2_w_gate", DFS, D), ("ffn2_w_up", DFS, D), ("ffn2_w_down", DFS, D),
    ("ple_w_up", PLE, 256), ("ple_w_gate", 256, D),
)
TRANSPOSED = ("ffn1_w_gate", "ffn1_w_up", "ffn2_w_gate", "ffn2_w_up")
SMALL = ("ffn1_pre_g", "ffn1_post_g", "mix_pre_g", "attn_norm_g", "ssm_lam_re", "ssm_lam_im", "ssm_log_dt",
         "ssm_b_re", "ssm_b_im", "ssm_c_re", "ssm_c_im", "ssm_d", "ssm_b_glu", "ssm_norm_g", "mix_post_g",
         "ffn2_pre_g", "ffn2_post_g", "ple_post_g")
WEIGHTS = ("ffn1_pre_g", "ffn1_w_gate", "ffn1_w_up", "ffn1_w_down", "ffn1_post_g", "mix_pre_g", "w_in", "attn_norm_g",
           "ssm_lam_re", "ssm_lam_im", "ssm_log_dt", "ssm_b_re", "ssm_b_im", "ssm_c_re", "ssm_c_im", "ssm_d",
           "ssm_w_glu", "ssm_b_glu", "ssm_norm_g", "w_out", "mix_post_g", "ffn2_pre_g", "ffn2_w_gate", "ffn2_w_up",
           "ffn2_w_down", "ffn2_post_g", "ple_w_up", "ple_w_gate", "ple_post_g")


def _pc(body, **kw):
    return pl.pallas_call(body, **kw)


def _cp(n_grid):
    return pltpu.CompilerParams(dimension_semantics=("arbitrary",) * n_grid, vmem_limit_bytes=VMEM_LIMIT)


def _dot(a, b):
    return jnp.dot(a, b, preferred_element_type=F32)


def _dot_nt(a, b):
    return lax.dot_general(a, b, (((1,), (1,)), ((), ())), preferred_element_type=F32)


def _dot_tn(a, b):
    return lax.dot_general(a, b, (((0,), (0,)), ((), ())), preferred_element_type=F32)


def _split(a):
    hi = a.astype(BF16)
    return hi, (a - hi.astype(F32)).astype(BF16)


def _dot3(fn, a, b):
    ah, al = _split(a)
    bh, bl = _split(b)
    return fn(ah, bh) + fn(ah, bl) + fn(al, bh)


def _rms_fwd(x, g):
    r = lax.rsqrt(jnp.mean(x * x, axis=-1, keepdims=True) + EPS)
    return x * r * g


def _rms_bwd(dy, x, g):
    r = lax.rsqrt(jnp.mean(x * x, axis=-1, keepdims=True) + EPS)
    xr = x * r
    gd = dy * g
    dx = r * (gd - xr * jnp.mean(gd * xr, axis=-1, keepdims=True))
    dg = jnp.sum(dy * xr, axis=0, keepdims=True)
    return dx, dg


def _gelu(y):
    k = math.sqrt(2.0 / math.pi)
    return 0.5 * y * (1.0 + jnp.tanh(k * (y + 0.044715 * y * y * y)))


def _gelu_grad(y):
    k = math.sqrt(2.0 / math.pi)
    t = jnp.tanh(k * (y + 0.044715 * y * y * y))
    return 0.5 * (1.0 + t) + 0.5 * y * (1.0 - t * t) * k * (1.0 + 3 * 0.044715 * y * y)


def _gain_spec(n, layer):
    return pl.BlockSpec((None, 1, n), lambda *_: (layer, 0, 0))


def _row_acc_spec(n):
    return pl.BlockSpec((1, n), lambda *_: (0, 0))


def _rot_tables(pos_col):
    T = pos_col.shape[0]
    half = HD // 8
    inv = (ROPE_THETA ** (-np.arange(half, dtype=np.float32) * (2.0 / (2 * half)))).astype(np.float32)
    lane_freq = np.tile(np.concatenate([inv, inv, np.zeros(HD - 2 * half, np.float32)]), NH)[None, :]

    def body(p_ref, f_ref, c_ref, s1_ref, s2_ref):
        ang = p_ref[...] * f_ref[...]
        d = lax.broadcasted_iota(jnp.int32, ang.shape, 1) % HD
        cs = jnp.cos(ang)
        sn = jnp.sin(ang)
        c_ref[...] = jnp.where(d < 2 * half, cs, 1.0)
        s1_ref[...] = jnp.where(d < half, -sn, 0.0)
        s2_ref[...] = jnp.where((d >= half) & (d < 2 * half), sn, 0.0)

    tm = TM
    return _pc(body, name="rot_tables", grid=(T // tm,),
               in_specs=[pl.BlockSpec((tm, 1), lambda i: (i, 0)), pl.BlockSpec((1, DA), lambda i: (0, 0))],
               out_specs=[pl.BlockSpec((tm, DA), lambda i: (i, 0))] * 3,
               out_shape=[S((T, DA), F32)] * 3, compiler_params=_cp(1))(pos_col, jnp.asarray(lane_freq))


def _rot_fwd(t, c, s1, s2):
    return t * c + pltpu.roll(t, DA - 8, 1) * s1 + pltpu.roll(t, 8, 1) * s2


def _rot_bwd(g, c, s1, s2):
    return g * c + pltpu.roll(g * s1, 8, 1) + pltpu.roll(g * s2, DA - 8, 1)


def _ffn_weight_spec():
    return pl.BlockSpec((NSH, None, DFS, D), lambda i: (0, 0, 0, 0), pipeline_mode=pl.Buffered(1))


def _ffn_fwd(h, pre_g, post_g, wg, wu, wd, layer, tag):
    T = h.shape[0]
    tm = TM
    nt = T // tm

    def body(h_ref, pg_ref, qg_ref, wg_ref, wu_ref, wd_ref, ho_ref, a_ref, b_ref, f_ref, xn_ref):
        hv = h_ref[...]
        xb = _rms_fwd(hv, pg_ref[...]).astype(BF16)
        xn_ref[...] = xb
        f = None
        for j in range(NSH):
            ab = _dot_nt(xb, wg_ref[j]).astype(BF16)
            bb = _dot_nt(xb, wu_ref[j]).astype(BF16)
            a_ref[j] = ab
            b_ref[j] = bb
            a = ab.astype(F32)
            hh = (a * jax.nn.sigmoid(a) * bb.astype(F32)).astype(BF16)
            part = _dot(hh, wd_ref[j])
            f = part if f is None else f + part
        f_ref[...] = f
        ho_ref[...] = hv + 0.5 * _rms_fwd(f, qg_ref[...])

    row = pl.BlockSpec((tm, D), lambda i: (i, 0))
    act = pl.BlockSpec((NSH, tm, DFS), lambda i: (0, i, 0))
    return _pc(body, name=f"ffn_fwd_{tag}_l{layer}", grid=(nt,),
               in_specs=[row, _gain_spec(D, layer), _gain_spec(D, layer)] + [_ffn_weight_spec()] * 3,
               out_specs=[row, act, act, row, row],
               out_shape=[S((T, D), F32), S((NSH, T, DFS), BF16), S((NSH, T, DFS), BF16), S((T, D), F32), S((T, D), BF16)],
               compiler_params=_cp(1))(h, pre_g, post_g, wg, wu, wd)


def _ffn_bwd(dout, h, f, a, b, pre_g, post_g, wg, wu, wd, layer, tag):
    T = h.shape[0]
    tm = TMB
    nt = T // tm

    def body(do_ref, h_ref, f_ref, a_ref, b_ref, pg_ref, qg_ref, wg_ref, wu_ref, wd_ref,
             dh_ref, df_ref, da_ref, db_ref, hh_ref, dpg_ref, dqg_ref):
        @pl.when(pl.program_id(0) == 0)
        def _():
            dpg_ref[...] = jnp.zeros_like(dpg_ref)
            dqg_ref[...] = jnp.zeros_like(dqg_ref)

        do = do_ref[...]
        df, dq = _rms_bwd(0.5 * do, f_ref[...], qg_ref[...])
        dqg_ref[...] += dq
        dfb = df.astype(BF16)
        df_ref[...] = dfb
        dxn = None
        for j in range(NSH):
            dhh = _dot_nt(dfb, wd_ref[j])
            av = a_ref[j].astype(F32)
            bv = b_ref[j].astype(F32)
            sg = jax.nn.sigmoid(av)
            sa = av * sg
            hh_ref[j] = (sa * bv).astype(BF16)
            dab = (dhh * bv * (sg + sa * (1.0 - sg))).astype(BF16)
            dbb = (dhh * sa).astype(BF16)
            da_ref[j] = dab
            db_ref[j] = dbb
            part = _dot(dab, wg_ref[j]) + _dot(dbb, wu_ref[j])
            dxn = part if dxn is None else dxn + part
        dx, dp = _rms_bwd(dxn, h_ref[...], pg_ref[...])
        dpg_ref[...] += dp
        dh_ref[...] = do + dx

    row = pl.BlockSpec((tm, D), lambda i: (i, 0))
    act = pl.BlockSpec((NSH, tm, DFS), lambda i: (0, i, 0))
    return _pc(body, name=f"ffn_bwd_{tag}_l{layer}", grid=(nt,),
               in_specs=[row, row, row, act, act, _gain_spec(D, layer), _gain_spec(D, layer)] + [_ffn_weight_spec()] * 3,
               out_specs=[row, row, act, act, act, _row_acc_spec(D), _row_acc_spec(D)],
               out_shape=[S((T, D), F32), S((T, D), BF16), S((NSH, T, DFS), BF16), S((NSH, T, DFS), BF16),
                          S((NSH, T, DFS), BF16), S((1, D), F32), S((1, D), F32)],
               compiler_params=_cp(1))(dout, h, f, a, b, pre_g, post_g, wg, wu, wd)


def _dw(A, B, buf, layer, kb, nb, a_mode, b_mode, name):
    T = A.shape[1]
    tt = TM
    nt = T // tt

    def pick(v, mode, j, w):
        if mode == "shard":
            return v[j]
        return v[0] if mode == "whole" else v[0][:, j * w:(j + 1) * w]

    def body(a_ref, b_ref, buf_ref, o_ref, acc):
        t = pl.program_id(0)

        @pl.when(t == 0)
        def _():
            acc[...] = jnp.zeros_like(acc)

        av = a_ref[...].astype(BF16)
        bv = b_ref[...].astype(BF16)
        for j in range(NSH):
            acc[j] += _dot_tn(pick(av, a_mode, j, kb), pick(bv, b_mode, j, nb))

        @pl.when(t == nt - 1)
        def _():
            o_ref[...] = acc[...].astype(o_ref.dtype)

    return _pc(body, name=name, grid=(nt,),
               in_specs=[pl.BlockSpec((A.shape[0], tt, A.shape[2]), lambda t: (0, t, 0)),
                         pl.BlockSpec((B.shape[0], tt, B.shape[2]), lambda t: (0, t, 0)),
                         pl.BlockSpec(memory_space=pl.ANY)],
               out_specs=pl.BlockSpec((NSH, None, kb, nb), lambda t: (0, layer, 0, 0)),
               out_shape=S(buf.shape, buf.dtype), input_output_aliases={2: 0},
               scratch_shapes=[pltpu.VMEM((NSH, kb, nb), F32)], compiler_params=_cp(1))(A, B, buf)


def _mix_proj(h, pre_g, win, rot, layer):
    T = h.shape[0]
    tm = TM

    def body(h_ref, g_ref, w_ref, c_ref, s1_ref, s2_ref, p_ref, xn_ref):
        xb = _rms_fwd(h_ref[...], g_ref[...]).astype(BF16)
        xn_ref[...] = xb
        for j in range(NSH):
            o = _dot(xb, w_ref[j])
            p_ref[j] = _rot_fwd(o, c_ref[...], s1_ref[...], s2_ref[...]) if j < 2 else o

    row = pl.BlockSpec((tm, D), lambda i: (i, 0))
    half = pl.BlockSpec((tm, DA), lambda i: (i, 0))
    return _pc(body, name=f"mix_proj_l{layer}", grid=(T // tm,),
               in_specs=[row, _gain_spec(D, layer), pl.BlockSpec((NSH, None, D, DA), lambda i: (0, 0, 0, 0)),
                         half, half, half],
               out_specs=[pl.BlockSpec((NSH, tm, DA), lambda i: (0, i, 0)), row],
               out_shape=[S((NSH, T, DA), F32), S((T, D), BF16)], compiler_params=_cp(1))(h, pre_g, win, *rot)


def _mix_proj_bwd(dq, dk, dv, du, dh_up, h, pre_g, win, rot, layer):
    T = h.shape[0]
    tm = TM

    def body(dq_ref, dk_ref, dv_ref, du_ref, up_ref, h_ref, g_ref, w_ref, c_ref, s1_ref, s2_ref,
             dh_ref, dp_ref, dg_ref):
        @pl.when(pl.program_id(0) == 0)
        def _():
            dg_ref[...] = jnp.zeros_like(dg_ref)

        rot = (c_ref[...], s1_ref[...], s2_ref[...])
        dps = [_rot_bwd(dq_ref[...], *rot), _rot_bwd(dk_ref[...], *rot), dv_ref[...], du_ref[...]]
        dxn = None
        for j in range(NSH):
            dpb = dps[j].astype(BF16)
            dp_ref[j] = dpb
            part = _dot_nt(dpb, w_ref[j])
            dxn = part if dxn is None else dxn + part
        dx, dg = _rms_bwd(dxn, h_ref[...], g_ref[...])
        dg_ref[...] += dg
        dh_ref[...] = up_ref[...] + dx

    row = pl.BlockSpec((tm, D), lambda i: (i, 0))
    half = pl.BlockSpec((tm, DA), lambda i: (i, 0))
    return _pc(body, name=f"mix_proj_bwd_l{layer}", grid=(T // tm,),
               in_specs=[half, half, half, half, row, row, _gain_spec(D, layer),
                         pl.BlockSpec((NSH, None, D, DA), lambda i: (0, 0, 0, 0)), half, half, half],
               out_specs=[row, pl.BlockSpec((NSH, tm, DA), lambda i: (0, i, 0)), _row_acc_spec(D)],
               out_shape=[S((T, D), F32), S((NSH, T, DA), BF16), S((1, D), F32)],
               compiler_params=_cp(1))(dq, dk, dv, du, dh_up, h, pre_g, win, *rot)


def _stream_pos(d, axis):
    i = lax.broadcasted_iota(jnp.int32, (BAND, BAND), axis)
    if d == 16:
        return i
    if d == 4:
        return 4 * (i % 32) + i // 32
    return 16 * (i % 8) + i // 8


def _band_masks(b, d):
    qi, kj = _stream_pos(d, 0), _stream_pos(d, 1)
    return kj <= qi, (kj >= qi) & (b > 0)


def _pattern(d, T):
    n16 = T // 16
    if d == 16:
        return (16, n16, DA), (None, BAND, DA), lambda r, k: (r, k, 0)
    if d == 4:
        return (4, 4, n16, DA), (4, None, 32, DA), lambda r, k: (0, r, k, 0)
    return (16, n16, DA), (16, 8, DA), lambda r, k: (0, k, 0)


def _pattern_spec(d, T, kmap, lead=None):
    _, blk, idx = _pattern(d, T)
    if lead is None:
        return pl.BlockSpec(blk, lambda r, b: idx(r, kmap(b)))
    return pl.BlockSpec((None,) + blk, lambda r, b: (lead,) + idx(r, kmap(b)))


def _attn_fwd(P, d, layer):
    T = P.shape[1]
    nb = T // d // BAND
    vshape = _pattern(d, T)[0]
    Pv = P.reshape((NSH,) + vshape)
    scale = HD ** -0.5

    def body(q_ref, kp_ref, kc_ref, vp_ref, vc_ref, o_ref, l_ref, qs, ks, vs, osc, lsc):
        b = pl.program_id(1)
        flat = lambda ref: ref[...].reshape(BAND, DA).astype(BF16)
        qs[...] = flat(q_ref)
        ks[0:BAND, :] = flat(kp_ref)
        ks[BAND:, :] = flat(kc_ref)
        vs[0:BAND, :] = flat(vp_ref)
        vs[BAND:, :] = flat(vc_ref)
        mask_c, mask_p = _band_masks(b, d)
        mask = jnp.concatenate([mask_p, mask_c], axis=1)
        for hd in range(NH):
            sl = slice(hd * HD, (hd + 1) * HD)
            s = jnp.where(mask, _dot_nt(qs[:, sl], ks[:, sl]) * scale, -1e30)
            m = jnp.max(s, axis=-1, keepdims=True)
            e = jnp.exp(s - m)
            den = jnp.sum(e, axis=-1, keepdims=True)
            osc[:, sl] = _dot(e.astype(BF16), vs[:, sl]) / den
            lsc[:, sl] = jnp.broadcast_to(m + jnp.log(den), (BAND, HD))
        o_ref[...] = osc[...].reshape(o_ref.shape)
        l_ref[...] = lsc[...].reshape(l_ref.shape)

    cur = lambda b: b
    prev = lambda b: jnp.maximum(b - 1, 0)
    out = _pattern_spec(d, T, cur)
    o, l = _pc(body, name=f"attn_fwd_d{d}_l{layer}", grid=(d, nb),
               in_specs=[_pattern_spec(d, T, cur, 0), _pattern_spec(d, T, prev, 1), _pattern_spec(d, T, cur, 1),
                         _pattern_spec(d, T, prev, 2), _pattern_spec(d, T, cur, 2)],
               out_specs=[out, out], out_shape=[S(vshape, F32)] * 2,
               scratch_shapes=[pltpu.VMEM((BAND, DA), BF16)] + [pltpu.VMEM((2 * BAND, DA), BF16)] * 2
               + [pltpu.VMEM((BAND, DA), F32)] * 2,
               compiler_params=_cp(2))(Pv, Pv, Pv, Pv, Pv)
    return o.reshape(T, DA), l.reshape(T, DA)


def _attn_bwd(P, dO, lse, delta, acc, d, layer):
    T = P.shape[1]
    nb = T // d // BAND
    vshape = _pattern(d, T)[0]
    Pv = P.reshape((NSH,) + vshape)
    scale = HD ** -0.5
    first = acc is None

    def body(*refs):
        q_ref, kp_ref, kc_ref, vp_ref, vc_ref, do_ref, l_ref, dl_ref = refs[:8]
        if first:
            dq_ref, dk_ref, dv_ref = refs[8:11]
        else:
            aq_ref, ak_ref, av_ref, dq_ref, dk_ref, dv_ref = refs[8:14]
        qs, dos, ks, vs, ls, dls, oq, ok, ov, ck, cv = refs[-11:]
        b = pl.program_id(1)
        flat = lambda ref: ref[...].reshape(BAND, DA)

        @pl.when(b == 0)
        def _():
            ck[...] = jnp.zeros_like(ck)
            cv[...] = jnp.zeros_like(cv)

        @pl.when(b < nb)
        def _():
            qs[...] = flat(q_ref).astype(BF16)
            dos[...] = flat(do_ref).astype(BF16)
            ks[0:BAND, :] = flat(kp_ref).astype(BF16)
            ks[BAND:, :] = flat(kc_ref).astype(BF16)
            vs[0:BAND, :] = flat(vp_ref).astype(BF16)
            vs[BAND:, :] = flat(vc_ref).astype(BF16)
            ls[...] = flat(l_ref)
            dls[...] = flat(dl_ref)
            mask_c, mask_p = _band_masks(b, d)
            mask = jnp.concatenate([mask_p, mask_c], axis=1)
            for hd in range(NH):
                sl = slice(hd * HD, (hd + 1) * HD)
                one = slice(hd * HD, hd * HD + 1)
                q, do, kk = qs[:, sl], dos[:, sl], ks[:, sl]
                p = jnp.where(mask, jnp.exp(_dot_nt(q, kk) * scale - ls[:, one]), 0.0)
                ds = (p * (_dot_nt(do, vs[:, sl]) - dls[:, one]) * scale).astype(BF16)
                oq[:, sl] = _dot(ds, kk)
                dk2 = _dot_tn(ds, q)
                dv2 = _dot_tn(p.astype(BF16), do)
                ok[:, sl] = ck[:, sl] + dk2[0:BAND]
                ov[:, sl] = cv[:, sl] + dv2[0:BAND]
                ck[:, sl] = dk2[BAND:]
                cv[:, sl] = dv2[BAND:]
            if first:
                dq_ref[...] = oq[...].reshape(dq_ref.shape)
                dk_ref[...] = ok[...].reshape(dk_ref.shape)
                dv_ref[...] = ov[...].reshape(dv_ref.shape)
            else:
                dq_ref[...] = aq_ref[...] + oq[...].reshape(dq_ref.shape)
                dk_ref[...] = ak_ref[...] + ok[...].reshape(dk_ref.shape)
                dv_ref[...] = av_ref[...] + ov[...].reshape(dv_ref.shape)

        @pl.when(b == nb)
        def _():
            if first:
                dk_ref[...] = ck[...].reshape(dk_ref.shape)
                dv_ref[...] = cv[...].reshape(dv_ref.shape)
            else:
                dk_ref[...] = ak_ref[...] + ck[...].reshape(dk_ref.shape)
                dv_ref[...] = av_ref[...] + cv[...].reshape(dv_ref.shape)

    qb = lambda b: jnp.minimum(b, nb - 1)
    qprev = lambda b: jnp.maximum(qb(b) - 1, 0)
    kb = lambda b: jnp.maximum(b - 1, 0)
    qrow = _pattern_spec(d, T, qb)
    krow = _pattern_spec(d, T, kb)
    view = lambda t: t.reshape(vshape)
    ins = [Pv, Pv, Pv, Pv, Pv, view(dO), view(lse), view(delta)]
    specs = [_pattern_spec(d, T, qb, 0), _pattern_spec(d, T, qprev, 1), _pattern_spec(d, T, qb, 1),
             _pattern_spec(d, T, qprev, 2), _pattern_spec(d, T, qb, 2), qrow, qrow, qrow]
    if not first:
        ins += [view(t) for t in acc]
        specs += [qrow, krow, krow]
    dq, dk, dv = _pc(body, name=f"attn_bwd_d{d}_l{layer}", grid=(d, nb + 1), in_specs=specs,
                     out_specs=[qrow, krow, krow], out_shape=[S(vshape, F32)] * 3,
                     scratch_shapes=[pltpu.VMEM((BAND, DA), BF16)] * 2 + [pltpu.VMEM((2 * BAND, DA), BF16)] * 2
                     + [pltpu.VMEM((BAND, DA), F32)] * 7,
                     compiler_params=_cp(2))(*ins)
    return dq.reshape(T, DA), dk.reshape(T, DA), dv.reshape(T, DA)


def _ssm_prep(lam_re, lam_im, log_dt, b_re, b_im, c_re, c_im):
    dt = jnp.exp(log_dt)[:, None]
    er = jnp.exp(lam_re * dt)
    a_re = er * jnp.cos(lam_im * dt)
    a_im = er * jnp.sin(lam_im * dt)
    nr, ni = a_re - 1.0, a_im
    den = lam_re * lam_re + lam_im * lam_im
    cr = (nr * lam_re + ni * lam_im) / den
    ci = (ni * lam_re - nr * lam_im) / den
    bbr = cr[..., None] * b_re - ci[..., None] * b_im
    bbi = cr[..., None] * b_im + ci[..., None] * b_re
    eye = jnp.eye(8, dtype=F32)

    def bblock(bb):
        t = bb.reshape(4, 8, 64, 16).transpose(0, 1, 3, 2)
        return (t[:, :, :, None, :] * eye[None, :, None, :, None]).reshape(4, 128, 512)

    def cblock(cc):
        t = cc.reshape(4, 8, 16, 64).transpose(0, 1, 3, 2)
        return (t[:, :, :, None, :] * eye[None, :, None, :, None]).reshape(4, 512, 128)

    return (a_re.reshape(NLB, 1, 128), a_im.reshape(NLB, 1, 128), bblock(bbr), bblock(bbi), cblock(c_re), cblock(c_im))


def _perm_matrix(tm):
    n = tm // 16
    pm = np.zeros((tm, tm), np.float32)
    for r in range(16):
        pm[16 * np.arange(n) + r, r * n + np.arange(n)] = 1.0
    return jnp.asarray(pm, BF16)


def _pieces(x):
    p1 = x.astype(BF16)
    r1 = x - p1.astype(F32)
    p2 = r1.astype(BF16)
    return p1, p2, (r1 - p2.astype(F32)).astype(BF16)


def _to_time(x, pm):
    return sum(_dot(pm, p) for p in _pieces(x))


def _to_streams(x, pm):
    return sum(_dot_tn(pm, p) for p in _pieces(x))


def _stream_block(tm, cols, lead=None):
    if lead is None:
        return pl.BlockSpec((16, tm // 16, cols), lambda i: (0, i, 0))
    return pl.BlockSpec((None, 16, tm // 16, cols), lambda i: (lead, 0, i, 0))


def _reorder(t3, to_streams, name):
    B, T, C = t3.shape
    tm = TM

    def body(x_ref, pm_ref, o_ref):
        if to_streams:
            o_ref[...] = _to_streams(x_ref[...], pm_ref[...]).reshape(o_ref.shape)
        else:
            o_ref[...] = _to_time(x_ref[...].reshape(tm, C), pm_ref[...])

    time_blk = pl.BlockSpec((None, tm, C), lambda b, i: (b, i, 0))
    stream_blk = pl.BlockSpec((None, 16, tm // 16, C), lambda b, i: (b, 0, i, 0))
    src = t3 if to_streams else t3.reshape(B, 16, T // 16, C)
    out = _pc(body, name=name, grid=(B, T // tm),
              in_specs=[time_blk if to_streams else stream_blk, pl.BlockSpec((tm, tm), lambda b, i: (0, 0))],
              out_specs=stream_blk if to_streams else time_blk,
              out_shape=S((B, 16, T // 16, C) if to_streams else (B, T, C), F32),
              compiler_params=_cp(2))(src, _perm_matrix(tm))
    return out.reshape(B, T, C)


def _ssm_in(P, bre, bim, layer):
    T = P.shape[1]
    tm = TM

    def body(u_ref, pm_ref, br_ref, bi_ref, un_ref, or_ref, oi_ref):
        u = _to_time(u_ref[...].reshape(tm, DSS), pm_ref[...])
        un_ref[...] = u
        for s in range(4):
            uc = u[:, s * 128:(s + 1) * 128]
            r = _dot3(_dot, uc, br_ref[s])
            m = _dot3(_dot, uc, bi_ref[s])
            for q in range(4):
                or_ref[4 * s + q] = r[:, q * 128:(q + 1) * 128]
                oi_ref[4 * s + q] = m[:, q * 128:(q + 1) * 128]

    whole = pl.BlockSpec((4, 128, 512), lambda i: (0, 0, 0))
    st = pl.BlockSpec((NLB, tm, 128), lambda i: (0, i, 0))
    return _pc(body, name=f"ssm_in_l{layer}", grid=(T // tm,),
               in_specs=[_stream_block(tm, DSS, 3), pl.BlockSpec((tm, tm), lambda i: (0, 0)), whole, whole],
               out_specs=[pl.BlockSpec((tm, DSS), lambda i: (i, 0)), st, st],
               out_shape=[S((T, DSS), F32)] + [S((NLB, T, 128), F32)] * 2,
               compiler_params=_cp(1))(P.reshape(NSH, 16, T // 16, DSS), _perm_matrix(tm), bre, bim)


def _scan(br, bi, a_re, a_im, reverse, layer):
    T = br.shape[1]
    nbk = 4
    tt = min(T, 1024)
    nT = T // tt
    ntile = tt // 8
    sgn = -1.0 if reverse else 1.0
    last = 0 if reverse else 7

    def body(br_ref, bi_ref, ar_ref, ai_ref, xr_ref, xi_ref, cr, ci):
        @pl.when(pl.program_id(1) == 0)
        def _():
            cr[...] = jnp.zeros_like(cr)
            ci[...] = jnp.zeros_like(ci)

        row = lax.broadcasted_iota(jnp.int32, (8, 128), 0)
        consts = []
        for k in range(nbk):
            a1r = jnp.broadcast_to(ar_ref[k], (8, 128))
            a1i = sgn * jnp.broadcast_to(ai_ref[k], (8, 128))
            pows = [(a1r, a1i)]
            for _ in range(7):
                pr, pi_ = pows[-1]
                pows.append((a1r * pr - a1i * pi_, a1r * pi_ + a1i * pr))
            rounds = []
            for s in (1, 2, 4):
                inside = (row <= 7 - s) if reverse else (row >= s)
                rounds.append((jnp.where(inside, pows[s - 1][0], 0.0), jnp.where(inside, pows[s - 1][1], 0.0)))
            cmr, cmi = jnp.zeros((8, 128), F32), jnp.zeros((8, 128), F32)
            for r in range(8):
                e = (7 - r) if reverse else r
                cmr = jnp.where(row == r, pows[e][0], cmr)
                cmi = jnp.where(row == r, pows[e][1], cmi)
            consts.append((rounds, cmr, cmi))

        def tile(i, carry):
            j = (ntile - 1 - i) if reverse else i
            rows = pl.ds(pl.multiple_of(j * 8, 8), 8)
            out = []
            for k in range(nbk):
                rounds, cmr, cmi = consts[k]
                xr = br_ref[k, rows, :]
                xi = bi_ref[k, rows, :]
                for (mr, mi), s in zip(rounds, (1, 2, 4)):
                    sh = (8 - s) if reverse else s
                    rr = pltpu.roll(xr, sh, 0)
                    ri = pltpu.roll(xi, sh, 0)
                    xr, xi = xr + (mr * rr - mi * ri), xi + (mr * ri + mi * rr)
                c_r, c_i = carry[k]
                xr, xi = xr + (cmr * c_r - cmi * c_i), xi + (cmr * c_i + cmi * c_r)
                xr_ref[k, rows, :] = xr
                xi_ref[k, rows, :] = xi
                out.append((jnp.broadcast_to(xr[last:last + 1, :], (8, 128)),
                            jnp.broadcast_to(xi[last:last + 1, :], (8, 128))))
            return tuple(out)

        carry = lax.fori_loop(0, ntile, tile, tuple((cr[k], ci[k]) for k in range(nbk)), unroll=2)
        for k in range(nbk):
            cr[k] = carry[k][0]
            ci[k] = carry[k][1]

    tmap = (lambda t: nT - 1 - t) if reverse else (lambda t: t)
    st = pl.BlockSpec((nbk, tt, 128), lambda i, t: (i, tmap(t), 0))
    av = pl.BlockSpec((nbk, 1, 128), lambda i, t: (i, 0, 0))
    return _pc(body, name=f"scan_{'bwd' if reverse else 'fwd'}_l{layer}", grid=(NLB // nbk, nT),
               in_specs=[st, st, av, av], out_specs=[st, st], out_shape=[S((NLB, T, 128), F32)] * 2,
               scratch_shapes=[pltpu.VMEM((nbk, 8, 128), F32)] * 2, compiler_params=_cp(2))(br, bi, a_re, a_im)


def _ssm_out(xr, xi, u, cre, cim, dvec, wglu, bglu, layer):
    T = u.shape[0]
    tm = TM

    def body(xr_ref, xi_ref, u_ref, pm_ref, cr_ref, ci_ref, d_ref, w_ref, bg_ref, s_ref, y_ref, z_ref):
        ys = []
        for s in range(4):
            xrc = jnp.concatenate([xr_ref[4 * s + q] for q in range(4)], axis=1)
            xic = jnp.concatenate([xi_ref[4 * s + q] for q in range(4)], axis=1)
            ys.append(_dot3(_dot, xrc, cr_ref[s]) - _dot3(_dot, xic, ci_ref[s]))
        y = jnp.concatenate(ys, axis=1) + d_ref[...] * u_ref[...]
        yg = _gelu(y)
        ygb = yg.astype(BF16)
        z = bg_ref[...] + sum(_dot(ygb[:, j * 128:(j + 1) * 128], w_ref[j]) for j in range(NSH))
        y_ref[...] = y
        z_ref[...] = z
        s_ref[...] = _to_streams(yg * jax.nn.sigmoid(z), pm_ref[...]).reshape(s_ref.shape)

    st = pl.BlockSpec((NLB, tm, 128), lambda i: (0, i, 0))
    cw = pl.BlockSpec((4, 512, 128), lambda i: (0, 0, 0))
    half = pl.BlockSpec((tm, DSS), lambda i: (i, 0))
    s, y, z = _pc(body, name=f"ssm_out_l{layer}", grid=(T // tm,),
                  in_specs=[st, st, half, pl.BlockSpec((tm, tm), lambda i: (0, 0)), cw, cw, _gain_spec(DSS, layer),
                            pl.BlockSpec((NSH, None, 128, DSS), lambda i: (0, 0, 0, 0)), _gain_spec(DSS, layer)],
                  out_specs=[_stream_block(tm, DSS), half, half],
                  out_shape=[S((16, T // 16, DSS), F32), S((T, DSS), F32), S((T, DSS), F32)],
                  compiler_params=_cp(1))(xr, xi, u, _perm_matrix(tm), cre, cim, dvec, wglu, bglu)
    return s.reshape(T, DSS), y, z


def _ssm_out_bwd(dssm, y, z, xr, xi, u, cre, cim, dvec, wglu, layer):
    T = u.shape[0]
    tm = TMB

    def body(ds_ref, pm_ref, y_ref, z_ref, xr_ref, xi_ref, u_ref, cr_ref, ci_ref, d_ref, w_ref,
             gr_ref, gi_ref, du_ref, dz_ref, yg_ref, dbg_ref, dd_ref, dcr_ref, dci_ref):
        i = pl.program_id(0)

        @pl.when(i == 0)
        def _():
            dbg_ref[...] = jnp.zeros_like(dbg_ref)
            dd_ref[...] = jnp.zeros_like(dd_ref)
            dcr_ref[...] = jnp.zeros_like(dcr_ref)
            dci_ref[...] = jnp.zeros_like(dci_ref)

        yv = y_ref[...]
        yg = _gelu(yv)
        sg = jax.nn.sigmoid(z_ref[...])
        ds = _to_time(ds_ref[...].reshape(tm, DSS), pm_ref[...])
        dz = ds * yg * sg * (1.0 - sg)
        dzb = dz.astype(BF16)
        dz_ref[...] = dzb
        yg_ref[...] = yg.astype(BF16)
        dbg_ref[...] += jnp.sum(dz, axis=0, keepdims=True)
        dyg = ds * sg + jnp.concatenate([_dot_nt(dzb, w_ref[j]) for j in range(NSH)], axis=1)
        dy = dyg * _gelu_grad(yv)
        u = u_ref[...]
        dd_ref[...] += jnp.sum(dy * u, axis=0, keepdims=True)
        du_ref[...] = dy * d_ref[...]
        for s in range(4):
            dyc = dy[:, s * 128:(s + 1) * 128]
            g_r = _dot3(_dot_nt, dyc, cr_ref[s])
            g_i = -_dot3(_dot_nt, dyc, ci_ref[s])
            for q in range(4):
                gr_ref[4 * s + q] = g_r[:, q * 128:(q + 1) * 128]
                gi_ref[4 * s + q] = g_i[:, q * 128:(q + 1) * 128]
            xrc = jnp.concatenate([xr_ref[4 * s + q] for q in range(4)], axis=1)
            xic = jnp.concatenate([xi_ref[4 * s + q] for q in range(4)], axis=1)
            dcr_ref[s] += _dot3(_dot_tn, xrc, dyc)
            dci_ref[s] -= _dot3(_dot_tn, xic, dyc)

    st = pl.BlockSpec((NLB, tm, 128), lambda i: (0, i, 0))
    cw = pl.BlockSpec((4, 512, 128), lambda i: (0, 0, 0))
    half = pl.BlockSpec((tm, DSS), lambda i: (i, 0))
    return _pc(body, name=f"ssm_out_bwd_l{layer}", grid=(T // tm,),
               in_specs=[_stream_block(tm, DSS), pl.BlockSpec((tm, tm), lambda i: (0, 0)), half, half, st, st, half,
                         cw, cw, _gain_spec(DSS, layer), pl.BlockSpec((NSH, None, 128, DSS), lambda i: (0, 0, 0, 0))],
               out_specs=[st, st, half, half, half, _row_acc_spec(DSS), _row_acc_spec(DSS), cw, cw],
               out_shape=[S((NLB, T, 128), F32)] * 2 + [S((T, DSS), F32), S((T, DSS), BF16), S((T, DSS), BF16),
                                                        S((1, DSS), F32), S((1, DSS), F32),
                                                        S((4, 512, 128), F32), S((4, 512, 128), F32)],
               compiler_params=_cp(1))(dssm.reshape(16, T // 16, DSS), _perm_matrix(tm), y, z, xr, xi, u, cre, cim,
                                       dvec, wglu)


def _ssm_da(gr, gi, xr, xi, layer):
    T = gr.shape[1]
    tb = 4096 if T % 4096 == 0 else T

    def body(gr_ref, gi_ref, xr_ref, xi_ref, dr_ref, di_ref, lr, li):
        t = pl.program_id(1)

        @pl.when(t == 0)
        def _():
            dr_ref[...] = jnp.zeros_like(dr_ref)
            di_ref[...] = jnp.zeros_like(di_ref)
            lr[...] = jnp.zeros_like(lr)
            li[...] = jnp.zeros_like(li)

        g_r, g_i, x_r, x_i = gr_ref[...], gi_ref[...], xr_ref[...], xi_ref[...]
        pr = pltpu.roll(x_r, 1, 0)
        pi_ = pltpu.roll(x_i, 1, 0)
        g0r, g0i = g_r[0:1, :], g_i[0:1, :]
        fr = lr[7:8, :] - x_r[tb - 1:tb, :]
        fi = li[7:8, :] - x_i[tb - 1:tb, :]
        dr_ref[...] += jnp.sum(g_r * pr + g_i * pi_, axis=0, keepdims=True) + g0r * fr + g0i * fi
        di_ref[...] += jnp.sum(g_i * pr - g_r * pi_, axis=0, keepdims=True) + g0i * fr - g0r * fi
        lr[...] = x_r[tb - 8:tb, :]
        li[...] = x_i[tb - 8:tb, :]

    st = pl.BlockSpec((None, tb, 128), lambda k, t: (k, t, 0))
    out = pl.BlockSpec((None, 1, 128), lambda k, t: (k, 0, 0))
    return _pc(body, name=f"ssm_da_l{layer}", grid=(NLB, T // tb), in_specs=[st] * 4, out_specs=[out, out],
               out_shape=[S((NLB, 1, 128), F32)] * 2, scratch_shapes=[pltpu.VMEM((8, 128), F32)] * 2,
               compiler_params=_cp(2))(gr, gi, xr, xi)


def _ssm_in_bwd(gr, gi, u, bre, bim, du_direct, layer):
    T = u.shape[0]
    tm = TM

    def body(gr_ref, gi_ref, u_ref, pm_ref, br_ref, bi_ref, dd_ref, du_ref, dbr_ref, dbi_ref):
        i = pl.program_id(0)

        @pl.when(i == 0)
        def _():
            dbr_ref[...] = jnp.zeros_like(dbr_ref)
            dbi_ref[...] = jnp.zeros_like(dbi_ref)

        dus = []
        for s in range(4):
            grc = jnp.concatenate([gr_ref[4 * s + q] for q in range(4)], axis=1)
            gic = jnp.concatenate([gi_ref[4 * s + q] for q in range(4)], axis=1)
            uc = u_ref[:, s * 128:(s + 1) * 128]
            dus.append(_dot3(_dot_nt, grc, br_ref[s]) + _dot3(_dot_nt, gic, bi_ref[s]))
            dbr_ref[s] += _dot3(_dot_tn, uc, grc)
            dbi_ref[s] += _dot3(_dot_tn, uc, gic)
        du = jnp.concatenate(dus, axis=1) + dd_ref[...]
        du_ref[...] = _to_streams(du, pm_ref[...]).reshape(du_ref.shape)

    whole = pl.BlockSpec((4, 128, 512), lambda i: (0, 0, 0))
    st = pl.BlockSpec((NLB, tm, 128), lambda i: (0, i, 0))
    half = pl.BlockSpec((tm, DSS), lambda i: (i, 0))
    du, dbr, dbi = _pc(body, name=f"ssm_in_bwd_l{layer}", grid=(T // tm,),
                       in_specs=[st, st, half, pl.BlockSpec((tm, tm), lambda i: (0, 0)), whole, whole, half],
                       out_specs=[_stream_block(tm, DSS), whole, whole],
                       out_shape=[S((16, T // 16, DSS), F32), S((4, 128, 512), F32), S((4, 128, 512), F32)],
                       compiler_params=_cp(1))(gr, gi, u, _perm_matrix(tm), bre, bim, du_direct)
    return du.reshape(T, DSS), dbr, dbi


def _mix_out(outs, lses, ssm, h, attn_g, ssm_g, post_g, wout, layer):
    T = h.shape[0]
    tm = TM

    def body(o1, o2, o3, l1, l2, l3, s_ref, h_ref, ag_ref, sg_ref, pg_ref, w_ref, ho_ref, at_ref, ls_ref, mx_ref, mo_ref):
        la, lb, lc = l1[...], l2[...], l3[...]
        m = jnp.maximum(jnp.maximum(la, lb), lc)
        wa, wb, wc = jnp.exp(la - m), jnp.exp(lb - m), jnp.exp(lc - m)
        zs = wa + wb + wc
        attn = (wa * o1[...] + wb * o2[...] + wc * o3[...]) / zs
        at_ref[...] = attn
        ls_ref[...] = m + jnp.log(zs)
        mixed = jnp.concatenate([_rms_fwd(attn, ag_ref[...]), _rms_fwd(s_ref[...], sg_ref[...])], axis=1).astype(BF16)
        mx_ref[...] = mixed
        mo = sum(_dot(mixed[:, j * 256:(j + 1) * 256], w_ref[j]) for j in range(NSH))
        mo_ref[...] = mo
        ho_ref[...] = h_ref[...] + _rms_fwd(mo, pg_ref[...])

    row = pl.BlockSpec((tm, D), lambda i: (i, 0))
    half = pl.BlockSpec((tm, DA), lambda i: (i, 0))
    return _pc(body, name=f"mix_out_l{layer}", grid=(T // tm,),
               in_specs=[half] * 7 + [row, _gain_spec(DA, layer), _gain_spec(DSS, layer), _gain_spec(D, layer),
                                      pl.BlockSpec((NSH, None, 256, D), lambda i: (0, 0, 0, 0))],
               out_specs=[row, half, half, row, row],
               out_shape=[S((T, D), F32), S((T, DA), F32), S((T, DA), F32), S((T, D), BF16), S((T, D), F32)],
               compiler_params=_cp(1))(*outs, *lses, ssm, h, attn_g, ssm_g, post_g, wout)


def _mix_out_bwd(dout, mo, attn, ssm, attn_g, ssm_g, post_g, wout, layer):
    T = dout.shape[0]
    tm = TMB
    head_sum = jnp.asarray(np.kron(np.eye(NH, dtype=np.float32), np.ones((HD, HD), np.float32)), BF16)

    def body(do_ref, mo_ref, at_ref, s_ref, ag_ref, sg_ref, pg_ref, w_ref, e_ref,
             da_ref, ds_ref, dl_ref, dmo_ref, dpg_ref, dag_ref, dsg_ref):
        i = pl.program_id(0)

        @pl.when(i == 0)
        def _():
            dpg_ref[...] = jnp.zeros_like(dpg_ref)
            dag_ref[...] = jnp.zeros_like(dag_ref)
            dsg_ref[...] = jnp.zeros_like(dsg_ref)

        dmo, dpg = _rms_bwd(do_ref[...], mo_ref[...], pg_ref[...])
        dpg_ref[...] += dpg
        dmob = dmo.astype(BF16)
        dmo_ref[...] = dmob
        dmix = jnp.concatenate([_dot_nt(dmob, w_ref[j]) for j in range(NSH)], axis=1)
        attn = at_ref[...]
        dat, dag = _rms_bwd(dmix[:, :DA], attn, ag_ref[...])
        dss, dsg = _rms_bwd(dmix[:, DA:], s_ref[...], sg_ref[...])
        dag_ref[...] += dag
        dsg_ref[...] += dsg
        da_ref[...] = dat
        ds_ref[...] = dss
        prod = dat * attn
        p1 = prod.astype(BF16)
        r1 = prod - p1.astype(F32)
        p2 = r1.astype(BF16)
        p3 = (r1 - p2.astype(F32)).astype(BF16)
        e = e_ref[...]
        dl_ref[...] = _dot(p1, e) + _dot(p2, e) + _dot(p3, e)

    row = pl.BlockSpec((tm, D), lambda i: (i, 0))
    half = pl.BlockSpec((tm, DA), lambda i: (i, 0))
    return _pc(body, name=f"mix_out_bwd_l{layer}", grid=(T // tm,),
               in_specs=[row, row, half, half, _gain_spec(DA, layer), _gain_spec(DSS, layer), _gain_spec(D, layer),
                         pl.BlockSpec((NSH, None, 256, D), lambda i: (0, 0, 0, 0)),
                         pl.BlockSpec((DA, DA), lambda i: (0, 0))],
               out_specs=[half, half, half, row, _row_acc_spec(D), _row_acc_spec(DA), _row_acc_spec(DSS)],
               out_shape=[S((T, DA), F32)] * 3 + [S((T, D), BF16), S((1, D), F32), S((1, DA), F32), S((1, DSS), F32)],
               compiler_params=_cp(1))(dout, mo, attn, ssm, attn_g, ssm_g, post_g, wout, head_sum)


def _ple_fwd(h, p3, wup, wgate, post_g, layer):
    T = h.shape[0]
    tm = TM

    def body(h_ref, p_ref, wu_ref, wg_ref, g_ref, ho_ref, e_ref, gt_ref):
        hv = h_ref[...]
        hb = hv.astype(BF16)
        pb = p_ref[...].astype(BF16)
        gte = sum(_dot(hb[:, j * 256:(j + 1) * 256], wg_ref[j]) for j in range(NSH))
        e = jnp.concatenate([_dot(pb, wu_ref[j]) for j in range(NSH)], axis=1)
        e_ref[...] = e
        gt_ref[...] = gte
        ho_ref[...] = hv + _rms_fwd(e * jax.nn.sigmoid(gte), g_ref[...])

    row = pl.BlockSpec((tm, D), lambda i: (i, 0))
    return _pc(body, name=f"ple_fwd_l{layer}", grid=(T // tm,),
               in_specs=[row, pl.BlockSpec((None, tm, PLE), lambda i: (layer, i, 0)),
                         pl.BlockSpec((NSH, None, PLE, 256), lambda i: (0, 0, 0, 0)),
                         pl.BlockSpec((NSH, None, 256, D), lambda i: (0, 0, 0, 0)), _gain_spec(D, layer)],
               out_specs=[row, row, row], out_shape=[S((T, D), F32)] * 3,
               compiler_params=_cp(1))(h, p3, wup, wgate, post_g)


def _ple_bwd(dout, e, gte, wgate, post_g, layer):
    T = dout.shape[0]
    tm = TMB

    def body(do_ref, e_ref, gt_ref, wg_ref, g_ref, dh_ref, de_ref, dgt_ref, dg_ref):
        i = pl.program_id(0)

        @pl.when(i == 0)
        def _():
            dg_ref[...] = jnp.zeros_like(dg_ref)

        ev = e_ref[...]
        sg = jax.nn.sigmoid(gt_ref[...])
        do = do_ref[...]
        dple, dg = _rms_bwd(do, ev * sg, g_ref[...])
        dg_ref[...] += dg
        de = (dple * sg).astype(BF16)
        for j in range(NSH):
            de_ref[j] = de[:, j * 256:(j + 1) * 256]
        dgb = (dple * ev * sg * (1.0 - sg)).astype(BF16)
        dgt_ref[...] = dgb
        dh_ref[...] = do + jnp.concatenate([_dot_nt(dgb, wg_ref[j]) for j in range(NSH)], axis=1)

    row = pl.BlockSpec((tm, D), lambda i: (i, 0))
    return _pc(body, name=f"ple_bwd_l{layer}", grid=(T // tm,),
               in_specs=[row, row, row, pl.BlockSpec((NSH, None, 256, D), lambda i: (0, 0, 0, 0)), _gain_spec(D, layer)],
               out_specs=[row, pl.BlockSpec((NSH, tm, 256), lambda i: (0, i, 0)), row, _row_acc_spec(D)],
               out_shape=[S((T, D), F32), S((NSH, T, 256), BF16), S((T, D), BF16), S((1, D), F32)],
               compiler_params=_cp(1))(dout, e, gte, wgate, post_g)


def _loss_head(h, target):
    T = h.shape[0]
    tm = TM

    def body(h_ref, t_ref, dy_ref, l_ref):
        i = pl.program_id(0)

        @pl.when(i == 0)
        def _():
            l_ref[...] = jnp.zeros_like(l_ref)

        err = h_ref[...] - t_ref[...]
        dy_ref[...] = err * (1.0 / D)
        l_ref[...] += jnp.broadcast_to((0.5 / D) * jnp.sum(err * err), (1, 128))

    row = pl.BlockSpec((tm, D), lambda i: (i, 0))
    return _pc(body, name="loss_head", grid=(T // tm,), in_specs=[row, row],
               out_specs=[row, pl.BlockSpec((1, 128), lambda i: (0, 0))],
               out_shape=[S((T, D), F32), S((1, 128), F32)], compiler_params=_cp(1))(h, target)


def _local_step(x, p3, pos_col, target, weights_of, upper_grads_done, Sm):
    L = p3.shape[0]
    g3 = {n: Sm[n].reshape(L, 1, -1) for n in ("ffn1_pre_g", "ffn1_post_g", "mix_pre_g", "attn_norm_g", "ssm_norm_g",
                                                "mix_post_g", "ffn2_pre_g", "ffn2_post_g", "ple_post_g", "ssm_b_glu", "ssm_d")}
    rot = _rot_tables(pos_col)
    prep_names = ("ssm_lam_re", "ssm_lam_im", "ssm_log_dt", "ssm_b_re", "ssm_b_im", "ssm_c_re", "ssm_c_im")

    saved = []
    h = x
    for l in range(L):
        W = weights_of(l, h)
        sv = {"h0": h, "W": W}
        h, sv["a1"], sv["b1"], sv["f1"], sv["xn1"] = _ffn_fwd(
            h, g3["ffn1_pre_g"], g3["ffn1_post_g"], W["ffn1_w_gate"], W["ffn1_w_up"], W["ffn1_w_down"], l, "1")
        sv["h1"] = h
        P, sv["ain"] = _mix_proj(h, g3["mix_pre_g"], W["w_in"], rot, l)
        sv["P"] = P
        ol = [_attn_fwd(P, d, l) for d in PATTERN_DILATIONS]
        prep, sv["prep_vjp"] = jax.vjp(_ssm_prep, *[Sm[n][l] for n in prep_names])
        a_re, a_im, bre, bim, cre, cim = prep
        sv["prep"] = prep
        sv["u"], bur, bui = _ssm_in(P, bre, bim, l)
        xr, xi = _scan(bur, bui, a_re, a_im, False, l)
        sv["xr"], sv["xi"] = xr, xi
        ssm, sv["y"], sv["z"] = _ssm_out(xr, xi, sv["u"], cre, cim, g3["ssm_d"], W["ssm_w_glu"], g3["ssm_b_glu"], l)
        sv["ssm"] = ssm
        h, sv["attn"], sv["lse"], sv["mixed"], sv["mo"] = _mix_out(
            [o for o, _ in ol], [s for _, s in ol], ssm, h, g3["attn_norm_g"], g3["ssm_norm_g"], g3["mix_post_g"],
            W["w_out"], l)
        sv["h2"] = h
        h, sv["a2"], sv["b2"], sv["f2"], sv["xn2"] = _ffn_fwd(
            h, g3["ffn2_pre_g"], g3["ffn2_post_g"], W["ffn2_w_gate"], W["ffn2_w_up"], W["ffn2_w_down"], l, "2")
        sv["h3"] = h
        h, sv["e"], sv["gte"] = _ple_fwd(h, p3, W["ple_w_up"], W["ple_w_gate"], g3["ple_post_g"], l)
        saved.append(sv)

    dh, loss = _loss_head(h, target)

    G_upper = {n: lax.empty((NSH, L - 1, r, c), BF16) for n, r, c in BIG} if L > 1 else {}
    G_first = {n: lax.empty((NSH, 1, r, c), BF16) for n, r, c in BIG}
    sg = {n: [None] * L for n in SMALL}
    whole, shard, kcol = "whole", "shard", "cols"
    ple_g = g3["ple_post_g"]
    for l in reversed(range(L)):
        sv = saved[l]
        W = sv["W"]
        G, gl = (G_first, 0) if l == 0 else (G_upper, l - 1)
        if l == 0 and L > 1:
            ple_g = ple_g + upper_grads_done(G_upper)
        dh, de, dgte, sg["ple_post_g"][l] = _ple_bwd(dh, sv["e"], sv["gte"], W["ple_w_gate"], ple_g, l)
        G["ple_w_up"] = _dw(p3[l][None], de, G["ple_w_up"], gl, PLE, 256, whole, shard, f"dw_ple_up_l{l}")
        G["ple_w_gate"] = _dw(sv["h3"][None], dgte[None], G["ple_w_gate"], gl, 256, D, kcol, whole, f"dw_ple_gate_l{l}")
        dh, df, da, db, hh, sg["ffn2_pre_g"][l], sg["ffn2_post_g"][l] = _ffn_bwd(
            dh, sv["h2"], sv["f2"], sv["a2"], sv["b2"], g3["ffn2_pre_g"], g3["ffn2_post_g"],
            W["ffn2_w_gate"], W["ffn2_w_up"], W["ffn2_w_down"], l, "2")
        G["ffn2_w_gate"] = _dw(da, sv["xn2"][None], G["ffn2_w_gate"], gl, DFS, D, shard, whole, f"dw_ffn2_gate_l{l}")
        G["ffn2_w_up"] = _dw(db, sv["xn2"][None], G["ffn2_w_up"], gl, DFS, D, shard, whole, f"dw_ffn2_up_l{l}")
        G["ffn2_w_down"] = _dw(hh, df[None], G["ffn2_w_down"], gl, DFS, D, shard, whole, f"dw_ffn2_down_l{l}")
        a_re, a_im, bre, bim, cre, cim = sv["prep"]
        dattn, dssm, delta, dmo, sg["mix_post_g"][l], sg["attn_norm_g"][l], sg["ssm_norm_g"][l] = _mix_out_bwd(
            dh, sv["mo"], sv["attn"], sv["ssm"], g3["attn_norm_g"], g3["ssm_norm_g"], g3["mix_post_g"], W["w_out"], l)
        G["w_out"] = _dw(sv["mixed"][None], dmo[None], G["w_out"], gl, 256, D, kcol, whole, f"dw_out_l{l}")
        gnr, gni, du_direct, dz, yg, sg["ssm_b_glu"][l], dd, dcre, dcim = _ssm_out_bwd(
            dssm, sv["y"], sv["z"], sv["xr"], sv["xi"], sv["u"], cre, cim, g3["ssm_d"], W["ssm_w_glu"], l)
        sg["ssm_d"][l] = dd.reshape(Sm["ssm_d"].shape[1:])
        G["ssm_w_glu"] = _dw(yg[None], dz[None], G["ssm_w_glu"], gl, 128, DSS, kcol, whole, f"dw_glu_l{l}")
        gr, gi = _scan(gnr, gni, a_re, a_im, True, l)
        dar, dai = _ssm_da(gr, gi, sv["xr"], sv["xi"], l)
        du, dbre, dbim = _ssm_in_bwd(gr, gi, sv["u"], bre, bim, du_direct, l)
        for n, g in zip(prep_names, sv["prep_vjp"]((dar, dai, dbre, dbim, dcre, dcim))):
            sg[n][l] = g
        acc = None
        for d in PATTERN_DILATIONS:
            acc = _attn_bwd(sv["P"], dattn, sv["lse"], delta, acc, d, l)
        dh, dP, sg["mix_pre_g"][l] = _mix_proj_bwd(acc[0], acc[1], acc[2], du, dh, sv["h1"], g3["mix_pre_g"],
                                                   W["w_in"], rot, l)
        G["w_in"] = _dw(sv["ain"][None], dP, G["w_in"], gl, D, DA,whole, shard, f"dw_in_l{l}")
        dh, df, da, db, hh, sg["ffn1_pre_g"][l], sg["ffn1_post_g"][l] = _ffn_bwd(
            dh, sv["h0"], sv["f1"], sv["a1"], sv["b1"], g3["ffn1_pre_g"], g3["ffn1_post_g"],
            W["ffn1_w_gate"], W["ffn1_w_up"], W["ffn1_w_down"], l, "1")
        G["ffn1_w_gate"] = _dw(da, sv["xn1"][None], G["ffn1_w_gate"], gl, DFS, D, shard, whole, f"dw_ffn1_gate_l{l}")
        G["ffn1_w_up"] = _dw(db, sv["xn1"][None], G["ffn1_w_up"], gl, DFS, D, shard, whole, f"dw_ffn1_up_l{l}")
        G["ffn1_w_down"] = _dw(hh, df[None], G["ffn1_w_down"], gl, DFS, D, shard, whole, f"dw_ffn1_down_l{l}")

    small = {n: jnp.stack([g.reshape(Sm[n].shape[1:]) for g in sg[n]]) for n in SMALL}
    return loss, dh, G_upper, G_first, small


HBM_SPEC = pl.BlockSpec(memory_space=pltpu.HBM)


def _place():
    x, y, c = lax.axis_index("x"), lax.axis_index("y"), lax.axis_index("c")
    chips = [(1 - x, y), (x, 1 - y), (1 - x, 1 - y)]
    return x, y, c, chips


def _comm_params():
    return pltpu.CompilerParams(vmem_limit_bytes=VMEM_LIMIT)


def _gather_weights(ws, lands):
    n = len(ws)

    def body(*refs):
        ins, outs = refs[:n], refs[2 * n:3 * n]
        s_ici, r_ici, s_d2d, r_d2d = refs[3 * n:]
        x, y, c, chips = _place()

        def half(ref, t, hc):
            r2 = ws[t].shape[1] // 2
            return ref.at[:, pl.ds(hc * r2, r2), :]

        def ici(t, k, src_chip, to):
            j = 2 * src_chip[0] + src_chip[1]
            src = half(ins[t], t, c) if to is not None else half(outs[t].at[j], t, c)
            return pltpu.make_async_remote_copy(src_ref=src, dst_ref=half(outs[t].at[j], t, c),
                                                send_sem=s_ici.at[3 * t + k], recv_sem=r_ici.at[3 * t + k],
                                                device_id=to if to is not None else (x, y, c), device_id_type=MESH)

        def d2d(t, k, hc):
            j = 2 * chips[k][0] + chips[k][1]
            r = half(outs[t].at[j], t, hc)
            return pltpu.make_async_remote_copy(src_ref=r, dst_ref=r, send_sem=s_d2d.at[3 * t + k],
                                                recv_sem=r_d2d.at[3 * t + k], device_id=(x, y, 1 - c),
                                                device_id_type=MESH)

        sends = [ici(t, k, (x, y), (*chips[k], c)) for t in range(n) for k in range(3)]
        for cp in sends:
            cp.start()
        passed = []
        for t in range(n):
            for k in range(3):
                ici(t, k, chips[k], None).wait_recv()
                passed.append(d2d(t, k, c))
                passed[-1].start()
        for t in range(n):
            for k in range(3):
                d2d(t, k, 1 - c).wait_recv()
        for cp in sends + passed:
            cp.wait_send()

    return _pc(body, name="gather_weights", in_specs=[HBM_SPEC] * (2 * n), out_specs=[HBM_SPEC] * n,
               out_shape=[S(z.shape, z.dtype) for z in lands], input_output_aliases={n + t: t for t in range(n)},
               scratch_shapes=[pltpu.SemaphoreType.DMA((3 * n,))] * 4, compiler_params=_comm_params())(*ws, *lands)


SEM_SPEC = pl.BlockSpec(memory_space=pltpu.SEMAPHORE)
ANY_SPEC = pl.BlockSpec(memory_space=pl.ANY)
SPLIT_EFFECT = pltpu.SideEffectType.DATAFLOW_SIDE_EFFECTING


def _in_hbm(t):
    return pltpu.with_memory_space_constraint(t, pltpu.HBM)


def _place_own(ws, me_arr, layer):
    n = len(ws)

    def body(me_ref, *refs):
        for t in range(n):
            refs[n + t][...] = refs[t][...]

    gs = pltpu.PrefetchScalarGridSpec(
        num_scalar_prefetch=1, grid=(2,),
        in_specs=[pl.BlockSpec((w.shape[0], w.shape[1] // 2, w.shape[2]), lambda i, me: (0, i, 0)) for w in ws],
        out_specs=[pl.BlockSpec((None, w.shape[0], w.shape[1] // 2, w.shape[2]), lambda i, me: (me[0], 0, i, 0))
                   for w in ws])
    return _pc(body, name=f"gather_place_own_l{layer}", grid_spec=gs,
               out_shape=[S((NSH,) + w.shape, w.dtype) for w in ws], compiler_params=_cp(1))(me_arr, *ws)


def _gather_start(ws, lands, after, layer):
    n = len(ws)

    def body(*refs):
        ins, lz = refs[:n], refs[n:2 * n]
        s_sem, r_sem = refs[2 * n + 1], refs[2 * n + 2]
        token = refs[-1]
        x, y, c, chips = _place()
        for t in range(n):
            for k in range(3):
                pltpu.make_async_remote_copy(src_ref=ins[t], dst_ref=lz[t].at[2 * x + y], send_sem=s_sem.at[3 * t + k],
                                             recv_sem=r_sem.at[3 * t + k], device_id=(*chips[k], c),
                                             device_id_type=MESH).start()
        token[...] = jnp.zeros_like(token)

    hbm = [pltpu.HBM(w.shape, w.dtype) for w in ws] + [pltpu.HBM(z.shape, z.dtype) for z in lands]
    out = _pc(body, name=f"gather_start_l{layer}",
              out_shape=(pltpu.SemaphoreType.DMA((3 * n,)), pltpu.SemaphoreType.DMA((3 * n,)), *hbm, S((8, 128), F32)),
              in_specs=[HBM_SPEC] * (2 * n) + [ANY_SPEC],
              out_specs=(SEM_SPEC, SEM_SPEC, *([HBM_SPEC] * (2 * n)), pl.BlockSpec(memory_space=pltpu.VMEM)),
              input_output_aliases={i: 2 + i for i in range(2 * n)},
              compiler_params=pltpu.CompilerParams(has_side_effects=SPLIT_EFFECT))(
                  *[_in_hbm(w) for w in ws], *[_in_hbm(z) for z in lands], after)
    return out[0], out[1], out[2:2 + n], out[2 + n:2 + 2 * n], out[-1]


def _gather_wait(s_sem, r_sem, ws, lands, after, layer):
    n = len(ws)

    def body(*refs):
        ins, lz = refs[:n], refs[n:2 * n]
        s_ref, r_ref = refs[2 * n], refs[2 * n + 1]
        x, y, c, chips = _place()
        for t in range(n):
            for k in range(3):
                cp = pltpu.make_async_remote_copy(src_ref=ins[t], dst_ref=lz[t].at[2 * x + y], send_sem=s_ref.at[3 * t + k],
                                                  recv_sem=r_ref.at[3 * t + k], device_id=(*chips[k], c),
                                                  device_id_type=MESH)
                cp.wait_send()
                cp.wait_recv()

    hbm = [pltpu.HBM(w.shape, w.dtype) for w in ws] + [pltpu.HBM(z.shape, z.dtype) for z in lands]
    out = _pc(body, name=f"gather_wait_l{layer}", out_shape=tuple(hbm),
              in_specs=[HBM_SPEC] * (2 * n) + [SEM_SPEC, SEM_SPEC, ANY_SPEC], out_specs=tuple([HBM_SPEC] * (2 * n)),
              input_output_aliases={i: i for i in range(2 * n)},
              compiler_params=pltpu.CompilerParams(has_side_effects=SPLIT_EFFECT))(*ws, *lands, s_sem, r_sem, after)
    return out[n:]


def _swap_halves(gs, tag):
    n = len(gs)

    def body(*refs):
        ins, outs = refs[:n], refs[n:2 * n]
        s_sem, r_sem = refs[2 * n:]
        x, y, c, _ = _place()
        cps = []
        for t in range(n):
            r2 = gs[t].shape[2] // 2
            cps.append(pltpu.make_async_remote_copy(
                src_ref=ins[t].at[:, :, pl.ds((1 - c) * r2, r2), :], dst_ref=outs[t], send_sem=s_sem.at[t],
                recv_sem=r_sem.at[t], device_id=(x, y, 1 - c), device_id_type=MESH))
            cps[-1].start()
        for cp in cps:
            cp.wait_recv()
        for cp in cps:
            cp.wait_send()

    return _pc(body, name=f"grad_swap_halves_{tag}", in_specs=[HBM_SPEC] * n, out_specs=[HBM_SPEC] * n,
               out_shape=[S(g.shape[:2] + (g.shape[2] // 2, g.shape[3]), g.dtype) for g in gs],
               scratch_shapes=[pltpu.SemaphoreType.DMA((n,))] * 2, compiler_params=_comm_params())(*gs)


def _add_half(g, landed, c_arr, name):
    _, L, r2, cols = landed.shape

    def body(c_ref, g_ref, l_ref, o_ref):
        o_ref[...] = (g_ref[...].astype(F32) + l_ref[...].astype(F32)).astype(BF16)

    gs = pltpu.PrefetchScalarGridSpec(
        num_scalar_prefetch=1, grid=(NSH, L),
        in_specs=[pl.BlockSpec((None, None, r2, cols), lambda j, l, c: (j, l, c[0], 0)),
                  pl.BlockSpec((None, None, r2, cols), lambda j, l, c: (j, l, 0, 0))],
        out_specs=pl.BlockSpec((None, None, r2, cols), lambda j, l, c: (j, l, 0, 0)))
    return _pc(body, name=name, grid_spec=gs, out_shape=S(landed.shape, BF16), compiler_params=_cp(2))(c_arr, g, landed)


def _send_shards(ps):
    n = len(ps)

    def body(*refs):
        ins, outs = refs[:n], refs[n:2 * n]
        s_sem, r_sem = refs[2 * n:]
        x, y, c, chips = _place()
        cps = []
        for t in range(n):
            for k in range(3):
                cps.append(pltpu.make_async_remote_copy(
                    src_ref=ins[t].at[2 * chips[k][0] + chips[k][1]], dst_ref=outs[t].at[k],
                    send_sem=s_sem.at[3 * t + k], recv_sem=r_sem.at[3 * t + k], device_id=(*chips[k], c),
                    device_id_type=MESH))
                cps[-1].start()
        for cp in cps:
            cp.wait_recv()
        for cp in cps:
            cp.wait_send()

    return _pc(body, name="grad_send_shards", in_specs=[HBM_SPEC] * n, out_specs=[HBM_SPEC] * n,
               out_shape=[S((3,) + p.shape[1:], p.dtype) for p in ps],
               scratch_shapes=[pltpu.SemaphoreType.DMA((3 * n,))] * 2, compiler_params=_comm_params())(*ps)


def _send_start(ps, lands):
    n = len(ps)

    def body(*refs):
        ins, lz = refs[:n], refs[n:2 * n]
        s_sem, r_sem = refs[2 * n], refs[2 * n + 1]
        token = refs[-1]
        x, y, c, chips = _place()
        for t in range(n):
            for k in range(3):
                pltpu.make_async_remote_copy(src_ref=ins[t].at[2 * chips[k][0] + chips[k][1]], dst_ref=lz[t].at[k],
                                             send_sem=s_sem.at[3 * t + k], recv_sem=r_sem.at[3 * t + k],
                                             device_id=(*chips[k], c), device_id_type=MESH).start()
        token[...] = jnp.zeros_like(token)

    hbm = [pltpu.HBM(p.shape, p.dtype) for p in ps] + [pltpu.HBM(z.shape, z.dtype) for z in lands]
    out = _pc(body, name="grad_send_start",
              out_shape=(pltpu.SemaphoreType.DMA((3 * n,)), pltpu.SemaphoreType.DMA((3 * n,)), *hbm, S((8, 128), F32)),
              in_specs=[HBM_SPEC] * (2 * n),
              out_specs=(SEM_SPEC, SEM_SPEC, *([HBM_SPEC] * (2 * n)), pl.BlockSpec(memory_space=pltpu.VMEM)),
              input_output_aliases={i: 2 + i for i in range(2 * n)},
              compiler_params=pltpu.CompilerParams(has_side_effects=SPLIT_EFFECT))(
                  *[_in_hbm(p) for p in ps], *[_in_hbm(z) for z in lands])
    return out[0], out[1], out[2:2 + n], out[2 + n:2 + 2 * n], out[-1]


def _send_wait(s_sem, r_sem, ps, lands, after):
    n = len(ps)

    def body(*refs):
        ins, lz = refs[:n], refs[n:2 * n]
        s_ref, r_ref = refs[2 * n], refs[2 * n + 1]
        x, y, c, chips = _place()
        for t in range(n):
            for k in range(3):
                cp = pltpu.make_async_remote_copy(src_ref=ins[t].at[2 * chips[k][0] + chips[k][1]], dst_ref=lz[t].at[k],
                                                  send_sem=s_ref.at[3 * t + k], recv_sem=r_ref.at[3 * t + k],
                                                  device_id=(*chips[k], c), device_id_type=MESH)
                cp.wait_send()
                cp.wait_recv()

    hbm = [pltpu.HBM(p.shape, p.dtype) for p in ps] + [pltpu.HBM(z.shape, z.dtype) for z in lands]
    out = _pc(body, name="grad_send_wait", out_shape=tuple(hbm),
              in_specs=[HBM_SPEC] * (2 * n) + [SEM_SPEC, SEM_SPEC, ANY_SPEC], out_specs=tuple([HBM_SPEC] * (2 * n)),
              input_output_aliases={i: i for i in range(2 * n)},
              compiler_params=pltpu.CompilerParams(has_side_effects=SPLIT_EFFECT))(*ps, *lands, s_sem, r_sem, after)
    return out[:n], out[n:]


def _sum_shards(part, landed, me_arr, c_arr, buf, first_layer, name):
    _, nl, r2, cols = landed.shape

    def body(me_ref, c_ref, p_ref, l_ref, b_ref, o_ref):
        o_ref[...] = ((p_ref[...].astype(F32) + l_ref[0].astype(F32)) + l_ref[1].astype(F32)) + l_ref[2].astype(F32)

    gs = pltpu.PrefetchScalarGridSpec(
        num_scalar_prefetch=2, grid=(nl,),
        in_specs=[pl.BlockSpec((None, None, r2, cols), lambda l, me, c: (me[0], l, 0, 0)),
                  pl.BlockSpec((3, None, r2, cols), lambda l, me, c: (0, l, 0, 0)), ANY_SPEC],
        out_specs=pl.BlockSpec((None, r2, cols), lambda l, me, c: (first_layer + l, c[0], 0)))
    return _pc(body, name=name, grid_spec=gs, out_shape=S(buf.shape, F32), input_output_aliases={4: 0},
               compiler_params=_cp(1))(me_arr, c_arr, part, landed, buf)


def _share_halves(bufs):
    n = len(bufs)

    def body(*refs):
        ins, outs = refs[:n], refs[n:2 * n]
        s_sem, r_sem = refs[2 * n:]
        x, y, c, _ = _place()
        cps = []
        for t in range(n):
            r2 = bufs[t].shape[1] // 2
            cps.append(pltpu.make_async_remote_copy(
                src_ref=ins[t].at[:, pl.ds(c * r2, r2), :], dst_ref=outs[t].at[:, pl.ds(c * r2, r2), :],
                send_sem=s_sem.at[t], recv_sem=r_sem.at[t], device_id=(x, y, 1 - c), device_id_type=MESH))
            cps[-1].start()
        for cp in cps:
            cp.wait_recv()
        for cp in cps:
            cp.wait_send()

    return _pc(body, name="grad_share_halves", in_specs=[HBM_SPEC] * n, out_specs=[HBM_SPEC] * n,
               out_shape=[S(b.shape, b.dtype) for b in bufs], input_output_aliases={t: t for t in range(n)},
               scratch_shapes=[pltpu.SemaphoreType.DMA((n,))] * 2, compiler_params=_comm_params())(*bufs)


def _gather_small(v):
    nr = v.shape[0]

    def body(v_ref, out_ref, send_sems, recv_sems, local_sem):
        x, y, c, chips = _place()
        me, sibling = (x, y, c), (x, y, 1 - c)

        def rows(px, py, pc):
            return out_ref.at[pl.ds((4 * px + 2 * py + pc) * nr, nr), :]

        def copy(k, block, to, src=None):
            return pltpu.make_async_remote_copy(src_ref=rows(*block) if src is None else src, dst_ref=rows(*block),
                                                send_sem=send_sems.at[k], recv_sem=recv_sems.at[k], device_id=to,
                                                device_id_type=MESH)

        mine = pltpu.make_async_copy(v_ref, rows(*me), local_sem)
        mine.start()
        first = [copy(0, me, sibling, src=v_ref)]
        first += [copy(1 + j, me, (*chip, c), src=v_ref) for j, chip in enumerate(chips)]
        for cp in first:
            cp.start()
        passed = [copy(4 + j, (*chip, c), sibling) for j, chip in enumerate(chips)]
        for j, chip in enumerate(chips):
            copy(1 + j, (*chip, c), me).wait_recv()
            passed[j].start()
        copy(0, sibling, me).wait_recv()
        for j, chip in enumerate(chips):
            copy(4 + j, (*chip, 1 - c), me).wait_recv()
        for cp in first + passed:
            cp.wait_send()
        mine.wait()

    vm = pl.BlockSpec(memory_space=pltpu.VMEM)
    return _pc(body, name="gather_small_grads", in_specs=[vm], out_specs=vm, out_shape=S((8 * nr, 128), F32),
               scratch_shapes=[pltpu.SemaphoreType.DMA((7,)), pltpu.SemaphoreType.DMA((7,)), pltpu.SemaphoreType.DMA],
               compiler_params=_comm_params())(v)


def _adamw_math(w, g, m, v):
    m2 = ADAM_B1 * m + (1.0 - ADAM_B1) * g
    v2 = ADAM_B2 * v + (1.0 - ADAM_B2) * (g * g)
    m_hat = m2 / (1.0 - ADAM_B1 ** ADAM_STEP)
    v_hat = v2 / (1.0 - ADAM_B2 ** ADAM_STEP)
    return -ADAM_LR * (m_hat / (jnp.sqrt(v_hat) + ADAM_EPS) + ADAM_WD * w), m2, v2


def _adamw(w, g, m, v, name):
    L, R, C = w.shape
    rb = R // 2 if R >= 512 else R

    def body(w_ref, g_ref, m_ref, v_ref, d_ref, m2_ref, v2_ref):
        d_ref[...], m2_ref[...], v2_ref[...] = _adamw_math(w_ref[...], g_ref[...], m_ref[...], v_ref[...])

    blk = pl.BlockSpec((None, rb, C), lambda l, r: (l, r, 0))
    return _pc(body, name=name, grid=(L, R // rb), in_specs=[blk] * 4, out_specs=[blk] * 3,
               out_shape=[S(w.shape, F32)] * 3, compiler_params=_cp(2))(w, g, m, v)


def _adamw_small(gathered, w, m, v):
    nr = w.shape[0]
    rb = nr // 5

    def body(a_ref, w_ref, m_ref, v_ref, g_ref, d_ref, m2_ref, v2_ref):
        g = a_ref[0]
        for k in range(1, 8):
            g = g + a_ref[k]
        g_ref[...] = g
        d_ref[...], m2_ref[...], v2_ref[...] = _adamw_math(w_ref[...], g, m_ref[...], v_ref[...])

    blk = pl.BlockSpec((rb, 128), lambda i: (i, 0))
    return _pc(body, name="adamw_small", grid=(nr // rb,), in_specs=[pl.BlockSpec((8, rb, 128), lambda i: (0, i, 0))] + [blk] * 3,
               out_specs=[blk] * 4, out_shape=[S((nr, 128), F32)] * 4, compiler_params=_cp(1))(gathered, w, m, v)


SMALL_ROWS = 4520


def _pack(arrs):
    flat = jnp.concatenate([a.reshape(-1) for a in arrs])
    return jnp.pad(flat, (0, SMALL_ROWS * 128 - flat.shape[0])).reshape(SMALL_ROWS, 128)


def _unpack(packed, like):
    flat = packed.reshape(-1)
    out, off = [], 0
    for a in like:
        out.append(flat[off:off + a.size].reshape(a.shape))
        off += a.size
    return out


def kernel(x, p, positions, ffn1_pre_g, ffn1_w_gate, ffn1_w_up, ffn1_w_down, ffn1_post_g, mix_pre_g, w_in, attn_norm_g, ssm_lam_re, ssm_lam_im, ssm_log_dt, ssm_b_re, ssm_b_im, ssm_c_re, ssm_c_im, ssm_d, ssm_w_glu, ssm_b_glu, ssm_norm_g, w_out, mix_post_g, ffn2_pre_g, ffn2_w_gate, ffn2_w_up, ffn2_w_down, ffn2_post_g, ple_w_up, ple_w_gate, ple_post_g, loss_target, m_ffn1_pre_g, m_ffn1_w_gate, m_ffn1_w_up, m_ffn1_w_down, m_ffn1_post_g, m_mix_pre_g, m_w_in, m_attn_norm_g, m_ssm_lam_re, m_ssm_lam_im, m_ssm_log_dt, m_ssm_b_re, m_ssm_b_im, m_ssm_c_re, m_ssm_c_im, m_ssm_d, m_ssm_w_glu, m_ssm_b_glu, m_ssm_norm_g, m_w_out, m_mix_post_g, m_ffn2_pre_g, m_ffn2_w_gate, m_ffn2_w_up, m_ffn2_w_down, m_ffn2_post_g, m_ple_w_up, m_ple_w_gate, m_ple_post_g, v_ffn1_pre_g, v_ffn1_w_gate, v_ffn1_w_up, v_ffn1_w_down, v_ffn1_post_g, v_mix_pre_g, v_w_in, v_attn_norm_g, v_ssm_lam_re, v_ssm_lam_im, v_ssm_log_dt, v_ssm_b_re, v_ssm_b_im, v_ssm_c_re, v_ssm_c_im, v_ssm_d, v_ssm_w_glu, v_ssm_b_glu, v_ssm_norm_g, v_w_out, v_mix_post_g, v_ffn2_pre_g, v_ffn2_w_gate, v_ffn2_w_up, v_ffn2_w_down, v_ffn2_post_g, v_ple_w_up, v_ple_w_gate, v_ple_post_g):
    a = dict(locals())
    T = x.shape[1]
    big_names = [n for n, _, _ in BIG]
    for n in TRANSPOSED:
        for pre in ("", "m_", "v_"):
            a[pre + n] = jnp.swapaxes(a[pre + n], 1, 2)

    own = [a[n].astype(BF16) for n in big_names]
    n_layers = own[0].shape[0]
    per_layer = [[w[l:l + 1] for w in own] for l in range(n_layers)]
    c_arr = lax.axis_index("c").astype(jnp.int32).reshape(1)
    me_arr = (2 * lax.axis_index("x") + lax.axis_index("y")).astype(jnp.int32).reshape(1)
    first = dict(zip(big_names, _gather_weights(per_layer[0], _place_own(per_layer[0], me_arr, 0))))
    pending, anchor, queued_behind = {}, jnp.zeros((), F32), first[big_names[0]]
    for l in range(1, n_layers):
        s_sem, r_sem, ws_thru, lands_thru, token = _gather_start(per_layer[l], _place_own(per_layer[l], me_arr, l),
                                                                 queued_behind, l)
        pending[l] = (s_sem, r_sem, ws_thru, lands_thru)
        anchor = anchor + token[0, 0]
        queued_behind = token

    def weights_of(l, after):
        if l == 0:
            return first
        return dict(zip(big_names, _gather_wait(*pending[l], after, l)))

    Sm = {n: a[n] for n in SMALL}
    Sm["ffn1_pre_g"] = Sm["ffn1_pre_g"] + anchor

    pos = jnp.broadcast_to(positions.reshape(1, T, 1).astype(F32), (1, T, 128))
    def chip_partials(G, tag):
        gs = [G[n] for n in big_names]
        landed = _swap_halves(gs, tag)
        return [_add_half(g, la, c_arr, f"grad_add_half_{tag}_{n}") for g, la, n in zip(gs, landed, big_names)]

    upper = {}

    def upper_grads_done(G_upper):
        parts = chip_partials(G_upper, "upper")
        lands = [lax.empty((3,) + pt.shape[1:], BF16) for pt in parts]
        s_sem, r_sem, parts_thru, lands_thru, token = _send_start(parts, lands)
        upper["pending"] = (s_sem, r_sem, parts_thru, lands_thru)
        return token[0, 0]

    loss, gx, G_upper, G_first, small = _local_step(
        _reorder(x, True, "to_streams_x")[0], _reorder(p[:, 0], True, "to_streams_p"),
        _reorder(pos, True, "to_streams_pos")[0, :, :1], _reorder(loss_target, True, "to_streams_target")[0],
        weights_of, upper_grads_done, Sm)
    gx = _reorder(gx[None], False, "to_time_grad_x")

    bufs = [lax.empty((n_layers, r, c), F32) for _, r, c in BIG]
    if n_layers > 1:
        parts, landed = _send_wait(*upper["pending"], gx)
        bufs = [_sum_shards(pt, la, me_arr, c_arr, b, 1, f"grad_sum_shards_upper_{n}")
                for pt, la, b, n in zip(parts, landed, bufs, big_names)]
    parts = chip_partials(G_first, "first")
    landed = _send_shards(parts)
    bufs = [_sum_shards(pt, la, me_arr, c_arr, b, 0, f"grad_sum_shards_first_{n}")
            for pt, la, b, n in zip(parts, landed, bufs, big_names)]
    grads = dict(zip(big_names, _share_halves(bufs)))

    small_g = _gather_small(_pack([small[n] for n in SMALL])).reshape(8, SMALL_ROWS, 128)
    sg, sd, sm, sv = _adamw_small(small_g, _pack([a[n] for n in SMALL]), _pack([a["m_" + n] for n in SMALL]),
                                  _pack([a["v_" + n] for n in SMALL]))
    like = [a[n] for n in SMALL]
    res = {}
    for n, g_, d_, m_, v_ in zip(SMALL, _unpack(sg, like), _unpack(sd, like), _unpack(sm, like), _unpack(sv, like)):
        res[n] = (g_, d_, m_, v_)
    for n in big_names:
        d_, m_, v_ = _adamw(a[n], grads[n], a["m_" + n], a["v_" + n], f"adamw_{n}")
        res[n] = (grads[n], d_, m_, v_)
        if n in TRANSPOSED:
            res[n] = tuple(jnp.swapaxes(t, 1, 2) for t in res[n])

    total = lax.psum(loss[0, 0], ("x", "y", "c"))
    return (total, gx, *[res[n][0] for n in WEIGHTS], *[res[n][1] for n in WEIGHTS],
            *[res[n][2] for n in WEIGHTS], *[res[n][3] for n in WEIGHTS])
```

```python
import functools
import math

import numpy as np
import jax
import jax.numpy as jnp
from jax import lax
from jax.experimental import pallas as pl
from jax.experimental.pallas import tpu as pltpu

F32 = jnp.float32
BF16 = jnp.bfloat16
S = jax.ShapeDtypeStruct
MESH = pl.DeviceIdType.MESH

D = 1024
DA = 512
DSS = 512
HD = 64
NH = 8
BAND = 128
NSH = 4
DFS = 704
PLE = 256
EPS = 1e-6
ROPE_THETA = 500000.0
PATTERN_DILATIONS = (1, 4, 16)
NLB = 16
ADAM_LR, ADAM_B1, ADAM_B2, ADAM_EPS, ADAM_WD, ADAM_STEP = 0.001, 0.9, 0.999, 1e-08, 0.01, 10

VMEM_LIMIT = 56 * 1024 * 1024
TM = 512
TMB = 256

BIG = (
    ("ffn1_w_gate", DFS, D), ("ffn1_w_up", DFS, D), ("ffn1_w_down", DFS, D),
    ("w_in", D, 512), ("ssm_w_glu", 128, 512), ("w_out", 256, D),
    ("ffn2_w_gate", DFS, D), ("ffn2_w_up", DFS, D), ("ffn2_w_down", DFS, D),
    ("ple_w_up", PLE, 256), ("ple_w_gate", 256, D),
)
TRANSPOSED = ("ffn1_w_gate", "ffn1_w_up", "ffn2_w_gate", "ffn2_w_up")
SMALL = ("ffn1_pre_g", "ffn1_post_g", "mix_pre_g", "attn_norm_g", "ssm_lam_re", "ssm_lam_im", "ssm_log_dt",
         "ssm_b_re", "ssm_b_im", "ssm_c_re", "ssm_c_im", "ssm_d", "ssm_b_glu", "ssm_norm_g", "mix_post_g",
         "ffn2_pre_g", "ffn2_post_g", "ple_post_g")
WEIGHTS = ("ffn1_pre_g", "ffn1_w_gate", "ffn1_w_up", "ffn1_w_down", "ffn1_post_g", "mix_pre_g", "w_in", "attn_norm_g",
           "ssm_lam_re", "ssm_lam_im", "ssm_log_dt", "ssm_b_re", "ssm_b_im", "ssm_c_re", "ssm_c_im", "ssm_d",
           "ssm_w_glu", "ssm_b_glu", "ssm_norm_g", "w_out", "mix_post_g", "ffn2_pre_g", "ffn2_w_gate", "ffn2_w_up",
           "ffn2_w_down", "ffn2_post_g", "ple_w_up", "ple_w_gate", "ple_post_g")


def _pc(body, **kw):
    return pl.pallas_call(body, **kw)


def _cp(n_grid):
    return pltpu.CompilerParams(dimension_semantics=("arbitrary",) * n_grid, vmem_limit_bytes=VMEM_LIMIT)


def _dot(a, b):
    return jnp.dot(a, b, preferred_element_type=F32)


def _dot_nt(a, b):
    return lax.dot_general(a, b, (((1,), (1,)), ((), ())), preferred_element_type=F32)


def _dot_tn(a, b):
    return lax.dot_general(a, b, (((0,), (0,)), ((), ())), preferred_element_type=F32)


def _split(a):
    hi = a.astype(BF16)
    return hi, (a - hi.astype(F32)).astype(BF16)


def _dot3(fn, a, b):
    ah, al = _split(a)
    bh, bl = _split(b)
    return fn(ah, bh) + fn(ah, bl) + fn(al, bh)


def _rms_fwd(x, g):
    r = lax.rsqrt(jnp.mean(x * x, axis=-1, keepdims=True) + EPS)
    return x * r * g


def _rms_bwd(dy, x, g):
    r = lax.rsqrt(jnp.mean(x * x, axis=-1, keepdims=True) + EPS)
    xr = x * r
    gd = dy * g
    dx = r * (gd - xr * jnp.mean(gd * xr, axis=-1, keepdims=True))
    dg = jnp.sum(dy * xr, axis=0, keepdims=True)
    return dx, dg


def _gelu(y):
    k = math.sqrt(2.0 / math.pi)
    return 0.5 * y * (1.0 + jnp.tanh(k * (y + 0.044715 * y * y * y)))


def _gelu_grad(y):
    k = math.sqrt(2.0 / math.pi)
    t = jnp.tanh(k * (y + 0.044715 * y * y * y))
    return 0.5 * (1.0 + t) + 0.5 * y * (1.0 - t * t) * k * (1.0 + 3 * 0.044715 * y * y)


def _gain_spec(n, layer):
    return pl.BlockSpec((None, 1, n), lambda *_: (layer, 0, 0))


def _row_acc_spec(n):
    return pl.BlockSpec((1, n), lambda *_: (0, 0))


def _rot_tables(pos_col):
    T = pos_col.shape[0]
    half = HD // 8
    inv = (ROPE_THETA ** (-np.arange(half, dtype=np.float32) * (2.0 / (2 * half)))).astype(np.float32)
    lane_freq = np.tile(np.concatenate([inv, inv, np.zeros(HD - 2 * half, np.float32)]), NH)[None, :]

    def body(p_ref, f_ref, c_ref, s1_ref, s2_ref):
        ang = p_ref[...] * f_ref[...]
        d = lax.broadcasted_iota(jnp.int32, ang.shape, 1) % HD
        cs = jnp.cos(ang)
        sn = jnp.sin(ang)
        c_ref[...] = jnp.where(d < 2 * half, cs, 1.0)
        s1_ref[...] = jnp.where(d < half, -sn, 0.0)
        s2_ref[...] = jnp.where((d >= half) & (d < 2 * half), sn, 0.0)

    tm = TM
    return _pc(body, name="rot_tables", grid=(T // tm,),
               in_specs=[pl.BlockSpec((tm, 1), lambda i: (i, 0)), pl.BlockSpec((1, DA), lambda i: (0, 0))],
               out_specs=[pl.BlockSpec((tm, DA), lambda i: (i, 0))] * 3,
               out_shape=[S((T, DA), F32)] * 3, compiler_params=_cp(1))(pos_col, jnp.asarray(lane_freq))


def _rot_fwd(t, c, s1, s2):
    return t * c + pltpu.roll(t, DA - 8, 1) * s1 + pltpu.roll(t, 8, 1) * s2


def _rot_bwd(g, c, s1, s2):
    return g * c + pltpu.roll(g * s1, 8, 1) + pltpu.roll(g * s2, DA - 8, 1)


def _ffn_weight_spec():
    return pl.BlockSpec((NSH, None, DFS, D), lambda i: (0, 0, 0, 0), pipeline_mode=pl.Buffered(1))


def _ffn_fwd(h, pre_g, post_g, wg, wu, wd, layer, tag):
    T = h.shape[0]
    tm = TM
    nt = T // tm

    def body(h_ref, pg_ref, qg_ref, wg_ref, wu_ref, wd_ref, ho_ref, a_ref, b_ref, f_ref, xn_ref):
        hv = h_ref[...]
        xb = _rms_fwd(hv, pg_ref[...]).astype(BF16)
        xn_ref[...] = xb
        f = None
        for j in range(NSH):
            ab = _dot_nt(xb, wg_ref[j]).astype(BF16)
            bb = _dot_nt(xb, wu_ref[j]).astype(BF16)
            a_ref[j] = ab
            b_ref[j] = bb
            a = ab.astype(F32)
            hh = (a * jax.nn.sigmoid(a) * bb.astype(F32)).astype(BF16)
            part = _dot(hh, wd_ref[j])
            f = part if f is None else f + part
        f_ref[...] = f
        ho_ref[...] = hv + 0.5 * _rms_fwd(f, qg_ref[...])

    row = pl.BlockSpec((tm, D), lambda i: (i, 0))
    act = pl.BlockSpec((NSH, tm, DFS), lambda i: (0, i, 0))
    return _pc(body, name=f"ffn_fwd_{tag}_l{layer}", grid=(nt,),
               in_specs=[row, _gain_spec(D, layer), _gain_spec(D, layer)] + [_ffn_weight_spec()] * 3,
               out_specs=[row, act, act, row, row],
               out_shape=[S((T, D), F32), S((NSH, T, DFS), BF16), S((NSH, T, DFS), BF16), S((T, D), F32), S((T, D), BF16)],
               compiler_params=_cp(1))(h, pre_g, post_g, wg, wu, wd)


def _ffn_bwd(dout, h, f, a, b, pre_g, post_g, wg, wu, wd, layer, tag):
    T = h.shape[0]
    tm = TMB
    nt = T // tm

    def body(do_ref, h_ref, f_ref, a_ref, b_ref, pg_ref, qg_ref, wg_ref, wu_ref, wd_ref,
             dh_ref, df_ref, da_ref, db_ref, hh_ref, dpg_ref, dqg_ref):
        @pl.when(pl.program_id(0) == 0)
        def _():
            dpg_ref[...] = jnp.zeros_like(dpg_ref)
            dqg_ref[...] = jnp.zeros_like(dqg_ref)

        do = do_ref[...]
        df, dq = _rms_bwd(0.5 * do, f_ref[...], qg_ref[...])
        dqg_ref[...] += dq
        dfb = df.astype(BF16)
        df_ref[...] = dfb
        dxn = None
        for j in range(NSH):
            dhh = _dot_nt(dfb, wd_ref[j])
            av = a_ref[j].astype(F32)
            bv = b_ref[j].astype(F32)
            sg = jax.nn.sigmoid(av)
            sa = av * sg
            hh_ref[j] = (sa * bv).astype(BF16)
            dab = (dhh * bv * (sg + sa * (1.0 - sg))).astype(BF16)
            dbb = (dhh * sa).astype(BF16)
            da_ref[j] = dab
            db_ref[j] = dbb
            part = _dot(dab, wg_ref[j]) + _dot(dbb, wu_ref[j])
            dxn = part if dxn is None else dxn + part
        dx, dp = _rms_bwd(dxn, h_ref[...], pg_ref[...])
        dpg_ref[...] += dp
        dh_ref[...] = do + dx

    row = pl.BlockSpec((tm, D), lambda i: (i, 0))
    act = pl.BlockSpec((NSH, tm, DFS), lambda i: (0, i, 0))
    return _pc(body, name=f"ffn_bwd_{tag}_l{layer}", grid=(nt,),
               in_specs=[row, row, row, act, act, _gain_spec(D, layer), _gain_spec(D, layer)] + [_ffn_weight_spec()] * 3,
               out_specs=[row, row, act, act, act, _row_acc_spec(D), _row_acc_spec(D)],
               out_shape=[S((T, D), F32), S((T, D), BF16), S((NSH, T, DFS), BF16), S((NSH, T, DFS), BF16),
                          S((NSH, T, DFS), BF16), S((1, D), F32), S((1, D), F32)],
               compiler_params=_cp(1))(dout, h, f, a, b, pre_g, post_g, wg, wu, wd)


def _dw(A, B, buf, layer, kb, nb, a_mode, b_mode, name):
    T = A.shape[1]
    tt = TM
    nt = T // tt

    def pick(v, mode, j, w):
        if mode == "shard":
            return v[j]
        return v[0] if mode == "whole" else v[0][:, j * w:(j + 1) * w]

    def body(a_ref, b_ref, buf_ref, o_ref, acc):
        t = pl.program_id(0)

        @pl.when(t == 0)
        def _():
            acc[...] = jnp.zeros_like(acc)

        av = a_ref[...].astype(BF16)
        bv = b_ref[...].astype(BF16)
        for j in range(NSH):
            acc[j] += _dot_tn(pick(av, a_mode, j, kb), pick(bv, b_mode, j, nb))

        @pl.when(t == nt - 1)
        def _():
            o_ref[...] = acc[...].astype(o_ref.dtype)

    return _pc(body, name=name, grid=(nt,),
               in_specs=[pl.BlockSpec((A.shape[0], tt, A.shape[2]), lambda t: (0, t, 0)),
                         pl.BlockSpec((B.shape[0], tt, B.shape[2]), lambda t: (0, t, 0)),
                         pl.BlockSpec(memory_space=pl.ANY)],
               out_specs=pl.BlockSpec((NSH, None, kb, nb), lambda t: (0, layer, 0, 0)),
               out_shape=S(buf.shape, buf.dtype), input_output_aliases={2: 0},
               scratch_shapes=[pltpu.VMEM((NSH, kb, nb), F32)], compiler_params=_cp(1))(A, B, buf)


def _mix_proj(h, pre_g, win, rot, layer):
    T = h.shape[0]
    tm = TM

    def body(h_ref, g_ref, w_ref, c_ref, s1_ref, s2_ref, p_ref, xn_ref):
        xb = _rms_fwd(h_ref[...], g_ref[...]).astype(BF16)
        xn_ref[...] = xb
        for j in range(NSH):
            o = _dot(xb, w_ref[j])
            p_ref[j] = _rot_fwd(o, c_ref[...], s1_ref[...], s2_ref[...]) if j < 2 else o

    row = pl.BlockSpec((tm, D), lambda i: (i, 0))
    half = pl.BlockSpec((tm, DA), lambda i: (i, 0))
    return _pc(body, name=f"mix_proj_l{layer}", grid=(T // tm,),
               in_specs=[row, _gain_spec(D, layer), pl.BlockSpec((NSH, None, D, DA), lambda i: (0, 0, 0, 0)),
                         half, half, half],
               out_specs=[pl.BlockSpec((NSH, tm, DA), lambda i: (0, i, 0)), row],
               out_shape=[S((NSH, T, DA), F32), S((T, D), BF16)], compiler_params=_cp(1))(h, pre_g, win, *rot)


def _mix_proj_bwd(dq, dk, dv, du, dh_up, h, pre_g, win, rot, layer):
    T = h.shape[0]
    tm = TM

    def body(dq_ref, dk_ref, dv_ref, du_ref, up_ref, h_ref, g_ref, w_ref, c_ref, s1_ref, s2_ref,
             dh_ref, dp_ref, dg_ref):
        @pl.when(pl.program_id(0) == 0)
        def _():
            dg_ref[...] = jnp.zeros_like(dg_ref)

        rot = (c_ref[...], s1_ref[...], s2_ref[...])
        dps = [_rot_bwd(dq_ref[...], *rot), _rot_bwd(dk_ref[...], *rot), dv_ref[...], du_ref[...]]
        dxn = None
        for j in range(NSH):
            dpb = dps[j].astype(BF16)
            dp_ref[j] = dpb
            part = _dot_nt(dpb, w_ref[j])
            dxn = part if dxn is None else dxn + part
        dx, dg = _rms_bwd(dxn, h_ref[...], g_ref[...])
        dg_ref[...] += dg
        dh_ref[...] = up_ref[...] + dx

    row = pl.BlockSpec((tm, D), lambda i: (i, 0))
    half = pl.BlockSpec((tm, DA), lambda i: (i, 0))
    return _pc(body, name=f"mix_proj_bwd_l{layer}", grid=(T // tm,),
               in_specs=[half, half, half, half, row, row, _gain_spec(D, layer),
                         pl.BlockSpec((NSH, None, D, DA), lambda i: (0, 0, 0, 0)), half, half, half],
               out_specs=[row, pl.BlockSpec((NSH, tm, DA), lambda i: (0, i, 0)), _row_acc_spec(D)],
               out_shape=[S((T, D), F32), S((NSH, T, DA), BF16), S((1, D), F32)],
               compiler_params=_cp(1))(dq, dk, dv, du, dh_up, h, pre_g, win, *rot)


def _stream_pos(d, axis):
    i = lax.broadcasted_iota(jnp.int32, (BAND, BAND), axis)
    if d == 16:
        return i
    if d == 4:
        return 4 * (i % 32) + i // 32
    return 16 * (i % 8) + i // 8


def _band_masks(b, d):
    qi, kj = _stream_pos(d, 0), _stream_pos(d, 1)
    return kj <= qi, (kj >= qi) & (b > 0)


def _pattern(d, T):
    n16 = T // 16
    if d == 16:
        return (16, n16, DA), (None, BAND, DA), lambda r, k: (r, k, 0)
    if d == 4:
        return (4, 4, n16, DA), (4, None, 32, DA), lambda r, k: (0, r, k, 0)
    return (16, n16, DA), (16, 8, DA), lambda r, k: (0, k, 0)


def _pattern_spec(d, T, kmap, lead=None):
    _, blk, idx = _pattern(d, T)
    if lead is None:
        return pl.BlockSpec(blk, lambda r, b: idx(r, kmap(b)))
    return pl.BlockSpec((None,) + blk, lambda r, b: (lead,) + idx(r, kmap(b)))


def _whole_stream_specs(T, n_plain):
    n16 = T // 16
    p_spec = lambda s: pl.BlockSpec((None, None, n16, DA), lambda r: (s, r, 0, 0))
    plain = pl.BlockSpec((None, n16, DA), lambda r: (r, 0, 0))
    return [p_spec(0), p_spec(1), p_spec(2)] + [plain] * n_plain, plain


def _stream_masks():
    qi = lax.broadcasted_iota(jnp.int32, (BAND, BAND), 0)
    kj = lax.broadcasted_iota(jnp.int32, (BAND, BAND), 1)
    mask_c = kj <= qi
    return mask_c, jnp.concatenate([kj >= qi, mask_c], axis=1)


def _attn_fwd_stream(P, layer):
    T = P.shape[1]
    n16 = T // 16
    nb = n16 // BAND
    scale = HD ** -0.5

    def body(q_ref, k_ref, v_ref, o_ref, l_ref, qs, ks, vs):
        for src, dst in ((q_ref, qs), (k_ref, ks), (v_ref, vs)):
            dst[...] = src[...].astype(BF16)
        mask_c, mask_pc = _stream_masks()
        for b in range(nb):
            rows = slice(b * BAND, (b + 1) * BAND)
            krows = slice(max(b - 1, 0) * BAND, (b + 1) * BAND)
            mask = mask_c if b == 0 else mask_pc
            for hd in range(NH):
                sl = slice(hd * HD, (hd + 1) * HD)
                s = jnp.where(mask, _dot_nt(qs[rows, sl], ks[krows, sl]) * scale, -1e30)
                m = jnp.max(s, axis=-1, keepdims=True)
                e = jnp.exp(s - m)
                den = jnp.sum(e, axis=-1, keepdims=True)
                o_ref[rows, sl] = _dot(e.astype(BF16), vs[krows, sl]) / den
                l_ref[rows, sl] = jnp.broadcast_to(m + jnp.log(den), (BAND, HD))

    ins, out = _whole_stream_specs(T, 0)
    Pv = P.reshape(NSH, 16, n16, DA)
    o, l = _pc(body, name=f"attn_fwd_d16_l{layer}", grid=(16,), in_specs=ins, out_specs=[out, out],
               out_shape=[S((16, n16, DA), F32)] * 2, scratch_shapes=[pltpu.VMEM((n16, DA), BF16)] * 3,
               compiler_params=_cp(1))(Pv, Pv, Pv)
    return o.reshape(T, DA), l.reshape(T, DA)


def _attn_bwd_stream(P, dO, lse, delta, acc, layer):
    T = P.shape[1]
    n16 = T // 16
    nb = n16 // BAND
    scale = HD ** -0.5
    first = acc is None

    def body(*refs):
        q_ref, k_ref, v_ref, do_ref, l_ref, dl_ref = refs[:6]
        if first:
            dq_ref, dk_ref, dv_ref = refs[6:9]
        else:
            aq_ref, ak_ref, av_ref, dq_ref, dk_ref, dv_ref = refs[6:12]
        qs, ks, vs, dos, okf, ovf = refs[-6:]
        for src, dst in ((q_ref, qs), (k_ref, ks), (v_ref, vs), (do_ref, dos)):
            dst[...] = src[...].astype(BF16)
        okf[...] = jnp.zeros_like(okf)
        ovf[...] = jnp.zeros_like(ovf)
        mask_c, mask_pc = _stream_masks()
        for b in range(nb):
            rows = slice(b * BAND, (b + 1) * BAND)
            krows = slice(max(b - 1, 0) * BAND, (b + 1) * BAND)
            mask = mask_c if b == 0 else mask_pc
            for hd in range(NH):
                sl = slice(hd * HD, (hd + 1) * HD)
                one = slice(hd * HD, hd * HD + 1)
                q, do, kk = qs[rows, sl], dos[rows, sl], ks[krows, sl]
                p = jnp.where(mask, jnp.exp(_dot_nt(q, kk) * scale - l_ref[rows, one]), 0.0)
                ds = (p * (_dot_nt(do, vs[krows, sl]) - dl_ref[rows, one]) * scale).astype(BF16)
                dq = _dot(ds, kk)
                dq_ref[rows, sl] = dq if first else aq_ref[rows, sl] + dq
                okf[krows, sl] += _dot_tn(ds, q)
                ovf[krows, sl] += _dot_tn(p.astype(BF16), do)
        dk_ref[...] = okf[...] if first else ak_ref[...] + okf[...]
        dv_ref[...] = ovf[...] if first else av_ref[...] + ovf[...]

    ins, out = _whole_stream_specs(T, 3 if first else 6)
    Pv = P.reshape(NSH, 16, n16, DA)
    view = lambda t: t.reshape(16, n16, DA)
    args = [Pv, Pv, Pv, view(dO), view(lse), view(delta)] + ([] if first else [view(t) for t in acc])
    dq, dk, dv = _pc(body, name=f"attn_bwd_d16_l{layer}", grid=(16,), in_specs=ins, out_specs=[out, out, out],
                     out_shape=[S((16, n16, DA), F32)] * 3,
                     scratch_shapes=[pltpu.VMEM((n16, DA), BF16)] * 4 + [pltpu.VMEM((n16, DA), F32)] * 2,
                     compiler_params=_cp(1))(*args)
    return dq.reshape(T, DA), dk.reshape(T, DA), dv.reshape(T, DA)


def _attn_fwd(P, d, layer):
    if d == 16:
        return _attn_fwd_stream(P, layer)
    T = P.shape[1]
    nb = T // d // BAND
    vshape = _pattern(d, T)[0]
    Pv = P.reshape((NSH,) + vshape)
    scale = HD ** -0.5

    def body(q_ref, kp_ref, kc_ref, vp_ref, vc_ref, o_ref, l_ref, qs, ks, vs, osc, lsc):
        b = pl.program_id(1)
        flat = lambda ref: ref[...].reshape(BAND, DA).astype(BF16)
        qs[...] = flat(q_ref)
        ks[0:BAND, :] = flat(kp_ref)
        ks[BAND:, :] = flat(kc_ref)
        vs[0:BAND, :] = flat(vp_ref)
        vs[BAND:, :] = flat(vc_ref)
        mask_c, mask_p = _band_masks(b, d)
        mask = jnp.concatenate([mask_p, mask_c], axis=1)
        for hd in range(NH):
            sl = slice(hd * HD, (hd + 1) * HD)
            s = jnp.where(mask, _dot_nt(qs[:, sl], ks[:, sl]) * scale, -1e30)
            m = jnp.max(s, axis=-1, keepdims=True)
            e = jnp.exp(s - m)
            den = jnp.sum(e, axis=-1, keepdims=True)
            osc[:, sl] = _dot(e.astype(BF16), vs[:, sl]) / den
            lsc[:, sl] = jnp.broadcast_to(m + jnp.log(den), (BAND, HD))
        o_ref[...] = osc[...].reshape(o_ref.shape)
        l_ref[...] = lsc[...].reshape(l_ref.shape)

    cur = lambda b: b
    prev = lambda b: jnp.maximum(b - 1, 0)
    out = _pattern_spec(d, T, cur)
    o, l = _pc(body, name=f"attn_fwd_d{d}_l{layer}", grid=(d, nb),
               in_specs=[_pattern_spec(d, T, cur, 0), _pattern_spec(d, T, prev, 1), _pattern_spec(d, T, cur, 1),
                         _pattern_spec(d, T, prev, 2), _pattern_spec(d, T, cur, 2)],
               out_specs=[out, out], out_shape=[S(vshape, F32)] * 2,
               scratch_shapes=[pltpu.VMEM((BAND, DA), BF16)] + [pltpu.VMEM((2 * BAND, DA), BF16)] * 2
               + [pltpu.VMEM((BAND, DA), F32)] * 2,
               compiler_params=_cp(2))(Pv, Pv, Pv, Pv, Pv)
    return o.reshape(T, DA), l.reshape(T, DA)


def _attn_bwd(P, dO, lse, delta, acc, d, layer):
    if d == 16:
        return _attn_bwd_stream(P, dO, lse, delta, acc, layer)
    T = P.shape[1]
    nb = T // d // BAND
    vshape = _pattern(d, T)[0]
    Pv = P.reshape((NSH,) + vshape)
    scale = HD ** -0.5
    first = acc is None

    def body(*refs):
        q_ref, kp_ref, kc_ref, vp_ref, vc_ref, do_ref, l_ref, dl_ref = refs[:8]
        if first:
            dq_ref, dk_ref, dv_ref = refs[8:11]
        else:
            aq_ref, ak_ref, av_ref, dq_ref, dk_ref, dv_ref = refs[8:14]
        qs, dos, ks, vs, ls, dls, oq, ok, ov, ck, cv = refs[-11:]
        b = pl.program_id(1)
        flat = lambda ref: ref[...].reshape(BAND, DA)

        @pl.when(b == 0)
        def _():
            ck[...] = jnp.zeros_like(ck)
            cv[...] = jnp.zeros_like(cv)

        @pl.when(b < nb)
        def _():
            qs[...] = flat(q_ref).astype(BF16)
            dos[...] = flat(do_ref).astype(BF16)
            ks[0:BAND, :] = flat(kp_ref).astype(BF16)
            ks[BAND:, :] = flat(kc_ref).astype(BF16)
            vs[0:BAND, :] = flat(vp_ref).astype(BF16)
            vs[BAND:, :] = flat(vc_ref).astype(BF16)
            ls[...] = flat(l_ref)
            dls[...] = flat(dl_ref)
            mask_c, mask_p = _band_masks(b, d)
            mask = jnp.concatenate([mask_p, mask_c], axis=1)
            for hd in range(NH):
                sl = slice(hd * HD, (hd + 1) * HD)
                one = slice(hd * HD, hd * HD + 1)
                q, do, kk = qs[:, sl], dos[:, sl], ks[:, sl]
                p = jnp.where(mask, jnp.exp(_dot_nt(q, kk) * scale - ls[:, one]), 0.0)
                ds = (p * (_dot_nt(do, vs[:, sl]) - dls[:, one]) * scale).astype(BF16)
                oq[:, sl] = _dot(ds, kk)
                dk2 = _dot_tn(ds, q)
                dv2 = _dot_tn(p.astype(BF16), do)
                ok[:, sl] = ck[:, sl] + dk2[0:BAND]
                ov[:, sl] = cv[:, sl] + dv2[0:BAND]
                ck[:, sl] = dk2[BAND:]
                cv[:, sl] = dv2[BAND:]
            if first:
                dq_ref[...] = oq[...].reshape(dq_ref.shape)
                dk_ref[...] = ok[...].reshape(dk_ref.shape)
                dv_ref[...] = ov[...].reshape(dv_ref.shape)
            else:
                dq_ref[...] = aq_ref[...] + oq[...].reshape(dq_ref.shape)
                dk_ref[...] = ak_ref[...] + ok[...].reshape(dk_ref.shape)
                dv_ref[...] = av_ref[...] + ov[...].reshape(dv_ref.shape)

        @pl.when(b == nb)
        def _():
            if first:
                dk_ref[...] = ck[...].reshape(dk_ref.shape)
                dv_ref[...] = cv[...].reshape(dv_ref.shape)
            else:
                dk_ref[...] = ak_ref[...] + ck[...].reshape(dk_ref.shape)
                dv_ref[...] = av_ref[...] + cv[...].reshape(dv_ref.shape)

    qb = lambda b: jnp.minimum(b, nb - 1)
    qprev = lambda b: jnp.maximum(qb(b) - 1, 0)
    kb = lambda b: jnp.maximum(b - 1, 0)
    qrow = _pattern_spec(d, T, qb)
    krow = _pattern_spec(d, T, kb)
    view = lambda t: t.reshape(vshape)
    ins = [Pv, Pv, Pv, Pv, Pv, view(dO), view(lse), view(delta)]
    specs = [_pattern_spec(d, T, qb, 0), _pattern_spec(d, T, qprev, 1), _pattern_spec(d, T, qb, 1),
             _pattern_spec(d, T, qprev, 2), _pattern_spec(d, T, qb, 2), qrow, qrow, qrow]
    if not first:
        ins += [view(t) for t in acc]
        specs += [qrow, krow, krow]
    dq, dk, dv = _pc(body, name=f"attn_bwd_d{d}_l{layer}", grid=(d, nb + 1), in_specs=specs,
                     out_specs=[qrow, krow, krow], out_shape=[S(vshape, F32)] * 3,
                     scratch_shapes=[pltpu.VMEM((BAND, DA), BF16)] * 2 + [pltpu.VMEM((2 * BAND, DA), BF16)] * 2
                     + [pltpu.VMEM((BAND, DA), F32)] * 7,
                     compiler_params=_cp(2))(*ins)
    return dq.reshape(T, DA), dk.reshape(T, DA), dv.reshape(T, DA)


def _ssm_prep(lam_re, lam_im, log_dt, b_re, b_im, c_re, c_im):
    dt = jnp.exp(log_dt)[:, None]
    er = jnp.exp(lam_re * dt)
    a_re = er * jnp.cos(lam_im * dt)
    a_im = er * jnp.sin(lam_im * dt)
    nr, ni = a_re - 1.0, a_im
    den = lam_re * lam_re + lam_im * lam_im
    cr = (nr * lam_re + ni * lam_im) / den
    ci = (ni * lam_re - nr * lam_im) / den
    bbr = cr[..., None] * b_re - ci[..., None] * b_im
    bbi = cr[..., None] * b_im + ci[..., None] * b_re
    eye = jnp.eye(8, dtype=F32)

    def bblock(bb):
        t = bb.reshape(4, 8, 64, 16).transpose(0, 1, 3, 2)
        return (t[:, :, :, None, :] * eye[None, :, None, :, None]).reshape(4, 128, 512)

    def cblock(cc):
        t = cc.reshape(4, 8, 16, 64).transpose(0, 1, 3, 2)
        return (t[:, :, :, None, :] * eye[None, :, None, :, None]).reshape(4, 512, 128)

    return (a_re.reshape(NLB, 1, 128), a_im.reshape(NLB, 1, 128), bblock(bbr), bblock(bbi), cblock(c_re), cblock(c_im))


def _perm_matrix(tm):
    n = tm // 16
    pm = np.zeros((tm, tm), np.float32)
    for r in range(16):
        pm[16 * np.arange(n) + r, r * n + np.arange(n)] = 1.0
    return jnp.asarray(pm, BF16)


def _pieces(x):
    p1 = x.astype(BF16)
    r1 = x - p1.astype(F32)
    p2 = r1.astype(BF16)
    return p1, p2, (r1 - p2.astype(F32)).astype(BF16)


def _to_time(x, pm):
    return sum(_dot(pm, p) for p in _pieces(x))


def _to_streams(x, pm):
    return sum(_dot_tn(pm, p) for p in _pieces(x))


def _stream_block(tm, cols, lead=None):
    if lead is None:
        return pl.BlockSpec((16, tm // 16, cols), lambda i: (0, i, 0))
    return pl.BlockSpec((None, 16, tm // 16, cols), lambda i: (lead, 0, i, 0))


def _reorder(t3, to_streams, name):
    B, T, C = t3.shape
    tm = TM

    def body(x_ref, pm_ref, o_ref):
        if to_streams:
            o_ref[...] = _to_streams(x_ref[...], pm_ref[...]).reshape(o_ref.shape)
        else:
            o_ref[...] = _to_time(x_ref[...].reshape(tm, C), pm_ref[...])

    time_blk = pl.BlockSpec((None, tm, C), lambda b, i: (b, i, 0))
    stream_blk = pl.BlockSpec((None, 16, tm // 16, C), lambda b, i: (b, 0, i, 0))
    src = t3 if to_streams else t3.reshape(B, 16, T // 16, C)
    out = _pc(body, name=name, grid=(B, T // tm),
              in_specs=[time_blk if to_streams else stream_blk, pl.BlockSpec((tm, tm), lambda b, i: (0, 0))],
              out_specs=stream_blk if to_streams else time_blk,
              out_shape=S((B, 16, T // 16, C) if to_streams else (B, T, C), F32),
              compiler_params=_cp(2))(src, _perm_matrix(tm))
    return out.reshape(B, T, C)


def _ssm_in(P, bre, bim, layer):
    T = P.shape[1]
    tm = TM

    def body(u_ref, pm_ref, br_ref, bi_ref, un_ref, or_ref, oi_ref):
        u = _to_time(u_ref[...].reshape(tm, DSS), pm_ref[...])
        un_ref[...] = u
        for s in range(4):
            uc = u[:, s * 128:(s + 1) * 128]
            r = _dot3(_dot, uc, br_ref[s])
            m = _dot3(_dot, uc, bi_ref[s])
            for q in range(4):
                or_ref[4 * s + q] = r[:, q * 128:(q + 1) * 128]
                oi_ref[4 * s + q] = m[:, q * 128:(q + 1) * 128]

    whole = pl.BlockSpec((4, 128, 512), lambda i: (0, 0, 0))
    st = pl.BlockSpec((NLB, tm, 128), lambda i: (0, i, 0))
    return _pc(body, name=f"ssm_in_l{layer}", grid=(T // tm,),
               in_specs=[_stream_block(tm, DSS, 3), pl.BlockSpec((tm, tm), lambda i: (0, 0)), whole, whole],
               out_specs=[pl.BlockSpec((tm, DSS), lambda i: (i, 0)), st, st],
               out_shape=[S((T, DSS), F32)] + [S((NLB, T, 128), F32)] * 2,
               compiler_params=_cp(1))(P.reshape(NSH, 16, T // 16, DSS), _perm_matrix(tm), bre, bim)


def _scan(br, bi, a_re, a_im, reverse, layer):
    T = br.shape[1]
    nbk = 4
    tt = min(T, 1024)
    nT = T // tt
    ntile = tt // 8
    sgn = -1.0 if reverse else 1.0
    last = 0 if reverse else 7

    def body(br_ref, bi_ref, ar_ref, ai_ref, xr_ref, xi_ref, cr, ci):
        @pl.when(pl.program_id(1) == 0)
        def _():
            cr[...] = jnp.zeros_like(cr)
            ci[...] = jnp.zeros_like(ci)

        row = lax.broadcasted_iota(jnp.int32, (8, 128), 0)
        consts = []
        for k in range(nbk):
            a1r = jnp.broadcast_to(ar_ref[k], (8, 128))
            a1i = sgn * jnp.broadcast_to(ai_ref[k], (8, 128))
            pows = [(a1r, a1i)]
            for _ in range(7):
                pr, pi_ = pows[-1]
                pows.append((a1r * pr - a1i * pi_, a1r * pi_ + a1i * pr))
            rounds = []
            for s in (1, 2, 4):
                inside = (row <= 7 - s) if reverse else (row >= s)
                rounds.append((jnp.where(inside, pows[s - 1][0], 0.0), jnp.where(inside, pows[s - 1][1], 0.0)))
            cmr, cmi = jnp.zeros((8, 128), F32), jnp.zeros((8, 128), F32)
            for r in range(8):
                e = (7 - r) if reverse else r
                cmr = jnp.where(row == r, pows[e][0], cmr)
                cmi = jnp.where(row == r, pows[e][1], cmi)
            consts.append((rounds, cmr, cmi))

        def tile(i, carry):
            j = (ntile - 1 - i) if reverse else i
            rows = pl.ds(pl.multiple_of(j * 8, 8), 8)
            out = []
            for k in range(nbk):
                rounds, cmr, cmi = consts[k]
                xr = br_ref[k, rows, :]
                xi = bi_ref[k, rows, :]
                for (mr, mi), s in zip(rounds, (1, 2, 4)):
                    sh = (8 - s) if reverse else s
                    rr = pltpu.roll(xr, sh, 0)
                    ri = pltpu.roll(xi, sh, 0)
                    xr, xi = xr + (mr * rr - mi * ri), xi + (mr * ri + mi * rr)
                c_r, c_i = carry[k]
                xr, xi = xr + (cmr * c_r - cmi * c_i), xi + (cmr * c_i + cmi * c_r)
                xr_ref[k, rows, :] = xr
                xi_ref[k, rows, :] = xi
                out.append((jnp.broadcast_to(xr[last:last + 1, :], (8, 128)),
                            jnp.broadcast_to(xi[last:last + 1, :], (8, 128))))
            return tuple(out)

        carry = lax.fori_loop(0, ntile, tile, tuple((cr[k], ci[k]) for k in range(nbk)), unroll=2)
        for k in range(nbk):
            cr[k] = carry[k][0]
            ci[k] = carry[k][1]

    tmap = (lambda t: nT - 1 - t) if reverse else (lambda t: t)
    st = pl.BlockSpec((nbk, tt, 128), lambda i, t: (i, tmap(t), 0))
    av = pl.BlockSpec((nbk, 1, 128), lambda i, t: (i, 0, 0))
    return _pc(body, name=f"scan_{'bwd' if reverse else 'fwd'}_l{layer}", grid=(NLB // nbk, nT),
               in_specs=[st, st, av, av], out_specs=[st, st], out_shape=[S((NLB, T, 128), F32)] * 2,
               scratch_shapes=[pltpu.VMEM((nbk, 8, 128), F32)] * 2, compiler_params=_cp(2))(br, bi, a_re, a_im)


def _ssm_out(xr, xi, u, cre, cim, dvec, wglu, bglu, layer):
    T = u.shape[0]
    tm = TM

    def body(xr_ref, xi_ref, u_ref, pm_ref, cr_ref, ci_ref, d_ref, w_ref, bg_ref, s_ref, y_ref, z_ref):
        ys = []
        for s in range(4):
            xrc = jnp.concatenate([xr_ref[4 * s + q] for q in range(4)], axis=1)
            xic = jnp.concatenate([xi_ref[4 * s + q] for q in range(4)], axis=1)
            ys.append(_dot3(_dot, xrc, cr_ref[s]) - _dot3(_dot, xic, ci_ref[s]))
        y = jnp.concatenate(ys, axis=1) + d_ref[...] * u_ref[...]
        yg = _gelu(y)
        ygb = yg.astype(BF16)
        z = bg_ref[...] + sum(_dot(ygb[:, j * 128:(j + 1) * 128], w_ref[j]) for j in range(NSH))
        y_ref[...] = y
        z_ref[...] = z
        s_ref[...] = _to_streams(yg * jax.nn.sigmoid(z), pm_ref[...]).reshape(s_ref.shape)

    st = pl.BlockSpec((NLB, tm, 128), lambda i: (0, i, 0))
    cw = pl.BlockSpec((4, 512, 128), lambda i: (0, 0, 0))
    half = pl.BlockSpec((tm, DSS), lambda i: (i, 0))
    s, y, z = _pc(body, name=f"ssm_out_l{layer}", grid=(T // tm,),
                  in_specs=[st, st, half, pl.BlockSpec((tm, tm), lambda i: (0, 0)), cw, cw, _gain_spec(DSS, layer),
                            pl.BlockSpec((NSH, None, 128, DSS), lambda i: (0, 0, 0, 0)), _gain_spec(DSS, layer)],
                  out_specs=[_stream_block(tm, DSS), half, half],
                  out_shape=[S((16, T // 16, DSS), F32), S((T, DSS), F32), S((T, DSS), F32)],
                  compiler_params=_cp(1))(xr, xi, u, _perm_matrix(tm), cre, cim, dvec, wglu, bglu)
    return s.reshape(T, DSS), y, z


def _ssm_out_bwd(dssm, y, z, xr, xi, u, cre, cim, dvec, wglu, layer):
    T = u.shape[0]
    tm = TM

    def body(ds_ref, pm_ref, y_ref, z_ref, xr_ref, xi_ref, u_ref, cr_ref, ci_ref, d_ref, w_ref,
             gr_ref, gi_ref, du_ref, dz_ref, yg_ref, dbg_ref, dd_ref, dcr_ref, dci_ref):
        i = pl.program_id(0)

        @pl.when(i == 0)
        def _():
            dbg_ref[...] = jnp.zeros_like(dbg_ref)
            dd_ref[...] = jnp.zeros_like(dd_ref)
            dcr_ref[...] = jnp.zeros_like(dcr_ref)
            dci_ref[...] = jnp.zeros_like(dci_ref)

        yv = y_ref[...]
        yg = _gelu(yv)
        sg = jax.nn.sigmoid(z_ref[...])
        ds = _to_time(ds_ref[...].reshape(tm, DSS), pm_ref[...])
        dz = ds * yg * sg * (1.0 - sg)
        dzb = dz.astype(BF16)
        dz_ref[...] = dzb
        yg_ref[...] = yg.astype(BF16)
        dbg_ref[...] += jnp.sum(dz, axis=0, keepdims=True)
        dyg = ds * sg + jnp.concatenate([_dot_nt(dzb, w_ref[j]) for j in range(NSH)], axis=1)
        dy = dyg * _gelu_grad(yv)
        u = u_ref[...]
        dd_ref[...] += jnp.sum(dy * u, axis=0, keepdims=True)
        du_ref[...] = dy * d_ref[...]
        for s in range(4):
            dyc = dy[:, s * 128:(s + 1) * 128]
            g_r = _dot3(_dot_nt, dyc, cr_ref[s])
            g_i = -_dot3(_dot_nt, dyc, ci_ref[s])
            for q in range(4):
                gr_ref[4 * s + q] = g_r[:, q * 128:(q + 1) * 128]
                gi_ref[4 * s + q] = g_i[:, q * 128:(q + 1) * 128]
            xrc = jnp.concatenate([xr_ref[4 * s + q] for q in range(4)], axis=1)
            xic = jnp.concatenate([xi_ref[4 * s + q] for q in range(4)], axis=1)
            dcr_ref[s] += _dot3(_dot_tn, xrc, dyc)
            dci_ref[s] -= _dot3(_dot_tn, xic, dyc)

    st = pl.BlockSpec((NLB, tm, 128), lambda i: (0, i, 0))
    cw = pl.BlockSpec((4, 512, 128), lambda i: (0, 0, 0))
    half = pl.BlockSpec((tm, DSS), lambda i: (i, 0))
    return _pc(body, name=f"ssm_out_bwd_l{layer}", grid=(T // tm,),
               in_specs=[_stream_block(tm, DSS), pl.BlockSpec((tm, tm), lambda i: (0, 0)), half, half, st, st, half,
                         cw, cw, _gain_spec(DSS, layer), pl.BlockSpec((NSH, None, 128, DSS), lambda i: (0, 0, 0, 0))],
               out_specs=[st, st, half, half, half, _row_acc_spec(DSS), _row_acc_spec(DSS), cw, cw],
               out_shape=[S((NLB, T, 128), F32)] * 2 + [S((T, DSS), F32), S((T, DSS), BF16), S((T, DSS), BF16),
                                                        S((1, DSS), F32), S((1, DSS), F32),
                                                        S((4, 512, 128), F32), S((4, 512, 128), F32)],
               compiler_params=_cp(1))(dssm.reshape(16, T // 16, DSS), _perm_matrix(tm), y, z, xr, xi, u, cre, cim,
                                       dvec, wglu)


def _ssm_da(gr, gi, xr, xi, layer):
    T = gr.shape[1]
    tb = 4096 if T % 4096 == 0 else T

    def body(gr_ref, gi_ref, xr_ref, xi_ref, dr_ref, di_ref, lr, li):
        t = pl.program_id(1)

        @pl.when(t == 0)
        def _():
            dr_ref[...] = jnp.zeros_like(dr_ref)
            di_ref[...] = jnp.zeros_like(di_ref)
            lr[...] = jnp.zeros_like(lr)
            li[...] = jnp.zeros_like(li)

        g_r, g_i, x_r, x_i = gr_ref[...], gi_ref[...], xr_ref[...], xi_ref[...]
        pr = pltpu.roll(x_r, 1, 0)
        pi_ = pltpu.roll(x_i, 1, 0)
        g0r, g0i = g_r[0:1, :], g_i[0:1, :]
        fr = lr[7:8, :] - x_r[tb - 1:tb, :]
        fi = li[7:8, :] - x_i[tb - 1:tb, :]
        dr_ref[...] += jnp.sum(g_r * pr + g_i * pi_, axis=0, keepdims=True) + g0r * fr + g0i * fi
        di_ref[...] += jnp.sum(g_i * pr - g_r * pi_, axis=0, keepdims=True) + g0i * fr - g0r * fi
        lr[...] = x_r[tb - 8:tb, :]
        li[...] = x_i[tb - 8:tb, :]

    st = pl.BlockSpec((None, tb, 128), lambda k, t: (k, t, 0))
    out = pl.BlockSpec((None, 1, 128), lambda k, t: (k, 0, 0))
    return _pc(body, name=f"ssm_da_l{layer}", grid=(NLB, T // tb), in_specs=[st] * 4, out_specs=[out, out],
               out_shape=[S((NLB, 1, 128), F32)] * 2, scratch_shapes=[pltpu.VMEM((8, 128), F32)] * 2,
               compiler_params=_cp(2))(gr, gi, xr, xi)


def _ssm_in_bwd(gr, gi, u, bre, bim, du_direct, layer):
    T = u.shape[0]
    tm = TM

    def body(gr_ref, gi_ref, u_ref, pm_ref, br_ref, bi_ref, dd_ref, du_ref, dbr_ref, dbi_ref):
        i = pl.program_id(0)

        @pl.when(i == 0)
        def _():
            dbr_ref[...] = jnp.zeros_like(dbr_ref)
            dbi_ref[...] = jnp.zeros_like(dbi_ref)

        dus = []
        for s in range(4):
            grc = jnp.concatenate([gr_ref[4 * s + q] for q in range(4)], axis=1)
            gic = jnp.concatenate([gi_ref[4 * s + q] for q in range(4)], axis=1)
            uc = u_ref[:, s * 128:(s + 1) * 128]
            dus.append(_dot3(_dot_nt, grc, br_ref[s]) + _dot3(_dot_nt, gic, bi_ref[s]))
            dbr_ref[s] += _dot3(_dot_tn, uc, grc)
            dbi_ref[s] += _dot3(_dot_tn, uc, gic)
        du = jnp.concatenate(dus, axis=1) + dd_ref[...]
        du_ref[...] = _to_streams(du, pm_ref[...]).reshape(du_ref.shape)

    whole = pl.BlockSpec((4, 128, 512), lambda i: (0, 0, 0))
    st = pl.BlockSpec((NLB, tm, 128), lambda i: (0, i, 0))
    half = pl.BlockSpec((tm, DSS), lambda i: (i, 0))
    du, dbr, dbi = _pc(body, name=f"ssm_in_bwd_l{layer}", grid=(T // tm,),
                       in_specs=[st, st, half, pl.BlockSpec((tm, tm), lambda i: (0, 0)), whole, whole, half],
                       out_specs=[_stream_block(tm, DSS), whole, whole],
                       out_shape=[S((16, T // 16, DSS), F32), S((4, 128, 512), F32), S((4, 128, 512), F32)],
                       compiler_params=_cp(1))(gr, gi, u, _perm_matrix(tm), bre, bim, du_direct)
    return du.reshape(T, DSS), dbr, dbi


def _mix_out(outs, lses, ssm, h, attn_g, ssm_g, post_g, wout, layer):
    T = h.shape[0]
    tm = TM

    def body(o1, o2, o3, l1, l2, l3, s_ref, h_ref, ag_ref, sg_ref, pg_ref, w_ref, ho_ref, at_ref, ls_ref, mx_ref, mo_ref):
        la, lb, lc = l1[...], l2[...], l3[...]
        m = jnp.maximum(jnp.maximum(la, lb), lc)
        wa, wb, wc = jnp.exp(la - m), jnp.exp(lb - m), jnp.exp(lc - m)
        zs = wa + wb + wc
        attn = (wa * o1[...] + wb * o2[...] + wc * o3[...]) / zs
        at_ref[...] = attn
        ls_ref[...] = m + jnp.log(zs)
        mixed = jnp.concatenate([_rms_fwd(attn, ag_ref[...]), _rms_fwd(s_ref[...], sg_ref[...])], axis=1).astype(BF16)
        mx_ref[...] = mixed
        mo = sum(_dot(mixed[:, j * 256:(j + 1) * 256], w_ref[j]) for j in range(NSH))
        mo_ref[...] = mo
        ho_ref[...] = h_ref[...] + _rms_fwd(mo, pg_ref[...])

    row = pl.BlockSpec((tm, D), lambda i: (i, 0))
    half = pl.BlockSpec((tm, DA), lambda i: (i, 0))
    return _pc(body, name=f"mix_out_l{layer}", grid=(T // tm,),
               in_specs=[half] * 7 + [row, _gain_spec(DA, layer), _gain_spec(DSS, layer), _gain_spec(D, layer),
                                      pl.BlockSpec((NSH, None, 256, D), lambda i: (0, 0, 0, 0))],
               out_specs=[row, half, half, row, row],
               out_shape=[S((T, D), F32), S((T, DA), F32), S((T, DA), F32), S((T, D), BF16), S((T, D), F32)],
               compiler_params=_cp(1))(*outs, *lses, ssm, h, attn_g, ssm_g, post_g, wout)


def _mix_out_bwd(dout, mo, attn, ssm, attn_g, ssm_g, post_g, wout, layer):
    T = dout.shape[0]
    tm = TM
    head_sum =jnp.asarray(np.kron(np.eye(NH, dtype=np.float32), np.ones((HD, HD), np.float32)), BF16)

    def body(do_ref, mo_ref, at_ref, s_ref, ag_ref, sg_ref, pg_ref, w_ref, e_ref,
             da_ref, ds_ref, dl_ref, dmo_ref, dpg_ref, dag_ref, dsg_ref):
        i = pl.program_id(0)

        @pl.when(i == 0)
        def _():
            dpg_ref[...] = jnp.zeros_like(dpg_ref)
            dag_ref[...] = jnp.zeros_like(dag_ref)
            dsg_ref[...] = jnp.zeros_like(dsg_ref)

        dmo, dpg = _rms_bwd(do_ref[...], mo_ref[...], pg_ref[...])
        dpg_ref[...] += dpg
        dmob = dmo.astype(BF16)
        dmo_ref[...] = dmob
        dmix = jnp.concatenate([_dot_nt(dmob, w_ref[j]) for j in range(NSH)], axis=1)
        attn = at_ref[...]
        dat, dag = _rms_bwd(dmix[:, :DA], attn, ag_ref[...])
        dss, dsg = _rms_bwd(dmix[:, DA:], s_ref[...], sg_ref[...])
        dag_ref[...] += dag
        dsg_ref[...] += dsg
        da_ref[...] = dat
        ds_ref[...] = dss
        prod = dat * attn
        p1 = prod.astype(BF16)
        r1 = prod - p1.astype(F32)
        p2 = r1.astype(BF16)
        p3 = (r1 - p2.astype(F32)).astype(BF16)
        e = e_ref[...]
        dl_ref[...] = _dot(p1, e) + _dot(p2, e) + _dot(p3, e)

    row = pl.BlockSpec((tm, D), lambda i: (i, 0))
    half = pl.BlockSpec((tm, DA), lambda i: (i, 0))
    return _pc(body, name=f"mix_out_bwd_l{layer}", grid=(T // tm,),
               in_specs=[row, row, half, half, _gain_spec(DA, layer), _gain_spec(DSS, layer), _gain_spec(D, layer),
                         pl.BlockSpec((NSH, None, 256, D), lambda i: (0, 0, 0, 0)),
                         pl.BlockSpec((DA, DA), lambda i: (0, 0))],
               out_specs=[half, half, half, row, _row_acc_spec(D), _row_acc_spec(DA), _row_acc_spec(DSS)],
               out_shape=[S((T, DA), F32)] * 3 + [S((T, D), BF16), S((1, D), F32), S((1, DA), F32), S((1, DSS), F32)],
               compiler_params=_cp(1))(dout, mo, attn, ssm, attn_g, ssm_g, post_g, wout, head_sum)


def _ple_fwd(h, p3, wup, wgate, post_g, layer):
    T = h.shape[0]
    tm = TM

    def body(h_ref, p_ref, wu_ref, wg_ref, g_ref, ho_ref, e_ref, gt_ref):
        hv = h_ref[...]
        hb = hv.astype(BF16)
        pb = p_ref[...].astype(BF16)
        gte = sum(_dot(hb[:, j * 256:(j + 1) * 256], wg_ref[j]) for j in range(NSH))
        e = jnp.concatenate([_dot(pb, wu_ref[j]) for j in range(NSH)], axis=1)
        e_ref[...] = e
        gt_ref[...] = gte
        ho_ref[...] = hv + _rms_fwd(e * jax.nn.sigmoid(gte), g_ref[...])

    row = pl.BlockSpec((tm, D), lambda i: (i, 0))
    return _pc(body, name=f"ple_fwd_l{layer}", grid=(T // tm,),
               in_specs=[row, pl.BlockSpec((None, tm, PLE), lambda i: (layer, i, 0)),
                         pl.BlockSpec((NSH, None, PLE, 256), lambda i: (0, 0, 0, 0)),
                         pl.BlockSpec((NSH, None, 256, D), lambda i: (0, 0, 0, 0)), _gain_spec(D, layer)],
               out_specs=[row, row, row], out_shape=[S((T, D), F32)] * 3,
               compiler_params=_cp(1))(h, p3, wup, wgate, post_g)


def _ple_bwd(dout, e, gte, wgate, post_g, layer):
    T = dout.shape[0]
    tm = TM

    def body(do_ref, e_ref, gt_ref, wg_ref, g_ref, dh_ref, de_ref, dgt_ref, dg_ref):
        i = pl.program_id(0)

        @pl.when(i == 0)
        def _():
            dg_ref[...] = jnp.zeros_like(dg_ref)

        ev = e_ref[...]
        sg = jax.nn.sigmoid(gt_ref[...])
        do = do_ref[...]
        dple, dg = _rms_bwd(do, ev * sg, g_ref[...])
        dg_ref[...] += dg
        de = (dple * sg).astype(BF16)
        for j in range(NSH):
            de_ref[j] = de[:, j * 256:(j + 1) * 256]
        dgb = (dple * ev * sg * (1.0 - sg)).astype(BF16)
        dgt_ref[...] = dgb
        dh_ref[...] = do + jnp.concatenate([_dot_nt(dgb, wg_ref[j]) for j in range(NSH)], axis=1)

    row = pl.BlockSpec((tm, D), lambda i: (i, 0))
    return _pc(body, name=f"ple_bwd_l{layer}", grid=(T // tm,),
               in_specs=[row, row, row, pl.BlockSpec((NSH, None, 256, D), lambda i: (0, 0, 0, 0)), _gain_spec(D, layer)],
               out_specs=[row, pl.BlockSpec((NSH, tm, 256), lambda i: (0, i, 0)), row, _row_acc_spec(D)],
               out_shape=[S((T, D), F32), S((NSH, T, 256), BF16), S((T, D), BF16), S((1, D), F32)],
               compiler_params=_cp(1))(dout, e, gte, wgate, post_g)


def _loss_head(h, target):
    T = h.shape[0]
    tm = TM

    def body(h_ref, t_ref, dy_ref, l_ref):
        i = pl.program_id(0)

        @pl.when(i == 0)
        def _():
            l_ref[...] = jnp.zeros_like(l_ref)

        err = h_ref[...] - t_ref[...]
        dy_ref[...] = err * (1.0 / D)
        l_ref[...] += jnp.broadcast_to((0.5 / D) * jnp.sum(err * err), (1, 128))

    row = pl.BlockSpec((tm, D), lambda i: (i, 0))
    return _pc(body, name="loss_head", grid=(T // tm,), in_specs=[row, row],
               out_specs=[row, pl.BlockSpec((1, 128), lambda i: (0, 0))],
               out_shape=[S((T, D), F32), S((1, 128), F32)], compiler_params=_cp(1))(h, target)


def _local_step(x, p3, pos_col, target, weights_of, upper_grads_done, Sm):
    L = p3.shape[0]
    g3 = {n: Sm[n].reshape(L, 1, -1) for n in ("ffn1_pre_g", "ffn1_post_g", "mix_pre_g", "attn_norm_g", "ssm_norm_g",
                                                "mix_post_g", "ffn2_pre_g", "ffn2_post_g", "ple_post_g", "ssm_b_glu", "ssm_d")}
    rot = _rot_tables(pos_col)
    prep_names = ("ssm_lam_re", "ssm_lam_im", "ssm_log_dt", "ssm_b_re", "ssm_b_im", "ssm_c_re", "ssm_c_im")
    prep_all, prep_vjp = jax.vjp(jax.vmap(_ssm_prep), *[Sm[n] for n in prep_names])
    prep_cot = [None] * L

    saved = []
    h = x
    for l in range(L):
        W = weights_of(l, h)
        sv = {"h0": h, "W": W}
        h, sv["a1"], sv["b1"], sv["f1"], sv["xn1"] = _ffn_fwd(
            h, g3["ffn1_pre_g"], g3["ffn1_post_g"], W["ffn1_w_gate"], W["ffn1_w_up"], W["ffn1_w_down"], l, "1")
        sv["h1"] = h
        P, sv["ain"] = _mix_proj(h, g3["mix_pre_g"], W["w_in"], rot, l)
        sv["P"] = P
        ol = [_attn_fwd(P, d, l) for d in PATTERN_DILATIONS]
        prep = tuple(t[l] for t in prep_all)
        a_re, a_im, bre, bim, cre, cim = prep
        sv["prep"] = prep
        sv["u"], bur, bui = _ssm_in(P, bre, bim, l)
        xr, xi = _scan(bur, bui, a_re, a_im, False, l)
        sv["xr"], sv["xi"] = xr, xi
        ssm, sv["y"], sv["z"] = _ssm_out(xr, xi, sv["u"], cre, cim, g3["ssm_d"], W["ssm_w_glu"], g3["ssm_b_glu"], l)
        sv["ssm"] = ssm
        h, sv["attn"], sv["lse"], sv["mixed"], sv["mo"] = _mix_out(
            [o for o, _ in ol], [s for _, s in ol], ssm, h, g3["attn_norm_g"], g3["ssm_norm_g"], g3["mix_post_g"],
            W["w_out"], l)
        sv["h2"] = h
        h, sv["a2"], sv["b2"], sv["f2"], sv["xn2"] = _ffn_fwd(
            h, g3["ffn2_pre_g"], g3["ffn2_post_g"], W["ffn2_w_gate"], W["ffn2_w_up"], W["ffn2_w_down"], l, "2")
        sv["h3"] = h
        h, sv["e"], sv["gte"] = _ple_fwd(h, p3, W["ple_w_up"], W["ple_w_gate"], g3["ple_post_g"], l)
        saved.append(sv)

    dh, loss = _loss_head(h, target)

    G_upper = {n: lax.empty((NSH, L - 1, r, c), BF16) for n, r, c in BIG} if L > 1 else {}
    G_first = {n: lax.empty((NSH, 1, r, c), BF16) for n, r, c in BIG}
    sg = {n: [None] * L for n in SMALL}
    whole, shard, kcol = "whole", "shard", "cols"
    ple_g = g3["ple_post_g"]
    for l in reversed(range(L)):
        sv = saved[l]
        W = sv["W"]
        G, gl = (G_first, 0) if l == 0 else (G_upper, l - 1)
        if l == 0 and L > 1:
            ple_g = ple_g + upper_grads_done(G_upper)
        dh, de, dgte, sg["ple_post_g"][l] = _ple_bwd(dh, sv["e"], sv["gte"], W["ple_w_gate"], ple_g, l)
        G["ple_w_up"] = _dw(p3[l][None], de, G["ple_w_up"], gl, PLE, 256, whole, shard, f"dw_ple_up_l{l}")
        G["ple_w_gate"] = _dw(sv["h3"][None], dgte[None], G["ple_w_gate"], gl, 256, D, kcol, whole, f"dw_ple_gate_l{l}")
        dh, df, da, db, hh, sg["ffn2_pre_g"][l], sg["ffn2_post_g"][l] = _ffn_bwd(
            dh, sv["h2"], sv["f2"], sv["a2"], sv["b2"], g3["ffn2_pre_g"], g3["ffn2_post_g"],
            W["ffn2_w_gate"], W["ffn2_w_up"], W["ffn2_w_down"], l, "2")
        G["ffn2_w_gate"] = _dw(da, sv["xn2"][None], G["ffn2_w_gate"], gl, DFS, D, shard, whole, f"dw_ffn2_gate_l{l}")
        G["ffn2_w_up"] = _dw(db, sv["xn2"][None], G["ffn2_w_up"], gl, DFS, D, shard, whole, f"dw_ffn2_up_l{l}")
        G["ffn2_w_down"] = _dw(hh, df[None], G["ffn2_w_down"], gl, DFS, D, shard, whole, f"dw_ffn2_down_l{l}")
        a_re, a_im, bre, bim, cre, cim = sv["prep"]
        dattn, dssm, delta, dmo, sg["mix_post_g"][l], sg["attn_norm_g"][l], sg["ssm_norm_g"][l] = _mix_out_bwd(
            dh, sv["mo"], sv["attn"], sv["ssm"], g3["attn_norm_g"], g3["ssm_norm_g"], g3["mix_post_g"], W["w_out"], l)
        G["w_out"] = _dw(sv["mixed"][None], dmo[None], G["w_out"], gl, 256, D, kcol, whole, f"dw_out_l{l}")
        gnr, gni, du_direct, dz, yg, sg["ssm_b_glu"][l], dd, dcre, dcim = _ssm_out_bwd(
            dssm, sv["y"], sv["z"], sv["xr"], sv["xi"], sv["u"], cre, cim, g3["ssm_d"], W["ssm_w_glu"], l)
        sg["ssm_d"][l] = dd.reshape(Sm["ssm_d"].shape[1:])
        G["ssm_w_glu"] = _dw(yg[None], dz[None], G["ssm_w_glu"], gl, 128, DSS, kcol, whole, f"dw_glu_l{l}")
        gr, gi = _scan(gnr, gni, a_re, a_im, True, l)
        dar, dai = _ssm_da(gr, gi, sv["xr"], sv["xi"], l)
        du, dbre, dbim = _ssm_in_bwd(gr, gi, sv["u"], bre, bim, du_direct, l)
        prep_cot[l] = (dar, dai, dbre, dbim, dcre, dcim)
        acc = None
        for d in PATTERN_DILATIONS:
            acc = _attn_bwd(sv["P"], dattn, sv["lse"], delta, acc, d, l)
        dh, dP, sg["mix_pre_g"][l] = _mix_proj_bwd(acc[0], acc[1], acc[2], du, dh, sv["h1"], g3["mix_pre_g"],
                                                   W["w_in"], rot, l)
        G["w_in"] = _dw(sv["ain"][None], dP, G["w_in"], gl, D, DA,whole, shard, f"dw_in_l{l}")
        dh, df, da, db, hh, sg["ffn1_pre_g"][l], sg["ffn1_post_g"][l] = _ffn_bwd(
            dh, sv["h0"], sv["f1"], sv["a1"], sv["b1"], g3["ffn1_pre_g"], g3["ffn1_post_g"],
            W["ffn1_w_gate"], W["ffn1_w_up"], W["ffn1_w_down"], l, "1")
        G["ffn1_w_gate"] = _dw(da, sv["xn1"][None], G["ffn1_w_gate"], gl, DFS, D, shard, whole, f"dw_ffn1_gate_l{l}")
        G["ffn1_w_up"] = _dw(db, sv["xn1"][None], G["ffn1_w_up"], gl, DFS, D, shard, whole, f"dw_ffn1_up_l{l}")
        G["ffn1_w_down"] = _dw(hh, df[None], G["ffn1_w_down"], gl, DFS, D, shard, whole, f"dw_ffn1_down_l{l}")

    small = {n: jnp.stack([g.reshape(Sm[n].shape[1:]) for g in sg[n]]) for n in SMALL if n not in prep_names}
    small.update(zip(prep_names, prep_vjp(tuple(jnp.stack(c) for c in zip(*prep_cot)))))
    return loss, dh, G_upper, G_first, small


HBM_SPEC = pl.BlockSpec(memory_space=pltpu.HBM)


def _place():
    x, y, c = lax.axis_index("x"), lax.axis_index("y"), lax.axis_index("c")
    chips = [(1 - x, y), (x, 1 - y), (1 - x, 1 - y)]
    return x, y, c, chips


def _comm_params():
    return pltpu.CompilerParams(vmem_limit_bytes=VMEM_LIMIT)


def _gather_weights(ws, lands):
    n = len(ws)

    def body(*refs):
        ins, outs = refs[:n], refs[2 * n:3 * n]
        s_ici, r_ici, s_d2d, r_d2d = refs[3 * n:]
        x, y, c, chips = _place()

        def half(ref, t, hc):
            r2 = ws[t].shape[1] // 2
            return ref.at[:, pl.ds(hc * r2, r2), :]

        def ici(t, k, src_chip, to):
            j = 2 * src_chip[0] + src_chip[1]
            src = half(ins[t], t, c) if to is not None else half(outs[t].at[j], t, c)
            return pltpu.make_async_remote_copy(src_ref=src, dst_ref=half(outs[t].at[j], t, c),
                                                send_sem=s_ici.at[3 * t + k], recv_sem=r_ici.at[3 * t + k],
                                                device_id=to if to is not None else (x, y, c), device_id_type=MESH)

        def d2d(t, k, hc):
            j = 2 * chips[k][0] + chips[k][1]
            r = half(outs[t].at[j], t, hc)
            return pltpu.make_async_remote_copy(src_ref=r, dst_ref=r, send_sem=s_d2d.at[3 * t + k],
                                                recv_sem=r_d2d.at[3 * t + k], device_id=(x, y, 1 - c),
                                                device_id_type=MESH)

        sends = [ici(t, k, (x, y), (*chips[k], c)) for t in range(n) for k in range(3)]
        for cp in sends:
            cp.start()
        passed = []
        for t in range(n):
            for k in range(3):
                ici(t, k, chips[k], None).wait_recv()
                passed.append(d2d(t, k, c))
                passed[-1].start()
        for t in range(n):
            for k in range(3):
                d2d(t, k, 1 - c).wait_recv()
        for cp in sends + passed:
            cp.wait_send()

    return _pc(body, name="gather_weights", in_specs=[HBM_SPEC] * (2 * n), out_specs=[HBM_SPEC] * n,
               out_shape=[S(z.shape, z.dtype) for z in lands], input_output_aliases={n + t: t for t in range(n)},
               scratch_shapes=[pltpu.SemaphoreType.DMA((3 * n,))] * 4, compiler_params=_comm_params())(*ws, *lands)


SEM_SPEC = pl.BlockSpec(memory_space=pltpu.SEMAPHORE)
ANY_SPEC = pl.BlockSpec(memory_space=pl.ANY)
SPLIT_EFFECT = pltpu.SideEffectType.DATAFLOW_SIDE_EFFECTING


def _in_hbm(t):
    return pltpu.with_memory_space_constraint(t, pltpu.HBM)


def _place_own(ws, me_arr, layer):
    n = len(ws)

    def body(me_ref, *refs):
        for t in range(n):
            refs[n + t][...] = refs[t][...]

    gs = pltpu.PrefetchScalarGridSpec(
        num_scalar_prefetch=1, grid=(2,),
        in_specs=[pl.BlockSpec((w.shape[0], w.shape[1] // 2, w.shape[2]), lambda i, me: (0, i, 0)) for w in ws],
        out_specs=[pl.BlockSpec((None, w.shape[0], w.shape[1] // 2, w.shape[2]), lambda i, me: (me[0], 0, i, 0))
                   for w in ws])
    return _pc(body, name=f"gather_place_own_l{layer}", grid_spec=gs,
               out_shape=[S((NSH,) + w.shape, w.dtype) for w in ws], compiler_params=_cp(1))(me_arr, *ws)


def _gather_start(ws, lands, after, layer):
    n = len(ws)

    def body(*refs):
        ins, lz = refs[:n], refs[n:2 * n]
        s_sem, r_sem = refs[2 * n + 1], refs[2 * n + 2]
        token = refs[-1]
        x, y, c, chips = _place()
        for t in range(n):
            for k in range(3):
                pltpu.make_async_remote_copy(src_ref=ins[t], dst_ref=lz[t].at[2 * x + y], send_sem=s_sem.at[3 * t + k],
                                             recv_sem=r_sem.at[3 * t + k], device_id=(*chips[k], c),
                                             device_id_type=MESH).start()
        token[...] = jnp.zeros_like(token)

    hbm = [pltpu.HBM(w.shape, w.dtype) for w in ws] + [pltpu.HBM(z.shape, z.dtype) for z in lands]
    out = _pc(body, name=f"gather_start_l{layer}",
              out_shape=(pltpu.SemaphoreType.DMA((3 * n,)), pltpu.SemaphoreType.DMA((3 * n,)), *hbm, S((8, 128), F32)),
              in_specs=[HBM_SPEC] * (2 * n) + [ANY_SPEC],
              out_specs=(SEM_SPEC, SEM_SPEC, *([HBM_SPEC] * (2 * n)), pl.BlockSpec(memory_space=pltpu.VMEM)),
              input_output_aliases={i: 2 + i for i in range(2 * n)},
              compiler_params=pltpu.CompilerParams(has_side_effects=SPLIT_EFFECT))(
                  *[_in_hbm(w) for w in ws], *[_in_hbm(z) for z in lands], after)
    return out[0], out[1], out[2:2 + n], out[2 + n:2 + 2 * n], out[-1]


def _gather_wait(s_sem, r_sem, ws, lands, after, layer):
    n = len(ws)

    def body(*refs):
        ins, lz = refs[:n], refs[n:2 * n]
        s_ref, r_ref = refs[2 * n], refs[2 * n + 1]
        x, y, c, chips = _place()
        for t in range(n):
            for k in range(3):
                cp = pltpu.make_async_remote_copy(src_ref=ins[t], dst_ref=lz[t].at[2 * x + y], send_sem=s_ref.at[3 * t + k],
                                                  recv_sem=r_ref.at[3 * t + k], device_id=(*chips[k], c),
                                                  device_id_type=MESH)
                cp.wait_send()
                cp.wait_recv()

    hbm = [pltpu.HBM(w.shape, w.dtype) for w in ws] + [pltpu.HBM(z.shape, z.dtype) for z in lands]
    out = _pc(body, name=f"gather_wait_l{layer}", out_shape=tuple(hbm),
              in_specs=[HBM_SPEC] * (2 * n) + [SEM_SPEC, SEM_SPEC, ANY_SPEC], out_specs=tuple([HBM_SPEC] * (2 * n)),
              input_output_aliases={i: i for i in range(2 * n)},
              compiler_params=pltpu.CompilerParams(has_side_effects=SPLIT_EFFECT))(*ws, *lands, s_sem, r_sem, after)
    return out[n:]


def _swap_halves(gs, tag):
    n = len(gs)

    def body(*refs):
        ins, outs = refs[:n], refs[n:2 * n]
        s_sem, r_sem = refs[2 * n:]
        x, y, c, _ = _place()
        cps = []
        for t in range(n):
            r2 = gs[t].shape[2] // 2
            cps.append(pltpu.make_async_remote_copy(
                src_ref=ins[t].at[:, :, pl.ds((1 - c) * r2, r2), :], dst_ref=outs[t], send_sem=s_sem.at[t],
                recv_sem=r_sem.at[t], device_id=(x, y, 1 - c), device_id_type=MESH))
            cps[-1].start()
        for cp in cps:
            cp.wait_recv()
        for cp in cps:
            cp.wait_send()

    return _pc(body, name=f"grad_swap_halves_{tag}", in_specs=[HBM_SPEC] * n, out_specs=[HBM_SPEC] * n,
               out_shape=[S(g.shape[:2] + (g.shape[2] // 2, g.shape[3]), g.dtype) for g in gs],
               scratch_shapes=[pltpu.SemaphoreType.DMA((n,))] * 2, compiler_params=_comm_params())(*gs)


def _add_half(g, landed, c_arr, name):
    _, L, r2, cols = landed.shape

    def body(c_ref, g_ref, l_ref, o_ref):
        o_ref[...] = (g_ref[...].astype(F32) + l_ref[...].astype(F32)).astype(BF16)

    gs = pltpu.PrefetchScalarGridSpec(
        num_scalar_prefetch=1, grid=(NSH, L),
        in_specs=[pl.BlockSpec((None, None, r2, cols), lambda j, l, c: (j, l, c[0], 0)),
                  pl.BlockSpec((None, None, r2, cols), lambda j, l, c: (j, l, 0, 0))],
        out_specs=pl.BlockSpec((None, None, r2, cols), lambda j, l, c: (j, l, 0, 0)))
    return _pc(body, name=name, grid_spec=gs, out_shape=S(landed.shape, BF16), compiler_params=_cp(2))(c_arr, g, landed)


def _send_shards(ps):
    n = len(ps)

    def body(*refs):
        ins, outs = refs[:n], refs[n:2 * n]
        s_sem, r_sem = refs[2 * n:]
        x, y, c, chips = _place()
        cps = []
        for t in range(n):
            for k in range(3):
                cps.append(pltpu.make_async_remote_copy(
                    src_ref=ins[t].at[2 * chips[k][0] + chips[k][1]], dst_ref=outs[t].at[k],
                    send_sem=s_sem.at[3 * t + k], recv_sem=r_sem.at[3 * t + k], device_id=(*chips[k], c),
                    device_id_type=MESH))
                cps[-1].start()
        for cp in cps:
            cp.wait_recv()
        for cp in cps:
            cp.wait_send()

    return _pc(body, name="grad_send_shards", in_specs=[HBM_SPEC] * n, out_specs=[HBM_SPEC] * n,
               out_shape=[S((3,) + p.shape[1:], p.dtype) for p in ps],
               scratch_shapes=[pltpu.SemaphoreType.DMA((3 * n,))] * 2, compiler_params=_comm_params())(*ps)


def _send_start(ps, lands):
    n = len(ps)

    def body(*refs):
        ins, lz = refs[:n], refs[n:2 * n]
        s_sem, r_sem = refs[2 * n], refs[2 * n + 1]
        token = refs[-1]
        x, y, c, chips = _place()
        for t in range(n):
            for k in range(3):
                pltpu.make_async_remote_copy(src_ref=ins[t].at[2 * chips[k][0] + chips[k][1]], dst_ref=lz[t].at[k],
                                             send_sem=s_sem.at[3 * t + k], recv_sem=r_sem.at[3 * t + k],
                                             device_id=(*chips[k], c), device_id_type=MESH).start()
        token[...] = jnp.zeros_like(token)

    hbm = [pltpu.HBM(p.shape, p.dtype) for p in ps] + [pltpu.HBM(z.shape, z.dtype) for z in lands]
    out = _pc(body, name="grad_send_start",
              out_shape=(pltpu.SemaphoreType.DMA((3 * n,)), pltpu.SemaphoreType.DMA((3 * n,)), *hbm, S((8, 128), F32)),
              in_specs=[HBM_SPEC] * (2 * n),
              out_specs=(SEM_SPEC, SEM_SPEC, *([HBM_SPEC] * (2 * n)), pl.BlockSpec(memory_space=pltpu.VMEM)),
              input_output_aliases={i: 2 + i for i in range(2 * n)},
              compiler_params=pltpu.CompilerParams(has_side_effects=SPLIT_EFFECT))(
                  *[_in_hbm(p) for p in ps], *[_in_hbm(z) for z in lands])
    return out[0], out[1], out[2:2 + n], out[2 + n:2 + 2 * n], out[-1]


def _send_wait(s_sem, r_sem, ps, lands, after):
    n = len(ps)

    def body(*refs):
        ins, lz = refs[:n], refs[n:2 * n]
        s_ref, r_ref = refs[2 * n], refs[2 * n + 1]
        x, y, c, chips = _place()
        for t in range(n):
            for k in range(3):
                cp = pltpu.make_async_remote_copy(src_ref=ins[t].at[2 * chips[k][0] + chips[k][1]], dst_ref=lz[t].at[k],
                                                  send_sem=s_ref.at[3 * t + k], recv_sem=r_ref.at[3 * t + k],
                                                  device_id=(*chips[k], c), device_id_type=MESH)
                cp.wait_send()
                cp.wait_recv()

    hbm = [pltpu.HBM(p.shape, p.dtype) for p in ps] + [pltpu.HBM(z.shape, z.dtype) for z in lands]
    out = _pc(body, name="grad_send_wait", out_shape=tuple(hbm),
              in_specs=[HBM_SPEC] * (2 * n) + [SEM_SPEC, SEM_SPEC, ANY_SPEC], out_specs=tuple([HBM_SPEC] * (2 * n)),
              input_output_aliases={i: i for i in range(2 * n)},
              compiler_params=pltpu.CompilerParams(has_side_effects=SPLIT_EFFECT))(*ps, *lands, s_sem, r_sem, after)
    return out[:n], out[n:]


def _sum_shards(part, landed, me_arr, c_arr, buf, first_layer, name):
    _, nl, r2, cols = landed.shape

    def body(me_ref, c_ref, p_ref, l_ref, b_ref, o_ref):
        o_ref[...] = ((p_ref[...].astype(F32) + l_ref[0].astype(F32)) + l_ref[1].astype(F32)) + l_ref[2].astype(F32)

    gs = pltpu.PrefetchScalarGridSpec(
        num_scalar_prefetch=2, grid=(nl,),
        in_specs=[pl.BlockSpec((None, None, r2, cols), lambda l, me, c: (me[0], l, 0, 0)),
                  pl.BlockSpec((3, None, r2, cols), lambda l, me, c: (0, l, 0, 0)), ANY_SPEC],
        out_specs=pl.BlockSpec((None, r2, cols), lambda l, me, c: (first_layer + l, c[0], 0)))
    return _pc(body, name=name, grid_spec=gs, out_shape=S(buf.shape, F32), input_output_aliases={4: 0},
               compiler_params=_cp(1))(me_arr, c_arr, part, landed, buf)


def _share_halves(bufs):
    n = len(bufs)

    def body(*refs):
        ins, outs = refs[:n], refs[n:2 * n]
        s_sem, r_sem = refs[2 * n:]
        x, y, c, _ = _place()
        cps = []
        for t in range(n):
            r2 = bufs[t].shape[1] // 2
            cps.append(pltpu.make_async_remote_copy(
                src_ref=ins[t].at[:, pl.ds(c * r2, r2), :], dst_ref=outs[t].at[:, pl.ds(c * r2, r2), :],
                send_sem=s_sem.at[t], recv_sem=r_sem.at[t], device_id=(x, y, 1 - c), device_id_type=MESH))
            cps[-1].start()
        for cp in cps:
            cp.wait_recv()
        for cp in cps:
            cp.wait_send()

    return _pc(body, name="grad_share_halves", in_specs=[HBM_SPEC] * n, out_specs=[HBM_SPEC] * n,
               out_shape=[S(b.shape, b.dtype) for b in bufs], input_output_aliases={t: t for t in range(n)},
               scratch_shapes=[pltpu.SemaphoreType.DMA((n,))] * 2, compiler_params=_comm_params())(*bufs)


def _gather_small(v):
    nr = v.shape[0]

    def body(v_ref, out_ref, send_sems, recv_sems, local_sem):
        x, y, c, chips = _place()
        me, sibling = (x, y, c), (x, y, 1 - c)

        def rows(px, py, pc):
            return out_ref.at[pl.ds((4 * px + 2 * py + pc) * nr, nr), :]

        def copy(k, block, to, src=None):
            return pltpu.make_async_remote_copy(src_ref=rows(*block) if src is None else src, dst_ref=rows(*block),
                                                send_sem=send_sems.at[k], recv_sem=recv_sems.at[k], device_id=to,
                                                device_id_type=MESH)

        mine = pltpu.make_async_copy(v_ref, rows(*me), local_sem)
        mine.start()
        first = [copy(0, me, sibling, src=v_ref)]
        first += [copy(1 + j, me, (*chip, c), src=v_ref) for j, chip in enumerate(chips)]
        for cp in first:
            cp.start()
        passed = [copy(4 + j, (*chip, c), sibling) for j, chip in enumerate(chips)]
        for j, chip in enumerate(chips):
            copy(1 + j, (*chip, c), me).wait_recv()
            passed[j].start()
        copy(0, sibling, me).wait_recv()
        for j, chip in enumerate(chips):
            copy(4 + j, (*chip, 1 - c), me).wait_recv()
        for cp in first + passed:
            cp.wait_send()
        mine.wait()

    vm = pl.BlockSpec(memory_space=pltpu.VMEM)
    return _pc(body, name="gather_small_grads", in_specs=[vm], out_specs=vm, out_shape=S((8 * nr, 128), F32),
               scratch_shapes=[pltpu.SemaphoreType.DMA((7,)), pltpu.SemaphoreType.DMA((7,)), pltpu.SemaphoreType.DMA],
               compiler_params=_comm_params())(v)


def _adamw_math(w, g, m, v):
    m2 = ADAM_B1 * m + (1.0 - ADAM_B1) * g
    v2 = ADAM_B2 * v + (1.0 - ADAM_B2) * (g * g)
    m_hat = m2 / (1.0 - ADAM_B1 ** ADAM_STEP)
    v_hat = v2 / (1.0 - ADAM_B2 ** ADAM_STEP)
    return -ADAM_LR * (m_hat / (jnp.sqrt(v_hat) + ADAM_EPS) + ADAM_WD * w), m2, v2


def _adamw(w, g, m, v, name):
    L, R, C = w.shape
    rb = R // 2 if R >= 512 else R

    def body(w_ref, g_ref, m_ref, v_ref, d_ref, m2_ref, v2_ref):
        d_ref[...], m2_ref[...], v2_ref[...] = _adamw_math(w_ref[...], g_ref[...], m_ref[...], v_ref[...])

    blk = pl.BlockSpec((None, rb, C), lambda l, r: (l, r, 0))
    return _pc(body, name=name, grid=(L, R // rb), in_specs=[blk] * 4, out_specs=[blk] * 3,
               out_shape=[S(w.shape, F32)] * 3, compiler_params=_cp(2))(w, g, m, v)


def _adamw_small(gathered, w, m, v):
    nr = w.shape[0]
    rb = nr // 5

    def body(a_ref, w_ref, m_ref, v_ref, g_ref, d_ref, m2_ref, v2_ref):
        g = a_ref[0]
        for k in range(1, 8):
            g = g + a_ref[k]
        g_ref[...] = g
        d_ref[...], m2_ref[...], v2_ref[...] = _adamw_math(w_ref[...], g, m_ref[...], v_ref[...])

    blk = pl.BlockSpec((rb, 128), lambda i: (i, 0))
    return _pc(body, name="adamw_small", grid=(nr // rb,), in_specs=[pl.BlockSpec((8, rb, 128), lambda i: (0, i, 0))] + [blk] * 3,
               out_specs=[blk] * 4, out_shape=[S((nr, 128), F32)] * 4, compiler_params=_cp(1))(gathered, w, m, v)


SMALL_ROWS = 4520


def _pack(arrs):
    flat = jnp.concatenate([a.reshape(-1) for a in arrs])
    return jnp.pad(flat, (0, SMALL_ROWS * 128 - flat.shape[0])).reshape(SMALL_ROWS, 128)


def _unpack(packed, like):
    flat = packed.reshape(-1)
    out, off = [], 0
    for a in like:
        out.append(flat[off:off + a.size].reshape(a.shape))
        off += a.size
    return out


def kernel(x, p, positions, ffn1_pre_g, ffn1_w_gate, ffn1_w_up, ffn1_w_down, ffn1_post_g, mix_pre_g, w_in, attn_norm_g, ssm_lam_re, ssm_lam_im, ssm_log_dt, ssm_b_re, ssm_b_im, ssm_c_re, ssm_c_im, ssm_d, ssm_w_glu, ssm_b_glu, ssm_norm_g, w_out, mix_post_g, ffn2_pre_g, ffn2_w_gate, ffn2_w_up, ffn2_w_down, ffn2_post_g, ple_w_up, ple_w_gate, ple_post_g, loss_target, m_ffn1_pre_g, m_ffn1_w_gate, m_ffn1_w_up, m_ffn1_w_down, m_ffn1_post_g, m_mix_pre_g, m_w_in, m_attn_norm_g, m_ssm_lam_re, m_ssm_lam_im, m_ssm_log_dt, m_ssm_b_re, m_ssm_b_im, m_ssm_c_re, m_ssm_c_im, m_ssm_d, m_ssm_w_glu, m_ssm_b_glu, m_ssm_norm_g, m_w_out, m_mix_post_g, m_ffn2_pre_g, m_ffn2_w_gate, m_ffn2_w_up, m_ffn2_w_down, m_ffn2_post_g, m_ple_w_up, m_ple_w_gate, m_ple_post_g, v_ffn1_pre_g, v_ffn1_w_gate, v_ffn1_w_up, v_ffn1_w_down, v_ffn1_post_g, v_mix_pre_g, v_w_in, v_attn_norm_g, v_ssm_lam_re, v_ssm_lam_im, v_ssm_log_dt, v_ssm_b_re, v_ssm_b_im, v_ssm_c_re, v_ssm_c_im, v_ssm_d, v_ssm_w_glu, v_ssm_b_glu, v_ssm_norm_g, v_w_out, v_mix_post_g, v_ffn2_pre_g, v_ffn2_w_gate, v_ffn2_w_up, v_ffn2_w_down, v_ffn2_post_g, v_ple_w_up, v_ple_w_gate, v_ple_post_g):
    a = dict(locals())
    T = x.shape[1]
    big_names = [n for n, _, _ in BIG]
    for n in TRANSPOSED:
        for pre in ("", "m_", "v_"):
            a[pre + n] = jnp.swapaxes(a[pre + n], 1, 2)

    own = [a[n].astype(BF16) for n in big_names]
    n_layers = own[0].shape[0]
    per_layer = [[w[l:l + 1] for w in own] for l in range(n_layers)]
    c_arr = lax.axis_index("c").astype(jnp.int32).reshape(1)
    me_arr = (2 * lax.axis_index("x") + lax.axis_index("y")).astype(jnp.int32).reshape(1)
    first = dict(zip(big_names, _gather_weights(per_layer[0], _place_own(per_layer[0], me_arr, 0))))
    pending, anchor, queued_behind = {}, jnp.zeros((), F32), first[big_names[0]]
    for l in range(1, n_layers):
        s_sem, r_sem, ws_thru, lands_thru, token = _gather_start(per_layer[l], _place_own(per_layer[l], me_arr, l),
                                                                 queued_behind, l)
        pending[l] = (s_sem, r_sem, ws_thru, lands_thru)
        anchor = anchor + token[0, 0]
        queued_behind = token

    def weights_of(l, after):
        if l == 0:
            return first
        return dict(zip(big_names, _gather_wait(*pending[l], after, l)))

    Sm = {n: a[n] for n in SMALL}
    Sm["ffn1_pre_g"] = Sm["ffn1_pre_g"] + anchor

    pos = jnp.broadcast_to(positions.reshape(1, T, 1).astype(F32), (1, T, 128))
    def chip_partials(G, tag):
        gs = [G[n] for n in big_names]
        landed = _swap_halves(gs, tag)
        return [_add_half(g, la, c_arr, f"grad_add_half_{tag}_{n}") for g, la, n in zip(gs, landed, big_names)]

    upper = {}

    def upper_grads_done(G_upper):
        parts = chip_partials(G_upper, "upper")
        lands = [lax.empty((3,) + pt.shape[1:], BF16) for pt in parts]
        s_sem, r_sem, parts_thru, lands_thru, token = _send_start(parts, lands)
        upper["pending"] = (s_sem, r_sem, parts_thru, lands_thru)
        return token[0, 0]

    loss, gx, G_upper, G_first, small = _local_step(
        _reorder(x, True, "to_streams_x")[0], _reorder(p[:, 0], True, "to_streams_p"),
        _reorder(pos, True, "to_streams_pos")[0, :, :1], _reorder(loss_target, True, "to_streams_target")[0],
        weights_of, upper_grads_done, Sm)
    gx = _reorder(gx[None], False, "to_time_grad_x")

    bufs = [lax.empty((n_layers, r, c), F32) for _, r, c in BIG]
    if n_layers > 1:
        parts, landed = _send_wait(*upper["pending"], gx)
        bufs = [_sum_shards(pt, la, me_arr, c_arr, b, 1, f"grad_sum_shards_upper_{n}")
                for pt, la, b, n in zip(parts, landed, bufs, big_names)]
    parts = chip_partials(G_first, "first")
    landed = _send_shards(parts)
    bufs = [_sum_shards(pt, la, me_arr, c_arr, b, 0, f"grad_sum_shards_first_{n}")
            for pt, la, b, n in zip(parts, landed, bufs, big_names)]
    grads = dict(zip(big_names, _share_halves(bufs)))

    small_g = _gather_small(_pack([small[n] for n in SMALL])).reshape(8, SMALL_ROWS, 128)
    sg, sd, sm, sv = _adamw_small(small_g, _pack([a[n] for n in SMALL]), _pack([a["m_" + n] for n in SMALL]),
                                  _pack([a["v_" + n] for n in SMALL]))
    like = [a[n] for n in SMALL]
    res = {}
    for n, g_, d_, m_, v_ in zip(SMALL, _unpack(sg, like), _unpack(sd, like), _unpack(sm, like), _unpack(sv, like)):
        res[n] = (g_, d_, m_, v_)
    for n in big_names:
        d_, m_, v_ = _adamw(a[n], grads[n], a["m_" + n], a["v_" + n], f"adamw_{n}")
        res[n] = (grads[n], d_, m_, v_)
        if n in TRANSPOSED:
            res[n] = tuple(jnp.swapaxes(t, 1, 2) for t in res[n])

    total = lax.psum(loss[0, 0], ("x", "y", "c"))
    return (total, gx, *[res[n][0] for n in WEIGHTS], *[res[n][1] for n in WEIGHTS],
            *[res[n][2] for n in WEIGHTS], *[res[n][3] for n in WEIGHTS])
```

```python
import functools
import math

import numpy as np
import jax
import jax.numpy as jnp
from jax import lax
from jax.experimental import pallas as pl
from jax.experimental.pallas import tpu as pltpu

F32 = jnp.float32
BF16 = jnp.bfloat16
S = jax.ShapeDtypeStruct
MESH = pl.DeviceIdType.MESH

D = 1024
DA = 512
DSS = 512
HD = 64
NH = 8
BAND = 128
NSH = 4
DFS = 704
PLE = 256
EPS = 1e-6
ROPE_THETA = 500000.0
PATTERN_DILATIONS = (1, 4, 16)
NLB = 16
ADAM_LR, ADAM_B1, ADAM_B2, ADAM_EPS, ADAM_WD, ADAM_STEP = 0.001, 0.9, 0.999, 1e-08, 0.01, 10

VMEM_LIMIT = 56 * 1024 * 1024
TM = 512
TMB = 256

BIG = (
    ("ffn1_w_gate", DFS, D), ("ffn1_w_up", DFS, D), ("ffn1_w_down", DFS, D),
    ("w_in", D, 512), ("ssm_w_glu", 128, 512), ("w_out", 256, D),
    ("ffn2_w_gate", DFS, D), ("ffn2_w_up", DFS, D), ("ffn2_w_down", DFS, D),
    ("ple_w_up", PLE, 256), ("ple_w_gate", 256, D),
)
TRANSPOSED = ("ffn1_w_gate", "ffn1_w_up", "ffn2_w_gate", "ffn2_w_up")
SMALL = ("ffn1_pre_g", "ffn1_post_g", "mix_pre_g", "attn_norm_g", "ssm_lam_re", "ssm_lam_im", "ssm_log_dt",
         "ssm_b_re", "ssm_b_im", "ssm_c_re", "ssm_c_im", "ssm_d", "ssm_b_glu", "ssm_norm_g", "mix_post_g",
         "ffn2_pre_g", "ffn2_post_g", "ple_post_g")
WEIGHTS = ("ffn1_pre_g", "ffn1_w_gate", "ffn1_w_up", "ffn1_w_down", "ffn1_post_g", "mix_pre_g", "w_in", "attn_norm_g",
           "ssm_lam_re", "ssm_lam_im", "ssm_log_dt", "ssm_b_re", "ssm_b_im", "ssm_c_re", "ssm_c_im", "ssm_d",
           "ssm_w_glu", "ssm_b_glu", "ssm_norm_g", "w_out", "mix_post_g", "ffn2_pre_g", "ffn2_w_gate", "ffn2_w_up",
           "ffn2_w_down", "ffn2_post_g", "ple_w_up", "ple_w_gate", "ple_post_g")


def _pc(body, **kw):
    return pl.pallas_call(body, **kw)


def _cp(n_grid):
    return pltpu.CompilerParams(dimension_semantics=("arbitrary",) * n_grid, vmem_limit_bytes=VMEM_LIMIT)


def _dot(a, b):
    return jnp.dot(a, b, preferred_element_type=F32)


def _dot_nt(a, b):
    return lax.dot_general(a, b, (((1,), (1,)), ((), ())), preferred_element_type=F32)


def _dot_tn(a, b):
    return lax.dot_general(a, b, (((0,), (0,)), ((), ())), preferred_element_type=F32)


def _split(a):
    hi = a.astype(BF16)
    return hi, (a - hi.astype(F32)).astype(BF16)


def _dot3(fn, a, b):
    ah, al = _split(a)
    bh, bl = _split(b)
    return fn(ah, bh) + fn(ah, bl) + fn(al, bh)


def _rms_fwd(x, g):
    r = lax.rsqrt(jnp.mean(x * x, axis=-1, keepdims=True) + EPS)
    return x * r * g


def _rms_bwd(dy, x, g):
    r = lax.rsqrt(jnp.mean(x * x, axis=-1, keepdims=True) + EPS)
    xr = x * r
    gd = dy * g
    dx = r * (gd - xr * jnp.mean(gd * xr, axis=-1, keepdims=True))
    dg = jnp.sum(dy * xr, axis=0, keepdims=True)
    return dx, dg


def _gelu(y):
    k = math.sqrt(2.0 / math.pi)
    return 0.5 * y * (1.0 + jnp.tanh(k * (y + 0.044715 * y * y * y)))


def _gelu_grad(y):
    k = math.sqrt(2.0 / math.pi)
    t = jnp.tanh(k * (y + 0.044715 * y * y * y))
    return 0.5 * (1.0 + t) + 0.5 * y * (1.0 - t * t) * k * (1.0 + 3 * 0.044715 * y * y)


def _gain_spec(n, layer):
    return pl.BlockSpec((None, 1, n), lambda *_: (layer, 0, 0))


def _row_acc_spec(n):
    return pl.BlockSpec((1, n), lambda *_: (0, 0))


def _rot_tables(pos_col):
    T = pos_col.shape[0]
    half = HD // 8
    inv = (ROPE_THETA ** (-np.arange(half, dtype=np.float32) * (2.0 / (2 * half)))).astype(np.float32)
    lane_freq = np.tile(np.concatenate([inv, inv, np.zeros(HD - 2 * half, np.float32)]), NH)[None, :]

    def body(p_ref, f_ref, c_ref, s1_ref, s2_ref):
        ang = p_ref[...] * f_ref[...]
        d = lax.broadcasted_iota(jnp.int32, ang.shape, 1) % HD
        cs = jnp.cos(ang)
        sn = jnp.sin(ang)
        c_ref[...] = jnp.where(d < 2 * half, cs, 1.0)
        s1_ref[...] = jnp.where(d < half, -sn, 0.0)
        s2_ref[...] = jnp.where((d >= half) & (d < 2 * half), sn, 0.0)

    tm = TM
    return _pc(body, name="rot_tables", grid=(T // tm,),
               in_specs=[pl.BlockSpec((tm, 1), lambda i: (i, 0)), pl.BlockSpec((1, DA), lambda i: (0, 0))],
               out_specs=[pl.BlockSpec((tm, DA), lambda i: (i, 0))] * 3,
               out_shape=[S((T, DA), F32)] * 3, compiler_params=_cp(1))(pos_col, jnp.asarray(lane_freq))


def _rot_fwd(t, c, s1, s2):
    return t * c + pltpu.roll(t, DA - 8, 1) * s1 + pltpu.roll(t, 8, 1) * s2


def _rot_bwd(g, c, s1, s2):
    return g * c + pltpu.roll(g * s1, 8, 1) + pltpu.roll(g * s2, DA - 8, 1)


def _ffn_weight_spec():
    return pl.BlockSpec((NSH, None, DFS, D), lambda i: (0, 0, 0, 0), pipeline_mode=pl.Buffered(1))


def _ffn_fwd(h, pre_g, post_g, wg, wu, wd, layer, tag):
    T = h.shape[0]
    tm = TM
    nt = T // tm

    def body(h_ref, pg_ref, qg_ref, wg_ref, wu_ref, wd_ref, ho_ref, a_ref, b_ref, f_ref, xn_ref):
        hv = h_ref[...]
        xb = _rms_fwd(hv, pg_ref[...]).astype(BF16)
        xn_ref[...] = xb
        f = None
        for j in range(NSH):
            ab = _dot_nt(xb, wg_ref[j]).astype(BF16)
            bb = _dot_nt(xb, wu_ref[j]).astype(BF16)
            a_ref[j] = ab
            b_ref[j] = bb
            a = ab.astype(F32)
            hh = (a * jax.nn.sigmoid(a) * bb.astype(F32)).astype(BF16)
            part = _dot(hh, wd_ref[j])
            f = part if f is None else f + part
        f_ref[...] = f
        ho_ref[...] = hv + 0.5 * _rms_fwd(f, qg_ref[...])

    row = pl.BlockSpec((tm, D), lambda i: (i, 0))
    act = pl.BlockSpec((NSH, tm, DFS), lambda i: (0, i, 0))
    return _pc(body, name=f"ffn_fwd_{tag}_l{layer}", grid=(nt,),
               in_specs=[row, _gain_spec(D, layer), _gain_spec(D, layer)] + [_ffn_weight_spec()] * 3,
               out_specs=[row, act, act, row, row],
               out_shape=[S((T, D), F32), S((NSH, T, DFS), BF16), S((NSH, T, DFS), BF16), S((T, D), F32), S((T, D), BF16)],
               compiler_params=_cp(1))(h, pre_g, post_g, wg, wu, wd)


def _ffn_bwd(dout, h, f, a, b, pre_g, post_g, wg, wu, wd, layer, tag):
    T = h.shape[0]
    tm = TMB
    nt = T // tm

    def body(do_ref, h_ref, f_ref, a_ref, b_ref, pg_ref, qg_ref, wg_ref, wu_ref, wd_ref,
             dh_ref, df_ref, da_ref, db_ref, hh_ref, dpg_ref, dqg_ref):
        @pl.when(pl.program_id(0) == 0)
        def _():
            dpg_ref[...] = jnp.zeros_like(dpg_ref)
            dqg_ref[...] = jnp.zeros_like(dqg_ref)

        do = do_ref[...]
        df, dq = _rms_bwd(0.5 * do, f_ref[...], qg_ref[...])
        dqg_ref[...] += dq
        dfb = df.astype(BF16)
        df_ref[...] = dfb
        dxn = None
        for j in range(NSH):
            dhh = _dot_nt(dfb, wd_ref[j])
            av = a_ref[j].astype(F32)
            bv = b_ref[j].astype(F32)
            sg = jax.nn.sigmoid(av)
            sa = av * sg
            hh_ref[j] = (sa * bv).astype(BF16)
            dab = (dhh * bv * (sg + sa * (1.0 - sg))).astype(BF16)
            dbb = (dhh * sa).astype(BF16)
            da_ref[j] = dab
            db_ref[j] = dbb
            part = _dot(dab, wg_ref[j]) + _dot(dbb, wu_ref[j])
            dxn = part if dxn is None else dxn + part
        dx, dp = _rms_bwd(dxn, h_ref[...], pg_ref[...])
        dpg_ref[...] += dp
        dh_ref[...] = do + dx

    row = pl.BlockSpec((tm, D), lambda i: (i, 0))
    act = pl.BlockSpec((NSH, tm, DFS), lambda i: (0, i, 0))
    return _pc(body, name=f"ffn_bwd_{tag}_l{layer}", grid=(nt,),
               in_specs=[row, row, row, act, act, _gain_spec(D, layer), _gain_spec(D, layer)] + [_ffn_weight_spec()] * 3,
               out_specs=[row, row, act, act, act, _row_acc_spec(D), _row_acc_spec(D)],
               out_shape=[S((T, D), F32), S((T, D), BF16), S((NSH, T, DFS), BF16), S((NSH, T, DFS), BF16),
                          S((NSH, T, DFS), BF16), S((1, D), F32), S((1, D), F32)],
               compiler_params=_cp(1))(dout, h, f, a, b, pre_g, post_g, wg, wu, wd)


def _dw(A, B, buf, layer, kb, nb, a_mode, b_mode, name):
    T = A.shape[1]
    tt = TM
    nt = T // tt

    def pick(v, mode, j, w):
        if mode == "shard":
            return v[j]
        return v[0] if mode == "whole" else v[0][:, j * w:(j + 1) * w]

    def body(a_ref, b_ref, buf_ref, o_ref, acc):
        t = pl.program_id(0)

        @pl.when(t == 0)
        def _():
            acc[...] = jnp.zeros_like(acc)

        av = a_ref[...].astype(BF16)
        bv = b_ref[...].astype(BF16)
        for j in range(NSH):
            acc[j] += _dot_tn(pick(av, a_mode, j, kb), pick(bv, b_mode, j, nb))

        @pl.when(t == nt - 1)
        def _():
            o_ref[...] = acc[...].astype(o_ref.dtype)

    return _pc(body, name=name, grid=(nt,),
               in_specs=[pl.BlockSpec((A.shape[0], tt, A.shape[2]), lambda t: (0, t, 0)),
                         pl.BlockSpec((B.shape[0], tt, B.shape[2]), lambda t: (0, t, 0)),
                         pl.BlockSpec(memory_space=pl.ANY)],
               out_specs=pl.BlockSpec((NSH, None, kb, nb), lambda t: (0, layer, 0, 0)),
               out_shape=S(buf.shape, buf.dtype), input_output_aliases={2: 0},
               scratch_shapes=[pltpu.VMEM((NSH, kb, nb), F32)], compiler_params=_cp(1))(A, B, buf)


def _mix_proj(h, pre_g, win, rot, layer):
    T = h.shape[0]
    tm = TM

    def body(h_ref, g_ref, w_ref, c_ref, s1_ref, s2_ref, p_ref, xn_ref):
        xb = _rms_fwd(h_ref[...], g_ref[...]).astype(BF16)
        xn_ref[...] = xb
        for j in range(NSH):
            o = _dot(xb, w_ref[j])
            p_ref[j] = _rot_fwd(o, c_ref[...], s1_ref[...], s2_ref[...]) if j < 2 else o

    row = pl.BlockSpec((tm, D), lambda i: (i, 0))
    half = pl.BlockSpec((tm, DA), lambda i: (i, 0))
    return _pc(body, name=f"mix_proj_l{layer}", grid=(T // tm,),
               in_specs=[row, _gain_spec(D, layer), pl.BlockSpec((NSH, None, D, DA), lambda i: (0, 0, 0, 0)),
                         half, half, half],
               out_specs=[pl.BlockSpec((NSH, tm, DA), lambda i: (0, i, 0)), row],
               out_shape=[S((NSH, T, DA), F32), S((T, D), BF16)], compiler_params=_cp(1))(h, pre_g, win, *rot)


def _mix_proj_bwd(dq, dk, dv, du, dh_up, h, pre_g, win, rot, layer):
    T = h.shape[0]
    tm = TM

    def body(dq_ref, dk_ref, dv_ref, du_ref, up_ref, h_ref, g_ref, w_ref, c_ref, s1_ref, s2_ref,
             dh_ref, dp_ref, dg_ref):
        @pl.when(pl.program_id(0) == 0)
        def _():
            dg_ref[...] = jnp.zeros_like(dg_ref)

        rot = (c_ref[...], s1_ref[...], s2_ref[...])
        dps = [_rot_bwd(dq_ref[...], *rot), _rot_bwd(dk_ref[...], *rot), dv_ref[...], du_ref[...]]
        dxn = None
        for j in range(NSH):
            dpb = dps[j].astype(BF16)
            dp_ref[j] = dpb
            part = _dot_nt(dpb, w_ref[j])
            dxn = part if dxn is None else dxn + part
        dx, dg = _rms_bwd(dxn, h_ref[...], g_ref[...])
        dg_ref[...] += dg
        dh_ref[...] = up_ref[...] + dx

    row = pl.BlockSpec((tm, D), lambda i: (i, 0))
    half = pl.BlockSpec((tm, DA), lambda i: (i, 0))
    return _pc(body, name=f"mix_proj_bwd_l{layer}", grid=(T // tm,),
               in_specs=[half, half, half, half, row, row, _gain_spec(D, layer),
                         pl.BlockSpec((NSH, None, D, DA), lambda i: (0, 0, 0, 0)), half, half, half],
               out_specs=[row, pl.BlockSpec((NSH, tm, DA), lambda i: (0, i, 0)), _row_acc_spec(D)],
               out_shape=[S((T, D), F32), S((NSH, T, DA), BF16), S((1, D), F32)],
               compiler_params=_cp(1))(dq, dk, dv, du, dh_up, h, pre_g, win, *rot)


def _stream_pos(d, axis):
    i = lax.broadcasted_iota(jnp.int32, (BAND, BAND), axis)
    if d == 16:
        return i
    if d == 4:
        return 4 * (i % 32) + i // 32
    return 16 * (i % 8) + i // 8


def _band_masks(b, d):
    qi, kj = _stream_pos(d, 0), _stream_pos(d, 1)
    return kj <= qi, (kj >= qi) & (b > 0)


def _pattern(d, T):
    n16 = T // 16
    if d == 16:
        return (16, n16, DA), (None, BAND, DA), lambda r, k: (r, k, 0)
    if d == 4:
        return (4, 4, n16, DA), (4, None, 32, DA), lambda r, k: (0, r, k, 0)
    return (16, n16, DA), (16, 8, DA), lambda r, k: (0, k, 0)


def _pattern_spec(d, T, kmap, lead=None):
    _, blk, idx = _pattern(d, T)
    if lead is None:
        return pl.BlockSpec(blk, lambda r, b: idx(r, kmap(b)))
    return pl.BlockSpec((None,) + blk, lambda r, b: (lead,) + idx(r, kmap(b)))


def _whole_stream_specs(T, n_plain):
    n16 = T // 16
    p_spec = lambda s: pl.BlockSpec((None, None, n16, DA), lambda r: (s, r, 0, 0))
    plain = pl.BlockSpec((None, n16, DA), lambda r: (r, 0, 0))
    return [p_spec(0), p_spec(1), p_spec(2)] + [plain] * n_plain, plain


def _stream_masks():
    qi = lax.broadcasted_iota(jnp.int32, (BAND, BAND), 0)
    kj = lax.broadcasted_iota(jnp.int32, (BAND, BAND), 1)
    mask_c = kj <= qi
    return mask_c, jnp.concatenate([kj >= qi, mask_c], axis=1)


def _attn_fwd_stream(P, layer):
    T = P.shape[1]
    n16 = T // 16
    nb = n16 // BAND
    scale = HD ** -0.5

    def body(q_ref, k_ref, v_ref, o_ref, l_ref, qs, ks, vs):
        for src, dst in ((q_ref, qs), (k_ref, ks), (v_ref, vs)):
            dst[...] = src[...].astype(BF16)
        mask_c, mask_pc = _stream_masks()
        for b in range(nb):
            rows = slice(b * BAND, (b + 1) * BAND)
            krows = slice(max(b - 1, 0) * BAND, (b + 1) * BAND)
            mask = mask_c if b == 0 else mask_pc
            for hd in range(NH):
                sl = slice(hd * HD, (hd + 1) * HD)
                s = jnp.where(mask, _dot_nt(qs[rows, sl], ks[krows, sl]) * scale, -1e30)
                m = jnp.max(s, axis=-1, keepdims=True)
                e = jnp.exp(s - m)
                den = jnp.sum(e, axis=-1, keepdims=True)
                o_ref[rows, sl] = _dot(e.astype(BF16), vs[krows, sl]) / den
                l_ref[rows, sl] = jnp.broadcast_to(m + jnp.log(den), (BAND, HD))

    ins, out = _whole_stream_specs(T, 0)
    Pv = P.reshape(NSH, 16, n16, DA)
    o, l = _pc(body, name=f"attn_fwd_d16_l{layer}", grid=(16,), in_specs=ins, out_specs=[out, out],
               out_shape=[S((16, n16, DA), F32)] * 2, scratch_shapes=[pltpu.VMEM((n16, DA), BF16)] * 3,
               compiler_params=_cp(1))(Pv, Pv, Pv)
    return o.reshape(T, DA), l.reshape(T, DA)


def _attn_bwd_stream(P, dO, lse, delta, acc, layer):
    T = P.shape[1]
    n16 = T // 16
    nb = n16 // BAND
    scale = HD ** -0.5
    first = acc is None

    def body(*refs):
        q_ref, k_ref, v_ref, do_ref, l_ref, dl_ref = refs[:6]
        if first:
            dq_ref, dk_ref, dv_ref = refs[6:9]
        else:
            aq_ref, ak_ref, av_ref, dq_ref, dk_ref, dv_ref = refs[6:12]
        qs, ks, vs, dos, okf, ovf = refs[-6:]
        for src, dst in ((q_ref, qs), (k_ref, ks), (v_ref, vs), (do_ref, dos)):
            dst[...] = src[...].astype(BF16)
        okf[...] = jnp.zeros_like(okf)
        ovf[...] = jnp.zeros_like(ovf)
        mask_c, mask_pc = _stream_masks()
        for b in range(nb):
            rows = slice(b * BAND, (b + 1) * BAND)
            krows = slice(max(b - 1, 0) * BAND, (b + 1) * BAND)
            mask = mask_c if b == 0 else mask_pc
            for hd in range(NH):
                sl = slice(hd * HD, (hd + 1) * HD)
                one = slice(hd * HD, hd * HD + 1)
                q, do, kk = qs[rows, sl], dos[rows, sl], ks[krows, sl]
                p = jnp.where(mask, jnp.exp(_dot_nt(q, kk) * scale - l_ref[rows, one]), 0.0)
                ds = (p * (_dot_nt(do, vs[krows, sl]) - dl_ref[rows, one]) * scale).astype(BF16)
                dq = _dot(ds, kk)
                dq_ref[rows, sl] = dq if first else aq_ref[rows, sl] + dq
                okf[krows, sl] += _dot_tn(ds, q)
                ovf[krows, sl] += _dot_tn(p.astype(BF16), do)
        dk_ref[...] = okf[...] if first else ak_ref[...] + okf[...]
        dv_ref[...] = ovf[...] if first else av_ref[...] + ovf[...]

    ins, out = _whole_stream_specs(T, 3 if first else 6)
    Pv = P.reshape(NSH, 16, n16, DA)
    view = lambda t: t.reshape(16, n16, DA)
    args = [Pv, Pv, Pv, view(dO), view(lse), view(delta)] + ([] if first else [view(t) for t in acc])
    dq, dk, dv = _pc(body, name=f"attn_bwd_d16_l{layer}", grid=(16,), in_specs=ins, out_specs=[out, out, out],
                     out_shape=[S((16, n16, DA), F32)] * 3,
                     scratch_shapes=[pltpu.VMEM((n16, DA), BF16)] * 4 + [pltpu.VMEM((n16, DA), F32)] * 2,
                     compiler_params=_cp(1))(*args)
    return dq.reshape(T, DA), dk.reshape(T, DA), dv.reshape(T, DA)


def _attn_fwd(P, d, layer):
    if d == 16:
        return _attn_fwd_stream(P, layer)
    T = P.shape[1]
    nb = T // d // BAND
    vshape = _pattern(d, T)[0]
    Pv = P.reshape((NSH,) + vshape)
    scale = HD ** -0.5

    def body(q_ref, kp_ref, kc_ref, vp_ref, vc_ref, o_ref, l_ref, qs, ks, vs, osc, lsc):
        b = pl.program_id(1)
        flat = lambda ref: ref[...].reshape(BAND, DA).astype(BF16)
        qs[...] = flat(q_ref)
        ks[0:BAND, :] = flat(kp_ref)
        ks[BAND:, :] = flat(kc_ref)
        vs[0:BAND, :] = flat(vp_ref)
        vs[BAND:, :] = flat(vc_ref)
        mask_c, mask_p = _band_masks(b, d)
        mask = jnp.concatenate([mask_p, mask_c], axis=1)
        for hd in range(NH):
            sl = slice(hd * HD, (hd + 1) * HD)
            s = jnp.where(mask, _dot_nt(qs[:, sl], ks[:, sl]) * scale, -1e30)
            m = jnp.max(s, axis=-1, keepdims=True)
            e = jnp.exp(s - m)
            den = jnp.sum(e, axis=-1, keepdims=True)
            osc[:, sl] = _dot(e.astype(BF16), vs[:, sl]) / den
            lsc[:, sl] = jnp.broadcast_to(m + jnp.log(den), (BAND, HD))
        o_ref[...] = osc[...].reshape(o_ref.shape)
        l_ref[...] = lsc[...].reshape(l_ref.shape)

    cur = lambda b: b
    prev = lambda b: jnp.maximum(b - 1, 0)
    out = _pattern_spec(d, T, cur)
    o, l = _pc(body, name=f"attn_fwd_d{d}_l{layer}", grid=(d, nb),
               in_specs=[_pattern_spec(d, T, cur, 0), _pattern_spec(d, T, prev, 1), _pattern_spec(d, T, cur, 1),
                         _pattern_spec(d, T, prev, 2), _pattern_spec(d, T, cur, 2)],
               out_specs=[out, out], out_shape=[S(vshape, F32)] * 2,
               scratch_shapes=[pltpu.VMEM((BAND, DA), BF16)] + [pltpu.VMEM((2 * BAND, DA), BF16)] * 2
               + [pltpu.VMEM((BAND, DA), F32)] * 2,
               compiler_params=_cp(2))(Pv, Pv, Pv, Pv, Pv)
    return o.reshape(T, DA), l.reshape(T, DA)


def _attn_bwd(P, dO, lse, delta, acc, d, layer):
    if d == 16:
        return _attn_bwd_stream(P, dO, lse, delta, acc, layer)
    T = P.shape[1]
    nb = T // d // BAND
    vshape = _pattern(d, T)[0]
    Pv = P.reshape((NSH,) + vshape)
    scale = HD ** -0.5
    first = acc is None

    def body(*refs):
        q_ref, kp_ref, kc_ref, vp_ref, vc_ref, do_ref, l_ref, dl_ref = refs[:8]
        if first:
            dq_ref, dk_ref, dv_ref = refs[8:11]
        else:
            aq_ref, ak_ref, av_ref, dq_ref, dk_ref, dv_ref = refs[8:14]
        qs, dos, ks, vs, ls, dls, oq, ok, ov, ck, cv = refs[-11:]
        b = pl.program_id(1)
        flat = lambda ref: ref[...].reshape(BAND, DA)

        @pl.when(b == 0)
        def _():
            ck[...] = jnp.zeros_like(ck)
            cv[...] = jnp.zeros_like(cv)

        @pl.when(b < nb)
        def _():
            qs[...] = flat(q_ref).astype(BF16)
            dos[...] = flat(do_ref).astype(BF16)
            ks[0:BAND, :] = flat(kp_ref).astype(BF16)
            ks[BAND:, :] = flat(kc_ref).astype(BF16)
            vs[0:BAND, :] = flat(vp_ref).astype(BF16)
            vs[BAND:, :] = flat(vc_ref).astype(BF16)
            ls[...] = flat(l_ref)
            dls[...] = flat(dl_ref)
            mask_c, mask_p = _band_masks(b, d)
            mask = jnp.concatenate([mask_p, mask_c], axis=1)
            for hd in range(NH):
                sl = slice(hd * HD, (hd + 1) * HD)
                one = slice(hd * HD, hd * HD + 1)
                q, do, kk = qs[:, sl], dos[:, sl], ks[:, sl]
                p = jnp.where(mask, jnp.exp(_dot_nt(q, kk) * scale - ls[:, one]), 0.0)
                ds = (p * (_dot_nt(do, vs[:, sl]) - dls[:, one]) * scale).astype(BF16)
                oq[:, sl] = _dot(ds, kk)
                dk2 = _dot_tn(ds, q)
                dv2 = _dot_tn(p.astype(BF16), do)
                ok[:, sl] = ck[:, sl] + dk2[0:BAND]
                ov[:, sl] = cv[:, sl] + dv2[0:BAND]
                ck[:, sl] = dk2[BAND:]
                cv[:, sl] = dv2[BAND:]
            if first:
                dq_ref[...] = oq[...].reshape(dq_ref.shape)
                dk_ref[...] = ok[...].reshape(dk_ref.shape)
                dv_ref[...] = ov[...].reshape(dv_ref.shape)
            else:
                dq_ref[...] = aq_ref[...] + oq[...].reshape(dq_ref.shape)
                dk_ref[...] = ak_ref[...] + ok[...].reshape(dk_ref.shape)
                dv_ref[...] = av_ref[...] + ov[...].reshape(dv_ref.shape)

        @pl.when(b == nb)
        def _():
            if first:
                dk_ref[...] = ck[...].reshape(dk_ref.shape)
                dv_ref[...] = cv[...].reshape(dv_ref.shape)
            else:
                dk_ref[...] = ak_ref[...] + ck[...].reshape(dk_ref.shape)
                dv_ref[...] = av_ref[...] + cv[...].reshape(dv_ref.shape)

    qb = lambda b: jnp.minimum(b, nb - 1)
    qprev = lambda b: jnp.maximum(qb(b) - 1, 0)
    kb = lambda b: jnp.maximum(b - 1, 0)
    qrow = _pattern_spec(d, T, qb)
    krow = _pattern_spec(d, T, kb)
    view = lambda t: t.reshape(vshape)
    ins = [Pv, Pv, Pv, Pv, Pv, view(dO), view(lse), view(delta)]
    specs = [_pattern_spec(d, T, qb, 0), _pattern_spec(d, T, qprev, 1), _pattern_spec(d, T, qb, 1),
             _pattern_spec(d, T, qprev, 2), _pattern_spec(d, T, qb, 2), qrow, qrow, qrow]
    if not first:
        ins += [view(t) for t in acc]
        specs += [qrow, krow, krow]
    dq, dk, dv = _pc(body, name=f"attn_bwd_d{d}_l{layer}", grid=(d, nb + 1), in_specs=specs,
                     out_specs=[qrow, krow, krow], out_shape=[S(vshape, F32)] * 3,
                     scratch_shapes=[pltpu.VMEM((BAND, DA), BF16)] * 2 + [pltpu.VMEM((2 * BAND, DA), BF16)] * 2
                     + [pltpu.VMEM((BAND, DA), F32)] * 7,
                     compiler_params=_cp(2))(*ins)
    return dq.reshape(T, DA), dk.reshape(T, DA), dv.reshape(T, DA)


def _ssm_prep(lam_re, lam_im, log_dt, b_re, b_im, c_re, c_im):
    dt = jnp.exp(log_dt)[:, None]
    er = jnp.exp(lam_re * dt)
    a_re = er * jnp.cos(lam_im * dt)
    a_im = er * jnp.sin(lam_im * dt)
    nr, ni = a_re - 1.0, a_im
    den = lam_re * lam_re + lam_im * lam_im
    cr = (nr * lam_re + ni * lam_im) / den
    ci = (ni * lam_re - nr * lam_im) / den
    bbr = cr[..., None] * b_re - ci[..., None] * b_im
    bbi = cr[..., None] * b_im + ci[..., None] * b_re
    eye = jnp.eye(8, dtype=F32)

    def bblock(bb):
        t = bb.reshape(4, 8, 64, 16).transpose(0, 1, 3, 2)
        return (t[:, :, :, None, :] * eye[None, :, None, :, None]).reshape(4, 128, 512)

    def cblock(cc):
        t = cc.reshape(4, 8, 16, 64).transpose(0, 1, 3, 2)
        return (t[:, :, :, None, :] * eye[None, :, None, :, None]).reshape(4, 512, 128)

    return (a_re.reshape(NLB, 1, 128), a_im.reshape(NLB, 1, 128), bblock(bbr), bblock(bbi), cblock(c_re), cblock(c_im))


def _perm_matrix(tm):
    n = tm // 16
    pm = np.zeros((tm, tm), np.float32)
    for r in range(16):
        pm[16 * np.arange(n) + r, r * n + np.arange(n)] = 1.0
    return jnp.asarray(pm, BF16)


def _pieces(x):
    p1 = x.astype(BF16)
    r1 = x - p1.astype(F32)
    p2 = r1.astype(BF16)
    return p1, p2, (r1 - p2.astype(F32)).astype(BF16)


def _to_time(x, pm):
    return sum(_dot(pm, p) for p in _pieces(x))


def _to_streams(x, pm):
    return sum(_dot_tn(pm, p) for p in _pieces(x))


def _stream_block(tm, cols, lead=None):
    if lead is None:
        return pl.BlockSpec((16, tm // 16, cols), lambda i: (0, i, 0))
    return pl.BlockSpec((None, 16, tm // 16, cols), lambda i: (lead, 0, i, 0))


def _reorder(t3, to_streams, name):
    B, T, C = t3.shape
    tm = TM

    def body(x_ref, pm_ref, o_ref):
        if to_streams:
            o_ref[...] = _to_streams(x_ref[...], pm_ref[...]).reshape(o_ref.shape)
        else:
            o_ref[...] = _to_time(x_ref[...].reshape(tm, C), pm_ref[...])

    time_blk = pl.BlockSpec((None, tm, C), lambda b, i: (b, i, 0))
    stream_blk = pl.BlockSpec((None, 16, tm // 16, C), lambda b, i: (b, 0, i, 0))
    src = t3 if to_streams else t3.reshape(B, 16, T // 16, C)
    out = _pc(body, name=name, grid=(B, T // tm),
              in_specs=[time_blk if to_streams else stream_blk, pl.BlockSpec((tm, tm), lambda b, i: (0, 0))],
              out_specs=stream_blk if to_streams else time_blk,
              out_shape=S((B, 16, T // 16, C) if to_streams else (B, T, C), F32),
              compiler_params=_cp(2))(src, _perm_matrix(tm))
    return out.reshape(B, T, C)


def _ssm_in(P, bre, bim, layer):
    T = P.shape[1]
    tm = TM

    def body(u_ref, pm_ref, br_ref, bi_ref, un_ref, or_ref, oi_ref):
        u = _to_time(u_ref[...].reshape(tm, DSS), pm_ref[...])
        un_ref[...] = u
        for s in range(4):
            uc = u[:, s * 128:(s + 1) * 128]
            r = _dot3(_dot, uc, br_ref[s])
            m = _dot3(_dot, uc, bi_ref[s])
            for q in range(4):
                or_ref[4 * s + q] = r[:, q * 128:(q + 1) * 128]
                oi_ref[4 * s + q] = m[:, q * 128:(q + 1) * 128]

    whole = pl.BlockSpec((4, 128, 512), lambda i: (0, 0, 0))
    st = pl.BlockSpec((NLB, tm, 128), lambda i: (0, i, 0))
    return _pc(body, name=f"ssm_in_l{layer}", grid=(T // tm,),
               in_specs=[_stream_block(tm, DSS, 3), pl.BlockSpec((tm, tm), lambda i: (0, 0)), whole, whole],
               out_specs=[pl.BlockSpec((tm, DSS), lambda i: (i, 0)), st, st],
               out_shape=[S((T, DSS), F32)] + [S((NLB, T, 128), F32)] * 2,
               compiler_params=_cp(1))(P.reshape(NSH, 16, T // 16, DSS), _perm_matrix(tm), bre, bim)


def _scan(br, bi, a_re, a_im, reverse, layer):
    T = br.shape[1]
    nbk = 4
    tt = min(T, 1024)
    nT = T // tt
    ntile = tt // 8
    sgn = -1.0 if reverse else 1.0
    last = 0 if reverse else 7

    def body(br_ref, bi_ref, ar_ref, ai_ref, xr_ref, xi_ref, cr, ci):
        @pl.when(pl.program_id(1) == 0)
        def _():
            cr[...] = jnp.zeros_like(cr)
            ci[...] = jnp.zeros_like(ci)

        row = lax.broadcasted_iota(jnp.int32, (8, 128), 0)
        consts = []
        for k in range(nbk):
            a1r = jnp.broadcast_to(ar_ref[k], (8, 128))
            a1i = sgn * jnp.broadcast_to(ai_ref[k], (8, 128))
            pows = [(a1r, a1i)]
            for _ in range(7):
                pr, pi_ = pows[-1]
                pows.append((a1r * pr - a1i * pi_, a1r * pi_ + a1i * pr))
            rounds = []
            for s in (1, 2, 4):
                inside = (row <= 7 - s) if reverse else (row >= s)
                rounds.append((jnp.where(inside, pows[s - 1][0], 0.0), jnp.where(inside, pows[s - 1][1], 0.0)))
            cmr, cmi = jnp.zeros((8, 128), F32), jnp.zeros((8, 128), F32)
            for r in range(8):
                e = (7 - r) if reverse else r
                cmr = jnp.where(row == r, pows[e][0], cmr)
                cmi = jnp.where(row == r, pows[e][1], cmi)
            consts.append((rounds, cmr, cmi))

        def tile(i, carry):
            j = (ntile - 1 - i) if reverse else i
            rows = pl.ds(pl.multiple_of(j * 8, 8), 8)
            out = []
            for k in range(nbk):
                rounds, cmr, cmi = consts[k]
                xr = br_ref[k, rows, :]
                xi = bi_ref[k, rows, :]
                for (mr, mi), s in zip(rounds, (1, 2, 4)):
                    sh = (8 - s) if reverse else s
                    rr = pltpu.roll(xr, sh, 0)
                    ri = pltpu.roll(xi, sh, 0)
                    xr, xi = xr + (mr * rr - mi * ri), xi + (mr * ri + mi * rr)
                c_r, c_i = carry[k]
                xr, xi = xr + (cmr * c_r - cmi * c_i), xi + (cmr * c_i + cmi * c_r)
                xr_ref[k, rows, :] = xr
                xi_ref[k, rows, :] = xi
                out.append((jnp.broadcast_to(xr[last:last + 1, :], (8, 128)),
                            jnp.broadcast_to(xi[last:last + 1, :], (8, 128))))
            return tuple(out)

        carry = lax.fori_loop(0, ntile, tile, tuple((cr[k], ci[k]) for k in range(nbk)), unroll=2)
        for k in range(nbk):
            cr[k] = carry[k][0]
            ci[k] = carry[k][1]

    tmap = (lambda t: nT - 1 - t) if reverse else (lambda t: t)
    st = pl.BlockSpec((nbk, tt, 128), lambda i, t: (i, tmap(t), 0))
    av = pl.BlockSpec((nbk, 1, 128), lambda i, t: (i, 0, 0))
    return _pc(body, name=f"scan_{'bwd' if reverse else 'fwd'}_l{layer}", grid=(NLB // nbk, nT),
               in_specs=[st, st, av, av], out_specs=[st, st], out_shape=[S((NLB, T, 128), F32)] * 2,
               scratch_shapes=[pltpu.VMEM((nbk, 8, 128), F32)] * 2, compiler_params=_cp(2))(br, bi, a_re, a_im)


def _ssm_out(xr, xi, u, cre, cim, dvec, wglu, bglu, layer):
    T = u.shape[0]
    tm = TM

    def body(xr_ref, xi_ref, u_ref, pm_ref, cr_ref, ci_ref, d_ref, w_ref, bg_ref, s_ref, y_ref, z_ref):
        ys = []
        for s in range(4):
            xrc = jnp.concatenate([xr_ref[4 * s + q] for q in range(4)], axis=1)
            xic = jnp.concatenate([xi_ref[4 * s + q] for q in range(4)], axis=1)
            ys.append(_dot3(_dot, xrc, cr_ref[s]) - _dot3(_dot, xic, ci_ref[s]))
        y = jnp.concatenate(ys, axis=1) + d_ref[...] * u_ref[...]
        yg = _gelu(y)
        ygb = yg.astype(BF16)
        z = bg_ref[...] + sum(_dot(ygb[:, j * 128:(j + 1) * 128], w_ref[j]) for j in range(NSH))
        y_ref[...] = y
        z_ref[...] = z
        s_ref[...] = _to_streams(yg * jax.nn.sigmoid(z), pm_ref[...]).reshape(s_ref.shape)

    st = pl.BlockSpec((NLB, tm, 128), lambda i: (0, i, 0))
    cw = pl.BlockSpec((4, 512, 128), lambda i: (0, 0, 0))
    half = pl.BlockSpec((tm, DSS), lambda i: (i, 0))
    s, y, z = _pc(body, name=f"ssm_out_l{layer}", grid=(T // tm,),
                  in_specs=[st, st, half, pl.BlockSpec((tm, tm), lambda i: (0, 0)), cw, cw, _gain_spec(DSS, layer),
                            pl.BlockSpec((NSH, None, 128, DSS), lambda i: (0, 0, 0, 0)), _gain_spec(DSS, layer)],
                  out_specs=[_stream_block(tm, DSS), half, half],
                  out_shape=[S((16, T // 16, DSS), F32), S((T, DSS), F32), S((T, DSS), F32)],
                  compiler_params=_cp(1))(xr, xi, u, _perm_matrix(tm), cre, cim, dvec, wglu, bglu)
    return s.reshape(T, DSS), y, z


def _ssm_out_bwd(dssm, y, z, xr, xi, u, cre, cim, dvec, wglu, layer):
    T = u.shape[0]
    tm = TM

    def body(ds_ref, pm_ref, y_ref, z_ref, xr_ref, xi_ref, u_ref, cr_ref, ci_ref, d_ref, w_ref,
             gr_ref, gi_ref, du_ref, dz_ref, yg_ref, dbg_ref, dd_ref, dcr_ref, dci_ref):
        i = pl.program_id(0)

        @pl.when(i == 0)
        def _():
            dbg_ref[...] = jnp.zeros_like(dbg_ref)
            dd_ref[...] = jnp.zeros_like(dd_ref)
            dcr_ref[...] = jnp.zeros_like(dcr_ref)
            dci_ref[...] = jnp.zeros_like(dci_ref)

        yv = y_ref[...]
        yg = _gelu(yv)
        sg = jax.nn.sigmoid(z_ref[...])
        ds = _to_time(ds_ref[...].reshape(tm, DSS), pm_ref[...])
        dz = ds * yg * sg * (1.0 - sg)
        dzb = dz.astype(BF16)
        dz_ref[...] = dzb
        yg_ref[...] = yg.astype(BF16)
        dbg_ref[...] += jnp.sum(dz, axis=0, keepdims=True)
        dyg = ds * sg + jnp.concatenate([_dot_nt(dzb, w_ref[j]) for j in range(NSH)], axis=1)
        dy = dyg * _gelu_grad(yv)
        u = u_ref[...]
        dd_ref[...] += jnp.sum(dy * u, axis=0, keepdims=True)
        du_ref[...] = dy * d_ref[...]
        for s in range(4):
            dyc = dy[:, s * 128:(s + 1) * 128]
            g_r = _dot3(_dot_nt, dyc, cr_ref[s])
            g_i = -_dot3(_dot_nt, dyc, ci_ref[s])
            for q in range(4):
                gr_ref[4 * s + q] = g_r[:, q * 128:(q + 1) * 128]
                gi_ref[4 * s + q] = g_i[:, q * 128:(q + 1) * 128]
            xrc = jnp.concatenate([xr_ref[4 * s + q] for q in range(4)], axis=1)
            xic = jnp.concatenate([xi_ref[4 * s + q] for q in range(4)], axis=1)
            dcr_ref[s] += _dot3(_dot_tn, xrc, dyc)
            dci_ref[s] -= _dot3(_dot_tn, xic, dyc)

    st = pl.BlockSpec((NLB, tm, 128), lambda i: (0, i, 0))
    cw = pl.BlockSpec((4, 512, 128), lambda i: (0, 0, 0))
    half = pl.BlockSpec((tm, DSS), lambda i: (i, 0))
    return _pc(body, name=f"ssm_out_bwd_l{layer}", grid=(T // tm,),
               in_specs=[_stream_block(tm, DSS), pl.BlockSpec((tm, tm), lambda i: (0, 0)), half, half, st, st, half,
                         cw, cw, _gain_spec(DSS, layer), pl.BlockSpec((NSH, None, 128, DSS), lambda i: (0, 0, 0, 0))],
               out_specs=[st, st, half, half, half, _row_acc_spec(DSS), _row_acc_spec(DSS), cw, cw],
               out_shape=[S((NLB, T, 128), F32)] * 2 + [S((T, DSS), F32), S((T, DSS), BF16), S((T, DSS), BF16),
                                                        S((1, DSS), F32), S((1, DSS), F32),
                                                        S((4, 512, 128), F32), S((4, 512, 128), F32)],
               compiler_params=_cp(1))(dssm.reshape(16, T // 16, DSS), _perm_matrix(tm), y, z, xr, xi, u, cre, cim,
                                       dvec, wglu)


def _ssm_da(gr, gi, xr, xi, layer):
    T = gr.shape[1]
    tb = 4096 if T % 4096 == 0 else T

    def body(gr_ref, gi_ref, xr_ref, xi_ref, dr_ref, di_ref, lr, li):
        t = pl.program_id(1)

        @pl.when(t == 0)
        def _():
            dr_ref[...] = jnp.zeros_like(dr_ref)
            di_ref[...] = jnp.zeros_like(di_ref)
            lr[...] = jnp.zeros_like(lr)
            li[...] = jnp.zeros_like(li)

        g_r, g_i, x_r, x_i = gr_ref[...], gi_ref[...], xr_ref[...], xi_ref[...]
        pr = pltpu.roll(x_r, 1, 0)
        pi_ = pltpu.roll(x_i, 1, 0)
        g0r, g0i = g_r[0:1, :], g_i[0:1, :]
        fr = lr[7:8, :] - x_r[tb - 1:tb, :]
        fi = li[7:8, :] - x_i[tb - 1:tb, :]
        dr_ref[...] += jnp.sum(g_r * pr + g_i * pi_, axis=0, keepdims=True) + g0r * fr + g0i * fi
        di_ref[...] += jnp.sum(g_i * pr - g_r * pi_, axis=0, keepdims=True) + g0i * fr - g0r * fi
        lr[...] = x_r[tb - 8:tb, :]
        li[...] = x_i[tb - 8:tb, :]

    st = pl.BlockSpec((None, tb, 128), lambda k, t: (k, t, 0))
    out = pl.BlockSpec((None, 1, 128), lambda k, t: (k, 0, 0))
    return _pc(body, name=f"ssm_da_l{layer}", grid=(NLB, T // tb), in_specs=[st] * 4, out_specs=[out, out],
               out_shape=[S((NLB, 1, 128), F32)] * 2, scratch_shapes=[pltpu.VMEM((8, 128), F32)] * 2,
               compiler_params=_cp(2))(gr, gi, xr, xi)


def _ssm_in_bwd(gr, gi, u, bre, bim, du_direct, layer):
    T = u.shape[0]
    tm = TM

    def body(gr_ref, gi_ref, u_ref, pm_ref, br_ref, bi_ref, dd_ref, du_ref, dbr_ref, dbi_ref):
        i = pl.program_id(0)

        @pl.when(i == 0)
        def _():
            dbr_ref[...] = jnp.zeros_like(dbr_ref)
            dbi_ref[...] = jnp.zeros_like(dbi_ref)

        dus = []
        for s in range(4):
            grc = jnp.concatenate([gr_ref[4 * s + q] for q in range(4)], axis=1)
            gic = jnp.concatenate([gi_ref[4 * s + q] for q in range(4)], axis=1)
            uc = u_ref[:, s * 128:(s + 1) * 128]
            dus.append(_dot3(_dot_nt, grc, br_ref[s]) + _dot3(_dot_nt, gic, bi_ref[s]))
            dbr_ref[s] += _dot3(_dot_tn, uc, grc)
            dbi_ref[s] += _dot3(_dot_tn, uc, gic)
        du = jnp.concatenate(dus, axis=1) + dd_ref[...]
        du_ref[...] = _to_streams(du, pm_ref[...]).reshape(du_ref.shape)

    whole = pl.BlockSpec((4, 128, 512), lambda i: (0, 0, 0))
    st = pl.BlockSpec((NLB, tm, 128), lambda i: (0, i, 0))
    half = pl.BlockSpec((tm, DSS), lambda i: (i, 0))
    du, dbr, dbi = _pc(body, name=f"ssm_in_bwd_l{layer}", grid=(T // tm,),
                       in_specs=[st, st, half, pl.BlockSpec((tm, tm), lambda i: (0, 0)), whole, whole, half],
                       out_specs=[_stream_block(tm, DSS), whole, whole],
                       out_shape=[S((16, T // 16, DSS), F32), S((4, 128, 512), F32), S((4, 128, 512), F32)],
                       compiler_params=_cp(1))(gr, gi, u, _perm_matrix(tm), bre, bim, du_direct)
    return du.reshape(T, DSS), dbr, dbi


def _mix_out(outs, lses, ssm, h, attn_g, ssm_g, post_g, wout, layer):
    T = h.shape[0]
    tm = TM

    def body(o1, o2, o3, l1, l2, l3, s_ref, h_ref, ag_ref, sg_ref, pg_ref, w_ref, ho_ref, at_ref, ls_ref, mx_ref, mo_ref):
        la, lb, lc = l1[...], l2[...], l3[...]
        m = jnp.maximum(jnp.maximum(la, lb), lc)
        wa, wb, wc = jnp.exp(la - m), jnp.exp(lb - m), jnp.exp(lc - m)
        zs = wa + wb + wc
        attn = (wa * o1[...] + wb * o2[...] + wc * o3[...]) / zs
        at_ref[...] = attn
        ls_ref[...] = m + jnp.log(zs)
        mixed = jnp.concatenate([_rms_fwd(attn, ag_ref[...]), _rms_fwd(s_ref[...], sg_ref[...])], axis=1).astype(BF16)
        mx_ref[...] = mixed
        mo = sum(_dot(mixed[:, j * 256:(j + 1) * 256], w_ref[j]) for j in range(NSH))
        mo_ref[...] = mo
        ho_ref[...] = h_ref[...] + _rms_fwd(mo, pg_ref[...])

    row = pl.BlockSpec((tm, D), lambda i: (i, 0))
    half = pl.BlockSpec((tm, DA), lambda i: (i, 0))
    return _pc(body, name=f"mix_out_l{layer}", grid=(T // tm,),
               in_specs=[half] * 7 + [row, _gain_spec(DA, layer), _gain_spec(DSS, layer), _gain_spec(D, layer),
                                      pl.BlockSpec((NSH, None, 256, D), lambda i: (0, 0, 0, 0))],
               out_specs=[row, half, half, row, row],
               out_shape=[S((T, D), F32), S((T, DA), F32), S((T, DA), F32), S((T, D), BF16), S((T, D), F32)],
               compiler_params=_cp(1))(*outs, *lses, ssm, h, attn_g, ssm_g, post_g, wout)


def _mix_out_bwd(dout, mo, attn, ssm, attn_g, ssm_g, post_g, wout, layer):
    T = dout.shape[0]
    tm = TM
    head_sum =jnp.asarray(np.kron(np.eye(NH, dtype=np.float32), np.ones((HD, HD), np.float32)), BF16)

    def body(do_ref, mo_ref, at_ref, s_ref, ag_ref, sg_ref, pg_ref, w_ref, e_ref,
             da_ref, ds_ref, dl_ref, dmo_ref, dpg_ref, dag_ref, dsg_ref):
        i = pl.program_id(0)

        @pl.when(i == 0)
        def _():
            dpg_ref[...] = jnp.zeros_like(dpg_ref)
            dag_ref[...] = jnp.zeros_like(dag_ref)
            dsg_ref[...] = jnp.zeros_like(dsg_ref)

        dmo, dpg = _rms_bwd(do_ref[...], mo_ref[...], pg_ref[...])
        dpg_ref[...] += dpg
        dmob = dmo.astype(BF16)
        dmo_ref[...] = dmob
        dmix = jnp.concatenate([_dot_nt(dmob, w_ref[j]) for j in range(NSH)], axis=1)
        attn = at_ref[...]
        dat, dag = _rms_bwd(dmix[:, :DA], attn, ag_ref[...])
        dss, dsg = _rms_bwd(dmix[:, DA:], s_ref[...], sg_ref[...])
        dag_ref[...] += dag
        dsg_ref[...] += dsg
        da_ref[...] = dat
        ds_ref[...] = dss
        prod = dat * attn
        p1 = prod.astype(BF16)
        r1 = prod - p1.astype(F32)
        p2 = r1.astype(BF16)
        p3 = (r1 - p2.astype(F32)).astype(BF16)
        e = e_ref[...]
        dl_ref[...] = _dot(p1, e) + _dot(p2, e) + _dot(p3, e)

    row = pl.BlockSpec((tm, D), lambda i: (i, 0))
    half = pl.BlockSpec((tm, DA), lambda i: (i, 0))
    return _pc(body, name=f"mix_out_bwd_l{layer}", grid=(T // tm,),
               in_specs=[row, row, half, half, _gain_spec(DA, layer), _gain_spec(DSS, layer), _gain_spec(D, layer),
                         pl.BlockSpec((NSH, None, 256, D), lambda i: (0, 0, 0, 0)),
                         pl.BlockSpec((DA, DA), lambda i: (0, 0))],
               out_specs=[half, half, half, row, _row_acc_spec(D), _row_acc_spec(DA), _row_acc_spec(DSS)],
               out_shape=[S((T, DA), F32)] * 3 + [S((T, D), BF16), S((1, D), F32), S((1, DA), F32), S((1, DSS), F32)],
               compiler_params=_cp(1))(dout, mo, attn, ssm, attn_g, ssm_g, post_g, wout, head_sum)


def _ple_fwd(h, p3, wup, wgate, post_g, layer):
    T = h.shape[0]
    tm = TM

    def body(h_ref, p_ref, wu_ref, wg_ref, g_ref, ho_ref, e_ref, gt_ref):
        hv = h_ref[...]
        hb = hv.astype(BF16)
        pb = p_ref[...].astype(BF16)
        gte = sum(_dot(hb[:, j * 256:(j + 1) * 256], wg_ref[j]) for j in range(NSH))
        e = jnp.concatenate([_dot(pb, wu_ref[j]) for j in range(NSH)], axis=1)
        e_ref[...] = e
        gt_ref[...] = gte
        ho_ref[...] = hv + _rms_fwd(e * jax.nn.sigmoid(gte), g_ref[...])

    row = pl.BlockSpec((tm, D), lambda i: (i, 0))
    return _pc(body, name=f"ple_fwd_l{layer}", grid=(T // tm,),
               in_specs=[row, pl.BlockSpec((None, tm, PLE), lambda i: (layer, i, 0)),
                         pl.BlockSpec((NSH, None, PLE, 256), lambda i: (0, 0, 0, 0)),
                         pl.BlockSpec((NSH, None, 256, D), lambda i: (0, 0, 0, 0)), _gain_spec(D, layer)],
               out_specs=[row, row, row], out_shape=[S((T, D), F32)] * 3,
               compiler_params=_cp(1))(h, p3, wup, wgate, post_g)


def _ple_bwd(dout, e, gte, wgate, post_g, layer):
    T = dout.shape[0]
    tm = TM

    def body(do_ref, e_ref, gt_ref, wg_ref, g_ref, dh_ref, de_ref, dgt_ref, dg_ref):
        i = pl.program_id(0)

        @pl.when(i == 0)
        def _():
            dg_ref[...] = jnp.zeros_like(dg_ref)

        ev = e_ref[...]
        sg = jax.nn.sigmoid(gt_ref[...])
        do = do_ref[...]
        dple, dg = _rms_bwd(do, ev * sg, g_ref[...])
        dg_ref[...] += dg
        de = (dple * sg).astype(BF16)
        for j in range(NSH):
            de_ref[j] = de[:, j * 256:(j + 1) * 256]
        dgb = (dple * ev * sg * (1.0 - sg)).astype(BF16)
        dgt_ref[...] = dgb
        dh_ref[...] = do + jnp.concatenate([_dot_nt(dgb, wg_ref[j]) for j in range(NSH)], axis=1)

    row = pl.BlockSpec((tm, D), lambda i: (i, 0))
    return _pc(body, name=f"ple_bwd_l{layer}", grid=(T // tm,),
               in_specs=[row, row, row, pl.BlockSpec((NSH, None, 256, D), lambda i: (0, 0, 0, 0)), _gain_spec(D, layer)],
               out_specs=[row, pl.BlockSpec((NSH, tm, 256), lambda i: (0, i, 0)), row, _row_acc_spec(D)],
               out_shape=[S((T, D), F32), S((NSH, T, 256), BF16), S((T, D), BF16), S((1, D), F32)],
               compiler_params=_cp(1))(dout, e, gte, wgate, post_g)


def _loss_head(h, target):
    T = h.shape[0]
    tm = TM

    def body(h_ref, t_ref, dy_ref, l_ref):
        i = pl.program_id(0)

        @pl.when(i == 0)
        def _():
            l_ref[...] = jnp.zeros_like(l_ref)

        err = h_ref[...] - t_ref[...]
        dy_ref[...] = err * (1.0 / D)
        l_ref[...] += jnp.broadcast_to((0.5 / D) * jnp.sum(err * err), (1, 128))

    row = pl.BlockSpec((tm, D), lambda i: (i, 0))
    return _pc(body, name="loss_head", grid=(T // tm,), in_specs=[row, row],
               out_specs=[row, pl.BlockSpec((1, 128), lambda i: (0, 0))],
               out_shape=[S((T, D), F32), S((1, 128), F32)], compiler_params=_cp(1))(h, target)


def _local_step(x, p3, pos_col, target, weights_of, upper_grads_done, Sm):
    L = p3.shape[0]
    g3 = {n: Sm[n].reshape(L, 1, -1) for n in ("ffn1_pre_g", "ffn1_post_g", "mix_pre_g", "attn_norm_g", "ssm_norm_g",
                                                "mix_post_g", "ffn2_pre_g", "ffn2_post_g", "ple_post_g", "ssm_b_glu", "ssm_d")}
    rot = _rot_tables(pos_col)
    prep_names = ("ssm_lam_re", "ssm_lam_im", "ssm_log_dt", "ssm_b_re", "ssm_b_im", "ssm_c_re", "ssm_c_im")
    prep_all, prep_vjp = jax.vjp(jax.vmap(_ssm_prep), *[Sm[n] for n in prep_names])
    prep_cot = [None] * L

    saved = []
    h = x
    for l in range(L):
        W = weights_of(l, h)
        sv = {"h0": h, "W": W}
        h, sv["a1"], sv["b1"], sv["f1"], sv["xn1"] = _ffn_fwd(
            h, g3["ffn1_pre_g"], g3["ffn1_post_g"], W["ffn1_w_gate"], W["ffn1_w_up"], W["ffn1_w_down"], l, "1")
        sv["h1"] = h
        P, sv["ain"] = _mix_proj(h, g3["mix_pre_g"], W["w_in"], rot, l)
        sv["P"] = P
        ol = [_attn_fwd(P, d, l) for d in PATTERN_DILATIONS]
        prep = tuple(t[l] for t in prep_all)
        a_re, a_im, bre, bim, cre, cim = prep
        sv["prep"] = prep
        sv["u"], bur, bui = _ssm_in(P, bre, bim, l)
        xr, xi = _scan(bur, bui, a_re, a_im, False, l)
        sv["xr"], sv["xi"] = xr, xi
        ssm, sv["y"], sv["z"] = _ssm_out(xr, xi, sv["u"], cre, cim, g3["ssm_d"], W["ssm_w_glu"], g3["ssm_b_glu"], l)
        sv["ssm"] = ssm
        h, sv["attn"], sv["lse"], sv["mixed"], sv["mo"] = _mix_out(
            [o for o, _ in ol], [s for _, s in ol], ssm, h, g3["attn_norm_g"], g3["ssm_norm_g"], g3["mix_post_g"],
            W["w_out"], l)
        sv["h2"] = h
        h, sv["a2"], sv["b2"], sv["f2"], sv["xn2"] = _ffn_fwd(
            h, g3["ffn2_pre_g"], g3["ffn2_post_g"], W["ffn2_w_gate"], W["ffn2_w_up"], W["ffn2_w_down"], l, "2")
        sv["h3"] = h
        h, sv["e"], sv["gte"] = _ple_fwd(h, p3, W["ple_w_up"], W["ple_w_gate"], g3["ple_post_g"], l)
        saved.append(sv)

    dh, loss = _loss_head(h, target)

    G_upper = {n: lax.empty((NSH, L - 1, r, c), BF16) for n, r, c in BIG} if L > 1 else {}
    G_first = {n: lax.empty((NSH, 1, r, c), BF16) for n, r, c in BIG}
    sg = {n: [None] * L for n in SMALL}
    whole, shard, kcol = "whole", "shard", "cols"
    ple_g = g3["ple_post_g"]
    for l in reversed(range(L)):
        sv = saved[l]
        W = sv["W"]
        G, gl = (G_first, 0) if l == 0 else (G_upper, l - 1)
        if l == 0 and L > 1:
            ple_g = ple_g + upper_grads_done(G_upper)
        dh, de, dgte, sg["ple_post_g"][l] = _ple_bwd(dh, sv["e"], sv["gte"], W["ple_w_gate"], ple_g, l)
        G["ple_w_up"] = _dw(p3[l][None], de, G["ple_w_up"], gl, PLE, 256, whole, shard, f"dw_ple_up_l{l}")
        G["ple_w_gate"] = _dw(sv["h3"][None], dgte[None], G["ple_w_gate"], gl, 256, D, kcol, whole, f"dw_ple_gate_l{l}")
        dh, df, da, db, hh, sg["ffn2_pre_g"][l], sg["ffn2_post_g"][l] = _ffn_bwd(
            dh, sv["h2"], sv["f2"], sv["a2"], sv["b2"], g3["ffn2_pre_g"], g3["ffn2_post_g"],
            W["ffn2_w_gate"], W["ffn2_w_up"], W["ffn2_w_down"], l, "2")
        G["ffn2_w_gate"] = _dw(da, sv["xn2"][None], G["ffn2_w_gate"], gl, DFS, D, shard, whole, f"dw_ffn2_gate_l{l}")
        G["ffn2_w_up"] = _dw(db, sv["xn2"][None], G["ffn2_w_up"], gl, DFS, D, shard, whole, f"dw_ffn2_up_l{l}")
        G["ffn2_w_down"] = _dw(hh, df[None], G["ffn2_w_down"], gl, DFS, D, shard, whole, f"dw_ffn2_down_l{l}")
        a_re, a_im, bre, bim, cre, cim = sv["prep"]
        dattn, dssm, delta, dmo, sg["mix_post_g"][l], sg["attn_norm_g"][l], sg["ssm_norm_g"][l] = _mix_out_bwd(
            dh, sv["mo"], sv["attn"], sv["ssm"], g3["attn_norm_g"], g3["ssm_norm_g"], g3["mix_post_g"], W["w_out"], l)
        G["w_out"] = _dw(sv["mixed"][None], dmo[None], G["w_out"], gl, 256, D, kcol, whole, f"dw_out_l{l}")
        gnr, gni, du_direct, dz, yg, sg["ssm_b_glu"][l], dd, dcre, dcim = _ssm_out_bwd(
            dssm, sv["y"], sv["z"], sv["xr"], sv["xi"], sv["u"], cre, cim, g3["ssm_d"], W["ssm_w_glu"], l)
        sg["ssm_d"][l] = dd.reshape(Sm["ssm_d"].shape[1:])
        G["ssm_w_glu"] = _dw(yg[None], dz[None], G["ssm_w_glu"], gl, 128, DSS, kcol, whole, f"dw_glu_l{l}")
        gr, gi = _scan(gnr, gni, a_re, a_im, True, l)
        dar, dai = _ssm_da(gr, gi, sv["xr"], sv["xi"], l)
        du, dbre, dbim = _ssm_in_bwd(gr, gi, sv["u"], bre, bim, du_direct, l)
        prep_cot[l] = (dar, dai, dbre, dbim, dcre, dcim)
        acc = None
        for d in PATTERN_DILATIONS:
            acc = _attn_bwd(sv["P"], dattn, sv["lse"], delta, acc, d, l)
        dh, dP, sg["mix_pre_g"][l] = _mix_proj_bwd(acc[0], acc[1], acc[2], du, dh, sv["h1"], g3["mix_pre_g"],
                                                   W["w_in"], rot, l)
        G["w_in"] = _dw(sv["ain"][None], dP, G["w_in"], gl, D, DA,whole, shard, f"dw_in_l{l}")
        dh, df, da, db, hh, sg["ffn1_pre_g"][l], sg["ffn1_post_g"][l] = _ffn_bwd(
            dh, sv["h0"], sv["f1"], sv["a1"], sv["b1"], g3["ffn1_pre_g"], g3["ffn1_post_g"],
            W["ffn1_w_gate"], W["ffn1_w_up"], W["ffn1_w_down"], l, "1")
        G["ffn1_w_gate"] = _dw(da, sv["xn1"][None], G["ffn1_w_gate"], gl, DFS, D, shard, whole, f"dw_ffn1_gate_l{l}")
        G["ffn1_w_up"] = _dw(db, sv["xn1"][None], G["ffn1_w_up"], gl, DFS, D, shard, whole, f"dw_ffn1_up_l{l}")
        G["ffn1_w_down"] = _dw(hh, df[None], G["ffn1_w_down"], gl, DFS, D, shard, whole, f"dw_ffn1_down_l{l}")

    small = {n: jnp.stack([g.reshape(Sm[n].shape[1:]) for g in sg[n]]) for n in SMALL if n not in prep_names}
    small.update(zip(prep_names, prep_vjp(tuple(jnp.stack(c) for c in zip(*prep_cot)))))
    return loss, dh, G_upper, G_first, small


HBM_SPEC = pl.BlockSpec(memory_space=pltpu.HBM)


def _place():
    x, y, c = lax.axis_index("x"), lax.axis_index("y"), lax.axis_index("c")
    chips = [(1 - x, y), (x, 1 - y), (1 - x, 1 - y)]
    return x, y, c, chips


def _comm_params():
    return pltpu.CompilerParams(vmem_limit_bytes=VMEM_LIMIT)


def _gather_weights(ws, lands):
    n = len(ws)

    def body(*refs):
        ins, outs = refs[:n], refs[2 * n:3 * n]
        s_ici, r_ici, s_d2d, r_d2d = refs[3 * n:]
        x, y, c, chips = _place()

        def half(ref, t, hc):
            r2 = ws[t].shape[1] // 2
            return ref.at[:, pl.ds(hc * r2, r2), :]

        def ici(t, k, src_chip, to):
            j = 2 * src_chip[0] + src_chip[1]
            src = half(ins[t], t, c) if to is not None else half(outs[t].at[j], t, c)
            return pltpu.make_async_remote_copy(src_ref=src, dst_ref=half(outs[t].at[j], t, c),
                                                send_sem=s_ici.at[3 * t + k], recv_sem=r_ici.at[3 * t + k],
                                                device_id=to if to is not None else (x, y, c), device_id_type=MESH)

        def d2d(t, k, hc):
            j = 2 * chips[k][0] + chips[k][1]
            r = half(outs[t].at[j], t, hc)
            return pltpu.make_async_remote_copy(src_ref=r, dst_ref=r, send_sem=s_d2d.at[3 * t + k],
                                                recv_sem=r_d2d.at[3 * t + k], device_id=(x, y, 1 - c),
                                                device_id_type=MESH)

        sends = [ici(t, k, (x, y), (*chips[k], c)) for t in range(n) for k in range(3)]
        for cp in sends:
            cp.start()
        passed = []
        for t in range(n):
            for k in range(3):
                ici(t, k, chips[k], None).wait_recv()
                passed.append(d2d(t, k, c))
                passed[-1].start()
        for t in range(n):
            for k in range(3):
                d2d(t, k, 1 - c).wait_recv()
        for cp in sends + passed:
            cp.wait_send()

    return _pc(body, name="gather_weights", in_specs=[HBM_SPEC] * (2 * n), out_specs=[HBM_SPEC] * n,
               out_shape=[S(z.shape, z.dtype) for z in lands], input_output_aliases={n + t: t for t in range(n)},
               scratch_shapes=[pltpu.SemaphoreType.DMA((3 * n,))] * 4, compiler_params=_comm_params())(*ws, *lands)


SEM_SPEC = pl.BlockSpec(memory_space=pltpu.SEMAPHORE)
ANY_SPEC = pl.BlockSpec(memory_space=pl.ANY)
SPLIT_EFFECT = pltpu.SideEffectType.DATAFLOW_SIDE_EFFECTING


def _in_hbm(t):
    return pltpu.with_memory_space_constraint(t, pltpu.HBM)


def _place_own(ws, me_arr, layer):
    n = len(ws)

    def body(me_ref, *refs):
        for t in range(n):
            refs[n + t][...] = refs[t][...]

    gs = pltpu.PrefetchScalarGridSpec(
        num_scalar_prefetch=1, grid=(2,),
        in_specs=[pl.BlockSpec((w.shape[0], w.shape[1] // 2, w.shape[2]), lambda i, me: (0, i, 0)) for w in ws],
        out_specs=[pl.BlockSpec((None, w.shape[0], w.shape[1] // 2, w.shape[2]), lambda i, me: (me[0], 0, i, 0))
                   for w in ws])
    return _pc(body, name=f"gather_place_own_l{layer}", grid_spec=gs,
               out_shape=[S((NSH,) + w.shape, w.dtype) for w in ws], compiler_params=_cp(1))(me_arr, *ws)


def _gather_start(ws, lands, after, layer):
    n = len(ws)

    def body(*refs):
        ins, lz = refs[:n], refs[n:2 * n]
        s_sem, r_sem = refs[2 * n + 1], refs[2 * n + 2]
        token = refs[-1]
        x, y, c, chips = _place()
        for t in range(n):
            for k in range(3):
                pltpu.make_async_remote_copy(src_ref=ins[t], dst_ref=lz[t].at[2 * x + y], send_sem=s_sem.at[3 * t + k],
                                             recv_sem=r_sem.at[3 * t + k], device_id=(*chips[k], c),
                                             device_id_type=MESH).start()
        token[...] = jnp.zeros_like(token)

    hbm = [pltpu.HBM(w.shape, w.dtype) for w in ws] + [pltpu.HBM(z.shape, z.dtype) for z in lands]
    out = _pc(body, name=f"gather_start_l{layer}",
              out_shape=(pltpu.SemaphoreType.DMA((3 * n,)), pltpu.SemaphoreType.DMA((3 * n,)), *hbm, S((8, 128), F32)),
              in_specs=[HBM_SPEC] * (2 * n) + [ANY_SPEC],
              out_specs=(SEM_SPEC, SEM_SPEC, *([HBM_SPEC] * (2 * n)), pl.BlockSpec(memory_space=pltpu.VMEM)),
              input_output_aliases={i: 2 + i for i in range(2 * n)},
              compiler_params=pltpu.CompilerParams(has_side_effects=SPLIT_EFFECT))(
                  *[_in_hbm(w) for w in ws], *[_in_hbm(z) for z in lands], after)
    return out[0], out[1], out[2:2 + n], out[2 + n:2 + 2 * n], out[-1]


def _gather_wait(s_sem, r_sem, ws, lands, after, layer):
    n = len(ws)

    def body(*refs):
        ins, lz = refs[:n], refs[n:2 * n]
        s_ref, r_ref = refs[2 * n], refs[2 * n + 1]
        x, y, c, chips = _place()
        for t in range(n):
            for k in range(3):
                cp = pltpu.make_async_remote_copy(src_ref=ins[t], dst_ref=lz[t].at[2 * x + y], send_sem=s_ref.at[3 * t + k],
                                                  recv_sem=r_ref.at[3 * t + k], device_id=(*chips[k], c),
                                                  device_id_type=MESH)
                cp.wait_send()
                cp.wait_recv()

    hbm = [pltpu.HBM(w.shape, w.dtype) for w in ws] + [pltpu.HBM(z.shape, z.dtype) for z in lands]
    out = _pc(body, name=f"gather_wait_l{layer}", out_shape=tuple(hbm),
              in_specs=[HBM_SPEC] * (2 * n) + [SEM_SPEC, SEM_SPEC, ANY_SPEC], out_specs=tuple([HBM_SPEC] * (2 * n)),
              input_output_aliases={i: i for i in range(2 * n)},
              compiler_params=pltpu.CompilerParams(has_side_effects=SPLIT_EFFECT))(*ws, *lands, s_sem, r_sem, after)
    return out[n:]


def _swap_halves(gs, tag):
    n = len(gs)

    def body(*refs):
        ins, outs = refs[:n], refs[n:2 * n]
        s_sem, r_sem = refs[2 * n:]
        x, y, c, _ = _place()
        cps = []
        for t in range(n):
            r2 = gs[t].shape[2] // 2
            cps.append(pltpu.make_async_remote_copy(
                src_ref=ins[t].at[:, :, pl.ds((1 - c) * r2, r2), :], dst_ref=outs[t], send_sem=s_sem.at[t],
                recv_sem=r_sem.at[t], device_id=(x, y, 1 - c), device_id_type=MESH))
            cps[-1].start()
        for cp in cps:
            cp.wait_recv()
        for cp in cps:
            cp.wait_send()

    return _pc(body, name=f"grad_swap_halves_{tag}", in_specs=[HBM_SPEC] * n, out_specs=[HBM_SPEC] * n,
               out_shape=[S(g.shape[:2] + (g.shape[2] // 2, g.shape[3]), g.dtype) for g in gs],
               scratch_shapes=[pltpu.SemaphoreType.DMA((n,))] * 2, compiler_params=_comm_params())(*gs)


def _add_half(g, landed, c_arr, name):
    _, L, r2, cols = landed.shape

    def body(c_ref, g_ref, l_ref, o_ref):
        o_ref[...] = (g_ref[...].astype(F32) + l_ref[...].astype(F32)).astype(BF16)

    gs = pltpu.PrefetchScalarGridSpec(
        num_scalar_prefetch=1, grid=(NSH, L),
        in_specs=[pl.BlockSpec((None, None, r2, cols), lambda j, l, c: (j, l, c[0], 0)),
                  pl.BlockSpec((None, None, r2, cols), lambda j, l, c: (j, l, 0, 0))],
        out_specs=pl.BlockSpec((None, None, r2, cols), lambda j, l, c: (j, l, 0, 0)))
    return _pc(body, name=name, grid_spec=gs, out_shape=S(landed.shape, BF16), compiler_params=_cp(2))(c_arr, g, landed)


def _send_shards(ps):
    n = len(ps)

    def body(*refs):
        ins, outs = refs[:n], refs[n:2 * n]
        s_sem, r_sem = refs[2 * n:]
        x, y, c, chips = _place()
        cps = []
        for t in range(n):
            for k in range(3):
                cps.append(pltpu.make_async_remote_copy(
                    src_ref=ins[t].at[2 * chips[k][0] + chips[k][1]], dst_ref=outs[t].at[k],
                    send_sem=s_sem.at[3 * t + k], recv_sem=r_sem.at[3 * t + k], device_id=(*chips[k], c),
                    device_id_type=MESH))
                cps[-1].start()
        for cp in cps:
            cp.wait_recv()
        for cp in cps:
            cp.wait_send()

    return _pc(body, name="grad_send_shards", in_specs=[HBM_SPEC] * n, out_specs=[HBM_SPEC] * n,
               out_shape=[S((3,) + p.shape[1:], p.dtype) for p in ps],
               scratch_shapes=[pltpu.SemaphoreType.DMA((3 * n,))] * 2, compiler_params=_comm_params())(*ps)


def _direct_grad_copies(ins, lz, s_sem, r_sem):
    x, y, c, chips = _place()
    sends, recvs = [], []
    for t in range(len(ins)):
        r2 = ins[t].shape[2] // 2
        half = lambda j, h: ins[t].at[j, :, pl.ds(h * r2, r2), :]

        def copy(src, slot, s_idx, r_idx, to):
            return pltpu.make_async_remote_copy(src_ref=src, dst_ref=lz[t].at[slot], send_sem=s_sem.at[7 * t + s_idx],
                                                recv_sem=r_sem.at[7 * t + r_idx], device_id=to, device_id_type=MESH)

        for k in range(3):
            for h in range(2):
                sends.append(copy(half(2 * chips[k][0] + chips[k][1], h), 2 * k + c, 2 * k + h, 2 * k + c, (*chips[k], h)))
        sends.append(copy(half(2 * x + y, 1 - c), 6, 6, 6, (x, y, 1 - c)))
        recvs += [copy(half(0, 0), s, s, s, (x, y, c)) for s in range(7)]
    return sends, recvs


def _send_start(gs, lands):
    n = len(gs)
    ps = gs

    def body(*refs):
        sends, _ = _direct_grad_copies(refs[:n], refs[n:2 * n], refs[2 * n], refs[2 * n + 1])
        for cp in sends:
            cp.start()
        refs[-1][...] = jnp.zeros_like(refs[-1])

    hbm = [pltpu.HBM(p.shape, p.dtype) for p in ps] + [pltpu.HBM(z.shape, z.dtype) for z in lands]
    out = _pc(body, name="grad_send_start",
              out_shape=(pltpu.SemaphoreType.DMA((7 * n,)), pltpu.SemaphoreType.DMA((7 * n,)), *hbm, S((8, 128), F32)),
              in_specs=[HBM_SPEC] * (2 * n),
              out_specs=(SEM_SPEC, SEM_SPEC, *([HBM_SPEC] * (2 * n)), pl.BlockSpec(memory_space=pltpu.VMEM)),
              input_output_aliases={i: 2 + i for i in range(2 * n)},
              compiler_params=pltpu.CompilerParams(has_side_effects=SPLIT_EFFECT))(
                  *[_in_hbm(p) for p in ps], *[_in_hbm(z) for z in lands])
    return out[0], out[1], out[2:2 + n], out[2 + n:2 + 2 * n], out[-1]


def _send_wait(s_sem, r_sem, ps, lands, after):
    n = len(ps)

    def body(*refs):
        sends, recvs = _direct_grad_copies(refs[:n], refs[n:2 * n], refs[2 * n], refs[2 * n + 1])
        for cp in sends:
            cp.wait_send()
        for cp in recvs:
            cp.wait_recv()

    hbm = [pltpu.HBM(p.shape, p.dtype) for p in ps] + [pltpu.HBM(z.shape, z.dtype) for z in lands]
    out = _pc(body, name="grad_send_wait", out_shape=tuple(hbm),
              in_specs=[HBM_SPEC] * (2 * n) + [SEM_SPEC, SEM_SPEC, ANY_SPEC], out_specs=tuple([HBM_SPEC] * (2 * n)),
              input_output_aliases={i: i for i in range(2 * n)},
              compiler_params=pltpu.CompilerParams(has_side_effects=SPLIT_EFFECT))(*ps, *lands, s_sem, r_sem, after)
    return out[:n], out[n:]


def _sum_direct(g, landed, me_arr, c_arr, buf, first_layer, name):
    _, nl, r2, cols = landed.shape

    def body(me_ref, c_ref, g_ref, l_ref, b_ref, o_ref):
        tot = g_ref[...].astype(F32)
        for s in range(7):
            tot = tot + l_ref[s].astype(F32)
        o_ref[...] = tot

    gs = pltpu.PrefetchScalarGridSpec(
        num_scalar_prefetch=2, grid=(nl,),
        in_specs=[pl.BlockSpec((None, None, r2, cols), lambda l, me, c: (me[0], l, c[0], 0)),
                  pl.BlockSpec((7, None, r2, cols), lambda l, me, c: (0, l, 0, 0)), ANY_SPEC],
        out_specs=pl.BlockSpec((None, r2, cols), lambda l, me, c: (first_layer + l, c[0], 0)))
    return _pc(body, name=name, grid_spec=gs, out_shape=S(buf.shape, F32), input_output_aliases={4: 0},
               compiler_params=_cp(1))(me_arr, c_arr, g, landed, buf)


def _sum_shards(part, landed, me_arr, c_arr, buf, first_layer, name):
    _, nl, r2, cols = landed.shape

    def body(me_ref, c_ref, p_ref, l_ref, b_ref, o_ref):
        o_ref[...] = ((p_ref[...].astype(F32) + l_ref[0].astype(F32)) + l_ref[1].astype(F32)) + l_ref[2].astype(F32)

    gs = pltpu.PrefetchScalarGridSpec(
        num_scalar_prefetch=2, grid=(nl,),
        in_specs=[pl.BlockSpec((None, None, r2, cols), lambda l, me, c: (me[0], l, 0, 0)),
                  pl.BlockSpec((3, None, r2, cols), lambda l, me, c: (0, l, 0, 0)), ANY_SPEC],
        out_specs=pl.BlockSpec((None, r2, cols), lambda l, me, c: (first_layer + l, c[0], 0)))
    return _pc(body, name=name, grid_spec=gs, out_shape=S(buf.shape, F32), input_output_aliases={4: 0},
               compiler_params=_cp(1))(me_arr, c_arr, part, landed, buf)


def _share_halves(bufs):
    n = len(bufs)

    def body(*refs):
        ins, outs = refs[:n], refs[n:2 * n]
        s_sem, r_sem = refs[2 * n:]
        x, y, c, _ = _place()
        cps = []
        for t in range(n):
            r2 = bufs[t].shape[1] // 2
            cps.append(pltpu.make_async_remote_copy(
                src_ref=ins[t].at[:, pl.ds(c * r2, r2), :], dst_ref=outs[t].at[:, pl.ds(c * r2, r2), :],
                send_sem=s_sem.at[t], recv_sem=r_sem.at[t], device_id=(x, y, 1 - c), device_id_type=MESH))
            cps[-1].start()
        for cp in cps:
            cp.wait_recv()
        for cp in cps:
            cp.wait_send()

    return _pc(body, name="grad_share_halves", in_specs=[HBM_SPEC] * n, out_specs=[HBM_SPEC] * n,
               out_shape=[S(b.shape, b.dtype) for b in bufs], input_output_aliases={t: t for t in range(n)},
               scratch_shapes=[pltpu.SemaphoreType.DMA((n,))] * 2, compiler_params=_comm_params())(*bufs)


def _gather_small(v):
    nr = v.shape[0]

    def body(v_ref, out_ref, send_sems, recv_sems, local_sem):
        x, y, c, chips = _place()
        me, sibling = (x, y, c), (x, y, 1 - c)

        def rows(px, py, pc):
            return out_ref.at[pl.ds((4 * px + 2 * py + pc) * nr, nr), :]

        def copy(k, block, to, src=None):
            return pltpu.make_async_remote_copy(src_ref=rows(*block) if src is None else src, dst_ref=rows(*block),
                                                send_sem=send_sems.at[k], recv_sem=recv_sems.at[k], device_id=to,
                                                device_id_type=MESH)

        mine = pltpu.make_async_copy(v_ref, rows(*me), local_sem)
        mine.start()
        first = [copy(0, me, sibling, src=v_ref)]
        first += [copy(1 + j, me, (*chip, c), src=v_ref) for j, chip in enumerate(chips)]
        for cp in first:
            cp.start()
        passed = [copy(4 + j, (*chip, c), sibling) for j, chip in enumerate(chips)]
        for j, chip in enumerate(chips):
            copy(1 + j, (*chip, c), me).wait_recv()
            passed[j].start()
        copy(0, sibling, me).wait_recv()
        for j, chip in enumerate(chips):
            copy(4 + j, (*chip, 1 - c), me).wait_recv()
        for cp in first + passed:
            cp.wait_send()
        mine.wait()

    vm = pl.BlockSpec(memory_space=pltpu.VMEM)
    return _pc(body, name="gather_small_grads", in_specs=[vm], out_specs=vm, out_shape=S((8 * nr, 128), F32),
               scratch_shapes=[pltpu.SemaphoreType.DMA((7,)), pltpu.SemaphoreType.DMA((7,)), pltpu.SemaphoreType.DMA],
               compiler_params=_comm_params())(v)


def _adamw_math(w, g, m, v):
    m2 = ADAM_B1 * m + (1.0 - ADAM_B1) * g
    v2 = ADAM_B2 * v + (1.0 - ADAM_B2) * (g * g)
    m_hat = m2 / (1.0 - ADAM_B1 ** ADAM_STEP)
    v_hat = v2 / (1.0 - ADAM_B2 ** ADAM_STEP)
    return -ADAM_LR * (m_hat / (jnp.sqrt(v_hat) + ADAM_EPS) + ADAM_WD * w), m2, v2


def _adamw(w, g, m, v, name):
    L, R, C = w.shape
    rb = R // 2 if R >= 512 else R

    def body(w_ref, g_ref, m_ref, v_ref, d_ref, m2_ref, v2_ref):
        d_ref[...], m2_ref[...], v2_ref[...] = _adamw_math(w_ref[...], g_ref[...], m_ref[...], v_ref[...])

    blk = pl.BlockSpec((None, rb, C), lambda l, r: (l, r, 0))
    return _pc(body, name=name, grid=(L, R // rb), in_specs=[blk] * 4, out_specs=[blk] * 3,
               out_shape=[S(w.shape, F32)] * 3, compiler_params=_cp(2))(w, g, m, v)


def _adamw_small(gathered, w, m, v):
    nr = w.shape[0]
    rb = nr // 5

    def body(a_ref, w_ref, m_ref, v_ref, g_ref, d_ref, m2_ref, v2_ref):
        g = a_ref[0]
        for k in range(1, 8):
            g = g + a_ref[k]
        g_ref[...] = g
        d_ref[...], m2_ref[...], v2_ref[...] = _adamw_math(w_ref[...], g, m_ref[...], v_ref[...])

    blk = pl.BlockSpec((rb, 128), lambda i: (i, 0))
    return _pc(body, name="adamw_small", grid=(nr // rb,), in_specs=[pl.BlockSpec((8, rb, 128), lambda i: (0, i, 0))] + [blk] * 3,
               out_specs=[blk] * 4, out_shape=[S((nr, 128), F32)] * 4, compiler_params=_cp(1))(gathered, w, m, v)


SMALL_ROWS = 4520


def _pack(arrs):
    flat = jnp.concatenate([a.reshape(-1) for a in arrs])
    return jnp.pad(flat, (0, SMALL_ROWS * 128 - flat.shape[0])).reshape(SMALL_ROWS, 128)


def _unpack(packed, like):
    flat = packed.reshape(-1)
    out, off = [], 0
    for a in like:
        out.append(flat[off:off + a.size].reshape(a.shape))
        off += a.size
    return out


def kernel(x, p, positions, ffn1_pre_g, ffn1_w_gate, ffn1_w_up, ffn1_w_down, ffn1_post_g, mix_pre_g, w_in, attn_norm_g, ssm_lam_re, ssm_lam_im, ssm_log_dt, ssm_b_re, ssm_b_im, ssm_c_re, ssm_c_im, ssm_d, ssm_w_glu, ssm_b_glu, ssm_norm_g, w_out, mix_post_g, ffn2_pre_g, ffn2_w_gate, ffn2_w_up, ffn2_w_down, ffn2_post_g, ple_w_up, ple_w_gate, ple_post_g, loss_target, m_ffn1_pre_g, m_ffn1_w_gate, m_ffn1_w_up, m_ffn1_w_down, m_ffn1_post_g, m_mix_pre_g, m_w_in, m_attn_norm_g, m_ssm_lam_re, m_ssm_lam_im, m_ssm_log_dt, m_ssm_b_re, m_ssm_b_im, m_ssm_c_re, m_ssm_c_im, m_ssm_d, m_ssm_w_glu, m_ssm_b_glu, m_ssm_norm_g, m_w_out, m_mix_post_g, m_ffn2_pre_g, m_ffn2_w_gate, m_ffn2_w_up, m_ffn2_w_down, m_ffn2_post_g, m_ple_w_up, m_ple_w_gate, m_ple_post_g, v_ffn1_pre_g, v_ffn1_w_gate, v_ffn1_w_up, v_ffn1_w_down, v_ffn1_post_g, v_mix_pre_g, v_w_in, v_attn_norm_g, v_ssm_lam_re, v_ssm_lam_im, v_ssm_log_dt, v_ssm_b_re, v_ssm_b_im, v_ssm_c_re, v_ssm_c_im, v_ssm_d, v_ssm_w_glu, v_ssm_b_glu, v_ssm_norm_g, v_w_out, v_mix_post_g, v_ffn2_pre_g, v_ffn2_w_gate, v_ffn2_w_up, v_ffn2_w_down, v_ffn2_post_g, v_ple_w_up, v_ple_w_gate, v_ple_post_g):
    a = dict(locals())
    T = x.shape[1]
    big_names = [n for n, _, _ in BIG]
    for n in TRANSPOSED:
        for pre in ("", "m_", "v_"):
            a[pre + n] = jnp.swapaxes(a[pre + n], 1, 2)

    own = [a[n].astype(BF16) for n in big_names]
    n_layers = own[0].shape[0]
    per_layer = [[w[l:l + 1] for w in own] for l in range(n_layers)]
    c_arr = lax.axis_index("c").astype(jnp.int32).reshape(1)
    me_arr = (2 * lax.axis_index("x") + lax.axis_index("y")).astype(jnp.int32).reshape(1)
    first = dict(zip(big_names, _gather_weights(per_layer[0], _place_own(per_layer[0], me_arr, 0))))
    pending, anchor, queued_behind = {}, jnp.zeros((), F32), first[big_names[0]]
    for l in range(1, n_layers):
        s_sem, r_sem, ws_thru, lands_thru, token = _gather_start(per_layer[l], _place_own(per_layer[l], me_arr, l),
                                                                 queued_behind, l)
        pending[l] = (s_sem, r_sem, ws_thru, lands_thru)
        anchor = anchor + token[0, 0]
        queued_behind = token

    def weights_of(l, after):
        if l == 0:
            return first
        return dict(zip(big_names, _gather_wait(*pending[l], after, l)))

    Sm = {n: a[n] for n in SMALL}
    Sm["ffn1_pre_g"] = Sm["ffn1_pre_g"] + anchor

    pos = jnp.broadcast_to(positions.reshape(1, T, 1).astype(F32), (1, T, 128))
    def chip_partials(G, tag):
        gs = [G[n] for n in big_names]
        landed = _swap_halves(gs, tag)
        return [_add_half(g, la, c_arr, f"grad_add_half_{tag}_{n}") for g, la, n in zip(gs, landed, big_names)]

    upper = {}

    def upper_grads_done(G_upper):
        gs = [G_upper[n] for n in big_names]
        lands = [lax.empty((7, g.shape[1], g.shape[2] // 2, g.shape[3]), BF16) for g in gs]
        s_sem, r_sem, gs_thru, lands_thru, token = _send_start(gs, lands)
        upper["pending"] = (s_sem, r_sem, gs_thru, lands_thru)
        return token[0, 0]

    loss, gx, G_upper, G_first, small = _local_step(
        _reorder(x, True, "to_streams_x")[0], _reorder(p[:, 0], True, "to_streams_p"),
        _reorder(pos, True, "to_streams_pos")[0, :, :1], _reorder(loss_target, True, "to_streams_target")[0],
        weights_of, upper_grads_done, Sm)
    gx = _reorder(gx[None], False, "to_time_grad_x")

    bufs = [lax.empty((n_layers, r, c), F32) for _, r, c in BIG]
    if n_layers > 1:
        gs, landed = _send_wait(*upper["pending"], gx)
        bufs = [_sum_direct(g, la, me_arr, c_arr, b, 1, f"grad_sum_direct_upper_{n}")
                for g, la, b, n in zip(gs, landed, bufs, big_names)]
    parts = chip_partials(G_first, "first")
    landed = _send_shards(parts)
    bufs = [_sum_shards(pt, la, me_arr, c_arr, b, 0, f"grad_sum_shards_first_{n}")
            for pt, la, b, n in zip(parts, landed, bufs, big_names)]
    grads = dict(zip(big_names, _share_halves(bufs)))

    small_g = _gather_small(_pack([small[n] for n in SMALL])).reshape(8, SMALL_ROWS, 128)
    sg, sd, sm, sv = _adamw_small(small_g, _pack([a[n] for n in SMALL]), _pack([a["m_" + n] for n in SMALL]),
                                  _pack([a["v_" + n] for n in SMALL]))
    like = [a[n] for n in SMALL]
    res = {}
    for n, g_, d_, m_, v_ in zip(SMALL, _unpack(sg, like), _unpack(sd, like), _unpack(sm, like), _unpack(sv, like)):
        res[n] = (g_, d_, m_, v_)
    for n in big_names:
        d_, m_, v_ = _adamw(a[n], grads[n], a["m_" + n], a["v_" + n], f"adamw_{n}")
        res[n] = (grads[n], d_, m_, v_)
        if n in TRANSPOSED:
            res[n] = tuple(jnp.swapaxes(t, 1, 2) for t in res[n])

    total = lax.psum(loss[0, 0], ("x", "y", "c"))
    return (total, gx, *[res[n][0] for n in WEIGHTS], *[res[n][1] for n in WEIGHTS],
            *[res[n][2] for n in WEIGHTS], *[res[n][3] for n in WEIGHTS])
```

```python
import functools
import math

import numpy as np
import jax
import jax.numpy as jnp
from jax import lax
from jax.experimental import pallas as pl
from jax.experimental.pallas import tpu as pltpu

F32 = jnp.float32
BF16 = jnp.bfloat16
S = jax.ShapeDtypeStruct
MESH = pl.DeviceIdType.MESH

D = 1024
DA = 512
DSS = 512
HD = 64
NH = 8
BAND = 128
NSH = 4
DFS = 704
PLE = 256
EPS = 1e-6
ROPE_THETA = 500000.0
PATTERN_DILATIONS = (1, 4, 16)
NLB = 16
ADAM_LR, ADAM_B1, ADAM_B2, ADAM_EPS, ADAM_WD, ADAM_STEP = 0.001, 0.9, 0.999, 1e-08, 0.01, 10

VMEM_LIMIT = 56 * 1024 * 1024
TM = 512
TMB = 256

BIG = (
    ("ffn1_w_gate", DFS, D), ("ffn1_w_up", DFS, D), ("ffn1_w_down", DFS, D),
    ("w_in", D, 512), ("ssm_w_glu", 128, 512), ("w_out", 256, D),
    ("ffn2_w_gate", DFS, D), ("ffn2_w_up", DFS, D), ("ffn2_w_down", DFS, D),
    ("ple_w_up", PLE, 256), ("ple_w_gate", 256, D),
)
TRANSPOSED = ("ffn1_w_gate", "ffn1_w_up", "ffn2_w_gate", "ffn2_w_up")
SMALL = ("ffn1_pre_g", "ffn1_post_g", "mix_pre_g", "attn_norm_g", "ssm_lam_re", "ssm_lam_im", "ssm_log_dt",
         "ssm_b_re", "ssm_b_im", "ssm_c_re", "ssm_c_im", "ssm_d", "ssm_b_glu", "ssm_norm_g", "mix_post_g",
         "ffn2_pre_g", "ffn2_post_g", "ple_post_g")
WEIGHTS = ("ffn1_pre_g", "ffn1_w_gate", "ffn1_w_up", "ffn1_w_down", "ffn1_post_g", "mix_pre_g", "w_in", "attn_norm_g",
           "ssm_lam_re", "ssm_lam_im", "ssm_log_dt", "ssm_b_re", "ssm_b_im", "ssm_c_re", "ssm_c_im", "ssm_d",
           "ssm_w_glu", "ssm_b_glu", "ssm_norm_g", "w_out", "mix_post_g", "ffn2_pre_g", "ffn2_w_gate", "ffn2_w_up",
           "ffn2_w_down", "ffn2_post_g", "ple_w_up", "ple_w_gate", "ple_post_g")


def _pc(body, **kw):
    return pl.pallas_call(body, **kw)


def _cp(n_grid):
    return pltpu.CompilerParams(dimension_semantics=("arbitrary",) * n_grid, vmem_limit_bytes=VMEM_LIMIT)


def _dot(a, b):
    return jnp.dot(a, b, preferred_element_type=F32)


def _dot_nt(a, b):
    return lax.dot_general(a, b, (((1,), (1,)), ((), ())), preferred_element_type=F32)


def _dot_tn(a, b):
    return lax.dot_general(a, b, (((0,), (0,)), ((), ())), preferred_element_type=F32)


def _split(a):
    hi = a.astype(BF16)
    return hi, (a - hi.astype(F32)).astype(BF16)


def _dot3(fn, a, b):
    ah, al = _split(a)
    bh, bl = _split(b)
    return fn(ah, bh) + fn(ah, bl) + fn(al, bh)


def _rms_fwd(x, g):
    r = lax.rsqrt(jnp.mean(x * x, axis=-1, keepdims=True) + EPS)
    return x * r * g


def _rms_bwd(dy, x, g):
    r = lax.rsqrt(jnp.mean(x * x, axis=-1, keepdims=True) + EPS)
    xr = x * r
    gd = dy * g
    dx = r * (gd - xr * jnp.mean(gd * xr, axis=-1, keepdims=True))
    dg = jnp.sum(dy * xr, axis=0, keepdims=True)
    return dx, dg


def _gelu(y):
    k = math.sqrt(2.0 / math.pi)
    return 0.5 * y * (1.0 + jnp.tanh(k * (y + 0.044715 * y * y * y)))


def _gelu_grad(y):
    k = math.sqrt(2.0 / math.pi)
    t = jnp.tanh(k * (y + 0.044715 * y * y * y))
    return 0.5 * (1.0 + t) + 0.5 * y * (1.0 - t * t) * k * (1.0 + 3 * 0.044715 * y * y)


def _gain_spec(n, layer):
    return pl.BlockSpec((None, 1, n), lambda *_: (layer, 0, 0))


def _row_acc_spec(n):
    return pl.BlockSpec((1, n), lambda *_: (0, 0))


def _rot_tables(pos_col):
    T = pos_col.shape[0]
    half = HD // 8
    inv = (ROPE_THETA ** (-np.arange(half, dtype=np.float32) * (2.0 / (2 * half)))).astype(np.float32)
    lane_freq = np.tile(np.concatenate([inv, inv, np.zeros(HD - 2 * half, np.float32)]), NH)[None, :]

    def body(p_ref, f_ref, c_ref, s1_ref, s2_ref):
        ang = p_ref[...] * f_ref[...]
        d = lax.broadcasted_iota(jnp.int32, ang.shape, 1) % HD
        cs = jnp.cos(ang)
        sn = jnp.sin(ang)
        c_ref[...] = jnp.where(d < 2 * half, cs, 1.0)
        s1_ref[...] = jnp.where(d < half, -sn, 0.0)
        s2_ref[...] = jnp.where((d >= half) & (d < 2 * half), sn, 0.0)

    tm = TM
    return _pc(body, name="rot_tables", grid=(T // tm,),
               in_specs=[pl.BlockSpec((tm, 1), lambda i: (i, 0)), pl.BlockSpec((1, DA), lambda i: (0, 0))],
               out_specs=[pl.BlockSpec((tm, DA), lambda i: (i, 0))] * 3,
               out_shape=[S((T, DA), F32)] * 3, compiler_params=_cp(1))(pos_col, jnp.asarray(lane_freq))


def _rot_fwd(t, c, s1, s2):
    return t * c + pltpu.roll(t, DA - 8, 1) * s1 + pltpu.roll(t, 8, 1) * s2


def _rot_bwd(g, c, s1, s2):
    return g * c + pltpu.roll(g * s1, 8, 1) + pltpu.roll(g * s2, DA - 8, 1)


def _ffn_weight_spec():
    return pl.BlockSpec((NSH, None, DFS, D), lambda i: (0, 0, 0, 0), pipeline_mode=pl.Buffered(1))


def _ffn_fwd(h, pre_g, post_g, wg, wu, wd, layer, tag):
    T = h.shape[0]
    tm = TM
    nt = T // tm

    def body(h_ref, pg_ref, qg_ref, wg_ref, wu_ref, wd_ref, ho_ref, a_ref, b_ref, f_ref, xn_ref):
        hv = h_ref[...]
        xb = _rms_fwd(hv, pg_ref[...]).astype(BF16)
        xn_ref[...] = xb
        f = None
        for j in range(NSH):
            ab = _dot_nt(xb, wg_ref[j]).astype(BF16)
            bb = _dot_nt(xb, wu_ref[j]).astype(BF16)
            a_ref[j] = ab
            b_ref[j] = bb
            a = ab.astype(F32)
            hh = (a * jax.nn.sigmoid(a) * bb.astype(F32)).astype(BF16)
            part = _dot(hh, wd_ref[j])
            f = part if f is None else f + part
        f_ref[...] = f
        ho_ref[...] = hv + 0.5 * _rms_fwd(f, qg_ref[...])

    row = pl.BlockSpec((tm, D), lambda i: (i, 0))
    act = pl.BlockSpec((NSH, tm, DFS), lambda i: (0, i, 0))
    return _pc(body, name=f"ffn_fwd_{tag}_l{layer}", grid=(nt,),
               in_specs=[row, _gain_spec(D, layer), _gain_spec(D, layer)] + [_ffn_weight_spec()] * 3,
               out_specs=[row, act, act, row, row],
               out_shape=[S((T, D), F32), S((NSH, T, DFS), BF16), S((NSH, T, DFS), BF16), S((T, D), F32), S((T, D), BF16)],
               compiler_params=_cp(1))(h, pre_g, post_g, wg, wu, wd)


def _ffn_bwd(dout, h, f, a, b, pre_g, post_g, wg, wu, wd, layer, tag):
    T = h.shape[0]
    tm = TMB
    nt = T // tm

    def body(do_ref, h_ref, f_ref, a_ref, b_ref, pg_ref, qg_ref, wg_ref, wu_ref, wd_ref,
             dh_ref, df_ref, da_ref, db_ref, hh_ref, dpg_ref, dqg_ref):
        @pl.when(pl.program_id(0) == 0)
        def _():
            dpg_ref[...] = jnp.zeros_like(dpg_ref)
            dqg_ref[...] = jnp.zeros_like(dqg_ref)

        do = do_ref[...]
        df, dq = _rms_bwd(0.5 * do, f_ref[...], qg_ref[...])
        dqg_ref[...] += dq
        dfb = df.astype(BF16)
        df_ref[...] = dfb
        dxn = None
        for j in range(NSH):
            dhh = _dot_nt(dfb, wd_ref[j])
            av = a_ref[j].astype(F32)
            bv = b_ref[j].astype(F32)
            sg = jax.nn.sigmoid(av)
            sa = av * sg
            hh_ref[j] = (sa * bv).astype(BF16)
            dab = (dhh * bv * (sg + sa * (1.0 - sg))).astype(BF16)
            dbb = (dhh * sa).astype(BF16)
            da_ref[j] = dab
            db_ref[j] = dbb
            part = _dot(dab, wg_ref[j]) + _dot(dbb, wu_ref[j])
            dxn = part if dxn is None else dxn + part
        dx, dp = _rms_bwd(dxn, h_ref[...], pg_ref[...])
        dpg_ref[...] += dp
        dh_ref[...] = do + dx

    row = pl.BlockSpec((tm, D), lambda i: (i, 0))
    act = pl.BlockSpec((NSH, tm, DFS), lambda i: (0, i, 0))
    return _pc(body, name=f"ffn_bwd_{tag}_l{layer}", grid=(nt,),
               in_specs=[row, row, row, act, act, _gain_spec(D, layer), _gain_spec(D, layer)] + [_ffn_weight_spec()] * 3,
               out_specs=[row, row, act, act, act, _row_acc_spec(D), _row_acc_spec(D)],
               out_shape=[S((T, D), F32), S((T, D), BF16), S((NSH, T, DFS), BF16), S((NSH, T, DFS), BF16),
                          S((NSH, T, DFS), BF16), S((1, D), F32), S((1, D), F32)],
               compiler_params=_cp(1))(dout, h, f, a, b, pre_g, post_g, wg, wu, wd)


def _dw(A, B, buf, layer, kb, nb, a_mode, b_mode, name):
    T = A.shape[1]
    tt = TM
    nt = T // tt

    def pick(v, mode, j, w):
        if mode == "shard":
            return v[j]
        return v[0] if mode == "whole" else v[0][:, j * w:(j + 1) * w]

    def body(a_ref, b_ref, buf_ref, o_ref, acc):
        t = pl.program_id(0)

        @pl.when(t == 0)
        def _():
            acc[...] = jnp.zeros_like(acc)

        av = a_ref[...].astype(BF16)
        bv = b_ref[...].astype(BF16)
        for j in range(NSH):
            acc[j] += _dot_tn(pick(av, a_mode, j, kb), pick(bv, b_mode, j, nb))

        @pl.when(t == nt - 1)
        def _():
            o_ref[...] = acc[...].astype(o_ref.dtype)

    return _pc(body, name=name, grid=(nt,),
               in_specs=[pl.BlockSpec((A.shape[0], tt, A.shape[2]), lambda t: (0, t, 0)),
                         pl.BlockSpec((B.shape[0], tt, B.shape[2]), lambda t: (0, t, 0)),
                         pl.BlockSpec(memory_space=pl.ANY)],
               out_specs=pl.BlockSpec((NSH, None, kb, nb), lambda t: (0, layer, 0, 0)),
               out_shape=S(buf.shape, buf.dtype), input_output_aliases={2: 0},
               scratch_shapes=[pltpu.VMEM((NSH, kb, nb), F32)], compiler_params=_cp(1))(A, B, buf)


def _mix_proj(h, pre_g, win, rot, layer):
    T = h.shape[0]
    tm = TM

    def body(h_ref, g_ref, w_ref, c_ref, s1_ref, s2_ref, p_ref, xn_ref):
        xb = _rms_fwd(h_ref[...], g_ref[...]).astype(BF16)
        xn_ref[...] = xb
        for j in range(NSH):
            o = _dot(xb, w_ref[j])
            p_ref[j] = _rot_fwd(o, c_ref[...], s1_ref[...], s2_ref[...]) if j < 2 else o

    row = pl.BlockSpec((tm, D), lambda i: (i, 0))
    half = pl.BlockSpec((tm, DA), lambda i: (i, 0))
    return _pc(body, name=f"mix_proj_l{layer}", grid=(T // tm,),
               in_specs=[row, _gain_spec(D, layer), pl.BlockSpec((NSH, None, D, DA), lambda i: (0, 0, 0, 0)),
                         half, half, half],
               out_specs=[pl.BlockSpec((NSH, tm, DA), lambda i: (0, i, 0)), row],
               out_shape=[S((NSH, T, DA), F32), S((T, D), BF16)], compiler_params=_cp(1))(h, pre_g, win, *rot)


def _mix_proj_bwd(dq, dk, dv, du, dh_up, h, pre_g, win, rot, layer):
    T = h.shape[0]
    tm = TM

    def body(dq_ref, dk_ref, dv_ref, du_ref, up_ref, h_ref, g_ref, w_ref, c_ref, s1_ref, s2_ref,
             dh_ref, dp_ref, dg_ref):
        @pl.when(pl.program_id(0) == 0)
        def _():
            dg_ref[...] = jnp.zeros_like(dg_ref)

        rot = (c_ref[...], s1_ref[...], s2_ref[...])
        dps = [_rot_bwd(dq_ref[...], *rot), _rot_bwd(dk_ref[...], *rot), dv_ref[...], du_ref[...]]
        dxn = None
        for j in range(NSH):
            dpb = dps[j].astype(BF16)
            dp_ref[j] = dpb
            part = _dot_nt(dpb, w_ref[j])
            dxn = part if dxn is None else dxn + part
        dx, dg = _rms_bwd(dxn, h_ref[...], g_ref[...])
        dg_ref[...] += dg
        dh_ref[...] = up_ref[...] + dx

    row = pl.BlockSpec((tm, D), lambda i: (i, 0))
    half = pl.BlockSpec((tm, DA), lambda i: (i, 0))
    return _pc(body, name=f"mix_proj_bwd_l{layer}", grid=(T // tm,),
               in_specs=[half, half, half, half, row, row, _gain_spec(D, layer),
                         pl.BlockSpec((NSH, None, D, DA), lambda i: (0, 0, 0, 0)), half, half, half],
               out_specs=[row, pl.BlockSpec((NSH, tm, DA), lambda i: (0, i, 0)), _row_acc_spec(D)],
               out_shape=[S((T, D), F32), S((NSH, T, DA), BF16), S((1, D), F32)],
               compiler_params=_cp(1))(dq, dk, dv, du, dh_up, h, pre_g, win, *rot)


def _stream_pos(d, axis):
    i = lax.broadcasted_iota(jnp.int32, (BAND, BAND), axis)
    if d == 16:
        return i
    if d == 4:
        return 4 * (i % 32) + i // 32
    return 16 * (i % 8) + i // 8


def _band_masks(b, d):
    qi, kj = _stream_pos(d, 0), _stream_pos(d, 1)
    return kj <= qi, (kj >= qi) & (b > 0)


def _pattern(d, T):
    n16 = T // 16
    if d == 16:
        return (16, n16, DA), (None, BAND, DA), lambda r, k: (r, k, 0)
    if d == 4:
        return (4, 4, n16, DA), (4, None, 32, DA), lambda r, k: (0, r, k, 0)
    return (16, n16, DA), (16, 8, DA), lambda r, k: (0, k, 0)


def _pattern_spec(d, T, kmap, lead=None):
    _, blk, idx = _pattern(d, T)
    if lead is None:
        return pl.BlockSpec(blk, lambda r, b: idx(r, kmap(b)))
    return pl.BlockSpec((None,) + blk, lambda r, b: (lead,) + idx(r, kmap(b)))


def _whole_stream_specs(T, n_plain):
    n16 = T // 16
    p_spec = lambda s: pl.BlockSpec((None, None, n16, DA), lambda r: (s, r, 0, 0))
    plain = pl.BlockSpec((None, n16, DA), lambda r: (r, 0, 0))
    return [p_spec(0), p_spec(1), p_spec(2)] + [plain] * n_plain, plain


def _stream_masks():
    qi = lax.broadcasted_iota(jnp.int32, (BAND, BAND), 0)
    kj = lax.broadcasted_iota(jnp.int32, (BAND, BAND), 1)
    mask_c = kj <= qi
    return mask_c, jnp.concatenate([kj >= qi, mask_c], axis=1)


def _attn_fwd_stream(P, layer):
    T = P.shape[1]
    n16 = T // 16
    nb = n16 // BAND
    scale = HD ** -0.5

    def body(q_ref, k_ref, v_ref, o_ref, l_ref, qs, ks, vs):
        for src, dst in ((q_ref, qs), (k_ref, ks), (v_ref, vs)):
            dst[...] = src[...].astype(BF16)
        mask_c, mask_pc = _stream_masks()
        for b in range(nb):
            rows = slice(b * BAND, (b + 1) * BAND)
            krows = slice(max(b - 1, 0) * BAND, (b + 1) * BAND)
            mask = mask_c if b == 0 else mask_pc
            for hd in range(NH):
                sl = slice(hd * HD, (hd + 1) * HD)
                s = jnp.where(mask, _dot_nt(qs[rows, sl], ks[krows, sl]) * scale, -1e30)
                m = jnp.max(s, axis=-1, keepdims=True)
                e = jnp.exp(s - m)
                den = jnp.sum(e, axis=-1, keepdims=True)
                o_ref[rows, sl] = _dot(e.astype(BF16), vs[krows, sl]) / den
                l_ref[rows, sl] = jnp.broadcast_to(m + jnp.log(den), (BAND, HD))

    ins, out = _whole_stream_specs(T, 0)
    Pv = P.reshape(NSH, 16, n16, DA)
    o, l = _pc(body, name=f"attn_fwd_d16_l{layer}", grid=(16,), in_specs=ins, out_specs=[out, out],
               out_shape=[S((16, n16, DA), F32)] * 2, scratch_shapes=[pltpu.VMEM((n16, DA), BF16)] * 3,
               compiler_params=_cp(1))(Pv, Pv, Pv)
    return o.reshape(T, DA), l.reshape(T, DA)


def _attn_bwd_stream(P, dO, lse, delta, acc, layer):
    T = P.shape[1]
    n16 = T // 16
    nb = n16 // BAND
    scale = HD ** -0.5
    first = acc is None

    def body(*refs):
        q_ref, k_ref, v_ref, do_ref, l_ref, dl_ref = refs[:6]
        if first:
            dq_ref, dk_ref, dv_ref = refs[6:9]
        else:
            aq_ref, ak_ref, av_ref, dq_ref, dk_ref, dv_ref = refs[6:12]
        qs, ks, vs, dos, okf, ovf = refs[-6:]
        for src, dst in ((q_ref, qs), (k_ref, ks), (v_ref, vs), (do_ref, dos)):
            dst[...] = src[...].astype(BF16)
        okf[...] = jnp.zeros_like(okf)
        ovf[...] = jnp.zeros_like(ovf)
        mask_c, mask_pc = _stream_masks()
        for b in range(nb):
            rows = slice(b * BAND, (b + 1) * BAND)
            krows = slice(max(b - 1, 0) * BAND, (b + 1) * BAND)
            mask = mask_c if b == 0 else mask_pc
            for hd in range(NH):
                sl = slice(hd * HD, (hd + 1) * HD)
                one = slice(hd * HD, hd * HD + 1)
                q, do, kk = qs[rows, sl], dos[rows, sl], ks[krows, sl]
                p = jnp.where(mask, jnp.exp(_dot_nt(q, kk) * scale - l_ref[rows, one]), 0.0)
                ds = (p * (_dot_nt(do, vs[krows, sl]) - dl_ref[rows, one]) * scale).astype(BF16)
                dq = _dot(ds, kk)
                dq_ref[rows, sl] = dq if first else aq_ref[rows, sl] + dq
                okf[krows, sl] += _dot_tn(ds, q)
                ovf[krows, sl] += _dot_tn(p.astype(BF16), do)
        dk_ref[...] = okf[...] if first else ak_ref[...] + okf[...]
        dv_ref[...] = ovf[...] if first else av_ref[...] + ovf[...]

    ins, out = _whole_stream_specs(T, 3 if first else 6)
    Pv = P.reshape(NSH, 16, n16, DA)
    view = lambda t: t.reshape(16, n16, DA)
    args = [Pv, Pv, Pv, view(dO), view(lse), view(delta)] + ([] if first else [view(t) for t in acc])
    dq, dk, dv = _pc(body, name=f"attn_bwd_d16_l{layer}", grid=(16,), in_specs=ins, out_specs=[out, out, out],
                     out_shape=[S((16, n16, DA), F32)] * 3,
                     scratch_shapes=[pltpu.VMEM((n16, DA), BF16)] * 4 + [pltpu.VMEM((n16, DA), F32)] * 2,
                     compiler_params=_cp(1))(*args)
    return dq.reshape(T, DA), dk.reshape(T, DA), dv.reshape(T, DA)


def _attn_fwd(P, d, layer):
    if d == 16:
        return _attn_fwd_stream(P, layer)
    T = P.shape[1]
    nb = T // d // BAND
    vshape = _pattern(d, T)[0]
    Pv = P.reshape((NSH,) + vshape)
    scale = HD ** -0.5

    def body(q_ref, kp_ref, kc_ref, vp_ref, vc_ref, o_ref, l_ref, qs, ks, vs, osc, lsc):
        b = pl.program_id(1)
        flat = lambda ref: ref[...].reshape(BAND, DA).astype(BF16)
        qs[...] = flat(q_ref)
        ks[0:BAND, :] = flat(kp_ref)
        ks[BAND:, :] = flat(kc_ref)
        vs[0:BAND, :] = flat(vp_ref)
        vs[BAND:, :] = flat(vc_ref)
        mask_c, mask_p = _band_masks(b, d)
        mask = jnp.concatenate([mask_p, mask_c], axis=1)
        for hd in range(NH):
            sl = slice(hd * HD, (hd + 1) * HD)
            s = jnp.where(mask, _dot_nt(qs[:, sl], ks[:, sl]) * scale, -1e30)
            m = jnp.max(s, axis=-1, keepdims=True)
            e = jnp.exp(s - m)
            den = jnp.sum(e, axis=-1, keepdims=True)
            osc[:, sl] = _dot(e.astype(BF16), vs[:, sl]) / den
            lsc[:, sl] = jnp.broadcast_to(m + jnp.log(den), (BAND, HD))
        o_ref[...] = osc[...].reshape(o_ref.shape)
        l_ref[...] = lsc[...].reshape(l_ref.shape)

    cur = lambda b: b
    prev = lambda b: jnp.maximum(b - 1, 0)
    out = _pattern_spec(d, T, cur)
    o, l = _pc(body, name=f"attn_fwd_d{d}_l{layer}", grid=(d, nb),
               in_specs=[_pattern_spec(d, T, cur, 0), _pattern_spec(d, T, prev, 1), _pattern_spec(d, T, cur, 1),
                         _pattern_spec(d, T, prev, 2), _pattern_spec(d, T, cur, 2)],
               out_specs=[out, out], out_shape=[S(vshape, F32)] * 2,
               scratch_shapes=[pltpu.VMEM((BAND, DA), BF16)] + [pltpu.VMEM((2 * BAND, DA), BF16)] * 2
               + [pltpu.VMEM((BAND, DA), F32)] * 2,
               compiler_params=_cp(2))(Pv, Pv, Pv, Pv, Pv)
    return o.reshape(T, DA), l.reshape(T, DA)


def _attn_bwd(P, dO, lse, delta, acc, d, layer):
    if d == 16:
        return _attn_bwd_stream(P, dO, lse, delta, acc, layer)
    T = P.shape[1]
    nb = T // d // BAND
    vshape = _pattern(d, T)[0]
    Pv = P.reshape((NSH,) + vshape)
    scale = HD ** -0.5
    first = acc is None

    def body(*refs):
        q_ref, kp_ref, kc_ref, vp_ref, vc_ref, do_ref, l_ref, dl_ref = refs[:8]
        if first:
            dq_ref, dk_ref, dv_ref = refs[8:11]
        else:
            aq_ref, ak_ref, av_ref, dq_ref, dk_ref, dv_ref = refs[8:14]
        qs, dos, ks, vs, ls, dls, oq, ok, ov, ck, cv = refs[-11:]
        b = pl.program_id(1)
        flat = lambda ref: ref[...].reshape(BAND, DA)

        @pl.when(b == 0)
        def _():
            ck[...] = jnp.zeros_like(ck)
            cv[...] = jnp.zeros_like(cv)

        @pl.when(b < nb)
        def _():
            qs[...] = flat(q_ref).astype(BF16)
            dos[...] = flat(do_ref).astype(BF16)
            ks[0:BAND, :] = flat(kp_ref).astype(BF16)
            ks[BAND:, :] = flat(kc_ref).astype(BF16)
            vs[0:BAND, :] = flat(vp_ref).astype(BF16)
            vs[BAND:, :] = flat(vc_ref).astype(BF16)
            ls[...] = flat(l_ref)
            dls[...] = flat(dl_ref)
            mask_c, mask_p = _band_masks(b, d)
            mask = jnp.concatenate([mask_p, mask_c], axis=1)
            for hd in range(NH):
                sl = slice(hd * HD, (hd + 1) * HD)
                one = slice(hd * HD, hd * HD + 1)
                q, do, kk = qs[:, sl], dos[:, sl], ks[:, sl]
                p = jnp.where(mask, jnp.exp(_dot_nt(q, kk) * scale - ls[:, one]), 0.0)
                ds = (p * (_dot_nt(do, vs[:, sl]) - dls[:, one]) * scale).astype(BF16)
                oq[:, sl] = _dot(ds, kk)
                dk2 = _dot_tn(ds, q)
                dv2 = _dot_tn(p.astype(BF16), do)
                ok[:, sl] = ck[:, sl] + dk2[0:BAND]
                ov[:, sl] = cv[:, sl] + dv2[0:BAND]
                ck[:, sl] = dk2[BAND:]
                cv[:, sl] = dv2[BAND:]
            if first:
                dq_ref[...] = oq[...].reshape(dq_ref.shape)
                dk_ref[...] = ok[...].reshape(dk_ref.shape)
                dv_ref[...] = ov[...].reshape(dv_ref.shape)
            else:
                dq_ref[...] = aq_ref[...] + oq[...].reshape(dq_ref.shape)
                dk_ref[...] = ak_ref[...] + ok[...].reshape(dk_ref.shape)
                dv_ref[...] = av_ref[...] + ov[...].reshape(dv_ref.shape)

        @pl.when(b == nb)
        def _():
            if first:
                dk_ref[...] = ck[...].reshape(dk_ref.shape)
                dv_ref[...] = cv[...].reshape(dv_ref.shape)
            else:
                dk_ref[...] = ak_ref[...] + ck[...].reshape(dk_ref.shape)
                dv_ref[...] = av_ref[...] + cv[...].reshape(dv_ref.shape)

    qb = lambda b: jnp.minimum(b, nb - 1)
    qprev = lambda b: jnp.maximum(qb(b) - 1, 0)
    kb = lambda b: jnp.maximum(b - 1, 0)
    qrow = _pattern_spec(d, T, qb)
    krow = _pattern_spec(d, T, kb)
    view = lambda t: t.reshape(vshape)
    ins = [Pv, Pv, Pv, Pv, Pv, view(dO), view(lse), view(delta)]
    specs = [_pattern_spec(d, T, qb, 0), _pattern_spec(d, T, qprev, 1), _pattern_spec(d, T, qb, 1),
             _pattern_spec(d, T, qprev, 2), _pattern_spec(d, T, qb, 2), qrow, qrow, qrow]
    if not first:
        ins += [view(t) for t in acc]
        specs += [qrow, krow, krow]
    dq, dk, dv = _pc(body, name=f"attn_bwd_d{d}_l{layer}", grid=(d, nb + 1), in_specs=specs,
                     out_specs=[qrow, krow, krow], out_shape=[S(vshape, F32)] * 3,
                     scratch_shapes=[pltpu.VMEM((BAND, DA), BF16)] * 2 + [pltpu.VMEM((2 * BAND, DA), BF16)] * 2
                     + [pltpu.VMEM((BAND, DA), F32)] * 7,
                     compiler_params=_cp(2))(*ins)
    return dq.reshape(T, DA), dk.reshape(T, DA), dv.reshape(T, DA)


def _ssm_prep(lam_re, lam_im, log_dt, b_re, b_im, c_re, c_im):
    dt = jnp.exp(log_dt)[:, None]
    er = jnp.exp(lam_re * dt)
    a_re = er * jnp.cos(lam_im * dt)
    a_im = er * jnp.sin(lam_im * dt)
    nr, ni = a_re - 1.0, a_im
    den = lam_re * lam_re + lam_im * lam_im
    cr = (nr * lam_re + ni * lam_im) / den
    ci = (ni * lam_re - nr * lam_im) / den
    bbr = cr[..., None] * b_re - ci[..., None] * b_im
    bbi = cr[..., None] * b_im + ci[..., None] * b_re
    eye = jnp.eye(8, dtype=F32)

    def bblock(bb):
        t = bb.reshape(4, 8, 64, 16).transpose(0, 1, 3, 2)
        return (t[:, :, :, None, :] * eye[None, :, None, :, None]).reshape(4, 128, 512)

    def cblock(cc):
        t = cc.reshape(4, 8, 16, 64).transpose(0, 1, 3, 2)
        return (t[:, :, :, None, :] * eye[None, :, None, :, None]).reshape(4, 512, 128)

    return (a_re.reshape(NLB, 1, 128), a_im.reshape(NLB, 1, 128), bblock(bbr), bblock(bbi), cblock(c_re), cblock(c_im))


def _perm_matrix(tm):
    n = tm // 16
    pm = np.zeros((tm, tm), np.float32)
    for r in range(16):
        pm[16 * np.arange(n) + r, r * n + np.arange(n)] = 1.0
    return jnp.asarray(pm, BF16)


def _pieces(x):
    p1 = x.astype(BF16)
    r1 = x - p1.astype(F32)
    p2 = r1.astype(BF16)
    return p1, p2, (r1 - p2.astype(F32)).astype(BF16)


def _to_time(x, pm):
    return sum(_dot(pm, p) for p in _pieces(x))


def _to_streams(x, pm):
    return sum(_dot_tn(pm, p) for p in _pieces(x))


def _stream_block(tm, cols, lead=None):
    if lead is None:
        return pl.BlockSpec((16, tm // 16, cols), lambda i: (0, i, 0))
    return pl.BlockSpec((None, 16, tm // 16, cols), lambda i: (lead, 0, i, 0))


def _reorder(t3, to_streams, name):
    B, T, C = t3.shape
    tm = TM

    def body(x_ref, pm_ref, o_ref):
        if to_streams:
            o_ref[...] = _to_streams(x_ref[...], pm_ref[...]).reshape(o_ref.shape)
        else:
            o_ref[...] = _to_time(x_ref[...].reshape(tm, C), pm_ref[...])

    time_blk = pl.BlockSpec((None, tm, C), lambda b, i: (b, i, 0))
    stream_blk = pl.BlockSpec((None, 16, tm // 16, C), lambda b, i: (b, 0, i, 0))
    src = t3 if to_streams else t3.reshape(B, 16, T // 16, C)
    out = _pc(body, name=name, grid=(B, T // tm),
              in_specs=[time_blk if to_streams else stream_blk, pl.BlockSpec((tm, tm), lambda b, i: (0, 0))],
              out_specs=stream_blk if to_streams else time_blk,
              out_shape=S((B, 16, T // 16, C) if to_streams else (B, T, C), F32),
              compiler_params=_cp(2))(src, _perm_matrix(tm))
    return out.reshape(B, T, C)


def _ssm_in(P, bre, bim, layer):
    T = P.shape[1]
    tm = TM

    def body(u_ref, pm_ref, br_ref, bi_ref, un_ref, or_ref, oi_ref):
        u = _to_time(u_ref[...].reshape(tm, DSS), pm_ref[...])
        un_ref[...] = u
        for s in range(4):
            uc = u[:, s * 128:(s + 1) * 128]
            r = _dot3(_dot, uc, br_ref[s])
            m = _dot3(_dot, uc, bi_ref[s])
            for q in range(4):
                or_ref[4 * s + q] = r[:, q * 128:(q + 1) * 128]
                oi_ref[4 * s + q] = m[:, q * 128:(q + 1) * 128]

    whole = pl.BlockSpec((4, 128, 512), lambda i: (0, 0, 0))
    st = pl.BlockSpec((NLB, tm, 128), lambda i: (0, i, 0))
    return _pc(body, name=f"ssm_in_l{layer}", grid=(T // tm,),
               in_specs=[_stream_block(tm, DSS, 3), pl.BlockSpec((tm, tm), lambda i: (0, 0)), whole, whole],
               out_specs=[pl.BlockSpec((tm, DSS), lambda i: (i, 0)), st, st],
               out_shape=[S((T, DSS), F32)] + [S((NLB, T, 128), F32)] * 2,
               compiler_params=_cp(1))(P.reshape(NSH, 16, T // 16, DSS), _perm_matrix(tm), bre, bim)


def _scan(br, bi, a_re, a_im, reverse, layer):
    T = br.shape[1]
    nbk = 4
    tt = min(T, 1024)
    nT = T // tt
    ntile = tt // 8
    sgn = -1.0 if reverse else 1.0
    last = 0 if reverse else 7

    def body(br_ref, bi_ref, ar_ref, ai_ref, xr_ref, xi_ref, cr, ci):
        @pl.when(pl.program_id(1) == 0)
        def _():
            cr[...] = jnp.zeros_like(cr)
            ci[...] = jnp.zeros_like(ci)

        row = lax.broadcasted_iota(jnp.int32, (8, 128), 0)
        consts = []
        for k in range(nbk):
            a1r = jnp.broadcast_to(ar_ref[k], (8, 128))
            a1i = sgn * jnp.broadcast_to(ai_ref[k], (8, 128))
            pows = [(a1r, a1i)]
            for _ in range(7):
                pr, pi_ = pows[-1]
                pows.append((a1r * pr - a1i * pi_, a1r * pi_ + a1i * pr))
            rounds = []
            for s in (1, 2, 4):
                inside = (row <= 7 - s) if reverse else (row >= s)
                rounds.append((jnp.where(inside, pows[s - 1][0], 0.0), jnp.where(inside, pows[s - 1][1], 0.0)))
            cmr, cmi = jnp.zeros((8, 128), F32), jnp.zeros((8, 128), F32)
            for r in range(8):
                e = (7 - r) if reverse else r
                cmr = jnp.where(row == r, pows[e][0], cmr)
                cmi = jnp.where(row == r, pows[e][1], cmi)
            consts.append((rounds, cmr, cmi))

        def tile(i, carry):
            j = (ntile - 1 - i) if reverse else i
            rows = pl.ds(pl.multiple_of(j * 8, 8), 8)
            out = []
            for k in range(nbk):
                rounds, cmr, cmi = consts[k]
                xr = br_ref[k, rows, :]
                xi = bi_ref[k, rows, :]
                for (mr, mi), s in zip(rounds, (1, 2, 4)):
                    sh = (8 - s) if reverse else s
                    rr = pltpu.roll(xr, sh, 0)
                    ri = pltpu.roll(xi, sh, 0)
                    xr, xi = xr + (mr * rr - mi * ri), xi + (mr * ri + mi * rr)
                c_r, c_i = carry[k]
                xr, xi = xr + (cmr * c_r - cmi * c_i), xi + (cmr * c_i + cmi * c_r)
                xr_ref[k, rows, :] = xr
                xi_ref[k, rows, :] = xi
                out.append((jnp.broadcast_to(xr[last:last + 1, :], (8, 128)),
                            jnp.broadcast_to(xi[last:last + 1, :], (8, 128))))
            return tuple(out)

        carry = lax.fori_loop(0, ntile, tile, tuple((cr[k], ci[k]) for k in range(nbk)), unroll=2)
        for k in range(nbk):
            cr[k] = carry[k][0]
            ci[k] = carry[k][1]

    tmap = (lambda t: nT - 1 - t) if reverse else (lambda t: t)
    st = pl.BlockSpec((nbk, tt, 128), lambda i, t: (i, tmap(t), 0))
    av = pl.BlockSpec((nbk, 1, 128), lambda i, t: (i, 0, 0))
    return _pc(body, name=f"scan_{'bwd' if reverse else 'fwd'}_l{layer}", grid=(NLB // nbk, nT),
               in_specs=[st, st, av, av], out_specs=[st, st], out_shape=[S((NLB, T, 128), F32)] * 2,
               scratch_shapes=[pltpu.VMEM((nbk, 8, 128), F32)] * 2, compiler_params=_cp(2))(br, bi, a_re, a_im)


def _ssm_out(xr, xi, u, cre, cim, dvec, wglu, bglu, layer):
    T = u.shape[0]
    tm = TM

    def body(xr_ref, xi_ref, u_ref, pm_ref, cr_ref, ci_ref, d_ref, w_ref, bg_ref, s_ref, y_ref, z_ref):
        ys = []
        for s in range(4):
            xrc = jnp.concatenate([xr_ref[4 * s + q] for q in range(4)], axis=1)
            xic = jnp.concatenate([xi_ref[4 * s + q] for q in range(4)], axis=1)
            ys.append(_dot3(_dot, xrc, cr_ref[s]) - _dot3(_dot, xic, ci_ref[s]))
        y = jnp.concatenate(ys, axis=1) + d_ref[...] * u_ref[...]
        yg = _gelu(y)
        ygb = yg.astype(BF16)
        z = bg_ref[...] + sum(_dot(ygb[:, j * 128:(j + 1) * 128], w_ref[j]) for j in range(NSH))
        y_ref[...] = y
        z_ref[...] = z
        s_ref[...] = _to_streams(yg * jax.nn.sigmoid(z), pm_ref[...]).reshape(s_ref.shape)

    st = pl.BlockSpec((NLB, tm, 128), lambda i: (0, i, 0))
    cw = pl.BlockSpec((4, 512, 128), lambda i: (0, 0, 0))
    half = pl.BlockSpec((tm, DSS), lambda i: (i, 0))
    s, y, z = _pc(body, name=f"ssm_out_l{layer}", grid=(T // tm,),
                  in_specs=[st, st, half, pl.BlockSpec((tm, tm), lambda i: (0, 0)), cw, cw, _gain_spec(DSS, layer),
                            pl.BlockSpec((NSH, None, 128, DSS), lambda i: (0, 0, 0, 0)), _gain_spec(DSS, layer)],
                  out_specs=[_stream_block(tm, DSS), half, half],
                  out_shape=[S((16, T // 16, DSS), F32), S((T, DSS), F32), S((T, DSS), F32)],
                  compiler_params=_cp(1))(xr, xi, u, _perm_matrix(tm), cre, cim, dvec, wglu, bglu)
    return s.reshape(T, DSS), y, z


def _ssm_out_bwd(dssm, y, z, xr, xi, u, cre, cim, dvec, wglu, layer):
    T = u.shape[0]
    tm = TM

    def body(ds_ref, pm_ref, y_ref, z_ref, xr_ref, xi_ref, u_ref, cr_ref, ci_ref, d_ref, w_ref,
             gr_ref, gi_ref, du_ref, dz_ref, yg_ref, dbg_ref, dd_ref, dcr_ref, dci_ref):
        i = pl.program_id(0)

        @pl.when(i == 0)
        def _():
            dbg_ref[...] = jnp.zeros_like(dbg_ref)
            dd_ref[...] = jnp.zeros_like(dd_ref)
            dcr_ref[...] = jnp.zeros_like(dcr_ref)
            dci_ref[...] = jnp.zeros_like(dci_ref)

        yv = y_ref[...]
        yg = _gelu(yv)
        sg = jax.nn.sigmoid(z_ref[...])
        ds = _to_time(ds_ref[...].reshape(tm, DSS), pm_ref[...])
        dz = ds * yg * sg * (1.0 - sg)
        dzb = dz.astype(BF16)
        dz_ref[...] = dzb
        yg_ref[...] = yg.astype(BF16)
        dbg_ref[...] += jnp.sum(dz, axis=0, keepdims=True)
        dyg = ds * sg + jnp.concatenate([_dot_nt(dzb, w_ref[j]) for j in range(NSH)], axis=1)
        dy = dyg * _gelu_grad(yv)
        u = u_ref[...]
        dd_ref[...] += jnp.sum(dy * u, axis=0, keepdims=True)
        du_ref[...] = dy * d_ref[...]
        for s in range(4):
            dyc = dy[:, s * 128:(s + 1) * 128]
            g_r = _dot3(_dot_nt, dyc, cr_ref[s])
            g_i = -_dot3(_dot_nt, dyc, ci_ref[s])
            for q in range(4):
                gr_ref[4 * s + q] = g_r[:, q * 128:(q + 1) * 128]
                gi_ref[4 * s + q] = g_i[:, q * 128:(q + 1) * 128]
            xrc = jnp.concatenate([xr_ref[4 * s + q] for q in range(4)], axis=1)
            xic = jnp.concatenate([xi_ref[4 * s + q] for q in range(4)], axis=1)
            dcr_ref[s] += _dot3(_dot_tn, xrc, dyc)
            dci_ref[s] -= _dot3(_dot_tn, xic, dyc)

    st = pl.BlockSpec((NLB, tm, 128), lambda i: (0, i, 0))
    cw = pl.BlockSpec((4, 512, 128), lambda i: (0, 0, 0))
    half = pl.BlockSpec((tm, DSS), lambda i: (i, 0))
    return _pc(body, name=f"ssm_out_bwd_l{layer}", grid=(T // tm,),
               in_specs=[_stream_block(tm, DSS), pl.BlockSpec((tm, tm), lambda i: (0, 0)), half, half, st, st, half,
                         cw, cw, _gain_spec(DSS, layer), pl.BlockSpec((NSH, None, 128, DSS), lambda i: (0, 0, 0, 0))],
               out_specs=[st, st, half, half, half, _row_acc_spec(DSS), _row_acc_spec(DSS), cw, cw],
               out_shape=[S((NLB, T, 128), F32)] * 2 + [S((T, DSS), F32), S((T, DSS), BF16), S((T, DSS), BF16),
                                                        S((1, DSS), F32), S((1, DSS), F32),
                                                        S((4, 512, 128), F32), S((4, 512, 128), F32)],
               compiler_params=_cp(1))(dssm.reshape(16, T // 16, DSS), _perm_matrix(tm), y, z, xr, xi, u, cre, cim,
                                       dvec, wglu)


def _ssm_da(gr, gi, xr, xi, layer):
    T = gr.shape[1]
    tb = 4096 if T % 4096 == 0 else T

    def body(gr_ref, gi_ref, xr_ref, xi_ref, dr_ref, di_ref, lr, li):
        t = pl.program_id(1)

        @pl.when(t == 0)
        def _():
            dr_ref[...] = jnp.zeros_like(dr_ref)
            di_ref[...] = jnp.zeros_like(di_ref)
            lr[...] = jnp.zeros_like(lr)
            li[...] = jnp.zeros_like(li)

        g_r, g_i, x_r, x_i = gr_ref[...], gi_ref[...], xr_ref[...], xi_ref[...]
        pr = pltpu.roll(x_r, 1, 0)
        pi_ = pltpu.roll(x_i, 1, 0)
        g0r, g0i = g_r[0:1, :], g_i[0:1, :]
        fr = lr[7:8, :] - x_r[tb - 1:tb, :]
        fi = li[7:8, :] - x_i[tb - 1:tb, :]
        dr_ref[...] += jnp.sum(g_r * pr + g_i * pi_, axis=0, keepdims=True) + g0r * fr + g0i * fi
        di_ref[...] += jnp.sum(g_i * pr - g_r * pi_, axis=0, keepdims=True) + g0i * fr - g0r * fi
        lr[...] = x_r[tb - 8:tb, :]
        li[...] = x_i[tb - 8:tb, :]

    st = pl.BlockSpec((None, tb, 128), lambda k, t: (k, t, 0))
    out = pl.BlockSpec((None, 1, 128), lambda k, t: (k, 0, 0))
    return _pc(body, name=f"ssm_da_l{layer}", grid=(NLB, T // tb), in_specs=[st] * 4, out_specs=[out, out],
               out_shape=[S((NLB, 1, 128), F32)] * 2, scratch_shapes=[pltpu.VMEM((8, 128), F32)] * 2,
               compiler_params=_cp(2))(gr, gi, xr, xi)


def _ssm_in_bwd(gr, gi, u, bre, bim, du_direct, layer):
    T = u.shape[0]
    tm = TM

    def body(gr_ref, gi_ref, u_ref, pm_ref, br_ref, bi_ref, dd_ref, du_ref, dbr_ref, dbi_ref):
        i = pl.program_id(0)

        @pl.when(i == 0)
        def _():
            dbr_ref[...] = jnp.zeros_like(dbr_ref)
            dbi_ref[...] = jnp.zeros_like(dbi_ref)

        dus = []
        for s in range(4):
            grc = jnp.concatenate([gr_ref[4 * s + q] for q in range(4)], axis=1)
            gic = jnp.concatenate([gi_ref[4 * s + q] for q in range(4)], axis=1)
            uc = u_ref[:, s * 128:(s + 1) * 128]
            dus.append(_dot3(_dot_nt, grc, br_ref[s]) + _dot3(_dot_nt, gic, bi_ref[s]))
            dbr_ref[s] += _dot3(_dot_tn, uc, grc)
            dbi_ref[s] += _dot3(_dot_tn, uc, gic)
        du = jnp.concatenate(dus, axis=1) + dd_ref[...]
        du_ref[...] = _to_streams(du, pm_ref[...]).reshape(du_ref.shape)

    whole = pl.BlockSpec((4, 128, 512), lambda i: (0, 0, 0))
    st = pl.BlockSpec((NLB, tm, 128), lambda i: (0, i, 0))
    half = pl.BlockSpec((tm, DSS), lambda i: (i, 0))
    du, dbr, dbi = _pc(body, name=f"ssm_in_bwd_l{layer}", grid=(T // tm,),
                       in_specs=[st, st, half, pl.BlockSpec((tm, tm), lambda i: (0, 0)), whole, whole, half],
                       out_specs=[_stream_block(tm, DSS), whole, whole],
                       out_shape=[S((16, T // 16, DSS), F32), S((4, 128, 512), F32), S((4, 128, 512), F32)],
                       compiler_params=_cp(1))(gr, gi, u, _perm_matrix(tm), bre, bim, du_direct)
    return du.reshape(T, DSS), dbr, dbi


def _mix_out(outs, lses, ssm, h, attn_g, ssm_g, post_g, wout, layer):
    T = h.shape[0]
    tm = TM

    def body(o1, o2, o3, l1, l2, l3, s_ref, h_ref, ag_ref, sg_ref, pg_ref, w_ref, ho_ref, at_ref, ls_ref, mx_ref, mo_ref):
        la, lb, lc = l1[...], l2[...], l3[...]
        m = jnp.maximum(jnp.maximum(la, lb), lc)
        wa, wb, wc = jnp.exp(la - m), jnp.exp(lb - m), jnp.exp(lc - m)
        zs = wa + wb + wc
        attn = (wa * o1[...] + wb * o2[...] + wc * o3[...]) / zs
        at_ref[...] = attn
        ls_ref[...] = m + jnp.log(zs)
        mixed = jnp.concatenate([_rms_fwd(attn, ag_ref[...]), _rms_fwd(s_ref[...], sg_ref[...])], axis=1).astype(BF16)
        mx_ref[...] = mixed
        mo = sum(_dot(mixed[:, j * 256:(j + 1) * 256], w_ref[j]) for j in range(NSH))
        mo_ref[...] = mo
        ho_ref[...] = h_ref[...] + _rms_fwd(mo, pg_ref[...])

    row = pl.BlockSpec((tm, D), lambda i: (i, 0))
    half = pl.BlockSpec((tm, DA), lambda i: (i, 0))
    return _pc(body, name=f"mix_out_l{layer}", grid=(T // tm,),
               in_specs=[half] * 7 + [row, _gain_spec(DA, layer), _gain_spec(DSS, layer), _gain_spec(D, layer),
                                      pl.BlockSpec((NSH, None, 256, D), lambda i: (0, 0, 0, 0))],
               out_specs=[row, half, half, row, row],
               out_shape=[S((T, D), F32), S((T, DA), F32), S((T, DA), F32), S((T, D), BF16), S((T, D), F32)],
               compiler_params=_cp(1))(*outs, *lses, ssm, h, attn_g, ssm_g, post_g, wout)


def _mix_out_bwd(dout, mo, attn, ssm, attn_g, ssm_g, post_g, wout, layer):
    T = dout.shape[0]
    tm = TM
    head_sum =jnp.asarray(np.kron(np.eye(NH, dtype=np.float32), np.ones((HD, HD), np.float32)), BF16)

    def body(do_ref, mo_ref, at_ref, s_ref, ag_ref, sg_ref, pg_ref, w_ref, e_ref,
             da_ref, ds_ref, dl_ref, dmo_ref, dpg_ref, dag_ref, dsg_ref):
        i = pl.program_id(0)

        @pl.when(i == 0)
        def _():
            dpg_ref[...] = jnp.zeros_like(dpg_ref)
            dag_ref[...] = jnp.zeros_like(dag_ref)
            dsg_ref[...] = jnp.zeros_like(dsg_ref)

        dmo, dpg = _rms_bwd(do_ref[...], mo_ref[...], pg_ref[...])
        dpg_ref[...] += dpg
        dmob = dmo.astype(BF16)
        dmo_ref[...] = dmob
        dmix = jnp.concatenate([_dot_nt(dmob, w_ref[j]) for j in range(NSH)], axis=1)
        attn = at_ref[...]
        dat, dag = _rms_bwd(dmix[:, :DA], attn, ag_ref[...])
        dss, dsg = _rms_bwd(dmix[:, DA:], s_ref[...], sg_ref[...])
        dag_ref[...] += dag
        dsg_ref[...] += dsg
        da_ref[...] = dat
        ds_ref[...] = dss
        prod = dat * attn
        p1 = prod.astype(BF16)
        r1 = prod - p1.astype(F32)
        p2 = r1.astype(BF16)
        p3 = (r1 - p2.astype(F32)).astype(BF16)
        e = e_ref[...]
        dl_ref[...] = _dot(p1, e) + _dot(p2, e) + _dot(p3, e)

    row = pl.BlockSpec((tm, D), lambda i: (i, 0))
    half = pl.BlockSpec((tm, DA), lambda i: (i, 0))
    return _pc(body, name=f"mix_out_bwd_l{layer}", grid=(T // tm,),
               in_specs=[row, row, half, half, _gain_spec(DA, layer), _gain_spec(DSS, layer), _gain_spec(D, layer),
                         pl.BlockSpec((NSH, None, 256, D), lambda i: (0, 0, 0, 0)),
                         pl.BlockSpec((DA, DA), lambda i: (0, 0))],
               out_specs=[half, half, half, row, _row_acc_spec(D), _row_acc_spec(DA), _row_acc_spec(DSS)],
               out_shape=[S((T, DA), F32)] * 3 + [S((T, D), BF16), S((1, D), F32), S((1, DA), F32), S((1, DSS), F32)],
               compiler_params=_cp(1))(dout, mo, attn, ssm, attn_g, ssm_g, post_g, wout, head_sum)


def _ple_fwd(h, p3, wup, wgate, post_g, layer):
    T = h.shape[0]
    tm = TM

    def body(h_ref, p_ref, wu_ref, wg_ref, g_ref, ho_ref, e_ref, gt_ref):
        hv = h_ref[...]
        hb = hv.astype(BF16)
        pb = p_ref[...].astype(BF16)
        gte = sum(_dot(hb[:, j * 256:(j + 1) * 256], wg_ref[j]) for j in range(NSH))
        e = jnp.concatenate([_dot(pb, wu_ref[j]) for j in range(NSH)], axis=1)
        e_ref[...] = e
        gt_ref[...] = gte
        ho_ref[...] = hv + _rms_fwd(e * jax.nn.sigmoid(gte), g_ref[...])

    row = pl.BlockSpec((tm, D), lambda i: (i, 0))
    return _pc(body, name=f"ple_fwd_l{layer}", grid=(T // tm,),
               in_specs=[row, pl.BlockSpec((None, tm, PLE), lambda i: (layer, i, 0)),
                         pl.BlockSpec((NSH, None, PLE, 256), lambda i: (0, 0, 0, 0)),
                         pl.BlockSpec((NSH, None, 256, D), lambda i: (0, 0, 0, 0)), _gain_spec(D, layer)],
               out_specs=[row, row, row], out_shape=[S((T, D), F32)] * 3,
               compiler_params=_cp(1))(h, p3, wup, wgate, post_g)


def _ple_bwd(dout, e, gte, wgate, post_g, layer):
    T = dout.shape[0]
    tm = TM

    def body(do_ref, e_ref, gt_ref, wg_ref, g_ref, dh_ref, de_ref, dgt_ref, dg_ref):
        i = pl.program_id(0)

        @pl.when(i == 0)
        def _():
            dg_ref[...] = jnp.zeros_like(dg_ref)

        ev = e_ref[...]
        sg = jax.nn.sigmoid(gt_ref[...])
        do = do_ref[...]
        dple, dg = _rms_bwd(do, ev * sg, g_ref[...])
        dg_ref[...] += dg
        de = (dple * sg).astype(BF16)
        for j in range(NSH):
            de_ref[j] = de[:, j * 256:(j + 1) * 256]
        dgb = (dple * ev * sg * (1.0 - sg)).astype(BF16)
        dgt_ref[...] = dgb
        dh_ref[...] = do + jnp.concatenate([_dot_nt(dgb, wg_ref[j]) for j in range(NSH)], axis=1)

    row = pl.BlockSpec((tm, D), lambda i: (i, 0))
    return _pc(body, name=f"ple_bwd_l{layer}", grid=(T // tm,),
               in_specs=[row, row, row, pl.BlockSpec((NSH, None, 256, D), lambda i: (0, 0, 0, 0)), _gain_spec(D, layer)],
               out_specs=[row, pl.BlockSpec((NSH, tm, 256), lambda i: (0, i, 0)), row, _row_acc_spec(D)],
               out_shape=[S((T, D), F32), S((NSH, T, 256), BF16), S((T, D), BF16), S((1, D), F32)],
               compiler_params=_cp(1))(dout, e, gte, wgate, post_g)


def _loss_head(h, target):
    T = h.shape[0]
    tm = TM

    def body(h_ref, t_ref, dy_ref, l_ref):
        i = pl.program_id(0)

        @pl.when(i == 0)
        def _():
            l_ref[...] = jnp.zeros_like(l_ref)

        err = h_ref[...] - t_ref[...]
        dy_ref[...] = err * (1.0 / D)
        l_ref[...] += jnp.broadcast_to((0.5 / D) * jnp.sum(err * err), (1, 128))

    row = pl.BlockSpec((tm, D), lambda i: (i, 0))
    return _pc(body, name="loss_head", grid=(T // tm,), in_specs=[row, row],
               out_specs=[row, pl.BlockSpec((1, 128), lambda i: (0, 0))],
               out_shape=[S((T, D), F32), S((1, 128), F32)], compiler_params=_cp(1))(h, target)


def _local_step(x, p3, pos_col, target, weights_of, layer_grads_done, Sm):
    L = p3.shape[0]
    g3 = {n: Sm[n].reshape(L, 1, -1) for n in ("ffn1_pre_g", "ffn1_post_g", "mix_pre_g", "attn_norm_g", "ssm_norm_g",
                                                "mix_post_g", "ffn2_pre_g", "ffn2_post_g", "ple_post_g", "ssm_b_glu", "ssm_d")}
    rot = _rot_tables(pos_col)
    prep_names = ("ssm_lam_re", "ssm_lam_im", "ssm_log_dt", "ssm_b_re", "ssm_b_im", "ssm_c_re", "ssm_c_im")
    prep_all, prep_vjp = jax.vjp(jax.vmap(_ssm_prep), *[Sm[n] for n in prep_names])
    prep_cot = [None] * L

    saved = []
    h = x
    for l in range(L):
        W = weights_of(l, h)
        sv = {"h0": h, "W": W}
        h, sv["a1"], sv["b1"], sv["f1"], sv["xn1"] = _ffn_fwd(
            h, g3["ffn1_pre_g"], g3["ffn1_post_g"], W["ffn1_w_gate"], W["ffn1_w_up"], W["ffn1_w_down"], l, "1")
        sv["h1"] = h
        P, sv["ain"] = _mix_proj(h, g3["mix_pre_g"], W["w_in"], rot, l)
        sv["P"] = P
        ol = [_attn_fwd(P, d, l) for d in PATTERN_DILATIONS]
        prep = tuple(t[l] for t in prep_all)
        a_re, a_im, bre, bim, cre, cim = prep
        sv["prep"] = prep
        sv["u"], bur, bui = _ssm_in(P, bre, bim, l)
        xr, xi = _scan(bur, bui, a_re, a_im, False, l)
        sv["xr"], sv["xi"] = xr, xi
        ssm, sv["y"], sv["z"] = _ssm_out(xr, xi, sv["u"], cre, cim, g3["ssm_d"], W["ssm_w_glu"], g3["ssm_b_glu"], l)
        sv["ssm"] = ssm
        h, sv["attn"], sv["lse"], sv["mixed"], sv["mo"] = _mix_out(
            [o for o, _ in ol], [s for _, s in ol], ssm, h, g3["attn_norm_g"], g3["ssm_norm_g"], g3["mix_post_g"],
            W["w_out"], l)
        sv["h2"] = h
        h, sv["a2"], sv["b2"], sv["f2"], sv["xn2"] = _ffn_fwd(
            h, g3["ffn2_pre_g"], g3["ffn2_post_g"], W["ffn2_w_gate"], W["ffn2_w_up"], W["ffn2_w_down"], l, "2")
        sv["h3"] = h
        h, sv["e"], sv["gte"] = _ple_fwd(h, p3, W["ple_w_up"], W["ple_w_gate"], g3["ple_post_g"], l)
        saved.append(sv)

    dh, loss = _loss_head(h, target)

    G_layers = [{n: lax.empty((NSH, 1, r, c), BF16) for n, r, c in BIG} for _ in range(L)]
    sg = {n: [None] * L for n in SMALL}
    whole, shard, kcol = "whole", "shard", "cols"
    ple_g = g3["ple_post_g"]
    for l in reversed(range(L)):
        sv = saved[l]
        W = sv["W"]
        G, gl = G_layers[l], 0
        if l + 1 < L:
            ple_g = ple_g + layer_grads_done(l + 1, G_layers[l + 1])
        dh, de, dgte, sg["ple_post_g"][l] = _ple_bwd(dh, sv["e"], sv["gte"], W["ple_w_gate"], ple_g, l)
        G["ple_w_up"] = _dw(p3[l][None], de, G["ple_w_up"], gl, PLE, 256, whole, shard, f"dw_ple_up_l{l}")
        G["ple_w_gate"] = _dw(sv["h3"][None], dgte[None], G["ple_w_gate"], gl, 256, D, kcol, whole, f"dw_ple_gate_l{l}")
        dh, df, da, db, hh, sg["ffn2_pre_g"][l], sg["ffn2_post_g"][l] = _ffn_bwd(
            dh, sv["h2"], sv["f2"], sv["a2"], sv["b2"], g3["ffn2_pre_g"], g3["ffn2_post_g"],
            W["ffn2_w_gate"], W["ffn2_w_up"], W["ffn2_w_down"], l, "2")
        G["ffn2_w_gate"] = _dw(da, sv["xn2"][None], G["ffn2_w_gate"], gl, DFS, D, shard, whole, f"dw_ffn2_gate_l{l}")
        G["ffn2_w_up"] = _dw(db, sv["xn2"][None], G["ffn2_w_up"], gl, DFS, D, shard, whole, f"dw_ffn2_up_l{l}")
        G["ffn2_w_down"] = _dw(hh, df[None], G["ffn2_w_down"], gl, DFS, D, shard, whole, f"dw_ffn2_down_l{l}")
        a_re, a_im, bre, bim, cre, cim = sv["prep"]
        dattn, dssm, delta, dmo, sg["mix_post_g"][l], sg["attn_norm_g"][l], sg["ssm_norm_g"][l] = _mix_out_bwd(
            dh, sv["mo"], sv["attn"], sv["ssm"], g3["attn_norm_g"], g3["ssm_norm_g"], g3["mix_post_g"], W["w_out"], l)
        G["w_out"] = _dw(sv["mixed"][None], dmo[None], G["w_out"], gl, 256, D, kcol, whole, f"dw_out_l{l}")
        gnr, gni, du_direct, dz, yg, sg["ssm_b_glu"][l], dd, dcre, dcim = _ssm_out_bwd(
            dssm, sv["y"], sv["z"], sv["xr"], sv["xi"], sv["u"], cre, cim, g3["ssm_d"], W["ssm_w_glu"], l)
        sg["ssm_d"][l] = dd.reshape(Sm["ssm_d"].shape[1:])
        G["ssm_w_glu"] = _dw(yg[None], dz[None], G["ssm_w_glu"], gl, 128, DSS, kcol, whole, f"dw_glu_l{l}")
        gr, gi = _scan(gnr, gni, a_re, a_im, True, l)
        dar, dai = _ssm_da(gr, gi, sv["xr"], sv["xi"], l)
        du, dbre, dbim = _ssm_in_bwd(gr, gi, sv["u"], bre, bim, du_direct, l)
        prep_cot[l] = (dar, dai, dbre, dbim, dcre, dcim)
        acc = None
        for d in PATTERN_DILATIONS:
            acc = _attn_bwd(sv["P"], dattn, sv["lse"], delta, acc, d, l)
        dh, dP, sg["mix_pre_g"][l] = _mix_proj_bwd(acc[0], acc[1], acc[2], du, dh, sv["h1"], g3["mix_pre_g"],
                                                   W["w_in"], rot, l)
        G["w_in"] = _dw(sv["ain"][None], dP, G["w_in"], gl, D, DA,whole, shard, f"dw_in_l{l}")
        dh, df, da, db, hh, sg["ffn1_pre_g"][l], sg["ffn1_post_g"][l] = _ffn_bwd(
            dh, sv["h0"], sv["f1"], sv["a1"], sv["b1"], g3["ffn1_pre_g"], g3["ffn1_post_g"],
            W["ffn1_w_gate"], W["ffn1_w_up"], W["ffn1_w_down"], l, "1")
        G["ffn1_w_gate"] = _dw(da, sv["xn1"][None], G["ffn1_w_gate"], gl, DFS, D, shard, whole, f"dw_ffn1_gate_l{l}")
        G["ffn1_w_up"] = _dw(db, sv["xn1"][None], G["ffn1_w_up"], gl, DFS, D, shard, whole, f"dw_ffn1_up_l{l}")
        G["ffn1_w_down"] = _dw(hh, df[None], G["ffn1_w_down"], gl, DFS, D, shard, whole, f"dw_ffn1_down_l{l}")

    small = {n: jnp.stack([g.reshape(Sm[n].shape[1:]) for g in sg[n]]) for n in SMALL if n not in prep_names}
    small.update(zip(prep_names, prep_vjp(tuple(jnp.stack(c) for c in zip(*prep_cot)))))
    return loss, dh, G_layers[0], small


HBM_SPEC = pl.BlockSpec(memory_space=pltpu.HBM)


def _place():
    x, y, c = lax.axis_index("x"), lax.axis_index("y"), lax.axis_index("c")
    chips = [(1 - x, y), (x, 1 - y), (1 - x, 1 - y)]
    return x, y, c, chips


def _comm_params():
    return pltpu.CompilerParams(vmem_limit_bytes=VMEM_LIMIT)


def _gather_weights(ws, lands):
    n = len(ws)

    def body(*refs):
        ins, outs = refs[:n], refs[2 * n:3 * n]
        s_ici, r_ici, s_d2d, r_d2d = refs[3 * n:]
        x, y, c, chips = _place()

        def half(ref, t, hc):
            r2 = ws[t].shape[1] // 2
            return ref.at[:, pl.ds(hc * r2, r2), :]

        def ici(t, k, src_chip, to):
            j = 2 * src_chip[0] + src_chip[1]
            src = half(ins[t], t, c) if to is not None else half(outs[t].at[j], t, c)
            return pltpu.make_async_remote_copy(src_ref=src, dst_ref=half(outs[t].at[j], t, c),
                                                send_sem=s_ici.at[3 * t + k], recv_sem=r_ici.at[3 * t + k],
                                                device_id=to if to is not None else (x, y, c), device_id_type=MESH)

        def d2d(t, k, hc):
            j = 2 * chips[k][0] + chips[k][1]
            r = half(outs[t].at[j], t, hc)
            return pltpu.make_async_remote_copy(src_ref=r, dst_ref=r, send_sem=s_d2d.at[3 * t + k],
                                                recv_sem=r_d2d.at[3 * t + k], device_id=(x, y, 1 - c),
                                                device_id_type=MESH)

        sends = [ici(t, k, (x, y), (*chips[k], c)) for t in range(n) for k in range(3)]
        for cp in sends:
            cp.start()
        passed = []
        for t in range(n):
            for k in range(3):
                ici(t, k, chips[k], None).wait_recv()
                passed.append(d2d(t, k, c))
                passed[-1].start()
        for t in range(n):
            for k in range(3):
                d2d(t, k, 1 - c).wait_recv()
        for cp in sends + passed:
            cp.wait_send()

    return _pc(body, name="gather_weights", in_specs=[HBM_SPEC] * (2 * n), out_specs=[HBM_SPEC] * n,
               out_shape=[S(z.shape, z.dtype) for z in lands], input_output_aliases={n + t: t for t in range(n)},
               scratch_shapes=[pltpu.SemaphoreType.DMA((3 * n,))] * 4, compiler_params=_comm_params())(*ws, *lands)


SEM_SPEC = pl.BlockSpec(memory_space=pltpu.SEMAPHORE)
ANY_SPEC = pl.BlockSpec(memory_space=pl.ANY)
SPLIT_EFFECT = pltpu.SideEffectType.DATAFLOW_SIDE_EFFECTING


def _in_hbm(t):
    return pltpu.with_memory_space_constraint(t, pltpu.HBM)


def _place_own(ws, me_arr, layer):
    n = len(ws)

    def body(me_ref, *refs):
        for t in range(n):
            refs[n + t][...] = refs[t][...]

    gs = pltpu.PrefetchScalarGridSpec(
        num_scalar_prefetch=1, grid=(2,),
        in_specs=[pl.BlockSpec((w.shape[0], w.shape[1] // 2, w.shape[2]), lambda i, me: (0, i, 0)) for w in ws],
        out_specs=[pl.BlockSpec((None, w.shape[0], w.shape[1] // 2, w.shape[2]), lambda i, me: (me[0], 0, i, 0))
                   for w in ws])
    return _pc(body, name=f"gather_place_own_l{layer}", grid_spec=gs,
               out_shape=[S((NSH,) + w.shape, w.dtype) for w in ws], compiler_params=_cp(1))(me_arr, *ws)


def _gather_start(ws, lands, after, layer):
    n = len(ws)

    def body(*refs):
        ins, lz = refs[:n], refs[n:2 * n]
        s_sem, r_sem = refs[2 * n + 1], refs[2 * n + 2]
        token = refs[-1]
        x, y, c, chips = _place()
        for t in range(n):
            for k in range(3):
                pltpu.make_async_remote_copy(src_ref=ins[t], dst_ref=lz[t].at[2 * x + y], send_sem=s_sem.at[3 * t + k],
                                             recv_sem=r_sem.at[3 * t + k], device_id=(*chips[k], c),
                                             device_id_type=MESH).start()
        token[...] = jnp.zeros_like(token)

    hbm = [pltpu.HBM(w.shape, w.dtype) for w in ws] + [pltpu.HBM(z.shape, z.dtype) for z in lands]
    out = _pc(body, name=f"gather_start_l{layer}",
              out_shape=(pltpu.SemaphoreType.DMA((3 * n,)), pltpu.SemaphoreType.DMA((3 * n,)), *hbm, S((8, 128), F32)),
              in_specs=[HBM_SPEC] * (2 * n) + [ANY_SPEC],
              out_specs=(SEM_SPEC, SEM_SPEC, *([HBM_SPEC] * (2 * n)), pl.BlockSpec(memory_space=pltpu.VMEM)),
              input_output_aliases={i: 2 + i for i in range(2 * n)},
              compiler_params=pltpu.CompilerParams(has_side_effects=SPLIT_EFFECT))(
                  *[_in_hbm(w) for w in ws], *[_in_hbm(z) for z in lands], after)
    return out[0], out[1], out[2:2 + n], out[2 + n:2 + 2 * n], out[-1]


def _gather_wait(s_sem, r_sem, ws, lands, after, layer):
    n = len(ws)

    def body(*refs):
        ins, lz = refs[:n], refs[n:2 * n]
        s_ref, r_ref = refs[2 * n], refs[2 * n + 1]
        x, y, c, chips = _place()
        for t in range(n):
            for k in range(3):
                cp = pltpu.make_async_remote_copy(src_ref=ins[t], dst_ref=lz[t].at[2 * x + y], send_sem=s_ref.at[3 * t + k],
                                                  recv_sem=r_ref.at[3 * t + k], device_id=(*chips[k], c),
                                                  device_id_type=MESH)
                cp.wait_send()
                cp.wait_recv()

    hbm = [pltpu.HBM(w.shape, w.dtype) for w in ws] + [pltpu.HBM(z.shape, z.dtype) for z in lands]
    out = _pc(body, name=f"gather_wait_l{layer}", out_shape=tuple(hbm),
              in_specs=[HBM_SPEC] * (2 * n) + [SEM_SPEC, SEM_SPEC, ANY_SPEC], out_specs=tuple([HBM_SPEC] * (2 * n)),
              input_output_aliases={i: i for i in range(2 * n)},
              compiler_params=pltpu.CompilerParams(has_side_effects=SPLIT_EFFECT))(*ws, *lands, s_sem, r_sem, after)
    return out[n:]


def _swap_halves(gs, tag):
    n = len(gs)

    def body(*refs):
        ins, outs = refs[:n], refs[n:2 * n]
        s_sem, r_sem = refs[2 * n:]
        x, y, c, _ = _place()
        cps = []
        for t in range(n):
            r2 = gs[t].shape[2] // 2
            cps.append(pltpu.make_async_remote_copy(
                src_ref=ins[t].at[:, :, pl.ds((1 - c) * r2, r2), :], dst_ref=outs[t], send_sem=s_sem.at[t],
                recv_sem=r_sem.at[t], device_id=(x, y, 1 - c), device_id_type=MESH))
            cps[-1].start()
        for cp in cps:
            cp.wait_recv()
        for cp in cps:
            cp.wait_send()

    return _pc(body, name=f"grad_swap_halves_{tag}", in_specs=[HBM_SPEC] * n, out_specs=[HBM_SPEC] * n,
               out_shape=[S(g.shape[:2] + (g.shape[2] // 2, g.shape[3]), g.dtype) for g in gs],
               scratch_shapes=[pltpu.SemaphoreType.DMA((n,))] * 2, compiler_params=_comm_params())(*gs)


def _add_half(g, landed, c_arr, name):
    _, L, r2, cols = landed.shape

    def body(c_ref, g_ref, l_ref, o_ref):
        o_ref[...] = (g_ref[...].astype(F32) + l_ref[...].astype(F32)).astype(BF16)

    gs = pltpu.PrefetchScalarGridSpec(
        num_scalar_prefetch=1, grid=(NSH, L),
        in_specs=[pl.BlockSpec((None, None, r2, cols), lambda j, l, c: (j, l, c[0], 0)),
                  pl.BlockSpec((None, None, r2, cols), lambda j, l, c: (j, l, 0, 0))],
        out_specs=pl.BlockSpec((None, None, r2, cols), lambda j, l, c: (j, l, 0, 0)))
    return _pc(body, name=name, grid_spec=gs, out_shape=S(landed.shape, BF16), compiler_params=_cp(2))(c_arr, g, landed)


def _send_shards(ps):
    n = len(ps)

    def body(*refs):
        ins, outs = refs[:n], refs[n:2 * n]
        s_sem, r_sem = refs[2 * n:]
        x, y, c, chips = _place()
        cps = []
        for t in range(n):
            for k in range(3):
                cps.append(pltpu.make_async_remote_copy(
                    src_ref=ins[t].at[2 * chips[k][0] + chips[k][1]], dst_ref=outs[t].at[k],
                    send_sem=s_sem.at[3 * t + k], recv_sem=r_sem.at[3 * t + k], device_id=(*chips[k], c),
                    device_id_type=MESH))
                cps[-1].start()
        for cp in cps:
            cp.wait_recv()
        for cp in cps:
            cp.wait_send()

    return _pc(body, name="grad_send_shards", in_specs=[HBM_SPEC] * n, out_specs=[HBM_SPEC] * n,
               out_shape=[S((3,) + p.shape[1:], p.dtype) for p in ps],
               scratch_shapes=[pltpu.SemaphoreType.DMA((3 * n,))] * 2, compiler_params=_comm_params())(*ps)


def _direct_grad_copies(ins, lz, s_sem, r_sem):
    x, y, c, chips = _place()
    sends, recvs = [], []
    for t in range(len(ins)):
        r2 = ins[t].shape[2] // 2
        half = lambda j, h: ins[t].at[j, :, pl.ds(h * r2, r2), :]

        def copy(src, slot, s_idx, r_idx, to):
            return pltpu.make_async_remote_copy(src_ref=src, dst_ref=lz[t].at[slot], send_sem=s_sem.at[7 * t + s_idx],
                                                recv_sem=r_sem.at[7 * t + r_idx], device_id=to, device_id_type=MESH)

        for k in range(3):
            for h in range(2):
                sends.append(copy(half(2 * chips[k][0] + chips[k][1], h), 2 * k + c, 2 * k + h, 2 * k + c, (*chips[k], h)))
        sends.append(copy(half(2 * x + y, 1 - c), 6, 6, 6, (x, y, 1 - c)))
        recvs += [copy(half(0, 0), s, s, s, (x, y, c)) for s in range(7)]
    return sends, recvs


def _send_start(gs, lands, layer):
    n = len(gs)
    ps = gs

    def body(*refs):
        sends, _ = _direct_grad_copies(refs[:n], refs[n:2 * n], refs[2 * n], refs[2 * n + 1])
        for cp in sends:
            cp.start()
        refs[-1][...] = jnp.zeros_like(refs[-1])

    hbm = [pltpu.HBM(p.shape, p.dtype) for p in ps] + [pltpu.HBM(z.shape, z.dtype) for z in lands]
    out = _pc(body, name=f"grad_send_start_l{layer}",
              out_shape=(pltpu.SemaphoreType.DMA((7 * n,)), pltpu.SemaphoreType.DMA((7 * n,)), *hbm, S((8, 128), F32)),
              in_specs=[HBM_SPEC] * (2 * n),
              out_specs=(SEM_SPEC, SEM_SPEC, *([HBM_SPEC] * (2 * n)), pl.BlockSpec(memory_space=pltpu.VMEM)),
              input_output_aliases={i: 2 + i for i in range(2 * n)},
              compiler_params=pltpu.CompilerParams(has_side_effects=SPLIT_EFFECT))(
                  *[_in_hbm(p) for p in ps], *[_in_hbm(z) for z in lands])
    return out[0], out[1], out[2:2 + n], out[2 + n:2 + 2 * n], out[-1]


def _send_wait(s_sem, r_sem, ps, lands, after, layer):
    n = len(ps)

    def body(*refs):
        sends, recvs = _direct_grad_copies(refs[:n], refs[n:2 * n], refs[2 * n], refs[2 * n + 1])
        for cp in sends:
            cp.wait_send()
        for cp in recvs:
            cp.wait_recv()

    hbm = [pltpu.HBM(p.shape, p.dtype) for p in ps] + [pltpu.HBM(z.shape, z.dtype) for z in lands]
    out = _pc(body, name=f"grad_send_wait_l{layer}", out_shape=tuple(hbm),
              in_specs=[HBM_SPEC] * (2 * n) + [SEM_SPEC, SEM_SPEC, ANY_SPEC], out_specs=tuple([HBM_SPEC] * (2 * n)),
              input_output_aliases={i: i for i in range(2 * n)},
              compiler_params=pltpu.CompilerParams(has_side_effects=SPLIT_EFFECT))(*ps, *lands, s_sem, r_sem, after)
    return out[:n], out[n:]


def _sum_direct(g, landed, me_arr, c_arr, buf, first_layer, name):
    _, nl, r2, cols = landed.shape

    def body(me_ref, c_ref, g_ref, l_ref, b_ref, o_ref):
        tot = g_ref[...].astype(F32)
        for s in range(7):
            tot = tot + l_ref[s].astype(F32)
        o_ref[...] = tot

    gs = pltpu.PrefetchScalarGridSpec(
        num_scalar_prefetch=2, grid=(nl,),
        in_specs=[pl.BlockSpec((None, None, r2, cols), lambda l, me, c: (me[0], l, c[0], 0)),
                  pl.BlockSpec((7, None, r2, cols), lambda l, me, c: (0, l, 0, 0)), ANY_SPEC],
        out_specs=pl.BlockSpec((None, r2, cols), lambda l, me, c: (first_layer + l, c[0], 0)))
    return _pc(body, name=name, grid_spec=gs, out_shape=S(buf.shape, F32), input_output_aliases={4: 0},
               compiler_params=_cp(1))(me_arr, c_arr, g, landed, buf)


def _sum_shards(part, landed, me_arr, c_arr, buf, first_layer, name):
    _, nl, r2, cols = landed.shape

    def body(me_ref, c_ref, p_ref, l_ref, b_ref, o_ref):
        o_ref[...] = ((p_ref[...].astype(F32) + l_ref[0].astype(F32)) + l_ref[1].astype(F32)) + l_ref[2].astype(F32)

    gs = pltpu.PrefetchScalarGridSpec(
        num_scalar_prefetch=2, grid=(nl,),
        in_specs=[pl.BlockSpec((None, None, r2, cols), lambda l, me, c: (me[0], l, 0, 0)),
                  pl.BlockSpec((3, None, r2, cols), lambda l, me, c: (0, l, 0, 0)), ANY_SPEC],
        out_specs=pl.BlockSpec((None, r2, cols), lambda l, me, c: (first_layer + l, c[0], 0)))
    return _pc(body, name=name, grid_spec=gs, out_shape=S(buf.shape, F32), input_output_aliases={4: 0},
               compiler_params=_cp(1))(me_arr, c_arr, part, landed, buf)


def _share_halves(bufs):
    n = len(bufs)

    def body(*refs):
        ins, outs = refs[:n], refs[n:2 * n]
        s_sem, r_sem = refs[2 * n:]
        x, y, c, _ = _place()
        cps = []
        for t in range(n):
            r2 = bufs[t].shape[1] // 2
            cps.append(pltpu.make_async_remote_copy(
                src_ref=ins[t].at[:, pl.ds(c * r2, r2), :], dst_ref=outs[t].at[:, pl.ds(c * r2, r2), :],
                send_sem=s_sem.at[t], recv_sem=r_sem.at[t], device_id=(x, y, 1 - c), device_id_type=MESH))
            cps[-1].start()
        for cp in cps:
            cp.wait_recv()
        for cp in cps:
            cp.wait_send()

    return _pc(body, name="grad_share_halves", in_specs=[HBM_SPEC] * n, out_specs=[HBM_SPEC] * n,
               out_shape=[S(b.shape, b.dtype) for b in bufs], input_output_aliases={t: t for t in range(n)},
               scratch_shapes=[pltpu.SemaphoreType.DMA((n,))] * 2, compiler_params=_comm_params())(*bufs)


def _gather_small(v):
    nr = v.shape[0]

    def body(v_ref, out_ref, send_sems, recv_sems, local_sem):
        x, y, c, chips = _place()
        me, sibling = (x, y, c), (x, y, 1 - c)

        def rows(px, py, pc):
            return out_ref.at[pl.ds((4 * px + 2 * py + pc) * nr, nr), :]

        def copy(k, block, to, src=None):
            return pltpu.make_async_remote_copy(src_ref=rows(*block) if src is None else src, dst_ref=rows(*block),
                                                send_sem=send_sems.at[k], recv_sem=recv_sems.at[k], device_id=to,
                                                device_id_type=MESH)

        mine = pltpu.make_async_copy(v_ref, rows(*me), local_sem)
        mine.start()
        first = [copy(0, me, sibling, src=v_ref)]
        first += [copy(1 + j, me, (*chip, c), src=v_ref) for j, chip in enumerate(chips)]
        for cp in first:
            cp.start()
        passed = [copy(4 + j, (*chip, c), sibling) for j, chip in enumerate(chips)]
        for j, chip in enumerate(chips):
            copy(1 + j, (*chip, c), me).wait_recv()
            passed[j].start()
        copy(0, sibling, me).wait_recv()
        for j, chip in enumerate(chips):
            copy(4 + j, (*chip, 1 - c), me).wait_recv()
        for cp in first + passed:
            cp.wait_send()
        mine.wait()

    vm = pl.BlockSpec(memory_space=pltpu.VMEM)
    return _pc(body, name="gather_small_grads", in_specs=[vm], out_specs=vm, out_shape=S((8 * nr, 128), F32),
               scratch_shapes=[pltpu.SemaphoreType.DMA((7,)), pltpu.SemaphoreType.DMA((7,)), pltpu.SemaphoreType.DMA],
               compiler_params=_comm_params())(v)


def _adamw_math(w, g, m, v):
    m2 = ADAM_B1 * m + (1.0 - ADAM_B1) * g
    v2 = ADAM_B2 * v + (1.0 - ADAM_B2) * (g * g)
    m_hat = m2 / (1.0 - ADAM_B1 ** ADAM_STEP)
    v_hat = v2 / (1.0 - ADAM_B2 ** ADAM_STEP)
    return -ADAM_LR * (m_hat / (jnp.sqrt(v_hat) + ADAM_EPS) + ADAM_WD * w), m2, v2


def _adamw(w, g, m, v, name):
    L, R, C = w.shape
    rb = R // 2 if R >= 512 else R

    def body(w_ref, g_ref, m_ref, v_ref, d_ref, m2_ref, v2_ref):
        d_ref[...], m2_ref[...], v2_ref[...] = _adamw_math(w_ref[...], g_ref[...], m_ref[...], v_ref[...])

    blk = pl.BlockSpec((None, rb, C), lambda l, r: (l, r, 0))
    return _pc(body, name=name, grid=(L, R // rb), in_specs=[blk] * 4, out_specs=[blk] * 3,
               out_shape=[S(w.shape, F32)] * 3, compiler_params=_cp(2))(w, g, m, v)


def _adamw_small(gathered, w, m, v):
    nr = w.shape[0]
    rb = nr // 5

    def body(a_ref, w_ref, m_ref, v_ref, g_ref, d_ref, m2_ref, v2_ref):
        g = a_ref[0]
        for k in range(1, 8):
            g = g + a_ref[k]
        g_ref[...] = g
        d_ref[...], m2_ref[...], v2_ref[...] = _adamw_math(w_ref[...], g, m_ref[...], v_ref[...])

    blk = pl.BlockSpec((rb, 128), lambda i: (i, 0))
    return _pc(body, name="adamw_small", grid=(nr // rb,), in_specs=[pl.BlockSpec((8, rb, 128), lambda i: (0, i, 0))] + [blk] * 3,
               out_specs=[blk] * 4, out_shape=[S((nr, 128), F32)] * 4, compiler_params=_cp(1))(gathered, w, m, v)


SMALL_ROWS = 4520


def _pack(arrs):
    flat = jnp.concatenate([a.reshape(-1) for a in arrs])
    return jnp.pad(flat, (0, SMALL_ROWS * 128 - flat.shape[0])).reshape(SMALL_ROWS, 128)


def _unpack(packed, like):
    flat = packed.reshape(-1)
    out, off = [], 0
    for a in like:
        out.append(flat[off:off + a.size].reshape(a.shape))
        off += a.size
    return out


def kernel(x, p, positions, ffn1_pre_g, ffn1_w_gate, ffn1_w_up, ffn1_w_down, ffn1_post_g, mix_pre_g, w_in, attn_norm_g, ssm_lam_re, ssm_lam_im, ssm_log_dt, ssm_b_re, ssm_b_im, ssm_c_re, ssm_c_im, ssm_d, ssm_w_glu, ssm_b_glu, ssm_norm_g, w_out, mix_post_g, ffn2_pre_g, ffn2_w_gate, ffn2_w_up, ffn2_w_down, ffn2_post_g, ple_w_up, ple_w_gate, ple_post_g, loss_target, m_ffn1_pre_g, m_ffn1_w_gate, m_ffn1_w_up, m_ffn1_w_down, m_ffn1_post_g, m_mix_pre_g, m_w_in, m_attn_norm_g, m_ssm_lam_re, m_ssm_lam_im, m_ssm_log_dt, m_ssm_b_re, m_ssm_b_im, m_ssm_c_re, m_ssm_c_im, m_ssm_d, m_ssm_w_glu, m_ssm_b_glu, m_ssm_norm_g, m_w_out, m_mix_post_g, m_ffn2_pre_g, m_ffn2_w_gate, m_ffn2_w_up, m_ffn2_w_down, m_ffn2_post_g, m_ple_w_up, m_ple_w_gate, m_ple_post_g, v_ffn1_pre_g, v_ffn1_w_gate, v_ffn1_w_up, v_ffn1_w_down, v_ffn1_post_g, v_mix_pre_g, v_w_in, v_attn_norm_g, v_ssm_lam_re, v_ssm_lam_im, v_ssm_log_dt, v_ssm_b_re, v_ssm_b_im, v_ssm_c_re, v_ssm_c_im, v_ssm_d, v_ssm_w_glu, v_ssm_b_glu, v_ssm_norm_g, v_w_out, v_mix_post_g, v_ffn2_pre_g, v_ffn2_w_gate, v_ffn2_w_up, v_ffn2_w_down, v_ffn2_post_g, v_ple_w_up, v_ple_w_gate, v_ple_post_g):
    a = dict(locals())
    T = x.shape[1]
    big_names = [n for n, _, _ in BIG]
    for n in TRANSPOSED:
        for pre in ("", "m_", "v_"):
            a[pre + n] = jnp.swapaxes(a[pre + n], 1, 2)

    own = [a[n].astype(BF16) for n in big_names]
    n_layers = own[0].shape[0]
    per_layer = [[w[l:l + 1] for w in own] for l in range(n_layers)]
    c_arr = lax.axis_index("c").astype(jnp.int32).reshape(1)
    me_arr = (2 * lax.axis_index("x") + lax.axis_index("y")).astype(jnp.int32).reshape(1)
    first = dict(zip(big_names, _gather_weights(per_layer[0], _place_own(per_layer[0], me_arr, 0))))
    pending, anchor, queued_behind = {}, jnp.zeros((), F32), first[big_names[0]]
    for l in range(1, n_layers):
        s_sem, r_sem, ws_thru, lands_thru, token = _gather_start(per_layer[l], _place_own(per_layer[l], me_arr, l),
                                                                 queued_behind, l)
        pending[l] = (s_sem, r_sem, ws_thru, lands_thru)
        anchor = anchor + token[0, 0]
        queued_behind = token

    def weights_of(l, after):
        if l == 0:
            return first
        return dict(zip(big_names, _gather_wait(*pending[l], after, l)))

    Sm = {n: a[n] for n in SMALL}
    Sm["ffn1_pre_g"] = Sm["ffn1_pre_g"] + anchor

    pos = jnp.broadcast_to(positions.reshape(1, T, 1).astype(F32), (1, T, 128))
    def chip_partials(G, tag):
        gs = [G[n] for n in big_names]
        landed = _swap_halves(gs, tag)
        return [_add_half(g, la, c_arr, f"grad_add_half_{tag}_{n}") for g, la, n in zip(gs, landed, big_names)]

    sent = {}

    def layer_grads_done(l, G):
        gs = [G[n] for n in big_names]
        lands = [lax.empty((7, 1, g.shape[2] // 2, g.shape[3]), BF16) for g in gs]
        s_sem, r_sem, gs_thru, lands_thru, token = _send_start(gs, lands, l)
        sent[l] = (s_sem, r_sem, gs_thru, lands_thru)
        return token[0, 0]

    loss, gx, G_first, small = _local_step(
        _reorder(x, True, "to_streams_x")[0], _reorder(p[:, 0], True, "to_streams_p"),
        _reorder(pos, True, "to_streams_pos")[0, :, :1], _reorder(loss_target, True, "to_streams_target")[0],
        weights_of, layer_grads_done, Sm)
    gx = _reorder(gx[None], False, "to_time_grad_x")

    bufs = [lax.empty((n_layers, r, c), F32) for _, r, c in BIG]
    for l in sorted(sent, reverse=True):
        gs, landed = _send_wait(*sent[l], gx, l)
        bufs = [_sum_direct(g, la, me_arr, c_arr, b, l, f"grad_sum_direct_l{l}_{n}")
                for g, la, b, n in zip(gs, landed, bufs, big_names)]
    parts = chip_partials(G_first, "first")
    landed = _send_shards(parts)
    bufs = [_sum_shards(pt, la, me_arr, c_arr, b, 0, f"grad_sum_shards_first_{n}")
            for pt, la, b, n in zip(parts, landed, bufs, big_names)]
    grads = dict(zip(big_names, _share_halves(bufs)))

    small_g = _gather_small(_pack([small[n] for n in SMALL])).reshape(8, SMALL_ROWS, 128)
    sg, sd, sm, sv = _adamw_small(small_g, _pack([a[n] for n in SMALL]), _pack([a["m_" + n] for n in SMALL]),
                                  _pack([a["v_" + n] for n in SMALL]))
    like = [a[n] for n in SMALL]
    res = {}
    for n, g_, d_, m_, v_ in zip(SMALL, _unpack(sg, like), _unpack(sd, like), _unpack(sm, like), _unpack(sv, like)):
        res[n] = (g_, d_, m_, v_)
    for n in big_names:
        d_, m_, v_ = _adamw(a[n], grads[n], a["m_" + n], a["v_" + n], f"adamw_{n}")
        res[n] = (grads[n], d_, m_, v_)
        if n in TRANSPOSED:
            res[n] = tuple(jnp.swapaxes(t, 1, 2) for t in res[n])

    total = lax.psum(loss[0, 0], ("x", "y", "c"))
    return (total, gx, *[res[n][0] for n in WEIGHTS], *[res[n][1] for n in WEIGHTS],
            *[res[n][2] for n in WEIGHTS], *[res[n][3] for n in WEIGHTS])
```

```python
import functools
import math

import numpy as np
import jax
import jax.numpy as jnp
from jax import lax
from jax.experimental import pallas as pl
from jax.experimental.pallas import tpu as pltpu

F32 = jnp.float32
BF16 = jnp.bfloat16
S = jax.ShapeDtypeStruct
MESH = pl.DeviceIdType.MESH

D = 1024
DA = 512
DSS = 512
HD = 64
NH = 8
BAND = 128
NSH = 4
DFS = 704
PLE = 256
EPS = 1e-6
ROPE_THETA = 500000.0
PATTERN_DILATIONS = (1, 4, 16)
NLB = 16
ADAM_LR, ADAM_B1, ADAM_B2, ADAM_EPS, ADAM_WD, ADAM_STEP = 0.001, 0.9, 0.999, 1e-08, 0.01, 10

VMEM_LIMIT = 56 * 1024 * 1024
TM = 512
TMB = 256

BIG = (
    ("ffn1_w_gate", DFS, D), ("ffn1_w_up", DFS, D), ("ffn1_w_down", DFS, D),
    ("w_in", D, 512), ("ssm_w_glu", 128, 512), ("w_out", 256, D),
    ("ffn2_w_gate", DFS, D), ("ffn2_w_up", DFS, D), ("ffn2_w_down", DFS, D),
    ("ple_w_up", PLE, 256), ("ple_w_gate", 256, D),
)
TRANSPOSED = ("ffn1_w_gate", "ffn1_w_up", "ffn2_w_gate", "ffn2_w_up")
SMALL = ("ffn1_pre_g", "ffn1_post_g", "mix_pre_g", "attn_norm_g", "ssm_lam_re", "ssm_lam_im", "ssm_log_dt",
         "ssm_b_re", "ssm_b_im", "ssm_c_re", "ssm_c_im", "ssm_d", "ssm_b_glu", "ssm_norm_g", "mix_post_g",
         "ffn2_pre_g", "ffn2_post_g", "ple_post_g")
WEIGHTS = ("ffn1_pre_g", "ffn1_w_gate", "ffn1_w_up", "ffn1_w_down", "ffn1_post_g", "mix_pre_g", "w_in", "attn_norm_g",
           "ssm_lam_re", "ssm_lam_im", "ssm_log_dt", "ssm_b_re", "ssm_b_im", "ssm_c_re", "ssm_c_im", "ssm_d",
           "ssm_w_glu", "ssm_b_glu", "ssm_norm_g", "w_out", "mix_post_g", "ffn2_pre_g", "ffn2_w_gate", "ffn2_w_up",
           "ffn2_w_down", "ffn2_post_g", "ple_w_up", "ple_w_gate", "ple_post_g")


def _pc(body, **kw):
    return pl.pallas_call(body, **kw)


def _cp(n_grid):
    return pltpu.CompilerParams(dimension_semantics=("arbitrary",) * n_grid, vmem_limit_bytes=VMEM_LIMIT)


def _dot(a, b):
    return jnp.dot(a, b, preferred_element_type=F32)


def _dot_nt(a, b):
    return lax.dot_general(a, b, (((1,), (1,)), ((), ())), preferred_element_type=F32)


def _dot_tn(a, b):
    return lax.dot_general(a, b, (((0,), (0,)), ((), ())), preferred_element_type=F32)


def _split(a):
    hi = a.astype(BF16)
    return hi, (a - hi.astype(F32)).astype(BF16)


def _dot3(fn, a, b):
    ah, al = _split(a)
    bh, bl = _split(b)
    return fn(ah, bh) + fn(ah, bl) + fn(al, bh)


def _rms_fwd(x, g):
    r = lax.rsqrt(jnp.mean(x * x, axis=-1, keepdims=True) + EPS)
    return x * r * g


def _rms_bwd(dy, x, g):
    r = lax.rsqrt(jnp.mean(x * x, axis=-1, keepdims=True) + EPS)
    xr = x * r
    gd = dy * g
    dx = r * (gd - xr * jnp.mean(gd * xr, axis=-1, keepdims=True))
    dg = jnp.sum(dy * xr, axis=0, keepdims=True)
    return dx, dg


def _gelu(y):
    k = math.sqrt(2.0 / math.pi)
    return 0.5 * y * (1.0 + jnp.tanh(k * (y + 0.044715 * y * y * y)))


def _gelu_grad(y):
    k = math.sqrt(2.0 / math.pi)
    t = jnp.tanh(k * (y + 0.044715 * y * y * y))
    return 0.5 * (1.0 + t) + 0.5 * y * (1.0 - t * t) * k * (1.0 + 3 * 0.044715 * y * y)


def _gain_spec(n, layer):
    return pl.BlockSpec((None, 1, n), lambda *_: (layer, 0, 0))


def _row_acc_spec(n):
    return pl.BlockSpec((1, n), lambda *_: (0, 0))


def _rot_tables(pos_col):
    T = pos_col.shape[0]
    half = HD // 8
    inv = (ROPE_THETA ** (-np.arange(half, dtype=np.float32) * (2.0 / (2 * half)))).astype(np.float32)
    lane_freq = np.tile(np.concatenate([inv, inv, np.zeros(HD - 2 * half, np.float32)]), NH)[None, :]

    def body(p_ref, f_ref, c_ref, s1_ref, s2_ref):
        ang = p_ref[...] * f_ref[...]
        d = lax.broadcasted_iota(jnp.int32, ang.shape, 1) % HD
        cs = jnp.cos(ang)
        sn = jnp.sin(ang)
        c_ref[...] = jnp.where(d < 2 * half, cs, 1.0)
        s1_ref[...] = jnp.where(d < half, -sn, 0.0)
        s2_ref[...] = jnp.where((d >= half) & (d < 2 * half), sn, 0.0)

    tm = TM
    return _pc(body, name="rot_tables", grid=(T // tm,),
               in_specs=[pl.BlockSpec((tm, 1), lambda i: (i, 0)), pl.BlockSpec((1, DA), lambda i: (0, 0))],
               out_specs=[pl.BlockSpec((tm, DA), lambda i: (i, 0))] * 3,
               out_shape=[S((T, DA), F32)] * 3, compiler_params=_cp(1))(pos_col, jnp.asarray(lane_freq))


def _rot_fwd(t, c, s1, s2):
    return t * c + pltpu.roll(t, DA - 8, 1) * s1 + pltpu.roll(t, 8, 1) * s2


def _rot_bwd(g, c, s1, s2):
    return g * c + pltpu.roll(g * s1, 8, 1) + pltpu.roll(g * s2, DA - 8, 1)


def _ffn_weight_spec():
    return pl.BlockSpec((NSH, None, DFS, D), lambda i: (0, 0, 0, 0), pipeline_mode=pl.Buffered(1))


def _ffn_fwd(h, pre_g, post_g, wg, wu, wd, layer, tag):
    T = h.shape[0]
    tm = TM
    nt = T // tm

    def body(h_ref, pg_ref, qg_ref, wg_ref, wu_ref, wd_ref, ho_ref, a_ref, b_ref, f_ref, xn_ref):
        hv = h_ref[...]
        xb = _rms_fwd(hv, pg_ref[...]).astype(BF16)
        xn_ref[...] = xb
        f = None
        for j in range(NSH):
            ab = _dot_nt(xb, wg_ref[j]).astype(BF16)
            bb = _dot_nt(xb, wu_ref[j]).astype(BF16)
            a_ref[j] = ab
            b_ref[j] = bb
            a = ab.astype(F32)
            hh = (a * jax.nn.sigmoid(a) * bb.astype(F32)).astype(BF16)
            part = _dot(hh, wd_ref[j])
            f = part if f is None else f + part
        f_ref[...] = f
        ho_ref[...] = hv + 0.5 * _rms_fwd(f, qg_ref[...])

    row = pl.BlockSpec((tm, D), lambda i: (i, 0))
    act = pl.BlockSpec((NSH, tm, DFS), lambda i: (0, i, 0))
    return _pc(body, name=f"ffn_fwd_{tag}_l{layer}", grid=(nt,),
               in_specs=[row, _gain_spec(D, layer), _gain_spec(D, layer)] + [_ffn_weight_spec()] * 3,
               out_specs=[row, act, act, row, row],
               out_shape=[S((T, D), F32), S((NSH, T, DFS), BF16), S((NSH, T, DFS), BF16), S((T, D), F32), S((T, D), BF16)],
               compiler_params=_cp(1))(h, pre_g, post_g, wg, wu, wd)


def _ffn_bwd(dout, h, f, a, b, pre_g, post_g, wg, wu, wd, layer, tag):
    T = h.shape[0]
    tm = TMB
    nt = T // tm

    def body(do_ref, h_ref, f_ref, a_ref, b_ref, pg_ref, qg_ref, wg_ref, wu_ref, wd_ref,
             dh_ref, df_ref, da_ref, db_ref, hh_ref, dpg_ref, dqg_ref):
        @pl.when(pl.program_id(0) == 0)
        def _():
            dpg_ref[...] = jnp.zeros_like(dpg_ref)
            dqg_ref[...] = jnp.zeros_like(dqg_ref)

        do = do_ref[...]
        df, dq = _rms_bwd(0.5 * do, f_ref[...], qg_ref[...])
        dqg_ref[...] += dq
        dfb = df.astype(BF16)
        df_ref[...] = dfb
        dxn = None
        for j in range(NSH):
            dhh = _dot_nt(dfb, wd_ref[j])
            av = a_ref[j].astype(F32)
            bv = b_ref[j].astype(F32)
            sg = jax.nn.sigmoid(av)
            sa = av * sg
            hh_ref[j] = (sa * bv).astype(BF16)
            dab = (dhh * bv * (sg + sa * (1.0 - sg))).astype(BF16)
            dbb = (dhh * sa).astype(BF16)
            da_ref[j] = dab
            db_ref[j] = dbb
            part = _dot(dab, wg_ref[j]) + _dot(dbb, wu_ref[j])
            dxn = part if dxn is None else dxn + part
        dx, dp = _rms_bwd(dxn, h_ref[...], pg_ref[...])
        dpg_ref[...] += dp
        dh_ref[...] = do + dx

    row = pl.BlockSpec((tm, D), lambda i: (i, 0))
    act = pl.BlockSpec((NSH, tm, DFS), lambda i: (0, i, 0))
    return _pc(body, name=f"ffn_bwd_{tag}_l{layer}", grid=(nt,),
               in_specs=[row, row, row, act, act, _gain_spec(D, layer), _gain_spec(D, layer)] + [_ffn_weight_spec()] * 3,
               out_specs=[row, row, act, act, act, _row_acc_spec(D), _row_acc_spec(D)],
               out_shape=[S((T, D), F32), S((T, D), BF16), S((NSH, T, DFS), BF16), S((NSH, T, DFS), BF16),
                          S((NSH, T, DFS), BF16), S((1, D), F32), S((1, D), F32)],
               compiler_params=_cp(1))(dout, h, f, a, b, pre_g, post_g, wg, wu, wd)


def _dw(A, B, buf, layer, kb, nb, a_mode, b_mode, name):
    T = A.shape[1]
    tt = TM
    nt = T // tt

    def pick(v, mode, j, w):
        if mode == "shard":
            return v[j]
        return v[0] if mode == "whole" else v[0][:, j * w:(j + 1) * w]

    def body(a_ref, b_ref, buf_ref, o_ref, acc):
        t = pl.program_id(0)

        @pl.when(t == 0)
        def _():
            acc[...] = jnp.zeros_like(acc)

        av = a_ref[...].astype(BF16)
        bv = b_ref[...].astype(BF16)
        for j in range(NSH):
            acc[j] += _dot_tn(pick(av, a_mode, j, kb), pick(bv, b_mode, j, nb))

        @pl.when(t == nt - 1)
        def _():
            o_ref[...] = acc[...].astype(o_ref.dtype)

    return _pc(body, name=name, grid=(nt,),
               in_specs=[pl.BlockSpec((A.shape[0], tt, A.shape[2]), lambda t: (0, t, 0)),
                         pl.BlockSpec((B.shape[0], tt, B.shape[2]), lambda t: (0, t, 0)),
                         pl.BlockSpec(memory_space=pl.ANY)],
               out_specs=pl.BlockSpec((NSH, None, kb, nb), lambda t: (0, layer, 0, 0)),
               out_shape=S(buf.shape, buf.dtype), input_output_aliases={2: 0},
               scratch_shapes=[pltpu.VMEM((NSH, kb, nb), F32)], compiler_params=_cp(1))(A, B, buf)


def _mix_proj(h, pre_g, win, rot, layer):
    T = h.shape[0]
    tm = TM

    def body(h_ref, g_ref, w_ref, c_ref, s1_ref, s2_ref, p_ref, xn_ref):
        xb = _rms_fwd(h_ref[...], g_ref[...]).astype(BF16)
        xn_ref[...] = xb
        for j in range(NSH):
            o = _dot(xb, w_ref[j])
            p_ref[j] = _rot_fwd(o, c_ref[...], s1_ref[...], s2_ref[...]) if j < 2 else o

    row = pl.BlockSpec((tm, D), lambda i: (i, 0))
    half = pl.BlockSpec((tm, DA), lambda i: (i, 0))
    return _pc(body, name=f"mix_proj_l{layer}", grid=(T // tm,),
               in_specs=[row, _gain_spec(D, layer), pl.BlockSpec((NSH, None, D, DA), lambda i: (0, 0, 0, 0)),
                         half, half, half],
               out_specs=[pl.BlockSpec((NSH, tm, DA), lambda i: (0, i, 0)), row],
               out_shape=[S((NSH, T, DA), F32), S((T, D), BF16)], compiler_params=_cp(1))(h, pre_g, win, *rot)


def _mix_proj_bwd(dq, dk, dv, du, dh_up, h, pre_g, win, rot, layer):
    T = h.shape[0]
    tm = TM

    def body(dq_ref, dk_ref, dv_ref, du_ref, up_ref, h_ref, g_ref, w_ref, c_ref, s1_ref, s2_ref,
             dh_ref, dp_ref, dg_ref):
        @pl.when(pl.program_id(0) == 0)
        def _():
            dg_ref[...] = jnp.zeros_like(dg_ref)

        rot = (c_ref[...], s1_ref[...], s2_ref[...])
        dps = [_rot_bwd(dq_ref[...], *rot), _rot_bwd(dk_ref[...], *rot), dv_ref[...], du_ref[...]]
        dxn = None
        for j in range(NSH):
            dpb = dps[j].astype(BF16)
            dp_ref[j] = dpb
            part = _dot_nt(dpb, w_ref[j])
            dxn = part if dxn is None else dxn + part
        dx, dg = _rms_bwd(dxn, h_ref[...], g_ref[...])
        dg_ref[...] += dg
        dh_ref[...] = up_ref[...] + dx

    row = pl.BlockSpec((tm, D), lambda i: (i, 0))
    half = pl.BlockSpec((tm, DA), lambda i: (i, 0))
    return _pc(body, name=f"mix_proj_bwd_l{layer}", grid=(T // tm,),
               in_specs=[half, half, half, half, row, row, _gain_spec(D, layer),
                         pl.BlockSpec((NSH, None, D, DA), lambda i: (0, 0, 0, 0)), half, half, half],
               out_specs=[row, pl.BlockSpec((NSH, tm, DA), lambda i: (0, i, 0)), _row_acc_spec(D)],
               out_shape=[S((T, D), F32), S((NSH, T, DA), BF16), S((1, D), F32)],
               compiler_params=_cp(1))(dq, dk, dv, du, dh_up, h, pre_g, win, *rot)


def _stream_pos(d, axis):
    i = lax.broadcasted_iota(jnp.int32, (BAND, BAND), axis)
    if d == 16:
        return i
    if d == 4:
        return 4 * (i % 32) + i // 32
    return 16 * (i % 8) + i // 8


def _band_masks(b, d):
    qi, kj = _stream_pos(d, 0), _stream_pos(d, 1)
    return kj <= qi, (kj >= qi) & (b > 0)


def _pattern(d, T):
    n16 = T // 16
    if d == 16:
        return (16, n16, DA), (None, BAND, DA), lambda r, k: (r, k, 0)
    if d == 4:
        return (4, 4, n16, DA), (4, None, 32, DA), lambda r, k: (0, r, k, 0)
    return (16, n16, DA), (16, 8, DA), lambda r, k: (0, k, 0)


def _pattern_spec(d, T, kmap, lead=None):
    _, blk, idx = _pattern(d, T)
    if lead is None:
        return pl.BlockSpec(blk, lambda r, b: idx(r, kmap(b)))
    return pl.BlockSpec((None,) + blk, lambda r, b: (lead,) + idx(r, kmap(b)))


def _whole_stream_specs(T, n_plain):
    n16 = T // 16
    p_spec = lambda s: pl.BlockSpec((None, None, n16, DA), lambda r: (s, r, 0, 0))
    plain = pl.BlockSpec((None, n16, DA), lambda r: (r, 0, 0))
    return [p_spec(0), p_spec(1), p_spec(2)] + [plain] * n_plain, plain


def _stream_masks():
    qi = lax.broadcasted_iota(jnp.int32, (BAND, BAND), 0)
    kj = lax.broadcasted_iota(jnp.int32, (BAND, BAND), 1)
    mask_c = kj <= qi
    return mask_c, jnp.concatenate([kj >= qi, mask_c], axis=1)


def _attn_fwd_stream(P, layer):
    T = P.shape[1]
    n16 = T // 16
    nb = n16 // BAND
    scale = HD ** -0.5

    def body(q_ref, k_ref, v_ref, o_ref, l_ref, qs, ks, vs):
        for src, dst in ((q_ref, qs), (k_ref, ks), (v_ref, vs)):
            dst[...] = src[...].astype(BF16)
        mask_c, mask_pc = _stream_masks()
        for b in range(nb):
            rows = slice(b * BAND, (b + 1) * BAND)
            krows = slice(max(b - 1, 0) * BAND, (b + 1) * BAND)
            mask = mask_c if b == 0 else mask_pc
            for hd in range(NH):
                sl = slice(hd * HD, (hd + 1) * HD)
                s = jnp.where(mask, _dot_nt(qs[rows, sl], ks[krows, sl]) * scale, -1e30)
                m = jnp.max(s, axis=-1, keepdims=True)
                e = jnp.exp(s - m)
                den = jnp.sum(e, axis=-1, keepdims=True)
                o_ref[rows, sl] = _dot(e.astype(BF16), vs[krows, sl]) / den
                l_ref[rows, sl] = jnp.broadcast_to(m + jnp.log(den), (BAND, HD))

    ins, out = _whole_stream_specs(T, 0)
    Pv = P.reshape(NSH, 16, n16, DA)
    o, l = _pc(body, name=f"attn_fwd_d16_l{layer}", grid=(16,), in_specs=ins, out_specs=[out, out],
               out_shape=[S((16, n16, DA), F32)] * 2, scratch_shapes=[pltpu.VMEM((n16, DA), BF16)] * 3,
               compiler_params=_cp(1))(Pv, Pv, Pv)
    return o.reshape(T, DA), l.reshape(T, DA)


def _attn_bwd_stream(P, dO, lse, delta, acc, layer):
    T = P.shape[1]
    n16 = T // 16
    nb = n16 // BAND
    scale = HD ** -0.5
    first = acc is None

    def body(*refs):
        q_ref, k_ref, v_ref, do_ref, l_ref, dl_ref = refs[:6]
        if first:
            dq_ref, dk_ref, dv_ref = refs[6:9]
        else:
            aq_ref, ak_ref, av_ref, dq_ref, dk_ref, dv_ref = refs[6:12]
        qs, ks, vs, dos, okf, ovf = refs[-6:]
        for src, dst in ((q_ref, qs), (k_ref, ks), (v_ref, vs), (do_ref, dos)):
            dst[...] = src[...].astype(BF16)
        okf[...] = jnp.zeros_like(okf)
        ovf[...] = jnp.zeros_like(ovf)
        mask_c, mask_pc = _stream_masks()
        for b in range(nb):
            rows = slice(b * BAND, (b + 1) * BAND)
            krows = slice(max(b - 1, 0) * BAND, (b + 1) * BAND)
            mask = mask_c if b == 0 else mask_pc
            for hd in range(NH):
                sl = slice(hd * HD, (hd + 1) * HD)
                one = slice(hd * HD, hd * HD + 1)
                q, do, kk = qs[rows, sl], dos[rows, sl], ks[krows, sl]
                p = jnp.where(mask, jnp.exp(_dot_nt(q, kk) * scale - l_ref[rows, one]), 0.0)
                ds = (p * (_dot_nt(do, vs[krows, sl]) - dl_ref[rows, one]) * scale).astype(BF16)
                dq = _dot(ds, kk)
                dq_ref[rows, sl] = dq if first else aq_ref[rows, sl] + dq
                okf[krows, sl] += _dot_tn(ds, q)
                ovf[krows, sl] += _dot_tn(p.astype(BF16), do)
        dk_ref[...] = okf[...] if first else ak_ref[...] + okf[...]
        dv_ref[...] = ovf[...] if first else av_ref[...] + ovf[...]

    ins, out = _whole_stream_specs(T, 3 if first else 6)
    Pv = P.reshape(NSH, 16, n16, DA)
    view = lambda t: t.reshape(16, n16, DA)
    args = [Pv, Pv, Pv, view(dO), view(lse), view(delta)] + ([] if first else [view(t) for t in acc])
    dq, dk, dv = _pc(body, name=f"attn_bwd_d16_l{layer}", grid=(16,), in_specs=ins, out_specs=[out, out, out],
                     out_shape=[S((16, n16, DA), F32)] * 3,
                     scratch_shapes=[pltpu.VMEM((n16, DA), BF16)] * 4 + [pltpu.VMEM((n16, DA), F32)] * 2,
                     compiler_params=_cp(1))(*args)
    return dq.reshape(T, DA), dk.reshape(T, DA), dv.reshape(T, DA)


def _attn_fwd(P, d, layer):
    if d == 16:
        return _attn_fwd_stream(P, layer)
    T = P.shape[1]
    nb = T // d // BAND
    vshape = _pattern(d, T)[0]
    Pv = P.reshape((NSH,) + vshape)
    scale = HD ** -0.5

    def body(q_ref, kp_ref, kc_ref, vp_ref, vc_ref, o_ref, l_ref, qs, ks, vs, osc, lsc):
        b = pl.program_id(1)
        flat = lambda ref: ref[...].reshape(BAND, DA).astype(BF16)
        qs[...] = flat(q_ref)
        ks[0:BAND, :] = flat(kp_ref)
        ks[BAND:, :] = flat(kc_ref)
        vs[0:BAND, :] = flat(vp_ref)
        vs[BAND:, :] = flat(vc_ref)
        mask_c, mask_p = _band_masks(b, d)
        mask = jnp.concatenate([mask_p, mask_c], axis=1)
        for hd in range(NH):
            sl = slice(hd * HD, (hd + 1) * HD)
            s = jnp.where(mask, _dot_nt(qs[:, sl], ks[:, sl]) * scale, -1e30)
            m = jnp.max(s, axis=-1, keepdims=True)
            e = jnp.exp(s - m)
            den = jnp.sum(e, axis=-1, keepdims=True)
            osc[:, sl] = _dot(e.astype(BF16), vs[:, sl]) / den
            lsc[:, sl] = jnp.broadcast_to(m + jnp.log(den), (BAND, HD))
        o_ref[...] = osc[...].reshape(o_ref.shape)
        l_ref[...] = lsc[...].reshape(l_ref.shape)

    cur = lambda b: b
    prev = lambda b: jnp.maximum(b - 1, 0)
    out = _pattern_spec(d, T, cur)
    o, l = _pc(body, name=f"attn_fwd_d{d}_l{layer}", grid=(d, nb),
               in_specs=[_pattern_spec(d, T, cur, 0), _pattern_spec(d, T, prev, 1), _pattern_spec(d, T, cur, 1),
                         _pattern_spec(d, T, prev, 2), _pattern_spec(d, T, cur, 2)],
               out_specs=[out, out], out_shape=[S(vshape, F32)] * 2,
               scratch_shapes=[pltpu.VMEM((BAND, DA), BF16)] + [pltpu.VMEM((2 * BAND, DA), BF16)] * 2
               + [pltpu.VMEM((BAND, DA), F32)] * 2,
               compiler_params=_cp(2))(Pv, Pv, Pv, Pv, Pv)
    return o.reshape(T, DA), l.reshape(T, DA)


def _attn_bwd(P, dO, lse, delta, acc, d, layer):
    if d == 16:
        return _attn_bwd_stream(P, dO, lse, delta, acc, layer)
    T = P.shape[1]
    nb = T // d // BAND
    vshape = _pattern(d, T)[0]
    Pv = P.reshape((NSH,) + vshape)
    scale = HD ** -0.5
    first = acc is None

    def body(*refs):
        q_ref, kp_ref, kc_ref, vp_ref, vc_ref, do_ref, l_ref, dl_ref = refs[:8]
        if first:
            dq_ref, dk_ref, dv_ref = refs[8:11]
        else:
            aq_ref, ak_ref, av_ref, dq_ref, dk_ref, dv_ref = refs[8:14]
        qs, dos, ks, vs, ls, dls, oq, ok, ov, ck, cv = refs[-11:]
        b = pl.program_id(1)
        flat = lambda ref: ref[...].reshape(BAND, DA)

        @pl.when(b == 0)
        def _():
            ck[...] = jnp.zeros_like(ck)
            cv[...] = jnp.zeros_like(cv)

        @pl.when(b < nb)
        def _():
            qs[...] = flat(q_ref).astype(BF16)
            dos[...] = flat(do_ref).astype(BF16)
            ks[0:BAND, :] = flat(kp_ref).astype(BF16)
            ks[BAND:, :] = flat(kc_ref).astype(BF16)
            vs[0:BAND, :] = flat(vp_ref).astype(BF16)
            vs[BAND:, :] = flat(vc_ref).astype(BF16)
            ls[...] = flat(l_ref)
            dls[...] = flat(dl_ref)
            mask_c, mask_p = _band_masks(b, d)
            mask = jnp.concatenate([mask_p, mask_c], axis=1)
            for hd in range(NH):
                sl = slice(hd * HD, (hd + 1) * HD)
                one = slice(hd * HD, hd * HD + 1)
                q, do, kk = qs[:, sl], dos[:, sl], ks[:, sl]
                p = jnp.where(mask, jnp.exp(_dot_nt(q, kk) * scale - ls[:, one]), 0.0)
                ds = (p * (_dot_nt(do, vs[:, sl]) - dls[:, one]) * scale).astype(BF16)
                oq[:, sl] = _dot(ds, kk)
                dk2 = _dot_tn(ds, q)
                dv2 = _dot_tn(p.astype(BF16), do)
                ok[:, sl] = ck[:, sl] + dk2[0:BAND]
                ov[:, sl] = cv[:, sl] + dv2[0:BAND]
                ck[:, sl] = dk2[BAND:]
                cv[:, sl] = dv2[BAND:]
            if first:
                dq_ref[...] = oq[...].reshape(dq_ref.shape)
                dk_ref[...] = ok[...].reshape(dk_ref.shape)
                dv_ref[...] = ov[...].reshape(dv_ref.shape)
            else:
                dq_ref[...] = aq_ref[...] + oq[...].reshape(dq_ref.shape)
                dk_ref[...] = ak_ref[...] + ok[...].reshape(dk_ref.shape)
                dv_ref[...] = av_ref[...] + ov[...].reshape(dv_ref.shape)

        @pl.when(b == nb)
        def _():
            if first:
                dk_ref[...] = ck[...].reshape(dk_ref.shape)
                dv_ref[...] = cv[...].reshape(dv_ref.shape)
            else:
                dk_ref[...] = ak_ref[...] + ck[...].reshape(dk_ref.shape)
                dv_ref[...] = av_ref[...] + cv[...].reshape(dv_ref.shape)

    qb = lambda b: jnp.minimum(b, nb - 1)
    qprev = lambda b: jnp.maximum(qb(b) - 1, 0)
    kb = lambda b: jnp.maximum(b - 1, 0)
    qrow = _pattern_spec(d, T, qb)
    krow = _pattern_spec(d, T, kb)
    view = lambda t: t.reshape(vshape)
    ins = [Pv, Pv, Pv, Pv, Pv, view(dO), view(lse), view(delta)]
    specs = [_pattern_spec(d, T, qb, 0), _pattern_spec(d, T, qprev, 1), _pattern_spec(d, T, qb, 1),
             _pattern_spec(d, T, qprev, 2), _pattern_spec(d, T, qb, 2), qrow, qrow, qrow]
    if not first:
        ins += [view(t) for t in acc]
        specs += [qrow, krow, krow]
    dq, dk, dv = _pc(body, name=f"attn_bwd_d{d}_l{layer}", grid=(d, nb + 1), in_specs=specs,
                     out_specs=[qrow, krow, krow], out_shape=[S(vshape, F32)] * 3,
                     scratch_shapes=[pltpu.VMEM((BAND, DA), BF16)] * 2 + [pltpu.VMEM((2 * BAND, DA), BF16)] * 2
                     + [pltpu.VMEM((BAND, DA), F32)] * 7,
                     compiler_params=_cp(2))(*ins)
    return dq.reshape(T, DA), dk.reshape(T, DA), dv.reshape(T, DA)


def _ssm_prep(lam_re, lam_im, log_dt, b_re, b_im, c_re, c_im):
    dt = jnp.exp(log_dt)[:, None]
    er = jnp.exp(lam_re * dt)
    a_re = er * jnp.cos(lam_im * dt)
    a_im = er * jnp.sin(lam_im * dt)
    nr, ni = a_re - 1.0, a_im
    den = lam_re * lam_re + lam_im * lam_im
    cr = (nr * lam_re + ni * lam_im) / den
    ci = (ni * lam_re - nr * lam_im) / den
    bbr = cr[..., None] * b_re - ci[..., None] * b_im
    bbi = cr[..., None] * b_im + ci[..., None] * b_re
    eye = jnp.eye(8, dtype=F32)

    def bblock(bb):
        t = bb.reshape(4, 8, 64, 16).transpose(0, 1, 3, 2)
        return (t[:, :, :, None, :] * eye[None, :, None, :, None]).reshape(4, 128, 512)

    def cblock(cc):
        t = cc.reshape(4, 8, 16, 64).transpose(0, 1, 3, 2)
        return (t[:, :, :, None, :] * eye[None, :, None, :, None]).reshape(4, 512, 128)

    return (a_re.reshape(NLB, 1, 128), a_im.reshape(NLB, 1, 128), bblock(bbr), bblock(bbi), cblock(c_re), cblock(c_im))


def _perm_matrix(tm):
    n = tm // 16
    pm = np.zeros((tm, tm), np.float32)
    for r in range(16):
        pm[16 * np.arange(n) + r, r * n + np.arange(n)] = 1.0
    return jnp.asarray(pm, BF16)


def _pieces(x):
    p1 = x.astype(BF16)
    r1 = x - p1.astype(F32)
    p2 = r1.astype(BF16)
    return p1, p2, (r1 - p2.astype(F32)).astype(BF16)


def _to_time(x, pm):
    return sum(_dot(pm, p) for p in _pieces(x))


def _to_streams(x, pm):
    return sum(_dot_tn(pm, p) for p in _pieces(x))


def _stream_block(tm, cols, lead=None):
    if lead is None:
        return pl.BlockSpec((16, tm // 16, cols), lambda i: (0, i, 0))
    return pl.BlockSpec((None, 16, tm // 16, cols), lambda i: (lead, 0, i, 0))


def _reorder(t3, to_streams, name):
    B, T, C = t3.shape
    tm = TM

    def body(x_ref, pm_ref, o_ref):
        if to_streams:
            o_ref[...] = _to_streams(x_ref[...], pm_ref[...]).reshape(o_ref.shape)
        else:
            o_ref[...] = _to_time(x_ref[...].reshape(tm, C), pm_ref[...])

    time_blk = pl.BlockSpec((None, tm, C), lambda b, i: (b, i, 0))
    stream_blk = pl.BlockSpec((None, 16, tm // 16, C), lambda b, i: (b, 0, i, 0))
    src = t3 if to_streams else t3.reshape(B, 16, T // 16, C)
    out = _pc(body, name=name, grid=(B, T // tm),
              in_specs=[time_blk if to_streams else stream_blk, pl.BlockSpec((tm, tm), lambda b, i: (0, 0))],
              out_specs=stream_blk if to_streams else time_blk,
              out_shape=S((B, 16, T // 16, C) if to_streams else (B, T, C), F32),
              compiler_params=_cp(2))(src, _perm_matrix(tm))
    return out.reshape(B, T, C)


def _ssm_in(P, bre, bim, layer):
    T = P.shape[1]
    tm = TM

    def body(u_ref, pm_ref, br_ref, bi_ref, un_ref, or_ref, oi_ref):
        u = _to_time(u_ref[...].reshape(tm, DSS), pm_ref[...])
        un_ref[...] = u
        for s in range(4):
            uc = u[:, s * 128:(s + 1) * 128]
            r = _dot3(_dot, uc, br_ref[s])
            m = _dot3(_dot, uc, bi_ref[s])
            for q in range(4):
                or_ref[4 * s + q] = r[:, q * 128:(q + 1) * 128]
                oi_ref[4 * s + q] = m[:, q * 128:(q + 1) * 128]

    whole = pl.BlockSpec((4, 128, 512), lambda i: (0, 0, 0))
    st = pl.BlockSpec((NLB, tm, 128), lambda i: (0, i, 0))
    return _pc(body, name=f"ssm_in_l{layer}", grid=(T // tm,),
               in_specs=[_stream_block(tm, DSS, 3), pl.BlockSpec((tm, tm), lambda i: (0, 0)), whole, whole],
               out_specs=[pl.BlockSpec((tm, DSS), lambda i: (i, 0)), st, st],
               out_shape=[S((T, DSS), F32)] + [S((NLB, T, 128), F32)] * 2,
               compiler_params=_cp(1))(P.reshape(NSH, 16, T // 16, DSS), _perm_matrix(tm), bre, bim)


def _scan(br, bi, a_re, a_im, reverse, layer):
    T = br.shape[1]
    nbk = 4
    tt = min(T, 1024)
    nT = T // tt
    ntile = tt // 8
    sgn = -1.0 if reverse else 1.0
    last = 0 if reverse else 7

    def body(br_ref, bi_ref, ar_ref, ai_ref, xr_ref, xi_ref, cr, ci):
        @pl.when(pl.program_id(1) == 0)
        def _():
            cr[...] = jnp.zeros_like(cr)
            ci[...] = jnp.zeros_like(ci)

        row = lax.broadcasted_iota(jnp.int32, (8, 128), 0)
        consts = []
        for k in range(nbk):
            a1r = jnp.broadcast_to(ar_ref[k], (8, 128))
            a1i = sgn * jnp.broadcast_to(ai_ref[k], (8, 128))
            pows = [(a1r, a1i)]
            for _ in range(7):
                pr, pi_ = pows[-1]
                pows.append((a1r * pr - a1i * pi_, a1r * pi_ + a1i * pr))
            rounds = []
            for s in (1, 2, 4):
                inside = (row <= 7 - s) if reverse else (row >= s)
                rounds.append((jnp.where(inside, pows[s - 1][0], 0.0), jnp.where(inside, pows[s - 1][1], 0.0)))
            cmr, cmi = jnp.zeros((8, 128), F32), jnp.zeros((8, 128), F32)
            for r in range(8):
                e = (7 - r) if reverse else r
                cmr = jnp.where(row == r, pows[e][0], cmr)
                cmi = jnp.where(row == r, pows[e][1], cmi)
            consts.append((rounds, cmr, cmi))

        def tile(i, carry):
            j = (ntile - 1 - i) if reverse else i
            rows = pl.ds(pl.multiple_of(j * 8, 8), 8)
            out = []
            for k in range(nbk):
                rounds, cmr, cmi = consts[k]
                xr = br_ref[k, rows, :]
                xi = bi_ref[k, rows, :]
                for (mr, mi), s in zip(rounds, (1, 2, 4)):
                    sh = (8 - s) if reverse else s
                    rr = pltpu.roll(xr, sh, 0)
                    ri = pltpu.roll(xi, sh, 0)
                    xr, xi = xr + (mr * rr - mi * ri), xi + (mr * ri + mi * rr)
                c_r, c_i = carry[k]
                xr, xi = xr + (cmr * c_r - cmi * c_i), xi + (cmr * c_i + cmi * c_r)
                xr_ref[k, rows, :] = xr
                xi_ref[k, rows, :] = xi
                out.append((jnp.broadcast_to(xr[last:last + 1, :], (8, 128)),
                            jnp.broadcast_to(xi[last:last + 1, :], (8, 128))))
            return tuple(out)

        carry = lax.fori_loop(0, ntile, tile, tuple((cr[k], ci[k]) for k in range(nbk)), unroll=2)
        for k in range(nbk):
            cr[k] = carry[k][0]
            ci[k] = carry[k][1]

    tmap = (lambda t: nT - 1 - t) if reverse else (lambda t: t)
    st = pl.BlockSpec((nbk, tt, 128), lambda i, t: (i, tmap(t), 0))
    av = pl.BlockSpec((nbk, 1, 128), lambda i, t: (i, 0, 0))
    return _pc(body, name=f"scan_{'bwd' if reverse else 'fwd'}_l{layer}", grid=(NLB // nbk, nT),
               in_specs=[st, st, av, av], out_specs=[st, st], out_shape=[S((NLB, T, 128), F32)] * 2,
               scratch_shapes=[pltpu.VMEM((nbk, 8, 128), F32)] * 2, compiler_params=_cp(2))(br, bi, a_re, a_im)


def _ssm_out(xr, xi, u, cre, cim, dvec, wglu, bglu, layer):
    T = u.shape[0]
    tm = TM

    def body(xr_ref, xi_ref, u_ref, pm_ref, cr_ref, ci_ref, d_ref, w_ref, bg_ref, s_ref, y_ref, z_ref):
        ys = []
        for s in range(4):
            xrc = jnp.concatenate([xr_ref[4 * s + q] for q in range(4)], axis=1)
            xic = jnp.concatenate([xi_ref[4 * s + q] for q in range(4)], axis=1)
            ys.append(_dot3(_dot, xrc, cr_ref[s]) - _dot3(_dot, xic, ci_ref[s]))
        y = jnp.concatenate(ys, axis=1) + d_ref[...] * u_ref[...]
        yg = _gelu(y)
        ygb = yg.astype(BF16)
        z = bg_ref[...] + sum(_dot(ygb[:, j * 128:(j + 1) * 128], w_ref[j]) for j in range(NSH))
        y_ref[...] = y
        z_ref[...] = z
        s_ref[...] = _to_streams(yg * jax.nn.sigmoid(z), pm_ref[...]).reshape(s_ref.shape)

    st = pl.BlockSpec((NLB, tm, 128), lambda i: (0, i, 0))
    cw = pl.BlockSpec((4, 512, 128), lambda i: (0, 0, 0))
    half = pl.BlockSpec((tm, DSS), lambda i: (i, 0))
    s, y, z = _pc(body, name=f"ssm_out_l{layer}", grid=(T // tm,),
                  in_specs=[st, st, half, pl.BlockSpec((tm, tm), lambda i: (0, 0)), cw, cw, _gain_spec(DSS, layer),
                            pl.BlockSpec((NSH, None, 128, DSS), lambda i: (0, 0, 0, 0)), _gain_spec(DSS, layer)],
                  out_specs=[_stream_block(tm, DSS), half, half],
                  out_shape=[S((16, T // 16, DSS), F32), S((T, DSS), F32), S((T, DSS), F32)],
                  compiler_params=_cp(1))(xr, xi, u, _perm_matrix(tm), cre, cim, dvec, wglu, bglu)
    return s.reshape(T, DSS), y, z


def _ssm_out_bwd(dssm, y, z, xr, xi, u, cre, cim, dvec, wglu, layer):
    T = u.shape[0]
    tm = TM

    def body(ds_ref, pm_ref, y_ref, z_ref, xr_ref, xi_ref, u_ref, cr_ref, ci_ref, d_ref, w_ref,
             gr_ref, gi_ref, du_ref, dz_ref, yg_ref, dbg_ref, dd_ref, dcr_ref, dci_ref):
        i = pl.program_id(0)

        @pl.when(i == 0)
        def _():
            dbg_ref[...] = jnp.zeros_like(dbg_ref)
            dd_ref[...] = jnp.zeros_like(dd_ref)
            dcr_ref[...] = jnp.zeros_like(dcr_ref)
            dci_ref[...] = jnp.zeros_like(dci_ref)

        yv = y_ref[...]
        yg = _gelu(yv)
        sg = jax.nn.sigmoid(z_ref[...])
        ds = _to_time(ds_ref[...].reshape(tm, DSS), pm_ref[...])
        dz = ds * yg * sg * (1.0 - sg)
        dzb = dz.astype(BF16)
        dz_ref[...] = dzb
        yg_ref[...] = yg.astype(BF16)
        dbg_ref[...] += jnp.sum(dz, axis=0, keepdims=True)
        dyg = ds * sg + jnp.concatenate([_dot_nt(dzb, w_ref[j]) for j in range(NSH)], axis=1)
        dy = dyg * _gelu_grad(yv)
        u = u_ref[...]
        dd_ref[...] += jnp.sum(dy * u, axis=0, keepdims=True)
        du_ref[...] = dy * d_ref[...]
        for s in range(4):
            dyc = dy[:, s * 128:(s + 1) * 128]
            g_r = _dot3(_dot_nt, dyc, cr_ref[s])
            g_i = -_dot3(_dot_nt, dyc, ci_ref[s])
            for q in range(4):
                gr_ref[4 * s + q] = g_r[:, q * 128:(q + 1) * 128]
                gi_ref[4 * s + q] = g_i[:, q * 128:(q + 1) * 128]
            xrc = jnp.concatenate([xr_ref[4 * s + q] for q in range(4)], axis=1)
            xic = jnp.concatenate([xi_ref[4 * s + q] for q in range(4)], axis=1)
            dcr_ref[s] += _dot3(_dot_tn, xrc, dyc)
            dci_ref[s] -= _dot3(_dot_tn, xic, dyc)

    st = pl.BlockSpec((NLB, tm, 128), lambda i: (0, i, 0))
    cw = pl.BlockSpec((4, 512, 128), lambda i: (0, 0, 0))
    half = pl.BlockSpec((tm, DSS), lambda i: (i, 0))
    return _pc(body, name=f"ssm_out_bwd_l{layer}", grid=(T // tm,),
               in_specs=[_stream_block(tm, DSS), pl.BlockSpec((tm, tm), lambda i: (0, 0)), half, half, st, st, half,
                         cw, cw, _gain_spec(DSS, layer), pl.BlockSpec((NSH, None, 128, DSS), lambda i: (0, 0, 0, 0))],
               out_specs=[st, st, half, half, half, _row_acc_spec(DSS), _row_acc_spec(DSS), cw, cw],
               out_shape=[S((NLB, T, 128), F32)] * 2 + [S((T, DSS), F32), S((T, DSS), BF16), S((T, DSS), BF16),
                                                        S((1, DSS), F32), S((1, DSS), F32),
                                                        S((4, 512, 128), F32), S((4, 512, 128), F32)],
               compiler_params=_cp(1))(dssm.reshape(16, T // 16, DSS), _perm_matrix(tm), y, z, xr, xi, u, cre, cim,
                                       dvec, wglu)


def _ssm_da(gr, gi, xr, xi, layer):
    T = gr.shape[1]
    tb = 4096 if T % 4096 == 0 else T

    def body(gr_ref, gi_ref, xr_ref, xi_ref, dr_ref, di_ref, lr, li):
        t = pl.program_id(1)

        @pl.when(t == 0)
        def _():
            dr_ref[...] = jnp.zeros_like(dr_ref)
            di_ref[...] = jnp.zeros_like(di_ref)
            lr[...] = jnp.zeros_like(lr)
            li[...] = jnp.zeros_like(li)

        g_r, g_i, x_r, x_i = gr_ref[...], gi_ref[...], xr_ref[...], xi_ref[...]
        pr = pltpu.roll(x_r, 1, 0)
        pi_ = pltpu.roll(x_i, 1, 0)
        g0r, g0i = g_r[0:1, :], g_i[0:1, :]
        fr = lr[7:8, :] - x_r[tb - 1:tb, :]
        fi = li[7:8, :] - x_i[tb - 1:tb, :]
        dr_ref[...] += jnp.sum(g_r * pr + g_i * pi_, axis=0, keepdims=True) + g0r * fr + g0i * fi
        di_ref[...] += jnp.sum(g_i * pr - g_r * pi_, axis=0, keepdims=True) + g0i * fr - g0r * fi
        lr[...] = x_r[tb - 8:tb, :]
        li[...] = x_i[tb - 8:tb, :]

    st = pl.BlockSpec((None, tb, 128), lambda k, t: (k, t, 0))
    out = pl.BlockSpec((None, 1, 128), lambda k, t: (k, 0, 0))
    return _pc(body, name=f"ssm_da_l{layer}", grid=(NLB, T // tb), in_specs=[st] * 4, out_specs=[out, out],
               out_shape=[S((NLB, 1, 128), F32)] * 2, scratch_shapes=[pltpu.VMEM((8, 128), F32)] * 2,
               compiler_params=_cp(2))(gr, gi, xr, xi)


def _ssm_in_bwd(gr, gi, u, bre, bim, du_direct, layer):
    T = u.shape[0]
    tm = TM

    def body(gr_ref, gi_ref, u_ref, pm_ref, br_ref, bi_ref, dd_ref, du_ref, dbr_ref, dbi_ref):
        i = pl.program_id(0)

        @pl.when(i == 0)
        def _():
            dbr_ref[...] = jnp.zeros_like(dbr_ref)
            dbi_ref[...] = jnp.zeros_like(dbi_ref)

        dus = []
        for s in range(4):
            grc = jnp.concatenate([gr_ref[4 * s + q] for q in range(4)], axis=1)
            gic = jnp.concatenate([gi_ref[4 * s + q] for q in range(4)], axis=1)
            uc = u_ref[:, s * 128:(s + 1) * 128]
            dus.append(_dot3(_dot_nt, grc, br_ref[s]) + _dot3(_dot_nt, gic, bi_ref[s]))
            dbr_ref[s] += _dot3(_dot_tn, uc, grc)
            dbi_ref[s] += _dot3(_dot_tn, uc, gic)
        du = jnp.concatenate(dus, axis=1) + dd_ref[...]
        du_ref[...] = _to_streams(du, pm_ref[...]).reshape(du_ref.shape)

    whole = pl.BlockSpec((4, 128, 512), lambda i: (0, 0, 0))
    st = pl.BlockSpec((NLB, tm, 128), lambda i: (0, i, 0))
    half = pl.BlockSpec((tm, DSS), lambda i: (i, 0))
    du, dbr, dbi = _pc(body, name=f"ssm_in_bwd_l{layer}", grid=(T // tm,),
                       in_specs=[st, st, half, pl.BlockSpec((tm, tm), lambda i: (0, 0)), whole, whole, half],
                       out_specs=[_stream_block(tm, DSS), whole, whole],
                       out_shape=[S((16, T // 16, DSS), F32), S((4, 128, 512), F32), S((4, 128, 512), F32)],
                       compiler_params=_cp(1))(gr, gi, u, _perm_matrix(tm), bre, bim, du_direct)
    return du.reshape(T, DSS), dbr, dbi


def _mix_out(outs, lses, ssm, h, attn_g, ssm_g, post_g, wout, layer):
    T = h.shape[0]
    tm = TM

    def body(o1, o2, o3, l1, l2, l3, s_ref, h_ref, ag_ref, sg_ref, pg_ref, w_ref, ho_ref, at_ref, ls_ref, mx_ref, mo_ref):
        la, lb, lc = l1[...], l2[...], l3[...]
        m = jnp.maximum(jnp.maximum(la, lb), lc)
        wa, wb, wc = jnp.exp(la - m), jnp.exp(lb - m), jnp.exp(lc - m)
        zs = wa + wb + wc
        attn = (wa * o1[...] + wb * o2[...] + wc * o3[...]) / zs
        at_ref[...] = attn
        ls_ref[...] = m + jnp.log(zs)
        mixed = jnp.concatenate([_rms_fwd(attn, ag_ref[...]), _rms_fwd(s_ref[...], sg_ref[...])], axis=1).astype(BF16)
        mx_ref[...] = mixed
        mo = sum(_dot(mixed[:, j * 256:(j + 1) * 256], w_ref[j]) for j in range(NSH))
        mo_ref[...] = mo
        ho_ref[...] = h_ref[...] + _rms_fwd(mo, pg_ref[...])

    row = pl.BlockSpec((tm, D), lambda i: (i, 0))
    half = pl.BlockSpec((tm, DA), lambda i: (i, 0))
    return _pc(body, name=f"mix_out_l{layer}", grid=(T // tm,),
               in_specs=[half] * 7 + [row, _gain_spec(DA, layer), _gain_spec(DSS, layer), _gain_spec(D, layer),
                                      pl.BlockSpec((NSH, None, 256, D), lambda i: (0, 0, 0, 0))],
               out_specs=[row, half, half, row, row],
               out_shape=[S((T, D), F32), S((T, DA), F32), S((T, DA), F32), S((T, D), BF16), S((T, D), F32)],
               compiler_params=_cp(1))(*outs, *lses, ssm, h, attn_g, ssm_g, post_g, wout)


def _mix_out_bwd(dout, mo, attn, ssm, attn_g, ssm_g, post_g, wout, layer):
    T = dout.shape[0]
    tm = TM
    head_sum =jnp.asarray(np.kron(np.eye(NH, dtype=np.float32), np.ones((HD, HD), np.float32)), BF16)

    def body(do_ref, mo_ref, at_ref, s_ref, ag_ref, sg_ref, pg_ref, w_ref, e_ref,
             da_ref, ds_ref, dl_ref, dmo_ref, dpg_ref, dag_ref, dsg_ref):
        i = pl.program_id(0)

        @pl.when(i == 0)
        def _():
            dpg_ref[...] = jnp.zeros_like(dpg_ref)
            dag_ref[...] = jnp.zeros_like(dag_ref)
            dsg_ref[...] = jnp.zeros_like(dsg_ref)

        dmo, dpg = _rms_bwd(do_ref[...], mo_ref[...], pg_ref[...])
        dpg_ref[...] += dpg
        dmob = dmo.astype(BF16)
        dmo_ref[...] = dmob
        dmix = jnp.concatenate([_dot_nt(dmob, w_ref[j]) for j in range(NSH)], axis=1)
        attn = at_ref[...]
        dat, dag = _rms_bwd(dmix[:, :DA], attn, ag_ref[...])
        dss, dsg = _rms_bwd(dmix[:, DA:], s_ref[...], sg_ref[...])
        dag_ref[...] += dag
        dsg_ref[...] += dsg
        da_ref[...] = dat
        ds_ref[...] = dss
        prod = dat * attn
        p1 = prod.astype(BF16)
        r1 = prod - p1.astype(F32)
        p2 = r1.astype(BF16)
        p3 = (r1 - p2.astype(F32)).astype(BF16)
        e = e_ref[...]
        dl_ref[...] = _dot(p1, e) + _dot(p2, e) + _dot(p3, e)

    row = pl.BlockSpec((tm, D), lambda i: (i, 0))
    half = pl.BlockSpec((tm, DA), lambda i: (i, 0))
    return _pc(body, name=f"mix_out_bwd_l{layer}", grid=(T // tm,),
               in_specs=[row, row, half, half, _gain_spec(DA, layer), _gain_spec(DSS, layer), _gain_spec(D, layer),
                         pl.BlockSpec((NSH, None, 256, D), lambda i: (0, 0, 0, 0)),
                         pl.BlockSpec((DA, DA), lambda i: (0, 0))],
               out_specs=[half, half, half, row, _row_acc_spec(D), _row_acc_spec(DA), _row_acc_spec(DSS)],
               out_shape=[S((T, DA), F32)] * 3 + [S((T, D), BF16), S((1, D), F32), S((1, DA), F32), S((1, DSS), F32)],
               compiler_params=_cp(1))(dout, mo, attn, ssm, attn_g, ssm_g, post_g, wout, head_sum)


def _ple_fwd(h, p3, wup, wgate, post_g, layer):
    T = h.shape[0]
    tm = TM

    def body(h_ref, p_ref, wu_ref, wg_ref, g_ref, ho_ref, e_ref, gt_ref):
        hv = h_ref[...]
        hb = hv.astype(BF16)
        pb = p_ref[...].astype(BF16)
        gte = sum(_dot(hb[:, j * 256:(j + 1) * 256], wg_ref[j]) for j in range(NSH))
        e = jnp.concatenate([_dot(pb, wu_ref[j]) for j in range(NSH)], axis=1)
        e_ref[...] = e
        gt_ref[...] = gte
        ho_ref[...] = hv + _rms_fwd(e * jax.nn.sigmoid(gte), g_ref[...])

    row = pl.BlockSpec((tm, D), lambda i: (i, 0))
    return _pc(body, name=f"ple_fwd_l{layer}", grid=(T // tm,),
               in_specs=[row, pl.BlockSpec((None, tm, PLE), lambda i: (layer, i, 0)),
                         pl.BlockSpec((NSH, None, PLE, 256), lambda i: (0, 0, 0, 0)),
                         pl.BlockSpec((NSH, None, 256, D), lambda i: (0, 0, 0, 0)), _gain_spec(D, layer)],
               out_specs=[row, row, row], out_shape=[S((T, D), F32)] * 3,
               compiler_params=_cp(1))(h, p3, wup, wgate, post_g)


def _ple_bwd(dout, e, gte, wgate, post_g, layer):
    T = dout.shape[0]
    tm = TM

    def body(do_ref, e_ref, gt_ref, wg_ref, g_ref, dh_ref, de_ref, dgt_ref, dg_ref):
        i = pl.program_id(0)

        @pl.when(i == 0)
        def _():
            dg_ref[...] = jnp.zeros_like(dg_ref)

        ev = e_ref[...]
        sg = jax.nn.sigmoid(gt_ref[...])
        do = do_ref[...]
        dple, dg = _rms_bwd(do, ev * sg, g_ref[...])
        dg_ref[...] += dg
        de = (dple * sg).astype(BF16)
        for j in range(NSH):
            de_ref[j] = de[:, j * 256:(j + 1) * 256]
        dgb = (dple * ev * sg * (1.0 - sg)).astype(BF16)
        dgt_ref[...] = dgb
        dh_ref[...] = do + jnp.concatenate([_dot_nt(dgb, wg_ref[j]) for j in range(NSH)], axis=1)

    row = pl.BlockSpec((tm, D), lambda i: (i, 0))
    return _pc(body, name=f"ple_bwd_l{layer}", grid=(T // tm,),
               in_specs=[row, row, row, pl.BlockSpec((NSH, None, 256, D), lambda i: (0, 0, 0, 0)), _gain_spec(D, layer)],
               out_specs=[row, pl.BlockSpec((NSH, tm, 256), lambda i: (0, i, 0)), row, _row_acc_spec(D)],
               out_shape=[S((T, D), F32), S((NSH, T, 256), BF16), S((T, D), BF16), S((1, D), F32)],
               compiler_params=_cp(1))(dout, e, gte, wgate, post_g)


def _loss_head(h, target):
    T = h.shape[0]
    tm = TM

    def body(h_ref, t_ref, dy_ref, l_ref):
        i = pl.program_id(0)

        @pl.when(i == 0)
        def _():
            l_ref[...] = jnp.zeros_like(l_ref)

        err = h_ref[...] - t_ref[...]
        dy_ref[...] = err * (1.0 / D)
        l_ref[...] += jnp.broadcast_to((0.5 / D) * jnp.sum(err * err), (1, 128))

    row = pl.BlockSpec((tm, D), lambda i: (i, 0))
    return _pc(body, name="loss_head", grid=(T // tm,), in_specs=[row, row],
               out_specs=[row, pl.BlockSpec((1, 128), lambda i: (0, 0))],
               out_shape=[S((T, D), F32), S((1, 128), F32)], compiler_params=_cp(1))(h, target)


def _local_step(x, p3, pos_col, target, weights_of, layer_grads_done, Sm):
    L = p3.shape[0]
    g3 = {n: Sm[n].reshape(L, 1, -1) for n in ("ffn1_pre_g", "ffn1_post_g", "mix_pre_g", "attn_norm_g", "ssm_norm_g",
                                                "mix_post_g", "ffn2_pre_g", "ffn2_post_g", "ple_post_g", "ssm_b_glu", "ssm_d")}
    rot = _rot_tables(pos_col)
    prep_names = ("ssm_lam_re", "ssm_lam_im", "ssm_log_dt", "ssm_b_re", "ssm_b_im", "ssm_c_re", "ssm_c_im")
    prep_all, prep_vjp = jax.vjp(jax.vmap(_ssm_prep), *[Sm[n] for n in prep_names])
    prep_cot = [None] * L

    saved = []
    h = x
    for l in range(L):
        W = weights_of(l, h)
        sv = {"h0": h, "W": W}
        h, sv["a1"], sv["b1"], sv["f1"], sv["xn1"] = _ffn_fwd(
            h, g3["ffn1_pre_g"], g3["ffn1_post_g"], W["ffn1_w_gate"], W["ffn1_w_up"], W["ffn1_w_down"], l, "1")
        sv["h1"] = h
        P, sv["ain"] = _mix_proj(h, g3["mix_pre_g"], W["w_in"], rot, l)
        sv["P"] = P
        ol = [_attn_fwd(P, d, l) for d in PATTERN_DILATIONS]
        prep = tuple(t[l] for t in prep_all)
        a_re, a_im, bre, bim, cre, cim = prep
        sv["prep"] = prep
        sv["u"], bur, bui = _ssm_in(P, bre, bim, l)
        xr, xi = _scan(bur, bui, a_re, a_im, False, l)
        sv["xr"], sv["xi"] = xr, xi
        ssm, sv["y"], sv["z"] = _ssm_out(xr, xi, sv["u"], cre, cim, g3["ssm_d"], W["ssm_w_glu"], g3["ssm_b_glu"], l)
        sv["ssm"] = ssm
        h, sv["attn"], sv["lse"], sv["mixed"], sv["mo"] = _mix_out(
            [o for o, _ in ol], [s for _, s in ol], ssm, h, g3["attn_norm_g"], g3["ssm_norm_g"], g3["mix_post_g"],
            W["w_out"], l)
        sv["h2"] = h
        h, sv["a2"], sv["b2"], sv["f2"], sv["xn2"] = _ffn_fwd(
            h, g3["ffn2_pre_g"], g3["ffn2_post_g"], W["ffn2_w_gate"], W["ffn2_w_up"], W["ffn2_w_down"], l, "2")
        sv["h3"] = h
        h, sv["e"], sv["gte"] = _ple_fwd(h, p3, W["ple_w_up"], W["ple_w_gate"], g3["ple_post_g"], l)
        saved.append(sv)

    dh, loss = _loss_head(h, target)

    G_layers = [{n: lax.empty((NSH, 1, r, c), BF16) for n, r, c in BIG} for _ in range(L)]
    sg = {n: [None] * L for n in SMALL}
    whole, shard, kcol = "whole", "shard", "cols"
    ple_g = g3["ple_post_g"]
    for l in reversed(range(L)):
        sv = saved[l]
        W = sv["W"]
        G, gl = G_layers[l], 0
        if l + 1 < L:
            ple_g = ple_g + layer_grads_done(l + 1, G_layers[l + 1])
        dh, de, dgte, sg["ple_post_g"][l] = _ple_bwd(dh, sv["e"], sv["gte"], W["ple_w_gate"], ple_g, l)
        G["ple_w_up"] = _dw(p3[l][None], de, G["ple_w_up"], gl, PLE, 256, whole, shard, f"dw_ple_up_l{l}")
        G["ple_w_gate"] = _dw(sv["h3"][None], dgte[None], G["ple_w_gate"], gl, 256, D, kcol, whole, f"dw_ple_gate_l{l}")
        dh, df, da, db, hh, sg["ffn2_pre_g"][l], sg["ffn2_post_g"][l] = _ffn_bwd(
            dh, sv["h2"], sv["f2"], sv["a2"], sv["b2"], g3["ffn2_pre_g"], g3["ffn2_post_g"],
            W["ffn2_w_gate"], W["ffn2_w_up"], W["ffn2_w_down"], l, "2")
        G["ffn2_w_gate"] = _dw(da, sv["xn2"][None], G["ffn2_w_gate"], gl, DFS, D, shard, whole, f"dw_ffn2_gate_l{l}")
        G["ffn2_w_up"] = _dw(db, sv["xn2"][None], G["ffn2_w_up"], gl, DFS, D, shard, whole, f"dw_ffn2_up_l{l}")
        G["ffn2_w_down"] = _dw(hh, df[None], G["ffn2_w_down"], gl, DFS, D, shard, whole, f"dw_ffn2_down_l{l}")
        a_re, a_im, bre, bim, cre, cim = sv["prep"]
        dattn, dssm, delta, dmo, sg["mix_post_g"][l], sg["attn_norm_g"][l], sg["ssm_norm_g"][l] = _mix_out_bwd(
            dh, sv["mo"], sv["attn"], sv["ssm"], g3["attn_norm_g"], g3["ssm_norm_g"], g3["mix_post_g"], W["w_out"], l)
        G["w_out"] = _dw(sv["mixed"][None], dmo[None], G["w_out"], gl, 256, D, kcol, whole, f"dw_out_l{l}")
        gnr, gni, du_direct, dz, yg, sg["ssm_b_glu"][l], dd, dcre, dcim = _ssm_out_bwd(
            dssm, sv["y"], sv["z"], sv["xr"], sv["xi"], sv["u"], cre, cim, g3["ssm_d"], W["ssm_w_glu"], l)
        sg["ssm_d"][l] = dd.reshape(Sm["ssm_d"].shape[1:])
        G["ssm_w_glu"] = _dw(yg[None], dz[None], G["ssm_w_glu"], gl, 128, DSS, kcol, whole, f"dw_glu_l{l}")
        gr, gi = _scan(gnr, gni, a_re, a_im, True, l)
        dar, dai = _ssm_da(gr, gi, sv["xr"], sv["xi"], l)
        du, dbre, dbim = _ssm_in_bwd(gr, gi, sv["u"], bre, bim, du_direct, l)
        prep_cot[l] = (dar, dai, dbre, dbim, dcre, dcim)
        acc = None
        for d in PATTERN_DILATIONS:
            acc = _attn_bwd(sv["P"], dattn, sv["lse"], delta, acc, d, l)
        dh, dP, sg["mix_pre_g"][l] = _mix_proj_bwd(acc[0], acc[1], acc[2], du, dh, sv["h1"], g3["mix_pre_g"],
                                                   W["w_in"], rot, l)
        G["w_in"] = _dw(sv["ain"][None], dP, G["w_in"], gl, D, DA,whole, shard, f"dw_in_l{l}")
        dh, df, da, db, hh, sg["ffn1_pre_g"][l], sg["ffn1_post_g"][l] = _ffn_bwd(
            dh, sv["h0"], sv["f1"], sv["a1"], sv["b1"], g3["ffn1_pre_g"], g3["ffn1_post_g"],
            W["ffn1_w_gate"], W["ffn1_w_up"], W["ffn1_w_down"], l, "1")
        G["ffn1_w_gate"] = _dw(da, sv["xn1"][None], G["ffn1_w_gate"], gl, DFS, D, shard, whole, f"dw_ffn1_gate_l{l}")
        G["ffn1_w_up"] = _dw(db, sv["xn1"][None], G["ffn1_w_up"], gl, DFS, D, shard, whole, f"dw_ffn1_up_l{l}")
        G["ffn1_w_down"] = _dw(hh, df[None], G["ffn1_w_down"], gl, DFS, D, shard, whole, f"dw_ffn1_down_l{l}")

    small = {n: jnp.stack([g.reshape(Sm[n].shape[1:]) for g in sg[n]]) for n in SMALL if n not in prep_names}
    small.update(zip(prep_names, prep_vjp(tuple(jnp.stack(c) for c in zip(*prep_cot)))))
    layer_grads_done(0, G_layers[0])
    return loss, dh, small


HBM_SPEC = pl.BlockSpec(memory_space=pltpu.HBM)


def _place():
    x, y, c = lax.axis_index("x"), lax.axis_index("y"), lax.axis_index("c")
    chips = [(1 - x, y), (x, 1 - y), (1 - x, 1 - y)]
    return x, y, c, chips


def _comm_params():
    return pltpu.CompilerParams(vmem_limit_bytes=VMEM_LIMIT)


def _gather_weights(ws, lands):
    n = len(ws)

    def body(*refs):
        ins, outs = refs[:n], refs[2 * n:3 * n]
        s_ici, r_ici, s_d2d, r_d2d = refs[3 * n:]
        x, y, c, chips = _place()

        def half(ref, t, hc):
            r2 = ws[t].shape[1] // 2
            return ref.at[:, pl.ds(hc * r2, r2), :]

        def ici(t, k, src_chip, to):
            j = 2 * src_chip[0] + src_chip[1]
            src = half(ins[t], t, c) if to is not None else half(outs[t].at[j], t, c)
            return pltpu.make_async_remote_copy(src_ref=src, dst_ref=half(outs[t].at[j], t, c),
                                                send_sem=s_ici.at[3 * t + k], recv_sem=r_ici.at[3 * t + k],
                                                device_id=to if to is not None else (x, y, c), device_id_type=MESH)

        def d2d(t, k, hc):
            j = 2 * chips[k][0] + chips[k][1]
            r = half(outs[t].at[j], t, hc)
            return pltpu.make_async_remote_copy(src_ref=r, dst_ref=r, send_sem=s_d2d.at[3 * t + k],
                                                recv_sem=r_d2d.at[3 * t + k], device_id=(x, y, 1 - c),
                                                device_id_type=MESH)

        sends = [ici(t, k, (x, y), (*chips[k], c)) for t in range(n) for k in range(3)]
        for cp in sends:
            cp.start()
        passed = []
        for t in range(n):
            for k in range(3):
                ici(t, k, chips[k], None).wait_recv()
                passed.append(d2d(t, k, c))
                passed[-1].start()
        for t in range(n):
            for k in range(3):
                d2d(t, k, 1 - c).wait_recv()
        for cp in sends + passed:
            cp.wait_send()

    return _pc(body, name="gather_weights", in_specs=[HBM_SPEC] * (2 * n), out_specs=[HBM_SPEC] * n,
               out_shape=[S(z.shape, z.dtype) for z in lands], input_output_aliases={n + t: t for t in range(n)},
               scratch_shapes=[pltpu.SemaphoreType.DMA((3 * n,))] * 4, compiler_params=_comm_params())(*ws, *lands)


SEM_SPEC = pl.BlockSpec(memory_space=pltpu.SEMAPHORE)
ANY_SPEC = pl.BlockSpec(memory_space=pl.ANY)
SPLIT_EFFECT = pltpu.SideEffectType.DATAFLOW_SIDE_EFFECTING


def _in_hbm(t):
    return pltpu.with_memory_space_constraint(t, pltpu.HBM)


def _place_own(ws, me_arr, layer):
    n = len(ws)

    def body(me_ref, *refs):
        for t in range(n):
            refs[n + t][...] = refs[t][...]

    gs = pltpu.PrefetchScalarGridSpec(
        num_scalar_prefetch=1, grid=(2,),
        in_specs=[pl.BlockSpec((w.shape[0], w.shape[1] // 2, w.shape[2]), lambda i, me: (0, i, 0)) for w in ws],
        out_specs=[pl.BlockSpec((None, w.shape[0], w.shape[1] // 2, w.shape[2]), lambda i, me: (me[0], 0, i, 0))
                   for w in ws])
    return _pc(body, name=f"gather_place_own_l{layer}", grid_spec=gs,
               out_shape=[S((NSH,) + w.shape, w.dtype) for w in ws], compiler_params=_cp(1))(me_arr, *ws)


def _gather_start(ws, lands, after, layer):
    n = len(ws)

    def body(*refs):
        ins, lz = refs[:n], refs[n:2 * n]
        s_sem, r_sem = refs[2 * n + 1], refs[2 * n + 2]
        token = refs[-1]
        x, y, c, chips = _place()
        for t in range(n):
            for k in range(3):
                pltpu.make_async_remote_copy(src_ref=ins[t], dst_ref=lz[t].at[2 * x + y], send_sem=s_sem.at[3 * t + k],
                                             recv_sem=r_sem.at[3 * t + k], device_id=(*chips[k], c),
                                             device_id_type=MESH).start()
        token[...] = jnp.zeros_like(token)

    hbm = [pltpu.HBM(w.shape, w.dtype) for w in ws] + [pltpu.HBM(z.shape, z.dtype) for z in lands]
    out = _pc(body, name=f"gather_start_l{layer}",
              out_shape=(pltpu.SemaphoreType.DMA((3 * n,)), pltpu.SemaphoreType.DMA((3 * n,)), *hbm, S((8, 128), F32)),
              in_specs=[HBM_SPEC] * (2 * n) + [ANY_SPEC],
              out_specs=(SEM_SPEC, SEM_SPEC, *([HBM_SPEC] * (2 * n)), pl.BlockSpec(memory_space=pltpu.VMEM)),
              input_output_aliases={i: 2 + i for i in range(2 * n)},
              compiler_params=pltpu.CompilerParams(has_side_effects=SPLIT_EFFECT))(
                  *[_in_hbm(w) for w in ws], *[_in_hbm(z) for z in lands], after)
    return out[0], out[1], out[2:2 + n], out[2 + n:2 + 2 * n], out[-1]


def _gather_wait(s_sem, r_sem, ws, lands, after, layer):
    n = len(ws)

    def body(*refs):
        ins, lz = refs[:n], refs[n:2 * n]
        s_ref, r_ref = refs[2 * n], refs[2 * n + 1]
        x, y, c, chips = _place()
        for t in range(n):
            for k in range(3):
                cp = pltpu.make_async_remote_copy(src_ref=ins[t], dst_ref=lz[t].at[2 * x + y], send_sem=s_ref.at[3 * t + k],
                                                  recv_sem=r_ref.at[3 * t + k], device_id=(*chips[k], c),
                                                  device_id_type=MESH)
                cp.wait_send()
                cp.wait_recv()

    hbm = [pltpu.HBM(w.shape, w.dtype) for w in ws] + [pltpu.HBM(z.shape, z.dtype) for z in lands]
    out = _pc(body, name=f"gather_wait_l{layer}", out_shape=tuple(hbm),
              in_specs=[HBM_SPEC] * (2 * n) + [SEM_SPEC, SEM_SPEC, ANY_SPEC], out_specs=tuple([HBM_SPEC] * (2 * n)),
              input_output_aliases={i: i for i in range(2 * n)},
              compiler_params=pltpu.CompilerParams(has_side_effects=SPLIT_EFFECT))(*ws, *lands, s_sem, r_sem, after)
    return out[n:]


def _direct_grad_copies(ins, lz, s_sem, r_sem):
    x, y, c, chips = _place()
    sends, recvs = [], []
    for t in range(len(ins)):
        r2 = ins[t].shape[2] // 2
        half = lambda j, h: ins[t].at[j, :, pl.ds(h * r2, r2), :]

        def copy(src, slot, s_idx, r_idx, to):
            return pltpu.make_async_remote_copy(src_ref=src, dst_ref=lz[t].at[slot], send_sem=s_sem.at[7 * t + s_idx],
                                                recv_sem=r_sem.at[7 * t + r_idx], device_id=to, device_id_type=MESH)

        for k in range(3):
            for h in range(2):
                sends.append(copy(half(2 * chips[k][0] + chips[k][1], h), 2 * k + c, 2 * k + h, 2 * k + c, (*chips[k], h)))
        sends.append(copy(half(2 * x + y, 1 - c), 6, 6, 6, (x, y, 1 - c)))
        recvs += [copy(half(0, 0), s, s, s, (x, y, c)) for s in range(7)]
    return sends, recvs


def _send_start(gs, lands, layer):
    n = len(gs)
    ps = gs

    def body(*refs):
        sends, _ = _direct_grad_copies(refs[:n], refs[n:2 * n], refs[2 * n], refs[2 * n + 1])
        for cp in sends:
            cp.start()
        refs[-1][...] = jnp.zeros_like(refs[-1])

    hbm = [pltpu.HBM(p.shape, p.dtype) for p in ps] + [pltpu.HBM(z.shape, z.dtype) for z in lands]
    out = _pc(body, name=f"grad_send_start_l{layer}",
              out_shape=(pltpu.SemaphoreType.DMA((7 * n,)), pltpu.SemaphoreType.DMA((7 * n,)), *hbm, S((8, 128), F32)),
              in_specs=[HBM_SPEC] * (2 * n),
              out_specs=(SEM_SPEC, SEM_SPEC, *([HBM_SPEC] * (2 * n)), pl.BlockSpec(memory_space=pltpu.VMEM)),
              input_output_aliases={i: 2 + i for i in range(2 * n)},
              compiler_params=pltpu.CompilerParams(has_side_effects=SPLIT_EFFECT))(
                  *[_in_hbm(p) for p in ps], *[_in_hbm(z) for z in lands])
    return out[0], out[1], out[2:2 + n], out[2 + n:2 + 2 * n], out[-1]


def _send_wait(s_sem, r_sem, ps, lands, after, layer):
    n = len(ps)

    def body(*refs):
        sends, recvs = _direct_grad_copies(refs[:n], refs[n:2 * n], refs[2 * n], refs[2 * n + 1])
        for cp in sends:
            cp.wait_send()
        for cp in recvs:
            cp.wait_recv()

    hbm = [pltpu.HBM(p.shape, p.dtype) for p in ps] + [pltpu.HBM(z.shape, z.dtype) for z in lands]
    out = _pc(body, name=f"grad_send_wait_l{layer}", out_shape=tuple(hbm),
              in_specs=[HBM_SPEC] * (2 * n) + [SEM_SPEC, SEM_SPEC, ANY_SPEC], out_specs=tuple([HBM_SPEC] * (2 * n)),
              input_output_aliases={i: i for i in range(2 * n)},
              compiler_params=pltpu.CompilerParams(has_side_effects=SPLIT_EFFECT))(*ps, *lands, s_sem, r_sem, after)
    return out[:n], out[n:]


def _sum_direct(g, landed, me_arr, c_arr, buf, first_layer, name):
    _, nl, r2, cols = landed.shape

    def body(me_ref, c_ref, g_ref, l_ref, b_ref, o_ref):
        tot = g_ref[...].astype(F32)
        for s in range(7):
            tot = tot + l_ref[s].astype(F32)
        o_ref[...] = tot

    gs = pltpu.PrefetchScalarGridSpec(
        num_scalar_prefetch=2, grid=(nl,),
        in_specs=[pl.BlockSpec((None, None, r2, cols), lambda l, me, c: (me[0], l, c[0], 0)),
                  pl.BlockSpec((7, None, r2, cols), lambda l, me, c: (0, l, 0, 0)), ANY_SPEC],
        out_specs=pl.BlockSpec((None, r2, cols), lambda l, me, c: (first_layer + l, c[0], 0)))
    return _pc(body, name=name, grid_spec=gs, out_shape=S(buf.shape, F32), input_output_aliases={4: 0},
               compiler_params=_cp(1))(me_arr, c_arr, g, landed, buf)


def _share_halves(bufs):
    n = len(bufs)

    def body(*refs):
        ins, outs = refs[:n], refs[n:2 * n]
        s_sem, r_sem = refs[2 * n:]
        x, y, c, _ = _place()
        cps = []
        for t in range(n):
            r2 = bufs[t].shape[1] // 2
            cps.append(pltpu.make_async_remote_copy(
                src_ref=ins[t].at[:, pl.ds(c * r2, r2), :], dst_ref=outs[t].at[:, pl.ds(c * r2, r2), :],
                send_sem=s_sem.at[t], recv_sem=r_sem.at[t], device_id=(x, y, 1 - c), device_id_type=MESH))
            cps[-1].start()
        for cp in cps:
            cp.wait_recv()
        for cp in cps:
            cp.wait_send()

    return _pc(body, name="grad_share_halves", in_specs=[HBM_SPEC] * n, out_specs=[HBM_SPEC] * n,
               out_shape=[S(b.shape, b.dtype) for b in bufs], input_output_aliases={t: t for t in range(n)},
               scratch_shapes=[pltpu.SemaphoreType.DMA((n,))] * 2, compiler_params=_comm_params())(*bufs)


def _gather_small(v):
    nr = v.shape[0]

    def body(v_ref, out_ref, send_sems, recv_sems, local_sem):
        x, y, c, chips = _place()
        me, sibling = (x, y, c), (x, y, 1 - c)

        def rows(px, py, pc):
            return out_ref.at[pl.ds((4 * px + 2 * py + pc) * nr, nr), :]

        def copy(k, block, to, src=None):
            return pltpu.make_async_remote_copy(src_ref=rows(*block) if src is None else src, dst_ref=rows(*block),
                                                send_sem=send_sems.at[k], recv_sem=recv_sems.at[k], device_id=to,
                                                device_id_type=MESH)

        mine = pltpu.make_async_copy(v_ref, rows(*me), local_sem)
        mine.start()
        first = [copy(0, me, sibling, src=v_ref)]
        first += [copy(1 + j, me, (*chip, c), src=v_ref) for j, chip in enumerate(chips)]
        for cp in first:
            cp.start()
        passed = [copy(4 + j, (*chip, c), sibling) for j, chip in enumerate(chips)]
        for j, chip in enumerate(chips):
            copy(1 + j, (*chip, c), me).wait_recv()
            passed[j].start()
        copy(0, sibling, me).wait_recv()
        for j, chip in enumerate(chips):
            copy(4 + j, (*chip, 1 - c), me).wait_recv()
        for cp in first + passed:
            cp.wait_send()
        mine.wait()

    vm = pl.BlockSpec(memory_space=pltpu.VMEM)
    return _pc(body, name="gather_small_grads", in_specs=[vm], out_specs=vm, out_shape=S((8 * nr, 128), F32),
               scratch_shapes=[pltpu.SemaphoreType.DMA((7,)), pltpu.SemaphoreType.DMA((7,)), pltpu.SemaphoreType.DMA],
               compiler_params=_comm_params())(v)


def _adamw_math(w, g, m, v):
    m2 = ADAM_B1 * m + (1.0 - ADAM_B1) * g
    v2 = ADAM_B2 * v + (1.0 - ADAM_B2) * (g * g)
    m_hat = m2 / (1.0 - ADAM_B1 ** ADAM_STEP)
    v_hat = v2 / (1.0 - ADAM_B2 ** ADAM_STEP)
    return -ADAM_LR * (m_hat / (jnp.sqrt(v_hat) + ADAM_EPS) + ADAM_WD * w), m2, v2


def _adamw(w, g, m, v, name):
    L, R, C = w.shape
    rb = R // 2 if R >= 512 else R

    def body(w_ref, g_ref, m_ref, v_ref, d_ref, m2_ref, v2_ref):
        d_ref[...], m2_ref[...], v2_ref[...] = _adamw_math(w_ref[...], g_ref[...], m_ref[...], v_ref[...])

    blk = pl.BlockSpec((None, rb, C), lambda l, r: (l, r, 0))
    return _pc(body, name=name, grid=(L, R // rb), in_specs=[blk] * 4, out_specs=[blk] * 3,
               out_shape=[S(w.shape, F32)] * 3, compiler_params=_cp(2))(w, g, m, v)


def _adamw_small(gathered, w, m, v):
    nr = w.shape[0]
    rb = nr // 5

    def body(a_ref, w_ref, m_ref, v_ref, g_ref, d_ref, m2_ref, v2_ref):
        g = a_ref[0]
        for k in range(1, 8):
            g = g + a_ref[k]
        g_ref[...] = g
        d_ref[...], m2_ref[...], v2_ref[...] = _adamw_math(w_ref[...], g, m_ref[...], v_ref[...])

    blk = pl.BlockSpec((rb, 128), lambda i: (i, 0))
    return _pc(body, name="adamw_small", grid=(nr // rb,), in_specs=[pl.BlockSpec((8, rb, 128), lambda i: (0, i, 0))] + [blk] * 3,
               out_specs=[blk] * 4, out_shape=[S((nr, 128), F32)] * 4, compiler_params=_cp(1))(gathered, w, m, v)


SMALL_ROWS = 4520


def _pack(arrs):
    flat = jnp.concatenate([a.reshape(-1) for a in arrs])
    return jnp.pad(flat, (0, SMALL_ROWS * 128 - flat.shape[0])).reshape(SMALL_ROWS, 128)


def _unpack(packed, like):
    flat = packed.reshape(-1)
    out, off = [], 0
    for a in like:
        out.append(flat[off:off + a.size].reshape(a.shape))
        off += a.size
    return out


def kernel(x, p, positions, ffn1_pre_g, ffn1_w_gate, ffn1_w_up, ffn1_w_down, ffn1_post_g, mix_pre_g, w_in, attn_norm_g, ssm_lam_re, ssm_lam_im, ssm_log_dt, ssm_b_re, ssm_b_im, ssm_c_re, ssm_c_im, ssm_d, ssm_w_glu, ssm_b_glu, ssm_norm_g, w_out, mix_post_g, ffn2_pre_g, ffn2_w_gate, ffn2_w_up, ffn2_w_down, ffn2_post_g, ple_w_up, ple_w_gate, ple_post_g, loss_target, m_ffn1_pre_g, m_ffn1_w_gate, m_ffn1_w_up, m_ffn1_w_down, m_ffn1_post_g, m_mix_pre_g, m_w_in, m_attn_norm_g, m_ssm_lam_re, m_ssm_lam_im, m_ssm_log_dt, m_ssm_b_re, m_ssm_b_im, m_ssm_c_re, m_ssm_c_im, m_ssm_d, m_ssm_w_glu, m_ssm_b_glu, m_ssm_norm_g, m_w_out, m_mix_post_g, m_ffn2_pre_g, m_ffn2_w_gate, m_ffn2_w_up, m_ffn2_w_down, m_ffn2_post_g, m_ple_w_up, m_ple_w_gate, m_ple_post_g, v_ffn1_pre_g, v_ffn1_w_gate, v_ffn1_w_up, v_ffn1_w_down, v_ffn1_post_g, v_mix_pre_g, v_w_in, v_attn_norm_g, v_ssm_lam_re, v_ssm_lam_im, v_ssm_log_dt, v_ssm_b_re, v_ssm_b_im, v_ssm_c_re, v_ssm_c_im, v_ssm_d, v_ssm_w_glu, v_ssm_b_glu, v_ssm_norm_g, v_w_out, v_mix_post_g, v_ffn2_pre_g, v_ffn2_w_gate, v_ffn2_w_up, v_ffn2_w_down, v_ffn2_post_g, v_ple_w_up, v_ple_w_gate, v_ple_post_g):
    a = dict(locals())
    T = x.shape[1]
    big_names = [n for n, _, _ in BIG]
    for n in TRANSPOSED:
        for pre in ("", "m_", "v_"):
            a[pre + n] = jnp.swapaxes(a[pre + n], 1, 2)

    own = [a[n].astype(BF16) for n in big_names]
    n_layers = own[0].shape[0]
    per_layer = [[w[l:l + 1] for w in own] for l in range(n_layers)]
    c_arr = lax.axis_index("c").astype(jnp.int32).reshape(1)
    me_arr = (2 * lax.axis_index("x") + lax.axis_index("y")).astype(jnp.int32).reshape(1)
    first = dict(zip(big_names, _gather_weights(per_layer[0], _place_own(per_layer[0], me_arr, 0))))
    pending, anchor, queued_behind = {}, jnp.zeros((), F32), first[big_names[0]]
    for l in range(1, n_layers):
        s_sem, r_sem, ws_thru, lands_thru, token = _gather_start(per_layer[l], _place_own(per_layer[l], me_arr, l),
                                                                 queued_behind, l)
        pending[l] = (s_sem, r_sem, ws_thru, lands_thru)
        anchor = anchor + token[0, 0]
        queued_behind = token

    def weights_of(l, after):
        if l == 0:
            return first
        return dict(zip(big_names, _gather_wait(*pending[l], after, l)))

    Sm = {n: a[n] for n in SMALL}
    Sm["ffn1_pre_g"] = Sm["ffn1_pre_g"] + anchor

    pos = jnp.broadcast_to(positions.reshape(1, T, 1).astype(F32), (1, T, 128))
    sent = {}

    def layer_grads_done(l, G):
        gs = [G[n] for n in big_names]
        lands = [lax.empty((7, 1, g.shape[2] // 2, g.shape[3]), BF16) for g in gs]
        s_sem, r_sem, gs_thru, lands_thru, token = _send_start(gs, lands, l)
        sent[l] = (s_sem, r_sem, gs_thru, lands_thru)
        return token[0, 0]

    loss, gx, small = _local_step(
        _reorder(x, True, "to_streams_x")[0], _reorder(p[:, 0], True, "to_streams_p"),
        _reorder(pos, True, "to_streams_pos")[0, :, :1], _reorder(loss_target, True, "to_streams_target")[0],
        weights_of, layer_grads_done, Sm)
    gx = _reorder(gx[None], False, "to_time_grad_x")

    small_g = _gather_small(_pack([small[n] for n in SMALL])).reshape(8, SMALL_ROWS, 128)
    sg, sd, sm, sv = _adamw_small(small_g, _pack([a[n] for n in SMALL]), _pack([a["m_" + n] for n in SMALL]),
                                  _pack([a["v_" + n] for n in SMALL]))

    bufs = [lax.empty((n_layers, r, c), F32) for _, r, c in BIG]
    for l in sorted(sent, reverse=True):
        gs, landed = _send_wait(*sent[l], gx if l else (bufs[0] if n_layers > 1 else sd), l)
        bufs = [_sum_direct(g, la, me_arr, c_arr, b, l, f"grad_sum_direct_l{l}_{n}")
                for g, la, b, n in zip(gs, landed, bufs, big_names)]
    grads = dict(zip(big_names, _share_halves(bufs)))
    like = [a[n] for n in SMALL]
    res = {}
    for n, g_, d_, m_, v_ in zip(SMALL, _unpack(sg, like), _unpack(sd, like), _unpack(sm, like), _unpack(sv, like)):
        res[n] = (g_, d_, m_, v_)
    for n in big_names:
        d_, m_, v_ = _adamw(a[n], grads[n], a["m_" + n], a["v_" + n], f"adamw_{n}")
        res[n] = (grads[n], d_, m_, v_)
        if n in TRANSPOSED:
            res[n] = tuple(jnp.swapaxes(t, 1, 2) for t in res[n])

    total = lax.psum(loss[0, 0], ("x", "y", "c"))
    return (total, gx, *[res[n][0] for n in WEIGHTS], *[res[n][1] for n in WEIGHTS],
            *[res[n][2] for n in WEIGHTS], *[res[n][3] for n in WEIGHTS])
```

```python
import functools
import math

import numpy as np
import jax
import jax.numpy as jnp
from jax import lax
from jax.experimental import pallas as pl
from jax.experimental.pallas import tpu as pltpu

F32 = jnp.float32
BF16 = jnp.bfloat16
S = jax.ShapeDtypeStruct
MESH = pl.DeviceIdType.MESH

D = 1024
DA = 512
DSS = 512
HD = 64
NH = 8
BAND = 128
NSH = 4
DFS = 704
PLE = 256
EPS = 1e-6
ROPE_THETA = 500000.0
PATTERN_DILATIONS = (1, 4, 16)
NLB = 16
ADAM_LR, ADAM_B1, ADAM_B2, ADAM_EPS, ADAM_WD, ADAM_STEP = 0.001, 0.9, 0.999, 1e-08, 0.01, 10

VMEM_LIMIT = 56 * 1024 * 1024
TM = 512
TMB = 256

BIG = (
    ("ffn1_w_gate", DFS, D), ("ffn1_w_up", DFS, D), ("ffn1_w_down", DFS, D),
    ("w_in", D, 512), ("ssm_w_glu", 128, 512), ("w_out", 256, D),
    ("ffn2_w_gate", DFS, D), ("ffn2_w_up", DFS, D), ("ffn2_w_down", DFS, D),
    ("ple_w_up", PLE, 256), ("ple_w_gate", 256, D),
)
TRANSPOSED = ("ffn1_w_gate", "ffn1_w_up", "ffn2_w_gate", "ffn2_w_up")
SMALL = ("ffn1_pre_g", "ffn1_post_g", "mix_pre_g", "attn_norm_g", "ssm_lam_re", "ssm_lam_im", "ssm_log_dt",
         "ssm_b_re", "ssm_b_im", "ssm_c_re", "ssm_c_im", "ssm_d", "ssm_b_glu", "ssm_norm_g", "mix_post_g",
         "ffn2_pre_g", "ffn2_post_g", "ple_post_g")
WEIGHTS = ("ffn1_pre_g", "ffn1_w_gate", "ffn1_w_up", "ffn1_w_down", "ffn1_post_g", "mix_pre_g", "w_in", "attn_norm_g",
           "ssm_lam_re", "ssm_lam_im", "ssm_log_dt", "ssm_b_re", "ssm_b_im", "ssm_c_re", "ssm_c_im", "ssm_d",
           "ssm_w_glu", "ssm_b_glu", "ssm_norm_g", "w_out", "mix_post_g", "ffn2_pre_g", "ffn2_w_gate", "ffn2_w_up",
           "ffn2_w_down", "ffn2_post_g", "ple_w_up", "ple_w_gate", "ple_post_g")


def _pc(body, **kw):
    return pl.pallas_call(body, **kw)


def _cp(n_grid):
    return pltpu.CompilerParams(dimension_semantics=("arbitrary",) * n_grid, vmem_limit_bytes=VMEM_LIMIT)


def _dot(a, b):
    return jnp.dot(a, b, preferred_element_type=F32)


def _dot_nt(a, b):
    return lax.dot_general(a, b, (((1,), (1,)), ((), ())), preferred_element_type=F32)


def _dot_tn(a, b):
    return lax.dot_general(a, b, (((0,), (0,)), ((), ())), preferred_element_type=F32)


def _split(a):
    hi = a.astype(BF16)
    return hi, (a - hi.astype(F32)).astype(BF16)


def _dot3(fn, a, b):
    ah, al = _split(a)
    bh, bl = _split(b)
    return fn(ah, bh) + fn(ah, bl) + fn(al, bh)


def _rms_fwd(x, g):
    r = lax.rsqrt(jnp.mean(x * x, axis=-1, keepdims=True) + EPS)
    return x * r * g


def _rms_bwd(dy, x, g):
    r = lax.rsqrt(jnp.mean(x * x, axis=-1, keepdims=True) + EPS)
    xr = x * r
    gd = dy * g
    dx = r * (gd - xr * jnp.mean(gd * xr, axis=-1, keepdims=True))
    dg = jnp.sum(dy * xr, axis=0, keepdims=True)
    return dx, dg


def _gelu(y):
    k = math.sqrt(2.0 / math.pi)
    return 0.5 * y * (1.0 + jnp.tanh(k * (y + 0.044715 * y * y * y)))


def _gelu_grad(y):
    k = math.sqrt(2.0 / math.pi)
    t = jnp.tanh(k * (y + 0.044715 * y * y * y))
    return 0.5 * (1.0 + t) + 0.5 * y * (1.0 - t * t) * k * (1.0 + 3 * 0.044715 * y * y)


def _gain_spec(n, layer):
    return pl.BlockSpec((None, 1, n), lambda *_: (layer, 0, 0))


def _row_acc_spec(n):
    return pl.BlockSpec((1, n), lambda *_: (0, 0))


def _rot_tables(pos_col):
    T = pos_col.shape[0]
    half = HD // 8
    inv = (ROPE_THETA ** (-np.arange(half, dtype=np.float32) * (2.0 / (2 * half)))).astype(np.float32)
    lane_freq = np.tile(np.concatenate([inv, inv, np.zeros(HD - 2 * half, np.float32)]), NH)[None, :]

    def body(p_ref, f_ref, c_ref, s1_ref, s2_ref):
        ang = p_ref[...] * f_ref[...]
        d = lax.broadcasted_iota(jnp.int32, ang.shape, 1) % HD
        cs = jnp.cos(ang)
        sn = jnp.sin(ang)
        c_ref[...] = jnp.where(d < 2 * half, cs, 1.0)
        s1_ref[...] = jnp.where(d < half, -sn, 0.0)
        s2_ref[...] = jnp.where((d >= half) & (d < 2 * half), sn, 0.0)

    tm = TM
    return _pc(body, name="rot_tables", grid=(T // tm,),
               in_specs=[pl.BlockSpec((tm, 1), lambda i: (i, 0)), pl.BlockSpec((1, DA), lambda i: (0, 0))],
               out_specs=[pl.BlockSpec((tm, DA), lambda i: (i, 0))] * 3,
               out_shape=[S((T, DA), F32)] * 3, compiler_params=_cp(1))(pos_col, jnp.asarray(lane_freq))


def _rot_fwd(t, c, s1, s2):
    return t * c + pltpu.roll(t, DA - 8, 1) * s1 + pltpu.roll(t, 8, 1) * s2


def _rot_bwd(g, c, s1, s2):
    return g * c + pltpu.roll(g * s1, 8, 1) + pltpu.roll(g * s2, DA - 8, 1)


def _ffn_weight_spec():
    return pl.BlockSpec((NSH, None, DFS, D), lambda i: (0, 0, 0, 0), pipeline_mode=pl.Buffered(1))


def _ffn_fwd(h, pre_g, post_g, wg, wu, wd, layer, tag):
    T = h.shape[0]
    tm = TM
    nt = T // tm

    def body(h_ref, pg_ref, qg_ref, wg_ref, wu_ref, wd_ref, ho_ref, a_ref, b_ref, f_ref, xn_ref):
        hv = h_ref[...]
        xb = _rms_fwd(hv, pg_ref[...]).astype(BF16)
        xn_ref[...] = xb
        f = None
        for j in range(NSH):
            ab = _dot_nt(xb, wg_ref[j]).astype(BF16)
            bb = _dot_nt(xb, wu_ref[j]).astype(BF16)
            a_ref[j] = ab
            b_ref[j] = bb
            a = ab.astype(F32)
            hh = (a * jax.nn.sigmoid(a) * bb.astype(F32)).astype(BF16)
            part = _dot(hh, wd_ref[j])
            f = part if f is None else f + part
        f_ref[...] = f
        ho_ref[...] = hv + 0.5 * _rms_fwd(f, qg_ref[...])

    row = pl.BlockSpec((tm, D), lambda i: (i, 0))
    act = pl.BlockSpec((NSH, tm, DFS), lambda i: (0, i, 0))
    return _pc(body, name=f"ffn_fwd_{tag}_l{layer}", grid=(nt,),
               in_specs=[row, _gain_spec(D, layer), _gain_spec(D, layer)] + [_ffn_weight_spec()] * 3,
               out_specs=[row, act, act, row, row],
               out_shape=[S((T, D), F32), S((NSH, T, DFS), BF16), S((NSH, T, DFS), BF16), S((T, D), F32), S((T, D), BF16)],
               compiler_params=_cp(1))(h, pre_g, post_g, wg, wu, wd)


def _ffn_bwd(dout, h, f, a, b, pre_g, post_g, wg, wu, wd, layer, tag):
    T = h.shape[0]
    tm = TMB
    nt = T // tm

    def body(do_ref, h_ref, f_ref, a_ref, b_ref, pg_ref, qg_ref, wg_ref, wu_ref, wd_ref,
             dh_ref, df_ref, da_ref, db_ref, hh_ref, dpg_ref, dqg_ref):
        @pl.when(pl.program_id(0) == 0)
        def _():
            dpg_ref[...] = jnp.zeros_like(dpg_ref)
            dqg_ref[...] = jnp.zeros_like(dqg_ref)

        do = do_ref[...]
        df, dq = _rms_bwd(0.5 * do, f_ref[...], qg_ref[...])
        dqg_ref[...] += dq
        dfb = df.astype(BF16)
        df_ref[...] = dfb
        dxn = None
        for j in range(NSH):
            dhh = _dot_nt(dfb, wd_ref[j])
            av = a_ref[j].astype(F32)
            bv = b_ref[j].astype(F32)
            sg = jax.nn.sigmoid(av)
            sa = av * sg
            hh_ref[j] = (sa * bv).astype(BF16)
            dab = (dhh * bv * (sg + sa * (1.0 - sg))).astype(BF16)
            dbb = (dhh * sa).astype(BF16)
            da_ref[j] = dab
            db_ref[j] = dbb
            part = _dot(dab, wg_ref[j]) + _dot(dbb, wu_ref[j])
            dxn = part if dxn is None else dxn + part
        dx, dp = _rms_bwd(dxn, h_ref[...], pg_ref[...])
        dpg_ref[...] += dp
        dh_ref[...] = do + dx

    row = pl.BlockSpec((tm, D), lambda i: (i, 0))
    act = pl.BlockSpec((NSH, tm, DFS), lambda i: (0, i, 0))
    return _pc(body, name=f"ffn_bwd_{tag}_l{layer}", grid=(nt,),
               in_specs=[row, row, row, act, act, _gain_spec(D, layer), _gain_spec(D, layer)] + [_ffn_weight_spec()] * 3,
               out_specs=[row, row, act, act, act, _row_acc_spec(D), _row_acc_spec(D)],
               out_shape=[S((T, D), F32), S((T, D), BF16), S((NSH, T, DFS), BF16), S((NSH, T, DFS), BF16),
                          S((NSH, T, DFS), BF16), S((1, D), F32), S((1, D), F32)],
               compiler_params=_cp(1))(dout, h, f, a, b, pre_g, post_g, wg, wu, wd)


def _dw(A, B, buf, layer, kb, nb, a_mode, b_mode, name):
    T = A.shape[1]
    tt = TM
    nt = T // tt

    def pick(v, mode, j, w):
        if mode == "shard":
            return v[j]
        return v[0] if mode == "whole" else v[0][:, j * w:(j + 1) * w]

    def body(a_ref, b_ref, buf_ref, o_ref, acc):
        t = pl.program_id(0)

        @pl.when(t == 0)
        def _():
            acc[...] = jnp.zeros_like(acc)

        av = a_ref[...].astype(BF16)
        bv = b_ref[...].astype(BF16)
        for j in range(NSH):
            acc[j] += _dot_tn(pick(av, a_mode, j, kb), pick(bv, b_mode, j, nb))

        @pl.when(t == nt - 1)
        def _():
            o_ref[...] = acc[...].astype(o_ref.dtype)

    return _pc(body, name=name, grid=(nt,),
               in_specs=[pl.BlockSpec((A.shape[0], tt, A.shape[2]), lambda t: (0, t, 0)),
                         pl.BlockSpec((B.shape[0], tt, B.shape[2]), lambda t: (0, t, 0)),
                         pl.BlockSpec(memory_space=pl.ANY)],
               out_specs=pl.BlockSpec((NSH, None, kb, nb), lambda t: (0, layer, 0, 0)),
               out_shape=S(buf.shape, buf.dtype), input_output_aliases={2: 0},
               scratch_shapes=[pltpu.VMEM((NSH, kb, nb), F32)], compiler_params=_cp(1))(A, B, buf)


def _mix_proj(h, pre_g, win, rot, layer):
    T = h.shape[0]
    tm = TM

    def body(h_ref, g_ref, w_ref, c_ref, s1_ref, s2_ref, p_ref, xn_ref):
        xb = _rms_fwd(h_ref[...], g_ref[...]).astype(BF16)
        xn_ref[...] = xb
        for j in range(NSH):
            o = _dot(xb, w_ref[j])
            p_ref[j] = _rot_fwd(o, c_ref[...], s1_ref[...], s2_ref[...]) if j < 2 else o

    row = pl.BlockSpec((tm, D), lambda i: (i, 0))
    half = pl.BlockSpec((tm, DA), lambda i: (i, 0))
    return _pc(body, name=f"mix_proj_l{layer}", grid=(T // tm,),
               in_specs=[row, _gain_spec(D, layer), pl.BlockSpec((NSH, None, D, DA), lambda i: (0, 0, 0, 0)),
                         half, half, half],
               out_specs=[pl.BlockSpec((NSH, tm, DA), lambda i: (0, i, 0)), row],
               out_shape=[S((NSH, T, DA), F32), S((T, D), BF16)], compiler_params=_cp(1))(h, pre_g, win, *rot)


def _mix_proj_bwd(dq, dk, dv, du, dh_up, h, pre_g, win, rot, layer):
    T = h.shape[0]
    tm = TM

    def body(dq_ref, dk_ref, dv_ref, du_ref, up_ref, h_ref, g_ref, w_ref, c_ref, s1_ref, s2_ref,
             dh_ref, dp_ref, dg_ref):
        @pl.when(pl.program_id(0) == 0)
        def _():
            dg_ref[...] = jnp.zeros_like(dg_ref)

        rot = (c_ref[...], s1_ref[...], s2_ref[...])
        dps = [_rot_bwd(dq_ref[...], *rot), _rot_bwd(dk_ref[...], *rot), dv_ref[...], du_ref[...]]
        dxn = None
        for j in range(NSH):
            dpb = dps[j].astype(BF16)
            dp_ref[j] = dpb
            part = _dot_nt(dpb, w_ref[j])
            dxn = part if dxn is None else dxn + part
        dx, dg = _rms_bwd(dxn, h_ref[...], g_ref[...])
        dg_ref[...] += dg
        dh_ref[...] = up_ref[...] + dx

    row = pl.BlockSpec((tm, D), lambda i: (i, 0))
    half = pl.BlockSpec((tm, DA), lambda i: (i, 0))
    return _pc(body, name=f"mix_proj_bwd_l{layer}", grid=(T // tm,),
               in_specs=[half, half, half, half, row, row, _gain_spec(D, layer),
                         pl.BlockSpec((NSH, None, D, DA), lambda i: (0, 0, 0, 0)), half, half, half],
               out_specs=[row, pl.BlockSpec((NSH, tm, DA), lambda i: (0, i, 0)), _row_acc_spec(D)],
               out_shape=[S((T, D), F32), S((NSH, T, DA), BF16), S((1, D), F32)],
               compiler_params=_cp(1))(dq, dk, dv, du, dh_up, h, pre_g, win, *rot)


def _stream_pos(d, axis):
    i = lax.broadcasted_iota(jnp.int32, (BAND, BAND), axis)
    if d == 16:
        return i
    if d == 4:
        return 4 * (i % 32) + i // 32
    return 16 * (i % 8) + i // 8


def _band_masks(b, d):
    qi, kj = _stream_pos(d, 0), _stream_pos(d, 1)
    return kj <= qi, (kj >= qi) & (b > 0)


def _pattern(d, T):
    n16 = T // 16
    if d == 16:
        return (16, n16, DA), (None, BAND, DA), lambda r, k: (r, k, 0)
    if d == 4:
        return (4, 4, n16, DA), (4, None, 32, DA), lambda r, k: (0, r, k, 0)
    return (16, n16, DA), (16, 8, DA), lambda r, k: (0, k, 0)


def _pattern_spec(d, T, kmap, lead=None):
    _, blk, idx = _pattern(d, T)
    if lead is None:
        return pl.BlockSpec(blk, lambda r, b: idx(r, kmap(b)))
    return pl.BlockSpec((None,) + blk, lambda r, b: (lead,) + idx(r, kmap(b)))


def _whole_stream_specs(T, n_plain):
    n16 = T // 16
    p_spec = lambda s: pl.BlockSpec((None, None, n16, DA), lambda r: (s, r, 0, 0))
    plain = pl.BlockSpec((None, n16, DA), lambda r: (r, 0, 0))
    return [p_spec(0), p_spec(1), p_spec(2)] + [plain] * n_plain, plain


def _stream_masks():
    qi = lax.broadcasted_iota(jnp.int32, (BAND, BAND), 0)
    kj = lax.broadcasted_iota(jnp.int32, (BAND, BAND), 1)
    mask_c = kj <= qi
    return mask_c, jnp.concatenate([kj >= qi, mask_c], axis=1)


def _attn_fwd_stream(P, layer):
    T = P.shape[1]
    n16 = T // 16
    nb = n16 // BAND
    scale = HD ** -0.5

    def body(q_ref, k_ref, v_ref, o_ref, l_ref, qs, ks, vs):
        for src, dst in ((q_ref, qs), (k_ref, ks), (v_ref, vs)):
            dst[...] = src[...].astype(BF16)
        mask_c, mask_pc = _stream_masks()
        for b in range(nb):
            rows = slice(b * BAND, (b + 1) * BAND)
            krows = slice(max(b - 1, 0) * BAND, (b + 1) * BAND)
            mask = mask_c if b == 0 else mask_pc
            for hd in range(NH):
                sl = slice(hd * HD, (hd + 1) * HD)
                s = jnp.where(mask, _dot_nt(qs[rows, sl], ks[krows, sl]) * scale, -1e30)
                m = jnp.max(s, axis=-1, keepdims=True)
                e = jnp.exp(s - m)
                den = jnp.sum(e, axis=-1, keepdims=True)
                o_ref[rows, sl] = _dot(e.astype(BF16), vs[krows, sl]) / den
                l_ref[rows, sl] = jnp.broadcast_to(m + jnp.log(den), (BAND, HD))

    ins, out = _whole_stream_specs(T, 0)
    Pv = P.reshape(NSH, 16, n16, DA)
    o, l = _pc(body, name=f"attn_fwd_d16_l{layer}", grid=(16,), in_specs=ins, out_specs=[out, out],
               out_shape=[S((16, n16, DA), F32)] * 2, scratch_shapes=[pltpu.VMEM((n16, DA), BF16)] * 3,
               compiler_params=_cp(1))(Pv, Pv, Pv)
    return o.reshape(T, DA), l.reshape(T, DA)


def _attn_bwd_stream(P, dO, lse, delta, acc, layer):
    T = P.shape[1]
    n16 = T // 16
    nb = n16 // BAND
    scale = HD ** -0.5
    first = acc is None

    def body(*refs):
        q_ref, k_ref, v_ref, do_ref, l_ref, dl_ref = refs[:6]
        if first:
            dq_ref, dk_ref, dv_ref = refs[6:9]
        else:
            aq_ref, ak_ref, av_ref, dq_ref, dk_ref, dv_ref = refs[6:12]
        qs, ks, vs, dos, okf, ovf = refs[-6:]
        for src, dst in ((q_ref, qs), (k_ref, ks), (v_ref, vs), (do_ref, dos)):
            dst[...] = src[...].astype(BF16)
        okf[...] = jnp.zeros_like(okf)
        ovf[...] = jnp.zeros_like(ovf)
        mask_c, mask_pc = _stream_masks()
        for b in range(nb):
            rows = slice(b * BAND, (b + 1) * BAND)
            krows = slice(max(b - 1, 0) * BAND, (b + 1) * BAND)
            mask = mask_c if b == 0 else mask_pc
            for hd in range(NH):
                sl = slice(hd * HD, (hd + 1) * HD)
                one = slice(hd * HD, hd * HD + 1)
                q, do, kk = qs[rows, sl], dos[rows, sl], ks[krows, sl]
                p = jnp.where(mask, jnp.exp(_dot_nt(q, kk) * scale - l_ref[rows, one]), 0.0)
                ds = (p * (_dot_nt(do, vs[krows, sl]) - dl_ref[rows, one]) * scale).astype(BF16)
                dq = _dot(ds, kk)
                dq_ref[rows, sl] = dq if first else aq_ref[rows, sl] + dq
                okf[krows, sl] += _dot_tn(ds, q)
                ovf[krows, sl] += _dot_tn(p.astype(BF16), do)
        dk_ref[...] = okf[...] if first else ak_ref[...] + okf[...]
        dv_ref[...] = ovf[...] if first else av_ref[...] + ovf[...]

    ins, out = _whole_stream_specs(T, 3 if first else 6)
    Pv = P.reshape(NSH, 16, n16, DA)
    view = lambda t: t.reshape(16, n16, DA)
    args = [Pv, Pv, Pv, view(dO), view(lse), view(delta)] + ([] if first else [view(t) for t in acc])
    dq, dk, dv = _pc(body, name=f"attn_bwd_d16_l{layer}", grid=(16,), in_specs=ins, out_specs=[out, out, out],
                     out_shape=[S((16, n16, DA), F32)] * 3,
                     scratch_shapes=[pltpu.VMEM((n16, DA), BF16)] * 4 + [pltpu.VMEM((n16, DA), F32)] * 2,
                     compiler_params=_cp(1))(*args)
    return dq.reshape(T, DA), dk.reshape(T, DA), dv.reshape(T, DA)


def _attn_fwd(P, d, layer):
    if d == 16:
        return _attn_fwd_stream(P, layer)
    T = P.shape[1]
    nb = T // d // BAND
    vshape = _pattern(d, T)[0]
    Pv = P.reshape((NSH,) + vshape)
    scale = HD ** -0.5

    def body(q_ref, kp_ref, kc_ref, vp_ref, vc_ref, o_ref, l_ref, qs, ks, vs, osc, lsc):
        b = pl.program_id(1)
        flat = lambda ref: ref[...].reshape(BAND, DA).astype(BF16)
        qs[...] = flat(q_ref)
        ks[0:BAND, :] = flat(kp_ref)
        ks[BAND:, :] = flat(kc_ref)
        vs[0:BAND, :] = flat(vp_ref)
        vs[BAND:, :] = flat(vc_ref)
        mask_c, mask_p = _band_masks(b, d)
        mask = jnp.concatenate([mask_p, mask_c], axis=1)
        for hd in range(NH):
            sl = slice(hd * HD, (hd + 1) * HD)
            s = jnp.where(mask, _dot_nt(qs[:, sl], ks[:, sl]) * scale, -1e30)
            m = jnp.max(s, axis=-1, keepdims=True)
            e = jnp.exp(s - m)
            den = jnp.sum(e, axis=-1, keepdims=True)
            osc[:, sl] = _dot(e.astype(BF16), vs[:, sl]) / den
            lsc[:, sl] = jnp.broadcast_to(m + jnp.log(den), (BAND, HD))
        o_ref[...] = osc[...].reshape(o_ref.shape)
        l_ref[...] = lsc[...].reshape(l_ref.shape)

    cur = lambda b: b
    prev = lambda b: jnp.maximum(b - 1, 0)
    out = _pattern_spec(d, T, cur)
    o, l = _pc(body, name=f"attn_fwd_d{d}_l{layer}", grid=(d, nb),
               in_specs=[_pattern_spec(d, T, cur, 0), _pattern_spec(d, T, prev, 1), _pattern_spec(d, T, cur, 1),
                         _pattern_spec(d, T, prev, 2), _pattern_spec(d, T, cur, 2)],
               out_specs=[out, out], out_shape=[S(vshape, F32)] * 2,
               scratch_shapes=[pltpu.VMEM((BAND, DA), BF16)] + [pltpu.VMEM((2 * BAND, DA), BF16)] * 2
               + [pltpu.VMEM((BAND, DA), F32)] * 2,
               compiler_params=_cp(2))(Pv, Pv, Pv, Pv, Pv)
    return o.reshape(T, DA), l.reshape(T, DA)


def _attn_bwd(P, dO, lse, delta, acc, d, layer):
    if d == 16:
        return _attn_bwd_stream(P, dO, lse, delta, acc, layer)
    T = P.shape[1]
    nb = T // d // BAND
    vshape = _pattern(d, T)[0]
    Pv = P.reshape((NSH,) + vshape)
    scale = HD ** -0.5
    first = acc is None

    def body(*refs):
        q_ref, kp_ref, kc_ref, vp_ref, vc_ref, do_ref, l_ref, dl_ref = refs[:8]
        if first:
            dq_ref, dk_ref, dv_ref = refs[8:11]
        else:
            aq_ref, ak_ref, av_ref, dq_ref, dk_ref, dv_ref = refs[8:14]
        qs, dos, ks, vs, ls, dls, oq, ok, ov, ck, cv = refs[-11:]
        b = pl.program_id(1)
        flat = lambda ref: ref[...].reshape(BAND, DA)

        @pl.when(b == 0)
        def _():
            ck[...] = jnp.zeros_like(ck)
            cv[...] = jnp.zeros_like(cv)

        @pl.when(b < nb)
        def _():
            qs[...] = flat(q_ref).astype(BF16)
            dos[...] = flat(do_ref).astype(BF16)
            ks[0:BAND, :] = flat(kp_ref).astype(BF16)
            ks[BAND:, :] = flat(kc_ref).astype(BF16)
            vs[0:BAND, :] = flat(vp_ref).astype(BF16)
            vs[BAND:, :] = flat(vc_ref).astype(BF16)
            ls[...] = flat(l_ref)
            dls[...] = flat(dl_ref)
            mask_c, mask_p = _band_masks(b, d)
            mask = jnp.concatenate([mask_p, mask_c], axis=1)
            for hd in range(NH):
                sl = slice(hd * HD, (hd + 1) * HD)
                one = slice(hd * HD, hd * HD + 1)
                q, do, kk = qs[:, sl], dos[:, sl], ks[:, sl]
                p = jnp.where(mask, jnp.exp(_dot_nt(q, kk) * scale - ls[:, one]), 0.0)
                ds = (p * (_dot_nt(do, vs[:, sl]) - dls[:, one]) * scale).astype(BF16)
                oq[:, sl] = _dot(ds, kk)
                dk2 = _dot_tn(ds, q)
                dv2 = _dot_tn(p.astype(BF16), do)
                ok[:, sl] = ck[:, sl] + dk2[0:BAND]
                ov[:, sl] = cv[:, sl] + dv2[0:BAND]
                ck[:, sl] = dk2[BAND:]
                cv[:, sl] = dv2[BAND:]
            if first:
                dq_ref[...] = oq[...].reshape(dq_ref.shape)
                dk_ref[...] = ok[...].reshape(dk_ref.shape)
                dv_ref[...] = ov[...].reshape(dv_ref.shape)
            else:
                dq_ref[...] = aq_ref[...] + oq[...].reshape(dq_ref.shape)
                dk_ref[...] = ak_ref[...] + ok[...].reshape(dk_ref.shape)
                dv_ref[...] = av_ref[...] + ov[...].reshape(dv_ref.shape)

        @pl.when(b == nb)
        def _():
            if first:
                dk_ref[...] = ck[...].reshape(dk_ref.shape)
                dv_ref[...] = cv[...].reshape(dv_ref.shape)
            else:
                dk_ref[...] = ak_ref[...] + ck[...].reshape(dk_ref.shape)
                dv_ref[...] = av_ref[...] + cv[...].reshape(dv_ref.shape)

    qb = lambda b: jnp.minimum(b, nb - 1)
    qprev = lambda b: jnp.maximum(qb(b) - 1, 0)
    kb = lambda b: jnp.maximum(b - 1, 0)
    qrow = _pattern_spec(d, T, qb)
    krow = _pattern_spec(d, T, kb)
    view = lambda t: t.reshape(vshape)
    ins = [Pv, Pv, Pv, Pv, Pv, view(dO), view(lse), view(delta)]
    specs = [_pattern_spec(d, T, qb, 0), _pattern_spec(d, T, qprev, 1), _pattern_spec(d, T, qb, 1),
             _pattern_spec(d, T, qprev, 2), _pattern_spec(d, T, qb, 2), qrow, qrow, qrow]
    if not first:
        ins += [view(t) for t in acc]
        specs += [qrow, krow, krow]
    dq, dk, dv = _pc(body, name=f"attn_bwd_d{d}_l{layer}", grid=(d, nb + 1), in_specs=specs,
                     out_specs=[qrow, krow, krow], out_shape=[S(vshape, F32)] * 3,
                     scratch_shapes=[pltpu.VMEM((BAND, DA), BF16)] * 2 + [pltpu.VMEM((2 * BAND, DA), BF16)] * 2
                     + [pltpu.VMEM((BAND, DA), F32)] * 7,
                     compiler_params=_cp(2))(*ins)
    return dq.reshape(T, DA), dk.reshape(T, DA), dv.reshape(T, DA)


def _ssm_prep(lam_re, lam_im, log_dt, b_re, b_im, c_re, c_im):
    dt = jnp.exp(log_dt)[:, None]
    er = jnp.exp(lam_re * dt)
    a_re = er * jnp.cos(lam_im * dt)
    a_im = er * jnp.sin(lam_im * dt)
    nr, ni = a_re - 1.0, a_im
    den = lam_re * lam_re + lam_im * lam_im
    cr = (nr * lam_re + ni * lam_im) / den
    ci = (ni * lam_re - nr * lam_im) / den
    bbr = cr[..., None] * b_re - ci[..., None] * b_im
    bbi = cr[..., None] * b_im + ci[..., None] * b_re
    eye = jnp.eye(8, dtype=F32)

    def bblock(bb):
        t = bb.reshape(4, 8, 64, 16).transpose(0, 1, 3, 2)
        return (t[:, :, :, None, :] * eye[None, :, None, :, None]).reshape(4, 128, 512)

    def cblock(cc):
        t = cc.reshape(4, 8, 16, 64).transpose(0, 1, 3, 2)
        return (t[:, :, :, None, :] * eye[None, :, None, :, None]).reshape(4, 512, 128)

    return (a_re.reshape(NLB, 1, 128), a_im.reshape(NLB, 1, 128), bblock(bbr), bblock(bbi), cblock(c_re), cblock(c_im))


def _perm_matrix(tm):
    n = tm // 16
    pm = np.zeros((tm, tm), np.float32)
    for r in range(16):
        pm[16 * np.arange(n) + r, r * n + np.arange(n)] = 1.0
    return jnp.asarray(pm, BF16)


def _pieces(x):
    p1 = x.astype(BF16)
    r1 = x - p1.astype(F32)
    p2 = r1.astype(BF16)
    return p1, p2, (r1 - p2.astype(F32)).astype(BF16)


def _to_time(x, pm):
    return sum(_dot(pm, p) for p in _pieces(x))


def _to_streams(x, pm):
    return sum(_dot_tn(pm, p) for p in _pieces(x))


def _stream_block(tm, cols, lead=None):
    if lead is None:
        return pl.BlockSpec((16, tm // 16, cols), lambda i: (0, i, 0))
    return pl.BlockSpec((None, 16, tm // 16, cols), lambda i: (lead, 0, i, 0))


def _reorder(t3, to_streams, name):
    B, T, C = t3.shape
    tm = TM

    def body(x_ref, pm_ref, o_ref):
        if to_streams:
            o_ref[...] = _to_streams(x_ref[...], pm_ref[...]).reshape(o_ref.shape)
        else:
            o_ref[...] = _to_time(x_ref[...].reshape(tm, C), pm_ref[...])

    time_blk = pl.BlockSpec((None, tm, C), lambda b, i: (b, i, 0))
    stream_blk = pl.BlockSpec((None, 16, tm // 16, C), lambda b, i: (b, 0, i, 0))
    src = t3 if to_streams else t3.reshape(B, 16, T // 16, C)
    out = _pc(body, name=name, grid=(B, T // tm),
              in_specs=[time_blk if to_streams else stream_blk, pl.BlockSpec((tm, tm), lambda b, i: (0, 0))],
              out_specs=stream_blk if to_streams else time_blk,
              out_shape=S((B, 16, T // 16, C) if to_streams else (B, T, C), F32),
              compiler_params=_cp(2))(src, _perm_matrix(tm))
    return out.reshape(B, T, C)


def _ssm_in(P, bre, bim, layer):
    T = P.shape[1]
    tm = TM

    def body(u_ref, pm_ref, br_ref, bi_ref, un_ref, or_ref, oi_ref):
        u = _to_time(u_ref[...].reshape(tm, DSS), pm_ref[...])
        un_ref[...] = u
        for s in range(4):
            uc = u[:, s * 128:(s + 1) * 128]
            r = _dot3(_dot, uc, br_ref[s])
            m = _dot3(_dot, uc, bi_ref[s])
            for q in range(4):
                or_ref[4 * s + q] = r[:, q * 128:(q + 1) * 128]
                oi_ref[4 * s + q] = m[:, q * 128:(q + 1) * 128]

    whole = pl.BlockSpec((4, 128, 512), lambda i: (0, 0, 0))
    st = pl.BlockSpec((NLB, tm, 128), lambda i: (0, i, 0))
    return _pc(body, name=f"ssm_in_l{layer}", grid=(T // tm,),
               in_specs=[_stream_block(tm, DSS, 3), pl.BlockSpec((tm, tm), lambda i: (0, 0)), whole, whole],
               out_specs=[pl.BlockSpec((tm, DSS), lambda i: (i, 0)), st, st],
               out_shape=[S((T, DSS), F32)] + [S((NLB, T, 128), F32)] * 2,
               compiler_params=_cp(1))(P.reshape(NSH, 16, T // 16, DSS), _perm_matrix(tm), bre, bim)


def _scan(br, bi, a_re, a_im, reverse, layer):
    T = br.shape[1]
    nbk = 4
    tt = min(T, 1024)
    nT = T // tt
    ntile = tt // 8
    sgn = -1.0 if reverse else 1.0
    last = 0 if reverse else 7

    def body(br_ref, bi_ref, ar_ref, ai_ref, xr_ref, xi_ref, cr, ci):
        @pl.when(pl.program_id(1) == 0)
        def _():
            cr[...] = jnp.zeros_like(cr)
            ci[...] = jnp.zeros_like(ci)

        row = lax.broadcasted_iota(jnp.int32, (8, 128), 0)
        consts = []
        for k in range(nbk):
            a1r = jnp.broadcast_to(ar_ref[k], (8, 128))
            a1i = sgn * jnp.broadcast_to(ai_ref[k], (8, 128))
            pows = [(a1r, a1i)]
            for _ in range(7):
                pr, pi_ = pows[-1]
                pows.append((a1r * pr - a1i * pi_, a1r * pi_ + a1i * pr))
            rounds = []
            for s in (1, 2, 4):
                inside = (row <= 7 - s) if reverse else (row >= s)
                rounds.append((jnp.where(inside, pows[s - 1][0], 0.0), jnp.where(inside, pows[s - 1][1], 0.0)))
            cmr, cmi = jnp.zeros((8, 128), F32), jnp.zeros((8, 128), F32)
            for r in range(8):
                e = (7 - r) if reverse else r
                cmr = jnp.where(row == r, pows[e][0], cmr)
                cmi = jnp.where(row == r, pows[e][1], cmi)
            consts.append((rounds, cmr, cmi))

        def tile(i, carry):
            j = (ntile - 1 - i) if reverse else i
            rows = pl.ds(pl.multiple_of(j * 8, 8), 8)
            out = []
            for k in range(nbk):
                rounds, cmr, cmi = consts[k]
                xr = br_ref[k, rows, :]
                xi = bi_ref[k, rows, :]
                for (mr, mi), s in zip(rounds, (1, 2, 4)):
                    sh = (8 - s) if reverse else s
                    rr = pltpu.roll(xr, sh, 0)
                    ri = pltpu.roll(xi, sh, 0)
                    xr, xi = xr + (mr * rr - mi * ri), xi + (mr * ri + mi * rr)
                c_r, c_i = carry[k]
                xr, xi = xr + (cmr * c_r - cmi * c_i), xi + (cmr * c_i + cmi * c_r)
                xr_ref[k, rows, :] = xr
                xi_ref[k, rows, :] = xi
                out.append((jnp.broadcast_to(xr[last:last + 1, :], (8, 128)),
                            jnp.broadcast_to(xi[last:last + 1, :], (8, 128))))
            return tuple(out)

        carry = lax.fori_loop(0, ntile, tile, tuple((cr[k], ci[k]) for k in range(nbk)), unroll=2)
        for k in range(nbk):
            cr[k] = carry[k][0]
            ci[k] = carry[k][1]

    tmap = (lambda t: nT - 1 - t) if reverse else (lambda t: t)
    st = pl.BlockSpec((nbk, tt, 128), lambda i, t: (i, tmap(t), 0))
    av = pl.BlockSpec((nbk, 1, 128), lambda i, t: (i, 0, 0))
    return _pc(body, name=f"scan_{'bwd' if reverse else 'fwd'}_l{layer}", grid=(NLB // nbk, nT),
               in_specs=[st, st, av, av], out_specs=[st, st], out_shape=[S((NLB, T, 128), F32)] * 2,
               scratch_shapes=[pltpu.VMEM((nbk, 8, 128), F32)] * 2, compiler_params=_cp(2))(br, bi, a_re, a_im)


def _ssm_out(xr, xi, u, cre, cim, dvec, wglu, bglu, layer):
    T = u.shape[0]
    tm = TM

    def body(xr_ref, xi_ref, u_ref, pm_ref, cr_ref, ci_ref, d_ref, w_ref, bg_ref, s_ref, y_ref, z_ref):
        ys = []
        for s in range(4):
            xrc = jnp.concatenate([xr_ref[4 * s + q] for q in range(4)], axis=1)
            xic = jnp.concatenate([xi_ref[4 * s + q] for q in range(4)], axis=1)
            ys.append(_dot3(_dot, xrc, cr_ref[s]) - _dot3(_dot, xic, ci_ref[s]))
        y = jnp.concatenate(ys, axis=1) + d_ref[...] * u_ref[...]
        yg = _gelu(y)
        ygb = yg.astype(BF16)
        z = bg_ref[...] + sum(_dot(ygb[:, j * 128:(j + 1) * 128], w_ref[j]) for j in range(NSH))
        y_ref[...] = y
        z_ref[...] = z
        s_ref[...] = _to_streams(yg * jax.nn.sigmoid(z), pm_ref[...]).reshape(s_ref.shape)

    st = pl.BlockSpec((NLB, tm, 128), lambda i: (0, i, 0))
    cw = pl.BlockSpec((4, 512, 128), lambda i: (0, 0, 0))
    half = pl.BlockSpec((tm, DSS), lambda i: (i, 0))
    s, y, z = _pc(body, name=f"ssm_out_l{layer}", grid=(T // tm,),
                  in_specs=[st, st, half, pl.BlockSpec((tm, tm), lambda i: (0, 0)), cw, cw, _gain_spec(DSS, layer),
                            pl.BlockSpec((NSH, None, 128, DSS), lambda i: (0, 0, 0, 0)), _gain_spec(DSS, layer)],
                  out_specs=[_stream_block(tm, DSS), half, half],
                  out_shape=[S((16, T // 16, DSS), F32), S((T, DSS), F32), S((T, DSS), F32)],
                  compiler_params=_cp(1))(xr, xi, u, _perm_matrix(tm), cre, cim, dvec, wglu, bglu)
    return s.reshape(T, DSS), y, z


def _ssm_out_bwd(dssm, y, z, xr, xi, u, cre, cim, dvec, wglu, layer):
    T = u.shape[0]
    tm = TM

    def body(ds_ref, pm_ref, y_ref, z_ref, xr_ref, xi_ref, u_ref, cr_ref, ci_ref, d_ref, w_ref,
             gr_ref, gi_ref, du_ref, dz_ref, yg_ref, dbg_ref, dd_ref, dcr_ref, dci_ref):
        i = pl.program_id(0)

        @pl.when(i == 0)
        def _():
            dbg_ref[...] = jnp.zeros_like(dbg_ref)
            dd_ref[...] = jnp.zeros_like(dd_ref)
            dcr_ref[...] = jnp.zeros_like(dcr_ref)
            dci_ref[...] = jnp.zeros_like(dci_ref)

        yv = y_ref[...]
        yg = _gelu(yv)
        sg = jax.nn.sigmoid(z_ref[...])
        ds = _to_time(ds_ref[...].reshape(tm, DSS), pm_ref[...])
        dz = ds * yg * sg * (1.0 - sg)
        dzb = dz.astype(BF16)
        dz_ref[...] = dzb
        yg_ref[...] = yg.astype(BF16)
        dbg_ref[...] += jnp.sum(dz, axis=0, keepdims=True)
        dyg = ds * sg + jnp.concatenate([_dot_nt(dzb, w_ref[j]) for j in range(NSH)], axis=1)
        dy = dyg * _gelu_grad(yv)
        u = u_ref[...]
        dd_ref[...] += jnp.sum(dy * u, axis=0, keepdims=True)
        du_ref[...] = dy * d_ref[...]
        for s in range(4):
            dyc = dy[:, s * 128:(s + 1) * 128]
            g_r = _dot3(_dot_nt, dyc, cr_ref[s])
            g_i = -_dot3(_dot_nt, dyc, ci_ref[s])
            for q in range(4):
                gr_ref[4 * s + q] = g_r[:, q * 128:(q + 1) * 128]
                gi_ref[4 * s + q] = g_i[:, q * 128:(q + 1) * 128]
            xrc = jnp.concatenate([xr_ref[4 * s + q] for q in range(4)], axis=1)
            xic = jnp.concatenate([xi_ref[4 * s + q] for q in range(4)], axis=1)
            dcr_ref[s] += _dot3(_dot_tn, xrc, dyc)
            dci_ref[s] -= _dot3(_dot_tn, xic, dyc)

    st = pl.BlockSpec((NLB, tm, 128), lambda i: (0, i, 0))
    cw = pl.BlockSpec((4, 512, 128), lambda i: (0, 0, 0))
    half = pl.BlockSpec((tm, DSS), lambda i: (i, 0))
    return _pc(body, name=f"ssm_out_bwd_l{layer}", grid=(T // tm,),
               in_specs=[_stream_block(tm, DSS), pl.BlockSpec((tm, tm), lambda i: (0, 0)), half, half, st, st, half,
                         cw, cw, _gain_spec(DSS, layer), pl.BlockSpec((NSH, None, 128, DSS), lambda i: (0, 0, 0, 0))],
               out_specs=[st, st, half, half, half, _row_acc_spec(DSS), _row_acc_spec(DSS), cw, cw],
               out_shape=[S((NLB, T, 128), F32)] * 2 + [S((T, DSS), F32), S((T, DSS), BF16), S((T, DSS), BF16),
                                                        S((1, DSS), F32), S((1, DSS), F32),
                                                        S((4, 512, 128), F32), S((4, 512, 128), F32)],
               compiler_params=_cp(1))(dssm.reshape(16, T // 16, DSS), _perm_matrix(tm), y, z, xr, xi, u, cre, cim,
                                       dvec, wglu)


def _ssm_da(gr, gi, xr, xi, layer):
    T = gr.shape[1]
    tb = 4096 if T % 4096 == 0 else T

    def body(gr_ref, gi_ref, xr_ref, xi_ref, dr_ref, di_ref, lr, li):
        t = pl.program_id(1)

        @pl.when(t == 0)
        def _():
            dr_ref[...] = jnp.zeros_like(dr_ref)
            di_ref[...] = jnp.zeros_like(di_ref)
            lr[...] = jnp.zeros_like(lr)
            li[...] = jnp.zeros_like(li)

        g_r, g_i, x_r, x_i = gr_ref[...], gi_ref[...], xr_ref[...], xi_ref[...]
        pr = pltpu.roll(x_r, 1, 0)
        pi_ = pltpu.roll(x_i, 1, 0)
        g0r, g0i = g_r[0:1, :], g_i[0:1, :]
        fr = lr[7:8, :] - x_r[tb - 1:tb, :]
        fi = li[7:8, :] - x_i[tb - 1:tb, :]
        dr_ref[...] += jnp.sum(g_r * pr + g_i * pi_, axis=0, keepdims=True) + g0r * fr + g0i * fi
        di_ref[...] += jnp.sum(g_i * pr - g_r * pi_, axis=0, keepdims=True) + g0i * fr - g0r * fi
        lr[...] = x_r[tb - 8:tb, :]
        li[...] = x_i[tb - 8:tb, :]

    st = pl.BlockSpec((None, tb, 128), lambda k, t: (k, t, 0))
    out = pl.BlockSpec((None, 1, 128), lambda k, t: (k, 0, 0))
    return _pc(body, name=f"ssm_da_l{layer}", grid=(NLB, T // tb), in_specs=[st] * 4, out_specs=[out, out],
               out_shape=[S((NLB, 1, 128), F32)] * 2, scratch_shapes=[pltpu.VMEM((8, 128), F32)] * 2,
               compiler_params=_cp(2))(gr, gi, xr, xi)


def _ssm_in_bwd(gr, gi, u, bre, bim, du_direct, layer):
    T = u.shape[0]
    tm = TM

    def body(gr_ref, gi_ref, u_ref, pm_ref, br_ref, bi_ref, dd_ref, du_ref, dbr_ref, dbi_ref):
        i = pl.program_id(0)

        @pl.when(i == 0)
        def _():
            dbr_ref[...] = jnp.zeros_like(dbr_ref)
            dbi_ref[...] = jnp.zeros_like(dbi_ref)

        dus = []
        for s in range(4):
            grc = jnp.concatenate([gr_ref[4 * s + q] for q in range(4)], axis=1)
            gic = jnp.concatenate([gi_ref[4 * s + q] for q in range(4)], axis=1)
            uc = u_ref[:, s * 128:(s + 1) * 128]
            dus.append(_dot3(_dot_nt, grc, br_ref[s]) + _dot3(_dot_nt, gic, bi_ref[s]))
            dbr_ref[s] += _dot3(_dot_tn, uc, grc)
            dbi_ref[s] += _dot3(_dot_tn, uc, gic)
        du = jnp.concatenate(dus, axis=1) + dd_ref[...]
        du_ref[...] = _to_streams(du, pm_ref[...]).reshape(du_ref.shape)

    whole = pl.BlockSpec((4, 128, 512), lambda i: (0, 0, 0))
    st = pl.BlockSpec((NLB, tm, 128), lambda i: (0, i, 0))
    half = pl.BlockSpec((tm, DSS), lambda i: (i, 0))
    du, dbr, dbi = _pc(body, name=f"ssm_in_bwd_l{layer}", grid=(T // tm,),
                       in_specs=[st, st, half, pl.BlockSpec((tm, tm), lambda i: (0, 0)), whole, whole, half],
                       out_specs=[_stream_block(tm, DSS), whole, whole],
                       out_shape=[S((16, T // 16, DSS), F32), S((4, 128, 512), F32), S((4, 128, 512), F32)],
                       compiler_params=_cp(1))(gr, gi, u, _perm_matrix(tm), bre, bim, du_direct)
    return du.reshape(T, DSS), dbr, dbi


def _mix_out(outs, lses, ssm, h, attn_g, ssm_g, post_g, wout, layer):
    T = h.shape[0]
    tm = TM

    def body(o1, o2, o3, l1, l2, l3, s_ref, h_ref, ag_ref, sg_ref, pg_ref, w_ref, ho_ref, at_ref, ls_ref, mx_ref, mo_ref):
        la, lb, lc = l1[...], l2[...], l3[...]
        m = jnp.maximum(jnp.maximum(la, lb), lc)
        wa, wb, wc = jnp.exp(la - m), jnp.exp(lb - m), jnp.exp(lc - m)
        zs = wa + wb + wc
        attn = (wa * o1[...] + wb * o2[...] + wc * o3[...]) / zs
        at_ref[...] = attn
        ls_ref[...] = m + jnp.log(zs)
        mixed = jnp.concatenate([_rms_fwd(attn, ag_ref[...]), _rms_fwd(s_ref[...], sg_ref[...])], axis=1).astype(BF16)
        mx_ref[...] = mixed
        mo = sum(_dot(mixed[:, j * 256:(j + 1) * 256], w_ref[j]) for j in range(NSH))
        mo_ref[...] = mo
        ho_ref[...] = h_ref[...] + _rms_fwd(mo, pg_ref[...])

    row = pl.BlockSpec((tm, D), lambda i: (i, 0))
    half = pl.BlockSpec((tm, DA), lambda i: (i, 0))
    return _pc(body, name=f"mix_out_l{layer}", grid=(T // tm,),
               in_specs=[half] * 7 + [row, _gain_spec(DA, layer), _gain_spec(DSS, layer), _gain_spec(D, layer),
                                      pl.BlockSpec((NSH, None, 256, D), lambda i: (0, 0, 0, 0))],
               out_specs=[row, half, half, row, row],
               out_shape=[S((T, D), F32), S((T, DA), F32), S((T, DA), F32), S((T, D), BF16), S((T, D), F32)],
               compiler_params=_cp(1))(*outs, *lses, ssm, h, attn_g, ssm_g, post_g, wout)


def _mix_out_bwd(dout, mo, attn, ssm, attn_g, ssm_g, post_g, wout, layer):
    T = dout.shape[0]
    tm = TM
    head_sum =jnp.asarray(np.kron(np.eye(NH, dtype=np.float32), np.ones((HD, HD), np.float32)), BF16)

    def body(do_ref, mo_ref, at_ref, s_ref, ag_ref, sg_ref, pg_ref, w_ref, e_ref,
             da_ref, ds_ref, dl_ref, dmo_ref, dpg_ref, dag_ref, dsg_ref):
        i = pl.program_id(0)

        @pl.when(i == 0)
        def _():
            dpg_ref[...] = jnp.zeros_like(dpg_ref)
            dag_ref[...] = jnp.zeros_like(dag_ref)
            dsg_ref[...] = jnp.zeros_like(dsg_ref)

        dmo, dpg = _rms_bwd(do_ref[...], mo_ref[...], pg_ref[...])
        dpg_ref[...] += dpg
        dmob = dmo.astype(BF16)
        dmo_ref[...] = dmob
        dmix = jnp.concatenate([_dot_nt(dmob, w_ref[j]) for j in range(NSH)], axis=1)
        attn = at_ref[...]
        dat, dag = _rms_bwd(dmix[:, :DA], attn, ag_ref[...])
        dss, dsg = _rms_bwd(dmix[:, DA:], s_ref[...], sg_ref[...])
        dag_ref[...] += dag
        dsg_ref[...] += dsg
        da_ref[...] = dat
        ds_ref[...] = dss
        prod = dat * attn
        p1 = prod.astype(BF16)
        r1 = prod - p1.astype(F32)
        p2 = r1.astype(BF16)
        p3 = (r1 - p2.astype(F32)).astype(BF16)
        e = e_ref[...]
        dl_ref[...] = _dot(p1, e) + _dot(p2, e) + _dot(p3, e)

    row = pl.BlockSpec((tm, D), lambda i: (i, 0))
    half = pl.BlockSpec((tm, DA), lambda i: (i, 0))
    return _pc(body, name=f"mix_out_bwd_l{layer}", grid=(T // tm,),
               in_specs=[row, row, half, half, _gain_spec(DA, layer), _gain_spec(DSS, layer), _gain_spec(D, layer),
                         pl.BlockSpec((NSH, None, 256, D), lambda i: (0, 0, 0, 0)),
                         pl.BlockSpec((DA, DA), lambda i: (0, 0))],
               out_specs=[half, half, half, row, _row_acc_spec(D), _row_acc_spec(DA), _row_acc_spec(DSS)],
               out_shape=[S((T, DA), F32)] * 3 + [S((T, D), BF16), S((1, D), F32), S((1, DA), F32), S((1, DSS), F32)],
               compiler_params=_cp(1))(dout, mo, attn, ssm, attn_g, ssm_g, post_g, wout, head_sum)


def _ple_fwd(h, p3, wup, wgate, post_g, layer):
    T = h.shape[0]
    tm = TM

    def body(h_ref, p_ref, wu_ref, wg_ref, g_ref, ho_ref, e_ref, gt_ref):
        hv = h_ref[...]
        hb = hv.astype(BF16)
        pb = p_ref[...].astype(BF16)
        gte = sum(_dot(hb[:, j * 256:(j + 1) * 256], wg_ref[j]) for j in range(NSH))
        e = jnp.concatenate([_dot(pb, wu_ref[j]) for j in range(NSH)], axis=1)
        e_ref[...] = e
        gt_ref[...] = gte
        ho_ref[...] = hv + _rms_fwd(e * jax.nn.sigmoid(gte), g_ref[...])

    row = pl.BlockSpec((tm, D), lambda i: (i, 0))
    return _pc(body, name=f"ple_fwd_l{layer}", grid=(T // tm,),
               in_specs=[row, pl.BlockSpec((None, tm, PLE), lambda i: (layer, i, 0)),
                         pl.BlockSpec((NSH, None, PLE, 256), lambda i: (0, 0, 0, 0)),
                         pl.BlockSpec((NSH, None, 256, D), lambda i: (0, 0, 0, 0)), _gain_spec(D, layer)],
               out_specs=[row, row, row], out_shape=[S((T, D), F32)] * 3,
               compiler_params=_cp(1))(h, p3, wup, wgate, post_g)


def _ple_bwd(dout, e, gte, wgate, post_g, layer):
    T = dout.shape[0]
    tm = TM

    def body(do_ref, e_ref, gt_ref, wg_ref, g_ref, dh_ref, de_ref, dgt_ref, dg_ref):
        i = pl.program_id(0)

        @pl.when(i == 0)
        def _():
            dg_ref[...] = jnp.zeros_like(dg_ref)

        ev = e_ref[...]
        sg = jax.nn.sigmoid(gt_ref[...])
        do = do_ref[...]
        dple, dg = _rms_bwd(do, ev * sg, g_ref[...])
        dg_ref[...] += dg
        de = (dple * sg).astype(BF16)
        for j in range(NSH):
            de_ref[j] = de[:, j * 256:(j + 1) * 256]
        dgb = (dple * ev * sg * (1.0 - sg)).astype(BF16)
        dgt_ref[...] = dgb
        dh_ref[...] = do + jnp.concatenate([_dot_nt(dgb, wg_ref[j]) for j in range(NSH)], axis=1)

    row = pl.BlockSpec((tm, D), lambda i: (i, 0))
    return _pc(body, name=f"ple_bwd_l{layer}", grid=(T // tm,),
               in_specs=[row, row, row, pl.BlockSpec((NSH, None, 256, D), lambda i: (0, 0, 0, 0)), _gain_spec(D, layer)],
               out_specs=[row, pl.BlockSpec((NSH, tm, 256), lambda i: (0, i, 0)), row, _row_acc_spec(D)],
               out_shape=[S((T, D), F32), S((NSH, T, 256), BF16), S((T, D), BF16), S((1, D), F32)],
               compiler_params=_cp(1))(dout, e, gte, wgate, post_g)


def _loss_head(h, target):
    T = h.shape[0]
    tm = TM

    def body(h_ref, t_ref, dy_ref, l_ref):
        i = pl.program_id(0)

        @pl.when(i == 0)
        def _():
            l_ref[...] = jnp.zeros_like(l_ref)

        err = h_ref[...] - t_ref[...]
        dy_ref[...] = err * (1.0 / D)
        l_ref[...] += jnp.broadcast_to((0.5 / D) * jnp.sum(err * err), (1, 128))

    row = pl.BlockSpec((tm, D), lambda i: (i, 0))
    return _pc(body, name="loss_head", grid=(T // tm,), in_specs=[row, row],
               out_specs=[row, pl.BlockSpec((1, 128), lambda i: (0, 0))],
               out_shape=[S((T, D), F32), S((1, 128), F32)], compiler_params=_cp(1))(h, target)


def _local_step(x, p3, pos_col, target, weights_of, layer_grads_done, Sm):
    L = p3.shape[0]
    g3 = {n: Sm[n].reshape(L, 1, -1) for n in ("ffn1_pre_g", "ffn1_post_g", "mix_pre_g", "attn_norm_g", "ssm_norm_g",
                                                "mix_post_g", "ffn2_pre_g", "ffn2_post_g", "ple_post_g", "ssm_b_glu", "ssm_d")}
    rot = _rot_tables(pos_col)
    prep_names = ("ssm_lam_re", "ssm_lam_im", "ssm_log_dt", "ssm_b_re", "ssm_b_im", "ssm_c_re", "ssm_c_im")
    prep_all, prep_vjp = jax.vjp(jax.vmap(_ssm_prep), *[Sm[n] for n in prep_names])
    prep_cot = [None] * L

    saved = []
    h = x
    for l in range(L):
        W = weights_of(l, h)
        sv = {"h0": h, "W": W}
        h, sv["a1"], sv["b1"], sv["f1"], sv["xn1"] = _ffn_fwd(
            h, g3["ffn1_pre_g"], g3["ffn1_post_g"], W["ffn1_w_gate"], W["ffn1_w_up"], W["ffn1_w_down"], l, "1")
        sv["h1"] = h
        P, sv["ain"] = _mix_proj(h, g3["mix_pre_g"], W["w_in"], rot, l)
        sv["P"] = P
        ol = [_attn_fwd(P, d, l) for d in PATTERN_DILATIONS]
        prep = tuple(t[l] for t in prep_all)
        a_re, a_im, bre, bim, cre, cim = prep
        sv["prep"] = prep
        sv["u"], bur, bui = _ssm_in(P, bre, bim, l)
        xr, xi = _scan(bur, bui, a_re, a_im, False, l)
        sv["xr"], sv["xi"] = xr, xi
        ssm, sv["y"], sv["z"] = _ssm_out(xr, xi, sv["u"], cre, cim, g3["ssm_d"], W["ssm_w_glu"], g3["ssm_b_glu"], l)
        sv["ssm"] = ssm
        h, sv["attn"], sv["lse"], sv["mixed"], sv["mo"] = _mix_out(
            [o for o, _ in ol], [s for _, s in ol], ssm, h, g3["attn_norm_g"], g3["ssm_norm_g"], g3["mix_post_g"],
            W["w_out"], l)
        sv["h2"] = h
        h, sv["a2"], sv["b2"], sv["f2"], sv["xn2"] = _ffn_fwd(
            h, g3["ffn2_pre_g"], g3["ffn2_post_g"], W["ffn2_w_gate"], W["ffn2_w_up"], W["ffn2_w_down"], l, "2")
        sv["h3"] = h
        h, sv["e"], sv["gte"] = _ple_fwd(h, p3, W["ple_w_up"], W["ple_w_gate"], g3["ple_post_g"], l)
        saved.append(sv)

    dh, loss = _loss_head(h, target)

    G_layers = [{n: lax.empty((NSH, 1, r, c), BF16) for n, r, c in BIG} for _ in range(L)]
    sg = {n: [None] * L for n in SMALL}
    whole, shard, kcol = "whole", "shard", "cols"
    ple_g = g3["ple_post_g"]
    for l in reversed(range(L)):
        sv = saved[l]
        W = sv["W"]
        G, gl = G_layers[l], 0
        if l + 1 < L:
            ple_g = ple_g + layer_grads_done(l + 1, G_layers[l + 1])
        dh, de, dgte, sg["ple_post_g"][l] = _ple_bwd(dh, sv["e"], sv["gte"], W["ple_w_gate"], ple_g, l)
        G["ple_w_up"] = _dw(p3[l][None], de, G["ple_w_up"], gl, PLE, 256, whole, shard, f"dw_ple_up_l{l}")
        G["ple_w_gate"] = _dw(sv["h3"][None], dgte[None], G["ple_w_gate"], gl, 256, D, kcol, whole, f"dw_ple_gate_l{l}")
        dh, df, da, db, hh, sg["ffn2_pre_g"][l], sg["ffn2_post_g"][l] = _ffn_bwd(
            dh, sv["h2"], sv["f2"], sv["a2"], sv["b2"], g3["ffn2_pre_g"], g3["ffn2_post_g"],
            W["ffn2_w_gate"], W["ffn2_w_up"], W["ffn2_w_down"], l, "2")
        G["ffn2_w_gate"] = _dw(da, sv["xn2"][None], G["ffn2_w_gate"], gl, DFS, D, shard, whole, f"dw_ffn2_gate_l{l}")
        G["ffn2_w_up"] = _dw(db, sv["xn2"][None], G["ffn2_w_up"], gl, DFS, D, shard, whole, f"dw_ffn2_up_l{l}")
        G["ffn2_w_down"] = _dw(hh, df[None], G["ffn2_w_down"], gl, DFS, D, shard, whole, f"dw_ffn2_down_l{l}")
        a_re, a_im, bre, bim, cre, cim = sv["prep"]
        dattn, dssm, delta, dmo, sg["mix_post_g"][l], sg["attn_norm_g"][l], sg["ssm_norm_g"][l] = _mix_out_bwd(
            dh, sv["mo"], sv["attn"], sv["ssm"], g3["attn_norm_g"], g3["ssm_norm_g"], g3["mix_post_g"], W["w_out"], l)
        G["w_out"] = _dw(sv["mixed"][None], dmo[None], G["w_out"], gl, 256, D, kcol, whole, f"dw_out_l{l}")
        gnr, gni, du_direct, dz, yg, sg["ssm_b_glu"][l], dd, dcre, dcim = _ssm_out_bwd(
            dssm, sv["y"], sv["z"], sv["xr"], sv["xi"], sv["u"], cre, cim, g3["ssm_d"], W["ssm_w_glu"], l)
        sg["ssm_d"][l] = dd.reshape(Sm["ssm_d"].shape[1:])
        G["ssm_w_glu"] = _dw(yg[None], dz[None], G["ssm_w_glu"], gl, 128, DSS, kcol, whole, f"dw_glu_l{l}")
        gr, gi = _scan(gnr, gni, a_re, a_im, True, l)
        dar, dai = _ssm_da(gr, gi, sv["xr"], sv["xi"], l)
        du, dbre, dbim = _ssm_in_bwd(gr, gi, sv["u"], bre, bim, du_direct, l)
        prep_cot[l] = (dar, dai, dbre, dbim, dcre, dcim)
        acc = None
        for d in PATTERN_DILATIONS:
            acc = _attn_bwd(sv["P"], dattn, sv["lse"], delta, acc, d, l)
        dh, dP, sg["mix_pre_g"][l] = _mix_proj_bwd(acc[0], acc[1], acc[2], du, dh, sv["h1"], g3["mix_pre_g"],
                                                   W["w_in"], rot, l)
        G["w_in"] = _dw(sv["ain"][None], dP, G["w_in"], gl, D, DA,whole, shard, f"dw_in_l{l}")
        dh, df, da, db, hh, sg["ffn1_pre_g"][l], sg["ffn1_post_g"][l] = _ffn_bwd(
            dh, sv["h0"], sv["f1"], sv["a1"], sv["b1"], g3["ffn1_pre_g"], g3["ffn1_post_g"],
            W["ffn1_w_gate"], W["ffn1_w_up"], W["ffn1_w_down"], l, "1")
        G["ffn1_w_gate"] = _dw(da, sv["xn1"][None], G["ffn1_w_gate"], gl, DFS, D, shard, whole, f"dw_ffn1_gate_l{l}")
        G["ffn1_w_up"] = _dw(db, sv["xn1"][None], G["ffn1_w_up"], gl, DFS, D, shard, whole, f"dw_ffn1_up_l{l}")
        G["ffn1_w_down"] = _dw(hh, df[None], G["ffn1_w_down"], gl, DFS, D, shard, whole, f"dw_ffn1_down_l{l}")

    small = {n: jnp.stack([g.reshape(Sm[n].shape[1:]) for g in sg[n]]) for n in SMALL if n not in prep_names}
    small.update(zip(prep_names, prep_vjp(tuple(jnp.stack(c) for c in zip(*prep_cot)))))
    return loss, dh, G_layers[0], small


HBM_SPEC = pl.BlockSpec(memory_space=pltpu.HBM)


def _place():
    x, y, c = lax.axis_index("x"), lax.axis_index("y"), lax.axis_index("c")
    chips = [(1 - x, y), (x, 1 - y), (1 - x, 1 - y)]
    return x, y, c, chips


def _comm_params():
    return pltpu.CompilerParams(vmem_limit_bytes=VMEM_LIMIT)


def _gather_weights(ws, lands):
    n = len(ws)

    def body(*refs):
        ins, outs = refs[:n], refs[2 * n:3 * n]
        s_ici, r_ici, s_d2d, r_d2d = refs[3 * n:]
        x, y, c, chips = _place()

        def half(ref, t, hc):
            r2 = ws[t].shape[1] // 2
            return ref.at[:, pl.ds(hc * r2, r2), :]

        def ici(t, k, src_chip, to):
            j = 2 * src_chip[0] + src_chip[1]
            src = half(ins[t], t, c) if to is not None else half(outs[t].at[j], t, c)
            return pltpu.make_async_remote_copy(src_ref=src, dst_ref=half(outs[t].at[j], t, c),
                                                send_sem=s_ici.at[3 * t + k], recv_sem=r_ici.at[3 * t + k],
                                                device_id=to if to is not None else (x, y, c), device_id_type=MESH)

        def d2d(t, k, hc):
            j = 2 * chips[k][0] + chips[k][1]
            r = half(outs[t].at[j], t, hc)
            return pltpu.make_async_remote_copy(src_ref=r, dst_ref=r, send_sem=s_d2d.at[3 * t + k],
                                                recv_sem=r_d2d.at[3 * t + k], device_id=(x, y, 1 - c),
                                                device_id_type=MESH)

        sends = [ici(t, k, (x, y), (*chips[k], c)) for t in range(n) for k in range(3)]
        for cp in sends:
            cp.start()
        passed = []
        for t in range(n):
            for k in range(3):
                ici(t, k, chips[k], None).wait_recv()
                passed.append(d2d(t, k, c))
                passed[-1].start()
        for t in range(n):
            for k in range(3):
                d2d(t, k, 1 - c).wait_recv()
        for cp in sends + passed:
            cp.wait_send()

    return _pc(body, name="gather_weights", in_specs=[HBM_SPEC] * (2 * n), out_specs=[HBM_SPEC] * n,
               out_shape=[S(z.shape, z.dtype) for z in lands], input_output_aliases={n + t: t for t in range(n)},
               scratch_shapes=[pltpu.SemaphoreType.DMA((3 * n,))] * 4, compiler_params=_comm_params())(*ws, *lands)


SEM_SPEC = pl.BlockSpec(memory_space=pltpu.SEMAPHORE)
ANY_SPEC = pl.BlockSpec(memory_space=pl.ANY)
SPLIT_EFFECT = pltpu.SideEffectType.DATAFLOW_SIDE_EFFECTING


def _in_hbm(t):
    return pltpu.with_memory_space_constraint(t, pltpu.HBM)


def _place_own(ws, me_arr, layer):
    n = len(ws)

    def body(me_ref, *refs):
        for t in range(n):
            refs[n + t][...] = refs[t][...]

    gs = pltpu.PrefetchScalarGridSpec(
        num_scalar_prefetch=1, grid=(2,),
        in_specs=[pl.BlockSpec((w.shape[0], w.shape[1] // 2, w.shape[2]), lambda i, me: (0, i, 0)) for w in ws],
        out_specs=[pl.BlockSpec((None, w.shape[0], w.shape[1] // 2, w.shape[2]), lambda i, me: (me[0], 0, i, 0))
                   for w in ws])
    return _pc(body, name=f"gather_place_own_l{layer}", grid_spec=gs,
               out_shape=[S((NSH,) + w.shape, w.dtype) for w in ws], compiler_params=_cp(1))(me_arr, *ws)


def _gather_start(ws, lands, after, layer):
    n = len(ws)

    def body(*refs):
        ins, lz = refs[:n], refs[n:2 * n]
        s_sem, r_sem = refs[2 * n + 1], refs[2 * n + 2]
        token = refs[-1]
        x, y, c, chips = _place()
        for t in range(n):
            for k in range(3):
                pltpu.make_async_remote_copy(src_ref=ins[t], dst_ref=lz[t].at[2 * x + y], send_sem=s_sem.at[3 * t + k],
                                             recv_sem=r_sem.at[3 * t + k], device_id=(*chips[k], c),
                                             device_id_type=MESH).start()
        token[...] = jnp.zeros_like(token)

    hbm = [pltpu.HBM(w.shape, w.dtype) for w in ws] + [pltpu.HBM(z.shape, z.dtype) for z in lands]
    out = _pc(body, name=f"gather_start_l{layer}",
              out_shape=(pltpu.SemaphoreType.DMA((3 * n,)), pltpu.SemaphoreType.DMA((3 * n,)), *hbm, S((8, 128), F32)),
              in_specs=[HBM_SPEC] * (2 * n) + [ANY_SPEC],
              out_specs=(SEM_SPEC, SEM_SPEC, *([HBM_SPEC] * (2 * n)), pl.BlockSpec(memory_space=pltpu.VMEM)),
              input_output_aliases={i: 2 + i for i in range(2 * n)},
              compiler_params=pltpu.CompilerParams(has_side_effects=SPLIT_EFFECT))(
                  *[_in_hbm(w) for w in ws], *[_in_hbm(z) for z in lands], after)
    return out[0], out[1], out[2:2 + n], out[2 + n:2 + 2 * n], out[-1]


def _gather_wait(s_sem, r_sem, ws, lands, after, layer):
    n = len(ws)

    def body(*refs):
        ins, lz = refs[:n], refs[n:2 * n]
        s_ref, r_ref = refs[2 * n], refs[2 * n + 1]
        x, y, c, chips = _place()
        for t in range(n):
            for k in range(3):
                cp = pltpu.make_async_remote_copy(src_ref=ins[t], dst_ref=lz[t].at[2 * x + y], send_sem=s_ref.at[3 * t + k],
                                                  recv_sem=r_ref.at[3 * t + k], device_id=(*chips[k], c),
                                                  device_id_type=MESH)
                cp.wait_send()
                cp.wait_recv()

    hbm = [pltpu.HBM(w.shape, w.dtype) for w in ws] + [pltpu.HBM(z.shape, z.dtype) for z in lands]
    out = _pc(body, name=f"gather_wait_l{layer}", out_shape=tuple(hbm),
              in_specs=[HBM_SPEC] * (2 * n) + [SEM_SPEC, SEM_SPEC, ANY_SPEC], out_specs=tuple([HBM_SPEC] * (2 * n)),
              input_output_aliases={i: i for i in range(2 * n)},
              compiler_params=pltpu.CompilerParams(has_side_effects=SPLIT_EFFECT))(*ws, *lands, s_sem, r_sem, after)
    return out[n:]


def _swap_halves(gs, tag):
    n = len(gs)

    def body(*refs):
        ins, outs = refs[:n], refs[n:2 * n]
        s_sem, r_sem = refs[2 * n:]
        x, y, c, _ = _place()
        cps = []
        for t in range(n):
            r2 = gs[t].shape[2] // 2
            cps.append(pltpu.make_async_remote_copy(
                src_ref=ins[t].at[:, :, pl.ds((1 - c) * r2, r2), :], dst_ref=outs[t], send_sem=s_sem.at[t],
                recv_sem=r_sem.at[t], device_id=(x, y, 1 - c), device_id_type=MESH))
            cps[-1].start()
        for cp in cps:
            cp.wait_recv()
        for cp in cps:
            cp.wait_send()

    return _pc(body, name=f"grad_swap_halves_{tag}", in_specs=[HBM_SPEC] * n, out_specs=[HBM_SPEC] * n,
               out_shape=[S(g.shape[:2] + (g.shape[2] // 2, g.shape[3]), g.dtype) for g in gs],
               scratch_shapes=[pltpu.SemaphoreType.DMA((n,))] * 2, compiler_params=_comm_params())(*gs)


def _add_half(g, landed, c_arr, name):
    _, L, r2, cols = landed.shape

    def body(c_ref, g_ref, l_ref, o_ref):
        o_ref[...] = (g_ref[...].astype(F32) + l_ref[...].astype(F32)).astype(BF16)

    gs = pltpu.PrefetchScalarGridSpec(
        num_scalar_prefetch=1, grid=(NSH, L),
        in_specs=[pl.BlockSpec((None, None, r2, cols), lambda j, l, c: (j, l, c[0], 0)),
                  pl.BlockSpec((None, None, r2, cols), lambda j, l, c: (j, l, 0, 0))],
        out_specs=pl.BlockSpec((None, None, r2, cols), lambda j, l, c: (j, l, 0, 0)))
    return _pc(body, name=name, grid_spec=gs, out_shape=S(landed.shape, BF16), compiler_params=_cp(2))(c_arr, g, landed)


def _partial_copies(ins, lz, s_sem, r_sem):
    x, y, c, chips = _place()
    return [pltpu.make_async_remote_copy(src_ref=ins[t].at[2 * chips[k][0] + chips[k][1]], dst_ref=lz[t].at[k],
                                         send_sem=s_sem.at[3 * t + k], recv_sem=r_sem.at[3 * t + k],
                                         device_id=(*chips[k], c), device_id_type=MESH)
            for t in range(len(ins)) for k in range(3)]


def _partial_send_start(ps, lands):
    n = len(ps)

    def body(*refs):
        for cp in _partial_copies(refs[:n], refs[n:2 * n], refs[2 * n], refs[2 * n + 1]):
            cp.start()
        refs[-1][...] = jnp.zeros_like(refs[-1])

    hbm = [pltpu.HBM(p.shape, p.dtype) for p in ps] + [pltpu.HBM(z.shape, z.dtype) for z in lands]
    out = _pc(body, name="grad_partial_send_start",
              out_shape=(pltpu.SemaphoreType.DMA((3 * n,)), pltpu.SemaphoreType.DMA((3 * n,)), *hbm, S((8, 128), F32)),
              in_specs=[HBM_SPEC] * (2 * n),
              out_specs=(SEM_SPEC, SEM_SPEC, *([HBM_SPEC] * (2 * n)), pl.BlockSpec(memory_space=pltpu.VMEM)),
              input_output_aliases={i: 2 + i for i in range(2 * n)},
              compiler_params=pltpu.CompilerParams(has_side_effects=SPLIT_EFFECT))(
                  *[_in_hbm(p) for p in ps], *[_in_hbm(z) for z in lands])
    return out[0], out[1], out[2:2 + n], out[2 + n:2 + 2 * n], out[-1]


def _partial_send_wait(s_sem, r_sem, ps, lands, after):
    n = len(ps)

    def body(*refs):
        for cp in _partial_copies(refs[:n], refs[n:2 * n], refs[2 * n], refs[2 * n + 1]):
            cp.wait_send()
            cp.wait_recv()

    hbm = [pltpu.HBM(p.shape, p.dtype) for p in ps] + [pltpu.HBM(z.shape, z.dtype) for z in lands]
    out = _pc(body, name="grad_partial_send_wait", out_shape=tuple(hbm),
              in_specs=[HBM_SPEC] * (2 * n) + [SEM_SPEC, SEM_SPEC, ANY_SPEC], out_specs=tuple([HBM_SPEC] * (2 * n)),
              input_output_aliases={i: i for i in range(2 * n)},
              compiler_params=pltpu.CompilerParams(has_side_effects=SPLIT_EFFECT))(*ps, *lands, s_sem, r_sem, after)
    return out[:n], out[n:]


def _sum_shards(part, landed, me_arr, c_arr, buf, first_layer, name):
    _, nl, r2, cols = landed.shape

    def body(me_ref, c_ref, p_ref, l_ref, b_ref, o_ref):
        o_ref[...] = ((p_ref[...].astype(F32) + l_ref[0].astype(F32)) + l_ref[1].astype(F32)) + l_ref[2].astype(F32)

    gs = pltpu.PrefetchScalarGridSpec(
        num_scalar_prefetch=2, grid=(nl,),
        in_specs=[pl.BlockSpec((None, None, r2, cols), lambda l, me, c: (me[0], l, 0, 0)),
                  pl.BlockSpec((3, None, r2, cols), lambda l, me, c: (0, l, 0, 0)), ANY_SPEC],
        out_specs=pl.BlockSpec((None, r2, cols), lambda l, me, c: (first_layer + l, c[0], 0)))
    return _pc(body, name=name, grid_spec=gs, out_shape=S(buf.shape, F32), input_output_aliases={4: 0},
               compiler_params=_cp(1))(me_arr, c_arr, part, landed, buf)


def _direct_grad_copies(ins, lz, s_sem, r_sem):
    x, y, c, chips = _place()
    sends, recvs = [], []
    for t in range(len(ins)):
        r2 = ins[t].shape[2] // 2
        half = lambda j, h: ins[t].at[j, :, pl.ds(h * r2, r2), :]

        def copy(src, slot, s_idx, r_idx, to):
            return pltpu.make_async_remote_copy(src_ref=src, dst_ref=lz[t].at[slot], send_sem=s_sem.at[7 * t + s_idx],
                                                recv_sem=r_sem.at[7 * t + r_idx], device_id=to, device_id_type=MESH)

        for k in range(3):
            for h in range(2):
                sends.append(copy(half(2 * chips[k][0] + chips[k][1], h), 2 * k + c, 2 * k + h, 2 * k + c, (*chips[k], h)))
        sends.append(copy(half(2 * x + y, 1 - c), 6, 6, 6, (x, y, 1 - c)))
        recvs += [copy(half(0, 0), s, s, s, (x, y, c)) for s in range(7)]
    return sends, recvs


def _send_start(gs, lands, layer):
    n = len(gs)
    ps = gs

    def body(*refs):
        sends, _ = _direct_grad_copies(refs[:n], refs[n:2 * n], refs[2 * n], refs[2 * n + 1])
        for cp in sends:
            cp.start()
        refs[-1][...] = jnp.zeros_like(refs[-1])

    hbm = [pltpu.HBM(p.shape, p.dtype) for p in ps] + [pltpu.HBM(z.shape, z.dtype) for z in lands]
    out = _pc(body, name=f"grad_send_start_l{layer}",
              out_shape=(pltpu.SemaphoreType.DMA((7 * n,)), pltpu.SemaphoreType.DMA((7 * n,)), *hbm, S((8, 128), F32)),
              in_specs=[HBM_SPEC] * (2 * n),
              out_specs=(SEM_SPEC, SEM_SPEC, *([HBM_SPEC] * (2 * n)), pl.BlockSpec(memory_space=pltpu.VMEM)),
              input_output_aliases={i: 2 + i for i in range(2 * n)},
              compiler_params=pltpu.CompilerParams(has_side_effects=SPLIT_EFFECT))(
                  *[_in_hbm(p) for p in ps], *[_in_hbm(z) for z in lands])
    return out[0], out[1], out[2:2 + n], out[2 + n:2 + 2 * n], out[-1]


def _send_wait(s_sem, r_sem, ps, lands, after, layer):
    n = len(ps)

    def body(*refs):
        sends, recvs = _direct_grad_copies(refs[:n], refs[n:2 * n], refs[2 * n], refs[2 * n + 1])
        for cp in sends:
            cp.wait_send()
        for cp in recvs:
            cp.wait_recv()

    hbm = [pltpu.HBM(p.shape, p.dtype) for p in ps] + [pltpu.HBM(z.shape, z.dtype) for z in lands]
    out = _pc(body, name=f"grad_send_wait_l{layer}", out_shape=tuple(hbm),
              in_specs=[HBM_SPEC] * (2 * n) + [SEM_SPEC, SEM_SPEC, ANY_SPEC], out_specs=tuple([HBM_SPEC] * (2 * n)),
              input_output_aliases={i: i for i in range(2 * n)},
              compiler_params=pltpu.CompilerParams(has_side_effects=SPLIT_EFFECT))(*ps, *lands, s_sem, r_sem, after)
    return out[:n], out[n:]


def _sum_direct(g, landed, me_arr, c_arr, buf, first_layer, name):
    _, nl, r2, cols = landed.shape

    def body(me_ref, c_ref, g_ref, l_ref, b_ref, o_ref):
        tot = g_ref[...].astype(F32)
        for s in range(7):
            tot = tot + l_ref[s].astype(F32)
        o_ref[...] = tot

    gs = pltpu.PrefetchScalarGridSpec(
        num_scalar_prefetch=2, grid=(nl,),
        in_specs=[pl.BlockSpec((None, None, r2, cols), lambda l, me, c: (me[0], l, c[0], 0)),
                  pl.BlockSpec((7, None, r2, cols), lambda l, me, c: (0, l, 0, 0)), ANY_SPEC],
        out_specs=pl.BlockSpec((None, r2, cols), lambda l, me, c: (first_layer + l, c[0], 0)))
    return _pc(body, name=name, grid_spec=gs, out_shape=S(buf.shape, F32), input_output_aliases={4: 0},
               compiler_params=_cp(1))(me_arr, c_arr, g, landed, buf)


def _share_halves(bufs):
    n = len(bufs)

    def body(*refs):
        ins, outs = refs[:n], refs[n:2 * n]
        s_sem, r_sem = refs[2 * n:]
        x, y, c, _ = _place()
        cps = []
        for t in range(n):
            r2 = bufs[t].shape[1] // 2
            cps.append(pltpu.make_async_remote_copy(
                src_ref=ins[t].at[:, pl.ds(c * r2, r2), :], dst_ref=outs[t].at[:, pl.ds(c * r2, r2), :],
                send_sem=s_sem.at[t], recv_sem=r_sem.at[t], device_id=(x, y, 1 - c), device_id_type=MESH))
            cps[-1].start()
        for cp in cps:
            cp.wait_recv()
        for cp in cps:
            cp.wait_send()

    return _pc(body, name="grad_share_halves", in_specs=[HBM_SPEC] * n, out_specs=[HBM_SPEC] * n,
               out_shape=[S(b.shape, b.dtype) for b in bufs], input_output_aliases={t: t for t in range(n)},
               scratch_shapes=[pltpu.SemaphoreType.DMA((n,))] * 2, compiler_params=_comm_params())(*bufs)


def _gather_small(v):
    nr = v.shape[0]

    def body(v_ref, out_ref, send_sems, recv_sems, local_sem):
        x, y, c, chips = _place()
        me, sibling = (x, y, c), (x, y, 1 - c)

        def rows(px, py, pc):
            return out_ref.at[pl.ds((4 * px + 2 * py + pc) * nr, nr), :]

        def copy(k, block, to, src=None):
            return pltpu.make_async_remote_copy(src_ref=rows(*block) if src is None else src, dst_ref=rows(*block),
                                                send_sem=send_sems.at[k], recv_sem=recv_sems.at[k], device_id=to,
                                                device_id_type=MESH)

        mine = pltpu.make_async_copy(v_ref, rows(*me), local_sem)
        mine.start()
        first = [copy(0, me, sibling, src=v_ref)]
        first += [copy(1 + j, me, (*chip, c), src=v_ref) for j, chip in enumerate(chips)]
        for cp in first:
            cp.start()
        passed = [copy(4 + j, (*chip, c), sibling) for j, chip in enumerate(chips)]
        for j, chip in enumerate(chips):
            copy(1 + j, (*chip, c), me).wait_recv()
            passed[j].start()
        copy(0, sibling, me).wait_recv()
        for j, chip in enumerate(chips):
            copy(4 + j, (*chip, 1 - c), me).wait_recv()
        for cp in first + passed:
            cp.wait_send()
        mine.wait()

    vm = pl.BlockSpec(memory_space=pltpu.VMEM)
    return _pc(body, name="gather_small_grads", in_specs=[vm], out_specs=vm, out_shape=S((8 * nr, 128), F32),
               scratch_shapes=[pltpu.SemaphoreType.DMA((7,)), pltpu.SemaphoreType.DMA((7,)), pltpu.SemaphoreType.DMA],
               compiler_params=_comm_params())(v)


def _adamw_math(w, g, m, v):
    m2 = ADAM_B1 * m + (1.0 - ADAM_B1) * g
    v2 = ADAM_B2 * v + (1.0 - ADAM_B2) * (g * g)
    m_hat = m2 / (1.0 - ADAM_B1 ** ADAM_STEP)
    v_hat = v2 / (1.0 - ADAM_B2 ** ADAM_STEP)
    return -ADAM_LR * (m_hat / (jnp.sqrt(v_hat) + ADAM_EPS) + ADAM_WD * w), m2, v2


def _adamw(w, g, m, v, name):
    L, R, C = w.shape
    rb = R // 2 if R >= 512 else R

    def body(w_ref, g_ref, m_ref, v_ref, d_ref, m2_ref, v2_ref):
        d_ref[...], m2_ref[...], v2_ref[...] = _adamw_math(w_ref[...], g_ref[...], m_ref[...], v_ref[...])

    blk = pl.BlockSpec((None, rb, C), lambda l, r: (l, r, 0))
    return _pc(body, name=name, grid=(L, R // rb), in_specs=[blk] * 4, out_specs=[blk] * 3,
               out_shape=[S(w.shape, F32)] * 3, compiler_params=_cp(2))(w, g, m, v)


def _adamw_small(gathered, w, m, v):
    nr = w.shape[0]
    rb = nr // 5

    def body(a_ref, w_ref, m_ref, v_ref, g_ref, d_ref, m2_ref, v2_ref):
        g = a_ref[0]
        for k in range(1, 8):
            g = g + a_ref[k]
        g_ref[...] = g
        d_ref[...], m2_ref[...], v2_ref[...] = _adamw_math(w_ref[...], g, m_ref[...], v_ref[...])

    blk = pl.BlockSpec((rb, 128), lambda i: (i, 0))
    return _pc(body, name="adamw_small", grid=(nr // rb,), in_specs=[pl.BlockSpec((8, rb, 128), lambda i: (0, i, 0))] + [blk] * 3,
               out_specs=[blk] * 4, out_shape=[S((nr, 128), F32)] * 4, compiler_params=_cp(1))(gathered, w, m, v)


SMALL_ROWS = 4520


def _pack(arrs):
    flat = jnp.concatenate([a.reshape(-1) for a in arrs])
    return jnp.pad(flat, (0, SMALL_ROWS * 128 - flat.shape[0])).reshape(SMALL_ROWS, 128)


def _unpack(packed, like):
    flat = packed.reshape(-1)
    out, off = [], 0
    for a in like:
        out.append(flat[off:off + a.size].reshape(a.shape))
        off += a.size
    return out


def kernel(x, p, positions, ffn1_pre_g, ffn1_w_gate, ffn1_w_up, ffn1_w_down, ffn1_post_g, mix_pre_g, w_in, attn_norm_g, ssm_lam_re, ssm_lam_im, ssm_log_dt, ssm_b_re, ssm_b_im, ssm_c_re, ssm_c_im, ssm_d, ssm_w_glu, ssm_b_glu, ssm_norm_g, w_out, mix_post_g, ffn2_pre_g, ffn2_w_gate, ffn2_w_up, ffn2_w_down, ffn2_post_g, ple_w_up, ple_w_gate, ple_post_g, loss_target, m_ffn1_pre_g, m_ffn1_w_gate, m_ffn1_w_up, m_ffn1_w_down, m_ffn1_post_g, m_mix_pre_g, m_w_in, m_attn_norm_g, m_ssm_lam_re, m_ssm_lam_im, m_ssm_log_dt, m_ssm_b_re, m_ssm_b_im, m_ssm_c_re, m_ssm_c_im, m_ssm_d, m_ssm_w_glu, m_ssm_b_glu, m_ssm_norm_g, m_w_out, m_mix_post_g, m_ffn2_pre_g, m_ffn2_w_gate, m_ffn2_w_up, m_ffn2_w_down, m_ffn2_post_g, m_ple_w_up, m_ple_w_gate, m_ple_post_g, v_ffn1_pre_g, v_ffn1_w_gate, v_ffn1_w_up, v_ffn1_w_down, v_ffn1_post_g, v_mix_pre_g, v_w_in, v_attn_norm_g, v_ssm_lam_re, v_ssm_lam_im, v_ssm_log_dt, v_ssm_b_re, v_ssm_b_im, v_ssm_c_re, v_ssm_c_im, v_ssm_d, v_ssm_w_glu, v_ssm_b_glu, v_ssm_norm_g, v_w_out, v_mix_post_g, v_ffn2_pre_g, v_ffn2_w_gate, v_ffn2_w_up, v_ffn2_w_down, v_ffn2_post_g, v_ple_w_up, v_ple_w_gate, v_ple_post_g):
    a = dict(locals())
    T = x.shape[1]
    big_names = [n for n, _, _ in BIG]
    for n in TRANSPOSED:
        for pre in ("", "m_", "v_"):
            a[pre + n] = jnp.swapaxes(a[pre + n], 1, 2)

    own = [a[n].astype(BF16) for n in big_names]
    n_layers = own[0].shape[0]
    per_layer = [[w[l:l + 1] for w in own] for l in range(n_layers)]
    c_arr = lax.axis_index("c").astype(jnp.int32).reshape(1)
    me_arr = (2 * lax.axis_index("x") + lax.axis_index("y")).astype(jnp.int32).reshape(1)
    first = dict(zip(big_names, _gather_weights(per_layer[0], _place_own(per_layer[0], me_arr, 0))))
    pending, anchor, queued_behind = {}, jnp.zeros((), F32), first[big_names[0]]
    for l in range(1, n_layers):
        s_sem, r_sem, ws_thru, lands_thru, token = _gather_start(per_layer[l], _place_own(per_layer[l], me_arr, l),
                                                                 queued_behind, l)
        pending[l] = (s_sem, r_sem, ws_thru, lands_thru)
        anchor = anchor + token[0, 0]
        queued_behind = token

    def weights_of(l, after):
        if l == 0:
            return first
        return dict(zip(big_names, _gather_wait(*pending[l], after, l)))

    Sm = {n: a[n] for n in SMALL}
    Sm["ffn1_pre_g"] = Sm["ffn1_pre_g"] + anchor

    pos = jnp.broadcast_to(positions.reshape(1, T, 1).astype(F32), (1, T, 128))
    sent = {}

    def layer_grads_done(l, G):
        gs = [G[n] for n in big_names]
        lands = [lax.empty((7, 1, g.shape[2] // 2, g.shape[3]), BF16) for g in gs]
        s_sem, r_sem, gs_thru, lands_thru, token = _send_start(gs, lands, l)
        sent[l] = (s_sem, r_sem, gs_thru, lands_thru)
        return token[0, 0]

    loss, gx, G_first, small = _local_step(
        _reorder(x, True, "to_streams_x")[0], _reorder(p[:, 0], True, "to_streams_p"),
        _reorder(pos, True, "to_streams_pos")[0, :, :1], _reorder(loss_target, True, "to_streams_target")[0],
        weights_of, layer_grads_done, Sm)
    gx = _reorder(gx[None], False, "to_time_grad_x")

    gs0 = [G_first[n] for n in big_names]
    parts = [_add_half(g, la, c_arr, f"grad_add_half_{n}") for g, la, n in zip(gs0, _swap_halves(gs0, "first"), big_names)]
    first_sent = _partial_send_start(parts, [lax.empty((3,) + pt.shape[1:], BF16) for pt in parts])

    bufs = [lax.empty((n_layers, r, c), F32) for _, r, c in BIG]
    for l in sorted(sent, reverse=True):
        gs, landed = _send_wait(*sent[l], first_sent[-1], l)
        bufs = [_sum_direct(g, la, me_arr, c_arr, b, l, f"grad_sum_direct_l{l}_{n}")
                for g, la, b, n in zip(gs, landed, bufs, big_names)]
    parts, landed = _partial_send_wait(*first_sent[:-1], bufs[0])
    bufs = [_sum_shards(pt, la, me_arr, c_arr, b, 0, f"grad_sum_shards_first_{n}")
            for pt, la, b, n in zip(parts, landed, bufs, big_names)]
    grads = dict(zip(big_names, _share_halves(bufs)))

    small_g = _gather_small(_pack([small[n] for n in SMALL])).reshape(8, SMALL_ROWS, 128)
    sg, sd, sm, sv = _adamw_small(small_g, _pack([a[n] for n in SMALL]), _pack([a["m_" + n] for n in SMALL]),
                                  _pack([a["v_" + n] for n in SMALL]))
    like = [a[n] for n in SMALL]
    res = {}
    for n, g_, d_, m_, v_ in zip(SMALL, _unpack(sg, like), _unpack(sd, like), _unpack(sm, like), _unpack(sv, like)):
        res[n] = (g_, d_, m_, v_)
    for n in big_names:
        d_, m_, v_ = _adamw(a[n], grads[n], a["m_" + n], a["v_" + n], f"adamw_{n}")
        res[n] = (grads[n], d_, m_, v_)
        if n in TRANSPOSED:
            res[n] = tuple(jnp.swapaxes(t, 1, 2) for t in res[n])

    total = lax.psum(loss[0, 0], ("x", "y", "c"))
    return (total, gx, *[res[n][0] for n in WEIGHTS], *[res[n][1] for n in WEIGHTS],
            *[res[n][2] for n in WEIGHTS], *[res[n][3] for n in WEIGHTS])
```

```python
import functools
import math

import numpy as np
import jax
import jax.numpy as jnp
from jax import lax
from jax.experimental import pallas as pl
from jax.experimental.pallas import tpu as pltpu

F32 = jnp.float32
BF16 = jnp.bfloat16
S = jax.ShapeDtypeStruct
MESH = pl.DeviceIdType.MESH

D = 1024
DA = 512
DSS = 512
HD = 64
NH = 8
BAND = 128
NSH = 4
DFS = 704
PLE = 256
EPS = 1e-6
ROPE_THETA = 500000.0
PATTERN_DILATIONS = (1, 4, 16)
NLB = 16
ADAM_LR, ADAM_B1, ADAM_B2, ADAM_EPS, ADAM_WD, ADAM_STEP = 0.001, 0.9, 0.999, 1e-08, 0.01, 10

VMEM_LIMIT = 56 * 1024 * 1024
TM = 512
TMB = 256

BIG = (
    ("ffn1_w_gate", DFS, D), ("ffn1_w_up", DFS, D), ("ffn1_w_down", DFS, D),
    ("w_in", D, 512), ("ssm_w_glu", 128, 512), ("w_out", 256, D),
    ("ffn2_w_gate", DFS, D), ("ffn2_w_up", DFS, D), ("ffn2_w_down", DFS, D),
    ("ple_w_up", PLE, 256), ("ple_w_gate", 256, D),
)
TRANSPOSED = ("ffn1_w_gate", "ffn1_w_up", "ffn2_w_gate", "ffn2_w_up")
SMALL = ("ffn1_pre_g", "ffn1_post_g", "mix_pre_g", "attn_norm_g", "ssm_lam_re", "ssm_lam_im", "ssm_log_dt",
         "ssm_b_re", "ssm_b_im", "ssm_c_re", "ssm_c_im", "ssm_d", "ssm_b_glu", "ssm_norm_g", "mix_post_g",
         "ffn2_pre_g", "ffn2_post_g", "ple_post_g")
WEIGHTS = ("ffn1_pre_g", "ffn1_w_gate", "ffn1_w_up", "ffn1_w_down", "ffn1_post_g", "mix_pre_g", "w_in", "attn_norm_g",
           "ssm_lam_re", "ssm_lam_im", "ssm_log_dt", "ssm_b_re", "ssm_b_im", "ssm_c_re", "ssm_c_im", "ssm_d",
           "ssm_w_glu", "ssm_b_glu", "ssm_norm_g", "w_out", "mix_post_g", "ffn2_pre_g", "ffn2_w_gate", "ffn2_w_up",
           "ffn2_w_down", "ffn2_post_g", "ple_w_up", "ple_w_gate", "ple_post_g")


def _pc(body, **kw):
    return pl.pallas_call(body, **kw)


def _cp(n_grid):
    return pltpu.CompilerParams(dimension_semantics=("arbitrary",) * n_grid, vmem_limit_bytes=VMEM_LIMIT)


def _dot(a, b):
    return jnp.dot(a, b, preferred_element_type=F32)


def _dot_nt(a, b):
    return lax.dot_general(a, b, (((1,), (1,)), ((), ())), preferred_element_type=F32)


def _dot_tn(a, b):
    return lax.dot_general(a, b, (((0,), (0,)), ((), ())), preferred_element_type=F32)


def _split(a):
    hi = a.astype(BF16)
    return hi, (a - hi.astype(F32)).astype(BF16)


def _dot3(fn, a, b):
    ah, al = _split(a)
    bh, bl = _split(b)
    return fn(ah, bh) + fn(ah, bl) + fn(al, bh)


def _rms_fwd(x, g):
    r = lax.rsqrt(jnp.mean(x * x, axis=-1, keepdims=True) + EPS)
    return x * r * g


def _rms_bwd(dy, x, g):
    r = lax.rsqrt(jnp.mean(x * x, axis=-1, keepdims=True) + EPS)
    xr = x * r
    gd = dy * g
    dx = r * (gd - xr * jnp.mean(gd * xr, axis=-1, keepdims=True))
    dg = jnp.sum(dy * xr, axis=0, keepdims=True)
    return dx, dg


def _gelu(y):
    k = math.sqrt(2.0 / math.pi)
    return 0.5 * y * (1.0 + jnp.tanh(k * (y + 0.044715 * y * y * y)))


def _gelu_grad(y):
    k = math.sqrt(2.0 / math.pi)
    t = jnp.tanh(k * (y + 0.044715 * y * y * y))
    return 0.5 * (1.0 + t) + 0.5 * y * (1.0 - t * t) * k * (1.0 + 3 * 0.044715 * y * y)


def _gain_spec(n, layer):
    return pl.BlockSpec((None, 1, n), lambda *_: (layer, 0, 0))


def _row_acc_spec(n):
    return pl.BlockSpec((1, n), lambda *_: (0, 0))


def _rot_tables(pos_col):
    T = pos_col.shape[0]
    half = HD // 8
    inv = (ROPE_THETA ** (-np.arange(half, dtype=np.float32) * (2.0 / (2 * half)))).astype(np.float32)
    lane_freq = np.tile(np.concatenate([inv, inv, np.zeros(HD - 2 * half, np.float32)]), NH)[None, :]

    def body(p_ref, f_ref, c_ref, s1_ref, s2_ref):
        ang = p_ref[...] * f_ref[...]
        d = lax.broadcasted_iota(jnp.int32, ang.shape, 1) % HD
        cs = jnp.cos(ang)
        sn = jnp.sin(ang)
        c_ref[...] = jnp.where(d < 2 * half, cs, 1.0)
        s1_ref[...] = jnp.where(d < half, -sn, 0.0)
        s2_ref[...] = jnp.where((d >= half) & (d < 2 * half), sn, 0.0)

    tm = TM
    return _pc(body, name="rot_tables", grid=(T // tm,),
               in_specs=[pl.BlockSpec((tm, 1), lambda i: (i, 0)), pl.BlockSpec((1, DA), lambda i: (0, 0))],
               out_specs=[pl.BlockSpec((tm, DA), lambda i: (i, 0))] * 3,
               out_shape=[S((T, DA), F32)] * 3, compiler_params=_cp(1))(pos_col, jnp.asarray(lane_freq))


def _rot_fwd(t, c, s1, s2):
    return t * c + pltpu.roll(t, DA - 8, 1) * s1 + pltpu.roll(t, 8, 1) * s2


def _rot_bwd(g, c, s1, s2):
    return g * c + pltpu.roll(g * s1, 8, 1) + pltpu.roll(g * s2, DA - 8, 1)


def _ffn_weight_spec():
    return pl.BlockSpec((NSH, None, DFS, D), lambda i: (0, 0, 0, 0), pipeline_mode=pl.Buffered(1))


def _ffn_fwd(h, pre_g, post_g, wg, wu, wd, layer, tag):
    T = h.shape[0]
    tm = TM
    nt = T // tm

    def body(h_ref, pg_ref, qg_ref, wg_ref, wu_ref, wd_ref, ho_ref, a_ref, b_ref, f_ref, xn_ref):
        hv = h_ref[...]
        xb = _rms_fwd(hv, pg_ref[...]).astype(BF16)
        xn_ref[...] = xb
        f = None
        for j in range(NSH):
            ab = _dot_nt(xb, wg_ref[j]).astype(BF16)
            bb = _dot_nt(xb, wu_ref[j]).astype(BF16)
            a_ref[j] = ab
            b_ref[j] = bb
            a = ab.astype(F32)
            hh = (a * jax.nn.sigmoid(a) * bb.astype(F32)).astype(BF16)
            part = _dot(hh, wd_ref[j])
            f = part if f is None else f + part
        f_ref[...] = f
        ho_ref[...] = hv + 0.5 * _rms_fwd(f, qg_ref[...])

    row = pl.BlockSpec((tm, D), lambda i: (i, 0))
    act = pl.BlockSpec((NSH, tm, DFS), lambda i: (0, i, 0))
    return _pc(body, name=f"ffn_fwd_{tag}_l{layer}", grid=(nt,),
               in_specs=[row, _gain_spec(D, layer), _gain_spec(D, layer)] + [_ffn_weight_spec()] * 3,
               out_specs=[row, act, act, row, row],
               out_shape=[S((T, D), F32), S((NSH, T, DFS), BF16), S((NSH, T, DFS), BF16), S((T, D), F32), S((T, D), BF16)],
               compiler_params=_cp(1))(h, pre_g, post_g, wg, wu, wd)


def _ffn_bwd(dout, h, f, a, b, pre_g, post_g, wg, wu, wd, layer, tag):
    T = h.shape[0]
    tm = TMB
    nt = T // tm

    def body(do_ref, h_ref, f_ref, a_ref, b_ref, pg_ref, qg_ref, wg_ref, wu_ref, wd_ref,
             dh_ref, df_ref, da_ref, db_ref, hh_ref, dpg_ref, dqg_ref):
        @pl.when(pl.program_id(0) == 0)
        def _():
            dpg_ref[...] = jnp.zeros_like(dpg_ref)
            dqg_ref[...] = jnp.zeros_like(dqg_ref)

        do = do_ref[...]
        df, dq = _rms_bwd(0.5 * do, f_ref[...], qg_ref[...])
        dqg_ref[...] += dq
        dfb = df.astype(BF16)
        df_ref[...] = dfb
        dxn = None
        for j in range(NSH):
            dhh = _dot_nt(dfb, wd_ref[j])
            av = a_ref[j].astype(F32)
            bv = b_ref[j].astype(F32)
            sg = jax.nn.sigmoid(av)
            sa = av * sg
            hh_ref[j] = (sa * bv).astype(BF16)
            dab = (dhh * bv * (sg + sa * (1.0 - sg))).astype(BF16)
            dbb = (dhh * sa).astype(BF16)
            da_ref[j] = dab
            db_ref[j] = dbb
            part = _dot(dab, wg_ref[j]) + _dot(dbb, wu_ref[j])
            dxn = part if dxn is None else dxn + part
        dx, dp = _rms_bwd(dxn, h_ref[...], pg_ref[...])
        dpg_ref[...] += dp
        dh_ref[...] = do + dx

    row = pl.BlockSpec((tm, D), lambda i: (i, 0))
    act = pl.BlockSpec((NSH, tm, DFS), lambda i: (0, i, 0))
    return _pc(body, name=f"ffn_bwd_{tag}_l{layer}", grid=(nt,),
               in_specs=[row, row, row, act, act, _gain_spec(D, layer), _gain_spec(D, layer)] + [_ffn_weight_spec()] * 3,
               out_specs=[row, row, act, act, act, _row_acc_spec(D), _row_acc_spec(D)],
               out_shape=[S((T, D), F32), S((T, D), BF16), S((NSH, T, DFS), BF16), S((NSH, T, DFS), BF16),
                          S((NSH, T, DFS), BF16), S((1, D), F32), S((1, D), F32)],
               compiler_params=_cp(1))(dout, h, f, a, b, pre_g, post_g, wg, wu, wd)


def _dw(A, B, buf, layer, kb, nb, a_mode, b_mode, name):
    T = A.shape[1]
    tt = TM
    nt = T // tt

    def pick(v, mode, j, w):
        if mode == "shard":
            return v[j]
        return v[0] if mode == "whole" else v[0][:, j * w:(j + 1) * w]

    def body(a_ref, b_ref, buf_ref, o_ref, acc):
        t = pl.program_id(0)

        @pl.when(t == 0)
        def _():
            acc[...] = jnp.zeros_like(acc)

        av = a_ref[...].astype(BF16)
        bv = b_ref[...].astype(BF16)
        for j in range(NSH):
            acc[j] += _dot_tn(pick(av, a_mode, j, kb), pick(bv, b_mode, j, nb))

        @pl.when(t == nt - 1)
        def _():
            o_ref[...] = acc[...].astype(o_ref.dtype)

    return _pc(body, name=name, grid=(nt,),
               in_specs=[pl.BlockSpec((A.shape[0], tt, A.shape[2]), lambda t: (0, t, 0)),
                         pl.BlockSpec((B.shape[0], tt, B.shape[2]), lambda t: (0, t, 0)),
                         pl.BlockSpec(memory_space=pl.ANY)],
               out_specs=pl.BlockSpec((NSH, None, kb, nb), lambda t: (0, layer, 0, 0)),
               out_shape=S(buf.shape, buf.dtype), input_output_aliases={2: 0},
               scratch_shapes=[pltpu.VMEM((NSH, kb, nb), F32)], compiler_params=_cp(1))(A, B, buf)


def _mix_proj(h, pre_g, win, rot, layer):
    T = h.shape[0]
    tm = TM

    def body(h_ref, g_ref, w_ref, c_ref, s1_ref, s2_ref, p_ref, xn_ref):
        xb = _rms_fwd(h_ref[...], g_ref[...]).astype(BF16)
        xn_ref[...] = xb
        for j in range(NSH):
            o = _dot(xb, w_ref[j])
            p_ref[j] = _rot_fwd(o, c_ref[...], s1_ref[...], s2_ref[...]) if j < 2 else o

    row = pl.BlockSpec((tm, D), lambda i: (i, 0))
    half = pl.BlockSpec((tm, DA), lambda i: (i, 0))
    return _pc(body, name=f"mix_proj_l{layer}", grid=(T // tm,),
               in_specs=[row, _gain_spec(D, layer), pl.BlockSpec((NSH, None, D, DA), lambda i: (0, 0, 0, 0)),
                         half, half, half],
               out_specs=[pl.BlockSpec((NSH, tm, DA), lambda i: (0, i, 0)), row],
               out_shape=[S((NSH, T, DA), F32), S((T, D), BF16)], compiler_params=_cp(1))(h, pre_g, win, *rot)


def _mix_proj_bwd(dq, dk, dv, du, dh_up, h, pre_g, win, rot, layer):
    T = h.shape[0]
    tm = TM

    def body(dq_ref, dk_ref, dv_ref, du_ref, up_ref, h_ref, g_ref, w_ref, c_ref, s1_ref, s2_ref,
             dh_ref, dp_ref, dg_ref):
        @pl.when(pl.program_id(0) == 0)
        def _():
            dg_ref[...] = jnp.zeros_like(dg_ref)

        rot = (c_ref[...], s1_ref[...], s2_ref[...])
        dps = [_rot_bwd(dq_ref[...], *rot), _rot_bwd(dk_ref[...], *rot), dv_ref[...], du_ref[...]]
        dxn = None
        for j in range(NSH):
            dpb = dps[j].astype(BF16)
            dp_ref[j] = dpb
            part = _dot_nt(dpb, w_ref[j])
            dxn = part if dxn is None else dxn + part
        dx, dg = _rms_bwd(dxn, h_ref[...], g_ref[...])
        dg_ref[...] += dg
        dh_ref[...] = up_ref[...] + dx

    row = pl.BlockSpec((tm, D), lambda i: (i, 0))
    half = pl.BlockSpec((tm, DA), lambda i: (i, 0))
    return _pc(body, name=f"mix_proj_bwd_l{layer}", grid=(T // tm,),
               in_specs=[half, half, half, half, row, row, _gain_spec(D, layer),
                         pl.BlockSpec((NSH, None, D, DA), lambda i: (0, 0, 0, 0)), half, half, half],
               out_specs=[row, pl.BlockSpec((NSH, tm, DA), lambda i: (0, i, 0)), _row_acc_spec(D)],
               out_shape=[S((T, D), F32), S((NSH, T, DA), BF16), S((1, D), F32)],
               compiler_params=_cp(1))(dq, dk, dv, du, dh_up, h, pre_g, win, *rot)


def _stream_pos(d, axis):
    i = lax.broadcasted_iota(jnp.int32, (BAND, BAND), axis)
    if d == 16:
        return i
    if d == 4:
        return 4 * (i % 32) + i // 32
    return 16 * (i % 8) + i // 8


def _band_masks(b, d):
    qi, kj = _stream_pos(d, 0), _stream_pos(d, 1)
    return kj <= qi, (kj >= qi) & (b > 0)


def _pattern(d, T):
    n16 = T // 16
    if d == 16:
        return (16, n16, DA), (None, BAND, DA), lambda r, k: (r, k, 0)
    if d == 4:
        return (4, 4, n16, DA), (4, None, 32, DA), lambda r, k: (0, r, k, 0)
    return (16, n16, DA), (16, 8, DA), lambda r, k: (0, k, 0)


def _pattern_spec(d, T, kmap, lead=None):
    _, blk, idx = _pattern(d, T)
    if lead is None:
        return pl.BlockSpec(blk, lambda r, b: idx(r, kmap(b)))
    return pl.BlockSpec((None,) + blk, lambda r, b: (lead,) + idx(r, kmap(b)))


def _whole_stream_specs(T, n_plain):
    n16 = T // 16
    p_spec = lambda s: pl.BlockSpec((None, None, n16, DA), lambda r: (s, r, 0, 0))
    plain = pl.BlockSpec((None, n16, DA), lambda r: (r, 0, 0))
    return [p_spec(0), p_spec(1), p_spec(2)] + [plain] * n_plain, plain


def _stream_masks():
    qi = lax.broadcasted_iota(jnp.int32, (BAND, BAND), 0)
    kj = lax.broadcasted_iota(jnp.int32, (BAND, BAND), 1)
    mask_c = kj <= qi
    return mask_c, jnp.concatenate([kj >= qi, mask_c], axis=1)


def _attn_fwd_stream(P, layer):
    T = P.shape[1]
    n16 = T // 16
    nb = n16 // BAND
    scale = HD ** -0.5

    def body(q_ref, k_ref, v_ref, o_ref, l_ref, qs, ks, vs):
        for src, dst in ((q_ref, qs), (k_ref, ks), (v_ref, vs)):
            dst[...] = src[...].astype(BF16)
        mask_c, mask_pc = _stream_masks()
        for b in range(nb):
            rows = slice(b * BAND, (b + 1) * BAND)
            krows = slice(max(b - 1, 0) * BAND, (b + 1) * BAND)
            mask = mask_c if b == 0 else mask_pc
            for hd in range(NH):
                sl = slice(hd * HD, (hd + 1) * HD)
                s = jnp.where(mask, _dot_nt(qs[rows, sl], ks[krows, sl]) * scale, -1e30)
                m = jnp.max(s, axis=-1, keepdims=True)
                e = jnp.exp(s - m)
                den = jnp.sum(e, axis=-1, keepdims=True)
                o_ref[rows, sl] = _dot(e.astype(BF16), vs[krows, sl]) / den
                l_ref[rows, sl] = jnp.broadcast_to(m + jnp.log(den), (BAND, HD))

    ins, out = _whole_stream_specs(T, 0)
    Pv = P.reshape(NSH, 16, n16, DA)
    o, l = _pc(body, name=f"attn_fwd_d16_l{layer}", grid=(16,), in_specs=ins, out_specs=[out, out],
               out_shape=[S((16, n16, DA), F32)] * 2, scratch_shapes=[pltpu.VMEM((n16, DA), BF16)] * 3,
               compiler_params=_cp(1))(Pv, Pv, Pv)
    return o.reshape(T, DA), l.reshape(T, DA)


def _attn_bwd_stream(P, dO, lse, delta, acc, layer):
    T = P.shape[1]
    n16 = T // 16
    nb = n16 // BAND
    scale = HD ** -0.5
    first = acc is None

    def body(*refs):
        q_ref, k_ref, v_ref, do_ref, l_ref, dl_ref = refs[:6]
        if first:
            dq_ref, dk_ref, dv_ref = refs[6:9]
        else:
            aq_ref, ak_ref, av_ref, dq_ref, dk_ref, dv_ref = refs[6:12]
        qs, ks, vs, dos, okf, ovf = refs[-6:]
        for src, dst in ((q_ref, qs), (k_ref, ks), (v_ref, vs), (do_ref, dos)):
            dst[...] = src[...].astype(BF16)
        okf[...] = jnp.zeros_like(okf)
        ovf[...] = jnp.zeros_like(ovf)
        mask_c, mask_pc = _stream_masks()
        for b in range(nb):
            rows = slice(b * BAND, (b + 1) * BAND)
            krows = slice(max(b - 1, 0) * BAND, (b + 1) * BAND)
            mask = mask_c if b == 0 else mask_pc
            for hd in range(NH):
                sl = slice(hd * HD, (hd + 1) * HD)
                one = slice(hd * HD, hd * HD + 1)
                q, do, kk = qs[rows, sl], dos[rows, sl], ks[krows, sl]
                p = jnp.where(mask, jnp.exp(_dot_nt(q, kk) * scale - l_ref[rows, one]), 0.0)
                ds = (p * (_dot_nt(do, vs[krows, sl]) - dl_ref[rows, one]) * scale).astype(BF16)
                dq = _dot(ds, kk)
                dq_ref[rows, sl] = dq if first else aq_ref[rows, sl] + dq
                okf[krows, sl] += _dot_tn(ds, q)
                ovf[krows, sl] += _dot_tn(p.astype(BF16), do)
        dk_ref[...] = okf[...] if first else ak_ref[...] + okf[...]
        dv_ref[...] = ovf[...] if first else av_ref[...] + ovf[...]

    ins, out = _whole_stream_specs(T, 3 if first else 6)
    Pv = P.reshape(NSH, 16, n16, DA)
    view = lambda t: t.reshape(16, n16, DA)
    args = [Pv, Pv, Pv, view(dO), view(lse), view(delta)] + ([] if first else [view(t) for t in acc])
    dq, dk, dv = _pc(body, name=f"attn_bwd_d16_l{layer}", grid=(16,), in_specs=ins, out_specs=[out, out, out],
                     out_shape=[S((16, n16, DA), F32)] * 3,
                     scratch_shapes=[pltpu.VMEM((n16, DA), BF16)] * 4 + [pltpu.VMEM((n16, DA), F32)] * 2,
                     compiler_params=_cp(1))(*args)
    return dq.reshape(T, DA), dk.reshape(T, DA), dv.reshape(T, DA)


def _attn_fwd(P, d, layer):
    if d == 16:
        return _attn_fwd_stream(P, layer)
    T = P.shape[1]
    nb = T // d // BAND
    vshape = _pattern(d, T)[0]
    Pv = P.reshape((NSH,) + vshape)
    scale = HD ** -0.5

    def body(q_ref, kp_ref, kc_ref, vp_ref, vc_ref, o_ref, l_ref, qs, ks, vs, osc, lsc):
        b = pl.program_id(1)
        flat = lambda ref: ref[...].reshape(BAND, DA).astype(BF16)
        qs[...] = flat(q_ref)
        ks[0:BAND, :] = flat(kp_ref)
        ks[BAND:, :] = flat(kc_ref)
        vs[0:BAND, :] = flat(vp_ref)
        vs[BAND:, :] = flat(vc_ref)
        mask_c, mask_p = _band_masks(b, d)
        mask = jnp.concatenate([mask_p, mask_c], axis=1)
        for hd in range(NH):
            sl = slice(hd * HD, (hd + 1) * HD)
            s = jnp.where(mask, _dot_nt(qs[:, sl], ks[:, sl]) * scale, -1e30)
            m = jnp.max(s, axis=-1, keepdims=True)
            e = jnp.exp(s - m)
            den = jnp.sum(e, axis=-1, keepdims=True)
            osc[:, sl] = _dot(e.astype(BF16), vs[:, sl]) / den
            lsc[:, sl] = jnp.broadcast_to(m + jnp.log(den), (BAND, HD))
        o_ref[...] = osc[...].reshape(o_ref.shape)
        l_ref[...] = lsc[...].reshape(l_ref.shape)

    cur = lambda b: b
    prev = lambda b: jnp.maximum(b - 1, 0)
    out = _pattern_spec(d, T, cur)
    o, l = _pc(body, name=f"attn_fwd_d{d}_l{layer}", grid=(d, nb),
               in_specs=[_pattern_spec(d, T, cur, 0), _pattern_spec(d, T, prev, 1), _pattern_spec(d, T, cur, 1),
                         _pattern_spec(d, T, prev, 2), _pattern_spec(d, T, cur, 2)],
               out_specs=[out, out], out_shape=[S(vshape, F32)] * 2,
               scratch_shapes=[pltpu.VMEM((BAND, DA), BF16)] + [pltpu.VMEM((2 * BAND, DA), BF16)] * 2
               + [pltpu.VMEM((BAND, DA), F32)] * 2,
               compiler_params=_cp(2))(Pv, Pv, Pv, Pv, Pv)
    return o.reshape(T, DA), l.reshape(T, DA)


def _attn_bwd(P, dO, lse, delta, acc, d, layer):
    if d == 16:
        return _attn_bwd_stream(P, dO, lse, delta, acc, layer)
    T = P.shape[1]
    nb = T // d // BAND
    vshape = _pattern(d, T)[0]
    Pv = P.reshape((NSH,) + vshape)
    scale = HD ** -0.5
    first = acc is None

    def body(*refs):
        q_ref, kp_ref, kc_ref, vp_ref, vc_ref, do_ref, l_ref, dl_ref = refs[:8]
        if first:
            dq_ref, dk_ref, dv_ref = refs[8:11]
        else:
            aq_ref, ak_ref, av_ref, dq_ref, dk_ref, dv_ref = refs[8:14]
        qs, dos, ks, vs, ls, dls, oq, ok, ov, ck, cv = refs[-11:]
        b = pl.program_id(1)
        flat = lambda ref: ref[...].reshape(BAND, DA)

        @pl.when(b == 0)
        def _():
            ck[...] = jnp.zeros_like(ck)
            cv[...] = jnp.zeros_like(cv)

        @pl.when(b < nb)
        def _():
            qs[...] = flat(q_ref).astype(BF16)
            dos[...] = flat(do_ref).astype(BF16)
            ks[0:BAND, :] = flat(kp_ref).astype(BF16)
            ks[BAND:, :] = flat(kc_ref).astype(BF16)
            vs[0:BAND, :] = flat(vp_ref).astype(BF16)
            vs[BAND:, :] = flat(vc_ref).astype(BF16)
            ls[...] = flat(l_ref)
            dls[...] = flat(dl_ref)
            mask_c, mask_p = _band_masks(b, d)
            mask = jnp.concatenate([mask_p, mask_c], axis=1)
            for hd in range(NH):
                sl = slice(hd * HD, (hd + 1) * HD)
                one = slice(hd * HD, hd * HD + 1)
                q, do, kk = qs[:, sl], dos[:, sl], ks[:, sl]
                p = jnp.where(mask, jnp.exp(_dot_nt(q, kk) * scale - ls[:, one]), 0.0)
                ds = (p * (_dot_nt(do, vs[:, sl]) - dls[:, one]) * scale).astype(BF16)
                oq[:, sl] = _dot(ds, kk)
                dk2 = _dot_tn(ds, q)
                dv2 = _dot_tn(p.astype(BF16), do)
                ok[:, sl] = ck[:, sl] + dk2[0:BAND]
                ov[:, sl] = cv[:, sl] + dv2[0:BAND]
                ck[:, sl] = dk2[BAND:]
                cv[:, sl] = dv2[BAND:]
            if first:
                dq_ref[...] = oq[...].reshape(dq_ref.shape)
                dk_ref[...] = ok[...].reshape(dk_ref.shape)
                dv_ref[...] = ov[...].reshape(dv_ref.shape)
            else:
                dq_ref[...] = aq_ref[...] + oq[...].reshape(dq_ref.shape)
                dk_ref[...] = ak_ref[...] + ok[...].reshape(dk_ref.shape)
                dv_ref[...] = av_ref[...] + ov[...].reshape(dv_ref.shape)

        @pl.when(b == nb)
        def _():
            if first:
                dk_ref[...] = ck[...].reshape(dk_ref.shape)
                dv_ref[...] = cv[...].reshape(dv_ref.shape)
            else:
                dk_ref[...] = ak_ref[...] + ck[...].reshape(dk_ref.shape)
                dv_ref[...] = av_ref[...] + cv[...].reshape(dv_ref.shape)

    qb = lambda b: jnp.minimum(b, nb - 1)
    qprev = lambda b: jnp.maximum(qb(b) - 1, 0)
    kb = lambda b: jnp.maximum(b - 1, 0)
    qrow = _pattern_spec(d, T, qb)
    krow = _pattern_spec(d, T, kb)
    view = lambda t: t.reshape(vshape)
    ins = [Pv, Pv, Pv, Pv, Pv, view(dO), view(lse), view(delta)]
    specs = [_pattern_spec(d, T, qb, 0), _pattern_spec(d, T, qprev, 1), _pattern_spec(d, T, qb, 1),
             _pattern_spec(d, T, qprev, 2), _pattern_spec(d, T, qb, 2), qrow, qrow, qrow]
    if not first:
        ins += [view(t) for t in acc]
        specs += [qrow, krow, krow]
    dq, dk, dv = _pc(body, name=f"attn_bwd_d{d}_l{layer}", grid=(d, nb + 1), in_specs=specs,
                     out_specs=[qrow, krow, krow], out_shape=[S(vshape, F32)] * 3,
                     scratch_shapes=[pltpu.VMEM((BAND, DA), BF16)] * 2 + [pltpu.VMEM((2 * BAND, DA), BF16)] * 2
                     + [pltpu.VMEM((BAND, DA), F32)] * 7,
                     compiler_params=_cp(2))(*ins)
    return dq.reshape(T, DA), dk.reshape(T, DA), dv.reshape(T, DA)


def _ssm_prep(lam_re, lam_im, log_dt, b_re, b_im, c_re, c_im):
    dt = jnp.exp(log_dt)[:, None]
    er = jnp.exp(lam_re * dt)
    a_re = er * jnp.cos(lam_im * dt)
    a_im = er * jnp.sin(lam_im * dt)
    nr, ni = a_re - 1.0, a_im
    den = lam_re * lam_re + lam_im * lam_im
    cr = (nr * lam_re + ni * lam_im) / den
    ci = (ni * lam_re - nr * lam_im) / den
    bbr = cr[..., None] * b_re - ci[..., None] * b_im
    bbi = cr[..., None] * b_im + ci[..., None] * b_re
    eye = jnp.eye(8, dtype=F32)

    def bblock(bb):
        t = bb.reshape(4, 8, 64, 16).transpose(0, 1, 3, 2)
        return (t[:, :, :, None, :] * eye[None, :, None, :, None]).reshape(4, 128, 512)

    def cblock(cc):
        t = cc.reshape(4, 8, 16, 64).transpose(0, 1, 3, 2)
        return (t[:, :, :, None, :] * eye[None, :, None, :, None]).reshape(4, 512, 128)

    return (a_re.reshape(NLB, 1, 128), a_im.reshape(NLB, 1, 128), bblock(bbr), bblock(bbi), cblock(c_re), cblock(c_im))


def _perm_matrix(tm):
    n = tm // 16
    pm = np.zeros((tm, tm), np.float32)
    for r in range(16):
        pm[16 * np.arange(n) + r, r * n + np.arange(n)] = 1.0
    return jnp.asarray(pm, BF16)


def _pieces(x):
    p1 = x.astype(BF16)
    r1 = x - p1.astype(F32)
    p2 = r1.astype(BF16)
    return p1, p2, (r1 - p2.astype(F32)).astype(BF16)


def _to_time(x, pm):
    return sum(_dot(pm, p) for p in _pieces(x))


def _to_streams(x, pm):
    return sum(_dot_tn(pm, p) for p in _pieces(x))


def _stream_block(tm, cols, lead=None):
    if lead is None:
        return pl.BlockSpec((16, tm // 16, cols), lambda i: (0, i, 0))
    return pl.BlockSpec((None, 16, tm // 16, cols), lambda i: (lead, 0, i, 0))


def _reorder(t3, to_streams, name):
    B, T, C = t3.shape
    tm = TM

    def body(x_ref, pm_ref, o_ref):
        if to_streams:
            o_ref[...] = _to_streams(x_ref[...], pm_ref[...]).reshape(o_ref.shape)
        else:
            o_ref[...] = _to_time(x_ref[...].reshape(tm, C), pm_ref[...])

    time_blk = pl.BlockSpec((None, tm, C), lambda b, i: (b, i, 0))
    stream_blk = pl.BlockSpec((None, 16, tm // 16, C), lambda b, i: (b, 0, i, 0))
    src = t3 if to_streams else t3.reshape(B, 16, T // 16, C)
    out = _pc(body, name=name, grid=(B, T // tm),
              in_specs=[time_blk if to_streams else stream_blk, pl.BlockSpec((tm, tm), lambda b, i: (0, 0))],
              out_specs=stream_blk if to_streams else time_blk,
              out_shape=S((B, 16, T // 16, C) if to_streams else (B, T, C), F32),
              compiler_params=_cp(2))(src, _perm_matrix(tm))
    return out.reshape(B, T, C)


def _ssm_in(P, bre, bim, layer):
    T = P.shape[1]
    tm = TM

    def body(u_ref, pm_ref, br_ref, bi_ref, un_ref, or_ref, oi_ref):
        u = _to_time(u_ref[...].reshape(tm, DSS), pm_ref[...])
        un_ref[...] = u
        for s in range(4):
            uc = u[:, s * 128:(s + 1) * 128]
            r = _dot3(_dot, uc, br_ref[s])
            m = _dot3(_dot, uc, bi_ref[s])
            for q in range(4):
                or_ref[4 * s + q] = r[:, q * 128:(q + 1) * 128]
                oi_ref[4 * s + q] = m[:, q * 128:(q + 1) * 128]

    whole = pl.BlockSpec((4, 128, 512), lambda i: (0, 0, 0))
    st = pl.BlockSpec((NLB, tm, 128), lambda i: (0, i, 0))
    return _pc(body, name=f"ssm_in_l{layer}", grid=(T // tm,),
               in_specs=[_stream_block(tm, DSS, 3), pl.BlockSpec((tm, tm), lambda i: (0, 0)), whole, whole],
               out_specs=[pl.BlockSpec((tm, DSS), lambda i: (i, 0)), st, st],
               out_shape=[S((T, DSS), F32)] + [S((NLB, T, 128), F32)] * 2,
               compiler_params=_cp(1))(P.reshape(NSH, 16, T // 16, DSS), _perm_matrix(tm), bre, bim)


def _scan(br, bi, a_re, a_im, reverse, layer):
    T = br.shape[1]
    nbk = 4
    tt = min(T, 1024)
    nT = T // tt
    ntile = tt // 8
    sgn = -1.0 if reverse else 1.0
    last = 0 if reverse else 7

    def body(br_ref, bi_ref, ar_ref, ai_ref, xr_ref, xi_ref, cr, ci):
        @pl.when(pl.program_id(1) == 0)
        def _():
            cr[...] = jnp.zeros_like(cr)
            ci[...] = jnp.zeros_like(ci)

        row = lax.broadcasted_iota(jnp.int32, (8, 128), 0)
        consts = []
        for k in range(nbk):
            a1r = jnp.broadcast_to(ar_ref[k], (8, 128))
            a1i = sgn * jnp.broadcast_to(ai_ref[k], (8, 128))
            pows = [(a1r, a1i)]
            for _ in range(7):
                pr, pi_ = pows[-1]
                pows.append((a1r * pr - a1i * pi_, a1r * pi_ + a1i * pr))
            rounds = []
            for s in (1, 2, 4):
                inside = (row <= 7 - s) if reverse else (row >= s)
                rounds.append((jnp.where(inside, pows[s - 1][0], 0.0), jnp.where(inside, pows[s - 1][1], 0.0)))
            cmr, cmi = jnp.zeros((8, 128), F32), jnp.zeros((8, 128), F32)
            for r in range(8):
                e = (7 - r) if reverse else r
                cmr = jnp.where(row == r, pows[e][0], cmr)
                cmi = jnp.where(row == r, pows[e][1], cmi)
            consts.append((rounds, cmr, cmi))

        def tile(i, carry):
            j = (ntile - 1 - i) if reverse else i
            rows = pl.ds(pl.multiple_of(j * 8, 8), 8)
            out = []
            for k in range(nbk):
                rounds, cmr, cmi = consts[k]
                xr = br_ref[k, rows, :]
                xi = bi_ref[k, rows, :]
                for (mr, mi), s in zip(rounds, (1, 2, 4)):
                    sh = (8 - s) if reverse else s
                    rr = pltpu.roll(xr, sh, 0)
                    ri = pltpu.roll(xi, sh, 0)
                    xr, xi = xr + (mr * rr - mi * ri), xi + (mr * ri + mi * rr)
                c_r, c_i = carry[k]
                xr, xi = xr + (cmr * c_r - cmi * c_i), xi + (cmr * c_i + cmi * c_r)
                xr_ref[k, rows, :] = xr
                xi_ref[k, rows, :] = xi
                out.append((jnp.broadcast_to(xr[last:last + 1, :], (8, 128)),
                            jnp.broadcast_to(xi[last:last + 1, :], (8, 128))))
            return tuple(out)

        carry = lax.fori_loop(0, ntile, tile, tuple((cr[k], ci[k]) for k in range(nbk)), unroll=2)
        for k in range(nbk):
            cr[k] = carry[k][0]
            ci[k] = carry[k][1]

    tmap = (lambda t: nT - 1 - t) if reverse else (lambda t: t)
    st = pl.BlockSpec((nbk, tt, 128), lambda i, t: (i, tmap(t), 0))
    av = pl.BlockSpec((nbk, 1, 128), lambda i, t: (i, 0, 0))
    return _pc(body, name=f"scan_{'bwd' if reverse else 'fwd'}_l{layer}", grid=(NLB // nbk, nT),
               in_specs=[st, st, av, av], out_specs=[st, st], out_shape=[S((NLB, T, 128), F32)] * 2,
               scratch_shapes=[pltpu.VMEM((nbk, 8, 128), F32)] * 2, compiler_params=_cp(2))(br, bi, a_re, a_im)


def _ssm_out(xr, xi, u, cre, cim, dvec, wglu, bglu, layer):
    T = u.shape[0]
    tm = TM

    def body(xr_ref, xi_ref, u_ref, pm_ref, cr_ref, ci_ref, d_ref, w_ref, bg_ref, s_ref, y_ref, z_ref):
        ys = []
        for s in range(4):
            xrc = jnp.concatenate([xr_ref[4 * s + q] for q in range(4)], axis=1)
            xic = jnp.concatenate([xi_ref[4 * s + q] for q in range(4)], axis=1)
            ys.append(_dot3(_dot, xrc, cr_ref[s]) - _dot3(_dot, xic, ci_ref[s]))
        y = jnp.concatenate(ys, axis=1) + d_ref[...] * u_ref[...]
        yg = _gelu(y)
        ygb = yg.astype(BF16)
        z = bg_ref[...] + sum(_dot(ygb[:, j * 128:(j + 1) * 128], w_ref[j]) for j in range(NSH))
        y_ref[...] = y
        z_ref[...] = z
        s_ref[...] = _to_streams(yg * jax.nn.sigmoid(z), pm_ref[...]).reshape(s_ref.shape)

    st = pl.BlockSpec((NLB, tm, 128), lambda i: (0, i, 0))
    cw = pl.BlockSpec((4, 512, 128), lambda i: (0, 0, 0))
    half = pl.BlockSpec((tm, DSS), lambda i: (i, 0))
    s, y, z = _pc(body, name=f"ssm_out_l{layer}", grid=(T // tm,),
                  in_specs=[st, st, half, pl.BlockSpec((tm, tm), lambda i: (0, 0)), cw, cw, _gain_spec(DSS, layer),
                            pl.BlockSpec((NSH, None, 128, DSS), lambda i: (0, 0, 0, 0)), _gain_spec(DSS, layer)],
                  out_specs=[_stream_block(tm, DSS), half, half],
                  out_shape=[S((16, T // 16, DSS), F32), S((T, DSS), F32), S((T, DSS), F32)],
                  compiler_params=_cp(1))(xr, xi, u, _perm_matrix(tm), cre, cim, dvec, wglu, bglu)
    return s.reshape(T, DSS), y, z


def _ssm_out_bwd(dssm, y, z, xr, xi, u, cre, cim, dvec, wglu, layer):
    T = u.shape[0]
    tm = TM

    def body(ds_ref, pm_ref, y_ref, z_ref, xr_ref, xi_ref, u_ref, cr_ref, ci_ref, d_ref, w_ref,
             gr_ref, gi_ref, du_ref, dz_ref, yg_ref, dbg_ref, dd_ref, dcr_ref, dci_ref):
        i = pl.program_id(0)

        @pl.when(i == 0)
        def _():
            dbg_ref[...] = jnp.zeros_like(dbg_ref)
            dd_ref[...] = jnp.zeros_like(dd_ref)
            dcr_ref[...] = jnp.zeros_like(dcr_ref)
            dci_ref[...] = jnp.zeros_like(dci_ref)

        yv = y_ref[...]
        yg = _gelu(yv)
        sg = jax.nn.sigmoid(z_ref[...])
        ds = _to_time(ds_ref[...].reshape(tm, DSS), pm_ref[...])
        dz = ds * yg * sg * (1.0 - sg)
        dzb = dz.astype(BF16)
        dz_ref[...] = dzb
        yg_ref[...] = yg.astype(BF16)
        dbg_ref[...] += jnp.sum(dz, axis=0, keepdims=True)
        dyg = ds * sg + jnp.concatenate([_dot_nt(dzb, w_ref[j]) for j in range(NSH)], axis=1)
        dy = dyg * _gelu_grad(yv)
        u = u_ref[...]
        dd_ref[...] += jnp.sum(dy * u, axis=0, keepdims=True)
        du_ref[...] = dy * d_ref[...]
        for s in range(4):
            dyc = dy[:, s * 128:(s + 1) * 128]
            g_r = _dot3(_dot_nt, dyc, cr_ref[s])
            g_i = -_dot3(_dot_nt, dyc, ci_ref[s])
            for q in range(4):
                gr_ref[4 * s + q] = g_r[:, q * 128:(q + 1) * 128]
                gi_ref[4 * s + q] = g_i[:, q * 128:(q + 1) * 128]
            xrc = jnp.concatenate([xr_ref[4 * s + q] for q in range(4)], axis=1)
            xic = jnp.concatenate([xi_ref[4 * s + q] for q in range(4)], axis=1)
            dcr_ref[s] += _dot3(_dot_tn, xrc, dyc)
            dci_ref[s] -= _dot3(_dot_tn, xic, dyc)

    st = pl.BlockSpec((NLB, tm, 128), lambda i: (0, i, 0))
    cw = pl.BlockSpec((4, 512, 128), lambda i: (0, 0, 0))
    half = pl.BlockSpec((tm, DSS), lambda i: (i, 0))
    return _pc(body, name=f"ssm_out_bwd_l{layer}", grid=(T // tm,),
               in_specs=[_stream_block(tm, DSS), pl.BlockSpec((tm, tm), lambda i: (0, 0)), half, half, st, st, half,
                         cw, cw, _gain_spec(DSS, layer), pl.BlockSpec((NSH, None, 128, DSS), lambda i: (0, 0, 0, 0))],
               out_specs=[st, st, half, half, half, _row_acc_spec(DSS), _row_acc_spec(DSS), cw, cw],
               out_shape=[S((NLB, T, 128), F32)] * 2 + [S((T, DSS), F32), S((T, DSS), BF16), S((T, DSS), BF16),
                                                        S((1, DSS), F32), S((1, DSS), F32),
                                                        S((4, 512, 128), F32), S((4, 512, 128), F32)],
               compiler_params=_cp(1))(dssm.reshape(16, T // 16, DSS), _perm_matrix(tm), y, z, xr, xi, u, cre, cim,
                                       dvec, wglu)


def _ssm_da(gr, gi, xr, xi, layer):
    T = gr.shape[1]
    tb = 4096 if T % 4096 == 0 else T

    def body(gr_ref, gi_ref, xr_ref, xi_ref, dr_ref, di_ref, lr, li):
        t = pl.program_id(1)

        @pl.when(t == 0)
        def _():
            dr_ref[...] = jnp.zeros_like(dr_ref)
            di_ref[...] = jnp.zeros_like(di_ref)
            lr[...] = jnp.zeros_like(lr)
            li[...] = jnp.zeros_like(li)

        g_r, g_i, x_r, x_i = gr_ref[...], gi_ref[...], xr_ref[...], xi_ref[...]
        pr = pltpu.roll(x_r, 1, 0)
        pi_ = pltpu.roll(x_i, 1, 0)
        g0r, g0i = g_r[0:1, :], g_i[0:1, :]
        fr = lr[7:8, :] - x_r[tb - 1:tb, :]
        fi = li[7:8, :] - x_i[tb - 1:tb, :]
        dr_ref[...] += jnp.sum(g_r * pr + g_i * pi_, axis=0, keepdims=True) + g0r * fr + g0i * fi
        di_ref[...] += jnp.sum(g_i * pr - g_r * pi_, axis=0, keepdims=True) + g0i * fr - g0r * fi
        lr[...] = x_r[tb - 8:tb, :]
        li[...] = x_i[tb - 8:tb, :]

    st = pl.BlockSpec((None, tb, 128), lambda k, t: (k, t, 0))
    out = pl.BlockSpec((None, 1, 128), lambda k, t: (k, 0, 0))
    return _pc(body, name=f"ssm_da_l{layer}", grid=(NLB, T // tb), in_specs=[st] * 4, out_specs=[out, out],
               out_shape=[S((NLB, 1, 128), F32)] * 2, scratch_shapes=[pltpu.VMEM((8, 128), F32)] * 2,
               compiler_params=_cp(2))(gr, gi, xr, xi)


def _ssm_in_bwd(gr, gi, u, bre, bim, du_direct, layer):
    T = u.shape[0]
    tm = TM

    def body(gr_ref, gi_ref, u_ref, pm_ref, br_ref, bi_ref, dd_ref, du_ref, dbr_ref, dbi_ref):
        i = pl.program_id(0)

        @pl.when(i == 0)
        def _():
            dbr_ref[...] = jnp.zeros_like(dbr_ref)
            dbi_ref[...] = jnp.zeros_like(dbi_ref)

        dus = []
        for s in range(4):
            grc = jnp.concatenate([gr_ref[4 * s + q] for q in range(4)], axis=1)
            gic = jnp.concatenate([gi_ref[4 * s + q] for q in range(4)], axis=1)
            uc = u_ref[:, s * 128:(s + 1) * 128]
            dus.append(_dot3(_dot_nt, grc, br_ref[s]) + _dot3(_dot_nt, gic, bi_ref[s]))
            dbr_ref[s] += _dot3(_dot_tn, uc, grc)
            dbi_ref[s] += _dot3(_dot_tn, uc, gic)
        du = jnp.concatenate(dus, axis=1) + dd_ref[...]
        du_ref[...] = _to_streams(du, pm_ref[...]).reshape(du_ref.shape)

    whole = pl.BlockSpec((4, 128, 512), lambda i: (0, 0, 0))
    st = pl.BlockSpec((NLB, tm, 128), lambda i: (0, i, 0))
    half = pl.BlockSpec((tm, DSS), lambda i: (i, 0))
    du, dbr, dbi = _pc(body, name=f"ssm_in_bwd_l{layer}", grid=(T // tm,),
                       in_specs=[st, st, half, pl.BlockSpec((tm, tm), lambda i: (0, 0)), whole, whole, half],
                       out_specs=[_stream_block(tm, DSS), whole, whole],
                       out_shape=[S((16, T // 16, DSS), F32), S((4, 128, 512), F32), S((4, 128, 512), F32)],
                       compiler_params=_cp(1))(gr, gi, u, _perm_matrix(tm), bre, bim, du_direct)
    return du.reshape(T, DSS), dbr, dbi


def _mix_out(outs, lses, ssm, h, attn_g, ssm_g, post_g, wout, layer):
    T = h.shape[0]
    tm = TM

    def body(o1, o2, o3, l1, l2, l3, s_ref, h_ref, ag_ref, sg_ref, pg_ref, w_ref, ho_ref, at_ref, ls_ref, mx_ref, mo_ref):
        la, lb, lc = l1[...], l2[...], l3[...]
        m = jnp.maximum(jnp.maximum(la, lb), lc)
        wa, wb, wc = jnp.exp(la - m), jnp.exp(lb - m), jnp.exp(lc - m)
        zs = wa + wb + wc
        attn = (wa * o1[...] + wb * o2[...] + wc * o3[...]) / zs
        at_ref[...] = attn
        ls_ref[...] = m + jnp.log(zs)
        mixed = jnp.concatenate([_rms_fwd(attn, ag_ref[...]), _rms_fwd(s_ref[...], sg_ref[...])], axis=1).astype(BF16)
        mx_ref[...] = mixed
        mo = sum(_dot(mixed[:, j * 256:(j + 1) * 256], w_ref[j]) for j in range(NSH))
        mo_ref[...] = mo
        ho_ref[...] = h_ref[...] + _rms_fwd(mo, pg_ref[...])

    row = pl.BlockSpec((tm, D), lambda i: (i, 0))
    half = pl.BlockSpec((tm, DA), lambda i: (i, 0))
    return _pc(body, name=f"mix_out_l{layer}", grid=(T // tm,),
               in_specs=[half] * 7 + [row, _gain_spec(DA, layer), _gain_spec(DSS, layer), _gain_spec(D, layer),
                                      pl.BlockSpec((NSH, None, 256, D), lambda i: (0, 0, 0, 0))],
               out_specs=[row, half, half, row, row],
               out_shape=[S((T, D), F32), S((T, DA), F32), S((T, DA), F32), S((T, D), BF16), S((T, D), F32)],
               compiler_params=_cp(1))(*outs, *lses, ssm, h, attn_g, ssm_g, post_g, wout)


def _mix_out_bwd(dout, mo, attn, ssm, attn_g, ssm_g, post_g, wout, layer):
    T = dout.shape[0]
    tm = TM
    head_sum =jnp.asarray(np.kron(np.eye(NH, dtype=np.float32), np.ones((HD, HD), np.float32)), BF16)

    def body(do_ref, mo_ref, at_ref, s_ref, ag_ref, sg_ref, pg_ref, w_ref, e_ref,
             da_ref, ds_ref, dl_ref, dmo_ref, dpg_ref, dag_ref, dsg_ref):
        i = pl.program_id(0)

        @pl.when(i == 0)
        def _():
            dpg_ref[...] = jnp.zeros_like(dpg_ref)
            dag_ref[...] = jnp.zeros_like(dag_ref)
            dsg_ref[...] = jnp.zeros_like(dsg_ref)

        dmo, dpg = _rms_bwd(do_ref[...], mo_ref[...], pg_ref[...])
        dpg_ref[...] += dpg
        dmob = dmo.astype(BF16)
        dmo_ref[...] = dmob
        dmix = jnp.concatenate([_dot_nt(dmob, w_ref[j]) for j in range(NSH)], axis=1)
        attn = at_ref[...]
        dat, dag = _rms_bwd(dmix[:, :DA], attn, ag_ref[...])
        dss, dsg = _rms_bwd(dmix[:, DA:], s_ref[...], sg_ref[...])
        dag_ref[...] += dag
        dsg_ref[...] += dsg
        da_ref[...] = dat
        ds_ref[...] = dss
        prod = dat * attn
        p1 = prod.astype(BF16)
        r1 = prod - p1.astype(F32)
        p2 = r1.astype(BF16)
        p3 = (r1 - p2.astype(F32)).astype(BF16)
        e = e_ref[...]
        dl_ref[...] = _dot(p1, e) + _dot(p2, e) + _dot(p3, e)

    row = pl.BlockSpec((tm, D), lambda i: (i, 0))
    half = pl.BlockSpec((tm, DA), lambda i: (i, 0))
    return _pc(body, name=f"mix_out_bwd_l{layer}", grid=(T // tm,),
               in_specs=[row, row, half, half, _gain_spec(DA, layer), _gain_spec(DSS, layer), _gain_spec(D, layer),
                         pl.BlockSpec((NSH, None, 256, D), lambda i: (0, 0, 0, 0)),
                         pl.BlockSpec((DA, DA), lambda i: (0, 0))],
               out_specs=[half, half, half, row, _row_acc_spec(D), _row_acc_spec(DA), _row_acc_spec(DSS)],
               out_shape=[S((T, DA), F32)] * 3 + [S((T, D), BF16), S((1, D), F32), S((1, DA), F32), S((1, DSS), F32)],
               compiler_params=_cp(1))(dout, mo, attn, ssm, attn_g, ssm_g, post_g, wout, head_sum)


def _ple_fwd(h, p3, wup, wgate, post_g, layer):
    T = h.shape[0]
    tm = TM

    def body(h_ref, p_ref, wu_ref, wg_ref, g_ref, ho_ref, e_ref, gt_ref):
        hv = h_ref[...]
        hb = hv.astype(BF16)
        pb = p_ref[...].astype(BF16)
        gte = sum(_dot(hb[:, j * 256:(j + 1) * 256], wg_ref[j]) for j in range(NSH))
        e = jnp.concatenate([_dot(pb, wu_ref[j]) for j in range(NSH)], axis=1)
        e_ref[...] = e
        gt_ref[...] = gte
        ho_ref[...] = hv + _rms_fwd(e * jax.nn.sigmoid(gte), g_ref[...])

    row = pl.BlockSpec((tm, D), lambda i: (i, 0))
    return _pc(body, name=f"ple_fwd_l{layer}", grid=(T // tm,),
               in_specs=[row, pl.BlockSpec((None, tm, PLE), lambda i: (layer, i, 0)),
                         pl.BlockSpec((NSH, None, PLE, 256), lambda i: (0, 0, 0, 0)),
                         pl.BlockSpec((NSH, None, 256, D), lambda i: (0, 0, 0, 0)), _gain_spec(D, layer)],
               out_specs=[row, row, row], out_shape=[S((T, D), F32)] * 3,
               compiler_params=_cp(1))(h, p3, wup, wgate, post_g)


def _ple_bwd(dout, e, gte, wgate, post_g, layer):
    T = dout.shape[0]
    tm = TM

    def body(do_ref, e_ref, gt_ref, wg_ref, g_ref, dh_ref, de_ref, dgt_ref, dg_ref):
        i = pl.program_id(0)

        @pl.when(i == 0)
        def _():
            dg_ref[...] = jnp.zeros_like(dg_ref)

        ev = e_ref[...]
        sg = jax.nn.sigmoid(gt_ref[...])
        do = do_ref[...]
        dple, dg = _rms_bwd(do, ev * sg, g_ref[...])
        dg_ref[...] += dg
        de = (dple * sg).astype(BF16)
        for j in range(NSH):
            de_ref[j] = de[:, j * 256:(j + 1) * 256]
        dgb = (dple * ev * sg * (1.0 - sg)).astype(BF16)
        dgt_ref[...] = dgb
        dh_ref[...] = do + jnp.concatenate([_dot_nt(dgb, wg_ref[j]) for j in range(NSH)], axis=1)

    row = pl.BlockSpec((tm, D), lambda i: (i, 0))
    return _pc(body, name=f"ple_bwd_l{layer}", grid=(T // tm,),
               in_specs=[row, row, row, pl.BlockSpec((NSH, None, 256, D), lambda i: (0, 0, 0, 0)), _gain_spec(D, layer)],
               out_specs=[row, pl.BlockSpec((NSH, tm, 256), lambda i: (0, i, 0)), row, _row_acc_spec(D)],
               out_shape=[S((T, D), F32), S((NSH, T, 256), BF16), S((T, D), BF16), S((1, D), F32)],
               compiler_params=_cp(1))(dout, e, gte, wgate, post_g)


def _loss_head(h, target):
    T = h.shape[0]
    tm = TM

    def body(h_ref, t_ref, dy_ref, l_ref):
        i = pl.program_id(0)

        @pl.when(i == 0)
        def _():
            l_ref[...] = jnp.zeros_like(l_ref)

        err = h_ref[...] - t_ref[...]
        dy_ref[...] = err * (1.0 / D)
        l_ref[...] += jnp.broadcast_to((0.5 / D) * jnp.sum(err * err), (1, 128))

    row = pl.BlockSpec((tm, D), lambda i: (i, 0))
    return _pc(body, name="loss_head", grid=(T // tm,), in_specs=[row, row],
               out_specs=[row, pl.BlockSpec((1, 128), lambda i: (0, 0))],
               out_shape=[S((T, D), F32), S((1, 128), F32)], compiler_params=_cp(1))(h, target)


def _local_step(x, p3, pos_col, target, weights_of, layer_grads_done, Sm):
    L = p3.shape[0]
    g3 = {n: Sm[n].reshape(L, 1, -1) for n in ("ffn1_pre_g", "ffn1_post_g", "mix_pre_g", "attn_norm_g", "ssm_norm_g",
                                                "mix_post_g", "ffn2_pre_g", "ffn2_post_g", "ple_post_g", "ssm_b_glu", "ssm_d")}
    rot = _rot_tables(pos_col)
    prep_names = ("ssm_lam_re", "ssm_lam_im", "ssm_log_dt", "ssm_b_re", "ssm_b_im", "ssm_c_re", "ssm_c_im")
    prep_all, prep_vjp = jax.vjp(jax.vmap(_ssm_prep), *[Sm[n] for n in prep_names])
    prep_cot = [None] * L

    saved = []
    h = x
    for l in range(L):
        W = weights_of(l, h)
        sv = {"h0": h, "W": W}
        h, sv["a1"], sv["b1"], sv["f1"], sv["xn1"] = _ffn_fwd(
            h, g3["ffn1_pre_g"], g3["ffn1_post_g"], W["ffn1_w_gate"], W["ffn1_w_up"], W["ffn1_w_down"], l, "1")
        sv["h1"] = h
        P, sv["ain"] = _mix_proj(h, g3["mix_pre_g"], W["w_in"], rot, l)
        sv["P"] = P
        ol = [_attn_fwd(P, d, l) for d in PATTERN_DILATIONS]
        prep = tuple(t[l] for t in prep_all)
        a_re, a_im, bre, bim, cre, cim = prep
        sv["prep"] = prep
        sv["u"], bur, bui = _ssm_in(P, bre, bim, l)
        xr, xi = _scan(bur, bui, a_re, a_im, False, l)
        sv["xr"], sv["xi"] = xr, xi
        ssm, sv["y"], sv["z"] = _ssm_out(xr, xi, sv["u"], cre, cim, g3["ssm_d"], W["ssm_w_glu"], g3["ssm_b_glu"], l)
        sv["ssm"] = ssm
        h, sv["attn"], sv["lse"], sv["mixed"], sv["mo"] = _mix_out(
            [o for o, _ in ol], [s for _, s in ol], ssm, h, g3["attn_norm_g"], g3["ssm_norm_g"], g3["mix_post_g"],
            W["w_out"], l)
        sv["h2"] = h
        h, sv["a2"], sv["b2"], sv["f2"], sv["xn2"] = _ffn_fwd(
            h, g3["ffn2_pre_g"], g3["ffn2_post_g"], W["ffn2_w_gate"], W["ffn2_w_up"], W["ffn2_w_down"], l, "2")
        sv["h3"] = h
        h, sv["e"], sv["gte"] = _ple_fwd(h, p3, W["ple_w_up"], W["ple_w_gate"], g3["ple_post_g"], l)
        saved.append(sv)

    dh, loss = _loss_head(h, target)

    G_layers = [{n: lax.empty((NSH, 1, r, c), BF16) for n, r, c in BIG} for _ in range(L)]
    sg = {n: [None] * L for n in SMALL}
    whole, shard, kcol = "whole", "shard", "cols"
    ple_g = g3["ple_post_g"]
    for l in reversed(range(L)):
        sv = saved[l]
        W = sv["W"]
        G, gl = G_layers[l], 0
        if l + 1 < L:
            ple_g = ple_g + layer_grads_done(l + 1, G_layers[l + 1])
        dh, de, dgte, sg["ple_post_g"][l] = _ple_bwd(dh, sv["e"], sv["gte"], W["ple_w_gate"], ple_g, l)
        G["ple_w_up"] = _dw(p3[l][None], de, G["ple_w_up"], gl, PLE, 256, whole, shard, f"dw_ple_up_l{l}")
        G["ple_w_gate"] = _dw(sv["h3"][None], dgte[None], G["ple_w_gate"], gl, 256, D, kcol, whole, f"dw_ple_gate_l{l}")
        dh, df, da, db, hh, sg["ffn2_pre_g"][l], sg["ffn2_post_g"][l] = _ffn_bwd(
            dh, sv["h2"], sv["f2"], sv["a2"], sv["b2"], g3["ffn2_pre_g"], g3["ffn2_post_g"],
            W["ffn2_w_gate"], W["ffn2_w_up"], W["ffn2_w_down"], l, "2")
        G["ffn2_w_gate"] = _dw(da, sv["xn2"][None], G["ffn2_w_gate"], gl, DFS, D, shard, whole, f"dw_ffn2_gate_l{l}")
        G["ffn2_w_up"] = _dw(db, sv["xn2"][None], G["ffn2_w_up"], gl, DFS, D, shard, whole, f"dw_ffn2_up_l{l}")
        G["ffn2_w_down"] = _dw(hh, df[None], G["ffn2_w_down"], gl, DFS, D, shard, whole, f"dw_ffn2_down_l{l}")
        a_re, a_im, bre, bim, cre, cim = sv["prep"]
        dattn, dssm, delta, dmo, sg["mix_post_g"][l], sg["attn_norm_g"][l], sg["ssm_norm_g"][l] = _mix_out_bwd(
            dh, sv["mo"], sv["attn"], sv["ssm"], g3["attn_norm_g"], g3["ssm_norm_g"], g3["mix_post_g"], W["w_out"], l)
        G["w_out"] = _dw(sv["mixed"][None], dmo[None], G["w_out"], gl, 256, D, kcol, whole, f"dw_out_l{l}")
        gnr, gni, du_direct, dz, yg, sg["ssm_b_glu"][l], dd, dcre, dcim = _ssm_out_bwd(
            dssm, sv["y"], sv["z"], sv["xr"], sv["xi"], sv["u"], cre, cim, g3["ssm_d"], W["ssm_w_glu"], l)
        sg["ssm_d"][l] = dd.reshape(Sm["ssm_d"].shape[1:])
        G["ssm_w_glu"] = _dw(yg[None], dz[None], G["ssm_w_glu"], gl, 128, DSS, kcol, whole, f"dw_glu_l{l}")
        gr, gi = _scan(gnr, gni, a_re, a_im, True, l)
        dar, dai = _ssm_da(gr, gi, sv["xr"], sv["xi"], l)
        du, dbre, dbim = _ssm_in_bwd(gr, gi, sv["u"], bre, bim, du_direct, l)
        prep_cot[l] = (dar, dai, dbre, dbim, dcre, dcim)
        acc = None
        for d in PATTERN_DILATIONS:
            acc = _attn_bwd(sv["P"], dattn, sv["lse"], delta, acc, d, l)
        dh, dP, sg["mix_pre_g"][l] = _mix_proj_bwd(acc[0], acc[1], acc[2], du, dh, sv["h1"], g3["mix_pre_g"],
                                                   W["w_in"], rot, l)
        G["w_in"] = _dw(sv["ain"][None], dP, G["w_in"], gl, D, DA,whole, shard, f"dw_in_l{l}")
        dh, df, da, db, hh, sg["ffn1_pre_g"][l], sg["ffn1_post_g"][l] = _ffn_bwd(
            dh, sv["h0"], sv["f1"], sv["a1"], sv["b1"], g3["ffn1_pre_g"], g3["ffn1_post_g"],
            W["ffn1_w_gate"], W["ffn1_w_up"], W["ffn1_w_down"], l, "1")
        G["ffn1_w_gate"] = _dw(da, sv["xn1"][None], G["ffn1_w_gate"], gl, DFS, D, shard, whole, f"dw_ffn1_gate_l{l}")
        G["ffn1_w_up"] = _dw(db, sv["xn1"][None], G["ffn1_w_up"], gl, DFS, D, shard, whole, f"dw_ffn1_up_l{l}")
        G["ffn1_w_down"] = _dw(hh, df[None], G["ffn1_w_down"], gl, DFS, D, shard, whole, f"dw_ffn1_down_l{l}")

    small = {n: jnp.stack([g.reshape(Sm[n].shape[1:]) for g in sg[n]]) for n in SMALL if n not in prep_names}
    small.update(zip(prep_names, prep_vjp(tuple(jnp.stack(c) for c in zip(*prep_cot)))))
    return loss, dh, G_layers[0], small


HBM_SPEC = pl.BlockSpec(memory_space=pltpu.HBM)


def _place():
    x, y, c = lax.axis_index("x"), lax.axis_index("y"), lax.axis_index("c")
    chips = [(1 - x, y), (x, 1 - y), (1 - x, 1 - y)]
    return x, y, c, chips


def _comm_params():
    return pltpu.CompilerParams(vmem_limit_bytes=VMEM_LIMIT)


def _gather_weights(ws, lands):
    n = len(ws)

    def body(*refs):
        ins, outs = refs[:n], refs[2 * n:3 * n]
        s_ici, r_ici, s_d2d, r_d2d = refs[3 * n:]
        x, y, c, chips = _place()

        def half(ref, t, hc):
            r2 = ws[t].shape[1] // 2
            return ref.at[:, pl.ds(hc * r2, r2), :]

        def ici(t, k, src_chip, to):
            j = 2 * src_chip[0] + src_chip[1]
            src = half(ins[t], t, c) if to is not None else half(outs[t].at[j], t, c)
            return pltpu.make_async_remote_copy(src_ref=src, dst_ref=half(outs[t].at[j], t, c),
                                                send_sem=s_ici.at[3 * t + k], recv_sem=r_ici.at[3 * t + k],
                                                device_id=to if to is not None else (x, y, c), device_id_type=MESH)

        def d2d(t, k, hc):
            j = 2 * chips[k][0] + chips[k][1]
            r = half(outs[t].at[j], t, hc)
            return pltpu.make_async_remote_copy(src_ref=r, dst_ref=r, send_sem=s_d2d.at[3 * t + k],
                                                recv_sem=r_d2d.at[3 * t + k], device_id=(x, y, 1 - c),
                                                device_id_type=MESH)

        sends = [ici(t, k, (x, y), (*chips[k], c)) for t in range(n) for k in range(3)]
        for cp in sends:
            cp.start()
        passed = []
        for t in range(n):
            for k in range(3):
                ici(t, k, chips[k], None).wait_recv()
                passed.append(d2d(t, k, c))
                passed[-1].start()
        for t in range(n):
            for k in range(3):
                d2d(t, k, 1 - c).wait_recv()
        for cp in sends + passed:
            cp.wait_send()

    return _pc(body, name="gather_weights", in_specs=[HBM_SPEC] * (2 * n), out_specs=[HBM_SPEC] * n,
               out_shape=[S(z.shape, z.dtype) for z in lands], input_output_aliases={n + t: t for t in range(n)},
               scratch_shapes=[pltpu.SemaphoreType.DMA((3 * n,))] * 4, compiler_params=_comm_params())(*ws, *lands)


SEM_SPEC = pl.BlockSpec(memory_space=pltpu.SEMAPHORE)
ANY_SPEC = pl.BlockSpec(memory_space=pl.ANY)
SPLIT_EFFECT = pltpu.SideEffectType.DATAFLOW_SIDE_EFFECTING


def _in_hbm(t):
    return pltpu.with_memory_space_constraint(t, pltpu.HBM)


def _place_own(ws, me_arr, layer):
    n = len(ws)

    def body(me_ref, *refs):
        for t in range(n):
            refs[n + t][...] = refs[t][...]

    gs = pltpu.PrefetchScalarGridSpec(
        num_scalar_prefetch=1, grid=(2,),
        in_specs=[pl.BlockSpec((w.shape[0], w.shape[1] // 2, w.shape[2]), lambda i, me: (0, i, 0)) for w in ws],
        out_specs=[pl.BlockSpec((None, w.shape[0], w.shape[1] // 2, w.shape[2]), lambda i, me: (me[0], 0, i, 0))
                   for w in ws])
    return _pc(body, name=f"gather_place_own_l{layer}", grid_spec=gs,
               out_shape=[S((NSH,) + w.shape, w.dtype) for w in ws], compiler_params=_cp(1))(me_arr, *ws)


def _gather_start(ws, lands, after, layer):
    n = len(ws)

    def body(*refs):
        ins, lz = refs[:n], refs[n:2 * n]
        s_sem, r_sem = refs[2 * n + 1], refs[2 * n + 2]
        token = refs[-1]
        x, y, c, chips = _place()
        for t in range(n):
            for k in range(3):
                pltpu.make_async_remote_copy(src_ref=ins[t], dst_ref=lz[t].at[2 * x + y], send_sem=s_sem.at[3 * t + k],
                                             recv_sem=r_sem.at[3 * t + k], device_id=(*chips[k], c),
                                             device_id_type=MESH).start()
        token[...] = jnp.zeros_like(token)

    hbm = [pltpu.HBM(w.shape, w.dtype) for w in ws] + [pltpu.HBM(z.shape, z.dtype) for z in lands]
    out = _pc(body, name=f"gather_start_l{layer}",
              out_shape=(pltpu.SemaphoreType.DMA((3 * n,)), pltpu.SemaphoreType.DMA((3 * n,)), *hbm, S((8, 128), F32)),
              in_specs=[HBM_SPEC] * (2 * n) + [ANY_SPEC],
              out_specs=(SEM_SPEC, SEM_SPEC, *([HBM_SPEC] * (2 * n)), pl.BlockSpec(memory_space=pltpu.VMEM)),
              input_output_aliases={i: 2 + i for i in range(2 * n)},
              compiler_params=pltpu.CompilerParams(has_side_effects=SPLIT_EFFECT))(
                  *[_in_hbm(w) for w in ws], *[_in_hbm(z) for z in lands], after)
    return out[0], out[1], out[2:2 + n], out[2 + n:2 + 2 * n], out[-1]


def _gather_wait(s_sem, r_sem, ws, lands, after, layer):
    n = len(ws)

    def body(*refs):
        ins, lz = refs[:n], refs[n:2 * n]
        s_ref, r_ref = refs[2 * n], refs[2 * n + 1]
        x, y, c, chips = _place()
        for t in range(n):
            for k in range(3):
                cp = pltpu.make_async_remote_copy(src_ref=ins[t], dst_ref=lz[t].at[2 * x + y], send_sem=s_ref.at[3 * t + k],
                                                  recv_sem=r_ref.at[3 * t + k], device_id=(*chips[k], c),
                                                  device_id_type=MESH)
                cp.wait_send()
                cp.wait_recv()

    hbm = [pltpu.HBM(w.shape, w.dtype) for w in ws] + [pltpu.HBM(z.shape, z.dtype) for z in lands]
    out = _pc(body, name=f"gather_wait_l{layer}", out_shape=tuple(hbm),
              in_specs=[HBM_SPEC] * (2 * n) + [SEM_SPEC, SEM_SPEC, ANY_SPEC], out_specs=tuple([HBM_SPEC] * (2 * n)),
              input_output_aliases={i: i for i in range(2 * n)},
              compiler_params=pltpu.CompilerParams(has_side_effects=SPLIT_EFFECT))(*ws, *lands, s_sem, r_sem, after)
    return out[n:]


def _swap_halves(gs, tag):
    n = len(gs)

    def body(*refs):
        ins, outs = refs[:n], refs[n:2 * n]
        s_sem, r_sem = refs[2 * n:]
        x, y, c, _ = _place()
        cps = []
        for t in range(n):
            r2 = gs[t].shape[2] // 2
            cps.append(pltpu.make_async_remote_copy(
                src_ref=ins[t].at[:, :, pl.ds((1 - c) * r2, r2), :], dst_ref=outs[t], send_sem=s_sem.at[t],
                recv_sem=r_sem.at[t], device_id=(x, y, 1 - c), device_id_type=MESH))
            cps[-1].start()
        for cp in cps:
            cp.wait_recv()
        for cp in cps:
            cp.wait_send()

    return _pc(body, name=f"grad_swap_halves_{tag}", in_specs=[HBM_SPEC] * n, out_specs=[HBM_SPEC] * n,
               out_shape=[S(g.shape[:2] + (g.shape[2] // 2, g.shape[3]), g.dtype) for g in gs],
               scratch_shapes=[pltpu.SemaphoreType.DMA((n,))] * 2, compiler_params=_comm_params())(*gs)


def _add_half(g, landed, c_arr, name):
    _, L, r2, cols = landed.shape

    def body(c_ref, g_ref, l_ref, o_ref):
        o_ref[...] = (g_ref[...].astype(F32) + l_ref[...].astype(F32)).astype(BF16)

    gs = pltpu.PrefetchScalarGridSpec(
        num_scalar_prefetch=1, grid=(NSH, L),
        in_specs=[pl.BlockSpec((None, None, r2, cols), lambda j, l, c: (j, l, c[0], 0)),
                  pl.BlockSpec((None, None, r2, cols), lambda j, l, c: (j, l, 0, 0))],
        out_specs=pl.BlockSpec((None, None, r2, cols), lambda j, l, c: (j, l, 0, 0)))
    return _pc(body, name=name, grid_spec=gs, out_shape=S(landed.shape, BF16), compiler_params=_cp(2))(c_arr, g, landed)


def _partial_copies(ins, lz, s_sem, r_sem):
    x, y, c, chips = _place()
    return [pltpu.make_async_remote_copy(src_ref=ins[t].at[2 * chips[k][0] + chips[k][1]], dst_ref=lz[t].at[k],
                                         send_sem=s_sem.at[3 * t + k], recv_sem=r_sem.at[3 * t + k],
                                         device_id=(*chips[k], c), device_id_type=MESH)
            for t in range(len(ins)) for k in range(3)]


def _partial_send_start(ps, lands):
    n = len(ps)

    def body(*refs):
        for cp in _partial_copies(refs[:n], refs[n:2 * n], refs[2 * n], refs[2 * n + 1]):
            cp.start()
        refs[-1][...] = jnp.zeros_like(refs[-1])

    hbm = [pltpu.HBM(p.shape, p.dtype) for p in ps] + [pltpu.HBM(z.shape, z.dtype) for z in lands]
    out = _pc(body, name="grad_partial_send_start",
              out_shape=(pltpu.SemaphoreType.DMA((3 * n,)), pltpu.SemaphoreType.DMA((3 * n,)), *hbm, S((8, 128), F32)),
              in_specs=[HBM_SPEC] * (2 * n),
              out_specs=(SEM_SPEC, SEM_SPEC, *([HBM_SPEC] * (2 * n)), pl.BlockSpec(memory_space=pltpu.VMEM)),
              input_output_aliases={i: 2 + i for i in range(2 * n)},
              compiler_params=pltpu.CompilerParams(has_side_effects=SPLIT_EFFECT))(
                  *[_in_hbm(p) for p in ps], *[_in_hbm(z) for z in lands])
    return out[0], out[1], out[2:2 + n], out[2 + n:2 + 2 * n], out[-1]


def _partial_send_wait(s_sem, r_sem, ps, lands, after):
    n = len(ps)

    def body(*refs):
        for cp in _partial_copies(refs[:n], refs[n:2 * n], refs[2 * n], refs[2 * n + 1]):
            cp.wait_send()
            cp.wait_recv()

    hbm = [pltpu.HBM(p.shape, p.dtype) for p in ps] + [pltpu.HBM(z.shape, z.dtype) for z in lands]
    out = _pc(body, name="grad_partial_send_wait", out_shape=tuple(hbm),
              in_specs=[HBM_SPEC] * (2 * n) + [SEM_SPEC, SEM_SPEC, ANY_SPEC], out_specs=tuple([HBM_SPEC] * (2 * n)),
              input_output_aliases={i: i for i in range(2 * n)},
              compiler_params=pltpu.CompilerParams(has_side_effects=SPLIT_EFFECT))(*ps, *lands, s_sem, r_sem, after)
    return out[:n], out[n:]


def _sum_shards(part, landed, me_arr, c_arr, buf, first_layer, name):
    _, nl, r2, cols = landed.shape

    def body(me_ref, c_ref, p_ref, l_ref, b_ref, o_ref):
        o_ref[...] = ((p_ref[...].astype(F32) + l_ref[0].astype(F32)) + l_ref[1].astype(F32)) + l_ref[2].astype(F32)

    gs = pltpu.PrefetchScalarGridSpec(
        num_scalar_prefetch=2, grid=(nl,),
        in_specs=[pl.BlockSpec((None, None, r2, cols), lambda l, me, c: (me[0], l, 0, 0)),
                  pl.BlockSpec((3, None, r2, cols), lambda l, me, c: (0, l, 0, 0)), ANY_SPEC],
        out_specs=pl.BlockSpec((None, r2, cols), lambda l, me, c: (first_layer + l, c[0], 0)))
    return _pc(body, name=name, grid_spec=gs, out_shape=S(buf.shape, F32), input_output_aliases={4: 0},
               compiler_params=_cp(1))(me_arr, c_arr, part, landed, buf)


def _direct_grad_copies(ins, lz, s_sem, r_sem):
    x, y, c, chips = _place()
    sends, recvs = [], []
    for t in range(len(ins)):
        r2 = ins[t].shape[2] // 2
        half = lambda j, h: ins[t].at[j, :, pl.ds(h * r2, r2), :]

        def copy(src, slot, s_idx, r_idx, to):
            return pltpu.make_async_remote_copy(src_ref=src, dst_ref=lz[t].at[slot], send_sem=s_sem.at[7 * t + s_idx],
                                                recv_sem=r_sem.at[7 * t + r_idx], device_id=to, device_id_type=MESH)

        for k in range(3):
            for h in range(2):
                sends.append(copy(half(2 * chips[k][0] + chips[k][1], h), 2 * k + c, 2 * k + h, 2 * k + c, (*chips[k], h)))
        sends.append(copy(half(2 * x + y, 1 - c), 6, 6, 6, (x, y, 1 - c)))
        recvs += [copy(half(0, 0), s, s, s, (x, y, c)) for s in range(7)]
    return sends, recvs


def _send_start(gs, lands, layer):
    n = len(gs)
    ps = gs

    def body(*refs):
        sends, _ = _direct_grad_copies(refs[:n], refs[n:2 * n], refs[2 * n], refs[2 * n + 1])
        for cp in sends:
            cp.start()
        refs[-1][...] = jnp.zeros_like(refs[-1])

    hbm = [pltpu.HBM(p.shape, p.dtype) for p in ps] + [pltpu.HBM(z.shape, z.dtype) for z in lands]
    out = _pc(body, name=f"grad_send_start_l{layer}",
              out_shape=(pltpu.SemaphoreType.DMA((7 * n,)), pltpu.SemaphoreType.DMA((7 * n,)), *hbm, S((8, 128), F32)),
              in_specs=[HBM_SPEC] * (2 * n),
              out_specs=(SEM_SPEC, SEM_SPEC, *([HBM_SPEC] * (2 * n)), pl.BlockSpec(memory_space=pltpu.VMEM)),
              input_output_aliases={i: 2 + i for i in range(2 * n)},
              compiler_params=pltpu.CompilerParams(has_side_effects=SPLIT_EFFECT))(
                  *[_in_hbm(p) for p in ps], *[_in_hbm(z) for z in lands])
    return out[0], out[1], out[2:2 + n], out[2 + n:2 + 2 * n], out[-1]


def _send_wait(s_sem, r_sem, ps, lands, after, layer):
    n = len(ps)

    def body(*refs):
        sends, recvs = _direct_grad_copies(refs[:n], refs[n:2 * n], refs[2 * n], refs[2 * n + 1])
        for cp in sends:
            cp.wait_send()
        for cp in recvs:
            cp.wait_recv()

    hbm = [pltpu.HBM(p.shape, p.dtype) for p in ps] + [pltpu.HBM(z.shape, z.dtype) for z in lands]
    out = _pc(body, name=f"grad_send_wait_l{layer}", out_shape=tuple(hbm),
              in_specs=[HBM_SPEC] * (2 * n) + [SEM_SPEC, SEM_SPEC, ANY_SPEC], out_specs=tuple([HBM_SPEC] * (2 * n)),
              input_output_aliases={i: i for i in range(2 * n)},
              compiler_params=pltpu.CompilerParams(has_side_effects=SPLIT_EFFECT))(*ps, *lands, s_sem, r_sem, after)
    return out[:n], out[n:]


def _sum_direct(g, landed, me_arr, c_arr, buf, first_layer, name):
    _, nl, r2, cols = landed.shape

    def body(me_ref, c_ref, g_ref, l_ref, b_ref, o_ref):
        tot = g_ref[...].astype(F32)
        for s in range(7):
            tot = tot + l_ref[s].astype(F32)
        o_ref[...] = tot

    gs = pltpu.PrefetchScalarGridSpec(
        num_scalar_prefetch=2, grid=(nl,),
        in_specs=[pl.BlockSpec((None, None, r2, cols), lambda l, me, c: (me[0], l, c[0], 0)),
                  pl.BlockSpec((7, None, r2, cols), lambda l, me, c: (0, l, 0, 0)), ANY_SPEC],
        out_specs=pl.BlockSpec((None, r2, cols), lambda l, me, c: (first_layer + l, c[0], 0)))
    return _pc(body, name=name, grid_spec=gs, out_shape=S(buf.shape, F32), input_output_aliases={4: 0},
               compiler_params=_cp(1))(me_arr, c_arr, g, landed, buf)


def _share_halves(bufs):
    n = len(bufs)

    def body(*refs):
        ins, outs = refs[:n], refs[n:2 * n]
        s_sem, r_sem = refs[2 * n:]
        x, y, c, _ = _place()
        cps = []
        for t in range(n):
            r2 = bufs[t].shape[1] // 2
            cps.append(pltpu.make_async_remote_copy(
                src_ref=ins[t].at[:, pl.ds(c * r2, r2), :], dst_ref=outs[t].at[:, pl.ds(c * r2, r2), :],
                send_sem=s_sem.at[t], recv_sem=r_sem.at[t], device_id=(x, y, 1 - c), device_id_type=MESH))
            cps[-1].start()
        for cp in cps:
            cp.wait_recv()
        for cp in cps:
            cp.wait_send()

    return _pc(body, name="grad_share_halves", in_specs=[HBM_SPEC] * n, out_specs=[HBM_SPEC] * n,
               out_shape=[S(b.shape, b.dtype) for b in bufs], input_output_aliases={t: t for t in range(n)},
               scratch_shapes=[pltpu.SemaphoreType.DMA((n,))] * 2, compiler_params=_comm_params())(*bufs)


def _gather_small(v, after):
    nr = v.shape[0]

    def body(v_ref, after_ref, out_ref, send_sems, recv_sems, local_sem):
        x, y, c, chips = _place()
        me, sibling = (x, y, c), (x, y, 1 - c)

        def rows(px, py, pc):
            return out_ref.at[pl.ds((4 * px + 2 * py + pc) * nr, nr), :]

        def copy(k, block, to, src=None):
            return pltpu.make_async_remote_copy(src_ref=rows(*block) if src is None else src, dst_ref=rows(*block),
                                                send_sem=send_sems.at[k], recv_sem=recv_sems.at[k], device_id=to,
                                                device_id_type=MESH)

        mine = pltpu.make_async_copy(v_ref, rows(*me), local_sem)
        mine.start()
        first = [copy(0, me, sibling, src=v_ref)]
        first += [copy(1 + j, me, (*chip, c), src=v_ref) for j, chip in enumerate(chips)]
        for cp in first:
            cp.start()
        passed = [copy(4 + j, (*chip, c), sibling) for j, chip in enumerate(chips)]
        for j, chip in enumerate(chips):
            copy(1 + j, (*chip, c), me).wait_recv()
            passed[j].start()
        copy(0, sibling, me).wait_recv()
        for j, chip in enumerate(chips):
            copy(4 + j, (*chip, 1 - c), me).wait_recv()
        for cp in first + passed:
            cp.wait_send()
        mine.wait()

    vm = pl.BlockSpec(memory_space=pltpu.VMEM)
    return _pc(body, name="gather_small_grads", in_specs=[vm, ANY_SPEC], out_specs=vm, out_shape=S((8 * nr, 128), F32),
               scratch_shapes=[pltpu.SemaphoreType.DMA((7,)), pltpu.SemaphoreType.DMA((7,)), pltpu.SemaphoreType.DMA],
               compiler_params=_comm_params())(v, after)


def _adamw_math(w, g, m, v):
    m2 = ADAM_B1 * m + (1.0 - ADAM_B1) * g
    v2 = ADAM_B2 * v + (1.0 - ADAM_B2) * (g * g)
    m_hat = m2 / (1.0 - ADAM_B1 ** ADAM_STEP)
    v_hat = v2 / (1.0 - ADAM_B2 ** ADAM_STEP)
    return -ADAM_LR * (m_hat / (jnp.sqrt(v_hat) + ADAM_EPS) + ADAM_WD * w), m2, v2


def _adamw(w, g, m, v, name):
    L, R, C = w.shape
    rb = R // 2 if R >= 512 else R

    def body(w_ref, g_ref, m_ref, v_ref, d_ref, m2_ref, v2_ref):
        d_ref[...], m2_ref[...], v2_ref[...] = _adamw_math(w_ref[...], g_ref[...], m_ref[...], v_ref[...])

    blk = pl.BlockSpec((None, rb, C), lambda l, r: (l, r, 0))
    return _pc(body, name=name, grid=(L, R // rb), in_specs=[blk] * 4, out_specs=[blk] * 3,
               out_shape=[S(w.shape, F32)] * 3, compiler_params=_cp(2))(w, g, m, v)


def _adamw_small(gathered, w, m, v):
    nr = w.shape[0]
    rb = nr // 5

    def body(a_ref, w_ref, m_ref, v_ref, g_ref, d_ref, m2_ref, v2_ref):
        g = a_ref[0]
        for k in range(1, 8):
            g = g + a_ref[k]
        g_ref[...] = g
        d_ref[...], m2_ref[...], v2_ref[...] = _adamw_math(w_ref[...], g, m_ref[...], v_ref[...])

    blk = pl.BlockSpec((rb, 128), lambda i: (i, 0))
    return _pc(body, name="adamw_small", grid=(nr // rb,), in_specs=[pl.BlockSpec((8, rb, 128), lambda i: (0, i, 0))] + [blk] * 3,
               out_specs=[blk] * 4, out_shape=[S((nr, 128), F32)] * 4, compiler_params=_cp(1))(gathered, w, m, v)


SMALL_ROWS = 4520


def _pack(arrs):
    flat = jnp.concatenate([a.reshape(-1) for a in arrs])
    return jnp.pad(flat, (0, SMALL_ROWS * 128 - flat.shape[0])).reshape(SMALL_ROWS, 128)


def _unpack(packed, like):
    flat = packed.reshape(-1)
    out, off = [], 0
    for a in like:
        out.append(flat[off:off + a.size].reshape(a.shape))
        off += a.size
    return out


def kernel(x, p, positions, ffn1_pre_g, ffn1_w_gate, ffn1_w_up, ffn1_w_down, ffn1_post_g, mix_pre_g, w_in, attn_norm_g, ssm_lam_re, ssm_lam_im, ssm_log_dt, ssm_b_re, ssm_b_im, ssm_c_re, ssm_c_im, ssm_d, ssm_w_glu, ssm_b_glu, ssm_norm_g, w_out, mix_post_g, ffn2_pre_g, ffn2_w_gate, ffn2_w_up, ffn2_w_down, ffn2_post_g, ple_w_up, ple_w_gate, ple_post_g, loss_target, m_ffn1_pre_g, m_ffn1_w_gate, m_ffn1_w_up, m_ffn1_w_down, m_ffn1_post_g, m_mix_pre_g, m_w_in, m_attn_norm_g, m_ssm_lam_re, m_ssm_lam_im, m_ssm_log_dt, m_ssm_b_re, m_ssm_b_im, m_ssm_c_re, m_ssm_c_im, m_ssm_d, m_ssm_w_glu, m_ssm_b_glu, m_ssm_norm_g, m_w_out, m_mix_post_g, m_ffn2_pre_g, m_ffn2_w_gate, m_ffn2_w_up, m_ffn2_w_down, m_ffn2_post_g, m_ple_w_up, m_ple_w_gate, m_ple_post_g, v_ffn1_pre_g, v_ffn1_w_gate, v_ffn1_w_up, v_ffn1_w_down, v_ffn1_post_g, v_mix_pre_g, v_w_in, v_attn_norm_g, v_ssm_lam_re, v_ssm_lam_im, v_ssm_log_dt, v_ssm_b_re, v_ssm_b_im, v_ssm_c_re, v_ssm_c_im, v_ssm_d, v_ssm_w_glu, v_ssm_b_glu, v_ssm_norm_g, v_w_out, v_mix_post_g, v_ffn2_pre_g, v_ffn2_w_gate, v_ffn2_w_up, v_ffn2_w_down, v_ffn2_post_g, v_ple_w_up, v_ple_w_gate, v_ple_post_g):
    a = dict(locals())
    T = x.shape[1]
    big_names = [n for n, _, _ in BIG]
    for n in TRANSPOSED:
        for pre in ("", "m_", "v_"):
            a[pre + n] = jnp.swapaxes(a[pre + n], 1, 2)

    own = [a[n].astype(BF16) for n in big_names]
    n_layers = own[0].shape[0]
    per_layer = [[w[l:l + 1] for w in own] for l in range(n_layers)]
    c_arr = lax.axis_index("c").astype(jnp.int32).reshape(1)
    me_arr = (2 * lax.axis_index("x") + lax.axis_index("y")).astype(jnp.int32).reshape(1)
    first = dict(zip(big_names, _gather_weights(per_layer[0], _place_own(per_layer[0], me_arr, 0))))
    pending, anchor, queued_behind = {}, jnp.zeros((), F32), first[big_names[0]]
    for l in range(1, n_layers):
        s_sem, r_sem, ws_thru, lands_thru, token = _gather_start(per_layer[l], _place_own(per_layer[l], me_arr, l),
                                                                 queued_behind, l)
        pending[l] = (s_sem, r_sem, ws_thru, lands_thru)
        anchor = anchor + token[0, 0]
        queued_behind = token

    def weights_of(l, after):
        if l == 0:
            return first
        return dict(zip(big_names, _gather_wait(*pending[l], after, l)))

    Sm = {n: a[n] for n in SMALL}
    Sm["ffn1_pre_g"] = Sm["ffn1_pre_g"] + anchor

    pos = jnp.broadcast_to(positions.reshape(1, T, 1).astype(F32), (1, T, 128))
    sent = {}

    def layer_grads_done(l, G):
        gs = [G[n] for n in big_names]
        lands = [lax.empty((7, 1, g.shape[2] // 2, g.shape[3]), BF16) for g in gs]
        s_sem, r_sem, gs_thru, lands_thru, token = _send_start(gs, lands, l)
        sent[l] = (s_sem, r_sem, gs_thru, lands_thru)
        return token[0, 0]

    loss, gx, G_first, small = _local_step(
        _reorder(x, True, "to_streams_x")[0], _reorder(p[:, 0], True, "to_streams_p"),
        _reorder(pos, True, "to_streams_pos")[0, :, :1], _reorder(loss_target, True, "to_streams_target")[0],
        weights_of, layer_grads_done, Sm)
    gx = _reorder(gx[None], False, "to_time_grad_x")

    gs0 = [G_first[n] for n in big_names]
    parts = [_add_half(g, la, c_arr, f"grad_add_half_{n}") for g, la, n in zip(gs0, _swap_halves(gs0, "first"), big_names)]
    first_sent = _partial_send_start(parts, [lax.empty((3,) + pt.shape[1:], BF16) for pt in parts])

    bufs = [lax.empty((n_layers, r, c), F32) for _, r, c in BIG]
    for l in sorted(sent, reverse=True):
        gs, landed = _send_wait(*sent[l], first_sent[-1], l)
        bufs = [_sum_direct(g, la, me_arr, c_arr, b, l, f"grad_sum_direct_l{l}_{n}")
                for g, la, b, n in zip(gs, landed, bufs, big_names)]
    small_g = _gather_small(_pack([small[n] for n in SMALL]), bufs[0]).reshape(8, SMALL_ROWS, 128)
    sg, sd, sm, sv = _adamw_small(small_g, _pack([a[n] for n in SMALL]), _pack([a["m_" + n] for n in SMALL]),
                                  _pack([a["v_" + n] for n in SMALL]))

    parts, landed = _partial_send_wait(*first_sent[:-1], sd)
    bufs = [_sum_shards(pt, la, me_arr, c_arr, b, 0, f"grad_sum_shards_first_{n}")
            for pt, la, b, n in zip(parts, landed, bufs, big_names)]
    grads = dict(zip(big_names, _share_halves(bufs)))
    like = [a[n] for n in SMALL]
    res = {}
    for n, g_, d_, m_, v_ in zip(SMALL, _unpack(sg, like), _unpack(sd, like), _unpack(sm, like), _unpack(sv, like)):
        res[n] = (g_, d_, m_, v_)
    for n in big_names:
        d_, m_, v_ = _adamw(a[n], grads[n], a["m_" + n], a["v_" + n], f"adamw_{n}")
        res[n] = (grads[n], d_, m_, v_)
        if n in TRANSPOSED:
            res[n] = tuple(jnp.swapaxes(t, 1, 2) for t in res[n])

    total = lax.psum(loss[0, 0], ("x", "y", "c"))
    return (total, gx, *[res[n][0] for n in WEIGHTS], *[res[n][1] for n in WEIGHTS],
            *[res[n][2] for n in WEIGHTS], *[res[n][3] for n in WEIGHTS])
```

```python
import functools
import math

import numpy as np
import jax
import jax.numpy as jnp
from jax import lax
from jax.experimental import pallas as pl
from jax.experimental.pallas import tpu as pltpu

F32 = jnp.float32
BF16 = jnp.bfloat16
S = jax.ShapeDtypeStruct
MESH = pl.DeviceIdType.MESH

D = 1024
DA = 512
DSS = 512
HD = 64
NH = 8
BAND = 128
NSH = 4
DFS = 704
PLE = 256
EPS = 1e-6
ROPE_THETA = 500000.0
PATTERN_DILATIONS = (1, 4, 16)
NLB = 16
ADAM_LR, ADAM_B1, ADAM_B2, ADAM_EPS, ADAM_WD, ADAM_STEP = 0.001, 0.9, 0.999, 1e-08, 0.01, 10

VMEM_LIMIT = 56 * 1024 * 1024
TM = 512
TMB = 256

BIG = (
    ("ffn1_w_gate", DFS, D), ("ffn1_w_up", DFS, D), ("ffn1_w_down", DFS, D),
    ("w_in", D, 512), ("ssm_w_glu", 128, 512), ("w_out", 256, D),
    ("ffn2_w_gate", DFS, D), ("ffn2_w_up", DFS, D), ("ffn2_w_down", DFS, D),
    ("ple_w_up", PLE, 256), ("ple_w_gate", 256, D),
)
TRANSPOSED = ("ffn1_w_gate", "ffn1_w_up", "ffn2_w_gate", "ffn2_w_up")
SMALL = ("ffn1_pre_g", "ffn1_post_g", "mix_pre_g", "attn_norm_g", "ssm_lam_re", "ssm_lam_im", "ssm_log_dt",
         "ssm_b_re", "ssm_b_im", "ssm_c_re", "ssm_c_im", "ssm_d", "ssm_b_glu", "ssm_norm_g", "mix_post_g",
         "ffn2_pre_g", "ffn2_post_g", "ple_post_g")
WEIGHTS = ("ffn1_pre_g", "ffn1_w_gate", "ffn1_w_up", "ffn1_w_down", "ffn1_post_g", "mix_pre_g", "w_in", "attn_norm_g",
           "ssm_lam_re", "ssm_lam_im", "ssm_log_dt", "ssm_b_re", "ssm_b_im", "ssm_c_re", "ssm_c_im", "ssm_d",
           "ssm_w_glu", "ssm_b_glu", "ssm_norm_g", "w_out", "mix_post_g", "ffn2_pre_g", "ffn2_w_gate", "ffn2_w_up",
           "ffn2_w_down", "ffn2_post_g", "ple_w_up", "ple_w_gate", "ple_post_g")


def _pc(body, **kw):
    return pl.pallas_call(body, **kw)


def _cp(n_grid):
    return pltpu.CompilerParams(dimension_semantics=("arbitrary",) * n_grid, vmem_limit_bytes=VMEM_LIMIT)


def _dot(a, b):
    return jnp.dot(a, b, preferred_element_type=F32)


def _dot_nt(a, b):
    return lax.dot_general(a, b, (((1,), (1,)), ((), ())), preferred_element_type=F32)


def _dot_tn(a, b):
    return lax.dot_general(a, b, (((0,), (0,)), ((), ())), preferred_element_type=F32)


def _split(a):
    hi = a.astype(BF16)
    return hi, (a - hi.astype(F32)).astype(BF16)


def _dot2(fn, a, b, exact):
    if exact == "a":
        ah, al = _split(a)
        b16 = b.astype(BF16)
        return fn(ah, b16) + fn(al, b16)
    bh, bl = _split(b)
    a16 = a.astype(BF16)
    return fn(a16, bh) + fn(a16, bl)


def _rms_fwd(x, g):
    r = lax.rsqrt(jnp.mean(x * x, axis=-1, keepdims=True) + EPS)
    return x * r * g


def _rms_bwd(dy, x, g):
    r = lax.rsqrt(jnp.mean(x * x, axis=-1, keepdims=True) + EPS)
    xr = x * r
    gd = dy * g
    dx = r * (gd - xr * jnp.mean(gd * xr, axis=-1, keepdims=True))
    dg = jnp.sum(dy * xr, axis=0, keepdims=True)
    return dx, dg


def _gelu(y):
    k = math.sqrt(2.0 / math.pi)
    return 0.5 * y * (1.0 + jnp.tanh(k * (y + 0.044715 * y * y * y)))


def _gelu_grad(y):
    k = math.sqrt(2.0 / math.pi)
    t = jnp.tanh(k * (y + 0.044715 * y * y * y))
    return 0.5 * (1.0 + t) + 0.5 * y * (1.0 - t * t) * k * (1.0 + 3 * 0.044715 * y * y)


def _gain_spec(n, layer):
    return pl.BlockSpec((None, 1, n), lambda *_: (layer, 0, 0))


def _row_acc_spec(n):
    return pl.BlockSpec((1, n), lambda *_: (0, 0))


def _rot_tables(pos_col):
    T = pos_col.shape[0]
    half = HD // 8
    inv = (ROPE_THETA ** (-np.arange(half, dtype=np.float32) * (2.0 / (2 * half)))).astype(np.float32)
    lane_freq = np.tile(np.concatenate([inv, inv, np.zeros(HD - 2 * half, np.float32)]), NH)[None, :]

    def body(p_ref, f_ref, c_ref, s1_ref, s2_ref):
        ang = p_ref[...] * f_ref[...]
        d = lax.broadcasted_iota(jnp.int32, ang.shape, 1) % HD
        cs = jnp.cos(ang)
        sn = jnp.sin(ang)
        c_ref[...] = jnp.where(d < 2 * half, cs, 1.0)
        s1_ref[...] = jnp.where(d < half, -sn, 0.0)
        s2_ref[...] = jnp.where((d >= half) & (d < 2 * half), sn, 0.0)

    tm = TM
    return _pc(body, name="rot_tables", grid=(T // tm,),
               in_specs=[pl.BlockSpec((tm, 1), lambda i: (i, 0)), pl.BlockSpec((1, DA), lambda i: (0, 0))],
               out_specs=[pl.BlockSpec((tm, DA), lambda i: (i, 0))] * 3,
               out_shape=[S((T, DA), F32)] * 3, compiler_params=_cp(1))(pos_col, jnp.asarray(lane_freq))


def _rot_fwd(t, c, s1, s2):
    return t * c + pltpu.roll(t, DA - 8, 1) * s1 + pltpu.roll(t, 8, 1) * s2


def _rot_bwd(g, c, s1, s2):
    return g * c + pltpu.roll(g * s1, 8, 1) + pltpu.roll(g * s2, DA - 8, 1)


def _ffn_weight_spec():
    return pl.BlockSpec((NSH, None, DFS, D), lambda i: (0, 0, 0, 0), pipeline_mode=pl.Buffered(1))


def _ffn_fwd(h, pre_g, post_g, wg, wu, wd, layer, tag):
    T = h.shape[0]
    tm = TM
    nt = T // tm

    def body(h_ref, pg_ref, qg_ref, wg_ref, wu_ref, wd_ref, ho_ref, a_ref, b_ref, f_ref, xn_ref):
        hv = h_ref[...]
        xb = _rms_fwd(hv, pg_ref[...]).astype(BF16)
        xn_ref[...] = xb
        f = None
        for j in range(NSH):
            ab = _dot_nt(xb, wg_ref[j]).astype(BF16)
            bb = _dot_nt(xb, wu_ref[j]).astype(BF16)
            a_ref[j] = ab
            b_ref[j] = bb
            a = ab.astype(F32)
            hh = (a * jax.nn.sigmoid(a) * bb.astype(F32)).astype(BF16)
            part = _dot(hh, wd_ref[j])
            f = part if f is None else f + part
        f_ref[...] = f
        ho_ref[...] = hv + 0.5 * _rms_fwd(f, qg_ref[...])

    row = pl.BlockSpec((tm, D), lambda i: (i, 0))
    act = pl.BlockSpec((NSH, tm, DFS), lambda i: (0, i, 0))
    return _pc(body, name=f"ffn_fwd_{tag}_l{layer}", grid=(nt,),
               in_specs=[row, _gain_spec(D, layer), _gain_spec(D, layer)] + [_ffn_weight_spec()] * 3,
               out_specs=[row, act, act, row, row],
               out_shape=[S((T, D), F32), S((NSH, T, DFS), BF16), S((NSH, T, DFS), BF16), S((T, D), F32), S((T, D), BF16)],
               compiler_params=_cp(1))(h, pre_g, post_g, wg, wu, wd)


def _ffn_bwd(dout, h, f, a, b, pre_g, post_g, wg, wu, wd, layer, tag):
    T = h.shape[0]
    tm = TMB
    nt = T // tm

    def body(do_ref, h_ref, f_ref, a_ref, b_ref, pg_ref, qg_ref, wg_ref, wu_ref, wd_ref,
             dh_ref, df_ref, da_ref, db_ref, hh_ref, dpg_ref, dqg_ref):
        @pl.when(pl.program_id(0) == 0)
        def _():
            dpg_ref[...] = jnp.zeros_like(dpg_ref)
            dqg_ref[...] = jnp.zeros_like(dqg_ref)

        do = do_ref[...]
        df, dq = _rms_bwd(0.5 * do, f_ref[...], qg_ref[...])
        dqg_ref[...] += dq
        dfb = df.astype(BF16)
        df_ref[...] = dfb
        dxn = None
        for j in range(NSH):
            dhh = _dot_nt(dfb, wd_ref[j])
            av = a_ref[j].astype(F32)
            bv = b_ref[j].astype(F32)
            sg = jax.nn.sigmoid(av)
            sa = av * sg
            hh_ref[j] = (sa * bv).astype(BF16)
            dab = (dhh * bv * (sg + sa * (1.0 - sg))).astype(BF16)
            dbb = (dhh * sa).astype(BF16)
            da_ref[j] = dab
            db_ref[j] = dbb
            part = _dot(dab, wg_ref[j]) + _dot(dbb, wu_ref[j])
            dxn = part if dxn is None else dxn + part
        dx, dp = _rms_bwd(dxn, h_ref[...], pg_ref[...])
        dpg_ref[...] += dp
        dh_ref[...] = do + dx

    row = pl.BlockSpec((tm, D), lambda i: (i, 0))
    act = pl.BlockSpec((NSH, tm, DFS), lambda i: (0, i, 0))
    return _pc(body, name=f"ffn_bwd_{tag}_l{layer}", grid=(nt,),
               in_specs=[row, row, row, act, act, _gain_spec(D, layer), _gain_spec(D, layer)] + [_ffn_weight_spec()] * 3,
               out_specs=[row, row, act, act, act, _row_acc_spec(D), _row_acc_spec(D)],
               out_shape=[S((T, D), F32), S((T, D), BF16), S((NSH, T, DFS), BF16), S((NSH, T, DFS), BF16),
                          S((NSH, T, DFS), BF16), S((1, D), F32), S((1, D), F32)],
               compiler_params=_cp(1))(dout, h, f, a, b, pre_g, post_g, wg, wu, wd)


def _dw(A, B, buf, layer, kb, nb, a_mode, b_mode, name):
    T = A.shape[1]
    tt = TM
    nt = T // tt

    def pick(v, mode, j, w):
        if mode == "shard":
            return v[j]
        return v[0] if mode == "whole" else v[0][:, j * w:(j + 1) * w]

    def body(a_ref, b_ref, buf_ref, o_ref, acc):
        t = pl.program_id(0)

        @pl.when(t == 0)
        def _():
            acc[...] = jnp.zeros_like(acc)

        av = a_ref[...].astype(BF16)
        bv = b_ref[...].astype(BF16)
        for j in range(NSH):
            acc[j] += _dot_tn(pick(av, a_mode, j, kb), pick(bv, b_mode, j, nb))

        @pl.when(t == nt - 1)
        def _():
            o_ref[...] = acc[...].astype(o_ref.dtype)

    return _pc(body, name=name, grid=(nt,),
               in_specs=[pl.BlockSpec((A.shape[0], tt, A.shape[2]), lambda t: (0, t, 0)),
                         pl.BlockSpec((B.shape[0], tt, B.shape[2]), lambda t: (0, t, 0)),
                         pl.BlockSpec(memory_space=pl.ANY)],
               out_specs=pl.BlockSpec((NSH, None, kb, nb), lambda t: (0, layer, 0, 0)),
               out_shape=S(buf.shape, buf.dtype), input_output_aliases={2: 0},
               scratch_shapes=[pltpu.VMEM((NSH, kb, nb), F32)], compiler_params=_cp(1))(A, B, buf)


def _mix_proj(h, pre_g, win, rot, layer):
    T = h.shape[0]
    tm = TM

    def body(h_ref, g_ref, w_ref, c_ref, s1_ref, s2_ref, p_ref, xn_ref):
        xb = _rms_fwd(h_ref[...], g_ref[...]).astype(BF16)
        xn_ref[...] = xb
        for j in range(NSH):
            o = _dot(xb, w_ref[j])
            p_ref[j] = _rot_fwd(o, c_ref[...], s1_ref[...], s2_ref[...]) if j < 2 else o

    row = pl.BlockSpec((tm, D), lambda i: (i, 0))
    half = pl.BlockSpec((tm, DA), lambda i: (i, 0))
    return _pc(body, name=f"mix_proj_l{layer}", grid=(T // tm,),
               in_specs=[row, _gain_spec(D, layer), pl.BlockSpec((NSH, None, D, DA), lambda i: (0, 0, 0, 0)),
                         half, half, half],
               out_specs=[pl.BlockSpec((NSH, tm, DA), lambda i: (0, i, 0)), row],
               out_shape=[S((NSH, T, DA), F32), S((T, D), BF16)], compiler_params=_cp(1))(h, pre_g, win, *rot)


def _mix_proj_bwd(dq, dk, dv, du, dh_up, h, pre_g, win, rot, layer):
    T = h.shape[0]
    tm = TM

    def body(dq_ref, dk_ref, dv_ref, du_ref, up_ref, h_ref, g_ref, w_ref, c_ref, s1_ref, s2_ref,
             dh_ref, dp_ref, dg_ref):
        @pl.when(pl.program_id(0) == 0)
        def _():
            dg_ref[...] = jnp.zeros_like(dg_ref)

        rot = (c_ref[...], s1_ref[...], s2_ref[...])
        dps = [_rot_bwd(dq_ref[...], *rot), _rot_bwd(dk_ref[...], *rot), dv_ref[...], du_ref[...]]
        dxn = None
        for j in range(NSH):
            dpb = dps[j].astype(BF16)
            dp_ref[j] = dpb
            part = _dot_nt(dpb, w_ref[j])
            dxn = part if dxn is None else dxn + part
        dx, dg = _rms_bwd(dxn, h_ref[...], g_ref[...])
        dg_ref[...] += dg
        dh_ref[...] = up_ref[...] + dx

    row = pl.BlockSpec((tm, D), lambda i: (i, 0))
    half = pl.BlockSpec((tm, DA), lambda i: (i, 0))
    return _pc(body, name=f"mix_proj_bwd_l{layer}", grid=(T // tm,),
               in_specs=[half, half, half, half, row, row, _gain_spec(D, layer),
                         pl.BlockSpec((NSH, None, D, DA), lambda i: (0, 0, 0, 0)), half, half, half],
               out_specs=[row, pl.BlockSpec((NSH, tm, DA), lambda i: (0, i, 0)), _row_acc_spec(D)],
               out_shape=[S((T, D), F32), S((NSH, T, DA), BF16), S((1, D), F32)],
               compiler_params=_cp(1))(dq, dk, dv, du, dh_up, h, pre_g, win, *rot)


def _stream_pos(d, axis):
    i = lax.broadcasted_iota(jnp.int32, (BAND, BAND), axis)
    if d == 16:
        return i
    if d == 4:
        return 4 * (i % 32) + i // 32
    return 16 * (i % 8) + i // 8


def _band_masks(b, d):
    qi, kj = _stream_pos(d, 0), _stream_pos(d, 1)
    return kj <= qi, (kj >= qi) & (b > 0)


def _pattern(d, T):
    n16 = T // 16
    if d == 16:
        return (16, n16, DA), (None, BAND, DA), lambda r, k: (r, k, 0)
    if d == 4:
        return (4, 4, n16, DA), (4, None, 32, DA), lambda r, k: (0, r, k, 0)
    return (16, n16, DA), (16, 8, DA), lambda r, k: (0, k, 0)


def _pattern_spec(d, T, kmap, lead=None):
    _, blk, idx = _pattern(d, T)
    if lead is None:
        return pl.BlockSpec(blk, lambda r, b: idx(r, kmap(b)))
    return pl.BlockSpec((None,) + blk, lambda r, b: (lead,) + idx(r, kmap(b)))


def _whole_stream_specs(T, n_plain):
    n16 = T // 16
    p_spec = lambda s: pl.BlockSpec((None, None, n16, DA), lambda r: (s, r, 0, 0))
    plain = pl.BlockSpec((None, n16, DA), lambda r: (r, 0, 0))
    return [p_spec(0), p_spec(1), p_spec(2)] + [plain] * n_plain, plain


def _stream_masks():
    qi = lax.broadcasted_iota(jnp.int32, (BAND, BAND), 0)
    kj = lax.broadcasted_iota(jnp.int32, (BAND, BAND), 1)
    mask_c = kj <= qi
    return mask_c, jnp.concatenate([kj >= qi, mask_c], axis=1)


def _attn_fwd_stream(P, layer):
    T = P.shape[1]
    n16 = T // 16
    nb = n16 // BAND
    scale = HD ** -0.5

    def body(q_ref, k_ref, v_ref, o_ref, l_ref, qs, ks, vs):
        for src, dst in ((q_ref, qs), (k_ref, ks), (v_ref, vs)):
            dst[...] = src[...].astype(BF16)
        mask_c, mask_pc = _stream_masks()
        for b in range(nb):
            rows = slice(b * BAND, (b + 1) * BAND)
            krows = slice(max(b - 1, 0) * BAND, (b + 1) * BAND)
            mask = mask_c if b == 0 else mask_pc
            for hd in range(NH):
                sl = slice(hd * HD, (hd + 1) * HD)
                s = jnp.where(mask, _dot_nt(qs[rows, sl], ks[krows, sl]) * scale, -1e30)
                m = jnp.max(s, axis=-1, keepdims=True)
                e = jnp.exp(s - m)
                den = jnp.sum(e, axis=-1, keepdims=True)
                o_ref[rows, sl] = _dot(e.astype(BF16), vs[krows, sl]) / den
                l_ref[rows, sl] = jnp.broadcast_to(m + jnp.log(den), (BAND, HD))

    ins, out = _whole_stream_specs(T, 0)
    Pv = P.reshape(NSH, 16, n16, DA)
    o, l = _pc(body, name=f"attn_fwd_d16_l{layer}", grid=(16,), in_specs=ins, out_specs=[out, out],
               out_shape=[S((16, n16, DA), F32)] * 2, scratch_shapes=[pltpu.VMEM((n16, DA), BF16)] * 3,
               compiler_params=_cp(1))(Pv, Pv, Pv)
    return o.reshape(T, DA), l.reshape(T, DA)


def _attn_bwd_stream(P, dO, lse, delta, acc, layer):
    T = P.shape[1]
    n16 = T // 16
    nb = n16 // BAND
    scale = HD ** -0.5
    first = acc is None

    def body(*refs):
        q_ref, k_ref, v_ref, do_ref, l_ref, dl_ref = refs[:6]
        if first:
            dq_ref, dk_ref, dv_ref = refs[6:9]
        else:
            aq_ref, ak_ref, av_ref, dq_ref, dk_ref, dv_ref = refs[6:12]
        qs, ks, vs, dos, okf, ovf = refs[-6:]
        for src, dst in ((q_ref, qs), (k_ref, ks), (v_ref, vs), (do_ref, dos)):
            dst[...] = src[...].astype(BF16)
        okf[...] = jnp.zeros_like(okf)
        ovf[...] = jnp.zeros_like(ovf)
        mask_c, mask_pc = _stream_masks()
        for b in range(nb):
            rows = slice(b * BAND, (b + 1) * BAND)
            krows = slice(max(b - 1, 0) * BAND, (b + 1) * BAND)
            mask = mask_c if b == 0 else mask_pc
            for hd in range(NH):
                sl = slice(hd * HD, (hd + 1) * HD)
                one = slice(hd * HD, hd * HD + 1)
                q, do, kk = qs[rows, sl], dos[rows, sl], ks[krows, sl]
                p = jnp.where(mask, jnp.exp(_dot_nt(q, kk) * scale - l_ref[rows, one]), 0.0)
                ds = (p * (_dot_nt(do, vs[krows, sl]) - dl_ref[rows, one]) * scale).astype(BF16)
                dq = _dot(ds, kk)
                dq_ref[rows, sl] = dq if first else aq_ref[rows, sl] + dq
                okf[krows, sl] += _dot_tn(ds, q)
                ovf[krows, sl] += _dot_tn(p.astype(BF16), do)
        dk_ref[...] = okf[...] if first else ak_ref[...] + okf[...]
        dv_ref[...] = ovf[...] if first else av_ref[...] + ovf[...]

    ins, out = _whole_stream_specs(T, 3 if first else 6)
    Pv = P.reshape(NSH, 16, n16, DA)
    view = lambda t: t.reshape(16, n16, DA)
    args = [Pv, Pv, Pv, view(dO), view(lse), view(delta)] + ([] if first else [view(t) for t in acc])
    dq, dk, dv = _pc(body, name=f"attn_bwd_d16_l{layer}", grid=(16,), in_specs=ins, out_specs=[out, out, out],
                     out_shape=[S((16, n16, DA), F32)] * 3,
                     scratch_shapes=[pltpu.VMEM((n16, DA), BF16)] * 4 + [pltpu.VMEM((n16, DA), F32)] * 2,
                     compiler_params=_cp(1))(*args)
    return dq.reshape(T, DA), dk.reshape(T, DA), dv.reshape(T, DA)


def _attn_fwd(P, d, layer):
    if d == 16:
        return _attn_fwd_stream(P, layer)
    T = P.shape[1]
    nb = T // d // BAND
    vshape = _pattern(d, T)[0]
    Pv = P.reshape((NSH,) + vshape)
    scale = HD ** -0.5

    def body(q_ref, kp_ref, kc_ref, vp_ref, vc_ref, o_ref, l_ref, qs, ks, vs, osc, lsc):
        b = pl.program_id(1)
        flat = lambda ref: ref[...].reshape(BAND, DA).astype(BF16)
        qs[...] = flat(q_ref)
        ks[0:BAND, :] = flat(kp_ref)
        ks[BAND:, :] = flat(kc_ref)
        vs[0:BAND, :] = flat(vp_ref)
        vs[BAND:, :] = flat(vc_ref)
        mask_c, mask_p = _band_masks(b, d)
        mask = jnp.concatenate([mask_p, mask_c], axis=1)
        for hd in range(NH):
            sl = slice(hd * HD, (hd + 1) * HD)
            s = jnp.where(mask, _dot_nt(qs[:, sl], ks[:, sl]) * scale, -1e30)
            m = jnp.max(s, axis=-1, keepdims=True)
            e = jnp.exp(s - m)
            den = jnp.sum(e, axis=-1, keepdims=True)
            osc[:, sl] = _dot(e.astype(BF16), vs[:, sl]) / den
            lsc[:, sl] = jnp.broadcast_to(m + jnp.log(den), (BAND, HD))
        o_ref[...] = osc[...].reshape(o_ref.shape)
        l_ref[...] = lsc[...].reshape(l_ref.shape)

    cur = lambda b: b
    prev = lambda b: jnp.maximum(b - 1, 0)
    out = _pattern_spec(d, T, cur)
    o, l = _pc(body, name=f"attn_fwd_d{d}_l{layer}", grid=(d, nb),
               in_specs=[_pattern_spec(d, T, cur, 0), _pattern_spec(d, T, prev, 1), _pattern_spec(d, T, cur, 1),
                         _pattern_spec(d, T, prev, 2), _pattern_spec(d, T, cur, 2)],
               out_specs=[out, out], out_shape=[S(vshape, F32)] * 2,
               scratch_shapes=[pltpu.VMEM((BAND, DA), BF16)] + [pltpu.VMEM((2 * BAND, DA), BF16)] * 2
               + [pltpu.VMEM((BAND, DA), F32)] * 2,
               compiler_params=_cp(2))(Pv, Pv, Pv, Pv, Pv)
    return o.reshape(T, DA), l.reshape(T, DA)


def _attn_bwd(P, dO, lse, delta, acc, d, layer):
    if d == 16:
        return _attn_bwd_stream(P, dO, lse, delta, acc, layer)
    T = P.shape[1]
    nb = T // d // BAND
    vshape = _pattern(d, T)[0]
    Pv = P.reshape((NSH,) + vshape)
    scale = HD ** -0.5
    first = acc is None

    def body(*refs):
        q_ref, kp_ref, kc_ref, vp_ref, vc_ref, do_ref, l_ref, dl_ref = refs[:8]
        if first:
            dq_ref, dk_ref, dv_ref = refs[8:11]
        else:
            aq_ref, ak_ref, av_ref, dq_ref, dk_ref, dv_ref = refs[8:14]
        qs, dos, ks, vs, ls, dls, oq, ok, ov, ck, cv = refs[-11:]
        b = pl.program_id(1)
        flat = lambda ref: ref[...].reshape(BAND, DA)

        @pl.when(b == 0)
        def _():
            ck[...] = jnp.zeros_like(ck)
            cv[...] = jnp.zeros_like(cv)

        @pl.when(b < nb)
        def _():
            qs[...] = flat(q_ref).astype(BF16)
            dos[...] = flat(do_ref).astype(BF16)
            ks[0:BAND, :] = flat(kp_ref).astype(BF16)
            ks[BAND:, :] = flat(kc_ref).astype(BF16)
            vs[0:BAND, :] = flat(vp_ref).astype(BF16)
            vs[BAND:, :] = flat(vc_ref).astype(BF16)
            ls[...] = flat(l_ref)
            dls[...] = flat(dl_ref)
            mask_c, mask_p = _band_masks(b, d)
            mask = jnp.concatenate([mask_p, mask_c], axis=1)
            for hd in range(NH):
                sl = slice(hd * HD, (hd + 1) * HD)
                one = slice(hd * HD, hd * HD + 1)
                q, do, kk = qs[:, sl], dos[:, sl], ks[:, sl]
                p = jnp.where(mask, jnp.exp(_dot_nt(q, kk) * scale - ls[:, one]), 0.0)
                ds = (p * (_dot_nt(do, vs[:, sl]) - dls[:, one]) * scale).astype(BF16)
                oq[:, sl] = _dot(ds, kk)
                dk2 = _dot_tn(ds, q)
                dv2 = _dot_tn(p.astype(BF16), do)
                ok[:, sl] = ck[:, sl] + dk2[0:BAND]
                ov[:, sl] = cv[:, sl] + dv2[0:BAND]
                ck[:, sl] = dk2[BAND:]
                cv[:, sl] = dv2[BAND:]
            if first:
                dq_ref[...] = oq[...].reshape(dq_ref.shape)
                dk_ref[...] = ok[...].reshape(dk_ref.shape)
                dv_ref[...] = ov[...].reshape(dv_ref.shape)
            else:
                dq_ref[...] = aq_ref[...] + oq[...].reshape(dq_ref.shape)
                dk_ref[...] = ak_ref[...] + ok[...].reshape(dk_ref.shape)
                dv_ref[...] = av_ref[...] + ov[...].reshape(dv_ref.shape)

        @pl.when(b == nb)
        def _():
            if first:
                dk_ref[...] = ck[...].reshape(dk_ref.shape)
                dv_ref[...] = cv[...].reshape(dv_ref.shape)
            else:
                dk_ref[...] = ak_ref[...] + ck[...].reshape(dk_ref.shape)
                dv_ref[...] = av_ref[...] + cv[...].reshape(dv_ref.shape)

    qb = lambda b: jnp.minimum(b, nb - 1)
    qprev = lambda b: jnp.maximum(qb(b) - 1, 0)
    kb = lambda b: jnp.maximum(b - 1, 0)
    qrow = _pattern_spec(d, T, qb)
    krow = _pattern_spec(d, T, kb)
    view = lambda t: t.reshape(vshape)
    ins = [Pv, Pv, Pv, Pv, Pv, view(dO), view(lse), view(delta)]
    specs = [_pattern_spec(d, T, qb, 0), _pattern_spec(d, T, qprev, 1), _pattern_spec(d, T, qb, 1),
             _pattern_spec(d, T, qprev, 2), _pattern_spec(d, T, qb, 2), qrow, qrow, qrow]
    if not first:
        ins += [view(t) for t in acc]
        specs += [qrow, krow, krow]
    dq, dk, dv = _pc(body, name=f"attn_bwd_d{d}_l{layer}", grid=(d, nb + 1), in_specs=specs,
                     out_specs=[qrow, krow, krow], out_shape=[S(vshape, F32)] * 3,
                     scratch_shapes=[pltpu.VMEM((BAND, DA), BF16)] * 2 + [pltpu.VMEM((2 * BAND, DA), BF16)] * 2
                     + [pltpu.VMEM((BAND, DA), F32)] * 7,
                     compiler_params=_cp(2))(*ins)
    return dq.reshape(T, DA), dk.reshape(T, DA), dv.reshape(T, DA)


def _ssm_prep(lam_re, lam_im, log_dt, b_re, b_im, c_re, c_im):
    dt = jnp.exp(log_dt)[:, None]
    er = jnp.exp(lam_re * dt)
    a_re = er * jnp.cos(lam_im * dt)
    a_im = er * jnp.sin(lam_im * dt)
    nr, ni = a_re - 1.0, a_im
    den = lam_re * lam_re + lam_im * lam_im
    cr = (nr * lam_re + ni * lam_im) / den
    ci = (ni * lam_re - nr * lam_im) / den
    bbr = cr[..., None] * b_re - ci[..., None] * b_im
    bbi = cr[..., None] * b_im + ci[..., None] * b_re
    eye = jnp.eye(8, dtype=F32)

    def bblock(bb):
        t = bb.reshape(4, 8, 64, 16).transpose(0, 1, 3, 2)
        return (t[:, :, :, None, :] * eye[None, :, None, :, None]).reshape(4, 128, 512)

    def cblock(cc):
        t = cc.reshape(4, 8, 16, 64).transpose(0, 1, 3, 2)
        return (t[:, :, :, None, :] * eye[None, :, None, :, None]).reshape(4, 512, 128)

    return (a_re.reshape(NLB, 1, 128), a_im.reshape(NLB, 1, 128), bblock(bbr), bblock(bbi), cblock(c_re), cblock(c_im))


def _perm_matrix(tm):
    n = tm // 16
    pm = np.zeros((tm, tm), np.float32)
    for r in range(16):
        pm[16 * np.arange(n) + r, r * n + np.arange(n)] = 1.0
    return jnp.asarray(pm, BF16)


def _pieces(x):
    p1 = x.astype(BF16)
    r1 = x - p1.astype(F32)
    p2 = r1.astype(BF16)
    return p1, p2, (r1 - p2.astype(F32)).astype(BF16)


def _to_time(x, pm):
    return sum(_dot(pm, p) for p in _pieces(x))


def _to_streams(x, pm):
    return sum(_dot_tn(pm, p) for p in _pieces(x))


def _stream_block(tm, cols, lead=None):
    if lead is None:
        return pl.BlockSpec((16, tm // 16, cols), lambda i: (0, i, 0))
    return pl.BlockSpec((None, 16, tm // 16, cols), lambda i: (lead, 0, i, 0))


def _reorder(t3, to_streams, name):
    B, T, C = t3.shape
    tm = TM

    def body(x_ref, pm_ref, o_ref):
        if to_streams:
            o_ref[...] = _to_streams(x_ref[...], pm_ref[...]).reshape(o_ref.shape)
        else:
            o_ref[...] = _to_time(x_ref[...].reshape(tm, C), pm_ref[...])

    time_blk = pl.BlockSpec((None, tm, C), lambda b, i: (b, i, 0))
    stream_blk = pl.BlockSpec((None, 16, tm // 16, C), lambda b, i: (b, 0, i, 0))
    src = t3 if to_streams else t3.reshape(B, 16, T // 16, C)
    out = _pc(body, name=name, grid=(B, T // tm),
              in_specs=[time_blk if to_streams else stream_blk, pl.BlockSpec((tm, tm), lambda b, i: (0, 0))],
              out_specs=stream_blk if to_streams else time_blk,
              out_shape=S((B, 16, T // 16, C) if to_streams else (B, T, C), F32),
              compiler_params=_cp(2))(src, _perm_matrix(tm))
    return out.reshape(B, T, C)


def _ssm_in(P, bre, bim, layer):
    T = P.shape[1]
    tm = TM

    def body(u_ref, pm_ref, br_ref, bi_ref, un_ref, or_ref, oi_ref):
        u = _to_time(u_ref[...].reshape(tm, DSS), pm_ref[...])
        un_ref[...] = u
        for s in range(4):
            uc = u[:, s * 128:(s + 1) * 128]
            r = _dot2(_dot, uc, br_ref[s], "b")
            m = _dot2(_dot, uc, bi_ref[s], "b")
            for q in range(4):
                or_ref[4 * s + q] = r[:, q * 128:(q + 1) * 128]
                oi_ref[4 * s + q] = m[:, q * 128:(q + 1) * 128]

    whole = pl.BlockSpec((4, 128, 512), lambda i: (0, 0, 0))
    st = pl.BlockSpec((NLB, tm, 128), lambda i: (0, i, 0))
    return _pc(body, name=f"ssm_in_l{layer}", grid=(T // tm,),
               in_specs=[_stream_block(tm, DSS, 3), pl.BlockSpec((tm, tm), lambda i: (0, 0)), whole, whole],
               out_specs=[pl.BlockSpec((tm, DSS), lambda i: (i, 0)), st, st],
               out_shape=[S((T, DSS), F32)] + [S((NLB, T, 128), F32)] * 2,
               compiler_params=_cp(1))(P.reshape(NSH, 16, T // 16, DSS), _perm_matrix(tm), bre, bim)


def _scan(br, bi, a_re, a_im, reverse, layer):
    T = br.shape[1]
    nbk = 4
    tt = min(T, 1024)
    nT = T // tt
    ntile = tt // 8
    sgn = -1.0 if reverse else 1.0
    last = 0 if reverse else 7

    def body(br_ref, bi_ref, ar_ref, ai_ref, xr_ref, xi_ref, cr, ci):
        @pl.when(pl.program_id(1) == 0)
        def _():
            cr[...] = jnp.zeros_like(cr)
            ci[...] = jnp.zeros_like(ci)

        row = lax.broadcasted_iota(jnp.int32, (8, 128), 0)
        consts = []
        for k in range(nbk):
            a1r = jnp.broadcast_to(ar_ref[k], (8, 128))
            a1i = sgn * jnp.broadcast_to(ai_ref[k], (8, 128))
            pows = [(a1r, a1i)]
            for _ in range(7):
                pr, pi_ = pows[-1]
                pows.append((a1r * pr - a1i * pi_, a1r * pi_ + a1i * pr))
            rounds = []
            for s in (1, 2, 4):
                inside = (row <= 7 - s) if reverse else (row >= s)
                rounds.append((jnp.where(inside, pows[s - 1][0], 0.0), jnp.where(inside, pows[s - 1][1], 0.0)))
            cmr, cmi = jnp.zeros((8, 128), F32), jnp.zeros((8, 128), F32)
            for r in range(8):
                e = (7 - r) if reverse else r
                cmr = jnp.where(row == r, pows[e][0], cmr)
                cmi = jnp.where(row == r, pows[e][1], cmi)
            consts.append((rounds, cmr, cmi))

        def tile(i, carry):
            j = (ntile - 1 - i) if reverse else i
            rows = pl.ds(pl.multiple_of(j * 8, 8), 8)
            out = []
            for k in range(nbk):
                rounds, cmr, cmi = consts[k]
                xr = br_ref[k, rows, :]
                xi = bi_ref[k, rows, :]
                for (mr, mi), s in zip(rounds, (1, 2, 4)):
                    sh = (8 - s) if reverse else s
                    rr = pltpu.roll(xr, sh, 0)
                    ri = pltpu.roll(xi, sh, 0)
                    xr, xi = xr + (mr * rr - mi * ri), xi + (mr * ri + mi * rr)
                c_r, c_i = carry[k]
                xr, xi = xr + (cmr * c_r - cmi * c_i), xi + (cmr * c_i + cmi * c_r)
                xr_ref[k, rows, :] = xr
                xi_ref[k, rows, :] = xi
                out.append((jnp.broadcast_to(xr[last:last + 1, :], (8, 128)),
                            jnp.broadcast_to(xi[last:last + 1, :], (8, 128))))
            return tuple(out)

        carry = lax.fori_loop(0, ntile, tile, tuple((cr[k], ci[k]) for k in range(nbk)), unroll=2)
        for k in range(nbk):
            cr[k] = carry[k][0]
            ci[k] = carry[k][1]

    tmap = (lambda t: nT - 1 - t) if reverse else (lambda t: t)
    st = pl.BlockSpec((nbk, tt, 128), lambda i, t: (i, tmap(t), 0))
    av = pl.BlockSpec((nbk, 1, 128), lambda i, t: (i, 0, 0))
    return _pc(body, name=f"scan_{'bwd' if reverse else 'fwd'}_l{layer}", grid=(NLB // nbk, nT),
               in_specs=[st, st, av, av], out_specs=[st, st], out_shape=[S((NLB, T, 128), F32)] * 2,
               scratch_shapes=[pltpu.VMEM((nbk, 8, 128), F32)] * 2, compiler_params=_cp(2))(br, bi, a_re, a_im)


def _ssm_out(xr, xi, u, cre, cim, dvec, wglu, bglu, layer):
    T = u.shape[0]
    tm = TM

    def body(xr_ref, xi_ref, u_ref, pm_ref, cr_ref, ci_ref, d_ref, w_ref, bg_ref, s_ref, y_ref, z_ref):
        ys = []
        for s in range(4):
            xrc = jnp.concatenate([xr_ref[4 * s + q] for q in range(4)], axis=1)
            xic = jnp.concatenate([xi_ref[4 * s + q] for q in range(4)], axis=1)
            ys.append(_dot2(_dot, xrc, cr_ref[s], "b") - _dot2(_dot, xic, ci_ref[s], "b"))
        y = jnp.concatenate(ys, axis=1) + d_ref[...] * u_ref[...]
        yg = _gelu(y)
        ygb = yg.astype(BF16)
        z = bg_ref[...] + sum(_dot(ygb[:, j * 128:(j + 1) * 128], w_ref[j]) for j in range(NSH))
        y_ref[...] = y
        z_ref[...] = z
        s_ref[...] = _to_streams(yg * jax.nn.sigmoid(z), pm_ref[...]).reshape(s_ref.shape)

    st = pl.BlockSpec((NLB, tm, 128), lambda i: (0, i, 0))
    cw = pl.BlockSpec((4, 512, 128), lambda i: (0, 0, 0))
    half = pl.BlockSpec((tm, DSS), lambda i: (i, 0))
    s, y, z = _pc(body, name=f"ssm_out_l{layer}", grid=(T // tm,),
                  in_specs=[st, st, half, pl.BlockSpec((tm, tm), lambda i: (0, 0)), cw, cw, _gain_spec(DSS, layer),
                            pl.BlockSpec((NSH, None, 128, DSS), lambda i: (0, 0, 0, 0)), _gain_spec(DSS, layer)],
                  out_specs=[_stream_block(tm, DSS), half, half],
                  out_shape=[S((16, T // 16, DSS), F32), S((T, DSS), F32), S((T, DSS), F32)],
                  compiler_params=_cp(1))(xr, xi, u, _perm_matrix(tm), cre, cim, dvec, wglu, bglu)
    return s.reshape(T, DSS), y, z


def _ssm_out_bwd(dssm, y, z, xr, xi, u, cre, cim, dvec, wglu, layer):
    T = u.shape[0]
    tm = TM

    def body(ds_ref, pm_ref, y_ref, z_ref, xr_ref, xi_ref, u_ref, cr_ref, ci_ref, d_ref, w_ref,
             gr_ref, gi_ref, du_ref, dz_ref, yg_ref, dbg_ref, dd_ref, dcr_ref, dci_ref):
        i = pl.program_id(0)

        @pl.when(i == 0)
        def _():
            dbg_ref[...] = jnp.zeros_like(dbg_ref)
            dd_ref[...] = jnp.zeros_like(dd_ref)
            dcr_ref[...] = jnp.zeros_like(dcr_ref)
            dci_ref[...] = jnp.zeros_like(dci_ref)

        yv = y_ref[...]
        yg = _gelu(yv)
        sg = jax.nn.sigmoid(z_ref[...])
        ds = _to_time(ds_ref[...].reshape(tm, DSS), pm_ref[...])
        dz = ds * yg * sg * (1.0 - sg)
        dzb = dz.astype(BF16)
        dz_ref[...] = dzb
        yg_ref[...] = yg.astype(BF16)
        dbg_ref[...] += jnp.sum(dz, axis=0, keepdims=True)
        dyg = ds * sg + jnp.concatenate([_dot_nt(dzb, w_ref[j]) for j in range(NSH)], axis=1)
        dy = dyg * _gelu_grad(yv)
        u = u_ref[...]
        dd_ref[...] += jnp.sum(dy * u, axis=0, keepdims=True)
        du_ref[...] = dy * d_ref[...]
        for s in range(4):
            dyc = dy[:, s * 128:(s + 1) * 128]
            g_r = _dot2(_dot_nt, dyc, cr_ref[s], "b")
            g_i = -_dot2(_dot_nt, dyc, ci_ref[s], "b")
            for q in range(4):
                gr_ref[4 * s + q] = g_r[:, q * 128:(q + 1) * 128]
                gi_ref[4 * s + q] = g_i[:, q * 128:(q + 1) * 128]
            xrc = jnp.concatenate([xr_ref[4 * s + q] for q in range(4)], axis=1)
            xic = jnp.concatenate([xi_ref[4 * s + q] for q in range(4)], axis=1)
            dcr_ref[s] += _dot2(_dot_tn, xrc, dyc, "a")
            dci_ref[s] -= _dot2(_dot_tn, xic, dyc, "a")

    st = pl.BlockSpec((NLB, tm, 128), lambda i: (0, i, 0))
    cw = pl.BlockSpec((4, 512, 128), lambda i: (0, 0, 0))
    half = pl.BlockSpec((tm, DSS), lambda i: (i, 0))
    return _pc(body, name=f"ssm_out_bwd_l{layer}", grid=(T // tm,),
               in_specs=[_stream_block(tm, DSS), pl.BlockSpec((tm, tm), lambda i: (0, 0)), half, half, st, st, half,
                         cw, cw, _gain_spec(DSS, layer), pl.BlockSpec((NSH, None, 128, DSS), lambda i: (0, 0, 0, 0))],
               out_specs=[st, st, half, half, half, _row_acc_spec(DSS), _row_acc_spec(DSS), cw, cw],
               out_shape=[S((NLB, T, 128), F32)] * 2 + [S((T, DSS), F32), S((T, DSS), BF16), S((T, DSS), BF16),
                                                        S((1, DSS), F32), S((1, DSS), F32),
                                                        S((4, 512, 128), F32), S((4, 512, 128), F32)],
               compiler_params=_cp(1))(dssm.reshape(16, T // 16, DSS), _perm_matrix(tm), y, z, xr, xi, u, cre, cim,
                                       dvec, wglu)


def _ssm_da(gr, gi, xr, xi, layer):
    T = gr.shape[1]
    tb = 4096 if T % 4096 == 0 else T

    def body(gr_ref, gi_ref, xr_ref, xi_ref, dr_ref, di_ref, lr, li):
        t = pl.program_id(1)

        @pl.when(t == 0)
        def _():
            dr_ref[...] = jnp.zeros_like(dr_ref)
            di_ref[...] = jnp.zeros_like(di_ref)
            lr[...] = jnp.zeros_like(lr)
            li[...] = jnp.zeros_like(li)

        g_r, g_i, x_r, x_i = gr_ref[...], gi_ref[...], xr_ref[...], xi_ref[...]
        pr = pltpu.roll(x_r, 1, 0)
        pi_ = pltpu.roll(x_i, 1, 0)
        g0r, g0i = g_r[0:1, :], g_i[0:1, :]
        fr = lr[7:8, :] - x_r[tb - 1:tb, :]
        fi = li[7:8, :] - x_i[tb - 1:tb, :]
        dr_ref[...] += jnp.sum(g_r * pr + g_i * pi_, axis=0, keepdims=True) + g0r * fr + g0i * fi
        di_ref[...] += jnp.sum(g_i * pr - g_r * pi_, axis=0, keepdims=True) + g0i * fr - g0r * fi
        lr[...] = x_r[tb - 8:tb, :]
        li[...] = x_i[tb - 8:tb, :]

    st = pl.BlockSpec((None, tb, 128), lambda k, t: (k, t, 0))
    out = pl.BlockSpec((None, 1, 128), lambda k, t: (k, 0, 0))
    return _pc(body, name=f"ssm_da_l{layer}", grid=(NLB, T // tb), in_specs=[st] * 4, out_specs=[out, out],
               out_shape=[S((NLB, 1, 128), F32)] * 2, scratch_shapes=[pltpu.VMEM((8, 128), F32)] * 2,
               compiler_params=_cp(2))(gr, gi, xr, xi)


def _ssm_in_bwd(gr, gi, u, bre, bim, du_direct, layer):
    T = u.shape[0]
    tm = TM

    def body(gr_ref, gi_ref, u_ref, pm_ref, br_ref, bi_ref, dd_ref, du_ref, dbr_ref, dbi_ref):
        i = pl.program_id(0)

        @pl.when(i == 0)
        def _():
            dbr_ref[...] = jnp.zeros_like(dbr_ref)
            dbi_ref[...] = jnp.zeros_like(dbi_ref)

        dus = []
        for s in range(4):
            grc = jnp.concatenate([gr_ref[4 * s + q] for q in range(4)], axis=1)
            gic = jnp.concatenate([gi_ref[4 * s + q] for q in range(4)], axis=1)
            uc = u_ref[:, s * 128:(s + 1) * 128]
            dus.append(_dot2(_dot_nt, grc, br_ref[s], "b") + _dot2(_dot_nt, gic, bi_ref[s], "b"))
            dbr_ref[s] += _dot2(_dot_tn, uc, grc, "a")
            dbi_ref[s] += _dot2(_dot_tn, uc, gic, "a")
        du = jnp.concatenate(dus, axis=1) + dd_ref[...]
        du_ref[...] = _to_streams(du, pm_ref[...]).reshape(du_ref.shape)

    whole = pl.BlockSpec((4, 128, 512), lambda i: (0, 0, 0))
    st = pl.BlockSpec((NLB, tm, 128), lambda i: (0, i, 0))
    half = pl.BlockSpec((tm, DSS), lambda i: (i, 0))
    du, dbr, dbi = _pc(body, name=f"ssm_in_bwd_l{layer}", grid=(T // tm,),
                       in_specs=[st, st, half, pl.BlockSpec((tm, tm), lambda i: (0, 0)), whole, whole, half],
                       out_specs=[_stream_block(tm, DSS), whole, whole],
                       out_shape=[S((16, T // 16, DSS), F32), S((4, 128, 512), F32), S((4, 128, 512), F32)],
                       compiler_params=_cp(1))(gr, gi, u, _perm_matrix(tm), bre, bim, du_direct)
    return du.reshape(T, DSS), dbr, dbi


def _mix_out(outs, lses, ssm, h, attn_g, ssm_g, post_g, wout, layer):
    T = h.shape[0]
    tm = TM

    def body(o1, o2, o3, l1, l2, l3, s_ref, h_ref, ag_ref, sg_ref, pg_ref, w_ref, ho_ref, at_ref, ls_ref, mx_ref, mo_ref):
        la, lb, lc = l1[...], l2[...], l3[...]
        m = jnp.maximum(jnp.maximum(la, lb), lc)
        wa, wb, wc = jnp.exp(la - m), jnp.exp(lb - m), jnp.exp(lc - m)
        zs = wa + wb + wc
        attn = (wa * o1[...] + wb * o2[...] + wc * o3[...]) / zs
        at_ref[...] = attn
        ls_ref[...] = m + jnp.log(zs)
        mixed = jnp.concatenate([_rms_fwd(attn, ag_ref[...]), _rms_fwd(s_ref[...], sg_ref[...])], axis=1).astype(BF16)
        mx_ref[...] = mixed
        mo = sum(_dot(mixed[:, j * 256:(j + 1) * 256], w_ref[j]) for j in range(NSH))
        mo_ref[...] = mo
        ho_ref[...] = h_ref[...] + _rms_fwd(mo, pg_ref[...])

    row = pl.BlockSpec((tm, D), lambda i: (i, 0))
    half = pl.BlockSpec((tm, DA), lambda i: (i, 0))
    return _pc(body, name=f"mix_out_l{layer}", grid=(T // tm,),
               in_specs=[half] * 7 + [row, _gain_spec(DA, layer), _gain_spec(DSS, layer), _gain_spec(D, layer),
                                      pl.BlockSpec((NSH, None, 256, D), lambda i: (0, 0, 0, 0))],
               out_specs=[row, half, half, row, row],
               out_shape=[S((T, D), F32), S((T, DA), F32), S((T, DA), F32), S((T, D), BF16), S((T, D), F32)],
               compiler_params=_cp(1))(*outs, *lses, ssm, h, attn_g, ssm_g, post_g, wout)


def _mix_out_bwd(dout, mo, attn, ssm, attn_g, ssm_g, post_g, wout, layer):
    T = dout.shape[0]
    tm = TM
    head_sum =jnp.asarray(np.kron(np.eye(NH, dtype=np.float32), np.ones((HD, HD), np.float32)), BF16)

    def body(do_ref, mo_ref, at_ref, s_ref, ag_ref, sg_ref, pg_ref, w_ref, e_ref,
             da_ref, ds_ref, dl_ref, dmo_ref, dpg_ref, dag_ref, dsg_ref):
        i = pl.program_id(0)

        @pl.when(i == 0)
        def _():
            dpg_ref[...] = jnp.zeros_like(dpg_ref)
            dag_ref[...] = jnp.zeros_like(dag_ref)
            dsg_ref[...] = jnp.zeros_like(dsg_ref)

        dmo, dpg = _rms_bwd(do_ref[...], mo_ref[...], pg_ref[...])
        dpg_ref[...] += dpg
        dmob = dmo.astype(BF16)
        dmo_ref[...] = dmob
        dmix = jnp.concatenate([_dot_nt(dmob, w_ref[j]) for j in range(NSH)], axis=1)
        attn = at_ref[...]
        dat, dag = _rms_bwd(dmix[:, :DA], attn, ag_ref[...])
        dss, dsg = _rms_bwd(dmix[:, DA:], s_ref[...], sg_ref[...])
        dag_ref[...] += dag
        dsg_ref[...] += dsg
        da_ref[...] = dat
        ds_ref[...] = dss
        prod = dat * attn
        p1 = prod.astype(BF16)
        r1 = prod - p1.astype(F32)
        p2 = r1.astype(BF16)
        p3 = (r1 - p2.astype(F32)).astype(BF16)
        e = e_ref[...]
        dl_ref[...] = _dot(p1, e) + _dot(p2, e) + _dot(p3, e)

    row = pl.BlockSpec((tm, D), lambda i: (i, 0))
    half = pl.BlockSpec((tm, DA), lambda i: (i, 0))
    return _pc(body, name=f"mix_out_bwd_l{layer}", grid=(T // tm,),
               in_specs=[row, row, half, half, _gain_spec(DA, layer), _gain_spec(DSS, layer), _gain_spec(D, layer),
                         pl.BlockSpec((NSH, None, 256, D), lambda i: (0, 0, 0, 0)),
                         pl.BlockSpec((DA, DA), lambda i: (0, 0))],
               out_specs=[half, half, half, row, _row_acc_spec(D), _row_acc_spec(DA), _row_acc_spec(DSS)],
               out_shape=[S((T, DA), F32)] * 3 + [S((T, D), BF16), S((1, D), F32), S((1, DA), F32), S((1, DSS), F32)],
               compiler_params=_cp(1))(dout, mo, attn, ssm, attn_g, ssm_g, post_g, wout, head_sum)


def _ple_fwd(h, p3, wup, wgate, post_g, layer):
    T = h.shape[0]
    tm = TM

    def body(h_ref, p_ref, wu_ref, wg_ref, g_ref, ho_ref, e_ref, gt_ref):
        hv = h_ref[...]
        hb = hv.astype(BF16)
        pb = p_ref[...].astype(BF16)
        gte = sum(_dot(hb[:, j * 256:(j + 1) * 256], wg_ref[j]) for j in range(NSH))
        e = jnp.concatenate([_dot(pb, wu_ref[j]) for j in range(NSH)], axis=1)
        e_ref[...] = e
        gt_ref[...] = gte
        ho_ref[...] = hv + _rms_fwd(e * jax.nn.sigmoid(gte), g_ref[...])

    row = pl.BlockSpec((tm, D), lambda i: (i, 0))
    return _pc(body, name=f"ple_fwd_l{layer}", grid=(T // tm,),
               in_specs=[row, pl.BlockSpec((None, tm, PLE), lambda i: (layer, i, 0)),
                         pl.BlockSpec((NSH, None, PLE, 256), lambda i: (0, 0, 0, 0)),
                         pl.BlockSpec((NSH, None, 256, D), lambda i: (0, 0, 0, 0)), _gain_spec(D, layer)],
               out_specs=[row, row, row], out_shape=[S((T, D), F32)] * 3,
               compiler_params=_cp(1))(h, p3, wup, wgate, post_g)


def _ple_bwd(dout, e, gte, wgate, post_g, layer):
    T = dout.shape[0]
    tm = TM

    def body(do_ref, e_ref, gt_ref, wg_ref, g_ref, dh_ref, de_ref, dgt_ref, dg_ref):
        i = pl.program_id(0)

        @pl.when(i == 0)
        def _():
            dg_ref[...] = jnp.zeros_like(dg_ref)

        ev = e_ref[...]
        sg = jax.nn.sigmoid(gt_ref[...])
        do = do_ref[...]
        dple, dg = _rms_bwd(do, ev * sg, g_ref[...])
        dg_ref[...] += dg
        de = (dple * sg).astype(BF16)
        for j in range(NSH):
            de_ref[j] = de[:, j * 256:(j + 1) * 256]
        dgb = (dple * ev * sg * (1.0 - sg)).astype(BF16)
        dgt_ref[...] = dgb
        dh_ref[...] = do + jnp.concatenate([_dot_nt(dgb, wg_ref[j]) for j in range(NSH)], axis=1)

    row = pl.BlockSpec((tm, D), lambda i: (i, 0))
    return _pc(body, name=f"ple_bwd_l{layer}", grid=(T // tm,),
               in_specs=[row, row, row, pl.BlockSpec((NSH, None, 256, D), lambda i: (0, 0, 0, 0)), _gain_spec(D, layer)],
               out_specs=[row, pl.BlockSpec((NSH, tm, 256), lambda i: (0, i, 0)), row, _row_acc_spec(D)],
               out_shape=[S((T, D), F32), S((NSH, T, 256), BF16), S((T, D), BF16), S((1, D), F32)],
               compiler_params=_cp(1))(dout, e, gte, wgate, post_g)


def _loss_head(h, target):
    T = h.shape[0]
    tm = TM

    def body(h_ref, t_ref, dy_ref, l_ref):
        i = pl.program_id(0)

        @pl.when(i == 0)
        def _():
            l_ref[...] = jnp.zeros_like(l_ref)

        err = h_ref[...] - t_ref[...]
        dy_ref[...] = err * (1.0 / D)
        l_ref[...] += jnp.broadcast_to((0.5 / D) * jnp.sum(err * err), (1, 128))

    row = pl.BlockSpec((tm, D), lambda i: (i, 0))
    return _pc(body, name="loss_head", grid=(T // tm,), in_specs=[row, row],
               out_specs=[row, pl.BlockSpec((1, 128), lambda i: (0, 0))],
               out_shape=[S((T, D), F32), S((1, 128), F32)], compiler_params=_cp(1))(h, target)


def _local_step(x, p3, pos_col, target, weights_of, layer_grads_done, Sm):
    L = p3.shape[0]
    g3 = {n: Sm[n].reshape(L, 1, -1) for n in ("ffn1_pre_g", "ffn1_post_g", "mix_pre_g", "attn_norm_g", "ssm_norm_g",
                                                "mix_post_g", "ffn2_pre_g", "ffn2_post_g", "ple_post_g", "ssm_b_glu", "ssm_d")}
    rot = _rot_tables(pos_col)
    prep_names = ("ssm_lam_re", "ssm_lam_im", "ssm_log_dt", "ssm_b_re", "ssm_b_im", "ssm_c_re", "ssm_c_im")
    prep_all, prep_vjp = jax.vjp(jax.vmap(_ssm_prep), *[Sm[n] for n in prep_names])
    prep_cot = [None] * L

    saved = []
    h = x
    for l in range(L):
        W = weights_of(l, h)
        sv = {"h0": h, "W": W}
        h, sv["a1"], sv["b1"], sv["f1"], sv["xn1"] = _ffn_fwd(
            h, g3["ffn1_pre_g"], g3["ffn1_post_g"], W["ffn1_w_gate"], W["ffn1_w_up"], W["ffn1_w_down"], l, "1")
        sv["h1"] = h
        P, sv["ain"] = _mix_proj(h, g3["mix_pre_g"], W["w_in"], rot, l)
        sv["P"] = P
        ol = [_attn_fwd(P, d, l) for d in PATTERN_DILATIONS]
        prep = tuple(t[l] for t in prep_all)
        a_re, a_im, bre, bim, cre, cim = prep
        sv["prep"] = prep
        sv["u"], bur, bui = _ssm_in(P, bre, bim, l)
        xr, xi = _scan(bur, bui, a_re, a_im, False, l)
        sv["xr"], sv["xi"] = xr, xi
        ssm, sv["y"], sv["z"] = _ssm_out(xr, xi, sv["u"], cre, cim, g3["ssm_d"], W["ssm_w_glu"], g3["ssm_b_glu"], l)
        sv["ssm"] = ssm
        h, sv["attn"], sv["lse"], sv["mixed"], sv["mo"] = _mix_out(
            [o for o, _ in ol], [s for _, s in ol], ssm, h, g3["attn_norm_g"], g3["ssm_norm_g"], g3["mix_post_g"],
            W["w_out"], l)
        sv["h2"] = h
        h, sv["a2"], sv["b2"], sv["f2"], sv["xn2"] = _ffn_fwd(
            h, g3["ffn2_pre_g"], g3["ffn2_post_g"], W["ffn2_w_gate"], W["ffn2_w_up"], W["ffn2_w_down"], l, "2")
        sv["h3"] = h
        h, sv["e"], sv["gte"] = _ple_fwd(h, p3, W["ple_w_up"], W["ple_w_gate"], g3["ple_post_g"], l)
        saved.append(sv)

    dh, loss = _loss_head(h, target)

    G_layers = [{n: lax.empty((NSH, 1, r, c), BF16) for n, r, c in BIG} for _ in range(L)]
    sg = {n: [None] * L for n in SMALL}
    whole, shard, kcol = "whole", "shard", "cols"
    ple_g = g3["ple_post_g"]
    for l in reversed(range(L)):
        sv = saved[l]
        W = sv["W"]
        G, gl = G_layers[l], 0
        if l + 1 < L:
            ple_g = ple_g + layer_grads_done(l + 1, G_layers[l + 1])
        dh, de, dgte, sg["ple_post_g"][l] = _ple_bwd(dh, sv["e"], sv["gte"], W["ple_w_gate"], ple_g, l)
        G["ple_w_up"] = _dw(p3[l][None], de, G["ple_w_up"], gl, PLE, 256, whole, shard, f"dw_ple_up_l{l}")
        G["ple_w_gate"] = _dw(sv["h3"][None], dgte[None], G["ple_w_gate"], gl, 256, D, kcol, whole, f"dw_ple_gate_l{l}")
        dh, df, da, db, hh, sg["ffn2_pre_g"][l], sg["ffn2_post_g"][l] = _ffn_bwd(
            dh, sv["h2"], sv["f2"], sv["a2"], sv["b2"], g3["ffn2_pre_g"], g3["ffn2_post_g"],
            W["ffn2_w_gate"], W["ffn2_w_up"], W["ffn2_w_down"], l, "2")
        G["ffn2_w_gate"] = _dw(da, sv["xn2"][None], G["ffn2_w_gate"], gl, DFS, D, shard, whole, f"dw_ffn2_gate_l{l}")
        G["ffn2_w_up"] = _dw(db, sv["xn2"][None], G["ffn2_w_up"], gl, DFS, D, shard, whole, f"dw_ffn2_up_l{l}")
        G["ffn2_w_down"] = _dw(hh, df[None], G["ffn2_w_down"], gl, DFS, D, shard, whole, f"dw_ffn2_down_l{l}")
        a_re, a_im, bre, bim, cre, cim = sv["prep"]
        dattn, dssm, delta, dmo, sg["mix_post_g"][l], sg["attn_norm_g"][l], sg["ssm_norm_g"][l] = _mix_out_bwd(
            dh, sv["mo"], sv["attn"], sv["ssm"], g3["attn_norm_g"], g3["ssm_norm_g"], g3["mix_post_g"], W["w_out"], l)
        G["w_out"] = _dw(sv["mixed"][None], dmo[None], G["w_out"], gl, 256, D, kcol, whole, f"dw_out_l{l}")
        gnr, gni, du_direct, dz, yg, sg["ssm_b_glu"][l], dd, dcre, dcim = _ssm_out_bwd(
            dssm, sv["y"], sv["z"], sv["xr"], sv["xi"], sv["u"], cre, cim, g3["ssm_d"], W["ssm_w_glu"], l)
        sg["ssm_d"][l] = dd.reshape(Sm["ssm_d"].shape[1:])
        G["ssm_w_glu"] = _dw(yg[None], dz[None], G["ssm_w_glu"], gl, 128, DSS, kcol, whole, f"dw_glu_l{l}")
        gr, gi = _scan(gnr, gni, a_re, a_im, True, l)
        dar, dai = _ssm_da(gr, gi, sv["xr"], sv["xi"], l)
        du, dbre, dbim = _ssm_in_bwd(gr, gi, sv["u"], bre, bim, du_direct, l)
        prep_cot[l] = (dar, dai, dbre, dbim, dcre, dcim)
        acc = None
        for d in PATTERN_DILATIONS:
            acc = _attn_bwd(sv["P"], dattn, sv["lse"], delta, acc, d, l)
        dh, dP, sg["mix_pre_g"][l] = _mix_proj_bwd(acc[0], acc[1], acc[2], du, dh, sv["h1"], g3["mix_pre_g"],
                                                   W["w_in"], rot, l)
        G["w_in"] = _dw(sv["ain"][None], dP, G["w_in"], gl, D, DA,whole, shard, f"dw_in_l{l}")
        dh, df, da, db, hh, sg["ffn1_pre_g"][l], sg["ffn1_post_g"][l] = _ffn_bwd(
            dh, sv["h0"], sv["f1"], sv["a1"], sv["b1"], g3["ffn1_pre_g"], g3["ffn1_post_g"],
            W["ffn1_w_gate"], W["ffn1_w_up"], W["ffn1_w_down"], l, "1")
        G["ffn1_w_gate"] = _dw(da, sv["xn1"][None], G["ffn1_w_gate"], gl, DFS, D, shard, whole, f"dw_ffn1_gate_l{l}")
        G["ffn1_w_up"] = _dw(db, sv["xn1"][None], G["ffn1_w_up"], gl, DFS, D, shard, whole, f"dw_ffn1_up_l{l}")
        G["ffn1_w_down"] = _dw(hh, df[None], G["ffn1_w_down"], gl, DFS, D, shard, whole, f"dw_ffn1_down_l{l}")

    small = {n: jnp.stack([g.reshape(Sm[n].shape[1:]) for g in sg[n]]) for n in SMALL if n not in prep_names}
    small.update(zip(prep_names, prep_vjp(tuple(jnp.stack(c) for c in zip(*prep_cot)))))
    return loss, dh, G_layers[0], small


HBM_SPEC = pl.BlockSpec(memory_space=pltpu.HBM)


def _place():
    x, y, c = lax.axis_index("x"), lax.axis_index("y"), lax.axis_index("c")
    chips = [(1 - x, y), (x, 1 - y), (1 - x, 1 - y)]
    return x, y, c, chips


def _comm_params():
    return pltpu.CompilerParams(vmem_limit_bytes=VMEM_LIMIT)


def _gather_weights(ws, lands):
    n = len(ws)

    def body(*refs):
        ins, outs = refs[:n], refs[2 * n:3 * n]
        s_ici, r_ici, s_d2d, r_d2d = refs[3 * n:]
        x, y, c, chips = _place()

        def half(ref, t, hc):
            r2 = ws[t].shape[1] // 2
            return ref.at[:, pl.ds(hc * r2, r2), :]

        def ici(t, k, src_chip, to):
            j = 2 * src_chip[0] + src_chip[1]
            src = half(ins[t], t, c) if to is not None else half(outs[t].at[j], t, c)
            return pltpu.make_async_remote_copy(src_ref=src, dst_ref=half(outs[t].at[j], t, c),
                                                send_sem=s_ici.at[3 * t + k], recv_sem=r_ici.at[3 * t + k],
                                                device_id=to if to is not None else (x, y, c), device_id_type=MESH)

        def d2d(t, k, hc):
            j = 2 * chips[k][0] + chips[k][1]
            r = half(outs[t].at[j], t, hc)
            return pltpu.make_async_remote_copy(src_ref=r, dst_ref=r, send_sem=s_d2d.at[3 * t + k],
                                                recv_sem=r_d2d.at[3 * t + k], device_id=(x, y, 1 - c),
                                                device_id_type=MESH)

        sends = [ici(t, k, (x, y), (*chips[k], c)) for t in range(n) for k in range(3)]
        for cp in sends:
            cp.start()
        passed = []
        for t in range(n):
            for k in range(3):
                ici(t, k, chips[k], None).wait_recv()
                passed.append(d2d(t, k, c))
                passed[-1].start()
        for t in range(n):
            for k in range(3):
                d2d(t, k, 1 - c).wait_recv()
        for cp in sends + passed:
            cp.wait_send()

    return _pc(body, name="gather_weights", in_specs=[HBM_SPEC] * (2 * n), out_specs=[HBM_SPEC] * n,
               out_shape=[S(z.shape, z.dtype) for z in lands], input_output_aliases={n + t: t for t in range(n)},
               scratch_shapes=[pltpu.SemaphoreType.DMA((3 * n,))] * 4, compiler_params=_comm_params())(*ws, *lands)


SEM_SPEC = pl.BlockSpec(memory_space=pltpu.SEMAPHORE)
ANY_SPEC = pl.BlockSpec(memory_space=pl.ANY)
SPLIT_EFFECT = pltpu.SideEffectType.DATAFLOW_SIDE_EFFECTING


def _in_hbm(t):
    return pltpu.with_memory_space_constraint(t, pltpu.HBM)


def _place_own(ws, me_arr, layer):
    n = len(ws)

    def body(me_ref, *refs):
        for t in range(n):
            refs[n + t][...] = refs[t][...]

    gs = pltpu.PrefetchScalarGridSpec(
        num_scalar_prefetch=1, grid=(2,),
        in_specs=[pl.BlockSpec((w.shape[0], w.shape[1] // 2, w.shape[2]), lambda i, me: (0, i, 0)) for w in ws],
        out_specs=[pl.BlockSpec((None, w.shape[0], w.shape[1] // 2, w.shape[2]), lambda i, me: (me[0], 0, i, 0))
                   for w in ws])
    return _pc(body, name=f"gather_place_own_l{layer}", grid_spec=gs,
               out_shape=[S((NSH,) + w.shape, w.dtype) for w in ws], compiler_params=_cp(1))(me_arr, *ws)


def _gather_start(ws, lands, after, layer):
    n = len(ws)

    def body(*refs):
        ins, lz = refs[:n], refs[n:2 * n]
        s_sem, r_sem = refs[2 * n + 1], refs[2 * n + 2]
        token = refs[-1]
        x, y, c, chips = _place()
        for t in range(n):
            for k in range(3):
                pltpu.make_async_remote_copy(src_ref=ins[t], dst_ref=lz[t].at[2 * x + y], send_sem=s_sem.at[3 * t + k],
                                             recv_sem=r_sem.at[3 * t + k], device_id=(*chips[k], c),
                                             device_id_type=MESH).start()
        token[...] = jnp.zeros_like(token)

    hbm = [pltpu.HBM(w.shape, w.dtype) for w in ws] + [pltpu.HBM(z.shape, z.dtype) for z in lands]
    out = _pc(body, name=f"gather_start_l{layer}",
              out_shape=(pltpu.SemaphoreType.DMA((3 * n,)), pltpu.SemaphoreType.DMA((3 * n,)), *hbm, S((8, 128), F32)),
              in_specs=[HBM_SPEC] * (2 * n) + [ANY_SPEC],
              out_specs=(SEM_SPEC, SEM_SPEC, *([HBM_SPEC] * (2 * n)), pl.BlockSpec(memory_space=pltpu.VMEM)),
              input_output_aliases={i: 2 + i for i in range(2 * n)},
              compiler_params=pltpu.CompilerParams(has_side_effects=SPLIT_EFFECT))(
                  *[_in_hbm(w) for w in ws], *[_in_hbm(z) for z in lands], after)
    return out[0], out[1], out[2:2 + n], out[2 + n:2 + 2 * n], out[-1]


def _gather_wait(s_sem, r_sem, ws, lands, after, layer):
    n = len(ws)

    def body(*refs):
        ins, lz = refs[:n], refs[n:2 * n]
        s_ref, r_ref = refs[2 * n], refs[2 * n + 1]
        x, y, c, chips = _place()
        for t in range(n):
            for k in range(3):
                cp = pltpu.make_async_remote_copy(src_ref=ins[t], dst_ref=lz[t].at[2 * x + y], send_sem=s_ref.at[3 * t + k],
                                                  recv_sem=r_ref.at[3 * t + k], device_id=(*chips[k], c),
                                                  device_id_type=MESH)
                cp.wait_send()
                cp.wait_recv()

    hbm = [pltpu.HBM(w.shape, w.dtype) for w in ws] + [pltpu.HBM(z.shape, z.dtype) for z in lands]
    out = _pc(body, name=f"gather_wait_l{layer}", out_shape=tuple(hbm),
              in_specs=[HBM_SPEC] * (2 * n) + [SEM_SPEC, SEM_SPEC, ANY_SPEC], out_specs=tuple([HBM_SPEC] * (2 * n)),
              input_output_aliases={i: i for i in range(2 * n)},
              compiler_params=pltpu.CompilerParams(has_side_effects=SPLIT_EFFECT))(*ws, *lands, s_sem, r_sem, after)
    return out[n:]


def _swap_halves(gs, tag):
    n = len(gs)

    def body(*refs):
        ins, outs = refs[:n], refs[n:2 * n]
        s_sem, r_sem = refs[2 * n:]
        x, y, c, _ = _place()
        cps = []
        for t in range(n):
            r2 = gs[t].shape[2] // 2
            cps.append(pltpu.make_async_remote_copy(
                src_ref=ins[t].at[:, :, pl.ds((1 - c) * r2, r2), :], dst_ref=outs[t], send_sem=s_sem.at[t],
                recv_sem=r_sem.at[t], device_id=(x, y, 1 - c), device_id_type=MESH))
            cps[-1].start()
        for cp in cps:
            cp.wait_recv()
        for cp in cps:
            cp.wait_send()

    return _pc(body, name=f"grad_swap_halves_{tag}", in_specs=[HBM_SPEC] * n, out_specs=[HBM_SPEC] * n,
               out_shape=[S(g.shape[:2] + (g.shape[2] // 2, g.shape[3]), g.dtype) for g in gs],
               scratch_shapes=[pltpu.SemaphoreType.DMA((n,))] * 2, compiler_params=_comm_params())(*gs)


def _add_half(g, landed, c_arr, name):
    _, L, r2, cols = landed.shape

    def body(c_ref, g_ref, l_ref, o_ref):
        o_ref[...] = (g_ref[...].astype(F32) + l_ref[...].astype(F32)).astype(BF16)

    gs = pltpu.PrefetchScalarGridSpec(
        num_scalar_prefetch=1, grid=(NSH, L),
        in_specs=[pl.BlockSpec((None, None, r2, cols), lambda j, l, c: (j, l, c[0], 0)),
                  pl.BlockSpec((None, None, r2, cols), lambda j, l, c: (j, l, 0, 0))],
        out_specs=pl.BlockSpec((None, None, r2, cols), lambda j, l, c: (j, l, 0, 0)))
    return _pc(body, name=name, grid_spec=gs, out_shape=S(landed.shape, BF16), compiler_params=_cp(2))(c_arr, g, landed)


def _partial_copies(ins, lz, s_sem, r_sem):
    x, y, c, chips = _place()
    return [pltpu.make_async_remote_copy(src_ref=ins[t].at[2 * chips[k][0] + chips[k][1]], dst_ref=lz[t].at[k],
                                         send_sem=s_sem.at[3 * t + k], recv_sem=r_sem.at[3 * t + k],
                                         device_id=(*chips[k], c), device_id_type=MESH)
            for t in range(len(ins)) for k in range(3)]


def _partial_send_start(ps, lands):
    n = len(ps)

    def body(*refs):
        for cp in _partial_copies(refs[:n], refs[n:2 * n], refs[2 * n], refs[2 * n + 1]):
            cp.start()
        refs[-1][...] = jnp.zeros_like(refs[-1])

    hbm = [pltpu.HBM(p.shape, p.dtype) for p in ps] + [pltpu.HBM(z.shape, z.dtype) for z in lands]
    out = _pc(body, name="grad_partial_send_start",
              out_shape=(pltpu.SemaphoreType.DMA((3 * n,)), pltpu.SemaphoreType.DMA((3 * n,)), *hbm, S((8, 128), F32)),
              in_specs=[HBM_SPEC] * (2 * n),
              out_specs=(SEM_SPEC, SEM_SPEC, *([HBM_SPEC] * (2 * n)), pl.BlockSpec(memory_space=pltpu.VMEM)),
              input_output_aliases={i: 2 + i for i in range(2 * n)},
              compiler_params=pltpu.CompilerParams(has_side_effects=SPLIT_EFFECT))(
                  *[_in_hbm(p) for p in ps], *[_in_hbm(z) for z in lands])
    return out[0], out[1], out[2:2 + n], out[2 + n:2 + 2 * n], out[-1]


def _partial_send_wait(s_sem, r_sem, ps, lands, after):
    n = len(ps)

    def body(*refs):
        for cp in _partial_copies(refs[:n], refs[n:2 * n], refs[2 * n], refs[2 * n + 1]):
            cp.wait_send()
            cp.wait_recv()

    hbm = [pltpu.HBM(p.shape, p.dtype) for p in ps] + [pltpu.HBM(z.shape, z.dtype) for z in lands]
    out = _pc(body, name="grad_partial_send_wait", out_shape=tuple(hbm),
              in_specs=[HBM_SPEC] * (2 * n) + [SEM_SPEC, SEM_SPEC, ANY_SPEC], out_specs=tuple([HBM_SPEC] * (2 * n)),
              input_output_aliases={i: i for i in range(2 * n)},
              compiler_params=pltpu.CompilerParams(has_side_effects=SPLIT_EFFECT))(*ps, *lands, s_sem, r_sem, after)
    return out[:n], out[n:]


def _sum_shards(part, landed, me_arr, c_arr, buf, first_layer, name):
    _, nl, r2, cols = landed.shape

    def body(me_ref, c_ref, p_ref, l_ref, b_ref, o_ref):
        o_ref[...] = ((p_ref[...].astype(F32) + l_ref[0].astype(F32)) + l_ref[1].astype(F32)) + l_ref[2].astype(F32)

    gs = pltpu.PrefetchScalarGridSpec(
        num_scalar_prefetch=2, grid=(nl,),
        in_specs=[pl.BlockSpec((None, None, r2, cols), lambda l, me, c: (me[0], l, 0, 0)),
                  pl.BlockSpec((3, None, r2, cols), lambda l, me, c: (0, l, 0, 0)), ANY_SPEC],
        out_specs=pl.BlockSpec((None, r2, cols), lambda l, me, c: (first_layer + l, c[0], 0)))
    return _pc(body, name=name, grid_spec=gs, out_shape=S(buf.shape, F32), input_output_aliases={4: 0},
               compiler_params=_cp(1))(me_arr, c_arr, part, landed, buf)


def _direct_grad_copies(ins, lz, s_sem, r_sem):
    x, y, c, chips = _place()
    sends, recvs = [], []
    for t in range(len(ins)):
        r2 = ins[t].shape[2] // 2
        half = lambda j, h: ins[t].at[j, :, pl.ds(h * r2, r2), :]

        def copy(src, slot, s_idx, r_idx, to):
            return pltpu.make_async_remote_copy(src_ref=src, dst_ref=lz[t].at[slot], send_sem=s_sem.at[7 * t + s_idx],
                                                recv_sem=r_sem.at[7 * t + r_idx], device_id=to, device_id_type=MESH)

        for k in range(3):
            for h in range(2):
                sends.append(copy(half(2 * chips[k][0] + chips[k][1], h), 2 * k + c, 2 * k + h, 2 * k + c, (*chips[k], h)))
        sends.append(copy(half(2 * x + y, 1 - c), 6, 6, 6, (x, y, 1 - c)))
        recvs += [copy(half(0, 0), s, s, s, (x, y, c)) for s in range(7)]
    return sends, recvs


def _send_start(gs, lands, layer):
    n = len(gs)
    ps = gs

    def body(*refs):
        sends, _ = _direct_grad_copies(refs[:n], refs[n:2 * n], refs[2 * n], refs[2 * n + 1])
        for cp in sends:
            cp.start()
        refs[-1][...] = jnp.zeros_like(refs[-1])

    hbm = [pltpu.HBM(p.shape, p.dtype) for p in ps] + [pltpu.HBM(z.shape, z.dtype) for z in lands]
    out = _pc(body, name=f"grad_send_start_l{layer}",
              out_shape=(pltpu.SemaphoreType.DMA((7 * n,)), pltpu.SemaphoreType.DMA((7 * n,)), *hbm, S((8, 128), F32)),
              in_specs=[HBM_SPEC] * (2 * n),
              out_specs=(SEM_SPEC, SEM_SPEC, *([HBM_SPEC] * (2 * n)), pl.BlockSpec(memory_space=pltpu.VMEM)),
              input_output_aliases={i: 2 + i for i in range(2 * n)},
              compiler_params=pltpu.CompilerParams(has_side_effects=SPLIT_EFFECT))(
                  *[_in_hbm(p) for p in ps], *[_in_hbm(z) for z in lands])
    return out[0], out[1], out[2:2 + n], out[2 + n:2 + 2 * n], out[-1]


def _send_wait(s_sem, r_sem, ps, lands, after, layer):
    n = len(ps)

    def body(*refs):
        sends, recvs = _direct_grad_copies(refs[:n], refs[n:2 * n], refs[2 * n], refs[2 * n + 1])
        for cp in sends:
            cp.wait_send()
        for cp in recvs:
            cp.wait_recv()

    hbm = [pltpu.HBM(p.shape, p.dtype) for p in ps] + [pltpu.HBM(z.shape, z.dtype) for z in lands]
    out = _pc(body, name=f"grad_send_wait_l{layer}", out_shape=tuple(hbm),
              in_specs=[HBM_SPEC] * (2 * n) + [SEM_SPEC, SEM_SPEC, ANY_SPEC], out_specs=tuple([HBM_SPEC] * (2 * n)),
              input_output_aliases={i: i for i in range(2 * n)},
              compiler_params=pltpu.CompilerParams(has_side_effects=SPLIT_EFFECT))(*ps, *lands, s_sem, r_sem, after)
    return out[:n], out[n:]


def _sum_direct(g, landed, me_arr, c_arr, buf, first_layer, name):
    _, nl, r2, cols = landed.shape

    def body(me_ref, c_ref, g_ref, l_ref, b_ref, o_ref):
        tot = g_ref[...].astype(F32)
        for s in range(7):
            tot = tot + l_ref[s].astype(F32)
        o_ref[...] = tot

    gs = pltpu.PrefetchScalarGridSpec(
        num_scalar_prefetch=2, grid=(nl,),
        in_specs=[pl.BlockSpec((None, None, r2, cols), lambda l, me, c: (me[0], l, c[0], 0)),
                  pl.BlockSpec((7, None, r2, cols), lambda l, me, c: (0, l, 0, 0)), ANY_SPEC],
        out_specs=pl.BlockSpec((None, r2, cols), lambda l, me, c: (first_layer + l, c[0], 0)))
    return _pc(body, name=name, grid_spec=gs, out_shape=S(buf.shape, F32), input_output_aliases={4: 0},
               compiler_params=_cp(1))(me_arr, c_arr, g, landed, buf)


def _share_halves(bufs):
    n = len(bufs)

    def body(*refs):
        ins, outs = refs[:n], refs[n:2 * n]
        s_sem, r_sem = refs[2 * n:]
        x, y, c, _ = _place()
        cps = []
        for t in range(n):
            r2 = bufs[t].shape[1] // 2
            cps.append(pltpu.make_async_remote_copy(
                src_ref=ins[t].at[:, pl.ds(c * r2, r2), :], dst_ref=outs[t].at[:, pl.ds(c * r2, r2), :],
                send_sem=s_sem.at[t], recv_sem=r_sem.at[t], device_id=(x, y, 1 - c), device_id_type=MESH))
            cps[-1].start()
        for cp in cps:
            cp.wait_recv()
        for cp in cps:
            cp.wait_send()

    return _pc(body, name="grad_share_halves", in_specs=[HBM_SPEC] * n, out_specs=[HBM_SPEC] * n,
               out_shape=[S(b.shape, b.dtype) for b in bufs], input_output_aliases={t: t for t in range(n)},
               scratch_shapes=[pltpu.SemaphoreType.DMA((n,))] * 2, compiler_params=_comm_params())(*bufs)


def _gather_small(v, after):
    nr = v.shape[0]

    def body(v_ref, after_ref, out_ref, send_sems, recv_sems, local_sem):
        x, y, c, chips = _place()
        me, sibling = (x, y, c), (x, y, 1 - c)

        def rows(px, py, pc):
            return out_ref.at[pl.ds((4 * px + 2 * py + pc) * nr, nr), :]

        def copy(k, block, to, src=None):
            return pltpu.make_async_remote_copy(src_ref=rows(*block) if src is None else src, dst_ref=rows(*block),
                                                send_sem=send_sems.at[k], recv_sem=recv_sems.at[k], device_id=to,
                                                device_id_type=MESH)

        mine = pltpu.make_async_copy(v_ref, rows(*me), local_sem)
        mine.start()
        first = [copy(0, me, sibling, src=v_ref)]
        first += [copy(1 + j, me, (*chip, c), src=v_ref) for j, chip in enumerate(chips)]
        for cp in first:
            cp.start()
        passed = [copy(4 + j, (*chip, c), sibling) for j, chip in enumerate(chips)]
        for j, chip in enumerate(chips):
            copy(1 + j, (*chip, c), me).wait_recv()
            passed[j].start()
        copy(0, sibling, me).wait_recv()
        for j, chip in enumerate(chips):
            copy(4 + j, (*chip, 1 - c), me).wait_recv()
        for cp in first + passed:
            cp.wait_send()
        mine.wait()

    vm = pl.BlockSpec(memory_space=pltpu.VMEM)
    return _pc(body, name="gather_small_grads", in_specs=[vm, ANY_SPEC], out_specs=vm, out_shape=S((8 * nr, 128), F32),
               scratch_shapes=[pltpu.SemaphoreType.DMA((7,)), pltpu.SemaphoreType.DMA((7,)), pltpu.SemaphoreType.DMA],
               compiler_params=_comm_params())(v, after)


def _adamw_math(w, g, m, v):
    m2 = ADAM_B1 * m + (1.0 - ADAM_B1) * g
    v2 = ADAM_B2 * v + (1.0 - ADAM_B2) * (g * g)
    m_hat = m2 / (1.0 - ADAM_B1 ** ADAM_STEP)
    v_hat = v2 / (1.0 - ADAM_B2 ** ADAM_STEP)
    return -ADAM_LR * (m_hat / (jnp.sqrt(v_hat) + ADAM_EPS) + ADAM_WD * w), m2, v2


def _adamw(w, g, m, v, name):
    L, R, C = w.shape
    rb = R // 2 if R >= 512 else R

    def body(w_ref, g_ref, m_ref, v_ref, d_ref, m2_ref, v2_ref):
        d_ref[...], m2_ref[...], v2_ref[...] = _adamw_math(w_ref[...], g_ref[...], m_ref[...], v_ref[...])

    blk = pl.BlockSpec((None, rb, C), lambda l, r: (l, r, 0))
    return _pc(body, name=name, grid=(L, R // rb), in_specs=[blk] * 4, out_specs=[blk] * 3,
               out_shape=[S(w.shape, F32)] * 3, compiler_params=_cp(2))(w, g, m, v)


def _adamw_small(gathered, w, m, v):
    nr = w.shape[0]
    rb = nr // 5

    def body(a_ref, w_ref, m_ref, v_ref, g_ref, d_ref, m2_ref, v2_ref):
        g = a_ref[0]
        for k in range(1, 8):
            g = g + a_ref[k]
        g_ref[...] = g
        d_ref[...], m2_ref[...], v2_ref[...] = _adamw_math(w_ref[...], g, m_ref[...], v_ref[...])

    blk = pl.BlockSpec((rb, 128), lambda i: (i, 0))
    return _pc(body, name="adamw_small", grid=(nr // rb,), in_specs=[pl.BlockSpec((8, rb, 128), lambda i: (0, i, 0))] + [blk] * 3,
               out_specs=[blk] * 4, out_shape=[S((nr, 128), F32)] * 4, compiler_params=_cp(1))(gathered, w, m, v)


SMALL_ROWS = 4520


def _pack(arrs):
    flat = jnp.concatenate([a.reshape(-1) for a in arrs])
    return jnp.pad(flat, (0, SMALL_ROWS * 128 - flat.shape[0])).reshape(SMALL_ROWS, 128)


def _unpack(packed, like):
    flat = packed.reshape(-1)
    out, off = [], 0
    for a in like:
        out.append(flat[off:off + a.size].reshape(a.shape))
        off += a.size
    return out


def kernel(x, p, positions, ffn1_pre_g, ffn1_w_gate, ffn1_w_up, ffn1_w_down, ffn1_post_g, mix_pre_g, w_in, attn_norm_g, ssm_lam_re, ssm_lam_im, ssm_log_dt, ssm_b_re, ssm_b_im, ssm_c_re, ssm_c_im, ssm_d, ssm_w_glu, ssm_b_glu, ssm_norm_g, w_out, mix_post_g, ffn2_pre_g, ffn2_w_gate, ffn2_w_up, ffn2_w_down, ffn2_post_g, ple_w_up, ple_w_gate, ple_post_g, loss_target, m_ffn1_pre_g, m_ffn1_w_gate, m_ffn1_w_up, m_ffn1_w_down, m_ffn1_post_g, m_mix_pre_g, m_w_in, m_attn_norm_g, m_ssm_lam_re, m_ssm_lam_im, m_ssm_log_dt, m_ssm_b_re, m_ssm_b_im, m_ssm_c_re, m_ssm_c_im, m_ssm_d, m_ssm_w_glu, m_ssm_b_glu, m_ssm_norm_g, m_w_out, m_mix_post_g, m_ffn2_pre_g, m_ffn2_w_gate, m_ffn2_w_up, m_ffn2_w_down, m_ffn2_post_g, m_ple_w_up, m_ple_w_gate, m_ple_post_g, v_ffn1_pre_g, v_ffn1_w_gate, v_ffn1_w_up, v_ffn1_w_down, v_ffn1_post_g, v_mix_pre_g, v_w_in, v_attn_norm_g, v_ssm_lam_re, v_ssm_lam_im, v_ssm_log_dt, v_ssm_b_re, v_ssm_b_im, v_ssm_c_re, v_ssm_c_im, v_ssm_d, v_ssm_w_glu, v_ssm_b_glu, v_ssm_norm_g, v_w_out, v_mix_post_g, v_ffn2_pre_g, v_ffn2_w_gate, v_ffn2_w_up, v_ffn2_w_down, v_ffn2_post_g, v_ple_w_up, v_ple_w_gate, v_ple_post_g):
    a = dict(locals())
    T = x.shape[1]
    big_names = [n for n, _, _ in BIG]
    for n in TRANSPOSED:
        for pre in ("", "m_", "v_"):
            a[pre + n] = jnp.swapaxes(a[pre + n], 1, 2)

    own = [a[n].astype(BF16) for n in big_names]
    n_layers = own[0].shape[0]
    per_layer = [[w[l:l + 1] for w in own] for l in range(n_layers)]
    c_arr = lax.axis_index("c").astype(jnp.int32).reshape(1)
    me_arr = (2 * lax.axis_index("x") + lax.axis_index("y")).astype(jnp.int32).reshape(1)
    first = dict(zip(big_names, _gather_weights(per_layer[0], _place_own(per_layer[0], me_arr, 0))))
    pending, anchor, queued_behind = {}, jnp.zeros((), F32), first[big_names[0]]
    for l in range(1, n_layers):
        s_sem, r_sem, ws_thru, lands_thru, token = _gather_start(per_layer[l], _place_own(per_layer[l], me_arr, l),
                                                                 queued_behind, l)
        pending[l] = (s_sem, r_sem, ws_thru, lands_thru)
        anchor = anchor + token[0, 0]
        queued_behind = token

    def weights_of(l, after):
        if l == 0:
            return first
        return dict(zip(big_names, _gather_wait(*pending[l], after, l)))

    Sm = {n: a[n] for n in SMALL}
    Sm["ffn1_pre_g"] = Sm["ffn1_pre_g"] + anchor

    pos = jnp.broadcast_to(positions.reshape(1, T, 1).astype(F32), (1, T, 128))
    sent = {}

    def layer_grads_done(l, G):
        gs = [G[n] for n in big_names]
        lands = [lax.empty((7, 1, g.shape[2] // 2, g.shape[3]), BF16) for g in gs]
        s_sem, r_sem, gs_thru, lands_thru, token = _send_start(gs, lands, l)
        sent[l] = (s_sem, r_sem, gs_thru, lands_thru)
        return token[0, 0]

    loss, gx, G_first, small = _local_step(
        _reorder(x, True, "to_streams_x")[0], _reorder(p[:, 0], True, "to_streams_p"),
        _reorder(pos, True, "to_streams_pos")[0, :, :1], _reorder(loss_target, True, "to_streams_target")[0],
        weights_of, layer_grads_done, Sm)
    gx = _reorder(gx[None], False, "to_time_grad_x")

    gs0 = [G_first[n] for n in big_names]
    parts = [_add_half(g, la, c_arr, f"grad_add_half_{n}") for g, la, n in zip(gs0, _swap_halves(gs0, "first"), big_names)]
    first_sent = _partial_send_start(parts, [lax.empty((3,) + pt.shape[1:], BF16) for pt in parts])

    bufs = [lax.empty((n_layers, r, c), F32) for _, r, c in BIG]
    for l in sorted(sent, reverse=True):
        gs, landed = _send_wait(*sent[l], first_sent[-1], l)
        bufs = [_sum_direct(g, la, me_arr, c_arr, b, l, f"grad_sum_direct_l{l}_{n}")
                for g, la, b, n in zip(gs, landed, bufs, big_names)]
    small_g = _gather_small(_pack([small[n] for n in SMALL]), bufs[0]).reshape(8, SMALL_ROWS, 128)
    sg, sd, sm, sv = _adamw_small(small_g, _pack([a[n] for n in SMALL]), _pack([a["m_" + n] for n in SMALL]),
                                  _pack([a["v_" + n] for n in SMALL]))

    parts, landed = _partial_send_wait(*first_sent[:-1], sd)
    bufs = [_sum_shards(pt, la, me_arr, c_arr, b, 0, f"grad_sum_shards_first_{n}")
            for pt, la, b, n in zip(parts, landed, bufs, big_names)]
    grads = dict(zip(big_names, _share_halves(bufs)))
    like = [a[n] for n in SMALL]
    res = {}
    for n, g_, d_, m_, v_ in zip(SMALL, _unpack(sg, like), _unpack(sd, like), _unpack(sm, like), _unpack(sv, like)):
        res[n] = (g_, d_, m_, v_)
    for n in big_names:
        d_, m_, v_ = _adamw(a[n], grads[n], a["m_" + n], a["v_" + n], f"adamw_{n}")
        res[n] = (grads[n], d_, m_, v_)
        if n in TRANSPOSED:
            res[n] = tuple(jnp.swapaxes(t, 1, 2) for t in res[n])

    total = lax.psum(loss[0, 0], ("x", "y", "c"))
    return (total, gx, *[res[n][0] for n in WEIGHTS], *[res[n][1] for n in WEIGHTS],
            *[res[n][2] for n in WEIGHTS], *[res[n][3] for n in WEIGHTS])
```

```python
import functools
import math

import numpy as np
import jax
import jax.numpy as jnp
from jax import lax
from jax.experimental import pallas as pl
from jax.experimental.pallas import tpu as pltpu

F32 = jnp.float32
BF16 = jnp.bfloat16
S = jax.ShapeDtypeStruct
MESH = pl.DeviceIdType.MESH

D = 1024
DA = 512
DSS = 512
HD = 64
NH = 8
BAND = 128
NSH = 4
DFS = 704
PLE = 256
EPS = 1e-6
ROPE_THETA = 500000.0
PATTERN_DILATIONS = (1, 4, 16)
NLB = 16
ADAM_LR, ADAM_B1, ADAM_B2, ADAM_EPS, ADAM_WD, ADAM_STEP = 0.001, 0.9, 0.999, 1e-08, 0.01, 10

VMEM_LIMIT = 56 * 1024 * 1024
TM = 512
TMB = 256

BIG = (
    ("ffn1_w_gate", DFS, D), ("ffn1_w_up", DFS, D), ("ffn1_w_down", DFS, D),
    ("w_in", D, 512), ("ssm_w_glu", 128, 512), ("w_out", 256, D),
    ("ffn2_w_gate", DFS, D), ("ffn2_w_up", DFS, D), ("ffn2_w_down", DFS, D),
    ("ple_w_up", PLE, 256), ("ple_w_gate", 256, D),
)
TRANSPOSED = ("ffn1_w_gate", "ffn1_w_up", "ffn2_w_gate", "ffn2_w_up")
SMALL = ("ffn1_pre_g", "ffn1_post_g", "mix_pre_g", "attn_norm_g", "ssm_lam_re", "ssm_lam_im", "ssm_log_dt",
         "ssm_b_re", "ssm_b_im", "ssm_c_re", "ssm_c_im", "ssm_d", "ssm_b_glu", "ssm_norm_g", "mix_post_g",
         "ffn2_pre_g", "ffn2_post_g", "ple_post_g")
WEIGHTS = ("ffn1_pre_g", "ffn1_w_gate", "ffn1_w_up", "ffn1_w_down", "ffn1_post_g", "mix_pre_g", "w_in", "attn_norm_g",
           "ssm_lam_re", "ssm_lam_im", "ssm_log_dt", "ssm_b_re", "ssm_b_im", "ssm_c_re", "ssm_c_im", "ssm_d",
           "ssm_w_glu", "ssm_b_glu", "ssm_norm_g", "w_out", "mix_post_g", "ffn2_pre_g", "ffn2_w_gate", "ffn2_w_up",
           "ffn2_w_down", "ffn2_post_g", "ple_w_up", "ple_w_gate", "ple_post_g")


def _pc(body, **kw):
    return pl.pallas_call(body, **kw)


def _cp(n_grid):
    return pltpu.CompilerParams(dimension_semantics=("arbitrary",) * n_grid, vmem_limit_bytes=VMEM_LIMIT)


def _dot(a, b):
    return jnp.dot(a, b, preferred_element_type=F32)


def _dot_nt(a, b):
    return lax.dot_general(a, b, (((1,), (1,)), ((), ())), preferred_element_type=F32)


def _dot_tn(a, b):
    return lax.dot_general(a, b, (((0,), (0,)), ((), ())), preferred_element_type=F32)


def _split(a):
    hi = a.astype(BF16)
    return hi, (a - hi.astype(F32)).astype(BF16)


def _dot2(fn, a, b, exact):
    if exact == "a":
        ah, al = _split(a)
        b16 = b.astype(BF16)
        return fn(ah, b16) + fn(al, b16)
    bh, bl = _split(b)
    a16 = a.astype(BF16)
    return fn(a16, bh) + fn(a16, bl)


def _rms_fwd(x, g):
    r = lax.rsqrt(jnp.mean(x * x, axis=-1, keepdims=True) + EPS)
    return x * r * g


def _rms_bwd(dy, x, g):
    r = lax.rsqrt(jnp.mean(x * x, axis=-1, keepdims=True) + EPS)
    xr = x * r
    gd = dy * g
    dx = r * (gd - xr * jnp.mean(gd * xr, axis=-1, keepdims=True))
    dg = jnp.sum(dy * xr, axis=0, keepdims=True)
    return dx, dg


def _gelu(y):
    k = math.sqrt(2.0 / math.pi)
    return 0.5 * y * (1.0 + jnp.tanh(k * (y + 0.044715 * y * y * y)))


def _gelu_grad(y):
    k = math.sqrt(2.0 / math.pi)
    t = jnp.tanh(k * (y + 0.044715 * y * y * y))
    return 0.5 * (1.0 + t) + 0.5 * y * (1.0 - t * t) * k * (1.0 + 3 * 0.044715 * y * y)


def _gain_spec(n, layer):
    return pl.BlockSpec((None, 1, n), lambda *_: (layer, 0, 0))


def _row_acc_spec(n):
    return pl.BlockSpec((1, n), lambda *_: (0, 0))


def _rot_tables(pos_col):
    T = pos_col.shape[0]
    half = HD // 8
    inv = (ROPE_THETA ** (-np.arange(half, dtype=np.float32) * (2.0 / (2 * half)))).astype(np.float32)
    lane_freq = np.tile(np.concatenate([inv, inv, np.zeros(HD - 2 * half, np.float32)]), NH)[None, :]

    def body(p_ref, f_ref, c_ref, s1_ref, s2_ref):
        ang = p_ref[...] * f_ref[...]
        d = lax.broadcasted_iota(jnp.int32, ang.shape, 1) % HD
        cs = jnp.cos(ang)
        sn = jnp.sin(ang)
        c_ref[...] = jnp.where(d < 2 * half, cs, 1.0)
        s1_ref[...] = jnp.where(d < half, -sn, 0.0)
        s2_ref[...] = jnp.where((d >= half) & (d < 2 * half), sn, 0.0)

    tm = TM
    return _pc(body, name="rot_tables", grid=(T // tm,),
               in_specs=[pl.BlockSpec((tm, 1), lambda i: (i, 0)), pl.BlockSpec((1, DA), lambda i: (0, 0))],
               out_specs=[pl.BlockSpec((tm, DA), lambda i: (i, 0))] * 3,
               out_shape=[S((T, DA), F32)] * 3, compiler_params=_cp(1))(pos_col, jnp.asarray(lane_freq))


def _rot_fwd(t, c, s1, s2):
    return t * c + pltpu.roll(t, DA - 8, 1) * s1 + pltpu.roll(t, 8, 1) * s2


def _rot_bwd(g, c, s1, s2):
    return g * c + pltpu.roll(g * s1, 8, 1) + pltpu.roll(g * s2, DA - 8, 1)


def _ffn_weight_spec():
    return pl.BlockSpec((NSH, None, DFS, D), lambda i: (0, 0, 0, 0), pipeline_mode=pl.Buffered(1))


def _ffn_fwd(h, pre_g, post_g, wg, wu, wd, layer, tag):
    T = h.shape[0]
    tm = TM
    nt = T // tm

    def body(h_ref, pg_ref, qg_ref, wg_ref, wu_ref, wd_ref, ho_ref, a_ref, b_ref, f_ref, xn_ref):
        hv = h_ref[...]
        xb = _rms_fwd(hv, pg_ref[...]).astype(BF16)
        xn_ref[...] = xb
        f = None
        for j in range(NSH):
            ab = _dot_nt(xb, wg_ref[j]).astype(BF16)
            bb = _dot_nt(xb, wu_ref[j]).astype(BF16)
            a_ref[j] = ab
            b_ref[j] = bb
            a = ab.astype(F32)
            hh = (a * jax.nn.sigmoid(a) * bb.astype(F32)).astype(BF16)
            part = _dot(hh, wd_ref[j])
            f = part if f is None else f + part
        f_ref[...] = f
        ho_ref[...] = hv + 0.5 * _rms_fwd(f, qg_ref[...])

    row = pl.BlockSpec((tm, D), lambda i: (i, 0))
    act = pl.BlockSpec((NSH, tm, DFS), lambda i: (0, i, 0))
    return _pc(body, name=f"ffn_fwd_{tag}_l{layer}", grid=(nt,),
               in_specs=[row, _gain_spec(D, layer), _gain_spec(D, layer)] + [_ffn_weight_spec()] * 3,
               out_specs=[row, act, act, row, row],
               out_shape=[S((T, D), F32), S((NSH, T, DFS), BF16), S((NSH, T, DFS), BF16), S((T, D), F32), S((T, D), BF16)],
               compiler_params=_cp(1))(h, pre_g, post_g, wg, wu, wd)


def _ffn_bwd(dout, h, f, a, b, pre_g, post_g, wg, wu, wd, layer, tag):
    T = h.shape[0]
    tm = TMB
    nt = T // tm

    def body(do_ref, h_ref, f_ref, a_ref, b_ref, pg_ref, qg_ref, wg_ref, wu_ref, wd_ref,
             dh_ref, df_ref, da_ref, db_ref, hh_ref, dpg_ref, dqg_ref):
        @pl.when(pl.program_id(0) == 0)
        def _():
            dpg_ref[...] = jnp.zeros_like(dpg_ref)
            dqg_ref[...] = jnp.zeros_like(dqg_ref)

        do = do_ref[...]
        df, dq = _rms_bwd(0.5 * do, f_ref[...], qg_ref[...])
        dqg_ref[...] += dq
        dfb = df.astype(BF16)
        df_ref[...] = dfb
        dxn = None
        for j in range(NSH):
            dhh = _dot_nt(dfb, wd_ref[j])
            av = a_ref[j].astype(F32)
            bv = b_ref[j].astype(F32)
            sg = jax.nn.sigmoid(av)
            sa = av * sg
            hh_ref[j] = (sa * bv).astype(BF16)
            dab = (dhh * bv * (sg + sa * (1.0 - sg))).astype(BF16)
            dbb = (dhh * sa).astype(BF16)
            da_ref[j] = dab
            db_ref[j] = dbb
            part = _dot(dab, wg_ref[j]) + _dot(dbb, wu_ref[j])
            dxn = part if dxn is None else dxn + part
        dx, dp = _rms_bwd(dxn, h_ref[...], pg_ref[...])
        dpg_ref[...] += dp
        dh_ref[...] = do + dx

    row = pl.BlockSpec((tm, D), lambda i: (i, 0))
    act = pl.BlockSpec((NSH, tm, DFS), lambda i: (0, i, 0))
    return _pc(body, name=f"ffn_bwd_{tag}_l{layer}", grid=(nt,),
               in_specs=[row, row, row, act, act, _gain_spec(D, layer), _gain_spec(D, layer)] + [_ffn_weight_spec()] * 3,
               out_specs=[row, row, act, act, act, _row_acc_spec(D), _row_acc_spec(D)],
               out_shape=[S((T, D), F32), S((T, D), BF16), S((NSH, T, DFS), BF16), S((NSH, T, DFS), BF16),
                          S((NSH, T, DFS), BF16), S((1, D), F32), S((1, D), F32)],
               compiler_params=_cp(1))(dout, h, f, a, b, pre_g, post_g, wg, wu, wd)


def _dw(A, B, buf, layer, kb, nb, a_mode, b_mode, name):
    T = A.shape[1]
    tt = TM
    nt = T // tt

    def pick(v, mode, j, w):
        if mode == "shard":
            return v[j]
        return v[0] if mode == "whole" else v[0][:, j * w:(j + 1) * w]

    def body(a_ref, b_ref, buf_ref, o_ref, acc):
        t = pl.program_id(0)

        @pl.when(t == 0)
        def _():
            acc[...] = jnp.zeros_like(acc)

        av = a_ref[...].astype(BF16)
        bv = b_ref[...].astype(BF16)
        for j in range(NSH):
            acc[j] += _dot_tn(pick(av, a_mode, j, kb), pick(bv, b_mode, j, nb))

        @pl.when(t == nt - 1)
        def _():
            o_ref[...] = acc[...].astype(o_ref.dtype)

    return _pc(body, name=name, grid=(nt,),
               in_specs=[pl.BlockSpec((A.shape[0], tt, A.shape[2]), lambda t: (0, t, 0)),
                         pl.BlockSpec((B.shape[0], tt, B.shape[2]), lambda t: (0, t, 0)),
                         pl.BlockSpec(memory_space=pl.ANY)],
               out_specs=pl.BlockSpec((NSH, None, kb, nb), lambda t: (0, layer, 0, 0)),
               out_shape=S(buf.shape, buf.dtype), input_output_aliases={2: 0},
               scratch_shapes=[pltpu.VMEM((NSH, kb, nb), F32)], compiler_params=_cp(1))(A, B, buf)


def _mix_proj(h, pre_g, win, rot, layer):
    T = h.shape[0]
    tm = TM

    def body(h_ref, g_ref, w_ref, c_ref, s1_ref, s2_ref, p_ref, xn_ref):
        xb = _rms_fwd(h_ref[...], g_ref[...]).astype(BF16)
        xn_ref[...] = xb
        for j in range(NSH):
            o = _dot(xb, w_ref[j])
            p_ref[j] = _rot_fwd(o, c_ref[...], s1_ref[...], s2_ref[...]) if j < 2 else o

    row = pl.BlockSpec((tm, D), lambda i: (i, 0))
    half = pl.BlockSpec((tm, DA), lambda i: (i, 0))
    return _pc(body, name=f"mix_proj_l{layer}", grid=(T // tm,),
               in_specs=[row, _gain_spec(D, layer), pl.BlockSpec((NSH, None, D, DA), lambda i: (0, 0, 0, 0)),
                         half, half, half],
               out_specs=[pl.BlockSpec((NSH, tm, DA), lambda i: (0, i, 0)), row],
               out_shape=[S((NSH, T, DA), F32), S((T, D), BF16)], compiler_params=_cp(1))(h, pre_g, win, *rot)


def _mix_proj_bwd(dq, dk, dv, du, dh_up, h, pre_g, win, rot, layer):
    T = h.shape[0]
    tm = TM

    def body(dq_ref, dk_ref, dv_ref, du_ref, up_ref, h_ref, g_ref, w_ref, c_ref, s1_ref, s2_ref,
             dh_ref, dp_ref, dg_ref):
        @pl.when(pl.program_id(0) == 0)
        def _():
            dg_ref[...] = jnp.zeros_like(dg_ref)

        rot = (c_ref[...], s1_ref[...], s2_ref[...])
        dps = [_rot_bwd(dq_ref[...], *rot), _rot_bwd(dk_ref[...], *rot), dv_ref[...], du_ref[...]]
        dxn = None
        for j in range(NSH):
            dpb = dps[j].astype(BF16)
            dp_ref[j] = dpb
            part = _dot_nt(dpb, w_ref[j])
            dxn = part if dxn is None else dxn + part
        dx, dg = _rms_bwd(dxn, h_ref[...], g_ref[...])
        dg_ref[...] += dg
        dh_ref[...] = up_ref[...] + dx

    row = pl.BlockSpec((tm, D), lambda i: (i, 0))
    half = pl.BlockSpec((tm, DA), lambda i: (i, 0))
    return _pc(body, name=f"mix_proj_bwd_l{layer}", grid=(T // tm,),
               in_specs=[half, half, half, half, row, row, _gain_spec(D, layer),
                         pl.BlockSpec((NSH, None, D, DA), lambda i: (0, 0, 0, 0)), half, half, half],
               out_specs=[row, pl.BlockSpec((NSH, tm, DA), lambda i: (0, i, 0)), _row_acc_spec(D)],
               out_shape=[S((T, D), F32), S((NSH, T, DA), BF16), S((1, D), F32)],
               compiler_params=_cp(1))(dq, dk, dv, du, dh_up, h, pre_g, win, *rot)


def _stream_pos(d, axis):
    i = lax.broadcasted_iota(jnp.int32, (BAND, BAND), axis)
    if d == 16:
        return i
    if d == 4:
        return 4 * (i % 32) + i // 32
    return 16 * (i % 8) + i // 8


def _band_masks(b, d):
    qi, kj = _stream_pos(d, 0), _stream_pos(d, 1)
    return kj <= qi, (kj >= qi) & (b > 0)


def _pattern(d, T):
    n16 = T // 16
    if d == 16:
        return (16, n16, DA), (None, BAND, DA), lambda r, k: (r, k, 0)
    if d == 4:
        return (4, 4, n16, DA), (4, None, 32, DA), lambda r, k: (0, r, k, 0)
    return (16, n16, DA), (16, 8, DA), lambda r, k: (0, k, 0)


def _pattern_spec(d, T, kmap, lead=None):
    _, blk, idx = _pattern(d, T)
    if lead is None:
        return pl.BlockSpec(blk, lambda r, b: idx(r, kmap(b)))
    return pl.BlockSpec((None,) + blk, lambda r, b: (lead,) + idx(r, kmap(b)))


def _whole_stream_specs(T, n_plain):
    n16 = T // 16
    p_spec = lambda s: pl.BlockSpec((None, None, n16, DA), lambda r: (s, r, 0, 0))
    plain = pl.BlockSpec((None, n16, DA), lambda r: (r, 0, 0))
    return [p_spec(0), p_spec(1), p_spec(2)] + [plain] * n_plain, plain


def _stream_masks():
    qi = lax.broadcasted_iota(jnp.int32, (BAND, BAND), 0)
    kj = lax.broadcasted_iota(jnp.int32, (BAND, BAND), 1)
    mask_c = kj <= qi
    return mask_c, jnp.concatenate([kj >= qi, mask_c], axis=1)


def _attn_fwd_stream(P, layer):
    T = P.shape[1]
    n16 = T // 16
    nb = n16 // BAND
    scale = HD ** -0.5

    def body(q_ref, k_ref, v_ref, o_ref, l_ref, qs, ks, vs):
        for src, dst in ((q_ref, qs), (k_ref, ks), (v_ref, vs)):
            dst[...] = src[...].astype(BF16)
        mask_c, mask_pc = _stream_masks()
        for b in range(nb):
            rows = slice(b * BAND, (b + 1) * BAND)
            krows = slice(max(b - 1, 0) * BAND, (b + 1) * BAND)
            mask = mask_c if b == 0 else mask_pc
            for hd in range(NH):
                sl = slice(hd * HD, (hd + 1) * HD)
                s = jnp.where(mask, _dot_nt(qs[rows, sl], ks[krows, sl]) * scale, -1e30)
                m = jnp.max(s, axis=-1, keepdims=True)
                e = jnp.exp(s - m)
                den = jnp.sum(e, axis=-1, keepdims=True)
                o_ref[rows, sl] = _dot(e.astype(BF16), vs[krows, sl]) / den
                l_ref[rows, sl] = jnp.broadcast_to(m + jnp.log(den), (BAND, HD))

    ins, out = _whole_stream_specs(T, 0)
    Pv = P.reshape(NSH, 16, n16, DA)
    o, l = _pc(body, name=f"attn_fwd_d16_l{layer}", grid=(16,), in_specs=ins, out_specs=[out, out],
               out_shape=[S((16, n16, DA), F32)] * 2, scratch_shapes=[pltpu.VMEM((n16, DA), BF16)] * 3,
               compiler_params=_cp(1))(Pv, Pv, Pv)
    return o.reshape(T, DA), l.reshape(T, DA)


def _attn_bwd_stream(P, dO, lse, delta, acc, layer):
    T = P.shape[1]
    n16 = T // 16
    nb = n16 // BAND
    scale = HD ** -0.5
    first = acc is None

    def body(*refs):
        q_ref, k_ref, v_ref, do_ref, l_ref, dl_ref = refs[:6]
        if first:
            dq_ref, dk_ref, dv_ref = refs[6:9]
        else:
            aq_ref, ak_ref, av_ref, dq_ref, dk_ref, dv_ref = refs[6:12]
        qs, ks, vs, dos, okf, ovf = refs[-6:]
        for src, dst in ((q_ref, qs), (k_ref, ks), (v_ref, vs), (do_ref, dos)):
            dst[...] = src[...].astype(BF16)
        okf[...] = jnp.zeros_like(okf)
        ovf[...] = jnp.zeros_like(ovf)
        mask_c, mask_pc = _stream_masks()
        for b in range(nb):
            rows = slice(b * BAND, (b + 1) * BAND)
            krows = slice(max(b - 1, 0) * BAND, (b + 1) * BAND)
            mask = mask_c if b == 0 else mask_pc
            for hd in range(NH):
                sl = slice(hd * HD, (hd + 1) * HD)
                one = slice(hd * HD, hd * HD + 1)
                q, do, kk = qs[rows, sl], dos[rows, sl], ks[krows, sl]
                p = jnp.where(mask, jnp.exp(_dot_nt(q, kk) * scale - l_ref[rows, one]), 0.0)
                ds = (p * (_dot_nt(do, vs[krows, sl]) - dl_ref[rows, one]) * scale).astype(BF16)
                dq = _dot(ds, kk)
                dq_ref[rows, sl] = dq if first else aq_ref[rows, sl] + dq
                okf[krows, sl] += _dot_tn(ds, q)
                ovf[krows, sl] += _dot_tn(p.astype(BF16), do)
        dk_ref[...] = okf[...] if first else ak_ref[...] + okf[...]
        dv_ref[...] = ovf[...] if first else av_ref[...] + ovf[...]

    ins, out = _whole_stream_specs(T, 3 if first else 6)
    Pv = P.reshape(NSH, 16, n16, DA)
    view = lambda t: t.reshape(16, n16, DA)
    args = [Pv, Pv, Pv, view(dO), view(lse), view(delta)] + ([] if first else [view(t) for t in acc])
    dq, dk, dv = _pc(body, name=f"attn_bwd_d16_l{layer}", grid=(16,), in_specs=ins, out_specs=[out, out, out],
                     out_shape=[S((16, n16, DA), F32)] * 3,
                     scratch_shapes=[pltpu.VMEM((n16, DA), BF16)] * 4 + [pltpu.VMEM((n16, DA), F32)] * 2,
                     compiler_params=_cp(1))(*args)
    return dq.reshape(T, DA), dk.reshape(T, DA), dv.reshape(T, DA)


def _attn_fwd(P, d, layer):
    if d == 16:
        return _attn_fwd_stream(P, layer)
    T = P.shape[1]
    nb = T // d // BAND
    vshape = _pattern(d, T)[0]
    Pv = P.reshape((NSH,) + vshape)
    scale = HD ** -0.5

    def body(q_ref, kp_ref, kc_ref, vp_ref, vc_ref, o_ref, l_ref, qs, ks, vs, osc, lsc):
        b = pl.program_id(1)
        flat = lambda ref: ref[...].reshape(BAND, DA).astype(BF16)
        qs[...] = flat(q_ref)
        ks[0:BAND, :] = flat(kp_ref)
        ks[BAND:, :] = flat(kc_ref)
        vs[0:BAND, :] = flat(vp_ref)
        vs[BAND:, :] = flat(vc_ref)
        mask_c, mask_p = _band_masks(b, d)
        mask = jnp.concatenate([mask_p, mask_c], axis=1)
        for hd in range(NH):
            sl = slice(hd * HD, (hd + 1) * HD)
            s = jnp.where(mask, _dot_nt(qs[:, sl], ks[:, sl]) * scale, -1e30)
            m = jnp.max(s, axis=-1, keepdims=True)
            e = jnp.exp(s - m)
            den = jnp.sum(e, axis=-1, keepdims=True)
            osc[:, sl] = _dot(e.astype(BF16), vs[:, sl]) / den
            lsc[:, sl] = jnp.broadcast_to(m + jnp.log(den), (BAND, HD))
        o_ref[...] = osc[...].reshape(o_ref.shape)
        l_ref[...] = lsc[...].reshape(l_ref.shape)

    cur = lambda b: b
    prev = lambda b: jnp.maximum(b - 1, 0)
    out = _pattern_spec(d, T, cur)
    o, l = _pc(body, name=f"attn_fwd_d{d}_l{layer}", grid=(d, nb),
               in_specs=[_pattern_spec(d, T, cur, 0), _pattern_spec(d, T, prev, 1), _pattern_spec(d, T, cur, 1),
                         _pattern_spec(d, T, prev, 2), _pattern_spec(d, T, cur, 2)],
               out_specs=[out, out], out_shape=[S(vshape, F32)] * 2,
               scratch_shapes=[pltpu.VMEM((BAND, DA), BF16)] + [pltpu.VMEM((2 * BAND, DA), BF16)] * 2
               + [pltpu.VMEM((BAND, DA), F32)] * 2,
               compiler_params=_cp(2))(Pv, Pv, Pv, Pv, Pv)
    return o.reshape(T, DA), l.reshape(T, DA)


def _attn_bwd(P, dO, lse, delta, acc, d, layer):
    if d == 16:
        return _attn_bwd_stream(P, dO, lse, delta, acc, layer)
    T = P.shape[1]
    nb = T // d // BAND
    vshape = _pattern(d, T)[0]
    Pv = P.reshape((NSH,) + vshape)
    scale = HD ** -0.5
    first = acc is None

    def body(*refs):
        q_ref, kp_ref, kc_ref, vp_ref, vc_ref, do_ref, l_ref, dl_ref = refs[:8]
        if first:
            dq_ref, dk_ref, dv_ref = refs[8:11]
        else:
            aq_ref, ak_ref, av_ref, dq_ref, dk_ref, dv_ref = refs[8:14]
        qs, dos, ks, vs, ls, dls, oq, ok, ov, ck, cv = refs[-11:]
        b = pl.program_id(1)
        flat = lambda ref: ref[...].reshape(BAND, DA)

        @pl.when(b == 0)
        def _():
            ck[...] = jnp.zeros_like(ck)
            cv[...] = jnp.zeros_like(cv)

        @pl.when(b < nb)
        def _():
            qs[...] = flat(q_ref).astype(BF16)
            dos[...] = flat(do_ref).astype(BF16)
            ks[0:BAND, :] = flat(kp_ref).astype(BF16)
            ks[BAND:, :] = flat(kc_ref).astype(BF16)
            vs[0:BAND, :] = flat(vp_ref).astype(BF16)
            vs[BAND:, :] = flat(vc_ref).astype(BF16)
            ls[...] = flat(l_ref)
            dls[...] = flat(dl_ref)
            mask_c, mask_p = _band_masks(b, d)
            mask = jnp.concatenate([mask_p, mask_c], axis=1)
            for hd in range(NH):
                sl = slice(hd * HD, (hd + 1) * HD)
                one = slice(hd * HD, hd * HD + 1)
                q, do, kk = qs[:, sl], dos[:, sl], ks[:, sl]
                p = jnp.where(mask, jnp.exp(_dot_nt(q, kk) * scale - ls[:, one]), 0.0)
                ds = (p * (_dot_nt(do, vs[:, sl]) - dls[:, one]) * scale).astype(BF16)
                oq[:, sl] = _dot(ds, kk)
                dk2 = _dot_tn(ds, q)
                dv2 = _dot_tn(p.astype(BF16), do)
                ok[:, sl] = ck[:, sl] + dk2[0:BAND]
                ov[:, sl] = cv[:, sl] + dv2[0:BAND]
                ck[:, sl] = dk2[BAND:]
                cv[:, sl] = dv2[BAND:]
            if first:
                dq_ref[...] = oq[...].reshape(dq_ref.shape)
                dk_ref[...] = ok[...].reshape(dk_ref.shape)
                dv_ref[...] = ov[...].reshape(dv_ref.shape)
            else:
                dq_ref[...] = aq_ref[...] + oq[...].reshape(dq_ref.shape)
                dk_ref[...] = ak_ref[...] + ok[...].reshape(dk_ref.shape)
                dv_ref[...] = av_ref[...] + ov[...].reshape(dv_ref.shape)

        @pl.when(b == nb)
        def _():
            if first:
                dk_ref[...] = ck[...].reshape(dk_ref.shape)
                dv_ref[...] = cv[...].reshape(dv_ref.shape)
            else:
                dk_ref[...] = ak_ref[...] + ck[...].reshape(dk_ref.shape)
                dv_ref[...] = av_ref[...] + cv[...].reshape(dv_ref.shape)

    qb = lambda b: jnp.minimum(b, nb - 1)
    qprev = lambda b: jnp.maximum(qb(b) - 1, 0)
    kb = lambda b: jnp.maximum(b - 1, 0)
    qrow = _pattern_spec(d, T, qb)
    krow = _pattern_spec(d, T, kb)
    view = lambda t: t.reshape(vshape)
    ins = [Pv, Pv, Pv, Pv, Pv, view(dO), view(lse), view(delta)]
    specs = [_pattern_spec(d, T, qb, 0), _pattern_spec(d, T, qprev, 1), _pattern_spec(d, T, qb, 1),
             _pattern_spec(d, T, qprev, 2), _pattern_spec(d, T, qb, 2), qrow, qrow, qrow]
    if not first:
        ins += [view(t) for t in acc]
        specs += [qrow, krow, krow]
    dq, dk, dv = _pc(body, name=f"attn_bwd_d{d}_l{layer}", grid=(d, nb + 1), in_specs=specs,
                     out_specs=[qrow, krow, krow], out_shape=[S(vshape, F32)] * 3,
                     scratch_shapes=[pltpu.VMEM((BAND, DA), BF16)] * 2 + [pltpu.VMEM((2 * BAND, DA), BF16)] * 2
                     + [pltpu.VMEM((BAND, DA), F32)] * 7,
                     compiler_params=_cp(2))(*ins)
    return dq.reshape(T, DA), dk.reshape(T, DA), dv.reshape(T, DA)


def _ssm_prep(lam_re, lam_im, log_dt, b_re, b_im, c_re, c_im):
    dt = jnp.exp(log_dt)[:, None]
    er = jnp.exp(lam_re * dt)
    a_re = er * jnp.cos(lam_im * dt)
    a_im = er * jnp.sin(lam_im * dt)
    nr, ni = a_re - 1.0, a_im
    den = lam_re * lam_re + lam_im * lam_im
    cr = (nr * lam_re + ni * lam_im) / den
    ci = (ni * lam_re - nr * lam_im) / den
    bbr = cr[..., None] * b_re - ci[..., None] * b_im
    bbi = cr[..., None] * b_im + ci[..., None] * b_re
    eye = jnp.eye(8, dtype=F32)

    def bblock(bb):
        t = bb.reshape(4, 8, 64, 16).transpose(0, 1, 3, 2)
        return (t[:, :, :, None, :] * eye[None, :, None, :, None]).reshape(4, 128, 512)

    def cblock(cc):
        t = cc.reshape(4, 8, 16, 64).transpose(0, 1, 3, 2)
        return (t[:, :, :, None, :] * eye[None, :, None, :, None]).reshape(4, 512, 128)

    return (a_re.reshape(NLB, 1, 128), a_im.reshape(NLB, 1, 128), bblock(bbr), bblock(bbi), cblock(c_re), cblock(c_im))


def _perm_matrix(tm):
    n = tm // 16
    pm = np.zeros((tm, tm), np.float32)
    for r in range(16):
        pm[16 * np.arange(n) + r, r * n + np.arange(n)] = 1.0
    return jnp.asarray(pm, BF16)


def _pieces(x):
    p1 = x.astype(BF16)
    r1 = x - p1.astype(F32)
    p2 = r1.astype(BF16)
    return p1, p2, (r1 - p2.astype(F32)).astype(BF16)


def _to_time(x, pm, exact=True):
    return sum(_dot(pm, p) for p in (_pieces(x) if exact else _split(x)))


def _to_streams(x, pm, exact=True):
    return sum(_dot_tn(pm, p) for p in (_pieces(x) if exact else _split(x)))


def _stream_block(tm, cols, lead=None):
    if lead is None:
        return pl.BlockSpec((16, tm // 16, cols), lambda i: (0, i, 0))
    return pl.BlockSpec((None, 16, tm // 16, cols), lambda i: (lead, 0, i, 0))


def _reorder(t3, to_streams, name):
    B, T, C = t3.shape
    tm = TM

    def body(x_ref, pm_ref, o_ref):
        if to_streams:
            o_ref[...] = _to_streams(x_ref[...], pm_ref[...]).reshape(o_ref.shape)
        else:
            o_ref[...] = _to_time(x_ref[...].reshape(tm, C), pm_ref[...])

    time_blk = pl.BlockSpec((None, tm, C), lambda b, i: (b, i, 0))
    stream_blk = pl.BlockSpec((None, 16, tm // 16, C), lambda b, i: (b, 0, i, 0))
    src = t3 if to_streams else t3.reshape(B, 16, T // 16, C)
    out = _pc(body, name=name, grid=(B, T // tm),
              in_specs=[time_blk if to_streams else stream_blk, pl.BlockSpec((tm, tm), lambda b, i: (0, 0))],
              out_specs=stream_blk if to_streams else time_blk,
              out_shape=S((B, 16, T // 16, C) if to_streams else (B, T, C), F32),
              compiler_params=_cp(2))(src, _perm_matrix(tm))
    return out.reshape(B, T, C)


def _ssm_in(P, bre, bim, layer):
    T = P.shape[1]
    tm = TM

    def body(u_ref, pm_ref, br_ref, bi_ref, un_ref, or_ref, oi_ref):
        u = _to_time(u_ref[...].reshape(tm, DSS), pm_ref[...], exact=False)
        un_ref[...] = u
        for s in range(4):
            uc = u[:, s * 128:(s + 1) * 128]
            r = _dot2(_dot, uc, br_ref[s], "b")
            m = _dot2(_dot, uc, bi_ref[s], "b")
            for q in range(4):
                or_ref[4 * s + q] = r[:, q * 128:(q + 1) * 128]
                oi_ref[4 * s + q] = m[:, q * 128:(q + 1) * 128]

    whole = pl.BlockSpec((4, 128, 512), lambda i: (0, 0, 0))
    st = pl.BlockSpec((NLB, tm, 128), lambda i: (0, i, 0))
    return _pc(body, name=f"ssm_in_l{layer}", grid=(T // tm,),
               in_specs=[_stream_block(tm, DSS, 3), pl.BlockSpec((tm, tm), lambda i: (0, 0)), whole, whole],
               out_specs=[pl.BlockSpec((tm, DSS), lambda i: (i, 0)), st, st],
               out_shape=[S((T, DSS), F32)] + [S((NLB, T, 128), F32)] * 2,
               compiler_params=_cp(1))(P.reshape(NSH, 16, T // 16, DSS), _perm_matrix(tm), bre, bim)


def _scan(br, bi, a_re, a_im, reverse, layer):
    T = br.shape[1]
    nbk = 4
    tt = min(T, 1024)
    nT = T // tt
    ntile = tt // 8
    sgn = -1.0 if reverse else 1.0
    last = 0 if reverse else 7

    def body(br_ref, bi_ref, ar_ref, ai_ref, xr_ref, xi_ref, cr, ci):
        @pl.when(pl.program_id(1) == 0)
        def _():
            cr[...] = jnp.zeros_like(cr)
            ci[...] = jnp.zeros_like(ci)

        row = lax.broadcasted_iota(jnp.int32, (8, 128), 0)
        consts = []
        for k in range(nbk):
            a1r = jnp.broadcast_to(ar_ref[k], (8, 128))
            a1i = sgn * jnp.broadcast_to(ai_ref[k], (8, 128))
            pows = [(a1r, a1i)]
            for _ in range(7):
                pr, pi_ = pows[-1]
                pows.append((a1r * pr - a1i * pi_, a1r * pi_ + a1i * pr))
            rounds = []
            for s in (1, 2, 4):
                inside = (row <= 7 - s) if reverse else (row >= s)
                rounds.append((jnp.where(inside, pows[s - 1][0], 0.0), jnp.where(inside, pows[s - 1][1], 0.0)))
            cmr, cmi = jnp.zeros((8, 128), F32), jnp.zeros((8, 128), F32)
            for r in range(8):
                e = (7 - r) if reverse else r
                cmr = jnp.where(row == r, pows[e][0], cmr)
                cmi = jnp.where(row == r, pows[e][1], cmi)
            consts.append((rounds, cmr, cmi))

        def tile(i, carry):
            j = (ntile - 1 - i) if reverse else i
            rows = pl.ds(pl.multiple_of(j * 8, 8), 8)
            out = []
            for k in range(nbk):
                rounds, cmr, cmi = consts[k]
                xr = br_ref[k, rows, :]
                xi = bi_ref[k, rows, :]
                for (mr, mi), s in zip(rounds, (1, 2, 4)):
                    sh = (8 - s) if reverse else s
                    rr = pltpu.roll(xr, sh, 0)
                    ri = pltpu.roll(xi, sh, 0)
                    xr, xi = xr + (mr * rr - mi * ri), xi + (mr * ri + mi * rr)
                c_r, c_i = carry[k]
                xr, xi = xr + (cmr * c_r - cmi * c_i), xi + (cmr * c_i + cmi * c_r)
                xr_ref[k, rows, :] = xr
                xi_ref[k, rows, :] = xi
                out.append((jnp.broadcast_to(xr[last:last + 1, :], (8, 128)),
                            jnp.broadcast_to(xi[last:last + 1, :], (8, 128))))
            return tuple(out)

        carry = lax.fori_loop(0, ntile, tile, tuple((cr[k], ci[k]) for k in range(nbk)), unroll=2)
        for k in range(nbk):
            cr[k] = carry[k][0]
            ci[k] = carry[k][1]

    tmap = (lambda t: nT - 1 - t) if reverse else (lambda t: t)
    st = pl.BlockSpec((nbk, tt, 128), lambda i, t: (i, tmap(t), 0))
    av = pl.BlockSpec((nbk, 1, 128), lambda i, t: (i, 0, 0))
    return _pc(body, name=f"scan_{'bwd' if reverse else 'fwd'}_l{layer}", grid=(NLB // nbk, nT),
               in_specs=[st, st, av, av], out_specs=[st, st], out_shape=[S((NLB, T, 128), F32)] * 2,
               scratch_shapes=[pltpu.VMEM((nbk, 8, 128), F32)] * 2, compiler_params=_cp(2))(br, bi, a_re, a_im)


def _ssm_out(xr, xi, u, cre, cim, dvec, wglu, bglu, layer):
    T = u.shape[0]
    tm = TM

    def body(xr_ref, xi_ref, u_ref, pm_ref, cr_ref, ci_ref, d_ref, w_ref, bg_ref, s_ref, y_ref, z_ref):
        ys = []
        for s in range(4):
            xrc = jnp.concatenate([xr_ref[4 * s + q] for q in range(4)], axis=1)
            xic = jnp.concatenate([xi_ref[4 * s + q] for q in range(4)], axis=1)
            ys.append(_dot2(_dot, xrc, cr_ref[s], "b") - _dot2(_dot, xic, ci_ref[s], "b"))
        y = jnp.concatenate(ys, axis=1) + d_ref[...] * u_ref[...]
        yg = _gelu(y)
        ygb = yg.astype(BF16)
        z = bg_ref[...] + sum(_dot(ygb[:, j * 128:(j + 1) * 128], w_ref[j]) for j in range(NSH))
        y_ref[...] = y
        z_ref[...] = z
        s_ref[...] = _to_streams(yg * jax.nn.sigmoid(z), pm_ref[...], exact=False).reshape(s_ref.shape)

    st = pl.BlockSpec((NLB, tm, 128), lambda i: (0, i, 0))
    cw = pl.BlockSpec((4, 512, 128), lambda i: (0, 0, 0))
    half = pl.BlockSpec((tm, DSS), lambda i: (i, 0))
    s, y, z = _pc(body, name=f"ssm_out_l{layer}", grid=(T // tm,),
                  in_specs=[st, st, half, pl.BlockSpec((tm, tm), lambda i: (0, 0)), cw, cw, _gain_spec(DSS, layer),
                            pl.BlockSpec((NSH, None, 128, DSS), lambda i: (0, 0, 0, 0)), _gain_spec(DSS, layer)],
                  out_specs=[_stream_block(tm, DSS), half, half],
                  out_shape=[S((16, T // 16, DSS), F32), S((T, DSS), F32), S((T, DSS), F32)],
                  compiler_params=_cp(1))(xr, xi, u, _perm_matrix(tm), cre, cim, dvec, wglu, bglu)
    return s.reshape(T, DSS), y, z


def _ssm_out_bwd(dssm, y, z, xr, xi, u, cre, cim, dvec, wglu, layer):
    T = u.shape[0]
    tm = TM

    def body(ds_ref, pm_ref, y_ref, z_ref, xr_ref, xi_ref, u_ref, cr_ref, ci_ref, d_ref, w_ref,
             gr_ref, gi_ref, du_ref, dz_ref, yg_ref, dbg_ref, dd_ref, dcr_ref, dci_ref):
        i = pl.program_id(0)

        @pl.when(i == 0)
        def _():
            dbg_ref[...] = jnp.zeros_like(dbg_ref)
            dd_ref[...] = jnp.zeros_like(dd_ref)
            dcr_ref[...] = jnp.zeros_like(dcr_ref)
            dci_ref[...] = jnp.zeros_like(dci_ref)

        yv = y_ref[...]
        yg = _gelu(yv)
        sg = jax.nn.sigmoid(z_ref[...])
        ds = _to_time(ds_ref[...].reshape(tm, DSS), pm_ref[...], exact=False)
        dz = ds * yg * sg * (1.0 - sg)
        dzb = dz.astype(BF16)
        dz_ref[...] = dzb
        yg_ref[...] = yg.astype(BF16)
        dbg_ref[...] += jnp.sum(dz, axis=0, keepdims=True)
        dyg = ds * sg + jnp.concatenate([_dot_nt(dzb, w_ref[j]) for j in range(NSH)], axis=1)
        dy = dyg * _gelu_grad(yv)
        u = u_ref[...]
        dd_ref[...] += jnp.sum(dy * u, axis=0, keepdims=True)
        du_ref[...] = dy * d_ref[...]
        for s in range(4):
            dyc = dy[:, s * 128:(s + 1) * 128]
            g_r = _dot2(_dot_nt, dyc, cr_ref[s], "b")
            g_i = -_dot2(_dot_nt, dyc, ci_ref[s], "b")
            for q in range(4):
                gr_ref[4 * s + q] = g_r[:, q * 128:(q + 1) * 128]
                gi_ref[4 * s + q] = g_i[:, q * 128:(q + 1) * 128]
            xrc = jnp.concatenate([xr_ref[4 * s + q] for q in range(4)], axis=1)
            xic = jnp.concatenate([xi_ref[4 * s + q] for q in range(4)], axis=1)
            dcr_ref[s] += _dot2(_dot_tn, xrc, dyc, "a")
            dci_ref[s] -= _dot2(_dot_tn, xic, dyc, "a")

    st = pl.BlockSpec((NLB, tm, 128), lambda i: (0, i, 0))
    cw = pl.BlockSpec((4, 512, 128), lambda i: (0, 0, 0))
    half = pl.BlockSpec((tm, DSS), lambda i: (i, 0))
    return _pc(body, name=f"ssm_out_bwd_l{layer}", grid=(T // tm,),
               in_specs=[_stream_block(tm, DSS), pl.BlockSpec((tm, tm), lambda i: (0, 0)), half, half, st, st, half,
                         cw, cw, _gain_spec(DSS, layer), pl.BlockSpec((NSH, None, 128, DSS), lambda i: (0, 0, 0, 0))],
               out_specs=[st, st, half, half, half, _row_acc_spec(DSS), _row_acc_spec(DSS), cw, cw],
               out_shape=[S((NLB, T, 128), F32)] * 2 + [S((T, DSS), F32), S((T, DSS), BF16), S((T, DSS), BF16),
                                                        S((1, DSS), F32), S((1, DSS), F32),
                                                        S((4, 512, 128), F32), S((4, 512, 128), F32)],
               compiler_params=_cp(1))(dssm.reshape(16, T // 16, DSS), _perm_matrix(tm), y, z, xr, xi, u, cre, cim,
                                       dvec, wglu)


def _ssm_da(gr, gi, xr, xi, layer):
    T = gr.shape[1]
    tb = 4096 if T % 4096 == 0 else T

    def body(gr_ref, gi_ref, xr_ref, xi_ref, dr_ref, di_ref, lr, li):
        t = pl.program_id(1)

        @pl.when(t == 0)
        def _():
            dr_ref[...] = jnp.zeros_like(dr_ref)
            di_ref[...] = jnp.zeros_like(di_ref)
            lr[...] = jnp.zeros_like(lr)
            li[...] = jnp.zeros_like(li)

        g_r, g_i, x_r, x_i = gr_ref[...], gi_ref[...], xr_ref[...], xi_ref[...]
        pr = pltpu.roll(x_r, 1, 0)
        pi_ = pltpu.roll(x_i, 1, 0)
        g0r, g0i = g_r[0:1, :], g_i[0:1, :]
        fr = lr[7:8, :] - x_r[tb - 1:tb, :]
        fi = li[7:8, :] - x_i[tb - 1:tb, :]
        dr_ref[...] += jnp.sum(g_r * pr + g_i * pi_, axis=0, keepdims=True) + g0r * fr + g0i * fi
        di_ref[...] += jnp.sum(g_i * pr - g_r * pi_, axis=0, keepdims=True) + g0i * fr - g0r * fi
        lr[...] = x_r[tb - 8:tb, :]
        li[...] = x_i[tb - 8:tb, :]

    st = pl.BlockSpec((None, tb, 128), lambda k, t: (k, t, 0))
    out = pl.BlockSpec((None, 1, 128), lambda k, t: (k, 0, 0))
    return _pc(body, name=f"ssm_da_l{layer}", grid=(NLB, T // tb), in_specs=[st] * 4, out_specs=[out, out],
               out_shape=[S((NLB, 1, 128), F32)] * 2, scratch_shapes=[pltpu.VMEM((8, 128), F32)] * 2,
               compiler_params=_cp(2))(gr, gi, xr, xi)


def _ssm_in_bwd(gr, gi, u, bre, bim, du_direct, layer):
    T = u.shape[0]
    tm = TM

    def body(gr_ref, gi_ref, u_ref, pm_ref, br_ref, bi_ref, dd_ref, du_ref, dbr_ref, dbi_ref):
        i = pl.program_id(0)

        @pl.when(i == 0)
        def _():
            dbr_ref[...] = jnp.zeros_like(dbr_ref)
            dbi_ref[...] = jnp.zeros_like(dbi_ref)

        dus = []
        for s in range(4):
            grc = jnp.concatenate([gr_ref[4 * s + q] for q in range(4)], axis=1)
            gic = jnp.concatenate([gi_ref[4 * s + q] for q in range(4)], axis=1)
            uc = u_ref[:, s * 128:(s + 1) * 128]
            dus.append(_dot2(_dot_nt, grc, br_ref[s], "b") + _dot2(_dot_nt, gic, bi_ref[s], "b"))
            dbr_ref[s] += _dot2(_dot_tn, uc, grc, "a")
            dbi_ref[s] += _dot2(_dot_tn, uc, gic, "a")
        du = jnp.concatenate(dus, axis=1) + dd_ref[...]
        du_ref[...] = _to_streams(du, pm_ref[...], exact=False).reshape(du_ref.shape)

    whole = pl.BlockSpec((4, 128, 512), lambda i: (0, 0, 0))
    st = pl.BlockSpec((NLB, tm, 128), lambda i: (0, i, 0))
    half = pl.BlockSpec((tm, DSS), lambda i: (i, 0))
    du, dbr, dbi = _pc(body, name=f"ssm_in_bwd_l{layer}", grid=(T // tm,),
                       in_specs=[st, st, half, pl.BlockSpec((tm, tm), lambda i: (0, 0)), whole, whole, half],
                       out_specs=[_stream_block(tm, DSS), whole, whole],
                       out_shape=[S((16, T // 16, DSS), F32), S((4, 128, 512), F32), S((4, 128, 512), F32)],
                       compiler_params=_cp(1))(gr, gi, u, _perm_matrix(tm), bre, bim, du_direct)
    return du.reshape(T, DSS), dbr, dbi


def _mix_out(outs, lses, ssm, h, attn_g, ssm_g, post_g, wout, layer):
    T = h.shape[0]
    tm = TM

    def body(o1, o2, o3, l1, l2, l3, s_ref, h_ref, ag_ref, sg_ref, pg_ref, w_ref, ho_ref, at_ref, ls_ref, mx_ref, mo_ref):
        la, lb, lc = l1[...], l2[...], l3[...]
        m = jnp.maximum(jnp.maximum(la, lb), lc)
        wa, wb, wc = jnp.exp(la - m), jnp.exp(lb - m), jnp.exp(lc - m)
        zs = wa + wb + wc
        attn = (wa * o1[...] + wb * o2[...] + wc * o3[...]) / zs
        at_ref[...] = attn
        ls_ref[...] = m + jnp.log(zs)
        mixed = jnp.concatenate([_rms_fwd(attn, ag_ref[...]), _rms_fwd(s_ref[...], sg_ref[...])], axis=1).astype(BF16)
        mx_ref[...] = mixed
        mo = sum(_dot(mixed[:, j * 256:(j + 1) * 256], w_ref[j]) for j in range(NSH))
        mo_ref[...] = mo
        ho_ref[...] = h_ref[...] + _rms_fwd(mo, pg_ref[...])

    row = pl.BlockSpec((tm, D), lambda i: (i, 0))
    half = pl.BlockSpec((tm, DA), lambda i: (i, 0))
    return _pc(body, name=f"mix_out_l{layer}", grid=(T // tm,),
               in_specs=[half] * 7 + [row, _gain_spec(DA, layer), _gain_spec(DSS, layer), _gain_spec(D, layer),
                                      pl.BlockSpec((NSH, None, 256, D), lambda i: (0, 0, 0, 0))],
               out_specs=[row, half, half, row, row],
               out_shape=[S((T, D), F32), S((T, DA), F32), S((T, DA), F32), S((T, D), BF16), S((T, D), F32)],
               compiler_params=_cp(1))(*outs, *lses, ssm, h, attn_g, ssm_g, post_g, wout)


def _mix_out_bwd(dout, mo, attn, ssm, attn_g, ssm_g, post_g, wout, layer):
    T = dout.shape[0]
    tm = TM
    head_sum =jnp.asarray(np.kron(np.eye(NH, dtype=np.float32), np.ones((HD, HD), np.float32)), BF16)

    def body(do_ref, mo_ref, at_ref, s_ref, ag_ref, sg_ref, pg_ref, w_ref, e_ref,
             da_ref, ds_ref, dl_ref, dmo_ref, dpg_ref, dag_ref, dsg_ref):
        i = pl.program_id(0)

        @pl.when(i == 0)
        def _():
            dpg_ref[...] = jnp.zeros_like(dpg_ref)
            dag_ref[...] = jnp.zeros_like(dag_ref)
            dsg_ref[...] = jnp.zeros_like(dsg_ref)

        dmo, dpg = _rms_bwd(do_ref[...], mo_ref[...], pg_ref[...])
        dpg_ref[...] += dpg
        dmob = dmo.astype(BF16)
        dmo_ref[...] = dmob
        dmix = jnp.concatenate([_dot_nt(dmob, w_ref[j]) for j in range(NSH)], axis=1)
        attn = at_ref[...]
        dat, dag = _rms_bwd(dmix[:, :DA], attn, ag_ref[...])
        dss, dsg = _rms_bwd(dmix[:, DA:], s_ref[...], sg_ref[...])
        dag_ref[...] += dag
        dsg_ref[...] += dsg
        da_ref[...] = dat
        ds_ref[...] = dss
        prod = dat * attn
        p1 = prod.astype(BF16)
        r1 = prod - p1.astype(F32)
        p2 = r1.astype(BF16)
        p3 = (r1 - p2.astype(F32)).astype(BF16)
        e = e_ref[...]
        dl_ref[...] = _dot(p1, e) + _dot(p2, e) + _dot(p3, e)

    row = pl.BlockSpec((tm, D), lambda i: (i, 0))
    half = pl.BlockSpec((tm, DA), lambda i: (i, 0))
    return _pc(body, name=f"mix_out_bwd_l{layer}", grid=(T // tm,),
               in_specs=[row, row, half, half, _gain_spec(DA, layer), _gain_spec(DSS, layer), _gain_spec(D, layer),
                         pl.BlockSpec((NSH, None, 256, D), lambda i: (0, 0, 0, 0)),
                         pl.BlockSpec((DA, DA), lambda i: (0, 0))],
               out_specs=[half, half, half, row, _row_acc_spec(D), _row_acc_spec(DA), _row_acc_spec(DSS)],
               out_shape=[S((T, DA), F32)] * 3 + [S((T, D), BF16), S((1, D), F32), S((1, DA), F32), S((1, DSS), F32)],
               compiler_params=_cp(1))(dout, mo, attn, ssm, attn_g, ssm_g, post_g, wout, head_sum)


def _ple_fwd(h, p3, wup, wgate, post_g, layer):
    T = h.shape[0]
    tm = TM

    def body(h_ref, p_ref, wu_ref, wg_ref, g_ref, ho_ref, e_ref, gt_ref):
        hv = h_ref[...]
        hb = hv.astype(BF16)
        pb = p_ref[...].astype(BF16)
        gte = sum(_dot(hb[:, j * 256:(j + 1) * 256], wg_ref[j]) for j in range(NSH))
        e = jnp.concatenate([_dot(pb, wu_ref[j]) for j in range(NSH)], axis=1)
        e_ref[...] = e
        gt_ref[...] = gte
        ho_ref[...] = hv + _rms_fwd(e * jax.nn.sigmoid(gte), g_ref[...])

    row = pl.BlockSpec((tm, D), lambda i: (i, 0))
    return _pc(body, name=f"ple_fwd_l{layer}", grid=(T // tm,),
               in_specs=[row, pl.BlockSpec((None, tm, PLE), lambda i: (layer, i, 0)),
                         pl.BlockSpec((NSH, None, PLE, 256), lambda i: (0, 0, 0, 0)),
                         pl.BlockSpec((NSH, None, 256, D), lambda i: (0, 0, 0, 0)), _gain_spec(D, layer)],
               out_specs=[row, row, row], out_shape=[S((T, D), F32)] * 3,
               compiler_params=_cp(1))(h, p3, wup, wgate, post_g)


def _ple_bwd(dout, e, gte, wgate, post_g, layer):
    T = dout.shape[0]
    tm = TM

    def body(do_ref, e_ref, gt_ref, wg_ref, g_ref, dh_ref, de_ref, dgt_ref, dg_ref):
        i = pl.program_id(0)

        @pl.when(i == 0)
        def _():
            dg_ref[...] = jnp.zeros_like(dg_ref)

        ev = e_ref[...]
        sg = jax.nn.sigmoid(gt_ref[...])
        do = do_ref[...]
        dple, dg = _rms_bwd(do, ev * sg, g_ref[...])
        dg_ref[...] += dg
        de = (dple * sg).astype(BF16)
        for j in range(NSH):
            de_ref[j] = de[:, j * 256:(j + 1) * 256]
        dgb = (dple * ev * sg * (1.0 - sg)).astype(BF16)
        dgt_ref[...] = dgb
        dh_ref[...] = do + jnp.concatenate([_dot_nt(dgb, wg_ref[j]) for j in range(NSH)], axis=1)

    row = pl.BlockSpec((tm, D), lambda i: (i, 0))
    return _pc(body, name=f"ple_bwd_l{layer}", grid=(T // tm,),
               in_specs=[row, row, row, pl.BlockSpec((NSH, None, 256, D), lambda i: (0, 0, 0, 0)), _gain_spec(D, layer)],
               out_specs=[row, pl.BlockSpec((NSH, tm, 256), lambda i: (0, i, 0)), row, _row_acc_spec(D)],
               out_shape=[S((T, D), F32), S((NSH, T, 256), BF16), S((T, D), BF16), S((1, D), F32)],
               compiler_params=_cp(1))(dout, e, gte, wgate, post_g)


def _loss_head(h, target):
    T = h.shape[0]
    tm = TM

    def body(h_ref, t_ref, dy_ref, l_ref):
        i = pl.program_id(0)

        @pl.when(i == 0)
        def _():
            l_ref[...] = jnp.zeros_like(l_ref)

        err = h_ref[...] - t_ref[...]
        dy_ref[...] = err * (1.0 / D)
        l_ref[...] += jnp.broadcast_to((0.5 / D) * jnp.sum(err * err), (1, 128))

    row = pl.BlockSpec((tm, D), lambda i: (i, 0))
    return _pc(body, name="loss_head", grid=(T // tm,), in_specs=[row, row],
               out_specs=[row, pl.BlockSpec((1, 128), lambda i: (0, 0))],
               out_shape=[S((T, D), F32), S((1, 128), F32)], compiler_params=_cp(1))(h, target)


def _local_step(x, p3, pos_col, target, weights_of, layer_grads_done, Sm):
    L = p3.shape[0]
    g3 = {n: Sm[n].reshape(L, 1, -1) for n in ("ffn1_pre_g", "ffn1_post_g", "mix_pre_g", "attn_norm_g", "ssm_norm_g",
                                                "mix_post_g", "ffn2_pre_g", "ffn2_post_g", "ple_post_g", "ssm_b_glu", "ssm_d")}
    rot = _rot_tables(pos_col)
    prep_names = ("ssm_lam_re", "ssm_lam_im", "ssm_log_dt", "ssm_b_re", "ssm_b_im", "ssm_c_re", "ssm_c_im")
    prep_all, prep_vjp = jax.vjp(jax.vmap(_ssm_prep), *[Sm[n] for n in prep_names])
    prep_cot = [None] * L

    saved = []
    h = x
    for l in range(L):
        W = weights_of(l, h)
        sv = {"h0": h, "W": W}
        h, sv["a1"], sv["b1"], sv["f1"], sv["xn1"] = _ffn_fwd(
            h, g3["ffn1_pre_g"], g3["ffn1_post_g"], W["ffn1_w_gate"], W["ffn1_w_up"], W["ffn1_w_down"], l, "1")
        sv["h1"] = h
        P, sv["ain"] = _mix_proj(h, g3["mix_pre_g"], W["w_in"], rot, l)
        sv["P"] = P
        ol = [_attn_fwd(P, d, l) for d in PATTERN_DILATIONS]
        prep = tuple(t[l] for t in prep_all)
        a_re, a_im, bre, bim, cre, cim = prep
        sv["prep"] = prep
        sv["u"], bur, bui = _ssm_in(P, bre, bim, l)
        xr, xi = _scan(bur, bui, a_re, a_im, False, l)
        sv["xr"], sv["xi"] = xr, xi
        ssm, sv["y"], sv["z"] = _ssm_out(xr, xi, sv["u"], cre, cim, g3["ssm_d"], W["ssm_w_glu"], g3["ssm_b_glu"], l)
        sv["ssm"] = ssm
        h, sv["attn"], sv["lse"], sv["mixed"], sv["mo"] = _mix_out(
            [o for o, _ in ol], [s for _, s in ol], ssm, h, g3["attn_norm_g"], g3["ssm_norm_g"], g3["mix_post_g"],
            W["w_out"], l)
        sv["h2"] = h
        h, sv["a2"], sv["b2"], sv["f2"], sv["xn2"] = _ffn_fwd(
            h, g3["ffn2_pre_g"], g3["ffn2_post_g"], W["ffn2_w_gate"], W["ffn2_w_up"], W["ffn2_w_down"], l, "2")
        sv["h3"] = h
        h, sv["e"], sv["gte"] = _ple_fwd(h, p3, W["ple_w_up"], W["ple_w_gate"], g3["ple_post_g"], l)
        saved.append(sv)

    dh, loss = _loss_head(h, target)

    G_layers = [{n: lax.empty((NSH, 1, r, c), BF16) for n, r, c in BIG} for _ in range(L)]
    sg = {n: [None] * L for n in SMALL}
    whole, shard, kcol = "whole", "shard", "cols"
    ple_g = g3["ple_post_g"]
    for l in reversed(range(L)):
        sv = saved[l]
        W = sv["W"]
        G, gl = G_layers[l], 0
        if l + 1 < L:
            ple_g = ple_g + layer_grads_done(l + 1, G_layers[l + 1])
        dh, de, dgte, sg["ple_post_g"][l] = _ple_bwd(dh, sv["e"], sv["gte"], W["ple_w_gate"], ple_g, l)
        G["ple_w_up"] = _dw(p3[l][None], de, G["ple_w_up"], gl, PLE, 256, whole, shard, f"dw_ple_up_l{l}")
        G["ple_w_gate"] = _dw(sv["h3"][None], dgte[None], G["ple_w_gate"], gl, 256, D, kcol, whole, f"dw_ple_gate_l{l}")
        dh, df, da, db, hh, sg["ffn2_pre_g"][l], sg["ffn2_post_g"][l] = _ffn_bwd(
            dh, sv["h2"], sv["f2"], sv["a2"], sv["b2"], g3["ffn2_pre_g"], g3["ffn2_post_g"],
            W["ffn2_w_gate"], W["ffn2_w_up"], W["ffn2_w_down"], l, "2")
        G["ffn2_w_gate"] = _dw(da, sv["xn2"][None], G["ffn2_w_gate"], gl, DFS, D, shard, whole, f"dw_ffn2_gate_l{l}")
        G["ffn2_w_up"] = _dw(db, sv["xn2"][None], G["ffn2_w_up"], gl, DFS, D, shard, whole, f"dw_ffn2_up_l{l}")
        G["ffn2_w_down"] = _dw(hh, df[None], G["ffn2_w_down"], gl, DFS, D, shard, whole, f"dw_ffn2_down_l{l}")
        a_re, a_im, bre, bim, cre, cim = sv["prep"]
        dattn, dssm, delta, dmo, sg["mix_post_g"][l], sg["attn_norm_g"][l], sg["ssm_norm_g"][l] = _mix_out_bwd(
            dh, sv["mo"], sv["attn"], sv["ssm"], g3["attn_norm_g"], g3["ssm_norm_g"], g3["mix_post_g"], W["w_out"], l)
        G["w_out"] = _dw(sv["mixed"][None], dmo[None], G["w_out"], gl, 256, D, kcol, whole, f"dw_out_l{l}")
        gnr, gni, du_direct, dz, yg, sg["ssm_b_glu"][l], dd, dcre, dcim = _ssm_out_bwd(
            dssm, sv["y"], sv["z"], sv["xr"], sv["xi"], sv["u"], cre, cim, g3["ssm_d"], W["ssm_w_glu"], l)
        sg["ssm_d"][l] = dd.reshape(Sm["ssm_d"].shape[1:])
        G["ssm_w_glu"] = _dw(yg[None], dz[None], G["ssm_w_glu"], gl, 128, DSS, kcol, whole, f"dw_glu_l{l}")
        gr, gi = _scan(gnr, gni, a_re, a_im, True, l)
        dar, dai = _ssm_da(gr, gi, sv["xr"], sv["xi"], l)
        du, dbre, dbim = _ssm_in_bwd(gr, gi, sv["u"], bre, bim, du_direct, l)
        prep_cot[l] = (dar, dai, dbre, dbim, dcre, dcim)
        acc = None
        for d in PATTERN_DILATIONS:
            acc = _attn_bwd(sv["P"], dattn, sv["lse"], delta, acc, d, l)
        dh, dP, sg["mix_pre_g"][l] = _mix_proj_bwd(acc[0], acc[1], acc[2], du, dh, sv["h1"], g3["mix_pre_g"],
                                                   W["w_in"], rot, l)
        G["w_in"] = _dw(sv["ain"][None], dP, G["w_in"], gl, D, DA,whole, shard, f"dw_in_l{l}")
        dh, df, da, db, hh, sg["ffn1_pre_g"][l], sg["ffn1_post_g"][l] = _ffn_bwd(
            dh, sv["h0"], sv["f1"], sv["a1"], sv["b1"], g3["ffn1_pre_g"], g3["ffn1_post_g"],
            W["ffn1_w_gate"], W["ffn1_w_up"], W["ffn1_w_down"], l, "1")
        G["ffn1_w_gate"] = _dw(da, sv["xn1"][None], G["ffn1_w_gate"], gl, DFS, D, shard, whole, f"dw_ffn1_gate_l{l}")
        G["ffn1_w_up"] = _dw(db, sv["xn1"][None], G["ffn1_w_up"], gl, DFS, D, shard, whole, f"dw_ffn1_up_l{l}")
        G["ffn1_w_down"] = _dw(hh, df[None], G["ffn1_w_down"], gl, DFS, D, shard, whole, f"dw_ffn1_down_l{l}")

    small = {n: jnp.stack([g.reshape(Sm[n].shape[1:]) for g in sg[n]]) for n in SMALL if n not in prep_names}
    small.update(zip(prep_names, prep_vjp(tuple(jnp.stack(c) for c in zip(*prep_cot)))))
    return loss, dh, G_layers[0], small


HBM_SPEC = pl.BlockSpec(memory_space=pltpu.HBM)


def _place():
    x, y, c = lax.axis_index("x"), lax.axis_index("y"), lax.axis_index("c")
    chips = [(1 - x, y), (x, 1 - y), (1 - x, 1 - y)]
    return x, y, c, chips


def _comm_params():
    return pltpu.CompilerParams(vmem_limit_bytes=VMEM_LIMIT)


def _gather_weights(ws, lands):
    n = len(ws)

    def body(*refs):
        ins, outs = refs[:n], refs[2 * n:3 * n]
        s_ici, r_ici, s_d2d, r_d2d = refs[3 * n:]
        x, y, c, chips = _place()

        def half(ref, t, hc):
            r2 = ws[t].shape[1] // 2
            return ref.at[:, pl.ds(hc * r2, r2), :]

        def ici(t, k, src_chip, to):
            j = 2 * src_chip[0] + src_chip[1]
            src = half(ins[t], t, c) if to is not None else half(outs[t].at[j], t, c)
            return pltpu.make_async_remote_copy(src_ref=src, dst_ref=half(outs[t].at[j], t, c),
                                                send_sem=s_ici.at[3 * t + k], recv_sem=r_ici.at[3 * t + k],
                                                device_id=to if to is not None else (x, y, c), device_id_type=MESH)

        def d2d(t, k, hc):
            j = 2 * chips[k][0] + chips[k][1]
            r = half(outs[t].at[j], t, hc)
            return pltpu.make_async_remote_copy(src_ref=r, dst_ref=r, send_sem=s_d2d.at[3 * t + k],
                                                recv_sem=r_d2d.at[3 * t + k], device_id=(x, y, 1 - c),
                                                device_id_type=MESH)

        sends = [ici(t, k, (x, y), (*chips[k], c)) for t in range(n) for k in range(3)]
        for cp in sends:
            cp.start()
        passed = []
        for t in range(n):
            for k in range(3):
                ici(t, k, chips[k], None).wait_recv()
                passed.append(d2d(t, k, c))
                passed[-1].start()
        for t in range(n):
            for k in range(3):
                d2d(t, k, 1 - c).wait_recv()
        for cp in sends + passed:
            cp.wait_send()

    return _pc(body, name="gather_weights", in_specs=[HBM_SPEC] * (2 * n), out_specs=[HBM_SPEC] * n,
               out_shape=[S(z.shape, z.dtype) for z in lands], input_output_aliases={n + t: t for t in range(n)},
               scratch_shapes=[pltpu.SemaphoreType.DMA((3 * n,))] * 4, compiler_params=_comm_params())(*ws, *lands)


SEM_SPEC = pl.BlockSpec(memory_space=pltpu.SEMAPHORE)
ANY_SPEC = pl.BlockSpec(memory_space=pl.ANY)
SPLIT_EFFECT = pltpu.SideEffectType.DATAFLOW_SIDE_EFFECTING


def _in_hbm(t):
    return pltpu.with_memory_space_constraint(t, pltpu.HBM)


def _place_own(ws, me_arr, layer):
    n = len(ws)

    def body(me_ref, *refs):
        for t in range(n):
            refs[n + t][...] = refs[t][...]

    gs = pltpu.PrefetchScalarGridSpec(
        num_scalar_prefetch=1, grid=(2,),
        in_specs=[pl.BlockSpec((w.shape[0], w.shape[1] // 2, w.shape[2]), lambda i, me: (0, i, 0)) for w in ws],
        out_specs=[pl.BlockSpec((None, w.shape[0], w.shape[1] // 2, w.shape[2]), lambda i, me: (me[0], 0, i, 0))
                   for w in ws])
    return _pc(body, name=f"gather_place_own_l{layer}", grid_spec=gs,
               out_shape=[S((NSH,) + w.shape, w.dtype) for w in ws], compiler_params=_cp(1))(me_arr, *ws)


def _gather_start(ws, lands, after, layer):
    n = len(ws)

    def body(*refs):
        ins, lz = refs[:n], refs[n:2 * n]
        s_sem, r_sem = refs[2 * n + 1], refs[2 * n + 2]
        token = refs[-1]
        x, y, c, chips = _place()
        for t in range(n):
            for k in range(3):
                pltpu.make_async_remote_copy(src_ref=ins[t], dst_ref=lz[t].at[2 * x + y], send_sem=s_sem.at[3 * t + k],
                                             recv_sem=r_sem.at[3 * t + k], device_id=(*chips[k], c),
                                             device_id_type=MESH).start()
        token[...] = jnp.zeros_like(token)

    hbm = [pltpu.HBM(w.shape, w.dtype) for w in ws] + [pltpu.HBM(z.shape, z.dtype) for z in lands]
    out = _pc(body, name=f"gather_start_l{layer}",
              out_shape=(pltpu.SemaphoreType.DMA((3 * n,)), pltpu.SemaphoreType.DMA((3 * n,)), *hbm, S((8, 128), F32)),
              in_specs=[HBM_SPEC] * (2 * n) + [ANY_SPEC],
              out_specs=(SEM_SPEC, SEM_SPEC, *([HBM_SPEC] * (2 * n)), pl.BlockSpec(memory_space=pltpu.VMEM)),
              input_output_aliases={i: 2 + i for i in range(2 * n)},
              compiler_params=pltpu.CompilerParams(has_side_effects=SPLIT_EFFECT))(
                  *[_in_hbm(w) for w in ws], *[_in_hbm(z) for z in lands], after)
    return out[0], out[1], out[2:2 + n], out[2 + n:2 + 2 * n], out[-1]


def _gather_wait(s_sem, r_sem, ws, lands, after, layer):
    n = len(ws)

    def body(*refs):
        ins, lz = refs[:n], refs[n:2 * n]
        s_ref, r_ref = refs[2 * n], refs[2 * n + 1]
        x, y, c, chips = _place()
        for t in range(n):
            for k in range(3):
                cp = pltpu.make_async_remote_copy(src_ref=ins[t], dst_ref=lz[t].at[2 * x + y], send_sem=s_ref.at[3 * t + k],
                                                  recv_sem=r_ref.at[3 * t + k], device_id=(*chips[k], c),
                                                  device_id_type=MESH)
                cp.wait_send()
                cp.wait_recv()

    hbm = [pltpu.HBM(w.shape, w.dtype) for w in ws] + [pltpu.HBM(z.shape, z.dtype) for z in lands]
    out = _pc(body, name=f"gather_wait_l{layer}", out_shape=tuple(hbm),
              in_specs=[HBM_SPEC] * (2 * n) + [SEM_SPEC, SEM_SPEC, ANY_SPEC], out_specs=tuple([HBM_SPEC] * (2 * n)),
              input_output_aliases={i: i for i in range(2 * n)},
              compiler_params=pltpu.CompilerParams(has_side_effects=SPLIT_EFFECT))(*ws, *lands, s_sem, r_sem, after)
    return out[n:]


def _swap_halves(gs, tag):
    n = len(gs)

    def body(*refs):
        ins, outs = refs[:n], refs[n:2 * n]
        s_sem, r_sem = refs[2 * n:]
        x, y, c, _ = _place()
        cps = []
        for t in range(n):
            r2 = gs[t].shape[2] // 2
            cps.append(pltpu.make_async_remote_copy(
                src_ref=ins[t].at[:, :, pl.ds((1 - c) * r2, r2), :], dst_ref=outs[t], send_sem=s_sem.at[t],
                recv_sem=r_sem.at[t], device_id=(x, y, 1 - c), device_id_type=MESH))
            cps[-1].start()
        for cp in cps:
            cp.wait_recv()
        for cp in cps:
            cp.wait_send()

    return _pc(body, name=f"grad_swap_halves_{tag}", in_specs=[HBM_SPEC] * n, out_specs=[HBM_SPEC] * n,
               out_shape=[S(g.shape[:2] + (g.shape[2] // 2, g.shape[3]), g.dtype) for g in gs],
               scratch_shapes=[pltpu.SemaphoreType.DMA((n,))] * 2, compiler_params=_comm_params())(*gs)


def _add_half(g, landed, c_arr, name):
    _, L, r2, cols = landed.shape

    def body(c_ref, g_ref, l_ref, o_ref):
        o_ref[...] = (g_ref[...].astype(F32) + l_ref[...].astype(F32)).astype(BF16)

    gs = pltpu.PrefetchScalarGridSpec(
        num_scalar_prefetch=1, grid=(NSH, L),
        in_specs=[pl.BlockSpec((None, None, r2, cols), lambda j, l, c: (j, l, c[0], 0)),
                  pl.BlockSpec((None, None, r2, cols), lambda j, l, c: (j, l, 0, 0))],
        out_specs=pl.BlockSpec((None, None, r2, cols), lambda j, l, c: (j, l, 0, 0)))
    return _pc(body, name=name, grid_spec=gs, out_shape=S(landed.shape, BF16), compiler_params=_cp(2))(c_arr, g, landed)


def _partial_copies(ins, lz, s_sem, r_sem):
    x, y, c, chips = _place()
    return [pltpu.make_async_remote_copy(src_ref=ins[t].at[2 * chips[k][0] + chips[k][1]], dst_ref=lz[t].at[k],
                                         send_sem=s_sem.at[3 * t + k], recv_sem=r_sem.at[3 * t + k],
                                         device_id=(*chips[k], c), device_id_type=MESH)
            for t in range(len(ins)) for k in range(3)]


def _partial_send_start(ps, lands):
    n = len(ps)

    def body(*refs):
        for cp in _partial_copies(refs[:n], refs[n:2 * n], refs[2 * n], refs[2 * n + 1]):
            cp.start()
        refs[-1][...] = jnp.zeros_like(refs[-1])

    hbm = [pltpu.HBM(p.shape, p.dtype) for p in ps] + [pltpu.HBM(z.shape, z.dtype) for z in lands]
    out = _pc(body, name="grad_partial_send_start",
              out_shape=(pltpu.SemaphoreType.DMA((3 * n,)), pltpu.SemaphoreType.DMA((3 * n,)), *hbm, S((8, 128), F32)),
              in_specs=[HBM_SPEC] * (2 * n),
              out_specs=(SEM_SPEC, SEM_SPEC, *([HBM_SPEC] * (2 * n)), pl.BlockSpec(memory_space=pltpu.VMEM)),
              input_output_aliases={i: 2 + i for i in range(2 * n)},
              compiler_params=pltpu.CompilerParams(has_side_effects=SPLIT_EFFECT))(
                  *[_in_hbm(p) for p in ps], *[_in_hbm(z) for z in lands])
    return out[0], out[1], out[2:2 + n], out[2 + n:2 + 2 * n], out[-1]


def _partial_send_wait(s_sem, r_sem, ps, lands, after):
    n = len(ps)

    def body(*refs):
        for cp in _partial_copies(refs[:n], refs[n:2 * n], refs[2 * n], refs[2 * n + 1]):
            cp.wait_send()
            cp.wait_recv()

    hbm = [pltpu.HBM(p.shape, p.dtype) for p in ps] + [pltpu.HBM(z.shape, z.dtype) for z in lands]
    out = _pc(body, name="grad_partial_send_wait", out_shape=tuple(hbm),
              in_specs=[HBM_SPEC] * (2 * n) + [SEM_SPEC, SEM_SPEC, ANY_SPEC], out_specs=tuple([HBM_SPEC] * (2 * n)),
              input_output_aliases={i: i for i in range(2 * n)},
              compiler_params=pltpu.CompilerParams(has_side_effects=SPLIT_EFFECT))(*ps, *lands, s_sem, r_sem, after)
    return out[:n], out[n:]


def _sum_shards(part, landed, me_arr, c_arr, buf, first_layer, name):
    _, nl, r2, cols = landed.shape

    def body(me_ref, c_ref, p_ref, l_ref, b_ref, o_ref):
        o_ref[...] = ((p_ref[...].astype(F32) + l_ref[0].astype(F32)) + l_ref[1].astype(F32)) + l_ref[2].astype(F32)

    gs = pltpu.PrefetchScalarGridSpec(
        num_scalar_prefetch=2, grid=(nl,),
        in_specs=[pl.BlockSpec((None, None, r2, cols), lambda l, me, c: (me[0], l, 0, 0)),
                  pl.BlockSpec((3, None, r2, cols), lambda l, me, c: (0, l, 0, 0)), ANY_SPEC],
        out_specs=pl.BlockSpec((None, r2, cols), lambda l, me, c: (first_layer + l, c[0], 0)))
    return _pc(body, name=name, grid_spec=gs, out_shape=S(buf.shape, F32), input_output_aliases={4: 0},
               compiler_params=_cp(1))(me_arr, c_arr, part, landed, buf)


def _direct_grad_copies(ins, lz, s_sem, r_sem):
    x, y, c, chips = _place()
    sends, recvs = [], []
    for t in range(len(ins)):
        r2 = ins[t].shape[2] // 2
        half = lambda j, h: ins[t].at[j, :, pl.ds(h * r2, r2), :]

        def copy(src, slot, s_idx, r_idx, to):
            return pltpu.make_async_remote_copy(src_ref=src, dst_ref=lz[t].at[slot], send_sem=s_sem.at[7 * t + s_idx],
                                                recv_sem=r_sem.at[7 * t + r_idx], device_id=to, device_id_type=MESH)

        for k in range(3):
            for h in range(2):
                sends.append(copy(half(2 * chips[k][0] + chips[k][1], h), 2 * k + c, 2 * k + h, 2 * k + c, (*chips[k], h)))
        sends.append(copy(half(2 * x + y, 1 - c), 6, 6, 6, (x, y, 1 - c)))
        recvs += [copy(half(0, 0), s, s, s, (x, y, c)) for s in range(7)]
    return sends, recvs


def _send_start(gs, lands, layer):
    n = len(gs)
    ps = gs

    def body(*refs):
        sends, _ = _direct_grad_copies(refs[:n], refs[n:2 * n], refs[2 * n], refs[2 * n + 1])
        for cp in sends:
            cp.start()
        refs[-1][...] = jnp.zeros_like(refs[-1])

    hbm = [pltpu.HBM(p.shape, p.dtype) for p in ps] + [pltpu.HBM(z.shape, z.dtype) for z in lands]
    out = _pc(body, name=f"grad_send_start_l{layer}",
              out_shape=(pltpu.SemaphoreType.DMA((7 * n,)), pltpu.SemaphoreType.DMA((7 * n,)), *hbm, S((8, 128), F32)),
              in_specs=[HBM_SPEC] * (2 * n),
              out_specs=(SEM_SPEC, SEM_SPEC, *([HBM_SPEC] * (2 * n)), pl.BlockSpec(memory_space=pltpu.VMEM)),
              input_output_aliases={i: 2 + i for i in range(2 * n)},
              compiler_params=pltpu.CompilerParams(has_side_effects=SPLIT_EFFECT))(
                  *[_in_hbm(p) for p in ps], *[_in_hbm(z) for z in lands])
    return out[0], out[1], out[2:2 + n], out[2 + n:2 + 2 * n], out[-1]


def _send_wait(s_sem, r_sem, ps, lands, after, layer):
    n = len(ps)

    def body(*refs):
        sends, recvs = _direct_grad_copies(refs[:n], refs[n:2 * n], refs[2 * n], refs[2 * n + 1])
        for cp in sends:
            cp.wait_send()
        for cp in recvs:
            cp.wait_recv()

    hbm = [pltpu.HBM(p.shape, p.dtype) for p in ps] + [pltpu.HBM(z.shape, z.dtype) for z in lands]
    out = _pc(body, name=f"grad_send_wait_l{layer}", out_shape=tuple(hbm),
              in_specs=[HBM_SPEC] * (2 * n) + [SEM_SPEC, SEM_SPEC, ANY_SPEC], out_specs=tuple([HBM_SPEC] * (2 * n)),
              input_output_aliases={i: i for i in range(2 * n)},
              compiler_params=pltpu.CompilerParams(has_side_effects=SPLIT_EFFECT))(*ps, *lands, s_sem, r_sem, after)
    return out[:n], out[n:]


def _sum_direct(g, landed, me_arr, c_arr, buf, first_layer, name):
    _, nl, r2, cols = landed.shape

    def body(me_ref, c_ref, g_ref, l_ref, b_ref, o_ref):
        tot = g_ref[...].astype(F32)
        for s in range(7):
            tot = tot + l_ref[s].astype(F32)
        o_ref[...] = tot

    gs = pltpu.PrefetchScalarGridSpec(
        num_scalar_prefetch=2, grid=(nl,),
        in_specs=[pl.BlockSpec((None, None, r2, cols), lambda l, me, c: (me[0], l, c[0], 0)),
                  pl.BlockSpec((7, None, r2, cols), lambda l, me, c: (0, l, 0, 0)), ANY_SPEC],
        out_specs=pl.BlockSpec((None, r2, cols), lambda l, me, c: (first_layer + l, c[0], 0)))
    return _pc(body, name=name, grid_spec=gs, out_shape=S(buf.shape, F32), input_output_aliases={4: 0},
               compiler_params=_cp(1))(me_arr, c_arr, g, landed, buf)


def _share_halves(bufs):
    n = len(bufs)

    def body(*refs):
        ins, outs = refs[:n], refs[n:2 * n]
        s_sem, r_sem = refs[2 * n:]
        x, y, c, _ = _place()
        cps = []
        for t in range(n):
            r2 = bufs[t].shape[1] // 2
            cps.append(pltpu.make_async_remote_copy(
                src_ref=ins[t].at[:, pl.ds(c * r2, r2), :], dst_ref=outs[t].at[:, pl.ds(c * r2, r2), :],
                send_sem=s_sem.at[t], recv_sem=r_sem.at[t], device_id=(x, y, 1 - c), device_id_type=MESH))
            cps[-1].start()
        for cp in cps:
            cp.wait_recv()
        for cp in cps:
            cp.wait_send()

    return _pc(body, name="grad_share_halves", in_specs=[HBM_SPEC] * n, out_specs=[HBM_SPEC] * n,
               out_shape=[S(b.shape, b.dtype) for b in bufs], input_output_aliases={t: t for t in range(n)},
               scratch_shapes=[pltpu.SemaphoreType.DMA((n,))] * 2, compiler_params=_comm_params())(*bufs)


def _gather_small(v, after):
    nr = v.shape[0]

    def body(v_ref, after_ref, out_ref, send_sems, recv_sems, local_sem):
        x, y, c, chips = _place()
        me, sibling = (x, y, c), (x, y, 1 - c)

        def rows(px, py, pc):
            return out_ref.at[pl.ds((4 * px + 2 * py + pc) * nr, nr), :]

        def copy(k, block, to, src=None):
            return pltpu.make_async_remote_copy(src_ref=rows(*block) if src is None else src, dst_ref=rows(*block),
                                                send_sem=send_sems.at[k], recv_sem=recv_sems.at[k], device_id=to,
                                                device_id_type=MESH)

        mine = pltpu.make_async_copy(v_ref, rows(*me), local_sem)
        mine.start()
        first = [copy(0, me, sibling, src=v_ref)]
        first += [copy(1 + j, me, (*chip, c), src=v_ref) for j, chip in enumerate(chips)]
        for cp in first:
            cp.start()
        passed = [copy(4 + j, (*chip, c), sibling) for j, chip in enumerate(chips)]
        for j, chip in enumerate(chips):
            copy(1 + j, (*chip, c), me).wait_recv()
            passed[j].start()
        copy(0, sibling, me).wait_recv()
        for j, chip in enumerate(chips):
            copy(4 + j, (*chip, 1 - c), me).wait_recv()
        for cp in first + passed:
            cp.wait_send()
        mine.wait()

    vm = pl.BlockSpec(memory_space=pltpu.VMEM)
    return _pc(body, name="gather_small_grads", in_specs=[vm, ANY_SPEC], out_specs=vm, out_shape=S((8 * nr, 128), F32),
               scratch_shapes=[pltpu.SemaphoreType.DMA((7,)), pltpu.SemaphoreType.DMA((7,)), pltpu.SemaphoreType.DMA],
               compiler_params=_comm_params())(v, after)


def _adamw_math(w, g, m, v):
    m2 = ADAM_B1 * m + (1.0 - ADAM_B1) * g
    v2 = ADAM_B2 * v + (1.0 - ADAM_B2) * (g * g)
    m_hat = m2 / (1.0 - ADAM_B1 ** ADAM_STEP)
    v_hat = v2 / (1.0 - ADAM_B2 ** ADAM_STEP)
    return -ADAM_LR * (m_hat / (jnp.sqrt(v_hat) + ADAM_EPS) + ADAM_WD * w), m2, v2


def _adamw(w, g, m, v, name):
    L, R, C = w.shape
    rb = R // 2 if R >= 512 else R

    def body(w_ref, g_ref, m_ref, v_ref, d_ref, m2_ref, v2_ref):
        d_ref[...], m2_ref[...], v2_ref[...] = _adamw_math(w_ref[...], g_ref[...], m_ref[...], v_ref[...])

    blk = pl.BlockSpec((None, rb, C), lambda l, r: (l, r, 0))
    return _pc(body, name=name, grid=(L, R // rb), in_specs=[blk] * 4, out_specs=[blk] * 3,
               out_shape=[S(w.shape, F32)] * 3, compiler_params=_cp(2))(w, g, m, v)


def _adamw_small(gathered, w, m, v):
    nr = w.shape[0]
    rb = nr // 5

    def body(a_ref, w_ref, m_ref, v_ref, g_ref, d_ref, m2_ref, v2_ref):
        g = a_ref[0]
        for k in range(1, 8):
            g = g + a_ref[k]
        g_ref[...] = g
        d_ref[...], m2_ref[...], v2_ref[...] = _adamw_math(w_ref[...], g, m_ref[...], v_ref[...])

    blk = pl.BlockSpec((rb, 128), lambda i: (i, 0))
    return _pc(body, name="adamw_small", grid=(nr // rb,), in_specs=[pl.BlockSpec((8, rb, 128), lambda i: (0, i, 0))] + [blk] * 3,
               out_specs=[blk] * 4, out_shape=[S((nr, 128), F32)] * 4, compiler_params=_cp(1))(gathered, w, m, v)


SMALL_ROWS = 4520


def _pack(arrs):
    flat = jnp.concatenate([a.reshape(-1) for a in arrs])
    return jnp.pad(flat, (0, SMALL_ROWS * 128 - flat.shape[0])).reshape(SMALL_ROWS, 128)


def _unpack(packed, like):
    flat = packed.reshape(-1)
    out, off = [], 0
    for a in like:
        out.append(flat[off:off + a.size].reshape(a.shape))
        off += a.size
    return out


def kernel(x, p, positions, ffn1_pre_g, ffn1_w_gate, ffn1_w_up, ffn1_w_down, ffn1_post_g, mix_pre_g, w_in, attn_norm_g, ssm_lam_re, ssm_lam_im, ssm_log_dt, ssm_b_re, ssm_b_im, ssm_c_re, ssm_c_im, ssm_d, ssm_w_glu, ssm_b_glu, ssm_norm_g, w_out, mix_post_g, ffn2_pre_g, ffn2_w_gate, ffn2_w_up, ffn2_w_down, ffn2_post_g, ple_w_up, ple_w_gate, ple_post_g, loss_target, m_ffn1_pre_g, m_ffn1_w_gate, m_ffn1_w_up, m_ffn1_w_down, m_ffn1_post_g, m_mix_pre_g, m_w_in, m_attn_norm_g, m_ssm_lam_re, m_ssm_lam_im, m_ssm_log_dt, m_ssm_b_re, m_ssm_b_im, m_ssm_c_re, m_ssm_c_im, m_ssm_d, m_ssm_w_glu, m_ssm_b_glu, m_ssm_norm_g, m_w_out, m_mix_post_g, m_ffn2_pre_g, m_ffn2_w_gate, m_ffn2_w_up, m_ffn2_w_down, m_ffn2_post_g, m_ple_w_up, m_ple_w_gate, m_ple_post_g, v_ffn1_pre_g, v_ffn1_w_gate, v_ffn1_w_up, v_ffn1_w_down, v_ffn1_post_g, v_mix_pre_g, v_w_in, v_attn_norm_g, v_ssm_lam_re, v_ssm_lam_im, v_ssm_log_dt, v_ssm_b_re, v_ssm_b_im, v_ssm_c_re, v_ssm_c_im, v_ssm_d, v_ssm_w_glu, v_ssm_b_glu, v_ssm_norm_g, v_w_out, v_mix_post_g, v_ffn2_pre_g, v_ffn2_w_gate, v_ffn2_w_up, v_ffn2_w_down, v_ffn2_post_g, v_ple_w_up, v_ple_w_gate, v_ple_post_g):
    a = dict(locals())
    T = x.shape[1]
    big_names = [n for n, _, _ in BIG]
    for n in TRANSPOSED:
        for pre in ("", "m_", "v_"):
            a[pre + n] = jnp.swapaxes(a[pre + n], 1, 2)

    own = [a[n].astype(BF16) for n in big_names]
    n_layers = own[0].shape[0]
    per_layer = [[w[l:l + 1] for w in own] for l in range(n_layers)]
    c_arr = lax.axis_index("c").astype(jnp.int32).reshape(1)
    me_arr = (2 * lax.axis_index("x") + lax.axis_index("y")).astype(jnp.int32).reshape(1)
    first = dict(zip(big_names, _gather_weights(per_layer[0], _place_own(per_layer[0], me_arr, 0))))
    pending, anchor, queued_behind = {}, jnp.zeros((), F32), first[big_names[0]]
    for l in range(1, n_layers):
        s_sem, r_sem, ws_thru, lands_thru, token = _gather_start(per_layer[l], _place_own(per_layer[l], me_arr, l),
                                                                 queued_behind, l)
        pending[l] = (s_sem, r_sem, ws_thru, lands_thru)
        anchor = anchor + token[0, 0]
        queued_behind = token

    def weights_of(l, after):
        if l == 0:
            return first
        return dict(zip(big_names, _gather_wait(*pending[l], after, l)))

    Sm = {n: a[n] for n in SMALL}
    Sm["ffn1_pre_g"] = Sm["ffn1_pre_g"] + anchor

    pos = jnp.broadcast_to(positions.reshape(1, T, 1).astype(F32), (1, T, 128))
    sent = {}

    def layer_grads_done(l, G):
        gs = [G[n] for n in big_names]
        lands = [lax.empty((7, 1, g.shape[2] // 2, g.shape[3]), BF16) for g in gs]
        s_sem, r_sem, gs_thru, lands_thru, token = _send_start(gs, lands, l)
        sent[l] = (s_sem, r_sem, gs_thru, lands_thru)
        return token[0, 0]

    loss, gx, G_first, small = _local_step(
        _reorder(x, True, "to_streams_x")[0], _reorder(p[:, 0], True, "to_streams_p"),
        _reorder(pos, True, "to_streams_pos")[0, :, :1], _reorder(loss_target, True, "to_streams_target")[0],
        weights_of, layer_grads_done, Sm)
    gx = _reorder(gx[None], False, "to_time_grad_x")

    gs0 = [G_first[n] for n in big_names]
    parts = [_add_half(g, la, c_arr, f"grad_add_half_{n}") for g, la, n in zip(gs0, _swap_halves(gs0, "first"), big_names)]
    first_sent = _partial_send_start(parts, [lax.empty((3,) + pt.shape[1:], BF16) for pt in parts])

    bufs = [lax.empty((n_layers, r, c), F32) for _, r, c in BIG]
    for l in sorted(sent, reverse=True):
        gs, landed = _send_wait(*sent[l], first_sent[-1], l)
        bufs = [_sum_direct(g, la, me_arr, c_arr, b, l, f"grad_sum_direct_l{l}_{n}")
                for g, la, b, n in zip(gs, landed, bufs, big_names)]
    small_g = _gather_small(_pack([small[n] for n in SMALL]), bufs[0]).reshape(8, SMALL_ROWS, 128)
    sg, sd, sm, sv = _adamw_small(small_g, _pack([a[n] for n in SMALL]), _pack([a["m_" + n] for n in SMALL]),
                                  _pack([a["v_" + n] for n in SMALL]))

    parts, landed = _partial_send_wait(*first_sent[:-1], sd)
    bufs = [_sum_shards(pt, la, me_arr, c_arr, b, 0, f"grad_sum_shards_first_{n}")
            for pt, la, b, n in zip(parts, landed, bufs, big_names)]
    grads = dict(zip(big_names, _share_halves(bufs)))
    like = [a[n] for n in SMALL]
    res = {}
    for n, g_, d_, m_, v_ in zip(SMALL, _unpack(sg, like), _unpack(sd, like), _unpack(sm, like), _unpack(sv, like)):
        res[n] = (g_, d_, m_, v_)
    for n in big_names:
        d_, m_, v_ = _adamw(a[n], grads[n], a["m_" + n], a["v_" + n], f"adamw_{n}")
        res[n] = (grads[n], d_, m_, v_)
        if n in TRANSPOSED:
            res[n] = tuple(jnp.swapaxes(t, 1, 2) for t in res[n])

    total = lax.psum(loss[0, 0], ("x", "y", "c"))
    return (total, gx, *[res[n][0] for n in WEIGHTS], *[res[n][1] for n in WEIGHTS],
            *[res[n][2] for n in WEIGHTS], *[res[n][3] for n in WEIGHTS])
```

```python
import functools
import math

import numpy as np
import jax
import jax.numpy as jnp
from jax import lax
from jax.experimental import pallas as pl
from jax.experimental.pallas import tpu as pltpu

F32 = jnp.float32
BF16 = jnp.bfloat16
S = jax.ShapeDtypeStruct
MESH = pl.DeviceIdType.MESH

D = 1024
DA = 512
DSS = 512
HD = 64
NH = 8
BAND = 128
NSH = 4
DFS = 704
PLE = 256
EPS = 1e-6
ROPE_THETA = 500000.0
PATTERN_DILATIONS = (1, 4, 16)
NLB = 16
ADAM_LR, ADAM_B1, ADAM_B2, ADAM_EPS, ADAM_WD, ADAM_STEP = 0.001, 0.9, 0.999, 1e-08, 0.01, 10

VMEM_LIMIT = 56 * 1024 * 1024
TM = 512
TMB = 256

BIG = (
    ("ffn1_w_gate", DFS, D), ("ffn1_w_up", DFS, D), ("ffn1_w_down", DFS, D),
    ("w_in", D, 512), ("ssm_w_glu", 128, 512), ("w_out", 256, D),
    ("ffn2_w_gate", DFS, D), ("ffn2_w_up", DFS, D), ("ffn2_w_down", DFS, D),
    ("ple_w_up", PLE, 256), ("ple_w_gate", 256, D),
)
TRANSPOSED = ("ffn1_w_gate", "ffn1_w_up", "ffn2_w_gate", "ffn2_w_up")
SMALL = ("ffn1_pre_g", "ffn1_post_g", "mix_pre_g", "attn_norm_g", "ssm_lam_re", "ssm_lam_im", "ssm_log_dt",
         "ssm_b_re", "ssm_b_im", "ssm_c_re", "ssm_c_im", "ssm_d", "ssm_b_glu", "ssm_norm_g", "mix_post_g",
         "ffn2_pre_g", "ffn2_post_g", "ple_post_g")
WEIGHTS = ("ffn1_pre_g", "ffn1_w_gate", "ffn1_w_up", "ffn1_w_down", "ffn1_post_g", "mix_pre_g", "w_in", "attn_norm_g",
           "ssm_lam_re", "ssm_lam_im", "ssm_log_dt", "ssm_b_re", "ssm_b_im", "ssm_c_re", "ssm_c_im", "ssm_d",
           "ssm_w_glu", "ssm_b_glu", "ssm_norm_g", "w_out", "mix_post_g", "ffn2_pre_g", "ffn2_w_gate", "ffn2_w_up",
           "ffn2_w_down", "ffn2_post_g", "ple_w_up", "ple_w_gate", "ple_post_g")


def _pc(body, **kw):
    return pl.pallas_call(body, **kw)


def _cp(n_grid):
    return pltpu.CompilerParams(dimension_semantics=("arbitrary",) * n_grid, vmem_limit_bytes=VMEM_LIMIT)


def _dot(a, b):
    return jnp.dot(a, b, preferred_element_type=F32)


def _dot_nt(a, b):
    return lax.dot_general(a, b, (((1,), (1,)), ((), ())), preferred_element_type=F32)


def _dot_tn(a, b):
    return lax.dot_general(a, b, (((0,), (0,)), ((), ())), preferred_element_type=F32)


def _split(a):
    hi = a.astype(BF16)
    return hi, (a - hi.astype(F32)).astype(BF16)


def _dot2(fn, a, b, exact):
    if exact == "a":
        ah, al = _split(a)
        b16 = b.astype(BF16)
        return fn(ah, b16) + fn(al, b16)
    bh, bl = _split(b)
    a16 = a.astype(BF16)
    return fn(a16, bh) + fn(a16, bl)


def _rms_fwd(x, g):
    r = lax.rsqrt(jnp.mean(x * x, axis=-1, keepdims=True) + EPS)
    return x * r * g


def _rms_bwd(dy, x, g):
    r = lax.rsqrt(jnp.mean(x * x, axis=-1, keepdims=True) + EPS)
    xr = x * r
    gd = dy * g
    dx = r * (gd - xr * jnp.mean(gd * xr, axis=-1, keepdims=True))
    dg = jnp.sum(dy * xr, axis=0, keepdims=True)
    return dx, dg


def _gelu(y):
    k = math.sqrt(2.0 / math.pi)
    return 0.5 * y * (1.0 + jnp.tanh(k * (y + 0.044715 * y * y * y)))


def _gelu_grad(y):
    k = math.sqrt(2.0 / math.pi)
    t = jnp.tanh(k * (y + 0.044715 * y * y * y))
    return 0.5 * (1.0 + t) + 0.5 * y * (1.0 - t * t) * k * (1.0 + 3 * 0.044715 * y * y)


def _gain_spec(n, layer):
    return pl.BlockSpec((None, 1, n), lambda *_: (layer, 0, 0))


def _row_acc_spec(n):
    return pl.BlockSpec((1, n), lambda *_: (0, 0))


def _rot_tables(pos_col):
    T = pos_col.shape[0]
    half = HD // 8
    inv = (ROPE_THETA ** (-np.arange(half, dtype=np.float32) * (2.0 / (2 * half)))).astype(np.float32)
    lane_freq = np.tile(np.concatenate([inv, inv, np.zeros(HD - 2 * half, np.float32)]), NH)[None, :]

    def body(p_ref, f_ref, c_ref, s1_ref, s2_ref):
        ang = p_ref[...] * f_ref[...]
        d = lax.broadcasted_iota(jnp.int32, ang.shape, 1) % HD
        cs = jnp.cos(ang)
        sn = jnp.sin(ang)
        c_ref[...] = jnp.where(d < 2 * half, cs, 1.0)
        s1_ref[...] = jnp.where(d < half, -sn, 0.0)
        s2_ref[...] = jnp.where((d >= half) & (d < 2 * half), sn, 0.0)

    tm = TM
    return _pc(body, name="rot_tables", grid=(T // tm,),
               in_specs=[pl.BlockSpec((tm, 1), lambda i: (i, 0)), pl.BlockSpec((1, DA), lambda i: (0, 0))],
               out_specs=[pl.BlockSpec((tm, DA), lambda i: (i, 0))] * 3,
               out_shape=[S((T, DA), F32)] * 3, compiler_params=_cp(1))(pos_col, jnp.asarray(lane_freq))


def _rot_fwd(t, c, s1, s2):
    return t * c + pltpu.roll(t, DA - 8, 1) * s1 + pltpu.roll(t, 8, 1) * s2


def _rot_bwd(g, c, s1, s2):
    return g * c + pltpu.roll(g * s1, 8, 1) + pltpu.roll(g * s2, DA - 8, 1)


def _ffn_weight_spec():
    return pl.BlockSpec((NSH, None, DFS, D), lambda i: (0, 0, 0, 0), pipeline_mode=pl.Buffered(1))


def _ffn_fwd(h, pre_g, post_g, wg, wu, wd, layer, tag):
    T = h.shape[0]
    tm = TM
    nt = T // tm

    def body(h_ref, pg_ref, qg_ref, wg_ref, wu_ref, wd_ref, ho_ref, a_ref, b_ref, f_ref, xn_ref):
        hv = h_ref[...]
        xb = _rms_fwd(hv, pg_ref[...]).astype(BF16)
        xn_ref[...] = xb
        f = None
        for j in range(NSH):
            ab = _dot_nt(xb, wg_ref[j]).astype(BF16)
            bb = _dot_nt(xb, wu_ref[j]).astype(BF16)
            a_ref[j] = ab
            b_ref[j] = bb
            a = ab.astype(F32)
            hh = (a * jax.nn.sigmoid(a) * bb.astype(F32)).astype(BF16)
            part = _dot(hh, wd_ref[j])
            f = part if f is None else f + part
        f_ref[...] = f
        ho_ref[...] = hv + 0.5 * _rms_fwd(f, qg_ref[...])

    row = pl.BlockSpec((tm, D), lambda i: (i, 0))
    act = pl.BlockSpec((NSH, tm, DFS), lambda i: (0, i, 0))
    return _pc(body, name=f"ffn_fwd_{tag}_l{layer}", grid=(nt,),
               in_specs=[row, _gain_spec(D, layer), _gain_spec(D, layer)] + [_ffn_weight_spec()] * 3,
               out_specs=[row, act, act, row, row],
               out_shape=[S((T, D), F32), S((NSH, T, DFS), BF16), S((NSH, T, DFS), BF16), S((T, D), F32), S((T, D), BF16)],
               compiler_params=_cp(1))(h, pre_g, post_g, wg, wu, wd)


def _ffn_bwd(dout, h, f, a, b, pre_g, post_g, wg, wu, wd, layer, tag):
    T = h.shape[0]
    tm = TMB
    nt = T // tm

    def body(do_ref, h_ref, f_ref, a_ref, b_ref, pg_ref, qg_ref, wg_ref, wu_ref, wd_ref,
             dh_ref, df_ref, da_ref, db_ref, hh_ref, dpg_ref, dqg_ref):
        @pl.when(pl.program_id(0) == 0)
        def _():
            dpg_ref[...] = jnp.zeros_like(dpg_ref)
            dqg_ref[...] = jnp.zeros_like(dqg_ref)

        do = do_ref[...]
        df, dq = _rms_bwd(0.5 * do, f_ref[...], qg_ref[...])
        dqg_ref[...] += dq
        dfb = df.astype(BF16)
        df_ref[...] = dfb
        dxn = None
        for j in range(NSH):
            dhh = _dot_nt(dfb, wd_ref[j])
            av = a_ref[j].astype(F32)
            bv = b_ref[j].astype(F32)
            sg = jax.nn.sigmoid(av)
            sa = av * sg
            hh_ref[j] = (sa * bv).astype(BF16)
            dab = (dhh * bv * (sg + sa * (1.0 - sg))).astype(BF16)
            dbb = (dhh * sa).astype(BF16)
            da_ref[j] = dab
            db_ref[j] = dbb
            part = _dot(dab, wg_ref[j]) + _dot(dbb, wu_ref[j])
            dxn = part if dxn is None else dxn + part
        dx, dp = _rms_bwd(dxn, h_ref[...], pg_ref[...])
        dpg_ref[...] += dp
        dh_ref[...] = do + dx

    row = pl.BlockSpec((tm, D), lambda i: (i, 0))
    act = pl.BlockSpec((NSH, tm, DFS), lambda i: (0, i, 0))
    return _pc(body, name=f"ffn_bwd_{tag}_l{layer}", grid=(nt,),
               in_specs=[row, row, row, act, act, _gain_spec(D, layer), _gain_spec(D, layer)] + [_ffn_weight_spec()] * 3,
               out_specs=[row, row, act, act, act, _row_acc_spec(D), _row_acc_spec(D)],
               out_shape=[S((T, D), F32), S((T, D), BF16), S((NSH, T, DFS), BF16), S((NSH, T, DFS), BF16),
                          S((NSH, T, DFS), BF16), S((1, D), F32), S((1, D), F32)],
               compiler_params=_cp(1))(dout, h, f, a, b, pre_g, post_g, wg, wu, wd)


def _dw(A, B, buf, layer, kb, nb, a_mode, b_mode, name):
    T = A.shape[1]
    tt = 2 * TM if T % (2 * TM) == 0 else TM
    nt = T // tt

    def pick(v, mode, j, w):
        if mode == "shard":
            return v[j]
        return v[0] if mode == "whole" else v[0][:, j * w:(j + 1) * w]

    def body(a_ref, b_ref, buf_ref, o_ref, acc):
        t = pl.program_id(0)

        @pl.when(t == 0)
        def _():
            acc[...] = jnp.zeros_like(acc)

        av = a_ref[...].astype(BF16)
        bv = b_ref[...].astype(BF16)
        for j in range(NSH):
            acc[j] += _dot_tn(pick(av, a_mode, j, kb), pick(bv, b_mode, j, nb))

        @pl.when(t == nt - 1)
        def _():
            o_ref[...] = acc[...].astype(o_ref.dtype)

    return _pc(body, name=name, grid=(nt,),
               in_specs=[pl.BlockSpec((A.shape[0], tt, A.shape[2]), lambda t: (0, t, 0)),
                         pl.BlockSpec((B.shape[0], tt, B.shape[2]), lambda t: (0, t, 0)),
                         pl.BlockSpec(memory_space=pl.ANY)],
               out_specs=pl.BlockSpec((NSH, None, kb, nb), lambda t: (0, layer, 0, 0)),
               out_shape=S(buf.shape, buf.dtype), input_output_aliases={2: 0},
               scratch_shapes=[pltpu.VMEM((NSH, kb, nb), F32)], compiler_params=_cp(1))(A, B, buf)


def _mix_proj(h, pre_g, win, rot, layer):
    T = h.shape[0]
    tm = TM

    def body(h_ref, g_ref, w_ref, c_ref, s1_ref, s2_ref, p_ref, xn_ref):
        xb = _rms_fwd(h_ref[...], g_ref[...]).astype(BF16)
        xn_ref[...] = xb
        for j in range(NSH):
            o = _dot(xb, w_ref[j])
            p_ref[j] = _rot_fwd(o, c_ref[...], s1_ref[...], s2_ref[...]) if j < 2 else o

    row = pl.BlockSpec((tm, D), lambda i: (i, 0))
    half = pl.BlockSpec((tm, DA), lambda i: (i, 0))
    return _pc(body, name=f"mix_proj_l{layer}", grid=(T // tm,),
               in_specs=[row, _gain_spec(D, layer), pl.BlockSpec((NSH, None, D, DA), lambda i: (0, 0, 0, 0)),
                         half, half, half],
               out_specs=[pl.BlockSpec((NSH, tm, DA), lambda i: (0, i, 0)), row],
               out_shape=[S((NSH, T, DA), F32), S((T, D), BF16)], compiler_params=_cp(1))(h, pre_g, win, *rot)


def _mix_proj_bwd(dq, dk, dv, du, dh_up, h, pre_g, win, rot, layer):
    T = h.shape[0]
    tm = TM

    def body(dq_ref, dk_ref, dv_ref, du_ref, up_ref, h_ref, g_ref, w_ref, c_ref, s1_ref, s2_ref,
             dh_ref, dp_ref, dg_ref):
        @pl.when(pl.program_id(0) == 0)
        def _():
            dg_ref[...] = jnp.zeros_like(dg_ref)

        rot = (c_ref[...], s1_ref[...], s2_ref[...])
        dps = [_rot_bwd(dq_ref[...], *rot), _rot_bwd(dk_ref[...], *rot), dv_ref[...], du_ref[...]]
        dxn = None
        for j in range(NSH):
            dpb = dps[j].astype(BF16)
            dp_ref[j] = dpb
            part = _dot_nt(dpb, w_ref[j])
            dxn = part if dxn is None else dxn + part
        dx, dg = _rms_bwd(dxn, h_ref[...], g_ref[...])
        dg_ref[...] += dg
        dh_ref[...] = up_ref[...] + dx

    row = pl.BlockSpec((tm, D), lambda i: (i, 0))
    half = pl.BlockSpec((tm, DA), lambda i: (i, 0))
    return _pc(body, name=f"mix_proj_bwd_l{layer}", grid=(T // tm,),
               in_specs=[half, half, half, half, row, row, _gain_spec(D, layer),
                         pl.BlockSpec((NSH, None, D, DA), lambda i: (0, 0, 0, 0)), half, half, half],
               out_specs=[row, pl.BlockSpec((NSH, tm, DA), lambda i: (0, i, 0)), _row_acc_spec(D)],
               out_shape=[S((T, D), F32), S((NSH, T, DA), BF16), S((1, D), F32)],
               compiler_params=_cp(1))(dq, dk, dv, du, dh_up, h, pre_g, win, *rot)


def _stream_pos(d, axis):
    i = lax.broadcasted_iota(jnp.int32, (BAND, BAND), axis)
    if d == 16:
        return i
    if d == 4:
        return 4 * (i % 32) + i // 32
    return 16 * (i % 8) + i // 8


def _band_masks(b, d):
    qi, kj = _stream_pos(d, 0), _stream_pos(d, 1)
    return kj <= qi, (kj >= qi) & (b > 0)


def _pattern(d, T):
    n16 = T // 16
    if d == 16:
        return (16, n16, DA), (None, BAND, DA), lambda r, k: (r, k, 0)
    if d == 4:
        return (4, 4, n16, DA), (4, None, 32, DA), lambda r, k: (0, r, k, 0)
    return (16, n16, DA), (16, 8, DA), lambda r, k: (0, k, 0)


def _pattern_spec(d, T, kmap, lead=None):
    _, blk, idx = _pattern(d, T)
    if lead is None:
        return pl.BlockSpec(blk, lambda r, b: idx(r, kmap(b)))
    return pl.BlockSpec((None,) + blk, lambda r, b: (lead,) + idx(r, kmap(b)))


def _whole_stream_specs(T, n_plain):
    n16 = T // 16
    p_spec = lambda s: pl.BlockSpec((None, None, n16, DA), lambda r: (s, r, 0, 0))
    plain = pl.BlockSpec((None, n16, DA), lambda r: (r, 0, 0))
    return [p_spec(0), p_spec(1), p_spec(2)] + [plain] * n_plain, plain


def _stream_masks():
    qi = lax.broadcasted_iota(jnp.int32, (BAND, BAND), 0)
    kj = lax.broadcasted_iota(jnp.int32, (BAND, BAND), 1)
    mask_c = kj <= qi
    return mask_c, jnp.concatenate([kj >= qi, mask_c], axis=1)


def _attn_fwd_stream(P, layer):
    T = P.shape[1]
    n16 = T // 16
    nb = n16 // BAND
    scale = HD ** -0.5

    def body(q_ref, k_ref, v_ref, o_ref, l_ref, qs, ks, vs):
        for src, dst in ((q_ref, qs), (k_ref, ks), (v_ref, vs)):
            dst[...] = src[...].astype(BF16)
        mask_c, mask_pc = _stream_masks()
        for b in range(nb):
            rows = slice(b * BAND, (b + 1) * BAND)
            krows = slice(max(b - 1, 0) * BAND, (b + 1) * BAND)
            mask = mask_c if b == 0 else mask_pc
            for hd in range(NH):
                sl = slice(hd * HD, (hd + 1) * HD)
                s = jnp.where(mask, _dot_nt(qs[rows, sl], ks[krows, sl]) * scale, -1e30)
                m = jnp.max(s, axis=-1, keepdims=True)
                e = jnp.exp(s - m)
                den = jnp.sum(e, axis=-1, keepdims=True)
                o_ref[rows, sl] = _dot(e.astype(BF16), vs[krows, sl]) / den
                l_ref[rows, sl] = jnp.broadcast_to(m + jnp.log(den), (BAND, HD))

    ins, out = _whole_stream_specs(T, 0)
    Pv = P.reshape(NSH, 16, n16, DA)
    o, l = _pc(body, name=f"attn_fwd_d16_l{layer}", grid=(16,), in_specs=ins, out_specs=[out, out],
               out_shape=[S((16, n16, DA), F32)] * 2, scratch_shapes=[pltpu.VMEM((n16, DA), BF16)] * 3,
               compiler_params=_cp(1))(Pv, Pv, Pv)
    return o.reshape(T, DA), l.reshape(T, DA)


def _attn_bwd_stream(P, dO, lse, delta, acc, layer):
    T = P.shape[1]
    n16 = T // 16
    nb = n16 // BAND
    scale = HD ** -0.5
    first = acc is None

    def body(*refs):
        q_ref, k_ref, v_ref, do_ref, l_ref, dl_ref = refs[:6]
        if first:
            dq_ref, dk_ref, dv_ref = refs[6:9]
        else:
            aq_ref, ak_ref, av_ref, dq_ref, dk_ref, dv_ref = refs[6:12]
        qs, ks, vs, dos, okf, ovf = refs[-6:]
        for src, dst in ((q_ref, qs), (k_ref, ks), (v_ref, vs), (do_ref, dos)):
            dst[...] = src[...].astype(BF16)
        okf[...] = jnp.zeros_like(okf)
        ovf[...] = jnp.zeros_like(ovf)
        mask_c, mask_pc = _stream_masks()
        for b in range(nb):
            rows = slice(b * BAND, (b + 1) * BAND)
            krows = slice(max(b - 1, 0) * BAND, (b + 1) * BAND)
            mask = mask_c if b == 0 else mask_pc
            for hd in range(NH):
                sl = slice(hd * HD, (hd + 1) * HD)
                one = slice(hd * HD, hd * HD + 1)
                q, do, kk = qs[rows, sl], dos[rows, sl], ks[krows, sl]
                p = jnp.where(mask, jnp.exp(_dot_nt(q, kk) * scale - l_ref[rows, one]), 0.0)
                ds = (p * (_dot_nt(do, vs[krows, sl]) - dl_ref[rows, one]) * scale).astype(BF16)
                dq = _dot(ds, kk)
                dq_ref[rows, sl] = dq if first else aq_ref[rows, sl] + dq
                okf[krows, sl] += _dot_tn(ds, q)
                ovf[krows, sl] += _dot_tn(p.astype(BF16), do)
        dk_ref[...] = okf[...] if first else ak_ref[...] + okf[...]
        dv_ref[...] = ovf[...] if first else av_ref[...] + ovf[...]

    ins, out = _whole_stream_specs(T, 3 if first else 6)
    Pv = P.reshape(NSH, 16, n16, DA)
    view = lambda t: t.reshape(16, n16, DA)
    args = [Pv, Pv, Pv, view(dO), view(lse), view(delta)] + ([] if first else [view(t) for t in acc])
    dq, dk, dv = _pc(body, name=f"attn_bwd_d16_l{layer}", grid=(16,), in_specs=ins, out_specs=[out, out, out],
                     out_shape=[S((16, n16, DA), F32)] * 3,
                     scratch_shapes=[pltpu.VMEM((n16, DA), BF16)] * 4 + [pltpu.VMEM((n16, DA), F32)] * 2,
                     compiler_params=_cp(1))(*args)
    return dq.reshape(T, DA), dk.reshape(T, DA), dv.reshape(T, DA)


def _attn_fwd(P, d, layer):
    if d == 16:
        return _attn_fwd_stream(P, layer)
    T = P.shape[1]
    nb = T // d // BAND
    vshape = _pattern(d, T)[0]
    Pv = P.reshape((NSH,) + vshape)
    scale = HD ** -0.5

    def body(q_ref, kp_ref, kc_ref, vp_ref, vc_ref, o_ref, l_ref, qs, ks, vs, osc, lsc):
        b = pl.program_id(1)
        flat = lambda ref: ref[...].reshape(BAND, DA).astype(BF16)
        qs[...] = flat(q_ref)
        ks[0:BAND, :] = flat(kp_ref)
        ks[BAND:, :] = flat(kc_ref)
        vs[0:BAND, :] = flat(vp_ref)
        vs[BAND:, :] = flat(vc_ref)
        mask_c, mask_p = _band_masks(b, d)
        mask = jnp.concatenate([mask_p, mask_c], axis=1)
        for hd in range(NH):
            sl = slice(hd * HD, (hd + 1) * HD)
            s = jnp.where(mask, _dot_nt(qs[:, sl], ks[:, sl]) * scale, -1e30)
            m = jnp.max(s, axis=-1, keepdims=True)
            e = jnp.exp(s - m)
            den = jnp.sum(e, axis=-1, keepdims=True)
            osc[:, sl] = _dot(e.astype(BF16), vs[:, sl]) / den
            lsc[:, sl] = jnp.broadcast_to(m + jnp.log(den), (BAND, HD))
        o_ref[...] = osc[...].reshape(o_ref.shape)
        l_ref[...] = lsc[...].reshape(l_ref.shape)

    cur = lambda b: b
    prev = lambda b: jnp.maximum(b - 1, 0)
    out = _pattern_spec(d, T, cur)
    o, l = _pc(body, name=f"attn_fwd_d{d}_l{layer}", grid=(d, nb),
               in_specs=[_pattern_spec(d, T, cur, 0), _pattern_spec(d, T, prev, 1), _pattern_spec(d, T, cur, 1),
                         _pattern_spec(d, T, prev, 2), _pattern_spec(d, T, cur, 2)],
               out_specs=[out, out], out_shape=[S(vshape, F32)] * 2,
               scratch_shapes=[pltpu.VMEM((BAND, DA), BF16)] + [pltpu.VMEM((2 * BAND, DA), BF16)] * 2
               + [pltpu.VMEM((BAND, DA), F32)] * 2,
               compiler_params=_cp(2))(Pv, Pv, Pv, Pv, Pv)
    return o.reshape(T, DA), l.reshape(T, DA)


def _attn_bwd(P, dO, lse, delta, acc, d, layer):
    if d == 16:
        return _attn_bwd_stream(P, dO, lse, delta, acc, layer)
    T = P.shape[1]
    nb = T // d // BAND
    vshape = _pattern(d, T)[0]
    Pv = P.reshape((NSH,) + vshape)
    scale = HD ** -0.5
    first = acc is None

    def body(*refs):
        q_ref, kp_ref, kc_ref, vp_ref, vc_ref, do_ref, l_ref, dl_ref = refs[:8]
        if first:
            dq_ref, dk_ref, dv_ref = refs[8:11]
        else:
            aq_ref, ak_ref, av_ref, dq_ref, dk_ref, dv_ref = refs[8:14]
        qs, dos, ks, vs, ls, dls, oq, ok, ov, ck, cv = refs[-11:]
        b = pl.program_id(1)
        flat = lambda ref: ref[...].reshape(BAND, DA)

        @pl.when(b == 0)
        def _():
            ck[...] = jnp.zeros_like(ck)
            cv[...] = jnp.zeros_like(cv)

        @pl.when(b < nb)
        def _():
            qs[...] = flat(q_ref).astype(BF16)
            dos[...] = flat(do_ref).astype(BF16)
            ks[0:BAND, :] = flat(kp_ref).astype(BF16)
            ks[BAND:, :] = flat(kc_ref).astype(BF16)
            vs[0:BAND, :] = flat(vp_ref).astype(BF16)
            vs[BAND:, :] = flat(vc_ref).astype(BF16)
            ls[...] = flat(l_ref)
            dls[...] = flat(dl_ref)
            mask_c, mask_p = _band_masks(b, d)
            mask = jnp.concatenate([mask_p, mask_c], axis=1)
            for hd in range(NH):
                sl = slice(hd * HD, (hd + 1) * HD)
                one = slice(hd * HD, hd * HD + 1)
                q, do, kk = qs[:, sl], dos[:, sl], ks[:, sl]
                p = jnp.where(mask, jnp.exp(_dot_nt(q, kk) * scale - ls[:, one]), 0.0)
                ds = (p * (_dot_nt(do, vs[:, sl]) - dls[:, one]) * scale).astype(BF16)
                oq[:, sl] = _dot(ds, kk)
                dk2 = _dot_tn(ds, q)
                dv2 = _dot_tn(p.astype(BF16), do)
                ok[:, sl] = ck[:, sl] + dk2[0:BAND]
                ov[:, sl] = cv[:, sl] + dv2[0:BAND]
                ck[:, sl] = dk2[BAND:]
                cv[:, sl] = dv2[BAND:]
            if first:
                dq_ref[...] = oq[...].reshape(dq_ref.shape)
                dk_ref[...] = ok[...].reshape(dk_ref.shape)
                dv_ref[...] = ov[...].reshape(dv_ref.shape)
            else:
                dq_ref[...] = aq_ref[...] + oq[...].reshape(dq_ref.shape)
                dk_ref[...] = ak_ref[...] + ok[...].reshape(dk_ref.shape)
                dv_ref[...] = av_ref[...] + ov[...].reshape(dv_ref.shape)

        @pl.when(b == nb)
        def _():
            if first:
                dk_ref[...] = ck[...].reshape(dk_ref.shape)
                dv_ref[...] = cv[...].reshape(dv_ref.shape)
            else:
                dk_ref[...] = ak_ref[...] + ck[...].reshape(dk_ref.shape)
                dv_ref[...] = av_ref[...] + cv[...].reshape(dv_ref.shape)

    qb = lambda b: jnp.minimum(b, nb - 1)
    qprev = lambda b: jnp.maximum(qb(b) - 1, 0)
    kb = lambda b: jnp.maximum(b - 1, 0)
    qrow = _pattern_spec(d, T, qb)
    krow = _pattern_spec(d, T, kb)
    view = lambda t: t.reshape(vshape)
    ins = [Pv, Pv, Pv, Pv, Pv, view(dO), view(lse), view(delta)]
    specs = [_pattern_spec(d, T, qb, 0), _pattern_spec(d, T, qprev, 1), _pattern_spec(d, T, qb, 1),
             _pattern_spec(d, T, qprev, 2), _pattern_spec(d, T, qb, 2), qrow, qrow, qrow]
    if not first:
        ins += [view(t) for t in acc]
        specs += [qrow, krow, krow]
    dq, dk, dv = _pc(body, name=f"attn_bwd_d{d}_l{layer}", grid=(d, nb + 1), in_specs=specs,
                     out_specs=[qrow, krow, krow], out_shape=[S(vshape, F32)] * 3,
                     scratch_shapes=[pltpu.VMEM((BAND, DA), BF16)] * 2 + [pltpu.VMEM((2 * BAND, DA), BF16)] * 2
                     + [pltpu.VMEM((BAND, DA), F32)] * 7,
                     compiler_params=_cp(2))(*ins)
    return dq.reshape(T, DA), dk.reshape(T, DA), dv.reshape(T, DA)


def _ssm_prep(lam_re, lam_im, log_dt, b_re, b_im, c_re, c_im):
    dt = jnp.exp(log_dt)[:, None]
    er = jnp.exp(lam_re * dt)
    a_re = er * jnp.cos(lam_im * dt)
    a_im = er * jnp.sin(lam_im * dt)
    nr, ni = a_re - 1.0, a_im
    den = lam_re * lam_re + lam_im * lam_im
    cr = (nr * lam_re + ni * lam_im) / den
    ci = (ni * lam_re - nr * lam_im) / den
    bbr = cr[..., None] * b_re - ci[..., None] * b_im
    bbi = cr[..., None] * b_im + ci[..., None] * b_re
    eye = jnp.eye(8, dtype=F32)

    def bblock(bb):
        t = bb.reshape(4, 8, 64, 16).transpose(0, 1, 3, 2)
        return (t[:, :, :, None, :] * eye[None, :, None, :, None]).reshape(4, 128, 512)

    def cblock(cc):
        t = cc.reshape(4, 8, 16, 64).transpose(0, 1, 3, 2)
        return (t[:, :, :, None, :] * eye[None, :, None, :, None]).reshape(4, 512, 128)

    return (a_re.reshape(NLB, 1, 128), a_im.reshape(NLB, 1, 128), bblock(bbr), bblock(bbi), cblock(c_re), cblock(c_im))


def _perm_matrix(tm):
    n = tm // 16
    pm = np.zeros((tm, tm), np.float32)
    for r in range(16):
        pm[16 * np.arange(n) + r, r * n + np.arange(n)] = 1.0
    return jnp.asarray(pm, BF16)


def _pieces(x):
    p1 = x.astype(BF16)
    r1 = x - p1.astype(F32)
    p2 = r1.astype(BF16)
    return p1, p2, (r1 - p2.astype(F32)).astype(BF16)


def _to_time(x, pm, exact=True):
    return sum(_dot(pm, p) for p in (_pieces(x) if exact else _split(x)))


def _to_streams(x, pm, exact=True):
    return sum(_dot_tn(pm, p) for p in (_pieces(x) if exact else _split(x)))


def _stream_block(tm, cols, lead=None):
    if lead is None:
        return pl.BlockSpec((16, tm // 16, cols), lambda i: (0, i, 0))
    return pl.BlockSpec((None, 16, tm // 16, cols), lambda i: (lead, 0, i, 0))


def _reorder(t3, to_streams, name):
    B, T, C = t3.shape
    tm = TM

    def body(x_ref, pm_ref, o_ref):
        if to_streams:
            o_ref[...] = _to_streams(x_ref[...], pm_ref[...]).reshape(o_ref.shape)
        else:
            o_ref[...] = _to_time(x_ref[...].reshape(tm, C), pm_ref[...])

    time_blk = pl.BlockSpec((None, tm, C), lambda b, i: (b, i, 0))
    stream_blk = pl.BlockSpec((None, 16, tm // 16, C), lambda b, i: (b, 0, i, 0))
    src = t3 if to_streams else t3.reshape(B, 16, T // 16, C)
    out = _pc(body, name=name, grid=(B, T // tm),
              in_specs=[time_blk if to_streams else stream_blk, pl.BlockSpec((tm, tm), lambda b, i: (0, 0))],
              out_specs=stream_blk if to_streams else time_blk,
              out_shape=S((B, 16, T // 16, C) if to_streams else (B, T, C), F32),
              compiler_params=_cp(2))(src, _perm_matrix(tm))
    return out.reshape(B, T, C)


def _ssm_in(P, bre, bim, layer):
    T = P.shape[1]
    tm = TM

    def body(u_ref, pm_ref, br_ref, bi_ref, un_ref, or_ref, oi_ref):
        u = _to_time(u_ref[...].reshape(tm, DSS), pm_ref[...], exact=False)
        un_ref[...] = u
        for s in range(4):
            uc = u[:, s * 128:(s + 1) * 128]
            r = _dot2(_dot, uc, br_ref[s], "b")
            m = _dot2(_dot, uc, bi_ref[s], "b")
            for q in range(4):
                or_ref[4 * s + q] = r[:, q * 128:(q + 1) * 128]
                oi_ref[4 * s + q] = m[:, q * 128:(q + 1) * 128]

    whole = pl.BlockSpec((4, 128, 512), lambda i: (0, 0, 0))
    st = pl.BlockSpec((NLB, tm, 128), lambda i: (0, i, 0))
    return _pc(body, name=f"ssm_in_l{layer}", grid=(T // tm,),
               in_specs=[_stream_block(tm, DSS, 3), pl.BlockSpec((tm, tm), lambda i: (0, 0)), whole, whole],
               out_specs=[pl.BlockSpec((tm, DSS), lambda i: (i, 0)), st, st],
               out_shape=[S((T, DSS), F32)] + [S((NLB, T, 128), F32)] * 2,
               compiler_params=_cp(1))(P.reshape(NSH, 16, T // 16, DSS), _perm_matrix(tm), bre, bim)


def _scan(br, bi, a_re, a_im, reverse, layer):
    T = br.shape[1]
    nbk = 4
    tt = min(T, 1024)
    nT = T // tt
    ntile = tt // 8
    sgn = -1.0 if reverse else 1.0
    last = 0 if reverse else 7

    def body(br_ref, bi_ref, ar_ref, ai_ref, xr_ref, xi_ref, cr, ci):
        @pl.when(pl.program_id(1) == 0)
        def _():
            cr[...] = jnp.zeros_like(cr)
            ci[...] = jnp.zeros_like(ci)

        row = lax.broadcasted_iota(jnp.int32, (8, 128), 0)
        consts = []
        for k in range(nbk):
            a1r = jnp.broadcast_to(ar_ref[k], (8, 128))
            a1i = sgn * jnp.broadcast_to(ai_ref[k], (8, 128))
            pows = [(a1r, a1i)]
            for _ in range(7):
                pr, pi_ = pows[-1]
                pows.append((a1r * pr - a1i * pi_, a1r * pi_ + a1i * pr))
            rounds = []
            for s in (1, 2, 4):
                inside = (row <= 7 - s) if reverse else (row >= s)
                rounds.append((jnp.where(inside, pows[s - 1][0], 0.0), jnp.where(inside, pows[s - 1][1], 0.0)))
            cmr, cmi = jnp.zeros((8, 128), F32), jnp.zeros((8, 128), F32)
            for r in range(8):
                e = (7 - r) if reverse else r
                cmr = jnp.where(row == r, pows[e][0], cmr)
                cmi = jnp.where(row == r, pows[e][1], cmi)
            consts.append((rounds, cmr, cmi))

        def tile(i, carry):
            j = (ntile - 1 - i) if reverse else i
            rows = pl.ds(pl.multiple_of(j * 8, 8), 8)
            out = []
            for k in range(nbk):
                rounds, cmr, cmi = consts[k]
                xr = br_ref[k, rows, :]
                xi = bi_ref[k, rows, :]
                for (mr, mi), s in zip(rounds, (1, 2, 4)):
                    sh = (8 - s) if reverse else s
                    rr = pltpu.roll(xr, sh, 0)
                    ri = pltpu.roll(xi, sh, 0)
                    xr, xi = xr + (mr * rr - mi * ri), xi + (mr * ri + mi * rr)
                c_r, c_i = carry[k]
                xr, xi = xr + (cmr * c_r - cmi * c_i), xi + (cmr * c_i + cmi * c_r)
                xr_ref[k, rows, :] = xr
                xi_ref[k, rows, :] = xi
                out.append((jnp.broadcast_to(xr[last:last + 1, :], (8, 128)),
                            jnp.broadcast_to(xi[last:last + 1, :], (8, 128))))
            return tuple(out)

        carry = lax.fori_loop(0, ntile, tile, tuple((cr[k], ci[k]) for k in range(nbk)), unroll=2)
        for k in range(nbk):
            cr[k] = carry[k][0]
            ci[k] = carry[k][1]

    tmap = (lambda t: nT - 1 - t) if reverse else (lambda t: t)
    st = pl.BlockSpec((nbk, tt, 128), lambda i, t: (i, tmap(t), 0))
    av = pl.BlockSpec((nbk, 1, 128), lambda i, t: (i, 0, 0))
    return _pc(body, name=f"scan_{'bwd' if reverse else 'fwd'}_l{layer}", grid=(NLB // nbk, nT),
               in_specs=[st, st, av, av], out_specs=[st, st], out_shape=[S((NLB, T, 128), F32)] * 2,
               scratch_shapes=[pltpu.VMEM((nbk, 8, 128), F32)] * 2, compiler_params=_cp(2))(br, bi, a_re, a_im)


def _ssm_out(xr, xi, u, cre, cim, dvec, wglu, bglu, layer):
    T = u.shape[0]
    tm = TM

    def body(xr_ref, xi_ref, u_ref, pm_ref, cr_ref, ci_ref, d_ref, w_ref, bg_ref, s_ref, y_ref, z_ref):
        ys = []
        for s in range(4):
            xrc = jnp.concatenate([xr_ref[4 * s + q] for q in range(4)], axis=1)
            xic = jnp.concatenate([xi_ref[4 * s + q] for q in range(4)], axis=1)
            ys.append(_dot2(_dot, xrc, cr_ref[s], "b") - _dot2(_dot, xic, ci_ref[s], "b"))
        y = jnp.concatenate(ys, axis=1) + d_ref[...] * u_ref[...]
        yg = _gelu(y)
        ygb = yg.astype(BF16)
        z = bg_ref[...] + sum(_dot(ygb[:, j * 128:(j + 1) * 128], w_ref[j]) for j in range(NSH))
        y_ref[...] = y
        z_ref[...] = z
        s_ref[...] = _to_streams(yg * jax.nn.sigmoid(z), pm_ref[...], exact=False).reshape(s_ref.shape)

    st = pl.BlockSpec((NLB, tm, 128), lambda i: (0, i, 0))
    cw = pl.BlockSpec((4, 512, 128), lambda i: (0, 0, 0))
    half = pl.BlockSpec((tm, DSS), lambda i: (i, 0))
    s, y, z = _pc(body, name=f"ssm_out_l{layer}", grid=(T // tm,),
                  in_specs=[st, st, half, pl.BlockSpec((tm, tm), lambda i: (0, 0)), cw, cw, _gain_spec(DSS, layer),
                            pl.BlockSpec((NSH, None, 128, DSS), lambda i: (0, 0, 0, 0)), _gain_spec(DSS, layer)],
                  out_specs=[_stream_block(tm, DSS), half, half],
                  out_shape=[S((16, T // 16, DSS), F32), S((T, DSS), F32), S((T, DSS), F32)],
                  compiler_params=_cp(1))(xr, xi, u, _perm_matrix(tm), cre, cim, dvec, wglu, bglu)
    return s.reshape(T, DSS), y, z


def _ssm_out_bwd(dssm, y, z, xr, xi, u, cre, cim, dvec, wglu, layer):
    T = u.shape[0]
    tm = TM

    def body(ds_ref, pm_ref, y_ref, z_ref, xr_ref, xi_ref, u_ref, cr_ref, ci_ref, d_ref, w_ref,
             gr_ref, gi_ref, du_ref, dz_ref, yg_ref, dbg_ref, dd_ref, dcr_ref, dci_ref):
        i = pl.program_id(0)

        @pl.when(i == 0)
        def _():
            dbg_ref[...] = jnp.zeros_like(dbg_ref)
            dd_ref[...] = jnp.zeros_like(dd_ref)
            dcr_ref[...] = jnp.zeros_like(dcr_ref)
            dci_ref[...] = jnp.zeros_like(dci_ref)

        yv = y_ref[...]
        yg = _gelu(yv)
        sg = jax.nn.sigmoid(z_ref[...])
        ds = _to_time(ds_ref[...].reshape(tm, DSS), pm_ref[...], exact=False)
        dz = ds * yg * sg * (1.0 - sg)
        dzb = dz.astype(BF16)
        dz_ref[...] = dzb
        yg_ref[...] = yg.astype(BF16)
        dbg_ref[...] += jnp.sum(dz, axis=0, keepdims=True)
        dyg = ds * sg + jnp.concatenate([_dot_nt(dzb, w_ref[j]) for j in range(NSH)], axis=1)
        dy = dyg * _gelu_grad(yv)
        u = u_ref[...]
        dd_ref[...] += jnp.sum(dy * u, axis=0, keepdims=True)
        du_ref[...] = dy * d_ref[...]
        for s in range(4):
            dyc = dy[:, s * 128:(s + 1) * 128]
            g_r = _dot2(_dot_nt, dyc, cr_ref[s], "b")
            g_i = -_dot2(_dot_nt, dyc, ci_ref[s], "b")
            for q in range(4):
                gr_ref[4 * s + q] = g_r[:, q * 128:(q + 1) * 128]
                gi_ref[4 * s + q] = g_i[:, q * 128:(q + 1) * 128]
            xrc = jnp.concatenate([xr_ref[4 * s + q] for q in range(4)], axis=1)
            xic = jnp.concatenate([xi_ref[4 * s + q] for q in range(4)], axis=1)
            dcr_ref[s] += _dot2(_dot_tn, xrc, dyc, "a")
            dci_ref[s] -= _dot2(_dot_tn, xic, dyc, "a")

    st = pl.BlockSpec((NLB, tm, 128), lambda i: (0, i, 0))
    cw = pl.BlockSpec((4, 512, 128), lambda i: (0, 0, 0))
    half = pl.BlockSpec((tm, DSS), lambda i: (i, 0))
    return _pc(body, name=f"ssm_out_bwd_l{layer}", grid=(T // tm,),
               in_specs=[_stream_block(tm, DSS), pl.BlockSpec((tm, tm), lambda i: (0, 0)), half, half, st, st, half,
                         cw, cw, _gain_spec(DSS, layer), pl.BlockSpec((NSH, None, 128, DSS), lambda i: (0, 0, 0, 0))],
               out_specs=[st, st, half, half, half, _row_acc_spec(DSS), _row_acc_spec(DSS), cw, cw],
               out_shape=[S((NLB, T, 128), F32)] * 2 + [S((T, DSS), F32), S((T, DSS), BF16), S((T, DSS), BF16),
                                                        S((1, DSS), F32), S((1, DSS), F32),
                                                        S((4, 512, 128), F32), S((4, 512, 128), F32)],
               compiler_params=_cp(1))(dssm.reshape(16, T // 16, DSS), _perm_matrix(tm), y, z, xr, xi, u, cre, cim,
                                       dvec, wglu)


def _ssm_da(gr, gi, xr, xi, layer):
    T = gr.shape[1]
    tb = 4096 if T % 4096 == 0 else T

    def body(gr_ref, gi_ref, xr_ref, xi_ref, dr_ref, di_ref, lr, li):
        t = pl.program_id(1)

        @pl.when(t == 0)
        def _():
            dr_ref[...] = jnp.zeros_like(dr_ref)
            di_ref[...] = jnp.zeros_like(di_ref)
            lr[...] = jnp.zeros_like(lr)
            li[...] = jnp.zeros_like(li)

        g_r, g_i, x_r, x_i = gr_ref[...], gi_ref[...], xr_ref[...], xi_ref[...]
        pr = pltpu.roll(x_r, 1, 0)
        pi_ = pltpu.roll(x_i, 1, 0)
        g0r, g0i = g_r[0:1, :], g_i[0:1, :]
        fr = lr[7:8, :] - x_r[tb - 1:tb, :]
        fi = li[7:8, :] - x_i[tb - 1:tb, :]
        dr_ref[...] += jnp.sum(g_r * pr + g_i * pi_, axis=0, keepdims=True) + g0r * fr + g0i * fi
        di_ref[...] += jnp.sum(g_i * pr - g_r * pi_, axis=0, keepdims=True) + g0i * fr - g0r * fi
        lr[...] = x_r[tb - 8:tb, :]
        li[...] = x_i[tb - 8:tb, :]

    st = pl.BlockSpec((None, tb, 128), lambda k, t: (k, t, 0))
    out = pl.BlockSpec((None, 1, 128), lambda k, t: (k, 0, 0))
    return _pc(body, name=f"ssm_da_l{layer}", grid=(NLB, T // tb), in_specs=[st] * 4, out_specs=[out, out],
               out_shape=[S((NLB, 1, 128), F32)] * 2, scratch_shapes=[pltpu.VMEM((8, 128), F32)] * 2,
               compiler_params=_cp(2))(gr, gi, xr, xi)


def _ssm_in_bwd(gr, gi, u, bre, bim, du_direct, layer):
    T = u.shape[0]
    tm = TM

    def body(gr_ref, gi_ref, u_ref, pm_ref, br_ref, bi_ref, dd_ref, du_ref, dbr_ref, dbi_ref):
        i = pl.program_id(0)

        @pl.when(i == 0)
        def _():
            dbr_ref[...] = jnp.zeros_like(dbr_ref)
            dbi_ref[...] = jnp.zeros_like(dbi_ref)

        dus = []
        for s in range(4):
            grc = jnp.concatenate([gr_ref[4 * s + q] for q in range(4)], axis=1)
            gic = jnp.concatenate([gi_ref[4 * s + q] for q in range(4)], axis=1)
            uc = u_ref[:, s * 128:(s + 1) * 128]
            dus.append(_dot2(_dot_nt, grc, br_ref[s], "b") + _dot2(_dot_nt, gic, bi_ref[s], "b"))
            dbr_ref[s] += _dot2(_dot_tn, uc, grc, "a")
            dbi_ref[s] += _dot2(_dot_tn, uc, gic, "a")
        du = jnp.concatenate(dus, axis=1) + dd_ref[...]
        du_ref[...] = _to_streams(du, pm_ref[...], exact=False).reshape(du_ref.shape)

    whole = pl.BlockSpec((4, 128, 512), lambda i: (0, 0, 0))
    st = pl.BlockSpec((NLB, tm, 128), lambda i: (0, i, 0))
    half = pl.BlockSpec((tm, DSS), lambda i: (i, 0))
    du, dbr, dbi = _pc(body, name=f"ssm_in_bwd_l{layer}", grid=(T // tm,),
                       in_specs=[st, st, half, pl.BlockSpec((tm, tm), lambda i: (0, 0)), whole, whole, half],
                       out_specs=[_stream_block(tm, DSS), whole, whole],
                       out_shape=[S((16, T // 16, DSS), F32), S((4, 128, 512), F32), S((4, 128, 512), F32)],
                       compiler_params=_cp(1))(gr, gi, u, _perm_matrix(tm), bre, bim, du_direct)
    return du.reshape(T, DSS), dbr, dbi


def _mix_out(outs, lses, ssm, h, attn_g, ssm_g, post_g, wout, layer):
    T = h.shape[0]
    tm = TM

    def body(o1, o2, o3, l1, l2, l3, s_ref, h_ref, ag_ref, sg_ref, pg_ref, w_ref, ho_ref, at_ref, ls_ref, mx_ref, mo_ref):
        la, lb, lc = l1[...], l2[...], l3[...]
        m = jnp.maximum(jnp.maximum(la, lb), lc)
        wa, wb, wc = jnp.exp(la - m), jnp.exp(lb - m), jnp.exp(lc - m)
        zs = wa + wb + wc
        attn = (wa * o1[...] + wb * o2[...] + wc * o3[...]) / zs
        at_ref[...] = attn
        ls_ref[...] = m + jnp.log(zs)
        mixed = jnp.concatenate([_rms_fwd(attn, ag_ref[...]), _rms_fwd(s_ref[...], sg_ref[...])], axis=1).astype(BF16)
        mx_ref[...] = mixed
        mo = sum(_dot(mixed[:, j * 256:(j + 1) * 256], w_ref[j]) for j in range(NSH))
        mo_ref[...] = mo
        ho_ref[...] = h_ref[...] + _rms_fwd(mo, pg_ref[...])

    row = pl.BlockSpec((tm, D), lambda i: (i, 0))
    half = pl.BlockSpec((tm, DA), lambda i: (i, 0))
    return _pc(body, name=f"mix_out_l{layer}", grid=(T // tm,),
               in_specs=[half] * 7 + [row, _gain_spec(DA, layer), _gain_spec(DSS, layer), _gain_spec(D, layer),
                                      pl.BlockSpec((NSH, None, 256, D), lambda i: (0, 0, 0, 0))],
               out_specs=[row, half, half, row, row],
               out_shape=[S((T, D), F32), S((T, DA), F32), S((T, DA), F32), S((T, D), BF16), S((T, D), F32)],
               compiler_params=_cp(1))(*outs, *lses, ssm, h, attn_g, ssm_g, post_g, wout)


def _mix_out_bwd(dout, mo, attn, ssm, attn_g, ssm_g, post_g, wout, layer):
    T = dout.shape[0]
    tm = TM
    head_sum =jnp.asarray(np.kron(np.eye(NH, dtype=np.float32), np.ones((HD, HD), np.float32)), BF16)

    def body(do_ref, mo_ref, at_ref, s_ref, ag_ref, sg_ref, pg_ref, w_ref, e_ref,
             da_ref, ds_ref, dl_ref, dmo_ref, dpg_ref, dag_ref, dsg_ref):
        i = pl.program_id(0)

        @pl.when(i == 0)
        def _():
            dpg_ref[...] = jnp.zeros_like(dpg_ref)
            dag_ref[...] = jnp.zeros_like(dag_ref)
            dsg_ref[...] = jnp.zeros_like(dsg_ref)

        dmo, dpg = _rms_bwd(do_ref[...], mo_ref[...], pg_ref[...])
        dpg_ref[...] += dpg
        dmob = dmo.astype(BF16)
        dmo_ref[...] = dmob
        dmix = jnp.concatenate([_dot_nt(dmob, w_ref[j]) for j in range(NSH)], axis=1)
        attn = at_ref[...]
        dat, dag = _rms_bwd(dmix[:, :DA], attn, ag_ref[...])
        dss, dsg = _rms_bwd(dmix[:, DA:], s_ref[...], sg_ref[...])
        dag_ref[...] += dag
        dsg_ref[...] += dsg
        da_ref[...] = dat
        ds_ref[...] = dss
        prod = dat * attn
        p1 = prod.astype(BF16)
        r1 = prod - p1.astype(F32)
        p2 = r1.astype(BF16)
        p3 = (r1 - p2.astype(F32)).astype(BF16)
        e = e_ref[...]
        dl_ref[...] = _dot(p1, e) + _dot(p2, e) + _dot(p3, e)

    row = pl.BlockSpec((tm, D), lambda i: (i, 0))
    half = pl.BlockSpec((tm, DA), lambda i: (i, 0))
    return _pc(body, name=f"mix_out_bwd_l{layer}", grid=(T // tm,),
               in_specs=[row, row, half, half, _gain_spec(DA, layer), _gain_spec(DSS, layer), _gain_spec(D, layer),
                         pl.BlockSpec((NSH, None, 256, D), lambda i: (0, 0, 0, 0)),
                         pl.BlockSpec((DA, DA), lambda i: (0, 0))],
               out_specs=[half, half, half, row, _row_acc_spec(D), _row_acc_spec(DA), _row_acc_spec(DSS)],
               out_shape=[S((T, DA), F32)] * 3 + [S((T, D), BF16), S((1, D), F32), S((1, DA), F32), S((1, DSS), F32)],
               compiler_params=_cp(1))(dout, mo, attn, ssm, attn_g, ssm_g, post_g, wout, head_sum)


def _ple_fwd(h, p3, wup, wgate, post_g, layer):
    T = h.shape[0]
    tm = TM

    def body(h_ref, p_ref, wu_ref, wg_ref, g_ref, ho_ref, e_ref, gt_ref):
        hv = h_ref[...]
        hb = hv.astype(BF16)
        pb = p_ref[...].astype(BF16)
        gte = sum(_dot(hb[:, j * 256:(j + 1) * 256], wg_ref[j]) for j in range(NSH))
        e = jnp.concatenate([_dot(pb, wu_ref[j]) for j in range(NSH)], axis=1)
        e_ref[...] = e
        gt_ref[...] = gte
        ho_ref[...] = hv + _rms_fwd(e * jax.nn.sigmoid(gte), g_ref[...])

    row = pl.BlockSpec((tm, D), lambda i: (i, 0))
    return _pc(body, name=f"ple_fwd_l{layer}", grid=(T // tm,),
               in_specs=[row, pl.BlockSpec((None, tm, PLE), lambda i: (layer, i, 0)),
                         pl.BlockSpec((NSH, None, PLE, 256), lambda i: (0, 0, 0, 0)),
                         pl.BlockSpec((NSH, None, 256, D), lambda i: (0, 0, 0, 0)), _gain_spec(D, layer)],
               out_specs=[row, row, row], out_shape=[S((T, D), F32)] * 3,
               compiler_params=_cp(1))(h, p3, wup, wgate, post_g)


def _ple_bwd(dout, e, gte, wgate, post_g, layer):
    T = dout.shape[0]
    tm = TM

    def body(do_ref, e_ref, gt_ref, wg_ref, g_ref, dh_ref, de_ref, dgt_ref, dg_ref):
        i = pl.program_id(0)

        @pl.when(i == 0)
        def _():
            dg_ref[...] = jnp.zeros_like(dg_ref)

        ev = e_ref[...]
        sg = jax.nn.sigmoid(gt_ref[...])
        do = do_ref[...]
        dple, dg = _rms_bwd(do, ev * sg, g_ref[...])
        dg_ref[...] += dg
        de = (dple * sg).astype(BF16)
        for j in range(NSH):
            de_ref[j] = de[:, j * 256:(j + 1) * 256]
        dgb = (dple * ev * sg * (1.0 - sg)).astype(BF16)
        dgt_ref[...] = dgb
        dh_ref[...] = do + jnp.concatenate([_dot_nt(dgb, wg_ref[j]) for j in range(NSH)], axis=1)

    row = pl.BlockSpec((tm, D), lambda i: (i, 0))
    return _pc(body, name=f"ple_bwd_l{layer}", grid=(T // tm,),
               in_specs=[row, row, row, pl.BlockSpec((NSH, None, 256, D), lambda i: (0, 0, 0, 0)), _gain_spec(D, layer)],
               out_specs=[row, pl.BlockSpec((NSH, tm, 256), lambda i: (0, i, 0)), row, _row_acc_spec(D)],
               out_shape=[S((T, D), F32), S((NSH, T, 256), BF16), S((T, D), BF16), S((1, D), F32)],
               compiler_params=_cp(1))(dout, e, gte, wgate, post_g)


def _loss_head(h, target):
    T = h.shape[0]
    tm = TM

    def body(h_ref, t_ref, dy_ref, l_ref):
        i = pl.program_id(0)

        @pl.when(i == 0)
        def _():
            l_ref[...] = jnp.zeros_like(l_ref)

        err = h_ref[...] - t_ref[...]
        dy_ref[...] = err * (1.0 / D)
        l_ref[...] += jnp.broadcast_to((0.5 / D) * jnp.sum(err * err), (1, 128))

    row = pl.BlockSpec((tm, D), lambda i: (i, 0))
    return _pc(body, name="loss_head", grid=(T // tm,), in_specs=[row, row],
               out_specs=[row, pl.BlockSpec((1, 128), lambda i: (0, 0))],
               out_shape=[S((T, D), F32), S((1, 128), F32)], compiler_params=_cp(1))(h, target)


def _local_step(x, p3, pos_col, target, weights_of, layer_grads_done, Sm):
    L = p3.shape[0]
    g3 = {n: Sm[n].reshape(L, 1, -1) for n in ("ffn1_pre_g", "ffn1_post_g", "mix_pre_g", "attn_norm_g", "ssm_norm_g",
                                                "mix_post_g", "ffn2_pre_g", "ffn2_post_g", "ple_post_g", "ssm_b_glu", "ssm_d")}
    rot = _rot_tables(pos_col)
    prep_names = ("ssm_lam_re", "ssm_lam_im", "ssm_log_dt", "ssm_b_re", "ssm_b_im", "ssm_c_re", "ssm_c_im")
    prep_all, prep_vjp = jax.vjp(jax.vmap(_ssm_prep), *[Sm[n] for n in prep_names])
    prep_cot = [None] * L

    saved = []
    h = x
    for l in range(L):
        W = weights_of(l, h)
        sv = {"h0": h, "W": W}
        h, sv["a1"], sv["b1"], sv["f1"], sv["xn1"] = _ffn_fwd(
            h, g3["ffn1_pre_g"], g3["ffn1_post_g"], W["ffn1_w_gate"], W["ffn1_w_up"], W["ffn1_w_down"], l, "1")
        sv["h1"] = h
        P, sv["ain"] = _mix_proj(h, g3["mix_pre_g"], W["w_in"], rot, l)
        sv["P"] = P
        ol = [_attn_fwd(P, d, l) for d in PATTERN_DILATIONS]
        prep = tuple(t[l] for t in prep_all)
        a_re, a_im, bre, bim, cre, cim = prep
        sv["prep"] = prep
        sv["u"], bur, bui = _ssm_in(P, bre, bim, l)
        xr, xi = _scan(bur, bui, a_re, a_im, False, l)
        sv["xr"], sv["xi"] = xr, xi
        ssm, sv["y"], sv["z"] = _ssm_out(xr, xi, sv["u"], cre, cim, g3["ssm_d"], W["ssm_w_glu"], g3["ssm_b_glu"], l)
        sv["ssm"] = ssm
        h, sv["attn"], sv["lse"], sv["mixed"], sv["mo"] = _mix_out(
            [o for o, _ in ol], [s for _, s in ol], ssm, h, g3["attn_norm_g"], g3["ssm_norm_g"], g3["mix_post_g"],
            W["w_out"], l)
        sv["h2"] = h
        h, sv["a2"], sv["b2"], sv["f2"], sv["xn2"] = _ffn_fwd(
            h, g3["ffn2_pre_g"], g3["ffn2_post_g"], W["ffn2_w_gate"], W["ffn2_w_up"], W["ffn2_w_down"], l, "2")
        sv["h3"] = h
        h, sv["e"], sv["gte"] = _ple_fwd(h, p3, W["ple_w_up"], W["ple_w_gate"], g3["ple_post_g"], l)
        saved.append(sv)

    dh, loss = _loss_head(h, target)

    G_layers = [{n: lax.empty((NSH, 1, r, c), BF16) for n, r, c in BIG} for _ in range(L)]
    sg = {n: [None] * L for n in SMALL}
    whole, shard, kcol = "whole", "shard", "cols"
    ple_g = g3["ple_post_g"]
    for l in reversed(range(L)):
        sv = saved[l]
        W = sv["W"]
        G, gl = G_layers[l], 0
        if l + 1 < L:
            ple_g = ple_g + layer_grads_done(l + 1, G_layers[l + 1])
        dh, de, dgte, sg["ple_post_g"][l] = _ple_bwd(dh, sv["e"], sv["gte"], W["ple_w_gate"], ple_g, l)
        G["ple_w_up"] = _dw(p3[l][None], de, G["ple_w_up"], gl, PLE, 256, whole, shard, f"dw_ple_up_l{l}")
        G["ple_w_gate"] = _dw(sv["h3"][None], dgte[None], G["ple_w_gate"], gl, 256, D, kcol, whole, f"dw_ple_gate_l{l}")
        dh, df, da, db, hh, sg["ffn2_pre_g"][l], sg["ffn2_post_g"][l] = _ffn_bwd(
            dh, sv["h2"], sv["f2"], sv["a2"], sv["b2"], g3["ffn2_pre_g"], g3["ffn2_post_g"],
            W["ffn2_w_gate"], W["ffn2_w_up"], W["ffn2_w_down"], l, "2")
        G["ffn2_w_gate"] = _dw(da, sv["xn2"][None], G["ffn2_w_gate"], gl, DFS, D, shard, whole, f"dw_ffn2_gate_l{l}")
        G["ffn2_w_up"] = _dw(db, sv["xn2"][None], G["ffn2_w_up"], gl, DFS, D, shard, whole, f"dw_ffn2_up_l{l}")
        G["ffn2_w_down"] = _dw(hh, df[None], G["ffn2_w_down"], gl, DFS, D, shard, whole, f"dw_ffn2_down_l{l}")
        a_re, a_im, bre, bim, cre, cim = sv["prep"]
        dattn, dssm, delta, dmo, sg["mix_post_g"][l], sg["attn_norm_g"][l], sg["ssm_norm_g"][l] = _mix_out_bwd(
            dh, sv["mo"], sv["attn"], sv["ssm"], g3["attn_norm_g"], g3["ssm_norm_g"], g3["mix_post_g"], W["w_out"], l)
        G["w_out"] = _dw(sv["mixed"][None], dmo[None], G["w_out"], gl, 256, D, kcol, whole, f"dw_out_l{l}")
        gnr, gni, du_direct, dz, yg, sg["ssm_b_glu"][l], dd, dcre, dcim = _ssm_out_bwd(
            dssm, sv["y"], sv["z"], sv["xr"], sv["xi"], sv["u"], cre, cim, g3["ssm_d"], W["ssm_w_glu"], l)
        sg["ssm_d"][l] = dd.reshape(Sm["ssm_d"].shape[1:])
        G["ssm_w_glu"] = _dw(yg[None], dz[None], G["ssm_w_glu"], gl, 128, DSS, kcol, whole, f"dw_glu_l{l}")
        gr, gi = _scan(gnr, gni, a_re, a_im, True, l)
        dar, dai = _ssm_da(gr, gi, sv["xr"], sv["xi"], l)
        du, dbre, dbim = _ssm_in_bwd(gr, gi, sv["u"], bre, bim, du_direct, l)
        prep_cot[l] = (dar, dai, dbre, dbim, dcre, dcim)
        acc = None
        for d in PATTERN_DILATIONS:
            acc = _attn_bwd(sv["P"], dattn, sv["lse"], delta, acc, d, l)
        dh, dP, sg["mix_pre_g"][l] = _mix_proj_bwd(acc[0], acc[1], acc[2], du, dh, sv["h1"], g3["mix_pre_g"],
                                                   W["w_in"], rot, l)
        G["w_in"] = _dw(sv["ain"][None], dP, G["w_in"], gl, D, DA,whole, shard, f"dw_in_l{l}")
        dh, df, da, db, hh, sg["ffn1_pre_g"][l], sg["ffn1_post_g"][l] = _ffn_bwd(
            dh, sv["h0"], sv["f1"], sv["a1"], sv["b1"], g3["ffn1_pre_g"], g3["ffn1_post_g"],
            W["ffn1_w_gate"], W["ffn1_w_up"], W["ffn1_w_down"], l, "1")
        G["ffn1_w_gate"] = _dw(da, sv["xn1"][None], G["ffn1_w_gate"], gl, DFS, D, shard, whole, f"dw_ffn1_gate_l{l}")
        G["ffn1_w_up"] = _dw(db, sv["xn1"][None], G["ffn1_w_up"], gl, DFS, D, shard, whole, f"dw_ffn1_up_l{l}")
        G["ffn1_w_down"] = _dw(hh, df[None], G["ffn1_w_down"], gl, DFS, D, shard, whole, f"dw_ffn1_down_l{l}")

    small = {n: jnp.stack([g.reshape(Sm[n].shape[1:]) for g in sg[n]]) for n in SMALL if n not in prep_names}
    small.update(zip(prep_names, prep_vjp(tuple(jnp.stack(c) for c in zip(*prep_cot)))))
    return loss, dh, G_layers[0], small


HBM_SPEC = pl.BlockSpec(memory_space=pltpu.HBM)


def _place():
    x, y, c = lax.axis_index("x"), lax.axis_index("y"), lax.axis_index("c")
    chips = [(1 - x, y), (x, 1 - y), (1 - x, 1 - y)]
    return x, y, c, chips


def _comm_params():
    return pltpu.CompilerParams(vmem_limit_bytes=VMEM_LIMIT)


def _gather_weights(ws, lands):
    n = len(ws)

    def body(*refs):
        ins, outs = refs[:n], refs[2 * n:3 * n]
        s_ici, r_ici, s_d2d, r_d2d = refs[3 * n:]
        x, y, c, chips = _place()

        def half(ref, t, hc):
            r2 = ws[t].shape[1] // 2
            return ref.at[:, pl.ds(hc * r2, r2), :]

        def ici(t, k, src_chip, to):
            j = 2 * src_chip[0] + src_chip[1]
            src = half(ins[t], t, c) if to is not None else half(outs[t].at[j], t, c)
            return pltpu.make_async_remote_copy(src_ref=src, dst_ref=half(outs[t].at[j], t, c),
                                                send_sem=s_ici.at[3 * t + k], recv_sem=r_ici.at[3 * t + k],
                                                device_id=to if to is not None else (x, y, c), device_id_type=MESH)

        def d2d(t, k, hc):
            j = 2 * chips[k][0] + chips[k][1]
            r = half(outs[t].at[j], t, hc)
            return pltpu.make_async_remote_copy(src_ref=r, dst_ref=r, send_sem=s_d2d.at[3 * t + k],
                                                recv_sem=r_d2d.at[3 * t + k], device_id=(x, y, 1 - c),
                                                device_id_type=MESH)

        sends = [ici(t, k, (x, y), (*chips[k], c)) for t in range(n) for k in range(3)]
        for cp in sends:
            cp.start()
        passed = []
        for t in range(n):
            for k in range(3):
                ici(t, k, chips[k], None).wait_recv()
                passed.append(d2d(t, k, c))
                passed[-1].start()
        for t in range(n):
            for k in range(3):
                d2d(t, k, 1 - c).wait_recv()
        for cp in sends + passed:
            cp.wait_send()

    return _pc(body, name="gather_weights", in_specs=[HBM_SPEC] * (2 * n), out_specs=[HBM_SPEC] * n,
               out_shape=[S(z.shape, z.dtype) for z in lands], input_output_aliases={n + t: t for t in range(n)},
               scratch_shapes=[pltpu.SemaphoreType.DMA((3 * n,))] * 4, compiler_params=_comm_params())(*ws, *lands)


SEM_SPEC = pl.BlockSpec(memory_space=pltpu.SEMAPHORE)
ANY_SPEC = pl.BlockSpec(memory_space=pl.ANY)
SPLIT_EFFECT = pltpu.SideEffectType.DATAFLOW_SIDE_EFFECTING


def _in_hbm(t):
    return pltpu.with_memory_space_constraint(t, pltpu.HBM)


def _place_own(ws, me_arr, layer):
    n = len(ws)

    def body(me_ref, *refs):
        for t in range(n):
            refs[n + t][...] = refs[t][...]

    gs = pltpu.PrefetchScalarGridSpec(
        num_scalar_prefetch=1, grid=(2,),
        in_specs=[pl.BlockSpec((w.shape[0], w.shape[1] // 2, w.shape[2]), lambda i, me: (0, i, 0)) for w in ws],
        out_specs=[pl.BlockSpec((None, w.shape[0], w.shape[1] // 2, w.shape[2]), lambda i, me: (me[0], 0, i, 0))
                   for w in ws])
    return _pc(body, name=f"gather_place_own_l{layer}", grid_spec=gs,
               out_shape=[S((NSH,) + w.shape, w.dtype) for w in ws], compiler_params=_cp(1))(me_arr, *ws)


def _gather_start(ws, lands, after, layer):
    n = len(ws)

    def body(*refs):
        ins, lz = refs[:n], refs[n:2 * n]
        s_sem, r_sem = refs[2 * n + 1], refs[2 * n + 2]
        token = refs[-1]
        x, y, c, chips = _place()
        for t in range(n):
            for k in range(3):
                pltpu.make_async_remote_copy(src_ref=ins[t], dst_ref=lz[t].at[2 * x + y], send_sem=s_sem.at[3 * t + k],
                                             recv_sem=r_sem.at[3 * t + k], device_id=(*chips[k], c),
                                             device_id_type=MESH).start()
        token[...] = jnp.zeros_like(token)

    hbm = [pltpu.HBM(w.shape, w.dtype) for w in ws] + [pltpu.HBM(z.shape, z.dtype) for z in lands]
    out = _pc(body, name=f"gather_start_l{layer}",
              out_shape=(pltpu.SemaphoreType.DMA((3 * n,)), pltpu.SemaphoreType.DMA((3 * n,)), *hbm, S((8, 128), F32)),
              in_specs=[HBM_SPEC] * (2 * n) + [ANY_SPEC],
              out_specs=(SEM_SPEC, SEM_SPEC, *([HBM_SPEC] * (2 * n)), pl.BlockSpec(memory_space=pltpu.VMEM)),
              input_output_aliases={i: 2 + i for i in range(2 * n)},
              compiler_params=pltpu.CompilerParams(has_side_effects=SPLIT_EFFECT))(
                  *[_in_hbm(w) for w in ws], *[_in_hbm(z) for z in lands], after)
    return out[0], out[1], out[2:2 + n], out[2 + n:2 + 2 * n], out[-1]


def _gather_wait(s_sem, r_sem, ws, lands, after, layer):
    n = len(ws)

    def body(*refs):
        ins, lz = refs[:n], refs[n:2 * n]
        s_ref, r_ref = refs[2 * n], refs[2 * n + 1]
        x, y, c, chips = _place()
        for t in range(n):
            for k in range(3):
                cp = pltpu.make_async_remote_copy(src_ref=ins[t], dst_ref=lz[t].at[2 * x + y], send_sem=s_ref.at[3 * t + k],
                                                  recv_sem=r_ref.at[3 * t + k], device_id=(*chips[k], c),
                                                  device_id_type=MESH)
                cp.wait_send()
                cp.wait_recv()

    hbm = [pltpu.HBM(w.shape, w.dtype) for w in ws] + [pltpu.HBM(z.shape, z.dtype) for z in lands]
    out = _pc(body, name=f"gather_wait_l{layer}", out_shape=tuple(hbm),
              in_specs=[HBM_SPEC] * (2 * n) + [SEM_SPEC, SEM_SPEC, ANY_SPEC], out_specs=tuple([HBM_SPEC] * (2 * n)),
              input_output_aliases={i: i for i in range(2 * n)},
              compiler_params=pltpu.CompilerParams(has_side_effects=SPLIT_EFFECT))(*ws, *lands, s_sem, r_sem, after)
    return out[n:]


def _swap_halves(gs, tag):
    n = len(gs)

    def body(*refs):
        ins, outs = refs[:n], refs[n:2 * n]
        s_sem, r_sem = refs[2 * n:]
        x, y, c, _ = _place()
        cps = []
        for t in range(n):
            r2 = gs[t].shape[2] // 2
            cps.append(pltpu.make_async_remote_copy(
                src_ref=ins[t].at[:, :, pl.ds((1 - c) * r2, r2), :], dst_ref=outs[t], send_sem=s_sem.at[t],
                recv_sem=r_sem.at[t], device_id=(x, y, 1 - c), device_id_type=MESH))
            cps[-1].start()
        for cp in cps:
            cp.wait_recv()
        for cp in cps:
            cp.wait_send()

    return _pc(body, name=f"grad_swap_halves_{tag}", in_specs=[HBM_SPEC] * n, out_specs=[HBM_SPEC] * n,
               out_shape=[S(g.shape[:2] + (g.shape[2] // 2, g.shape[3]), g.dtype) for g in gs],
               scratch_shapes=[pltpu.SemaphoreType.DMA((n,))] * 2, compiler_params=_comm_params())(*gs)


def _add_half(g, landed, c_arr, name):
    _, L, r2, cols = landed.shape

    def body(c_ref, g_ref, l_ref, o_ref):
        o_ref[...] = (g_ref[...].astype(F32) + l_ref[...].astype(F32)).astype(BF16)

    gs = pltpu.PrefetchScalarGridSpec(
        num_scalar_prefetch=1, grid=(NSH, L),
        in_specs=[pl.BlockSpec((None, None, r2, cols), lambda j, l, c: (j, l, c[0], 0)),
                  pl.BlockSpec((None, None, r2, cols), lambda j, l, c: (j, l, 0, 0))],
        out_specs=pl.BlockSpec((None, None, r2, cols), lambda j, l, c: (j, l, 0, 0)))
    return _pc(body, name=name, grid_spec=gs, out_shape=S(landed.shape, BF16), compiler_params=_cp(2))(c_arr, g, landed)


def _partial_copies(ins, lz, s_sem, r_sem):
    x, y, c, chips = _place()
    return [pltpu.make_async_remote_copy(src_ref=ins[t].at[2 * chips[k][0] + chips[k][1]], dst_ref=lz[t].at[k],
                                         send_sem=s_sem.at[3 * t + k], recv_sem=r_sem.at[3 * t + k],
                                         device_id=(*chips[k], c), device_id_type=MESH)
            for t in range(len(ins)) for k in range(3)]


def _partial_send_start(ps, lands):
    n = len(ps)

    def body(*refs):
        for cp in _partial_copies(refs[:n], refs[n:2 * n], refs[2 * n], refs[2 * n + 1]):
            cp.start()
        refs[-1][...] = jnp.zeros_like(refs[-1])

    hbm = [pltpu.HBM(p.shape, p.dtype) for p in ps] + [pltpu.HBM(z.shape, z.dtype) for z in lands]
    out = _pc(body, name="grad_partial_send_start",
              out_shape=(pltpu.SemaphoreType.DMA((3 * n,)), pltpu.SemaphoreType.DMA((3 * n,)), *hbm, S((8, 128), F32)),
              in_specs=[HBM_SPEC] * (2 * n),
              out_specs=(SEM_SPEC, SEM_SPEC, *([HBM_SPEC] * (2 * n)), pl.BlockSpec(memory_space=pltpu.VMEM)),
              input_output_aliases={i: 2 + i for i in range(2 * n)},
              compiler_params=pltpu.CompilerParams(has_side_effects=SPLIT_EFFECT))(
                  *[_in_hbm(p) for p in ps], *[_in_hbm(z) for z in lands])
    return out[0], out[1], out[2:2 + n], out[2 + n:2 + 2 * n], out[-1]


def _partial_send_wait(s_sem, r_sem, ps, lands, after):
    n = len(ps)

    def body(*refs):
        for cp in _partial_copies(refs[:n], refs[n:2 * n], refs[2 * n], refs[2 * n + 1]):
            cp.wait_send()
            cp.wait_recv()

    hbm = [pltpu.HBM(p.shape, p.dtype) for p in ps] + [pltpu.HBM(z.shape, z.dtype) for z in lands]
    out = _pc(body, name="grad_partial_send_wait", out_shape=tuple(hbm),
              in_specs=[HBM_SPEC] * (2 * n) + [SEM_SPEC, SEM_SPEC, ANY_SPEC], out_specs=tuple([HBM_SPEC] * (2 * n)),
              input_output_aliases={i: i for i in range(2 * n)},
              compiler_params=pltpu.CompilerParams(has_side_effects=SPLIT_EFFECT))(*ps, *lands, s_sem, r_sem, after)
    return out[:n], out[n:]


def _sum_shards(part, landed, me_arr, c_arr, buf, first_layer, name):
    _, nl, r2, cols = landed.shape

    def body(me_ref, c_ref, p_ref, l_ref, b_ref, o_ref):
        o_ref[...] = ((p_ref[...].astype(F32) + l_ref[0].astype(F32)) + l_ref[1].astype(F32)) + l_ref[2].astype(F32)

    gs = pltpu.PrefetchScalarGridSpec(
        num_scalar_prefetch=2, grid=(nl,),
        in_specs=[pl.BlockSpec((None, None, r2, cols), lambda l, me, c: (me[0], l, 0, 0)),
                  pl.BlockSpec((3, None, r2, cols), lambda l, me, c: (0, l, 0, 0)), ANY_SPEC],
        out_specs=pl.BlockSpec((None, r2, cols), lambda l, me, c: (first_layer + l, c[0], 0)))
    return _pc(body, name=name, grid_spec=gs, out_shape=S(buf.shape, F32), input_output_aliases={4: 0},
               compiler_params=_cp(1))(me_arr, c_arr, part, landed, buf)


def _direct_grad_copies(ins, lz, s_sem, r_sem):
    x, y, c, chips = _place()
    sends, recvs = [], []
    for t in range(len(ins)):
        r2 = ins[t].shape[2] // 2
        half = lambda j, h: ins[t].at[j, :, pl.ds(h * r2, r2), :]

        def copy(src, slot, s_idx, r_idx, to):
            return pltpu.make_async_remote_copy(src_ref=src, dst_ref=lz[t].at[slot], send_sem=s_sem.at[7 * t + s_idx],
                                                recv_sem=r_sem.at[7 * t + r_idx], device_id=to, device_id_type=MESH)

        for k in range(3):
            for h in range(2):
                sends.append(copy(half(2 * chips[k][0] + chips[k][1], h), 2 * k + c, 2 * k + h, 2 * k + c, (*chips[k], h)))
        sends.append(copy(half(2 * x + y, 1 - c), 6, 6, 6, (x, y, 1 - c)))
        recvs += [copy(half(0, 0), s, s, s, (x, y, c)) for s in range(7)]
    return sends, recvs


def _send_start(gs, lands, layer):
    n = len(gs)
    ps = gs

    def body(*refs):
        sends, _ = _direct_grad_copies(refs[:n], refs[n:2 * n], refs[2 * n], refs[2 * n + 1])
        for cp in sends:
            cp.start()
        refs[-1][...] = jnp.zeros_like(refs[-1])

    hbm = [pltpu.HBM(p.shape, p.dtype) for p in ps] + [pltpu.HBM(z.shape, z.dtype) for z in lands]
    out = _pc(body, name=f"grad_send_start_l{layer}",
              out_shape=(pltpu.SemaphoreType.DMA((7 * n,)), pltpu.SemaphoreType.DMA((7 * n,)), *hbm, S((8, 128), F32)),
              in_specs=[HBM_SPEC] * (2 * n),
              out_specs=(SEM_SPEC, SEM_SPEC, *([HBM_SPEC] * (2 * n)), pl.BlockSpec(memory_space=pltpu.VMEM)),
              input_output_aliases={i: 2 + i for i in range(2 * n)},
              compiler_params=pltpu.CompilerParams(has_side_effects=SPLIT_EFFECT))(
                  *[_in_hbm(p) for p in ps], *[_in_hbm(z) for z in lands])
    return out[0], out[1], out[2:2 + n], out[2 + n:2 + 2 * n], out[-1]


def _send_wait(s_sem, r_sem, ps, lands, after, layer):
    n = len(ps)

    def body(*refs):
        sends, recvs = _direct_grad_copies(refs[:n], refs[n:2 * n], refs[2 * n], refs[2 * n + 1])
        for cp in sends:
            cp.wait_send()
        for cp in recvs:
            cp.wait_recv()

    hbm = [pltpu.HBM(p.shape, p.dtype) for p in ps] + [pltpu.HBM(z.shape, z.dtype) for z in lands]
    out = _pc(body, name=f"grad_send_wait_l{layer}", out_shape=tuple(hbm),
              in_specs=[HBM_SPEC] * (2 * n) + [SEM_SPEC, SEM_SPEC, ANY_SPEC], out_specs=tuple([HBM_SPEC] * (2 * n)),
              input_output_aliases={i: i for i in range(2 * n)},
              compiler_params=pltpu.CompilerParams(has_side_effects=SPLIT_EFFECT))(*ps, *lands, s_sem, r_sem, after)
    return out[:n], out[n:]


def _sum_direct(g, landed, me_arr, c_arr, buf, first_layer, name):
    _, nl, r2, cols = landed.shape

    def body(me_ref, c_ref, g_ref, l_ref, b_ref, o_ref):
        tot = g_ref[...].astype(F32)
        for s in range(7):
            tot = tot + l_ref[s].astype(F32)
        o_ref[...] = tot

    gs = pltpu.PrefetchScalarGridSpec(
        num_scalar_prefetch=2, grid=(nl,),
        in_specs=[pl.BlockSpec((None, None, r2, cols), lambda l, me, c: (me[0], l, c[0], 0)),
                  pl.BlockSpec((7, None, r2, cols), lambda l, me, c: (0, l, 0, 0)), ANY_SPEC],
        out_specs=pl.BlockSpec((None, r2, cols), lambda l, me, c: (first_layer + l, c[0], 0)))
    return _pc(body, name=name, grid_spec=gs, out_shape=S(buf.shape, F32), input_output_aliases={4: 0},
               compiler_params=_cp(1))(me_arr, c_arr, g, landed, buf)


def _share_halves(bufs):
    n = len(bufs)

    def body(*refs):
        ins, outs = refs[:n], refs[n:2 * n]
        s_sem, r_sem = refs[2 * n:]
        x, y, c, _ = _place()
        cps = []
        for t in range(n):
            r2 = bufs[t].shape[1] // 2
            cps.append(pltpu.make_async_remote_copy(
                src_ref=ins[t].at[:, pl.ds(c * r2, r2), :], dst_ref=outs[t].at[:, pl.ds(c * r2, r2), :],
                send_sem=s_sem.at[t], recv_sem=r_sem.at[t], device_id=(x, y, 1 - c), device_id_type=MESH))
            cps[-1].start()
        for cp in cps:
            cp.wait_recv()
        for cp in cps:
            cp.wait_send()

    return _pc(body, name="grad_share_halves", in_specs=[HBM_SPEC] * n, out_specs=[HBM_SPEC] * n,
               out_shape=[S(b.shape, b.dtype) for b in bufs], input_output_aliases={t: t for t in range(n)},
               scratch_shapes=[pltpu.SemaphoreType.DMA((n,))] * 2, compiler_params=_comm_params())(*bufs)


def _gather_small(v, after):
    nr = v.shape[0]

    def body(v_ref, after_ref, out_ref, send_sems, recv_sems, local_sem):
        x, y, c, chips = _place()
        me, sibling = (x, y, c), (x, y, 1 - c)

        def rows(px, py, pc):
            return out_ref.at[pl.ds((4 * px + 2 * py + pc) * nr, nr), :]

        def copy(k, block, to, src=None):
            return pltpu.make_async_remote_copy(src_ref=rows(*block) if src is None else src, dst_ref=rows(*block),
                                                send_sem=send_sems.at[k], recv_sem=recv_sems.at[k], device_id=to,
                                                device_id_type=MESH)

        mine = pltpu.make_async_copy(v_ref, rows(*me), local_sem)
        mine.start()
        first = [copy(0, me, sibling, src=v_ref)]
        first += [copy(1 + j, me, (*chip, c), src=v_ref) for j, chip in enumerate(chips)]
        for cp in first:
            cp.start()
        passed = [copy(4 + j, (*chip, c), sibling) for j, chip in enumerate(chips)]
        for j, chip in enumerate(chips):
            copy(1 + j, (*chip, c), me).wait_recv()
            passed[j].start()
        copy(0, sibling, me).wait_recv()
        for j, chip in enumerate(chips):
            copy(4 + j, (*chip, 1 - c), me).wait_recv()
        for cp in first + passed:
            cp.wait_send()
        mine.wait()

    vm = pl.BlockSpec(memory_space=pltpu.VMEM)
    return _pc(body, name="gather_small_grads", in_specs=[vm, ANY_SPEC], out_specs=vm, out_shape=S((8 * nr, 128), F32),
               scratch_shapes=[pltpu.SemaphoreType.DMA((7,)), pltpu.SemaphoreType.DMA((7,)), pltpu.SemaphoreType.DMA],
               compiler_params=_comm_params())(v, after)


def _adamw_math(w, g, m, v):
    m2 = ADAM_B1 * m + (1.0 - ADAM_B1) * g
    v2 = ADAM_B2 * v + (1.0 - ADAM_B2) * (g * g)
    m_hat = m2 / (1.0 - ADAM_B1 ** ADAM_STEP)
    v_hat = v2 / (1.0 - ADAM_B2 ** ADAM_STEP)
    return -ADAM_LR * (m_hat / (jnp.sqrt(v_hat) + ADAM_EPS) + ADAM_WD * w), m2, v2


def _adamw(w, g, m, v, name):
    L, R, C = w.shape
    rb = R // 2 if R >= 512 else R

    def body(w_ref, g_ref, m_ref, v_ref, d_ref, m2_ref, v2_ref):
        d_ref[...], m2_ref[...], v2_ref[...] = _adamw_math(w_ref[...], g_ref[...], m_ref[...], v_ref[...])

    blk = pl.BlockSpec((None, rb, C), lambda l, r: (l, r, 0))
    return _pc(body, name=name, grid=(L, R // rb), in_specs=[blk] * 4, out_specs=[blk] * 3,
               out_shape=[S(w.shape, F32)] * 3, compiler_params=_cp(2))(w, g, m, v)


def _adamw_small(gathered, w, m, v):
    nr = w.shape[0]
    rb = nr // 5

    def body(a_ref, w_ref, m_ref, v_ref, g_ref, d_ref, m2_ref, v2_ref):
        g = a_ref[0]
        for k in range(1, 8):
            g = g + a_ref[k]
        g_ref[...] = g
        d_ref[...], m2_ref[...], v2_ref[...] = _adamw_math(w_ref[...], g, m_ref[...], v_ref[...])

    blk = pl.BlockSpec((rb, 128), lambda i: (i, 0))
    return _pc(body, name="adamw_small", grid=(nr // rb,), in_specs=[pl.BlockSpec((8, rb, 128), lambda i: (0, i, 0))] + [blk] * 3,
               out_specs=[blk] * 4, out_shape=[S((nr, 128), F32)] * 4, compiler_params=_cp(1))(gathered, w, m, v)


SMALL_ROWS = 4520


def _pack(arrs):
    flat = jnp.concatenate([a.reshape(-1) for a in arrs])
    return jnp.pad(flat, (0, SMALL_ROWS * 128 - flat.shape[0])).reshape(SMALL_ROWS, 128)


def _unpack(packed, like):
    flat = packed.reshape(-1)
    out, off = [], 0
    for a in like:
        out.append(flat[off:off + a.size].reshape(a.shape))
        off += a.size
    return out


def kernel(x, p, positions, ffn1_pre_g, ffn1_w_gate, ffn1_w_up, ffn1_w_down, ffn1_post_g, mix_pre_g, w_in, attn_norm_g, ssm_lam_re, ssm_lam_im, ssm_log_dt, ssm_b_re, ssm_b_im, ssm_c_re, ssm_c_im, ssm_d, ssm_w_glu, ssm_b_glu, ssm_norm_g, w_out, mix_post_g, ffn2_pre_g, ffn2_w_gate, ffn2_w_up, ffn2_w_down, ffn2_post_g, ple_w_up, ple_w_gate, ple_post_g, loss_target, m_ffn1_pre_g, m_ffn1_w_gate, m_ffn1_w_up, m_ffn1_w_down, m_ffn1_post_g, m_mix_pre_g, m_w_in, m_attn_norm_g, m_ssm_lam_re, m_ssm_lam_im, m_ssm_log_dt, m_ssm_b_re, m_ssm_b_im, m_ssm_c_re, m_ssm_c_im, m_ssm_d, m_ssm_w_glu, m_ssm_b_glu, m_ssm_norm_g, m_w_out, m_mix_post_g, m_ffn2_pre_g, m_ffn2_w_gate, m_ffn2_w_up, m_ffn2_w_down, m_ffn2_post_g, m_ple_w_up, m_ple_w_gate, m_ple_post_g, v_ffn1_pre_g, v_ffn1_w_gate, v_ffn1_w_up, v_ffn1_w_down, v_ffn1_post_g, v_mix_pre_g, v_w_in, v_attn_norm_g, v_ssm_lam_re, v_ssm_lam_im, v_ssm_log_dt, v_ssm_b_re, v_ssm_b_im, v_ssm_c_re, v_ssm_c_im, v_ssm_d, v_ssm_w_glu, v_ssm_b_glu, v_ssm_norm_g, v_w_out, v_mix_post_g, v_ffn2_pre_g, v_ffn2_w_gate, v_ffn2_w_up, v_ffn2_w_down, v_ffn2_post_g, v_ple_w_up, v_ple_w_gate, v_ple_post_g):
    a = dict(locals())
    T = x.shape[1]
    big_names = [n for n, _, _ in BIG]
    for n in TRANSPOSED:
        for pre in ("", "m_", "v_"):
            a[pre + n] = jnp.swapaxes(a[pre + n], 1, 2)

    own = [a[n].astype(BF16) for n in big_names]
    n_layers = own[0].shape[0]
    per_layer = [[w[l:l + 1] for w in own] for l in range(n_layers)]
    c_arr = lax.axis_index("c").astype(jnp.int32).reshape(1)
    me_arr = (2 * lax.axis_index("x") + lax.axis_index("y")).astype(jnp.int32).reshape(1)
    first = dict(zip(big_names, _gather_weights(per_layer[0], _place_own(per_layer[0], me_arr, 0))))
    pending, anchor, queued_behind = {}, jnp.zeros((), F32), first[big_names[0]]
    for l in range(1, n_layers):
        s_sem, r_sem, ws_thru, lands_thru, token = _gather_start(per_layer[l], _place_own(per_layer[l], me_arr, l),
                                                                 queued_behind, l)
        pending[l] = (s_sem, r_sem, ws_thru, lands_thru)
        anchor = anchor + token[0, 0]
        queued_behind = token

    def weights_of(l, after):
        if l == 0:
            return first
        return dict(zip(big_names, _gather_wait(*pending[l], after, l)))

    Sm = {n: a[n] for n in SMALL}
    Sm["ffn1_pre_g"] = Sm["ffn1_pre_g"] + anchor

    pos = jnp.broadcast_to(positions.reshape(1, T, 1).astype(F32), (1, T, 128))
    sent = {}

    def layer_grads_done(l, G):
        gs = [G[n] for n in big_names]
        lands = [lax.empty((7, 1, g.shape[2] // 2, g.shape[3]), BF16) for g in gs]
        s_sem, r_sem, gs_thru, lands_thru, token = _send_start(gs, lands, l)
        sent[l] = (s_sem, r_sem, gs_thru, lands_thru)
        return token[0, 0]

    loss, gx, G_first, small = _local_step(
        _reorder(x, True, "to_streams_x")[0], _reorder(p[:, 0], True, "to_streams_p"),
        _reorder(pos, True, "to_streams_pos")[0, :, :1], _reorder(loss_target, True, "to_streams_target")[0],
        weights_of, layer_grads_done, Sm)
    gx = _reorder(gx[None], False, "to_time_grad_x")

    gs0 = [G_first[n] for n in big_names]
    parts = [_add_half(g, la, c_arr, f"grad_add_half_{n}") for g, la, n in zip(gs0, _swap_halves(gs0, "first"), big_names)]
    first_sent = _partial_send_start(parts, [lax.empty((3,) + pt.shape[1:], BF16) for pt in parts])

    bufs = [lax.empty((n_layers, r, c), F32) for _, r, c in BIG]
    for l in sorted(sent, reverse=True):
        gs, landed = _send_wait(*sent[l], first_sent[-1], l)
        bufs = [_sum_direct(g, la, me_arr, c_arr, b, l, f"grad_sum_direct_l{l}_{n}")
                for g, la, b, n in zip(gs, landed, bufs, big_names)]
    small_g = _gather_small(_pack([small[n] for n in SMALL]), bufs[0]).reshape(8, SMALL_ROWS, 128)
    sg, sd, sm, sv = _adamw_small(small_g, _pack([a[n] for n in SMALL]), _pack([a["m_" + n] for n in SMALL]),
                                  _pack([a["v_" + n] for n in SMALL]))

    parts, landed = _partial_send_wait(*first_sent[:-1], sd)
    bufs = [_sum_shards(pt, la, me_arr, c_arr, b, 0, f"grad_sum_shards_first_{n}")
            for pt, la, b, n in zip(parts, landed, bufs, big_names)]
    grads = dict(zip(big_names, _share_halves(bufs)))
    like = [a[n] for n in SMALL]
    res = {}
    for n, g_, d_, m_, v_ in zip(SMALL, _unpack(sg, like), _unpack(sd, like), _unpack(sm, like), _unpack(sv, like)):
        res[n] = (g_, d_, m_, v_)
    for n in big_names:
        d_, m_, v_ = _adamw(a[n], grads[n], a["m_" + n], a["v_" + n], f"adamw_{n}")
        res[n] = (grads[n], d_, m_, v_)
        if n in TRANSPOSED:
            res[n] = tuple(jnp.swapaxes(t, 1, 2) for t in res[n])

    total = lax.psum(loss[0, 0], ("x", "y", "c"))
    return (total, gx, *[res[n][0] for n in WEIGHTS], *[res[n][1] for n in WEIGHTS],
            *[res[n][2] for n in WEIGHTS], *[res[n][3] for n in WEIGHTS])
```
